```python
import math
import jax, jax.numpy as jnp
from jax import lax
import numpy as np

D_MODEL = 1024
BATCH = 2
SEQ = 8192
DEPTH = 1
DEC_BATCH = 128
DEC_SEQ = 1
PAST_LEN = 8192
PAGE_SIZE = 128

A_HEADS = 4
A_DK = 128
A_DV = 128
A_WIDTH = A_HEADS * A_DV
A_QKV = 3 * A_WIDTH
CONV_W = 4
CHUNK = 64
B_HEADS = 8
B_KV_HEADS = 2
B_HD = 64
B_GROUP = B_HEADS // B_KV_HEADS
B_WIDTH = B_HEADS * B_HD
B_KV_WIDTH = B_KV_HEADS * B_HD
WINDOW = 128
BLOCK = 128
ROPE_THETA = 10000.0
MIX_WIDTH = A_WIDTH + B_WIDTH

DEEPNORM_ALPHA = (2 * DEPTH) ** 0.25
DEEPNORM_BETA = (8 * DEPTH) ** -0.25
LN_EPS = 1e-5
RMS_EPS = 1e-6
L2_EPS = 1e-6

OFF_A_QKV = 0
OFF_A_Z = OFF_A_QKV + A_QKV
OFF_A_BETA = OFF_A_Z + A_WIDTH
OFF_A_DECAY = OFF_A_BETA + A_HEADS
OFF_B_Q = OFF_A_DECAY + A_HEADS
OFF_B_K = OFF_B_Q + B_WIDTH
OFF_B_V = OFF_B_K + B_KV_WIDTH
OFF_B_Z = OFF_B_V + B_KV_WIDTH
PROJ_COLS = OFF_B_Z + B_WIDTH

kernel_name = "hymba_gdn_swa_sink_decode_step"


def layer_norm(x, g, b):
    xf = x.astype(jnp.float32)
    mu = jnp.mean(xf, -1, keepdims=True)
    var = jnp.mean(jnp.square(xf - mu), -1, keepdims=True)
    return ((xf - mu) * lax.rsqrt(var + LN_EPS) * g.astype(jnp.float32) + b.astype(jnp.float32)).astype(x.dtype)


def rms_norm(x, w):
    xf = x.astype(jnp.float32)
    return xf * lax.rsqrt(jnp.mean(xf * xf, -1, keepdims=True) + RMS_EPS) * w.astype(jnp.float32)


def l2_normalize(x):
    xf = x.astype(jnp.float32)
    return xf * lax.rsqrt(jnp.sum(xf * xf, -1, keepdims=True) + L2_EPS)


def rotary(x, pos):
    half = x.shape[-1] // 2
    inv = 1.0 / (ROPE_THETA ** (jnp.arange(half, dtype=jnp.float32) / half))
    ang = pos.astype(jnp.float32)[:, None] * inv[None, :]
    cos = jnp.cos(ang)[None, :, None, :]
    sin = jnp.sin(ang)[None, :, None, :]
    xf = x.astype(jnp.float32)
    x1, x2 = xf[..., :half], xf[..., half:]
    return jnp.concatenate([x1 * cos - x2 * sin, x2 * cos + x1 * sin], -1)


def causal_conv(buf, u, w):
    xp = jnp.concatenate([buf.astype(u.dtype), u], axis=1)
    t = u.shape[1]
    acc = xp[:, 0:t] * w[0]
    for j in range(1, CONV_W):
        acc = acc + xp[:, j:j + t] * w[j]
    return jax.nn.silu(acc), xp[:, -(CONV_W - 1):]


def gated_delta_chunked(q, k, v, g, beta, s0):
    bsz, t, h, dk = q.shape
    pad = (-t) % CHUNK
    if pad:
        pw4 = ((0, 0), (0, pad), (0, 0), (0, 0))
        pw3 = ((0, 0), (0, pad), (0, 0))
        q, k, v = jnp.pad(q, pw4), jnp.pad(k, pw4), jnp.pad(v, pw4)
        g, beta = jnp.pad(g, pw3), jnp.pad(beta, pw3)
    n = (t + pad) // CHUNK

    def split4(a):
        return a.reshape(bsz, n, CHUNK, h, a.shape[-1]).transpose(1, 0, 3, 2, 4)

    def split3(a):
        return a.reshape(bsz, n, CHUNK, h).transpose(1, 0, 3, 2)

    q, k, v = split4(q), split4(k), split4(v)
    g, beta = split3(g), split3(beta)
    G = jnp.cumsum(g, axis=-1)
    diff = G[..., :, None] - G[..., None, :]
    idx = jnp.arange(CHUNK)
    incl = idx[:, None] >= idx[None, :]
    strict = idx[:, None] > idx[None, :]
    dec_strict = jnp.exp(jnp.where(strict, diff, -jnp.inf))
    dec_incl = jnp.exp(jnp.where(incl, diff, -jnp.inf))
    eye = jnp.eye(CHUNK, dtype=q.dtype)
    lower = eye + beta[..., :, None] * jnp.einsum('nbhtd,nbhid->nbhti', k, k) * dec_strict
    rhs = jnp.concatenate([(beta * jnp.exp(G))[..., None] * k, beta[..., None] * v], -1)
    sol = lax.linalg.triangular_solve(lower, rhs, left_side=True, lower=True, unit_diagonal=True)
    w_mat, u_tilde = sol[..., :dk], sol[..., dk:]
    a_qk = jnp.einsum('nbhtd,nbhid->nbhti', q, k) * dec_incl
    q_dec = jnp.exp(G)[..., None] * q
    k_dec = jnp.exp(G[..., -1:] - G)[..., None] * k
    g_last = jnp.exp(G[..., -1])

    def step(s, xs):
        w_c, ut_c, aqk_c, qd_c, kd_c, gl_c = xs
        u = ut_c - jnp.einsum('bhcd,bhde->bhce', w_c, s)
        o = jnp.einsum('bhcd,bhde->bhce', qd_c, s) + jnp.einsum('bhti,bhie->bhte', aqk_c, u)
        s = gl_c[..., None, None] * s + jnp.einsum('bhcd,bhce->bhde', kd_c, u)
        return s, o

    s_fin, o = lax.scan(step, s0, (w_mat, u_tilde, a_qk, q_dec, k_dec, g_last))
    o = o.transpose(1, 0, 3, 2, 4).reshape(bsz, n * CHUNK, h, v.shape[-1])[:, :t]
    return o, s_fin


def gated_delta_recurrent(q, k, v, g, beta, s0):
    def step(s, xs):
        q_t, k_t, v_t, g_t, b_t = xs
        s = jnp.exp(g_t)[..., None, None] * s
        pred = jnp.einsum('bhd,bhde->bhe', k_t, s)
        s = s + jnp.einsum('bhd,bhe->bhde', k_t, b_t[..., None] * (v_t - pred))
        o = jnp.einsum('bhd,bhde->bhe', q_t, s)
        return s, o

    xs = (q.swapaxes(0, 1), k.swapaxes(0, 1), v.swapaxes(0, 1), g.swapaxes(0, 1), beta.swapaxes(0, 1))
    s_fin, o = lax.scan(step, s0, xs)
    return o.swapaxes(0, 1), s_fin


def sink_softmax(s, mask, sink):
    s = jnp.where(mask, s, -jnp.inf)
    m = jnp.maximum(jnp.max(s, -1, keepdims=True), sink)
    p = jnp.exp(s - m)
    return p / (jnp.sum(p, -1, keepdims=True) + jnp.exp(sink - m))


def swa_prompt(q, k, v, sinks):
    bsz, t = q.shape[:2]
    nb = t // BLOCK
    qb = q.astype(jnp.float32).reshape(bsz, nb, BLOCK, B_KV_HEADS, B_GROUP, B_HD)

    def band(a):
        ab = a.astype(jnp.float32).reshape(bsz, nb, BLOCK, B_KV_HEADS, B_HD)
        prev = jnp.concatenate([jnp.zeros_like(ab[:, :1]), ab[:, :-1]], axis=1)
        return jnp.concatenate([prev, ab], axis=2)

    kb, vb = band(k), band(v)
    s = jnp.einsum('bnqkgd,bnskd->bnkgqs', qb, kb)
    a = jnp.arange(BLOCK)[:, None]
    j = jnp.arange(2 * BLOCK)[None, :]
    rel = a + BLOCK - j
    band_mask = (rel >= 0) & (rel <= WINDOW)
    valid = band_mask[None] & ((jnp.arange(nb)[:, None, None] > 0) | (j[None] >= BLOCK))
    sink = sinks.astype(jnp.float32).reshape(B_KV_HEADS, B_GROUP)[:, :, None, None]
    p = sink_softmax(s, valid[None, :, None, None], sink)
    o = jnp.einsum('bnkgqs,bnskd->bnqkgd', p, vb)
    return o.reshape(bsz, t, B_WIDTH)


def swa_sample(q, k, v, win_k, win_v, sinks):
    bsz, t = q.shape[:2]
    kk = jnp.concatenate([win_k.astype(k.dtype), k], axis=1)
    vv = jnp.concatenate([win_v.astype(v.dtype), v], axis=1)
    qg = q.astype(jnp.float32).reshape(bsz, t, B_KV_HEADS, B_GROUP, B_HD)
    s = jnp.einsum('bqkgd,bskd->bkgqs', qg, kk.astype(jnp.float32))
    a = jnp.arange(t)[:, None]
    j = jnp.arange(WINDOW + t)[None, :]
    rel = WINDOW + a - j
    mask = (rel >= 0) & (rel <= WINDOW)
    sink = sinks.astype(jnp.float32).reshape(B_KV_HEADS, B_GROUP)[:, :, None, None]
    p = sink_softmax(s, mask, sink)
    o = jnp.einsum('bkgqs,bskd->bqkgd', p, vv.astype(jnp.float32))
    return o.reshape(bsz, t, B_WIDTH), kk[:, -WINDOW:], vv[:, -WINDOW:]


def hybrid_layer(x, c, pos, conv_buf, s0, win_k, win_v, is_prompt,
                 w_ada, b_ada, w_in, conv_w, a_log, dt_bias, norm_a, sinks, w_out, ln_g, ln_b):
    bsz, t, _ = x.shape
    mod = (c @ w_ada + b_ada)[:, None, :]
    shift, scale, gate = jnp.split(mod, 3, axis=-1)
    h = x * (1 + scale) + shift
    proj = h @ w_in

    qkv, new_conv = causal_conv(conv_buf, proj[..., OFF_A_QKV:OFF_A_Z], conv_w)
    qa, ka, va = jnp.split(qkv, 3, axis=-1)
    qa = l2_normalize(qa.reshape(bsz, t, A_HEADS, A_DK)) * (A_DK ** -0.5)
    ka = l2_normalize(ka.reshape(bsz, t, A_HEADS, A_DK))
    va = va.reshape(bsz, t, A_HEADS, A_DV).astype(jnp.float32)
    za = proj[..., OFF_A_Z:OFF_A_BETA]
    beta = jax.nn.sigmoid(proj[..., OFF_A_BETA:OFF_A_DECAY].astype(jnp.float32))
    g = -jnp.exp(a_log.astype(jnp.float32)) * jax.nn.softplus(
        proj[..., OFF_A_DECAY:OFF_B_Q].astype(jnp.float32) + dt_bias.astype(jnp.float32))
    s0f = s0.astype(jnp.float32)
    if is_prompt:
        oa, s_new = gated_delta_chunked(qa, ka, va, g, beta, s0f)
    else:
        oa, s_new = gated_delta_recurrent(qa, ka, va, g, beta, s0f)
    oa = rms_norm(oa, norm_a).astype(x.dtype).reshape(bsz, t, A_WIDTH) * jax.nn.silu(za)

    qb = (rotary(proj[..., OFF_B_Q:OFF_B_K].reshape(bsz, t, B_HEADS, B_HD), pos) * (B_HD ** -0.5)).astype(x.dtype)
    kb = rotary(proj[..., OFF_B_K:OFF_B_V].reshape(bsz, t, B_KV_HEADS, B_HD), pos).astype(x.dtype)
    vb = proj[..., OFF_B_V:OFF_B_Z].reshape(bsz, t, B_KV_HEADS, B_HD)
    zb = proj[..., OFF_B_Z:PROJ_COLS]
    if is_prompt:
        ob = swa_prompt(qb, kb, vb, sinks)
        new_k, new_v = kb[:, -WINDOW:], vb[:, -WINDOW:]
    else:
        ob, new_k, new_v = swa_sample(qb, kb, vb, win_k, win_v, sinks)
    ob = ob.astype(x.dtype) * jax.nn.silu(zb)

    mix = jnp.concatenate([oa, ob], axis=-1) @ w_out
    y = layer_norm(DEEPNORM_ALPHA * x + (1 + gate) * mix, ln_g, ln_b)
    return y, new_conv, s_new.astype(s0.dtype), new_k, new_v


def setup_inputs(seed: int = 0) -> dict:
    key = jax.random.key(seed)
    ks = jax.random.split(key, 20)
    f32 = jnp.float32

    def nrm(k, shape, s):
        return jax.random.normal(k, shape, f32) * s

    x_prompt = nrm(ks[0], (BATCH, SEQ, D_MODEL), 1.0)
    x_sample = nrm(ks[1], (DEC_BATCH, DEC_SEQ, D_MODEL), 1.0)
    state_conv = nrm(ks[2], (DEPTH, DEC_BATCH, CONV_W - 1, A_QKV), 1.0)
    state_delta = nrm(ks[3], (DEPTH, DEC_BATCH, A_HEADS, A_DK, A_DV), 0.3)
    cache_swa_k = nrm(ks[4], (DEPTH, DEC_BATCH, WINDOW, B_KV_HEADS, B_HD), 1.0)
    cache_swa_v = nrm(ks[5], (DEPTH, DEC_BATCH, WINDOW, B_KV_HEADS, B_HD), 1.0)
    c_prompt = nrm(ks[6], (BATCH, D_MODEL), 1.0)
    c_sample = nrm(ks[7], (DEC_BATCH, D_MODEL), 1.0)
    w_ada = nrm(ks[8], (DEPTH, D_MODEL, 3 * D_MODEL), 0.1 * D_MODEL ** -0.5)
    b_ada = nrm(ks[9], (DEPTH, 3 * D_MODEL), 0.01)
    col_scale = (jnp.ones((PROJ_COLS,), f32)
                 .at[OFF_A_QKV + 2 * A_WIDTH:OFF_A_Z].set(DEEPNORM_BETA)
                 .at[OFF_B_V:OFF_B_Z].set(DEEPNORM_BETA))
    w_in = nrm(ks[10], (DEPTH, D_MODEL, PROJ_COLS), D_MODEL ** -0.5) * col_scale
    conv_w = nrm(ks[11], (DEPTH, CONV_W, A_QKV), CONV_W ** -0.5)
    a_log = jnp.log(jax.random.uniform(ks[12], (DEPTH, A_HEADS), f32, 1.0, 16.0))
    dt = jnp.exp(jax.random.uniform(ks[13], (DEPTH, A_HEADS), f32, math.log(1e-3), math.log(1e-1)))
    dt_bias = dt + jnp.log(-jnp.expm1(-dt))
    norm_a = 1.0 + nrm(ks[14], (DEPTH, A_DV), 0.05)
    sinks = nrm(ks[15], (DEPTH, B_HEADS), 0.5)
    w_out = nrm(ks[16], (DEPTH, MIX_WIDTH, D_MODEL), MIX_WIDTH ** -0.5 * DEEPNORM_BETA)
    ln_g = 1.0 + nrm(ks[17], (DEPTH, D_MODEL), 0.05)
    ln_b = nrm(ks[18], (DEPTH, D_MODEL), 0.02)
    return {"x_prompt": x_prompt, "x_sample": x_sample,
            "state_conv": state_conv, "state_delta": state_delta,
            "cache_swa_k": cache_swa_k, "cache_swa_v": cache_swa_v,
            "c_prompt": c_prompt, "c_sample": c_sample,
            "w_ada": w_ada, "b_ada": b_ada, "w_in": w_in, "conv_w": conv_w,
            "a_log": a_log, "dt_bias": dt_bias, "norm_a": norm_a, "sinks": sinks,
            "w_out": w_out, "ln_g": ln_g, "ln_b": ln_b}


def reference(x_prompt, x_sample, state_conv, state_delta, cache_swa_k, cache_swa_v,
              c_prompt, c_sample, w_ada, b_ada, w_in, conv_w, a_log, dt_bias, norm_a,
              sinks, w_out, ln_g, ln_b):
    pos_p = jnp.arange(SEQ)
    pos_s = PAST_LEN + jnp.arange(DEC_SEQ)
    yp, ys = x_prompt, x_sample
    conv_p, delta_p, kp_l, vp_l = [], [], [], []
    conv_s, delta_s, ks_l, vs_l = [], [], [], []
    for l in range(DEPTH):
        lp = (w_ada[l], b_ada[l], w_in[l], conv_w[l], a_log[l], dt_bias[l], norm_a[l],
              sinks[l], w_out[l], ln_g[l], ln_b[l])
        zero_conv = jnp.zeros((BATCH, CONV_W - 1, A_QKV), x_prompt.dtype)
        zero_state = jnp.zeros((BATCH, A_HEADS, A_DK, A_DV), x_prompt.dtype)
        yp, cp, sp, kp, vp = hybrid_layer(yp, c_prompt, pos_p, zero_conv, zero_state, None, None, True, *lp)
        ys, cs, ss, kss, vss = hybrid_layer(ys, c_sample, pos_s, state_conv[l], state_delta[l],
                                            cache_swa_k[l], cache_swa_v[l], False, *lp)
        conv_p.append(cp); delta_p.append(sp); kp_l.append(kp); vp_l.append(vp)
        conv_s.append(cs); delta_s.append(ss); ks_l.append(kss); vs_l.append(vss)
    new_conv_p = jnp.stack(conv_p, 0)
    new_delta_p = jnp.stack(delta_p, 0)
    new_swa_k_p = jnp.stack(kp_l, 0)
    new_swa_v_p = jnp.stack(vp_l, 0)
    new_conv_s = jnp.stack(conv_s, 0)
    new_delta_s = jnp.stack(delta_s, 0)
    new_swa_k_s = jnp.stack(ks_l, 0)
    new_swa_v_s = jnp.stack(vs_l, 0)
    return (yp, ys, new_conv_p, new_delta_p, new_swa_k_p, new_swa_v_p,
            new_conv_s, new_delta_s, new_swa_k_s, new_swa_v_s)
```

```python
import functools

import jax
import jax.numpy as jnp
from jax import lax
from jax.experimental import pallas as pl
from jax.experimental.pallas import tpu as pltpu

F32 = jnp.float32
BF16 = jnp.bfloat16

D_MODEL = 1024
DEPTH = 1
PAST_LEN = 8192
A_HEADS = 4
A_DK = 128
A_DV = 128
A_WIDTH = A_HEADS * A_DV
A_QKV = 3 * A_WIDTH
CONV_W = 4
CHUNK = 64
B_HEADS = 8
B_KV_HEADS = 2
B_HD = 64
B_GROUP = B_HEADS // B_KV_HEADS
B_WIDTH = B_HEADS * B_HD
B_KV_WIDTH = B_KV_HEADS * B_HD
WINDOW = 128
ROPE_THETA = 10000.0
MIX_WIDTH = A_WIDTH + B_WIDTH
DEEPNORM_ALPHA = (2 * DEPTH) ** 0.25
LN_EPS = 1e-5
RMS_EPS = 1e-6
L2_EPS = 1e-6

OFF_A_Z = A_QKV
OFF_A_BETA = OFF_A_Z + A_WIDTH
OFF_A_DECAY = OFF_A_BETA + A_HEADS
OFF_B_Q = OFF_A_DECAY + A_HEADS
OFF_B_K = OFF_B_Q + B_WIDTH
OFF_B_V = OFF_B_K + B_KV_WIDTH
OFF_B_Z = OFF_B_V + B_KV_WIDTH
PROJ_COLS = OFF_B_Z + B_WIDTH

LANES = 128
C_QKV = 0
C_ZA = C_QKV + A_QKV
C_QB = C_ZA + A_WIDTH
C_KB = C_QB + B_WIDTH
C_VB = C_KB + B_KV_WIDTH
C_ZB = C_VB + B_KV_WIDTH
C_BD = C_ZB + B_WIDTH
W_COLS = C_BD + LANES

VMEM_LIMIT = 56 * 1024 * 1024

PROJ_TM = 256
DELTA_CT = 256
OUT_TM = 512
STEP_BT = 8


def _dot(a, b):
    return jnp.dot(a, b, preferred_element_type=F32)


def _dot_nt(a, b):
    return lax.dot_general(a, b, (((1,), (1,)), ((), ())), preferred_element_type=F32)


def _silu(x):
    return x * jax.nn.sigmoid(x)


def _softplus(x):
    return jnp.maximum(x, 0.0) + jnp.log1p(jnp.exp(-jnp.abs(x)))


def _lane(shape):
    return lax.broadcasted_iota(jnp.int32, shape, len(shape) - 1)


def _l2norm_heads(y, scale):
    outs = []
    for h in range(A_HEADS):
        xh = y[:, h * A_DK:(h + 1) * A_DK]
        ss = jnp.sum(xh * xh, axis=-1, keepdims=True)
        xn = xh * lax.rsqrt(ss + L2_EPS)
        outs.append(xn * scale if scale != 1.0 else xn)
    return jnp.concatenate(outs, axis=-1)


def _rotary_group(xg, cos, sin_signed):
    lane = _lane(xg.shape)
    swapped = jnp.where((lane % B_HD) < (B_HD // 2),
                        pltpu.roll(xg, LANES - B_HD // 2, axis=1),
                        pltpu.roll(xg, B_HD // 2, axis=1))
    return xg * cos + swapped * sin_signed


def _gate_lanes(bd, alog_row, dt_row):
    lane = _lane(bd.shape)
    g = -jnp.exp(alog_row) * _softplus(bd + dt_row)
    return jnp.where(lane < A_HEADS, jax.nn.sigmoid(bd), g)


def _layer_norm(r, g, b):
    mu = jnp.mean(r, axis=-1, keepdims=True)
    d = r - mu
    var = jnp.mean(d * d, axis=-1, keepdims=True)
    return d * lax.rsqrt(var + LN_EPS) * g + b


def _ada_kernel(c_ref, w_ref, b_ref, o_ref):
    o_ref[...] = _dot(c_ref[...].astype(BF16), w_ref[...].astype(BF16)) + b_ref[...]


def _ada(c_all, w_ada, b_ada):
    rows = c_all.shape[0]
    tn = 768
    return pl.pallas_call(
        _ada_kernel,
        grid=(3 * D_MODEL // tn,),
        in_specs=[pl.BlockSpec((rows, D_MODEL), lambda j: (0, 0)),
                  pl.BlockSpec((D_MODEL, tn), lambda j: (0, j)),
                  pl.BlockSpec((1, tn), lambda j: (0, j))],
        out_specs=pl.BlockSpec((rows, tn), lambda j: (0, j)),
        out_shape=jax.ShapeDtypeStruct((rows, 3 * D_MODEL), F32),
        compiler_params=pltpu.CompilerParams(dimension_semantics=("arbitrary",),
                                             vmem_limit_bytes=VMEM_LIMIT),
        name="ada",
    )(c_all, w_ada, b_ada)


def _proj_kernel(x_ref, mod_ref, w_ref, cw_ref, alog_ref, dt_ref, cos_ref, sin_ref,
                 q_ref, k_ref, v_ref, za_ref, gb_ref, qb_ref, kb_ref, vb_ref, zb_ref, cst_ref,
                 cbuf):
    tm = x_ref.shape[1]
    t = pl.program_id(1)

    @pl.when(t == 0)
    def _():
        cbuf[0:8, :] = jnp.zeros((8, A_QKV), F32)

    shift = mod_ref[0, :, 0:D_MODEL]
    scale = mod_ref[0, :, D_MODEL:2 * D_MODEL]
    h = (x_ref[0] * (1.0 + scale) + shift).astype(BF16)

    for gi, o_ref in enumerate((q_ref, k_ref, v_ref)):
        c0 = gi * A_WIDTH
        u = _dot(h, w_ref[:, c0:c0 + A_WIDTH])
        cbuf[8:8 + tm, c0:c0 + A_WIDTH] = u
        acc = cbuf[5:5 + tm, c0:c0 + A_WIDTH] * cw_ref[0:1, c0:c0 + A_WIDTH]
        acc = acc + cbuf[6:6 + tm, c0:c0 + A_WIDTH] * cw_ref[1:2, c0:c0 + A_WIDTH]
        acc = acc + cbuf[7:7 + tm, c0:c0 + A_WIDTH] * cw_ref[2:3, c0:c0 + A_WIDTH]
        acc = acc + u * cw_ref[3:4, c0:c0 + A_WIDTH]
        y = _silu(acc)
        if gi == 0:
            y = _l2norm_heads(y, A_DK ** -0.5)
        elif gi == 1:
            y = _l2norm_heads(y, 1.0)
        o_ref[0] = y
    tail = cbuf[tm + 5:tm + 8, :]
    cst_ref[0] = tail
    cbuf[5:8, :] = tail

    za_ref[0] = _dot(h, w_ref[:, C_ZA:C_ZA + A_WIDTH])
    gb_ref[0] = _gate_lanes(_dot(h, w_ref[:, C_BD:C_BD + LANES]), alog_ref[...], dt_ref[...])

    cos = cos_ref[...]
    sin = sin_ref[...]
    uq = _dot(h, w_ref[:, C_QB:C_QB + B_WIDTH])
    for g in range(B_WIDTH // LANES):
        qb_ref[0, :, g * LANES:(g + 1) * LANES] = (
            _rotary_group(uq[:, g * LANES:(g + 1) * LANES], cos, sin) * (B_HD ** -0.5))
    kb_ref[0] = _rotary_group(_dot(h, w_ref[:, C_KB:C_KB + LANES]), cos, sin)
    vb_ref[0] = _dot(h, w_ref[:, C_VB:C_VB + LANES])
    zb_ref[0] = _dot(h, w_ref[:, C_ZB:C_ZB + B_WIDTH])


def _proj(x, mod_p, w_r, conv_w, alog_row, dt_row, cos_t, sin_t):
    bsz, t, _ = x.shape
    tm = PROJ_TM
    row = lambda w: pl.BlockSpec((1, tm, w), lambda b, i: (b, i, 0))
    const2 = lambda s: pl.BlockSpec(s, lambda b, i: (0, 0))
    wide = lambda w: jax.ShapeDtypeStruct((bsz, t, w), F32)
    return pl.pallas_call(
        _proj_kernel,
        grid=(bsz, t // tm),
        in_specs=[row(D_MODEL),
                  pl.BlockSpec((1, 1, 3 * D_MODEL), lambda b, i: (b, 0, 0)),
                  const2((D_MODEL, W_COLS)),
                  const2((CONV_W, A_QKV)),
                  const2((1, LANES)), const2((1, LANES)),
                  pl.BlockSpec((tm, LANES), lambda b, i: (i, 0)),
                  pl.BlockSpec((tm, LANES), lambda b, i: (i, 0))],
        out_specs=[row(A_WIDTH), row(A_WIDTH), row(A_WIDTH), row(A_WIDTH), row(LANES),
                   row(B_WIDTH), row(LANES), row(LANES), row(B_WIDTH),
                   pl.BlockSpec((1, CONV_W - 1, A_QKV), lambda b, i: (b, 0, 0))],
        out_shape=[wide(A_WIDTH), wide(A_WIDTH), wide(A_WIDTH), wide(A_WIDTH), wide(LANES),
                   wide(B_WIDTH), wide(LANES), wide(LANES), wide(B_WIDTH),
                   jax.ShapeDtypeStruct((bsz, CONV_W - 1, A_QKV), F32)],
        scratch_shapes=[pltpu.VMEM((tm + 8, A_QKV), F32)],
        compiler_params=pltpu.CompilerParams(dimension_semantics=("arbitrary", "arbitrary"),
                                             vmem_limit_bytes=VMEM_LIMIT),
        name="proj",
    )(x, mod_p, w_r, conv_w, alog_row, dt_row, cos_t, sin_t)


def _delta_kernel(q_ref, k_ref, v_ref, gb_ref, za_ref, na_ref, oa_ref, st_ref, s_scr):
    ct = q_ref.shape[1]
    t = pl.program_id(1)

    @pl.when(t == 0)
    def _():
        s_scr[...] = jnp.zeros(s_scr.shape, F32)

    gbv = gb_ref[0]
    rin = lax.broadcasted_iota(jnp.int32, gbv.shape, 0) % CHUNK
    gcs = gbv
    s = 1
    while s < CHUNK:
        gcs = gcs + jnp.where(rin >= s, pltpu.roll(gcs, s, axis=0), 0.0)
        s *= 2
    gcs_t = gcs.T

    ti = lax.broadcasted_iota(jnp.int32, (CHUNK, CHUNK), 0)
    ii = lax.broadcasted_iota(jnp.int32, (CHUNK, CHUNK), 1)
    na = na_ref[...]

    for c in range(ct // CHUNK):
        r0 = c * CHUNK
        for h in range(A_HEADS):
            hs = slice(h * A_DK, (h + 1) * A_DK)
            qc = q_ref[0, r0:r0 + CHUNK, hs]
            kc = k_ref[0, r0:r0 + CHUNK, hs]
            vc = v_ref[0, r0:r0 + CHUNK, hs]
            beta = gbv[r0:r0 + CHUNK, h:h + 1]
            g_col = gcs[r0:r0 + CHUNK, A_HEADS + h:A_HEADS + h + 1]
            g_row = gcs_t[A_HEADS + h:A_HEADS + h + 1, r0:r0 + CHUNK]
            g_last = gcs[r0 + CHUNK - 1:r0 + CHUNK, A_HEADS + h:A_HEADS + h + 1]
            diff = g_col - g_row
            dec_incl = jnp.exp(jnp.where(ti >= ii, diff, -jnp.inf))
            dec_strict = jnp.where(ti > ii, dec_incl, 0.0)
            eg = jnp.exp(g_col)

            n = -(beta * _dot_nt(kc, kc) * dec_strict)
            r = n
            p = n
            for step in range(6):
                if step == 0:
                    p = _dot(p, p)
                    continue
                rp = _dot(jnp.concatenate([r, p], axis=0), p) if step < 5 else _dot(r, p)
                r = r + p + rp[:CHUNK]
                if step < 5:
                    p = rp[CHUNK:]

            rhs = jnp.concatenate([(beta * eg) * kc, beta * vc], axis=-1)
            sol = rhs + _dot(r, rhs)
            w_c = sol[:, :A_DK]
            ut = sol[:, A_DK:]
            aqk = _dot_nt(qc, kc) * dec_incl
            qd = eg * qc
            kd = jnp.exp(g_last - g_col) * kc

            s_h = s_scr[h]
            ws = _dot(jnp.concatenate([w_c, qd], axis=0), s_h)
            u = ut - ws[:CHUNK]
            o = ws[CHUNK:] + _dot(aqk, u)
            s_scr[h] = jnp.exp(g_last) * s_h + _dot(kd.T, u)

            on = o * lax.rsqrt(jnp.mean(o * o, axis=-1, keepdims=True) + RMS_EPS) * na
            oa_ref[0, r0:r0 + CHUNK, hs] = on * _silu(za_ref[0, r0:r0 + CHUNK, hs])

    st_ref[0] = s_scr[...]


def _delta(q, k, v, gb, za, na_row):
    bsz, t, _ = q.shape
    ct = DELTA_CT
    row = lambda w: pl.BlockSpec((1, ct, w), lambda b, i: (b, i, 0))
    return pl.pallas_call(
        _delta_kernel,
        grid=(bsz, t // ct),
        in_specs=[row(A_WIDTH), row(A_WIDTH), row(A_WIDTH), row(LANES), row(A_WIDTH),
                  pl.BlockSpec((1, A_DV), lambda b, i: (0, 0))],
        out_specs=[row(A_WIDTH),
                   pl.BlockSpec((1, A_HEADS, A_DK, A_DV), lambda b, i: (b, 0, 0, 0))],
        out_shape=[jax.ShapeDtypeStruct((bsz, t, A_WIDTH), F32),
                   jax.ShapeDtypeStruct((bsz, A_HEADS, A_DK, A_DV), F32)],
        scratch_shapes=[pltpu.VMEM((A_HEADS, A_DK, A_DV), F32)],
        compiler_params=pltpu.CompilerParams(dimension_semantics=("arbitrary", "arbitrary"),
                                             vmem_limit_bytes=VMEM_LIMIT),
        name="delta",
    )(q, k, v, gb, za, na_row)


def _swa_kernel(sink_ref, qb_ref, kc_ref, kp_ref, vc_ref, vp_ref, zb_ref, ob_ref):
    n = pl.program_id(1)
    blk = qb_ref.shape[1]
    k2 = jnp.concatenate([kp_ref[0], kc_ref[0]], axis=0)
    v2 = jnp.concatenate([vp_ref[0], vc_ref[0]], axis=0)
    k2r = pltpu.roll(k2, B_HD, axis=1)
    v2r = pltpu.roll(v2, B_HD, axis=1)
    low2 = _lane(k2.shape) < B_HD
    vdup = (jnp.where(low2, v2, v2r), jnp.where(low2, v2r, v2))

    a = lax.broadcasted_iota(jnp.int32, (2 * blk, 2 * blk), 0) % blk
    j = lax.broadcasted_iota(jnp.int32, (2 * blk, 2 * blk), 1)
    rel = a + blk - j
    valid = (rel >= 0) & (rel <= WINDOW) & ((n > 0) | (j >= blk))
    top = lax.broadcasted_iota(jnp.int32, (2 * blk, 1), 0) < blk
    low = _lane((blk, LANES)) < B_HD

    outs = {}
    for kh in range(B_KV_HEADS):
        for half in range(2):
            qs = []
            for i in range(2):
                grp = kh * 2 + i
                xg = qb_ref[0, :, grp * LANES:(grp + 1) * LANES]
                qs.append(jnp.where(low if half == 0 else jnp.logical_not(low), xg, 0.0))
            qz = jnp.concatenate(qs, axis=0)
            kx = k2 if kh == half else k2r
            sc = jnp.where(valid, _dot_nt(qz, kx), -jnp.inf)
            sink = jnp.where(top, sink_ref[kh * B_GROUP + half], sink_ref[kh * B_GROUP + half + 2])
            m = jnp.maximum(jnp.max(sc, axis=-1, keepdims=True), sink)
            p = jnp.exp(sc - m)
            den = jnp.sum(p, axis=-1, keepdims=True) + jnp.exp(sink - m)
            outs[(kh, half)] = _dot(p, vdup[kh]) / den

    for grp in range(B_WIDTH // LANES):
        kh, i = grp // 2, grp % 2
        og = jnp.where(low, outs[(kh, 0)][i * blk:(i + 1) * blk], outs[(kh, 1)][i * blk:(i + 1) * blk])
        ob_ref[0, :, grp * LANES:(grp + 1) * LANES] = og * _silu(zb_ref[0, :, grp * LANES:(grp + 1) * LANES])


def _swa(sinks, qb, kb, vb, zb):
    bsz, t, _ = qb.shape
    blk = WINDOW
    cur = lambda w: pl.BlockSpec((1, blk, w), lambda b, i: (b, i, 0))
    prev = lambda w: pl.BlockSpec((1, blk, w), lambda b, i: (b, jnp.maximum(i - 1, 0), 0))
    return pl.pallas_call(
        _swa_kernel,
        grid=(bsz, t // blk),
        in_specs=[pl.BlockSpec(memory_space=pltpu.SMEM),
                  cur(B_WIDTH), cur(LANES), prev(LANES), cur(LANES), prev(LANES), cur(B_WIDTH)],
        out_specs=cur(B_WIDTH),
        out_shape=jax.ShapeDtypeStruct((bsz, t, B_WIDTH), F32),
        compiler_params=pltpu.CompilerParams(dimension_semantics=("arbitrary", "arbitrary"),
                                             vmem_limit_bytes=VMEM_LIMIT),
        name="swa",
    )(sinks, qb, kb, kb, vb, vb, zb)


def _out_kernel(oa_ref, ob_ref, x_ref, gate_ref, w_ref, g_ref, b_ref, y_ref):
    mix = (_dot(oa_ref[0].astype(BF16), w_ref[0:A_WIDTH, :])
           + _dot(ob_ref[0].astype(BF16), w_ref[A_WIDTH:MIX_WIDTH, :]))
    r = DEEPNORM_ALPHA * x_ref[0] + (1.0 + gate_ref[0]) * mix
    y_ref[0] = _layer_norm(r, g_ref[...], b_ref[...])


def _out(oa, ob, x, gate, w_out, ln_g, ln_b, tm):
    bsz, t, _ = x.shape
    gr = gate.shape[1]
    row = lambda w: pl.BlockSpec((1, tm, w), lambda b, i: (b, i, 0))
    gate_spec = (pl.BlockSpec((1, 1, D_MODEL), lambda b, i: (b, 0, 0)) if gr == 1
                 else row(D_MODEL))
    const2 = lambda s: pl.BlockSpec(s, lambda b, i: (0, 0))
    return pl.pallas_call(
        _out_kernel,
        grid=(bsz, t // tm),
        in_specs=[row(A_WIDTH), row(B_WIDTH), row(D_MODEL), gate_spec,
                  const2((MIX_WIDTH, D_MODEL)), const2((1, D_MODEL)), const2((1, D_MODEL))],
        out_specs=row(D_MODEL),
        out_shape=jax.ShapeDtypeStruct((bsz, t, D_MODEL), F32),
        compiler_params=pltpu.CompilerParams(dimension_semantics=("arbitrary", "arbitrary"),
                                             vmem_limit_bytes=VMEM_LIMIT),
        name="out",
    )(oa, ob, x, gate, w_out, ln_g, ln_b)


def _sproj_kernel(x_ref, mod_ref, w_ref, cw_ref, cst_ref, alog_ref, dt_ref, cos_ref, sin_ref,
                  q_ref, k_ref, v_ref, za_ref, gb_ref, qb_ref, kb_ref, vb_ref, zb_ref, ncs_ref):
    shift = mod_ref[:, 0:D_MODEL]
    scale = mod_ref[:, D_MODEL:2 * D_MODEL]
    h = (x_ref[...] * (1.0 + scale) + shift).astype(BF16)

    for gi, o_ref in enumerate((q_ref, k_ref, v_ref)):
        c0 = gi * A_WIDTH
        cs = slice(c0, c0 + A_WIDTH)
        u = _dot(h, w_ref[:, cs])
        acc = cst_ref[0, :, cs] * cw_ref[0:1, cs]
        acc = acc + cst_ref[1, :, cs] * cw_ref[1:2, cs]
        acc = acc + cst_ref[2, :, cs] * cw_ref[2:3, cs]
        acc = acc + u * cw_ref[3:4, cs]
        y = _silu(acc)
        if gi == 0:
            y = _l2norm_heads(y, A_DK ** -0.5)
        elif gi == 1:
            y = _l2norm_heads(y, 1.0)
        o_ref[...] = y
        ncs_ref[0, :, cs] = cst_ref[1, :, cs]
        ncs_ref[1, :, cs] = cst_ref[2, :, cs]
        ncs_ref[2, :, cs] = u

    za_ref[...] = _dot(h, w_ref[:, C_ZA:C_ZA + A_WIDTH])
    gb_ref[...] = _gate_lanes(_dot(h, w_ref[:, C_BD:C_BD + LANES]), alog_ref[...], dt_ref[...])

    cos = cos_ref[...]
    sin = sin_ref[...]
    uq = _dot(h, w_ref[:, C_QB:C_QB + B_WIDTH])
    for g in range(B_WIDTH // LANES):
        qb_ref[:, g * LANES:(g + 1) * LANES] = (
            _rotary_group(uq[:, g * LANES:(g + 1) * LANES], cos, sin) * (B_HD ** -0.5))
    kb_ref[...] = _rotary_group(_dot(h, w_ref[:, C_KB:C_KB + LANES]), cos, sin)
    vb_ref[...] = _dot(h, w_ref[:, C_VB:C_VB + LANES])
    zb_ref[...] = _dot(h, w_ref[:, C_ZB:C_ZB + B_WIDTH])


def _sproj(x, mod_s, w_r, conv_w, cst, alog_row, dt_row, cos_row, sin_row):
    n = x.shape[0]
    full = lambda s: pl.BlockSpec(s, lambda i: (0,) * len(s))
    wide = lambda w: jax.ShapeDtypeStruct((n, w), F32)
    return pl.pallas_call(
        _sproj_kernel,
        grid=(1,),
        in_specs=[full((n, D_MODEL)), full((n, 3 * D_MODEL)), full((D_MODEL, W_COLS)),
                  full((CONV_W, A_QKV)), full((CONV_W - 1, n, A_QKV)),
                  full((1, LANES)), full((1, LANES)), full((1, LANES)), full((1, LANES))],
        out_specs=[full((n, A_WIDTH)), full((n, A_WIDTH)), full((n, A_WIDTH)), full((n, A_WIDTH)),
                   full((n, LANES)), full((n, B_WIDTH)), full((n, LANES)), full((n, LANES)),
                   full((n, B_WIDTH)), full((CONV_W - 1, n, A_QKV))],
        out_shape=[wide(A_WIDTH), wide(A_WIDTH), wide(A_WIDTH), wide(A_WIDTH), wide(LANES),
                   wide(B_WIDTH), wide(LANES), wide(LANES), wide(B_WIDTH),
                   jax.ShapeDtypeStruct((CONV_W - 1, n, A_QKV), F32)],
        compiler_params=pltpu.CompilerParams(dimension_semantics=("arbitrary",),
                                             vmem_limit_bytes=VMEM_LIMIT),
        name="sproj",
    )(x, mod_s, w_r, conv_w, cst, alog_row, dt_row, cos_row, sin_row)


def _sstep_kernel(sink_ref, q_ref, k_ref, v_ref, gb_ref, za_ref, na_ref, st_ref,
                  qb_ref, kn_ref, vn_ref, zb_ref, ck_ref, cv_ref,
                  oa_ref, ob_ref, nst_ref, nck_ref, ncv_ref,
                  o_scr, ob_scr):
    bt = q_ref.shape[0]
    gbv = gb_ref[...]

    for h in range(A_HEADS):
        hs = slice(h * A_DK, (h + 1) * A_DK)
        q_t = q_ref[:, hs].T
        k_t = k_ref[:, hs].T
        for bb in range(bt):
            eg = jnp.exp(gbv[bb:bb + 1, A_HEADS + h:A_HEADS + h + 1])
            beta = gbv[bb:bb + 1, h:h + 1]
            kcol = k_t[:, bb:bb + 1]
            qcol = q_t[:, bb:bb + 1]
            s1 = eg * st_ref[bb, h]
            pred = jnp.sum(kcol * s1, axis=0, keepdims=True)
            upd = beta * (v_ref[bb:bb + 1, hs] - pred)
            s2 = s1 + kcol * upd
            nst_ref[bb, h] = s2
            o_scr[bb:bb + 1, hs] = jnp.sum(qcol * s2, axis=0, keepdims=True)
    na = na_ref[...]
    for h in range(A_HEADS):
        hs = slice(h * A_DK, (h + 1) * A_DK)
        o = o_scr[:, hs]
        on = o * lax.rsqrt(jnp.mean(o * o, axis=-1, keepdims=True) + RMS_EPS) * na
        oa_ref[:, hs] = on * _silu(za_ref[:, hs])

    row8 = lax.broadcasted_iota(jnp.int32, (B_HEADS, LANES), 0)
    lane8 = _lane((B_HEADS, LANES))
    own_half = (lane8 >= B_HD) == (row8 >= B_GROUP)
    rcol = lax.broadcasted_iota(jnp.int32, (B_HEADS, 1), 0)
    sink = jnp.zeros((B_HEADS, 1), F32)
    for r in range(B_HEADS):
        sink = jnp.where(rcol == r, sink_ref[r], sink)
    qv = qb_ref[...]
    qv_r = jnp.concatenate([pltpu.roll(qv[:, g * LANES:(g + 1) * LANES], B_HD, axis=1)
                            for g in range(B_WIDTH // LANES)], axis=-1)
    for bb in range(bt):
        qz = jnp.zeros((B_HEADS, LANES), F32)
        for r in range(B_HEADS):
            grp, half, kh = r // 2, r % 2, r // B_GROUP
            src = qv if half == kh else qv_r
            qz = jnp.where(row8 == r, src[bb:bb + 1, grp * LANES:(grp + 1) * LANES], qz)
        qz = jnp.where(own_half, qz, 0.0)
        kc = ck_ref[bb]
        vc = cv_ref[bb]
        knew = kn_ref[bb:bb + 1, :]
        vnew = vn_ref[bb:bb + 1, :]
        sc = _dot_nt(qz, kc)
        sc_new = jnp.sum(qz * knew, axis=-1, keepdims=True)
        m = jnp.maximum(jnp.maximum(jnp.max(sc, axis=-1, keepdims=True), sc_new), sink)
        p = jnp.exp(sc - m)
        p_new = jnp.exp(sc_new - m)
        den = jnp.sum(p, axis=-1, keepdims=True) + p_new + jnp.exp(sink - m)
        o = (_dot(p, vc) + p_new * vnew) / den
        o = jnp.where(own_half, o, 0.0)
        ob_scr[bb * B_HEADS:(bb + 1) * B_HEADS, :] = o + pltpu.roll(o, B_HD, axis=1)
        nck_ref[bb, 0:WINDOW - 1, :] = ck_ref[bb, 1:WINDOW, :]
        nck_ref[bb, WINDOW - 1:WINDOW, :] = knew
        ncv_ref[bb, 0:WINDOW - 1, :] = cv_ref[bb, 1:WINDOW, :]
        ncv_ref[bb, WINDOW - 1:WINDOW, :] = vnew
    low = _lane((bt, LANES)) < B_HD
    for grp in range(B_WIDTH // LANES):
        even = ob_scr[pl.ds(2 * grp, bt, stride=B_HEADS), :]
        odd = ob_scr[pl.ds(2 * grp + 1, bt, stride=B_HEADS), :]
        gs = slice(grp * LANES, (grp + 1) * LANES)
        ob_ref[:, gs] = jnp.where(low, even, odd) * _silu(zb_ref[:, gs])


def _sstep(sinks, q, k, v, gb, za, na_row, state, qb, kn, vn, zb, ck, cv):
    n = q.shape[0]
    bt = STEP_BT
    row = lambda w: pl.BlockSpec((bt, w), lambda i: (i, 0))
    st_spec = pl.BlockSpec((bt, A_HEADS, A_DK, A_DV), lambda i: (i, 0, 0, 0))
    c_spec = pl.BlockSpec((bt, WINDOW, LANES), lambda i: (i, 0, 0))
    return pl.pallas_call(
        _sstep_kernel,
        grid=(n // bt,),
        in_specs=[pl.BlockSpec(memory_space=pltpu.SMEM),
                  row(A_WIDTH), row(A_WIDTH), row(A_WIDTH), row(LANES), row(A_WIDTH),
                  pl.BlockSpec((1, A_DV), lambda i: (0, 0)), st_spec,
                  row(B_WIDTH), row(LANES), row(LANES), row(B_WIDTH), c_spec, c_spec],
        out_specs=[row(A_WIDTH), row(B_WIDTH), st_spec, c_spec, c_spec],
        out_shape=[jax.ShapeDtypeStruct((n, A_WIDTH), F32),
                   jax.ShapeDtypeStruct((n, B_WIDTH), F32),
                   jax.ShapeDtypeStruct((n, A_HEADS, A_DK, A_DV), F32),
                   jax.ShapeDtypeStruct((n, WINDOW, LANES), F32),
                   jax.ShapeDtypeStruct((n, WINDOW, LANES), F32)],
        scratch_shapes=[pltpu.VMEM((bt, A_WIDTH), F32), pltpu.VMEM((bt * B_HEADS, LANES), F32)],
        compiler_params=pltpu.CompilerParams(dimension_semantics=("arbitrary",),
                                             vmem_limit_bytes=VMEM_LIMIT),
        name="sstep",
    )(sinks, q, k, v, gb, za, na_row, state, qb, kn, vn, zb, ck, cv)


def _rope_tables(pos):
    half = B_HD // 2
    inv = 1.0 / (ROPE_THETA ** (jnp.arange(half, dtype=F32) / half))
    ang = pos.astype(F32)[:, None] * inv[None, :]
    cos = jnp.cos(ang)
    sin = jnp.sin(ang)
    reps = LANES // B_HD
    return (jnp.tile(jnp.concatenate([cos, cos], -1), (1, reps)),
            jnp.tile(jnp.concatenate([-sin, sin], -1), (1, reps)))


def _pad_row(vec, offset):
    return jnp.zeros((1, LANES), F32).at[0, offset:offset + vec.shape[0]].set(vec.astype(F32))


def _layer(x_prompt, x_sample, state_conv, state_delta, cache_k, cache_v, c_prompt, c_sample,
           w_ada, b_ada, w_in, conv_w, a_log, dt_bias, norm_a, sinks, w_out, ln_g, ln_b):
    bsz, seq, _ = x_prompt.shape
    n_s = x_sample.shape[0]

    w_r = jnp.concatenate([
        w_in[:, 0:OFF_A_BETA], w_in[:, OFF_B_Q:PROJ_COLS], w_in[:, OFF_A_BETA:OFF_B_Q],
        jnp.zeros((D_MODEL, LANES - 2 * A_HEADS), w_in.dtype)], axis=1).astype(BF16)
    w_o = w_out.astype(BF16)
    alog_row = _pad_row(a_log, A_HEADS)
    dt_row = _pad_row(dt_bias, A_HEADS)
    na_row = norm_a.reshape(1, A_DV)
    g_row = ln_g.reshape(1, D_MODEL)
    b_row = ln_b.reshape(1, D_MODEL)

    c_all = jnp.concatenate([c_sample, c_prompt, jnp.zeros((8 - bsz, D_MODEL), F32)], axis=0)
    mod = _ada(c_all, w_ada, b_ada.reshape(1, 3 * D_MODEL))
    mod_s = mod[0:n_s]
    mod_p = mod[n_s:n_s + bsz].reshape(bsz, 1, 3 * D_MODEL)

    cos_p, sin_p = _rope_tables(jnp.arange(seq))
    q, k, v, za, gb, qb, kb, vb, zb, conv_p = _proj(x_prompt, mod_p, w_r, conv_w, alog_row, dt_row,
                                                    cos_p, sin_p)
    oa, delta_p = _delta(q, k, v, gb, za, na_row)
    ob = _swa(sinks, qb, kb, vb, zb)
    y_p = _out(oa, ob, x_prompt, mod_p[:, :, 2 * D_MODEL:], w_o, g_row, b_row, OUT_TM)
    swa_k_p = kb[:, seq - WINDOW:, :].reshape(bsz, WINDOW, B_KV_HEADS, B_HD)
    swa_v_p = vb[:, seq - WINDOW:, :].reshape(bsz, WINDOW, B_KV_HEADS, B_HD)

    cos_s, sin_s = _rope_tables(PAST_LEN + jnp.arange(1))
    xs = x_sample.reshape(n_s, D_MODEL)
    cst = jnp.transpose(state_conv, (1, 0, 2))
    sq, sk, sv, sza, sgb, sqb, skn, svn, szb, ncs = _sproj(xs, mod_s, w_r, conv_w, cst, alog_row, dt_row,
                                                           cos_s, sin_s)
    soa, sob, delta_s, nck, ncv = _sstep(sinks, sq, sk, sv, sgb, sza, na_row, state_delta,
                                         sqb, skn, svn, szb,
                                         cache_k.reshape(n_s, WINDOW, LANES),
                                         cache_v.reshape(n_s, WINDOW, LANES))
    y_s = _out(soa[None], sob[None], xs[None], mod_s[None, :, 2 * D_MODEL:], w_o, g_row, b_row, n_s)
    conv_s = jnp.transpose(ncs, (1, 0, 2))
    return (y_p, y_s.reshape(n_s, 1, D_MODEL), conv_p, delta_p, swa_k_p, swa_v_p,
            conv_s, delta_s,
            nck.reshape(n_s, WINDOW, B_KV_HEADS, B_HD), ncv.reshape(n_s, WINDOW, B_KV_HEADS, B_HD))


def kernel(x_prompt, x_sample, state_conv, state_delta, cache_swa_k, cache_swa_v, c_prompt, c_sample,
           w_ada, b_ada, w_in, conv_w, a_log, dt_bias, norm_a, sinks, w_out, ln_g, ln_b):
    assert w_ada.shape[0] == DEPTH == 1
    outs = _layer(x_prompt, x_sample, state_conv[0], state_delta[0], cache_swa_k[0], cache_swa_v[0],
                  c_prompt, c_sample, w_ada[0], b_ada[0], w_in[0], conv_w[0], a_log[0], dt_bias[0],
                  norm_a[0], sinks[0], w_out[0], ln_g[0], ln_b[0])
    y_p, y_s = outs[0], outs[1]
    return (y_p, y_s) + tuple(o[None] for o in outs[2:])
```

```python
import functools

import jax
import jax.numpy as jnp
from jax import lax
from jax.experimental import pallas as pl
from jax.experimental.pallas import tpu as pltpu

F32 = jnp.float32
BF16 = jnp.bfloat16

D_MODEL = 1024
DEPTH = 1
PAST_LEN = 8192
A_HEADS = 4
A_DK = 128
A_DV = 128
A_WIDTH = A_HEADS * A_DV
A_QKV = 3 * A_WIDTH
CONV_W = 4
CHUNK = 64
B_HEADS = 8
B_KV_HEADS = 2
B_HD = 64
B_GROUP = B_HEADS // B_KV_HEADS
B_WIDTH = B_HEADS * B_HD
B_KV_WIDTH = B_KV_HEADS * B_HD
WINDOW = 128
ROPE_THETA = 10000.0
MIX_WIDTH = A_WIDTH + B_WIDTH
DEEPNORM_ALPHA = (2 * DEPTH) ** 0.25
LN_EPS = 1e-5
RMS_EPS = 1e-6
L2_EPS = 1e-6

OFF_A_Z = A_QKV
OFF_A_BETA = OFF_A_Z + A_WIDTH
OFF_A_DECAY = OFF_A_BETA + A_HEADS
OFF_B_Q = OFF_A_DECAY + A_HEADS
OFF_B_K = OFF_B_Q + B_WIDTH
OFF_B_V = OFF_B_K + B_KV_WIDTH
OFF_B_Z = OFF_B_V + B_KV_WIDTH
PROJ_COLS = OFF_B_Z + B_WIDTH

LANES = 128
C_QKV = 0
C_ZA = C_QKV + A_QKV
C_QB = C_ZA + A_WIDTH
C_KB = C_QB + B_WIDTH
C_VB = C_KB + B_KV_WIDTH
C_ZB = C_VB + B_KV_WIDTH
C_BD = C_ZB + B_WIDTH
W_COLS = C_BD + LANES

VMEM_LIMIT = 56 * 1024 * 1024

PROJ_TM = 256
DELTA_CT = 256
OUT_TM = 512
STEP_BT = 8


def _dot(a, b):
    return jnp.dot(a, b, preferred_element_type=F32)


def _dot_nt(a, b):
    return lax.dot_general(a, b, (((1,), (1,)), ((), ())), preferred_element_type=F32)


def _silu(x):
    return x * jax.nn.sigmoid(x)


def _softplus(x):
    return jnp.maximum(x, 0.0) + jnp.log1p(jnp.exp(-jnp.abs(x)))


def _lane(shape):
    return lax.broadcasted_iota(jnp.int32, shape, len(shape) - 1)


def _l2norm_heads(y, scale):
    outs = []
    for h in range(A_HEADS):
        xh = y[:, h * A_DK:(h + 1) * A_DK]
        ss = jnp.sum(xh * xh, axis=-1, keepdims=True)
        xn = xh * lax.rsqrt(ss + L2_EPS)
        outs.append(xn * scale if scale != 1.0 else xn)
    return jnp.concatenate(outs, axis=-1)


def _rotary_group(xg, cos, sin_signed):
    lane = _lane(xg.shape)
    swapped = jnp.where((lane % B_HD) < (B_HD // 2),
                        pltpu.roll(xg, LANES - B_HD // 2, axis=1),
                        pltpu.roll(xg, B_HD // 2, axis=1))
    return xg * cos + swapped * sin_signed


def _gate_lanes(bd, alog_row, dt_row):
    lane = _lane(bd.shape)
    g = -jnp.exp(alog_row) * _softplus(bd + dt_row)
    return jnp.where(lane < A_HEADS, jax.nn.sigmoid(bd), g)


def _layer_norm(r, g, b):
    mu = jnp.mean(r, axis=-1, keepdims=True)
    d = r - mu
    var = jnp.mean(d * d, axis=-1, keepdims=True)
    return d * lax.rsqrt(var + LN_EPS) * g + b


def _ada_kernel(c_ref, w_ref, b_ref, o_ref):
    o_ref[...] = _dot(c_ref[...].astype(BF16), w_ref[...].astype(BF16)) + b_ref[...]


def _ada(c_all, w_ada, b_ada):
    rows = c_all.shape[0]
    tn = 768
    return pl.pallas_call(
        _ada_kernel,
        grid=(3 * D_MODEL // tn,),
        in_specs=[pl.BlockSpec((rows, D_MODEL), lambda j: (0, 0)),
                  pl.BlockSpec((D_MODEL, tn), lambda j: (0, j)),
                  pl.BlockSpec((1, tn), lambda j: (0, j))],
        out_specs=pl.BlockSpec((rows, tn), lambda j: (0, j)),
        out_shape=jax.ShapeDtypeStruct((rows, 3 * D_MODEL), F32),
        compiler_params=pltpu.CompilerParams(dimension_semantics=("arbitrary",),
                                             vmem_limit_bytes=VMEM_LIMIT),
        name="ada",
    )(c_all, w_ada, b_ada)


def _proj_kernel(x_ref, mod_ref, w_ref, cw_ref, alog_ref, dt_ref, cos_ref, sin_ref,
                 q_ref, k_ref, v_ref, za_ref, gb_ref, qb_ref, kb_ref, vb_ref, zb_ref, cst_ref,
                 cbuf):
    tm = x_ref.shape[1]
    t = pl.program_id(1)

    @pl.when(t == 0)
    def _():
        cbuf[0:8, :] = jnp.zeros((8, A_QKV), F32)

    shift = mod_ref[0, :, 0:D_MODEL]
    scale = mod_ref[0, :, D_MODEL:2 * D_MODEL]
    h = (x_ref[0] * (1.0 + scale) + shift).astype(BF16)

    for gi, o_ref in enumerate((q_ref, k_ref, v_ref)):
        c0 = gi * A_WIDTH
        u = _dot(h, w_ref[:, c0:c0 + A_WIDTH])
        cbuf[8:8 + tm, c0:c0 + A_WIDTH] = u
        acc = cbuf[5:5 + tm, c0:c0 + A_WIDTH] * cw_ref[0:1, c0:c0 + A_WIDTH]
        acc = acc + cbuf[6:6 + tm, c0:c0 + A_WIDTH] * cw_ref[1:2, c0:c0 + A_WIDTH]
        acc = acc + cbuf[7:7 + tm, c0:c0 + A_WIDTH] * cw_ref[2:3, c0:c0 + A_WIDTH]
        acc = acc + u * cw_ref[3:4, c0:c0 + A_WIDTH]
        y = _silu(acc)
        if gi == 0:
            y = _l2norm_heads(y, A_DK ** -0.5)
        elif gi == 1:
            y = _l2norm_heads(y, 1.0)
        o_ref[0] = y
    tail = cbuf[tm + 5:tm + 8, :]
    cst_ref[0] = tail
    cbuf[5:8, :] = tail

    za_ref[0] = _dot(h, w_ref[:, C_ZA:C_ZA + A_WIDTH])
    gb_ref[0] = _gate_lanes(_dot(h, w_ref[:, C_BD:C_BD + LANES]), alog_ref[...], dt_ref[...])

    cos = cos_ref[...]
    sin = sin_ref[...]
    uq = _dot(h, w_ref[:, C_QB:C_QB + B_WIDTH])
    for g in range(B_WIDTH // LANES):
        qb_ref[0, :, g * LANES:(g + 1) * LANES] = (
            _rotary_group(uq[:, g * LANES:(g + 1) * LANES], cos, sin) * (B_HD ** -0.5))
    kb_ref[0] = _rotary_group(_dot(h, w_ref[:, C_KB:C_KB + LANES]), cos, sin)
    vb_ref[0] = _dot(h, w_ref[:, C_VB:C_VB + LANES])
    zb_ref[0] = _dot(h, w_ref[:, C_ZB:C_ZB + B_WIDTH])


def _proj(x, mod_p, w_r, conv_w, alog_row, dt_row, cos_t, sin_t):
    bsz, t, _ = x.shape
    tm = PROJ_TM
    row = lambda w: pl.BlockSpec((1, tm, w), lambda b, i: (b, i, 0))
    const2 = lambda s: pl.BlockSpec(s, lambda b, i: (0, 0))
    wide = lambda w: jax.ShapeDtypeStruct((bsz, t, w), F32)
    return pl.pallas_call(
        _proj_kernel,
        grid=(bsz, t // tm),
        in_specs=[row(D_MODEL),
                  pl.BlockSpec((1, 1, 3 * D_MODEL), lambda b, i: (b, 0, 0)),
                  const2((D_MODEL, W_COLS)),
                  const2((CONV_W, A_QKV)),
                  const2((1, LANES)), const2((1, LANES)),
                  pl.BlockSpec((tm, LANES), lambda b, i: (i, 0)),
                  pl.BlockSpec((tm, LANES), lambda b, i: (i, 0))],
        out_specs=[row(A_WIDTH), row(A_WIDTH), row(A_WIDTH), row(A_WIDTH), row(LANES),
                   row(B_WIDTH), row(LANES), row(LANES), row(B_WIDTH),
                   pl.BlockSpec((1, CONV_W - 1, A_QKV), lambda b, i: (b, 0, 0))],
        out_shape=[wide(A_WIDTH), wide(A_WIDTH), wide(A_WIDTH), wide(A_WIDTH), wide(LANES),
                   wide(B_WIDTH), wide(LANES), wide(LANES), wide(B_WIDTH),
                   jax.ShapeDtypeStruct((bsz, CONV_W - 1, A_QKV), F32)],
        scratch_shapes=[pltpu.VMEM((tm + 8, A_QKV), F32)],
        compiler_params=pltpu.CompilerParams(dimension_semantics=("arbitrary", "arbitrary"),
                                             vmem_limit_bytes=VMEM_LIMIT),
        name="proj",
    )(x, mod_p, w_r, conv_w, alog_row, dt_row, cos_t, sin_t)


def _delta_kernel(q_ref, k_ref, v_ref, gb_ref, za_ref, na_ref, oa_ref, st_ref, s_scr):
    ct = q_ref.shape[1]
    t = pl.program_id(1)

    @pl.when(t == 0)
    def _():
        s_scr[...] = jnp.zeros(s_scr.shape, F32)

    gbv = gb_ref[0]
    rin = lax.broadcasted_iota(jnp.int32, gbv.shape, 0) % CHUNK
    gcs = gbv
    s = 1
    while s < CHUNK:
        gcs = gcs + jnp.where(rin >= s, pltpu.roll(gcs, s, axis=0), 0.0)
        s *= 2
    gcs_t = gcs.T

    ti = lax.broadcasted_iota(jnp.int32, (CHUNK, CHUNK), 0)
    ii = lax.broadcasted_iota(jnp.int32, (CHUNK, CHUNK), 1)
    na = na_ref[...]

    units = [(c, h) for c in range(ct // CHUNK) for h in range(A_HEADS)]
    rows = lambda c: slice(c * CHUNK, (c + 1) * CHUNK)
    lanes = lambda h: slice(h * A_DK, (h + 1) * A_DK)

    beta, g_col, g_last, eg, dec_incl, nmat, aqk = {}, {}, {}, {}, {}, {}, {}
    for (c, h) in units:
        r0 = c * CHUNK
        beta[c, h] = gbv[rows(c), h:h + 1]
        g_col[c, h] = gcs[rows(c), A_HEADS + h:A_HEADS + h + 1]
        g_last[c, h] = gcs[r0 + CHUNK - 1:r0 + CHUNK, A_HEADS + h:A_HEADS + h + 1]
        g_row = gcs_t[A_HEADS + h:A_HEADS + h + 1, rows(c)]
        dec_incl[c, h] = jnp.exp(jnp.where(ti >= ii, g_col[c, h] - g_row, -jnp.inf))
        eg[c, h] = jnp.exp(g_col[c, h])
    for (c, h) in units:
        kc = k_ref[0, rows(c), lanes(h)]
        kq = _dot_nt(jnp.concatenate([kc, q_ref[0, rows(c), lanes(h)]], axis=0), kc)
        nmat[c, h] = -(beta[c, h] * kq[:CHUNK] * jnp.where(ti > ii, dec_incl[c, h], 0.0))
        aqk[c, h] = kq[CHUNK:] * dec_incl[c, h]

    rsum = dict(nmat)
    pw = {u_: _dot(nmat[u_], nmat[u_]) for u_ in units}
    for step in range(1, 6):
        last = step == 5
        rp = {u_: _dot(rsum[u_] if last else jnp.concatenate([rsum[u_], pw[u_]], axis=0), pw[u_])
              for u_ in units}
        for u_ in units:
            rsum[u_] = rsum[u_] + pw[u_] + rp[u_][:CHUNK]
            if not last:
                pw[u_] = rp[u_][CHUNK:]

    wq, ut, akd = {}, {}, {}
    for (c, h) in units:
        kc = k_ref[0, rows(c), lanes(h)]
        rhs = jnp.concatenate([(beta[c, h] * eg[c, h]) * kc, beta[c, h] * v_ref[0, rows(c), lanes(h)]],
                              axis=-1)
        sol = rhs + _dot(rsum[c, h], rhs)
        wq[c, h] = jnp.concatenate([sol[:, :A_DK], eg[c, h] * q_ref[0, rows(c), lanes(h)]], axis=0)
        ut[c, h] = sol[:, A_DK:]
        kd = jnp.exp(g_last[c, h] - g_col[c, h]) * kc
        akd[c, h] = jnp.concatenate([aqk[c, h], kd.T], axis=0)

    s_cur = [s_scr[h] for h in range(A_HEADS)]
    for c in range(ct // CHUNK):
        ws = [_dot(wq[c, h], s_cur[h]) for h in range(A_HEADS)]
        u = [ut[c, h] - ws[h][:CHUNK] for h in range(A_HEADS)]
        ou = [_dot(akd[c, h], u[h]) for h in range(A_HEADS)]
        for h in range(A_HEADS):
            o = ws[h][CHUNK:] + ou[h][:CHUNK]
            s_cur[h] = jnp.exp(g_last[c, h]) * s_cur[h] + ou[h][CHUNK:]
            on = o * lax.rsqrt(jnp.mean(o * o, axis=-1, keepdims=True) + RMS_EPS) * na
            oa_ref[0, rows(c), lanes(h)] = on * _silu(za_ref[0, rows(c), lanes(h)])
    for h in range(A_HEADS):
        s_scr[h] = s_cur[h]

    st_ref[0] = s_scr[...]


def _delta(q, k, v, gb, za, na_row):
    bsz, t, _ = q.shape
    ct = DELTA_CT
    row = lambda w: pl.BlockSpec((1, ct, w), lambda b, i: (b, i, 0))
    return pl.pallas_call(
        _delta_kernel,
        grid=(bsz, t // ct),
        in_specs=[row(A_WIDTH), row(A_WIDTH), row(A_WIDTH), row(LANES), row(A_WIDTH),
                  pl.BlockSpec((1, A_DV), lambda b, i: (0, 0))],
        out_specs=[row(A_WIDTH),
                   pl.BlockSpec((1, A_HEADS, A_DK, A_DV), lambda b, i: (b, 0, 0, 0))],
        out_shape=[jax.ShapeDtypeStruct((bsz, t, A_WIDTH), F32),
                   jax.ShapeDtypeStruct((bsz, A_HEADS, A_DK, A_DV), F32)],
        scratch_shapes=[pltpu.VMEM((A_HEADS, A_DK, A_DV), F32)],
        compiler_params=pltpu.CompilerParams(dimension_semantics=("arbitrary", "arbitrary"),
                                             vmem_limit_bytes=VMEM_LIMIT),
        name="delta",
    )(q, k, v, gb, za, na_row)


def _swa_kernel(sink_ref, qb_ref, kc_ref, kp_ref, vc_ref, vp_ref, zb_ref, ob_ref):
    n = pl.program_id(1)
    blk = qb_ref.shape[1]
    k2 = jnp.concatenate([kp_ref[0], kc_ref[0]], axis=0)
    v2 = jnp.concatenate([vp_ref[0], vc_ref[0]], axis=0)
    k2r = pltpu.roll(k2, B_HD, axis=1)
    v2r = pltpu.roll(v2, B_HD, axis=1)
    low2 = _lane(k2.shape) < B_HD
    vdup = (jnp.where(low2, v2, v2r), jnp.where(low2, v2r, v2))

    a = lax.broadcasted_iota(jnp.int32, (2 * blk, 2 * blk), 0) % blk
    j = lax.broadcasted_iota(jnp.int32, (2 * blk, 2 * blk), 1)
    rel = a + blk - j
    valid = (rel >= 0) & (rel <= WINDOW) & ((n > 0) | (j >= blk))
    top = lax.broadcasted_iota(jnp.int32, (2 * blk, 1), 0) < blk
    low = _lane((blk, LANES)) < B_HD

    outs = {}
    for kh in range(B_KV_HEADS):
        for half in range(2):
            qs = []
            for i in range(2):
                grp = kh * 2 + i
                xg = qb_ref[0, :, grp * LANES:(grp + 1) * LANES]
                qs.append(jnp.where(low if half == 0 else jnp.logical_not(low), xg, 0.0))
            qz = jnp.concatenate(qs, axis=0)
            kx = k2 if kh == half else k2r
            sc = jnp.where(valid, _dot_nt(qz, kx), -jnp.inf)
            sink = jnp.where(top, sink_ref[kh * B_GROUP + half], sink_ref[kh * B_GROUP + half + 2])
            m = jnp.maximum(jnp.max(sc, axis=-1, keepdims=True), sink)
            p = jnp.exp(sc - m)
            den = jnp.sum(p, axis=-1, keepdims=True) + jnp.exp(sink - m)
            outs[(kh, half)] = _dot(p, vdup[kh]) / den

    for grp in range(B_WIDTH // LANES):
        kh, i = grp // 2, grp % 2
        og = jnp.where(low, outs[(kh, 0)][i * blk:(i + 1) * blk], outs[(kh, 1)][i * blk:(i + 1) * blk])
        ob_ref[0, :, grp * LANES:(grp + 1) * LANES] = og * _silu(zb_ref[0, :, grp * LANES:(grp + 1) * LANES])


def _swa(sinks, qb, kb, vb, zb):
    bsz, t, _ = qb.shape
    blk = WINDOW
    cur = lambda w: pl.BlockSpec((1, blk, w), lambda b, i: (b, i, 0))
    prev = lambda w: pl.BlockSpec((1, blk, w), lambda b, i: (b, jnp.maximum(i - 1, 0), 0))
    return pl.pallas_call(
        _swa_kernel,
        grid=(bsz, t // blk),
        in_specs=[pl.BlockSpec(memory_space=pltpu.SMEM),
                  cur(B_WIDTH), cur(LANES), prev(LANES), cur(LANES), prev(LANES), cur(B_WIDTH)],
        out_specs=cur(B_WIDTH),
        out_shape=jax.ShapeDtypeStruct((bsz, t, B_WIDTH), F32),
        compiler_params=pltpu.CompilerParams(dimension_semantics=("arbitrary", "arbitrary"),
                                             vmem_limit_bytes=VMEM_LIMIT),
        name="swa",
    )(sinks, qb, kb, kb, vb, vb, zb)


def _out_kernel(oa_ref, ob_ref, x_ref, gate_ref, w_ref, g_ref, b_ref, y_ref):
    mix = (_dot(oa_ref[0].astype(BF16), w_ref[0:A_WIDTH, :])
           + _dot(ob_ref[0].astype(BF16), w_ref[A_WIDTH:MIX_WIDTH, :]))
    r = DEEPNORM_ALPHA * x_ref[0] + (1.0 + gate_ref[0]) * mix
    y_ref[0] = _layer_norm(r, g_ref[...], b_ref[...])


def _out(oa, ob, x, gate, w_out, ln_g, ln_b, tm):
    bsz, t, _ = x.shape
    gr = gate.shape[1]
    row = lambda w: pl.BlockSpec((1, tm, w), lambda b, i: (b, i, 0))
    gate_spec = (pl.BlockSpec((1, 1, D_MODEL), lambda b, i: (b, 0, 0)) if gr == 1
                 else row(D_MODEL))
    const2 = lambda s: pl.BlockSpec(s, lambda b, i: (0, 0))
    return pl.pallas_call(
        _out_kernel,
        grid=(bsz, t // tm),
        in_specs=[row(A_WIDTH), row(B_WIDTH), row(D_MODEL), gate_spec,
                  const2((MIX_WIDTH, D_MODEL)), const2((1, D_MODEL)), const2((1, D_MODEL))],
        out_specs=row(D_MODEL),
        out_shape=jax.ShapeDtypeStruct((bsz, t, D_MODEL), F32),
        compiler_params=pltpu.CompilerParams(dimension_semantics=("arbitrary", "arbitrary"),
                                             vmem_limit_bytes=VMEM_LIMIT),
        name="out",
    )(oa, ob, x, gate, w_out, ln_g, ln_b)


def _sproj_kernel(x_ref, mod_ref, w_ref, cw_ref, cst_ref, alog_ref, dt_ref, cos_ref, sin_ref,
                  q_ref, k_ref, v_ref, za_ref, gb_ref, qb_ref, kb_ref, vb_ref, zb_ref, ncs_ref):
    shift = mod_ref[:, 0:D_MODEL]
    scale = mod_ref[:, D_MODEL:2 * D_MODEL]
    h = (x_ref[...] * (1.0 + scale) + shift).astype(BF16)

    for gi, o_ref in enumerate((q_ref, k_ref, v_ref)):
        c0 = gi * A_WIDTH
        cs = slice(c0, c0 + A_WIDTH)
        u = _dot(h, w_ref[:, cs])
        acc = cst_ref[0, :, cs] * cw_ref[0:1, cs]
        acc = acc + cst_ref[1, :, cs] * cw_ref[1:2, cs]
        acc = acc + cst_ref[2, :, cs] * cw_ref[2:3, cs]
        acc = acc + u * cw_ref[3:4, cs]
        y = _silu(acc)
        if gi == 0:
            y = _l2norm_heads(y, A_DK ** -0.5)
        elif gi == 1:
            y = _l2norm_heads(y, 1.0)
        o_ref[...] = y
        ncs_ref[0, :, cs] = cst_ref[1, :, cs]
        ncs_ref[1, :, cs] = cst_ref[2, :, cs]
        ncs_ref[2, :, cs] = u

    za_ref[...] = _dot(h, w_ref[:, C_ZA:C_ZA + A_WIDTH])
    gb_ref[...] = _gate_lanes(_dot(h, w_ref[:, C_BD:C_BD + LANES]), alog_ref[...], dt_ref[...])

    cos = cos_ref[...]
    sin = sin_ref[...]
    uq = _dot(h, w_ref[:, C_QB:C_QB + B_WIDTH])
    for g in range(B_WIDTH // LANES):
        qb_ref[:, g * LANES:(g + 1) * LANES] = (
            _rotary_group(uq[:, g * LANES:(g + 1) * LANES], cos, sin) * (B_HD ** -0.5))
    kb_ref[...] = _rotary_group(_dot(h, w_ref[:, C_KB:C_KB + LANES]), cos, sin)
    vb_ref[...] = _dot(h, w_ref[:, C_VB:C_VB + LANES])
    zb_ref[...] = _dot(h, w_ref[:, C_ZB:C_ZB + B_WIDTH])


def _sproj(x, mod_s, w_r, conv_w, cst, alog_row, dt_row, cos_row, sin_row):
    n = x.shape[0]
    full = lambda s: pl.BlockSpec(s, lambda i: (0,) * len(s))
    wide = lambda w: jax.ShapeDtypeStruct((n, w), F32)
    return pl.pallas_call(
        _sproj_kernel,
        grid=(1,),
        in_specs=[full((n, D_MODEL)), full((n, 3 * D_MODEL)), full((D_MODEL, W_COLS)),
                  full((CONV_W, A_QKV)), full((CONV_W - 1, n, A_QKV)),
                  full((1, LANES)), full((1, LANES)), full((1, LANES)), full((1, LANES))],
        out_specs=[full((n, A_WIDTH)), full((n, A_WIDTH)), full((n, A_WIDTH)), full((n, A_WIDTH)),
                   full((n, LANES)), full((n, B_WIDTH)), full((n, LANES)), full((n, LANES)),
                   full((n, B_WIDTH)), full((CONV_W - 1, n, A_QKV))],
        out_shape=[wide(A_WIDTH), wide(A_WIDTH), wide(A_WIDTH), wide(A_WIDTH), wide(LANES),
                   wide(B_WIDTH), wide(LANES), wide(LANES), wide(B_WIDTH),
                   jax.ShapeDtypeStruct((CONV_W - 1, n, A_QKV), F32)],
        compiler_params=pltpu.CompilerParams(dimension_semantics=("arbitrary",),
                                             vmem_limit_bytes=VMEM_LIMIT),
        name="sproj",
    )(x, mod_s, w_r, conv_w, cst, alog_row, dt_row, cos_row, sin_row)


def _sstep_kernel(sink_ref, q_ref, k_ref, v_ref, gb_ref, za_ref, na_ref, st_ref,
                  qb_ref, kn_ref, vn_ref, zb_ref, ck_ref, cv_ref,
                  oa_ref, ob_ref, nst_ref, nck_ref, ncv_ref,
                  o_scr, ob_scr):
    bt = q_ref.shape[0]
    gbv = gb_ref[...]

    for h in range(A_HEADS):
        hs = slice(h * A_DK, (h + 1) * A_DK)
        q_t = q_ref[:, hs].T
        k_t = k_ref[:, hs].T
        for bb in range(bt):
            eg = jnp.exp(gbv[bb:bb + 1, A_HEADS + h:A_HEADS + h + 1])
            beta = gbv[bb:bb + 1, h:h + 1]
            kcol = k_t[:, bb:bb + 1]
            qcol = q_t[:, bb:bb + 1]
            s1 = eg * st_ref[bb, h]
            pred = jnp.sum(kcol * s1, axis=0, keepdims=True)
            upd = beta * (v_ref[bb:bb + 1, hs] - pred)
            s2 = s1 + kcol * upd
            nst_ref[bb, h] = s2
            o_scr[bb:bb + 1, hs] = jnp.sum(qcol * s2, axis=0, keepdims=True)
    na = na_ref[...]
    for h in range(A_HEADS):
        hs = slice(h * A_DK, (h + 1) * A_DK)
        o = o_scr[:, hs]
        on = o * lax.rsqrt(jnp.mean(o * o, axis=-1, keepdims=True) + RMS_EPS) * na
        oa_ref[:, hs] = on * _silu(za_ref[:, hs])

    row8 = lax.broadcasted_iota(jnp.int32, (B_HEADS, LANES), 0)
    lane8 = _lane((B_HEADS, LANES))
    own_half = (lane8 >= B_HD) == (row8 >= B_GROUP)
    rcol = lax.broadcasted_iota(jnp.int32, (B_HEADS, 1), 0)
    sink = jnp.zeros((B_HEADS, 1), F32)
    for r in range(B_HEADS):
        sink = jnp.where(rcol == r, sink_ref[r], sink)
    qv = qb_ref[...]
    qv_r = jnp.concatenate([pltpu.roll(qv[:, g * LANES:(g + 1) * LANES], B_HD, axis=1)
                            for g in range(B_WIDTH // LANES)], axis=-1)
    for bb in range(bt):
        qz = jnp.zeros((B_HEADS, LANES), F32)
        for r in range(B_HEADS):
            grp, half, kh = r // 2, r % 2, r // B_GROUP
            src = qv if half == kh else qv_r
            qz = jnp.where(row8 == r, src[bb:bb + 1, grp * LANES:(grp + 1) * LANES], qz)
        qz = jnp.where(own_half, qz, 0.0)
        kc = ck_ref[bb]
        vc = cv_ref[bb]
        knew = kn_ref[bb:bb + 1, :]
        vnew = vn_ref[bb:bb + 1, :]
        sc = _dot_nt(qz, kc)
        sc_new = jnp.sum(qz * knew, axis=-1, keepdims=True)
        m = jnp.maximum(jnp.maximum(jnp.max(sc, axis=-1, keepdims=True), sc_new), sink)
        p = jnp.exp(sc - m)
        p_new = jnp.exp(sc_new - m)
        den = jnp.sum(p, axis=-1, keepdims=True) + p_new + jnp.exp(sink - m)
        o = (_dot(p, vc) + p_new * vnew) / den
        o = jnp.where(own_half, o, 0.0)
        ob_scr[bb * B_HEADS:(bb + 1) * B_HEADS, :] = o + pltpu.roll(o, B_HD, axis=1)
        nck_ref[bb, 0:WINDOW - 1, :] = ck_ref[bb, 1:WINDOW, :]
        nck_ref[bb, WINDOW - 1:WINDOW, :] = knew
        ncv_ref[bb, 0:WINDOW - 1, :] = cv_ref[bb, 1:WINDOW, :]
        ncv_ref[bb, WINDOW - 1:WINDOW, :] = vnew
    low = _lane((bt, LANES)) < B_HD
    for grp in range(B_WIDTH // LANES):
        even = ob_scr[pl.ds(2 * grp, bt, stride=B_HEADS), :]
        odd = ob_scr[pl.ds(2 * grp + 1, bt, stride=B_HEADS), :]
        gs = slice(grp * LANES, (grp + 1) * LANES)
        ob_ref[:, gs] = jnp.where(low, even, odd) * _silu(zb_ref[:, gs])


def _sstep(sinks, q, k, v, gb, za, na_row, state, qb, kn, vn, zb, ck, cv):
    n = q.shape[0]
    bt = STEP_BT
    row = lambda w: pl.BlockSpec((bt, w), lambda i: (i, 0))
    st_spec = pl.BlockSpec((bt, A_HEADS, A_DK, A_DV), lambda i: (i, 0, 0, 0))
    c_spec = pl.BlockSpec((bt, WINDOW, LANES), lambda i: (i, 0, 0))
    return pl.pallas_call(
        _sstep_kernel,
        grid=(n // bt,),
        in_specs=[pl.BlockSpec(memory_space=pltpu.SMEM),
                  row(A_WIDTH), row(A_WIDTH), row(A_WIDTH), row(LANES), row(A_WIDTH),
                  pl.BlockSpec((1, A_DV), lambda i: (0, 0)), st_spec,
                  row(B_WIDTH), row(LANES), row(LANES), row(B_WIDTH), c_spec, c_spec],
        out_specs=[row(A_WIDTH), row(B_WIDTH), st_spec, c_spec, c_spec],
        out_shape=[jax.ShapeDtypeStruct((n, A_WIDTH), F32),
                   jax.ShapeDtypeStruct((n, B_WIDTH), F32),
                   jax.ShapeDtypeStruct((n, A_HEADS, A_DK, A_DV), F32),
                   jax.ShapeDtypeStruct((n, WINDOW, LANES), F32),
                   jax.ShapeDtypeStruct((n, WINDOW, LANES), F32)],
        scratch_shapes=[pltpu.VMEM((bt, A_WIDTH), F32), pltpu.VMEM((bt * B_HEADS, LANES), F32)],
        compiler_params=pltpu.CompilerParams(dimension_semantics=("arbitrary",),
                                             vmem_limit_bytes=VMEM_LIMIT),
        name="sstep",
    )(sinks, q, k, v, gb, za, na_row, state, qb, kn, vn, zb, ck, cv)


def _rope_tables(pos):
    half = B_HD // 2
    inv = 1.0 / (ROPE_THETA ** (jnp.arange(half, dtype=F32) / half))
    ang = pos.astype(F32)[:, None] * inv[None, :]
    cos = jnp.cos(ang)
    sin = jnp.sin(ang)
    reps = LANES // B_HD
    return (jnp.tile(jnp.concatenate([cos, cos], -1), (1, reps)),
            jnp.tile(jnp.concatenate([-sin, sin], -1), (1, reps)))


def _pad_row(vec, offset):
    return jnp.zeros((1, LANES), F32).at[0, offset:offset + vec.shape[0]].set(vec.astype(F32))


def _layer(x_prompt, x_sample, state_conv, state_delta, cache_k, cache_v, c_prompt, c_sample,
           w_ada, b_ada, w_in, conv_w, a_log, dt_bias, norm_a, sinks, w_out, ln_g, ln_b):
    bsz, seq, _ = x_prompt.shape
    n_s = x_sample.shape[0]

    w_r = jnp.concatenate([
        w_in[:, 0:OFF_A_BETA], w_in[:, OFF_B_Q:PROJ_COLS], w_in[:, OFF_A_BETA:OFF_B_Q],
        jnp.zeros((D_MODEL, LANES - 2 * A_HEADS), w_in.dtype)], axis=1).astype(BF16)
    w_o = w_out.astype(BF16)
    alog_row = _pad_row(a_log, A_HEADS)
    dt_row = _pad_row(dt_bias, A_HEADS)
    na_row = norm_a.reshape(1, A_DV)
    g_row = ln_g.reshape(1, D_MODEL)
    b_row = ln_b.reshape(1, D_MODEL)

    c_all = jnp.concatenate([c_sample, c_prompt, jnp.zeros((8 - bsz, D_MODEL), F32)], axis=0)
    mod = _ada(c_all, w_ada, b_ada.reshape(1, 3 * D_MODEL))
    mod_s = mod[0:n_s]
    mod_p = mod[n_s:n_s + bsz].reshape(bsz, 1, 3 * D_MODEL)

    cos_p, sin_p = _rope_tables(jnp.arange(seq))
    q, k, v, za, gb, qb, kb, vb, zb, conv_p = _proj(x_prompt, mod_p, w_r, conv_w, alog_row, dt_row,
                                                    cos_p, sin_p)
    oa, delta_p = _delta(q, k, v, gb, za, na_row)
    ob = _swa(sinks, qb, kb, vb, zb)
    y_p = _out(oa, ob, x_prompt, mod_p[:, :, 2 * D_MODEL:], w_o, g_row, b_row, OUT_TM)
    swa_k_p = kb[:, seq - WINDOW:, :].reshape(bsz, WINDOW, B_KV_HEADS, B_HD)
    swa_v_p = vb[:, seq - WINDOW:, :].reshape(bsz, WINDOW, B_KV_HEADS, B_HD)

    cos_s, sin_s = _rope_tables(PAST_LEN + jnp.arange(1))
    xs = x_sample.reshape(n_s, D_MODEL)
    cst = jnp.transpose(state_conv, (1, 0, 2))
    sq, sk, sv, sza, sgb, sqb, skn, svn, szb, ncs = _sproj(xs, mod_s, w_r, conv_w, cst, alog_row, dt_row,
                                                           cos_s, sin_s)
    soa, sob, delta_s, nck, ncv = _sstep(sinks, sq, sk, sv, sgb, sza, na_row, state_delta,
                                         sqb, skn, svn, szb,
                                         cache_k.reshape(n_s, WINDOW, LANES),
                                         cache_v.reshape(n_s, WINDOW, LANES))
    y_s = _out(soa[None], sob[None], xs[None], mod_s[None, :, 2 * D_MODEL:], w_o, g_row, b_row, n_s)
    conv_s = jnp.transpose(ncs, (1, 0, 2))
    return (y_p, y_s.reshape(n_s, 1, D_MODEL), conv_p, delta_p, swa_k_p, swa_v_p,
            conv_s, delta_s,
            nck.reshape(n_s, WINDOW, B_KV_HEADS, B_HD), ncv.reshape(n_s, WINDOW, B_KV_HEADS, B_HD))


def kernel(x_prompt, x_sample, state_conv, state_delta, cache_swa_k, cache_swa_v, c_prompt, c_sample,
           w_ada, b_ada, w_in, conv_w, a_log, dt_bias, norm_a, sinks, w_out, ln_g, ln_b):
    assert w_ada.shape[0] == DEPTH == 1
    outs = _layer(x_prompt, x_sample, state_conv[0], state_delta[0], cache_swa_k[0], cache_swa_v[0],
                  c_prompt, c_sample, w_ada[0], b_ada[0], w_in[0], conv_w[0], a_log[0], dt_bias[0],
                  norm_a[0], sinks[0], w_out[0], ln_g[0], ln_b[0])
    y_p, y_s = outs[0], outs[1]
    return (y_p, y_s) + tuple(o[None] for o in outs[2:])
```

```python
import jax
import jax.numpy as jnp
from jax import lax
from jax.experimental import pallas as pl
from jax.experimental.pallas import tpu as pltpu

F32 = jnp.float32
BF16 = jnp.bfloat16

D_MODEL = 1024
DEPTH = 1
PAST_LEN = 8192
A_HEADS = 4
A_DK = 128
A_DV = 128
A_WIDTH = A_HEADS * A_DV
A_QKV = 3 * A_WIDTH
CONV_W = 4
CHUNK = 64
B_HEADS = 8
B_KV_HEADS = 2
B_HD = 64
B_GROUP = B_HEADS // B_KV_HEADS
B_WIDTH = B_HEADS * B_HD
B_KV_WIDTH = B_KV_HEADS * B_HD
WINDOW = 128
ROPE_THETA = 10000.0
MIX_WIDTH = A_WIDTH + B_WIDTH
DEEPNORM_ALPHA = (2 * DEPTH) ** 0.25
LN_EPS = 1e-5
RMS_EPS = 1e-6
L2_EPS = 1e-6

OFF_A_Z = A_QKV
OFF_A_BETA = OFF_A_Z + A_WIDTH
OFF_A_DECAY = OFF_A_BETA + A_HEADS
OFF_B_Q = OFF_A_DECAY + A_HEADS
OFF_B_K = OFF_B_Q + B_WIDTH
OFF_B_V = OFF_B_K + B_KV_WIDTH
OFF_B_Z = OFF_B_V + B_KV_WIDTH
PROJ_COLS = OFF_B_Z + B_WIDTH

LANES = 128
C_QKV = 0
C_ZA = C_QKV + A_QKV
C_QB = C_ZA + A_WIDTH
C_KB = C_QB + B_WIDTH
C_VB = C_KB + B_KV_WIDTH
C_ZB = C_VB + B_KV_WIDTH
C_BD = C_ZB + B_WIDTH
W_COLS = C_BD + LANES

VMEM_LIMIT = 56 * 1024 * 1024

PROJ_TM = 512
DELTA_CT = 256
SWA_TQ = 512
OUT_TM = 512
STEP_BT = 16


def _dot(a, b):
    return jnp.dot(a, b, preferred_element_type=F32)


def _dot_nt(a, b):
    return lax.dot_general(a, b, (((1,), (1,)), ((), ())), preferred_element_type=F32)


def _silu(x):
    return x * jax.nn.sigmoid(x)


def _softplus(x):
    return jnp.maximum(x, 0.0) + jnp.log1p(jnp.exp(-jnp.abs(x)))


def _lane(shape):
    return lax.broadcasted_iota(jnp.int32, shape, len(shape) - 1)


def _l2norm_heads(y, scale):
    outs = []
    for h in range(A_HEADS):
        xh = y[:, h * A_DK:(h + 1) * A_DK]
        ss = jnp.sum(xh * xh, axis=-1, keepdims=True)
        xn = xh * lax.rsqrt(ss + L2_EPS)
        outs.append(xn * scale if scale != 1.0 else xn)
    return jnp.concatenate(outs, axis=-1)


def _rotary_group(xg, cos, sin_signed):
    lane = _lane(xg.shape)
    swapped = jnp.where((lane % B_HD) < (B_HD // 2),
                        pltpu.roll(xg, LANES - B_HD // 2, axis=1),
                        pltpu.roll(xg, B_HD // 2, axis=1))
    return xg * cos + swapped * sin_signed


def _kv_layouts(kb, vb):
    low = _lane(kb.shape) < B_HD
    kbr = pltpu.roll(kb, B_HD, axis=1)
    vbr = pltpu.roll(vb, B_HD, axis=1)
    return kb, kbr, jnp.where(low, vb, vbr), jnp.where(low, vbr, vb)


def _gate_lanes(bd, alog_row, dt_row):
    lane = _lane(bd.shape)
    g = -jnp.exp(alog_row) * _softplus(bd + dt_row)
    return jnp.where(lane < A_HEADS, jax.nn.sigmoid(bd), g)


def _layer_norm(r, g, b):
    mu = jnp.mean(r, axis=-1, keepdims=True)
    d = r - mu
    var = jnp.mean(d * d, axis=-1, keepdims=True)
    return d * lax.rsqrt(var + LN_EPS) * g + b


def _ada_kernel(c_ref, w_ref, b_ref, o_ref):
    o_ref[...] = _dot(c_ref[...].astype(BF16), w_ref[...].astype(BF16)) + b_ref[...]


def _ada(c_all, w_ada, b_ada):
    rows = c_all.shape[0]
    tn = 768
    return pl.pallas_call(
        _ada_kernel,
        grid=(3 * D_MODEL // tn,),
        in_specs=[pl.BlockSpec((rows, D_MODEL), lambda j: (0, 0)),
                  pl.BlockSpec((D_MODEL, tn), lambda j: (0, j)),
                  pl.BlockSpec((1, tn), lambda j: (0, j))],
        out_specs=pl.BlockSpec((rows, tn), lambda j: (0, j)),
        out_shape=jax.ShapeDtypeStruct((rows, 3 * D_MODEL), F32),
        compiler_params=pltpu.CompilerParams(dimension_semantics=("arbitrary",),
                                             vmem_limit_bytes=VMEM_LIMIT),
        name="ada",
    )(c_all, w_ada, b_ada)


def _proj_kernel(x_ref, mod_ref, w_ref, cw_ref, alog_ref, dt_ref, cos_ref, sin_ref,
                 q_ref, k_ref, v_ref, za_ref, gb_ref, qb_ref, kb_ref, kbr_ref, vd0_ref, vd1_ref,
                 zb_ref, cst_ref, kbl_ref, vbl_ref, tail):
    tm = x_ref.shape[1]
    t = pl.program_id(1)

    @pl.when(t == 0)
    def _():
        tail[...] = jnp.zeros(tail.shape, F32)

    shift = mod_ref[0, :, 0:D_MODEL]
    scale = mod_ref[0, :, D_MODEL:2 * D_MODEL]
    h = (x_ref[0] * (1.0 + scale) + shift).astype(BF16)

    row8 = lax.broadcasted_iota(jnp.int32, (8, A_WIDTH), 0)
    for gi, o_ref in enumerate((q_ref, k_ref, v_ref)):
        cs = slice(gi * A_WIDTH, (gi + 1) * A_WIDTH)
        u = _dot(h, w_ref[:, cs])
        prev = tail[:, cs]
        acc = None
        head = None
        for j in range(CONV_W - 1, 0, -1):
            wj = cw_ref[CONV_W - 1 - j:CONV_W - j, cs]
            rolled = pltpu.roll(u, j, axis=0)
            first = jnp.where(row8 < j, pltpu.roll(prev, j, axis=0), rolled[0:8]) * wj
            acc = rolled * wj if acc is None else acc + rolled * wj
            head = first if head is None else head + first
        w_last = cw_ref[CONV_W - 1:CONV_W, cs]
        acc = acc + u * w_last
        head = head + u[0:8] * w_last
        y = _silu(jnp.concatenate([head, acc[8:]], axis=0))
        if gi == 0:
            y = _l2norm_heads(y, A_DK ** -0.5)
        elif gi == 1:
            y = _l2norm_heads(y, 1.0)
        o_ref[0] = y
        tail[:, cs] = u[tm - 8:tm]
        cst_ref[0, :, cs] = u[tm - (CONV_W - 1):tm]

    za_ref[0] = _dot(h, w_ref[:, C_ZA:C_ZA + A_WIDTH])
    gb_ref[0] = _gate_lanes(_dot(h, w_ref[:, C_BD:C_BD + LANES]), alog_ref[...], dt_ref[...])

    cos = cos_ref[...]
    sin = sin_ref[...]
    uq = _dot(h, w_ref[:, C_QB:C_QB + B_WIDTH])
    for g in range(B_WIDTH // LANES):
        qb_ref[0, :, g * LANES:(g + 1) * LANES] = (
            _rotary_group(uq[:, g * LANES:(g + 1) * LANES], cos, sin) * (B_HD ** -0.5)).astype(BF16)
    kb = _rotary_group(_dot(h, w_ref[:, C_KB:C_KB + LANES]), cos, sin)
    vb = _dot(h, w_ref[:, C_VB:C_VB + LANES])
    for o_ref, val in zip((kb_ref, kbr_ref, vd0_ref, vd1_ref), _kv_layouts(kb, vb)):
        o_ref[0] = val.astype(BF16)
    zb_ref[0] = _dot(h, w_ref[:, C_ZB:C_ZB + B_WIDTH])

    @pl.when(t == pl.num_programs(1) - 1)
    def _():
        kbl_ref[0] = kb[tm - WINDOW:tm]
        vbl_ref[0] = vb[tm - WINDOW:tm]


def _proj(x, mod_p, w_r, conv_w, alog_row, dt_row, cos_t, sin_t):
    bsz, t, _ = x.shape
    tm = PROJ_TM
    row = lambda w: pl.BlockSpec((1, tm, w), lambda b, i: (b, i, 0))
    const2 = lambda s: pl.BlockSpec(s, lambda b, i: (0, 0))
    per_b = lambda r, w: pl.BlockSpec((1, r, w), lambda b, i: (b, 0, 0))
    wide = lambda w, dt=F32: jax.ShapeDtypeStruct((bsz, t, w), dt)
    return pl.pallas_call(
        _proj_kernel,
        grid=(bsz, t // tm),
        in_specs=[row(D_MODEL),
                  per_b(1, 3 * D_MODEL),
                  const2((D_MODEL, W_COLS)),
                  const2((CONV_W, A_QKV)),
                  const2((1, LANES)), const2((1, LANES)),
                  pl.BlockSpec((tm, LANES), lambda b, i: (i, 0)),
                  pl.BlockSpec((tm, LANES), lambda b, i: (i, 0))],
        out_specs=[row(A_WIDTH), row(A_WIDTH), row(A_WIDTH), row(A_WIDTH), row(LANES),
                   row(B_WIDTH), row(LANES), row(LANES), row(LANES), row(LANES), row(B_WIDTH),
                   per_b(CONV_W - 1, A_QKV), per_b(WINDOW, LANES), per_b(WINDOW, LANES)],
        out_shape=[wide(A_WIDTH), wide(A_WIDTH), wide(A_WIDTH), wide(A_WIDTH), wide(LANES),
                   wide(B_WIDTH, BF16), wide(LANES, BF16), wide(LANES, BF16), wide(LANES, BF16),
                   wide(LANES, BF16), wide(B_WIDTH),
                   jax.ShapeDtypeStruct((bsz, CONV_W - 1, A_QKV), F32),
                   jax.ShapeDtypeStruct((bsz, WINDOW, LANES), F32),
                   jax.ShapeDtypeStruct((bsz, WINDOW, LANES), F32)],
        scratch_shapes=[pltpu.VMEM((8, A_QKV), F32)],
        compiler_params=pltpu.CompilerParams(dimension_semantics=("arbitrary", "arbitrary"),
                                             vmem_limit_bytes=VMEM_LIMIT),
        name="proj",
    )(x, mod_p, w_r, conv_w, alog_row, dt_row, cos_t, sin_t)


def _delta_kernel(q_ref, k_ref, v_ref, gb_ref, za_ref, na_ref, oa_ref, st_ref, s_scr):
    ct = q_ref.shape[1]
    t = pl.program_id(1)

    @pl.when(t == 0)
    def _():
        s_scr[...] = jnp.zeros(s_scr.shape, F32)

    gbv = gb_ref[0]
    rin = lax.broadcasted_iota(jnp.int32, gbv.shape, 0) % CHUNK
    gcs = gbv
    s = 1
    while s < CHUNK:
        gcs = gcs + jnp.where(rin >= s, pltpu.roll(gcs, s, axis=0), 0.0)
        s *= 2
    gcs_t = gcs.T

    ti = lax.broadcasted_iota(jnp.int32, (CHUNK, CHUNK), 0)
    ii = lax.broadcasted_iota(jnp.int32, (CHUNK, CHUNK), 1)
    na = na_ref[...]

    units = [(c, h) for c in range(ct // CHUNK) for h in range(A_HEADS)]
    rows = lambda c: slice(c * CHUNK, (c + 1) * CHUNK)
    lanes = lambda h: slice(h * A_DK, (h + 1) * A_DK)

    beta, g_col, g_last, eg, dec_incl, nmat, aqk = {}, {}, {}, {}, {}, {}, {}
    for (c, h) in units:
        r0 = c * CHUNK
        beta[c, h] = gbv[rows(c), h:h + 1]
        g_col[c, h] = gcs[rows(c), A_HEADS + h:A_HEADS + h + 1]
        g_last[c, h] = gcs[r0 + CHUNK - 1:r0 + CHUNK, A_HEADS + h:A_HEADS + h + 1]
        g_row = gcs_t[A_HEADS + h:A_HEADS + h + 1, rows(c)]
        dec_incl[c, h] = jnp.exp(jnp.where(ti >= ii, g_col[c, h] - g_row, -jnp.inf))
        eg[c, h] = jnp.exp(g_col[c, h])
    for (c, h) in units:
        kcb = k_ref[0, rows(c), lanes(h)].astype(BF16)
        qcb = q_ref[0, rows(c), lanes(h)].astype(BF16)
        kq = _dot_nt(jnp.concatenate([kcb, qcb], axis=0), kcb)
        nmat[c, h] = -(beta[c, h] * kq[:CHUNK] * jnp.where(ti > ii, dec_incl[c, h], 0.0))
        aqk[c, h] = kq[CHUNK:] * dec_incl[c, h]

    rsum = dict(nmat)
    pwb = {u_: nmat[u_].astype(BF16) for u_ in units}
    pw = {u_: _dot(pwb[u_], pwb[u_]) for u_ in units}
    for step in range(1, 6):
        last = step == 5
        pwb = {u_: pw[u_].astype(BF16) for u_ in units}
        rp = {}
        for u_ in units:
            rb = rsum[u_].astype(BF16)
            rp[u_] = _dot(rb if last else jnp.concatenate([rb, pwb[u_]], axis=0), pwb[u_])
        for u_ in units:
            rsum[u_] = rsum[u_] + pw[u_] + rp[u_][:CHUNK]
            if not last:
                pw[u_] = rp[u_][CHUNK:]

    wq, ut, akd = {}, {}, {}
    for (c, h) in units:
        kc = k_ref[0, rows(c), lanes(h)]
        rhs = jnp.concatenate([(beta[c, h] * eg[c, h]) * kc, beta[c, h] * v_ref[0, rows(c), lanes(h)]],
                              axis=-1)
        sol = rhs + _dot(rsum[c, h].astype(BF16), rhs.astype(BF16))
        wq[c, h] = jnp.concatenate([sol[:, :A_DK], eg[c, h] * q_ref[0, rows(c), lanes(h)]],
                                   axis=0).astype(BF16)
        ut[c, h] = sol[:, A_DK:]
        kd = jnp.exp(g_last[c, h] - g_col[c, h]) * kc
        akd[c, h] = jnp.concatenate([aqk[c, h], kd.T], axis=0).astype(BF16)

    s_cur = [s_scr[h] for h in range(A_HEADS)]
    for c in range(ct // CHUNK):
        ws = [_dot(wq[c, h], s_cur[h].astype(BF16)) for h in range(A_HEADS)]
        u = [ut[c, h] - ws[h][:CHUNK] for h in range(A_HEADS)]
        ou = [_dot(akd[c, h], u[h].astype(BF16)) for h in range(A_HEADS)]
        for h in range(A_HEADS):
            o = ws[h][CHUNK:] + ou[h][:CHUNK]
            s_cur[h] = jnp.exp(g_last[c, h]) * s_cur[h] + ou[h][CHUNK:]
            on = o * lax.rsqrt(jnp.mean(o * o, axis=-1, keepdims=True) + RMS_EPS) * na
            oa_ref[0, rows(c), lanes(h)] = (on * _silu(za_ref[0, rows(c), lanes(h)])).astype(BF16)
    for h in range(A_HEADS):
        s_scr[h] = s_cur[h]

    st_ref[0] = s_scr[...]


def _delta(q, k, v, gb, za, na_row):
    bsz, t, _ = q.shape
    ct = DELTA_CT
    row = lambda w: pl.BlockSpec((1, ct, w), lambda b, i: (b, i, 0))
    return pl.pallas_call(
        _delta_kernel,
        grid=(bsz, t // ct),
        in_specs=[row(A_WIDTH), row(A_WIDTH), row(A_WIDTH), row(LANES), row(A_WIDTH),
                  pl.BlockSpec((1, A_DV), lambda b, i: (0, 0))],
        out_specs=[row(A_WIDTH),
                   pl.BlockSpec((1, A_HEADS, A_DK, A_DV), lambda b, i: (b, 0, 0, 0))],
        out_shape=[jax.ShapeDtypeStruct((bsz, t, A_WIDTH), BF16),
                   jax.ShapeDtypeStruct((bsz, A_HEADS, A_DK, A_DV), F32)],
        scratch_shapes=[pltpu.VMEM((A_HEADS, A_DK, A_DV), F32)],
        compiler_params=pltpu.CompilerParams(dimension_semantics=("arbitrary", "arbitrary"),
                                             vmem_limit_bytes=VMEM_LIMIT),
        name="delta",
    )(q, k, v, gb, za, na_row)


def _swa_kernel(sink_ref, qb_ref, kc_ref, kp_ref, krc_ref, krp_ref, v0c_ref, v0p_ref, v1c_ref, v1p_ref,
                zb_ref, ob_ref):
    n = pl.program_id(1)
    tq = qb_ref.shape[1]
    blk = WINDOW
    kx = (jnp.concatenate([kp_ref[0], kc_ref[0]], axis=0), jnp.concatenate([krp_ref[0], krc_ref[0]], axis=0))
    vd = (jnp.concatenate([v0p_ref[0], v0c_ref[0]], axis=0), jnp.concatenate([v1p_ref[0], v1c_ref[0]], axis=0))

    a = lax.broadcasted_iota(jnp.int32, (2 * blk, 2 * blk), 0) % blk
    j = lax.broadcasted_iota(jnp.int32, (2 * blk, 2 * blk), 1)
    rel = a + blk - j
    band = (rel >= 0) & (rel <= WINDOW)
    band_first = band & ((n > 0) | (j >= blk))
    top = lax.broadcasted_iota(jnp.int32, (2 * blk, 1), 0) < blk
    low = _lane((blk, LANES)) < B_HD
    zero = jnp.zeros((blk, LANES), BF16)

    for i in range(tq // blk):
        valid = band_first if i == 0 else band
        qrows = slice(i * blk, (i + 1) * blk)
        krows = slice(i * blk, (i + 2) * blk)
        outs = {}
        for kh in range(B_KV_HEADS):
            for half in range(2):
                qs = []
                for g in range(2):
                    grp = kh * 2 + g
                    xg = qb_ref[0, qrows, grp * LANES:(grp + 1) * LANES]
                    qs.append(jnp.where(low if half == 0 else jnp.logical_not(low), xg, zero))
                qz = jnp.concatenate(qs, axis=0)
                sc = jnp.where(valid, _dot_nt(qz, kx[0 if kh == half else 1][krows]), -jnp.inf)
                sink = jnp.where(top, sink_ref[kh * B_GROUP + half], sink_ref[kh * B_GROUP + half + 2])
                m = jnp.maximum(jnp.max(sc, axis=-1, keepdims=True), sink)
                p = jnp.exp(sc - m)
                den = jnp.sum(p, axis=-1, keepdims=True) + jnp.exp(sink - m)
                outs[(kh, half)] = _dot(p.astype(BF16), vd[kh][krows]) / den
        for grp in range(B_WIDTH // LANES):
            kh, g = grp // 2, grp % 2
            og = jnp.where(low, outs[(kh, 0)][g * blk:(g + 1) * blk], outs[(kh, 1)][g * blk:(g + 1) * blk])
            gs = slice(grp * LANES, (grp + 1) * LANES)
            ob_ref[0, qrows, gs] = (og * _silu(zb_ref[0, qrows, gs])).astype(BF16)


def _swa(sinks, qb, kb, kbr, vd0, vd1, zb):
    bsz, t, _ = qb.shape
    tq = SWA_TQ
    per = tq // WINDOW
    cur = lambda w: pl.BlockSpec((1, tq, w), lambda b, i: (b, i, 0))
    prev = pl.BlockSpec((1, WINDOW, LANES), lambda b, i: (b, jnp.maximum(i * per - 1, 0), 0))
    return pl.pallas_call(
        _swa_kernel,
        grid=(bsz, t // tq),
        in_specs=[pl.BlockSpec(memory_space=pltpu.SMEM), cur(B_WIDTH),
                  cur(LANES), prev, cur(LANES), prev, cur(LANES), prev, cur(LANES), prev,
                  cur(B_WIDTH)],
        out_specs=cur(B_WIDTH),
        out_shape=jax.ShapeDtypeStruct((bsz, t, B_WIDTH), BF16),
        compiler_params=pltpu.CompilerParams(dimension_semantics=("arbitrary", "arbitrary"),
                                             vmem_limit_bytes=VMEM_LIMIT),
        name="swa",
    )(sinks, qb, kb, kb, kbr, kbr, vd0, vd0, vd1, vd1, zb)


def _out_kernel(oa_ref, ob_ref, x_ref, gate_ref, w_ref, g_ref, b_ref, y_ref):
    mix = _dot(oa_ref[0], w_ref[0:A_WIDTH, :]) + _dot(ob_ref[0], w_ref[A_WIDTH:MIX_WIDTH, :])
    r = DEEPNORM_ALPHA * x_ref[0] + (1.0 + gate_ref[0]) * mix
    y_ref[0] = _layer_norm(r, g_ref[...], b_ref[...])


def _out(oa, ob, x, gate, w_out, ln_g, ln_b, tm):
    bsz, t, _ = x.shape
    gr = gate.shape[1]
    row = lambda w: pl.BlockSpec((1, tm, w), lambda b, i: (b, i, 0))
    gate_spec = (pl.BlockSpec((1, 1, D_MODEL), lambda b, i: (b, 0, 0)) if gr == 1
                 else row(D_MODEL))
    const2 = lambda s: pl.BlockSpec(s, lambda b, i: (0, 0))
    return pl.pallas_call(
        _out_kernel,
        grid=(bsz, t // tm),
        in_specs=[row(A_WIDTH), row(B_WIDTH), row(D_MODEL), gate_spec,
                  const2((MIX_WIDTH, D_MODEL)), const2((1, D_MODEL)), const2((1, D_MODEL))],
        out_specs=row(D_MODEL),
        out_shape=jax.ShapeDtypeStruct((bsz, t, D_MODEL), F32),
        compiler_params=pltpu.CompilerParams(dimension_semantics=("arbitrary", "arbitrary"),
                                             vmem_limit_bytes=VMEM_LIMIT),
        name="out",
    )(oa, ob, x, gate, w_out, ln_g, ln_b)


def _sproj_kernel(x_ref, mod_ref, w_ref, cw_ref, cst_ref, alog_ref, dt_ref, cos_ref, sin_ref,
                  q_ref, k_ref, v_ref, za_ref, gb_ref, qb_ref, kb_ref, vb_ref, zb_ref, ncs_ref):
    shift = mod_ref[:, 0:D_MODEL]
    scale = mod_ref[:, D_MODEL:2 * D_MODEL]
    h = (x_ref[...] * (1.0 + scale) + shift).astype(BF16)

    for gi, o_ref in enumerate((q_ref, k_ref, v_ref)):
        c0 = gi * A_WIDTH
        cs = slice(c0, c0 + A_WIDTH)
        u = _dot(h, w_ref[:, cs])
        acc = cst_ref[0, :, cs] * cw_ref[0:1, cs]
        acc = acc + cst_ref[1, :, cs] * cw_ref[1:2, cs]
        acc = acc + cst_ref[2, :, cs] * cw_ref[2:3, cs]
        acc = acc + u * cw_ref[3:4, cs]
        y = _silu(acc)
        if gi == 0:
            y = _l2norm_heads(y, A_DK ** -0.5)
        elif gi == 1:
            y = _l2norm_heads(y, 1.0)
        o_ref[...] = y
        ncs_ref[0, :, cs] = cst_ref[1, :, cs]
        ncs_ref[1, :, cs] = cst_ref[2, :, cs]
        ncs_ref[2, :, cs] = u

    za_ref[...] = _dot(h, w_ref[:, C_ZA:C_ZA + A_WIDTH])
    gb_ref[...] = _gate_lanes(_dot(h, w_ref[:, C_BD:C_BD + LANES]), alog_ref[...], dt_ref[...])

    cos = cos_ref[...]
    sin = sin_ref[...]
    uq = _dot(h, w_ref[:, C_QB:C_QB + B_WIDTH])
    for g in range(B_WIDTH // LANES):
        qb_ref[:, g * LANES:(g + 1) * LANES] = (
            _rotary_group(uq[:, g * LANES:(g + 1) * LANES], cos, sin) * (B_HD ** -0.5))
    kb_ref[...] = _rotary_group(_dot(h, w_ref[:, C_KB:C_KB + LANES]), cos, sin)
    vb_ref[...] = _dot(h, w_ref[:, C_VB:C_VB + LANES])
    zb_ref[...] = _dot(h, w_ref[:, C_ZB:C_ZB + B_WIDTH])


def _sproj(x, mod_s, w_r, conv_w, cst, alog_row, dt_row, cos_row, sin_row):
    n = x.shape[0]
    full = lambda s: pl.BlockSpec(s, lambda i: (0,) * len(s))
    wide = lambda w: jax.ShapeDtypeStruct((n, w), F32)
    return pl.pallas_call(
        _sproj_kernel,
        grid=(1,),
        in_specs=[full((n, D_MODEL)), full((n, 3 * D_MODEL)), full((D_MODEL, W_COLS)),
                  full((CONV_W, A_QKV)), full((CONV_W - 1, n, A_QKV)),
                  full((1, LANES)), full((1, LANES)), full((1, LANES)), full((1, LANES))],
        out_specs=[full((n, A_WIDTH)), full((n, A_WIDTH)), full((n, A_WIDTH)), full((n, A_WIDTH)),
                   full((n, LANES)), full((n, B_WIDTH)), full((n, LANES)), full((n, LANES)),
                   full((n, B_WIDTH)), full((CONV_W - 1, n, A_QKV))],
        out_shape=[wide(A_WIDTH), wide(A_WIDTH), wide(A_WIDTH), wide(A_WIDTH), wide(LANES),
                   wide(B_WIDTH), wide(LANES), wide(LANES), wide(B_WIDTH),
                   jax.ShapeDtypeStruct((CONV_W - 1, n, A_QKV), F32)],
        compiler_params=pltpu.CompilerParams(dimension_semantics=("arbitrary",),
                                             vmem_limit_bytes=VMEM_LIMIT),
        name="sproj",
    )(x, mod_s, w_r, conv_w, cst, alog_row, dt_row, cos_row, sin_row)


def _sstep_kernel(sink_ref, q_ref, k_ref, v_ref, gb_ref, za_ref, na_ref, st_ref,
                  qb_ref, kn_ref, vn_ref, zb_ref, ck_ref, cv_ref,
                  oa_ref, ob_ref, nst_ref, nck_ref, ncv_ref,
                  o_scr, ob_scr):
    bt = q_ref.shape[0]
    gbv = gb_ref[...]

    for h in range(A_HEADS):
        hs = slice(h * A_DK, (h + 1) * A_DK)
        q_t = q_ref[:, hs].T
        k_t = k_ref[:, hs].T
        for bb in range(bt):
            eg = jnp.exp(gbv[bb:bb + 1, A_HEADS + h:A_HEADS + h + 1])
            beta = gbv[bb:bb + 1, h:h + 1]
            kcol = k_t[:, bb:bb + 1]
            qcol = q_t[:, bb:bb + 1]
            s1 = eg * st_ref[bb, h]
            pred = jnp.sum(kcol * s1, axis=0, keepdims=True)
            upd = beta * (v_ref[bb:bb + 1, hs] - pred)
            s2 = s1 + kcol * upd
            nst_ref[bb, h] = s2
            o_scr[bb:bb + 1, hs] = jnp.sum(qcol * s2, axis=0, keepdims=True)
    na = na_ref[...]
    for h in range(A_HEADS):
        hs = slice(h * A_DK, (h + 1) * A_DK)
        o = o_scr[:, hs]
        on = o * lax.rsqrt(jnp.mean(o * o, axis=-1, keepdims=True) + RMS_EPS) * na
        oa_ref[:, hs] = (on * _silu(za_ref[:, hs])).astype(BF16)

    row8 = lax.broadcasted_iota(jnp.int32, (B_HEADS, LANES), 0)
    lane8 = _lane((B_HEADS, LANES))
    own_half = (lane8 >= B_HD) == (row8 >= B_GROUP)
    rcol = lax.broadcasted_iota(jnp.int32, (B_HEADS, 1), 0)
    sink = jnp.zeros((B_HEADS, 1), F32)
    for r in range(B_HEADS):
        sink = jnp.where(rcol == r, sink_ref[r], sink)
    qv = qb_ref[...]
    qv_r = jnp.concatenate([pltpu.roll(qv[:, g * LANES:(g + 1) * LANES], B_HD, axis=1)
                            for g in range(B_WIDTH // LANES)], axis=-1)
    for bb in range(bt):
        qz = jnp.zeros((B_HEADS, LANES), F32)
        for r in range(B_HEADS):
            grp, half, kh = r // 2, r % 2, r // B_GROUP
            src = qv if half == kh else qv_r
            qz = jnp.where(row8 == r, src[bb:bb + 1, grp * LANES:(grp + 1) * LANES], qz)
        qz = jnp.where(own_half, qz, 0.0)
        kc = ck_ref[bb]
        vc = cv_ref[bb]
        knew = kn_ref[bb:bb + 1, :]
        vnew = vn_ref[bb:bb + 1, :]
        sc = _dot_nt(qz, kc)
        sc_new = jnp.sum(qz * knew, axis=-1, keepdims=True)
        m = jnp.maximum(jnp.maximum(jnp.max(sc, axis=-1, keepdims=True), sc_new), sink)
        p = jnp.exp(sc - m)
        p_new = jnp.exp(sc_new - m)
        den = jnp.sum(p, axis=-1, keepdims=True) + p_new + jnp.exp(sink - m)
        o = (_dot(p, vc) + p_new * vnew) / den
        o = jnp.where(own_half, o, 0.0)
        ob_scr[bb * B_HEADS:(bb + 1) * B_HEADS, :] = o + pltpu.roll(o, B_HD, axis=1)
        nck_ref[bb, 0:WINDOW - 1, :] = ck_ref[bb, 1:WINDOW, :]
        nck_ref[bb, WINDOW - 1:WINDOW, :] = knew
        ncv_ref[bb, 0:WINDOW - 1, :] = cv_ref[bb, 1:WINDOW, :]
        ncv_ref[bb, WINDOW - 1:WINDOW, :] = vnew
    low = _lane((bt, LANES)) < B_HD
    for grp in range(B_WIDTH // LANES):
        even = ob_scr[pl.ds(2 * grp, bt, stride=B_HEADS), :]
        odd = ob_scr[pl.ds(2 * grp + 1, bt, stride=B_HEADS), :]
        gs = slice(grp * LANES, (grp + 1) * LANES)
        ob_ref[:, gs] = (jnp.where(low, even, odd) * _silu(zb_ref[:, gs])).astype(BF16)


def _sstep(sinks, q, k, v, gb, za, na_row, state, qb, kn, vn, zb, ck, cv):
    n = q.shape[0]
    bt = STEP_BT
    row = lambda w: pl.BlockSpec((bt, w), lambda i: (i, 0))
    st_spec = pl.BlockSpec((bt, A_HEADS, A_DK, A_DV), lambda i: (i, 0, 0, 0))
    c_spec = pl.BlockSpec((bt, WINDOW, LANES), lambda i: (i, 0, 0))
    return pl.pallas_call(
        _sstep_kernel,
        grid=(n // bt,),
        in_specs=[pl.BlockSpec(memory_space=pltpu.SMEM),
                  row(A_WIDTH), row(A_WIDTH), row(A_WIDTH), row(LANES), row(A_WIDTH),
                  pl.BlockSpec((1, A_DV), lambda i: (0, 0)), st_spec,
                  row(B_WIDTH), row(LANES), row(LANES), row(B_WIDTH), c_spec, c_spec],
        out_specs=[row(A_WIDTH), row(B_WIDTH), st_spec, c_spec, c_spec],
        out_shape=[jax.ShapeDtypeStruct((n, A_WIDTH), BF16),
                   jax.ShapeDtypeStruct((n, B_WIDTH), BF16),
                   jax.ShapeDtypeStruct((n, A_HEADS, A_DK, A_DV), F32),
                   jax.ShapeDtypeStruct((n, WINDOW, LANES), F32),
                   jax.ShapeDtypeStruct((n, WINDOW, LANES), F32)],
        scratch_shapes=[pltpu.VMEM((bt, A_WIDTH), F32), pltpu.VMEM((bt * B_HEADS, LANES), F32)],
        compiler_params=pltpu.CompilerParams(dimension_semantics=("arbitrary",),
                                             vmem_limit_bytes=VMEM_LIMIT),
        name="sstep",
    )(sinks, q, k, v, gb, za, na_row, state, qb, kn, vn, zb, ck, cv)


def _rope_tables(pos):
    half = B_HD // 2
    inv = 1.0 / (ROPE_THETA ** (jnp.arange(half, dtype=F32) / half))
    ang = pos.astype(F32)[:, None] * inv[None, :]
    cos = jnp.cos(ang)
    sin = jnp.sin(ang)
    reps = LANES // B_HD
    return (jnp.tile(jnp.concatenate([cos, cos], -1), (1, reps)),
            jnp.tile(jnp.concatenate([-sin, sin], -1), (1, reps)))


def _pad_row(vec, offset):
    return jnp.zeros((1, LANES), F32).at[0, offset:offset + vec.shape[0]].set(vec.astype(F32))


def _layer(x_prompt, x_sample, state_conv, state_delta, cache_k, cache_v, c_prompt, c_sample,
           w_ada, b_ada, w_in, conv_w, a_log, dt_bias, norm_a, sinks, w_out, ln_g, ln_b):
    bsz, seq, _ = x_prompt.shape
    n_s = x_sample.shape[0]

    w_r = jnp.concatenate([
        w_in[:, 0:OFF_A_BETA], w_in[:, OFF_B_Q:PROJ_COLS], w_in[:, OFF_A_BETA:OFF_B_Q],
        jnp.zeros((D_MODEL, LANES - 2 * A_HEADS), w_in.dtype)], axis=1).astype(BF16)
    w_o = w_out.astype(BF16)
    alog_row = _pad_row(a_log, A_HEADS)
    dt_row = _pad_row(dt_bias, A_HEADS)
    na_row = norm_a.reshape(1, A_DV)
    g_row = ln_g.reshape(1, D_MODEL)
    b_row = ln_b.reshape(1, D_MODEL)

    c_all = jnp.concatenate([c_sample, c_prompt, jnp.zeros((8 - bsz, D_MODEL), F32)], axis=0)
    mod = _ada(c_all, w_ada, b_ada.reshape(1, 3 * D_MODEL))
    mod_s = mod[0:n_s]
    mod_p = mod[n_s:n_s + bsz].reshape(bsz, 1, 3 * D_MODEL)

    cos_p, sin_p = _rope_tables(jnp.arange(seq))
    (q, k, v, za, gb, qb, kb, kbr, vd0, vd1, zb, conv_p, kb_last, vb_last) = _proj(
        x_prompt, mod_p, w_r, conv_w, alog_row, dt_row, cos_p, sin_p)
    oa, delta_p = _delta(q, k, v, gb, za, na_row)
    ob = _swa(sinks, qb, kb, kbr, vd0, vd1, zb)
    y_p = _out(oa, ob, x_prompt, mod_p[:, :, 2 * D_MODEL:], w_o, g_row, b_row, OUT_TM)
    swa_k_p = kb_last.reshape(bsz, WINDOW, B_KV_HEADS, B_HD)
    swa_v_p = vb_last.reshape(bsz, WINDOW, B_KV_HEADS, B_HD)

    cos_s, sin_s = _rope_tables(PAST_LEN + jnp.arange(1))
    xs = x_sample.reshape(n_s, D_MODEL)
    cst = jnp.transpose(state_conv, (1, 0, 2))
    sq, sk, sv, sza, sgb, sqb, skn, svn, szb, ncs = _sproj(xs, mod_s, w_r, conv_w, cst, alog_row, dt_row,
                                                           cos_s, sin_s)
    soa, sob, delta_s, nck, ncv = _sstep(sinks, sq, sk, sv, sgb, sza, na_row, state_delta,
                                         sqb, skn, svn, szb,
                                         cache_k.reshape(n_s, WINDOW, LANES),
                                         cache_v.reshape(n_s, WINDOW, LANES))
    y_s = _out(soa[None], sob[None], xs[None], mod_s[None, :, 2 * D_MODEL:], w_o, g_row, b_row, n_s)
    conv_s = jnp.transpose(ncs, (1, 0, 2))
    return (y_p, y_s.reshape(n_s, 1, D_MODEL), conv_p, delta_p, swa_k_p, swa_v_p,
            conv_s, delta_s,
            nck.reshape(n_s, WINDOW, B_KV_HEADS, B_HD), ncv.reshape(n_s, WINDOW, B_KV_HEADS, B_HD))


def kernel(x_prompt, x_sample, state_conv, state_delta, cache_swa_k, cache_swa_v, c_prompt, c_sample,
           w_ada, b_ada, w_in, conv_w, a_log, dt_bias, norm_a, sinks, w_out, ln_g, ln_b):
    assert w_ada.shape[0] == DEPTH == 1
    outs = _layer(x_prompt, x_sample, state_conv[0], state_delta[0], cache_swa_k[0], cache_swa_v[0],
                  c_prompt, c_sample, w_ada[0], b_ada[0], w_in[0], conv_w[0], a_log[0], dt_bias[0],
                  norm_a[0], sinks[0], w_out[0], ln_g[0], ln_b[0])
    y_p, y_s = outs[0], outs[1]
    return (y_p, y_s) + tuple(o[None] for o in outs[2:])
```

```python
import jax
import jax.numpy as jnp
from jax import lax
from jax.experimental import pallas as pl
from jax.experimental.pallas import tpu as pltpu

F32 = jnp.float32
BF16 = jnp.bfloat16

D_MODEL = 1024
DEPTH = 1
PAST_LEN = 8192
A_HEADS = 4
A_DK = 128
A_DV = 128
A_WIDTH = A_HEADS * A_DV
A_QKV = 3 * A_WIDTH
CONV_W = 4
CHUNK = 64
B_HEADS = 8
B_KV_HEADS = 2
B_HD = 64
B_GROUP = B_HEADS // B_KV_HEADS
B_WIDTH = B_HEADS * B_HD
B_KV_WIDTH = B_KV_HEADS * B_HD
WINDOW = 128
ROPE_THETA = 10000.0
MIX_WIDTH = A_WIDTH + B_WIDTH
DEEPNORM_ALPHA = (2 * DEPTH) ** 0.25
LN_EPS = 1e-5
RMS_EPS = 1e-6
L2_EPS = 1e-6

OFF_A_Z = A_QKV
OFF_A_BETA = OFF_A_Z + A_WIDTH
OFF_A_DECAY = OFF_A_BETA + A_HEADS
OFF_B_Q = OFF_A_DECAY + A_HEADS
OFF_B_K = OFF_B_Q + B_WIDTH
OFF_B_V = OFF_B_K + B_KV_WIDTH
OFF_B_Z = OFF_B_V + B_KV_WIDTH
PROJ_COLS = OFF_B_Z + B_WIDTH

LANES = 128
C_QKV = 0
C_ZA = C_QKV + A_QKV
C_QB = C_ZA + A_WIDTH
C_KB = C_QB + B_WIDTH
C_VB = C_KB + B_KV_WIDTH
C_ZB = C_VB + B_KV_WIDTH
C_BD = C_ZB + B_WIDTH
W_COLS = C_BD + LANES

VMEM_LIMIT = 56 * 1024 * 1024

PROJ_TM = 512
DELTA_CT = 256
SWA_TQ = 512
OUT_TM = 512
STEP_BT = 16


def _dot(a, b):
    return jnp.dot(a, b, preferred_element_type=F32)


def _dot_nt(a, b):
    return lax.dot_general(a, b, (((1,), (1,)), ((), ())), preferred_element_type=F32)


def _silu(x):
    return x * jax.nn.sigmoid(x)


def _softplus(x):
    return jnp.maximum(x, 0.0) + jnp.log1p(jnp.exp(-jnp.abs(x)))


def _lane(shape):
    return lax.broadcasted_iota(jnp.int32, shape, len(shape) - 1)


def _l2norm_heads(y, scale):
    outs = []
    for h in range(A_HEADS):
        xh = y[:, h * A_DK:(h + 1) * A_DK]
        ss = jnp.sum(xh * xh, axis=-1, keepdims=True)
        xn = xh * lax.rsqrt(ss + L2_EPS)
        outs.append(xn * scale if scale != 1.0 else xn)
    return jnp.concatenate(outs, axis=-1)


def _rotary_group(xg, cos, sin_signed):
    lane = _lane(xg.shape)
    swapped = jnp.where((lane % B_HD) < (B_HD // 2),
                        pltpu.roll(xg, LANES - B_HD // 2, axis=1),
                        pltpu.roll(xg, B_HD // 2, axis=1))
    return xg * cos + swapped * sin_signed


def _kv_layouts(kb, vb):
    low = _lane(kb.shape) < B_HD
    kbr = pltpu.roll(kb, B_HD, axis=1)
    vbr = pltpu.roll(vb, B_HD, axis=1)
    return kb, kbr, jnp.where(low, vb, vbr), jnp.where(low, vbr, vb)


def _gate_lanes(bd, alog_row, dt_row):
    lane = _lane(bd.shape)
    g = -jnp.exp(alog_row) * _softplus(bd + dt_row)
    return jnp.where(lane < A_HEADS, jax.nn.sigmoid(bd), g)


def _layer_norm(r, g, b):
    mu = jnp.mean(r, axis=-1, keepdims=True)
    d = r - mu
    var = jnp.mean(d * d, axis=-1, keepdims=True)
    return d * lax.rsqrt(var + LN_EPS) * g + b


def _ada_kernel(c_ref, w_ref, b_ref, o_ref):
    o_ref[...] = _dot(c_ref[...].astype(BF16), w_ref[...].astype(BF16)) + b_ref[...]


def _ada(c_all, w_ada, b_ada):
    rows = c_all.shape[0]
    tn = 768
    return pl.pallas_call(
        _ada_kernel,
        grid=(3 * D_MODEL // tn,),
        in_specs=[pl.BlockSpec((rows, D_MODEL), lambda j: (0, 0)),
                  pl.BlockSpec((D_MODEL, tn), lambda j: (0, j)),
                  pl.BlockSpec((1, tn), lambda j: (0, j))],
        out_specs=pl.BlockSpec((rows, tn), lambda j: (0, j)),
        out_shape=jax.ShapeDtypeStruct((rows, 3 * D_MODEL), F32),
        compiler_params=pltpu.CompilerParams(dimension_semantics=("arbitrary",),
                                             vmem_limit_bytes=VMEM_LIMIT),
        name="ada",
    )(c_all, w_ada, b_ada)


def _proj_kernel(x_ref, mod_ref, w_ref, cw_ref, alog_ref, dt_ref, cos_ref, sin_ref,
                 q_ref, k_ref, v_ref, za_ref, gb_ref, qb_ref, kb_ref, kbr_ref, vd0_ref, vd1_ref,
                 zb_ref, cst_ref, kbl_ref, vbl_ref, tail):
    tm = x_ref.shape[1]
    t = pl.program_id(1)

    @pl.when(t == 0)
    def _():
        tail[...] = jnp.zeros(tail.shape, F32)

    shift = mod_ref[0, :, 0:D_MODEL]
    scale = mod_ref[0, :, D_MODEL:2 * D_MODEL]
    h = (x_ref[0] * (1.0 + scale) + shift).astype(BF16)

    row8 = lax.broadcasted_iota(jnp.int32, (8, A_WIDTH), 0)
    for gi, o_ref in enumerate((q_ref, k_ref, v_ref)):
        cs = slice(gi * A_WIDTH, (gi + 1) * A_WIDTH)
        u = _dot(h, w_ref[:, cs])
        prev = tail[:, cs]
        acc = None
        head = None
        for j in range(CONV_W - 1, 0, -1):
            wj = cw_ref[CONV_W - 1 - j:CONV_W - j, cs]
            rolled = pltpu.roll(u, j, axis=0)
            first = jnp.where(row8 < j, pltpu.roll(prev, j, axis=0), rolled[0:8]) * wj
            acc = rolled * wj if acc is None else acc + rolled * wj
            head = first if head is None else head + first
        w_last = cw_ref[CONV_W - 1:CONV_W, cs]
        acc = acc + u * w_last
        head = head + u[0:8] * w_last
        y = _silu(jnp.concatenate([head, acc[8:]], axis=0))
        if gi == 0:
            y = _l2norm_heads(y, A_DK ** -0.5)
        elif gi == 1:
            y = _l2norm_heads(y, 1.0)
        o_ref[0] = y
        tail[:, cs] = u[tm - 8:tm]
        cst_ref[0, :, cs] = u[tm - (CONV_W - 1):tm]

    za_ref[0] = _dot(h, w_ref[:, C_ZA:C_ZA + A_WIDTH])
    gb_ref[0] = _gate_lanes(_dot(h, w_ref[:, C_BD:C_BD + LANES]), alog_ref[...], dt_ref[...])

    cos = cos_ref[...]
    sin = sin_ref[...]
    uq = _dot(h, w_ref[:, C_QB:C_QB + B_WIDTH])
    for g in range(B_WIDTH // LANES):
        qb_ref[0, :, g * LANES:(g + 1) * LANES] = (
            _rotary_group(uq[:, g * LANES:(g + 1) * LANES], cos, sin) * (B_HD ** -0.5)).astype(BF16)
    kb = _rotary_group(_dot(h, w_ref[:, C_KB:C_KB + LANES]), cos, sin)
    vb = _dot(h, w_ref[:, C_VB:C_VB + LANES])
    for o_ref, val in zip((kb_ref, kbr_ref, vd0_ref, vd1_ref), _kv_layouts(kb, vb)):
        o_ref[0] = val.astype(BF16)
    zb_ref[0] = _dot(h, w_ref[:, C_ZB:C_ZB + B_WIDTH])

    @pl.when(t == pl.num_programs(1) - 1)
    def _():
        kbl_ref[0] = kb[tm - WINDOW:tm]
        vbl_ref[0] = vb[tm - WINDOW:tm]


def _proj(x, mod_p, w_r, conv_w, alog_row, dt_row, cos_t, sin_t):
    bsz, t, _ = x.shape
    tm = PROJ_TM
    row = lambda w: pl.BlockSpec((1, tm, w), lambda b, i: (b, i, 0))
    const2 = lambda s: pl.BlockSpec(s, lambda b, i: (0, 0))
    per_b = lambda r, w: pl.BlockSpec((1, r, w), lambda b, i: (b, 0, 0))
    wide = lambda w, dt=F32: jax.ShapeDtypeStruct((bsz, t, w), dt)
    return pl.pallas_call(
        _proj_kernel,
        grid=(bsz, t // tm),
        in_specs=[row(D_MODEL),
                  per_b(1, 3 * D_MODEL),
                  const2((D_MODEL, W_COLS)),
                  const2((CONV_W, A_QKV)),
                  const2((1, LANES)), const2((1, LANES)),
                  pl.BlockSpec((tm, LANES), lambda b, i: (i, 0)),
                  pl.BlockSpec((tm, LANES), lambda b, i: (i, 0))],
        out_specs=[row(A_WIDTH), row(A_WIDTH), row(A_WIDTH), row(A_WIDTH), row(LANES),
                   row(B_WIDTH), row(LANES), row(LANES), row(LANES), row(LANES), row(B_WIDTH),
                   per_b(CONV_W - 1, A_QKV), per_b(WINDOW, LANES), per_b(WINDOW, LANES)],
        out_shape=[wide(A_WIDTH), wide(A_WIDTH), wide(A_WIDTH), wide(A_WIDTH), wide(LANES),
                   wide(B_WIDTH, BF16), wide(LANES, BF16), wide(LANES, BF16), wide(LANES, BF16),
                   wide(LANES, BF16), wide(B_WIDTH),
                   jax.ShapeDtypeStruct((bsz, CONV_W - 1, A_QKV), F32),
                   jax.ShapeDtypeStruct((bsz, WINDOW, LANES), F32),
                   jax.ShapeDtypeStruct((bsz, WINDOW, LANES), F32)],
        scratch_shapes=[pltpu.VMEM((8, A_QKV), F32)],
        compiler_params=pltpu.CompilerParams(dimension_semantics=("arbitrary", "arbitrary"),
                                             vmem_limit_bytes=VMEM_LIMIT),
        name="proj",
    )(x, mod_p, w_r, conv_w, alog_row, dt_row, cos_t, sin_t)


def _delta_kernel(q_ref, k_ref, v_ref, gb_ref, za_ref, na_ref, oa_ref, st_ref,
                  s_scr, wq_s, ut_s, akd_s, gl_s):
    bsz, ct = q_ref.shape[0], q_ref.shape[1]
    nch = ct // CHUNK
    t = pl.program_id(0)
    wslot = t % 2
    rslot = 1 - wslot

    @pl.when(t == 0)
    def _():
        s_scr[...] = jnp.zeros(s_scr.shape, F32)
        wq_s[...] = jnp.zeros(wq_s.shape, BF16)
        ut_s[...] = jnp.zeros(ut_s.shape, F32)
        akd_s[...] = jnp.zeros(akd_s.shape, BF16)
        gl_s[...] = jnp.zeros(gl_s.shape, F32)

    units = [(b, c, h) for b in range(bsz) for c in range(nch) for h in range(A_HEADS)]
    uid = {u_: i for i, u_ in enumerate(units)}
    rows = lambda c: slice(c * CHUNK, (c + 1) * CHUNK)
    lanes = lambda h: slice(h * A_DK, (h + 1) * A_DK)
    ti = lax.broadcasted_iota(jnp.int32, (CHUNK, CHUNK), 0)
    ii = lax.broadcasted_iota(jnp.int32, (CHUNK, CHUNK), 1)
    na = na_ref[...]

    s_cur = {(b, h): s_scr[b * A_HEADS + h] for b in range(bsz) for h in range(A_HEADS)}
    ws, uu = {}, {}

    def rec_ws(c):
        for b in range(bsz):
            for h in range(A_HEADS):
                i = uid[b, c, h]
                ws[b, h] = _dot(wq_s[rslot, i], s_cur[b, h].astype(BF16))
                uu[b, h] = (ut_s[rslot, i] - ws[b, h][:CHUNK]).astype(BF16)

    def rec_ou(c):
        for b in range(bsz):
            for h in range(A_HEADS):
                i = uid[b, c, h]
                ou = _dot(akd_s[rslot, i], uu[b, h])
                o = ws[b, h][CHUNK:] + ou[:CHUNK]
                s_cur[b, h] = gl_s[rslot, i] * s_cur[b, h] + ou[CHUNK:]
                on = o * lax.rsqrt(jnp.mean(o * o, axis=-1, keepdims=True) + RMS_EPS) * na
                oa_ref[b, rows(c), lanes(h)] = (on * _silu(za_ref[b, rows(c), lanes(h)])).astype(BF16)

    rec_stages = []
    for c in range(nch):
        rec_stages += [lambda c=c: rec_ws(c), lambda c=c: rec_ou(c)]

    def run_rec(n_left_after):
        while rec_stages and len(rec_stages) > n_left_after:
            rec_stages.pop(0)()

    beta, g_col, g_last, eg, dec_incl, nmat, aqk = {}, {}, {}, {}, {}, {}, {}
    for b in range(bsz):
        gbv = gb_ref[b]
        rin = lax.broadcasted_iota(jnp.int32, gbv.shape, 0) % CHUNK
        gcs = gbv
        s = 1
        while s < CHUNK:
            gcs = gcs + jnp.where(rin >= s, pltpu.roll(gcs, s, axis=0), 0.0)
            s *= 2
        gcs_t = gcs.T
        for c in range(nch):
            for h in range(A_HEADS):
                u_ = (b, c, h)
                r0 = c * CHUNK
                beta[u_] = jnp.broadcast_to(gbv[rows(c), h:h + 1], (CHUNK, A_DK))
                g_col[u_] = jnp.broadcast_to(gcs[rows(c), A_HEADS + h:A_HEADS + h + 1], (CHUNK, A_DK))
                g_last[u_] = gcs[r0 + CHUNK - 1:r0 + CHUNK, A_HEADS + h:A_HEADS + h + 1]
                g_row = gcs_t[A_HEADS + h:A_HEADS + h + 1, rows(c)]
                dec_incl[u_] = jnp.exp(jnp.where(ti >= ii, g_col[u_][:, :CHUNK] - g_row, -jnp.inf))
                eg[u_] = jnp.exp(g_col[u_])

    n_prep = 8
    for (b, c, h) in units:
        kcb = k_ref[b, rows(c), lanes(h)].astype(BF16)
        qcb = q_ref[b, rows(c), lanes(h)].astype(BF16)
        kq = _dot_nt(jnp.concatenate([kcb, qcb], axis=0), kcb)
        nmat[b, c, h] = -(beta[b, c, h][:, :CHUNK] * kq[:CHUNK] * jnp.where(ti > ii, dec_incl[b, c, h], 0.0))
        aqk[b, c, h] = kq[CHUNK:] * dec_incl[b, c, h]
    run_rec(n_prep - 1)

    rsum = dict(nmat)
    pwb = {u_: nmat[u_].astype(BF16) for u_ in units}
    pw = {u_: _dot(pwb[u_], pwb[u_]) for u_ in units}
    run_rec(n_prep - 2)
    for step in range(1, 6):
        last = step == 5
        pwb = {u_: pw[u_].astype(BF16) for u_ in units}
        rp = {}
        for u_ in units:
            rb = rsum[u_].astype(BF16)
            rp[u_] = _dot(rb if last else jnp.concatenate([rb, pwb[u_]], axis=0), pwb[u_])
        for u_ in units:
            rsum[u_] = rsum[u_] + pw[u_] + rp[u_][:CHUNK]
            if not last:
                pw[u_] = rp[u_][CHUNK:]
        run_rec(n_prep - 2 - step)

    for (b, c, h) in units:
        u_ = (b, c, h)
        i = uid[u_]
        kc = k_ref[b, rows(c), lanes(h)]
        rhs = jnp.concatenate([(beta[u_] * eg[u_]) * kc, beta[u_] * v_ref[b, rows(c), lanes(h)]],
                              axis=-1)
        sol = rhs + _dot(rsum[u_].astype(BF16), rhs.astype(BF16))
        wq_s[wslot, i] = jnp.concatenate([sol[:, :A_DK], eg[u_] * q_ref[b, rows(c), lanes(h)]],
                                         axis=0).astype(BF16)
        ut_s[wslot, i] = sol[:, A_DK:]
        kd = jnp.exp(g_last[u_] - g_col[u_]) * kc
        akd_s[wslot, i] = jnp.concatenate([aqk[u_], kd.T], axis=0).astype(BF16)
        gl_s[wslot, i] = jnp.broadcast_to(jnp.exp(g_last[u_]), (1, A_DV))
    run_rec(0)

    for b in range(bsz):
        for h in range(A_HEADS):
            s_scr[b * A_HEADS + h] = s_cur[b, h]

    @pl.when(t == pl.num_programs(0) - 1)
    def _():
        for b in range(bsz):
            for h in range(A_HEADS):
                st_ref[b, h] = s_cur[b, h]


def _delta(q, k, v, gb, za, na_row):
    bsz, t, _ = q.shape
    ct = DELTA_CT
    nt = t // ct
    n_units = bsz * (ct // CHUNK) * A_HEADS
    prep = lambda w: pl.BlockSpec((bsz, ct, w), lambda i: (0, jnp.minimum(i, nt - 1), 0))
    rec = lambda w: pl.BlockSpec((bsz, ct, w), lambda i: (0, jnp.maximum(i - 1, 0), 0))
    return pl.pallas_call(
        _delta_kernel,
        grid=(nt + 1,),
        in_specs=[prep(A_WIDTH), prep(A_WIDTH), prep(A_WIDTH), prep(LANES), rec(A_WIDTH),
                  pl.BlockSpec((1, A_DV), lambda i: (0, 0))],
        out_specs=[rec(A_WIDTH),
                   pl.BlockSpec((bsz, A_HEADS, A_DK, A_DV), lambda i: (0, 0, 0, 0))],
        out_shape=[jax.ShapeDtypeStruct((bsz, t, A_WIDTH), BF16),
                   jax.ShapeDtypeStruct((bsz, A_HEADS, A_DK, A_DV), F32)],
        scratch_shapes=[pltpu.VMEM((bsz * A_HEADS, A_DK, A_DV), F32),
                        pltpu.VMEM((2, n_units, 2 * CHUNK, A_DK), BF16),
                        pltpu.VMEM((2, n_units, CHUNK, A_DV), F32),
                        pltpu.VMEM((2, n_units, 3 * CHUNK, CHUNK), BF16),
                        pltpu.VMEM((2, n_units, 1, A_DV), F32)],
        compiler_params=pltpu.CompilerParams(dimension_semantics=("arbitrary",),
                                             vmem_limit_bytes=VMEM_LIMIT),
        name="delta",
    )(q, k, v, gb, za, na_row)


def _swa_kernel(sink_ref, qb_ref, kc_ref, kp_ref, krc_ref, krp_ref, v0c_ref, v0p_ref, v1c_ref, v1p_ref,
                zb_ref, ob_ref):
    n = pl.program_id(1)
    tq = qb_ref.shape[1]
    blk = WINDOW
    kx = (jnp.concatenate([kp_ref[0], kc_ref[0]], axis=0), jnp.concatenate([krp_ref[0], krc_ref[0]], axis=0))
    vd = (jnp.concatenate([v0p_ref[0], v0c_ref[0]], axis=0), jnp.concatenate([v1p_ref[0], v1c_ref[0]], axis=0))

    a = lax.broadcasted_iota(jnp.int32, (2 * blk, 2 * blk), 0) % blk
    j = lax.broadcasted_iota(jnp.int32, (2 * blk, 2 * blk), 1)
    rel = a + blk - j
    band = (rel >= 0) & (rel <= WINDOW)
    band_first = band & ((n > 0) | (j >= blk))
    top = lax.broadcasted_iota(jnp.int32, (2 * blk, 1), 0) < blk
    low = _lane((blk, LANES)) < B_HD
    zero = jnp.zeros((blk, LANES), BF16)

    for i in range(tq // blk):
        valid = band_first if i == 0 else band
        qrows = slice(i * blk, (i + 1) * blk)
        krows = slice(i * blk, (i + 2) * blk)
        outs = {}
        for kh in range(B_KV_HEADS):
            for half in range(2):
                qs = []
                for g in range(2):
                    grp = kh * 2 + g
                    xg = qb_ref[0, qrows, grp * LANES:(grp + 1) * LANES]
                    qs.append(jnp.where(low if half == 0 else jnp.logical_not(low), xg, zero))
                qz = jnp.concatenate(qs, axis=0)
                sc = jnp.where(valid, _dot_nt(qz, kx[0 if kh == half else 1][krows]), -jnp.inf)
                sink = jnp.where(top, sink_ref[kh * B_GROUP + half], sink_ref[kh * B_GROUP + half + 2])
                m = jnp.maximum(jnp.max(sc, axis=-1, keepdims=True), sink)
                p = jnp.exp(sc - m)
                den = jnp.sum(p, axis=-1, keepdims=True) + jnp.exp(sink - m)
                outs[(kh, half)] = _dot(p.astype(BF16), vd[kh][krows]) / den
        for grp in range(B_WIDTH // LANES):
            kh, g = grp // 2, grp % 2
            og = jnp.where(low, outs[(kh, 0)][g * blk:(g + 1) * blk], outs[(kh, 1)][g * blk:(g + 1) * blk])
            gs = slice(grp * LANES, (grp + 1) * LANES)
            ob_ref[0, qrows, gs] = (og * _silu(zb_ref[0, qrows, gs])).astype(BF16)


def _swa(sinks, qb, kb, kbr, vd0, vd1, zb):
    bsz, t, _ = qb.shape
    tq = SWA_TQ
    per = tq // WINDOW
    cur = lambda w: pl.BlockSpec((1, tq, w), lambda b, i: (b, i, 0))
    prev = pl.BlockSpec((1, WINDOW, LANES), lambda b, i: (b, jnp.maximum(i * per - 1, 0), 0))
    return pl.pallas_call(
        _swa_kernel,
        grid=(bsz, t // tq),
        in_specs=[pl.BlockSpec(memory_space=pltpu.SMEM), cur(B_WIDTH),
                  cur(LANES), prev, cur(LANES), prev, cur(LANES), prev, cur(LANES), prev,
                  cur(B_WIDTH)],
        out_specs=cur(B_WIDTH),
        out_shape=jax.ShapeDtypeStruct((bsz, t, B_WIDTH), BF16),
        compiler_params=pltpu.CompilerParams(dimension_semantics=("arbitrary", "arbitrary"),
                                             vmem_limit_bytes=VMEM_LIMIT),
        name="swa",
    )(sinks, qb, kb, kb, kbr, kbr, vd0, vd0, vd1, vd1, zb)


def _out_kernel(oa_ref, ob_ref, x_ref, gate_ref, w_ref, g_ref, b_ref, y_ref):
    mix = _dot(oa_ref[0], w_ref[0:A_WIDTH, :]) + _dot(ob_ref[0], w_ref[A_WIDTH:MIX_WIDTH, :])
    r = DEEPNORM_ALPHA * x_ref[0] + (1.0 + gate_ref[0]) * mix
    y_ref[0] = _layer_norm(r, g_ref[...], b_ref[...])


def _out(oa, ob, x, gate, w_out, ln_g, ln_b, tm):
    bsz, t, _ = x.shape
    gr = gate.shape[1]
    row = lambda w: pl.BlockSpec((1, tm, w), lambda b, i: (b, i, 0))
    gate_spec = (pl.BlockSpec((1, 1, D_MODEL), lambda b, i: (b, 0, 0)) if gr == 1
                 else row(D_MODEL))
    const2 = lambda s: pl.BlockSpec(s, lambda b, i: (0, 0))
    return pl.pallas_call(
        _out_kernel,
        grid=(bsz, t // tm),
        in_specs=[row(A_WIDTH), row(B_WIDTH), row(D_MODEL), gate_spec,
                  const2((MIX_WIDTH, D_MODEL)), const2((1, D_MODEL)), const2((1, D_MODEL))],
        out_specs=row(D_MODEL),
        out_shape=jax.ShapeDtypeStruct((bsz, t, D_MODEL), F32),
        compiler_params=pltpu.CompilerParams(dimension_semantics=("arbitrary", "arbitrary"),
                                             vmem_limit_bytes=VMEM_LIMIT),
        name="out",
    )(oa, ob, x, gate, w_out, ln_g, ln_b)


def _sproj_kernel(x_ref, mod_ref, w_ref, cw_ref, cst_ref, alog_ref, dt_ref, cos_ref, sin_ref,
                  q_ref, k_ref, v_ref, za_ref, gb_ref, qb_ref, kb_ref, vb_ref, zb_ref, ncs_ref):
    shift = mod_ref[:, 0:D_MODEL]
    scale = mod_ref[:, D_MODEL:2 * D_MODEL]
    h = (x_ref[...] * (1.0 + scale) + shift).astype(BF16)

    for gi, o_ref in enumerate((q_ref, k_ref, v_ref)):
        c0 = gi * A_WIDTH
        cs = slice(c0, c0 + A_WIDTH)
        u = _dot(h, w_ref[:, cs])
        acc = cst_ref[0, :, cs] * cw_ref[0:1, cs]
        acc = acc + cst_ref[1, :, cs] * cw_ref[1:2, cs]
        acc = acc + cst_ref[2, :, cs] * cw_ref[2:3, cs]
        acc = acc + u * cw_ref[3:4, cs]
        y = _silu(acc)
        if gi == 0:
            y = _l2norm_heads(y, A_DK ** -0.5)
        elif gi == 1:
            y = _l2norm_heads(y, 1.0)
        o_ref[...] = y
        ncs_ref[0, :, cs] = cst_ref[1, :, cs]
        ncs_ref[1, :, cs] = cst_ref[2, :, cs]
        ncs_ref[2, :, cs] = u

    za_ref[...] = _dot(h, w_ref[:, C_ZA:C_ZA + A_WIDTH])
    gb_ref[...] = _gate_lanes(_dot(h, w_ref[:, C_BD:C_BD + LANES]), alog_ref[...], dt_ref[...])

    cos = cos_ref[...]
    sin = sin_ref[...]
    uq = _dot(h, w_ref[:, C_QB:C_QB + B_WIDTH])
    for g in range(B_WIDTH // LANES):
        qb_ref[:, g * LANES:(g + 1) * LANES] = (
            _rotary_group(uq[:, g * LANES:(g + 1) * LANES], cos, sin) * (B_HD ** -0.5))
    kb_ref[...] = _rotary_group(_dot(h, w_ref[:, C_KB:C_KB + LANES]), cos, sin)
    vb_ref[...] = _dot(h, w_ref[:, C_VB:C_VB + LANES])
    zb_ref[...] = _dot(h, w_ref[:, C_ZB:C_ZB + B_WIDTH])


def _sproj(x, mod_s, w_r, conv_w, cst, alog_row, dt_row, cos_row, sin_row):
    n = x.shape[0]
    full = lambda s: pl.BlockSpec(s, lambda i: (0,) * len(s))
    wide = lambda w: jax.ShapeDtypeStruct((n, w), F32)
    return pl.pallas_call(
        _sproj_kernel,
        grid=(1,),
        in_specs=[full((n, D_MODEL)), full((n, 3 * D_MODEL)), full((D_MODEL, W_COLS)),
                  full((CONV_W, A_QKV)), full((CONV_W - 1, n, A_QKV)),
                  full((1, LANES)), full((1, LANES)), full((1, LANES)), full((1, LANES))],
        out_specs=[full((n, A_WIDTH)), full((n, A_WIDTH)), full((n, A_WIDTH)), full((n, A_WIDTH)),
                   full((n, LANES)), full((n, B_WIDTH)), full((n, LANES)), full((n, LANES)),
                   full((n, B_WIDTH)), full((CONV_W - 1, n, A_QKV))],
        out_shape=[wide(A_WIDTH), wide(A_WIDTH), wide(A_WIDTH), wide(A_WIDTH), wide(LANES),
                   wide(B_WIDTH), wide(LANES), wide(LANES), wide(B_WIDTH),
                   jax.ShapeDtypeStruct((CONV_W - 1, n, A_QKV), F32)],
        compiler_params=pltpu.CompilerParams(dimension_semantics=("arbitrary",),
                                             vmem_limit_bytes=VMEM_LIMIT),
        name="sproj",
    )(x, mod_s, w_r, conv_w, cst, alog_row, dt_row, cos_row, sin_row)


def _sstep_kernel(sink_ref, q_ref, k_ref, v_ref, gb_ref, za_ref, na_ref, st_ref,
                  qb_ref, kn_ref, vn_ref, zb_ref, ck_ref, cv_ref,
                  oa_ref, ob_ref, nst_ref, nck_ref, ncv_ref,
                  o_scr, ob_scr):
    bt = q_ref.shape[0]
    gbv = gb_ref[...]

    for h in range(A_HEADS):
        hs = slice(h * A_DK, (h + 1) * A_DK)
        q_t = q_ref[:, hs].T
        k_t = k_ref[:, hs].T
        for bb in range(bt):
            eg = jnp.exp(gbv[bb:bb + 1, A_HEADS + h:A_HEADS + h + 1])
            beta = gbv[bb:bb + 1, h:h + 1]
            kcol = k_t[:, bb:bb + 1]
            qcol = q_t[:, bb:bb + 1]
            s1 = eg * st_ref[bb, h]
            pred = jnp.sum(kcol * s1, axis=0, keepdims=True)
            upd = beta * (v_ref[bb:bb + 1, hs] - pred)
            s2 = s1 + kcol * upd
            nst_ref[bb, h] = s2
            o_scr[bb:bb + 1, hs] = jnp.sum(qcol * s2, axis=0, keepdims=True)
    na = na_ref[...]
    for h in range(A_HEADS):
        hs = slice(h * A_DK, (h + 1) * A_DK)
        o = o_scr[:, hs]
        on = o * lax.rsqrt(jnp.mean(o * o, axis=-1, keepdims=True) + RMS_EPS) * na
        oa_ref[:, hs] = (on * _silu(za_ref[:, hs])).astype(BF16)

    row8 = lax.broadcasted_iota(jnp.int32, (B_HEADS, LANES), 0)
    lane8 = _lane((B_HEADS, LANES))
    own_half = (lane8 >= B_HD) == (row8 >= B_GROUP)
    rcol = lax.broadcasted_iota(jnp.int32, (B_HEADS, 1), 0)
    sink = jnp.zeros((B_HEADS, 1), F32)
    for r in range(B_HEADS):
        sink = jnp.where(rcol == r, sink_ref[r], sink)
    qv = qb_ref[...]
    qv_r = jnp.concatenate([pltpu.roll(qv[:, g * LANES:(g + 1) * LANES], B_HD, axis=1)
                            for g in range(B_WIDTH // LANES)], axis=-1)
    for bb in range(bt):
        qz = jnp.zeros((B_HEADS, LANES), F32)
        for r in range(B_HEADS):
            grp, half, kh = r // 2, r % 2, r // B_GROUP
            src = qv if half == kh else qv_r
            qz = jnp.where(row8 == r, src[bb:bb + 1, grp * LANES:(grp + 1) * LANES], qz)
        qz = jnp.where(own_half, qz, 0.0)
        kc = ck_ref[bb]
        vc = cv_ref[bb]
        knew = kn_ref[bb:bb + 1, :]
        vnew = vn_ref[bb:bb + 1, :]
        sc = _dot_nt(qz, kc)
        sc_new = jnp.sum(qz * knew, axis=-1, keepdims=True)
        m = jnp.maximum(jnp.maximum(jnp.max(sc, axis=-1, keepdims=True), sc_new), sink)
        p = jnp.exp(sc - m)
        p_new = jnp.exp(sc_new - m)
        den = jnp.sum(p, axis=-1, keepdims=True) + p_new + jnp.exp(sink - m)
        o = (_dot(p, vc) + p_new * vnew) / den
        o = jnp.where(own_half, o, 0.0)
        ob_scr[bb * B_HEADS:(bb + 1) * B_HEADS, :] = o + pltpu.roll(o, B_HD, axis=1)
        nck_ref[bb, 0:WINDOW - 1, :] = ck_ref[bb, 1:WINDOW, :]
        nck_ref[bb, WINDOW - 1:WINDOW, :] = knew
        ncv_ref[bb, 0:WINDOW - 1, :] = cv_ref[bb, 1:WINDOW, :]
        ncv_ref[bb, WINDOW - 1:WINDOW, :] = vnew
    low = _lane((bt, LANES)) < B_HD
    for grp in range(B_WIDTH // LANES):
        even = ob_scr[pl.ds(2 * grp, bt, stride=B_HEADS), :]
        odd = ob_scr[pl.ds(2 * grp + 1, bt, stride=B_HEADS), :]
        gs = slice(grp * LANES, (grp + 1) * LANES)
        ob_ref[:, gs] = (jnp.where(low, even, odd) * _silu(zb_ref[:, gs])).astype(BF16)


def _sstep(sinks, q, k, v, gb, za, na_row, state, qb, kn, vn, zb, ck, cv):
    n = q.shape[0]
    bt = STEP_BT
    row = lambda w: pl.BlockSpec((bt, w), lambda i: (i, 0))
    st_spec = pl.BlockSpec((bt, A_HEADS, A_DK, A_DV), lambda i: (i, 0, 0, 0))
    c_spec = pl.BlockSpec((bt, WINDOW, LANES), lambda i: (i, 0, 0))
    return pl.pallas_call(
        _sstep_kernel,
        grid=(n // bt,),
        in_specs=[pl.BlockSpec(memory_space=pltpu.SMEM),
                  row(A_WIDTH), row(A_WIDTH), row(A_WIDTH), row(LANES), row(A_WIDTH),
                  pl.BlockSpec((1, A_DV), lambda i: (0, 0)), st_spec,
                  row(B_WIDTH), row(LANES), row(LANES), row(B_WIDTH), c_spec, c_spec],
        out_specs=[row(A_WIDTH), row(B_WIDTH), st_spec, c_spec, c_spec],
        out_shape=[jax.ShapeDtypeStruct((n, A_WIDTH), BF16),
                   jax.ShapeDtypeStruct((n, B_WIDTH), BF16),
                   jax.ShapeDtypeStruct((n, A_HEADS, A_DK, A_DV), F32),
                   jax.ShapeDtypeStruct((n, WINDOW, LANES), F32),
                   jax.ShapeDtypeStruct((n, WINDOW, LANES), F32)],
        scratch_shapes=[pltpu.VMEM((bt, A_WIDTH), F32), pltpu.VMEM((bt * B_HEADS, LANES), F32)],
        compiler_params=pltpu.CompilerParams(dimension_semantics=("arbitrary",),
                                             vmem_limit_bytes=VMEM_LIMIT),
        name="sstep",
    )(sinks, q, k, v, gb, za, na_row, state, qb, kn, vn, zb, ck, cv)


def _rope_tables(pos):
    half = B_HD // 2
    inv = 1.0 / (ROPE_THETA ** (jnp.arange(half, dtype=F32) / half))
    ang = pos.astype(F32)[:, None] * inv[None, :]
    cos = jnp.cos(ang)
    sin = jnp.sin(ang)
    reps = LANES // B_HD
    return (jnp.tile(jnp.concatenate([cos, cos], -1), (1, reps)),
            jnp.tile(jnp.concatenate([-sin, sin], -1), (1, reps)))


def _pad_row(vec, offset):
    return jnp.zeros((1, LANES), F32).at[0, offset:offset + vec.shape[0]].set(vec.astype(F32))


def _layer(x_prompt, x_sample, state_conv, state_delta, cache_k, cache_v, c_prompt, c_sample,
           w_ada, b_ada, w_in, conv_w, a_log, dt_bias, norm_a, sinks, w_out, ln_g, ln_b):
    bsz, seq, _ = x_prompt.shape
    n_s = x_sample.shape[0]

    w_r = jnp.concatenate([
        w_in[:, 0:OFF_A_BETA], w_in[:, OFF_B_Q:PROJ_COLS], w_in[:, OFF_A_BETA:OFF_B_Q],
        jnp.zeros((D_MODEL, LANES - 2 * A_HEADS), w_in.dtype)], axis=1).astype(BF16)
    w_o = w_out.astype(BF16)
    alog_row = _pad_row(a_log, A_HEADS)
    dt_row = _pad_row(dt_bias, A_HEADS)
    na_row = norm_a.reshape(1, A_DV)
    g_row = ln_g.reshape(1, D_MODEL)
    b_row = ln_b.reshape(1, D_MODEL)

    c_all = jnp.concatenate([c_sample, c_prompt, jnp.zeros((8 - bsz, D_MODEL), F32)], axis=0)
    mod = _ada(c_all, w_ada, b_ada.reshape(1, 3 * D_MODEL))
    mod_s = mod[0:n_s]
    mod_p = mod[n_s:n_s + bsz].reshape(bsz, 1, 3 * D_MODEL)

    cos_p, sin_p = _rope_tables(jnp.arange(seq))
    (q, k, v, za, gb, qb, kb, kbr, vd0, vd1, zb, conv_p, kb_last, vb_last) = _proj(
        x_prompt, mod_p, w_r, conv_w, alog_row, dt_row, cos_p, sin_p)
    oa, delta_p = _delta(q, k, v, gb, za, na_row)
    ob = _swa(sinks, qb, kb, kbr, vd0, vd1, zb)
    y_p = _out(oa, ob, x_prompt, mod_p[:, :, 2 * D_MODEL:], w_o, g_row, b_row, OUT_TM)
    swa_k_p = kb_last.reshape(bsz, WINDOW, B_KV_HEADS, B_HD)
    swa_v_p = vb_last.reshape(bsz, WINDOW, B_KV_HEADS, B_HD)

    cos_s, sin_s = _rope_tables(PAST_LEN + jnp.arange(1))
    xs = x_sample.reshape(n_s, D_MODEL)
    cst = jnp.transpose(state_conv, (1, 0, 2))
    sq, sk, sv, sza, sgb, sqb, skn, svn, szb, ncs = _sproj(xs, mod_s, w_r, conv_w, cst, alog_row, dt_row,
                                                           cos_s, sin_s)
    soa, sob, delta_s, nck, ncv = _sstep(sinks, sq, sk, sv, sgb, sza, na_row, state_delta,
                                         sqb, skn, svn, szb,
                                         cache_k.reshape(n_s, WINDOW, LANES),
                                         cache_v.reshape(n_s, WINDOW, LANES))
    y_s = _out(soa[None], sob[None], xs[None], mod_s[None, :, 2 * D_MODEL:], w_o, g_row, b_row, n_s)
    conv_s = jnp.transpose(ncs, (1, 0, 2))
    return (y_p, y_s.reshape(n_s, 1, D_MODEL), conv_p, delta_p, swa_k_p, swa_v_p,
            conv_s, delta_s,
            nck.reshape(n_s, WINDOW, B_KV_HEADS, B_HD), ncv.reshape(n_s, WINDOW, B_KV_HEADS, B_HD))


def kernel(x_prompt, x_sample, state_conv, state_delta, cache_swa_k, cache_swa_v, c_prompt, c_sample,
           w_ada, b_ada, w_in, conv_w, a_log, dt_bias, norm_a, sinks, w_out, ln_g, ln_b):
    assert w_ada.shape[0] == DEPTH == 1
    outs = _layer(x_prompt, x_sample, state_conv[0], state_delta[0], cache_swa_k[0], cache_swa_v[0],
                  c_prompt, c_sample, w_ada[0], b_ada[0], w_in[0], conv_w[0], a_log[0], dt_bias[0],
                  norm_a[0], sinks[0], w_out[0], ln_g[0], ln_b[0])
    y_p, y_s = outs[0], outs[1]
    return (y_p, y_s) + tuple(o[None] for o in outs[2:])
```

```python
import functools

import jax
import jax.numpy as jnp
import numpy as np
from jax import lax
from jax.experimental import pallas as pl
from jax.experimental.pallas import tpu as pltpu

F32 = jnp.float32
BF16 = jnp.bfloat16

D_MODEL = 1024
DEPTH = 1
PAST_LEN = 8192
A_HEADS = 4
A_DK = 128
A_DV = 128
A_WIDTH = A_HEADS * A_DV
A_QKV = 3 * A_WIDTH
CONV_W = 4
CHUNK = 64
B_HEADS = 8
B_KV_HEADS = 2
B_HD = 64
B_GROUP = B_HEADS // B_KV_HEADS
B_WIDTH = B_HEADS * B_HD
B_KV_WIDTH = B_KV_HEADS * B_HD
WINDOW = 128
ROPE_THETA = 10000.0
MIX_WIDTH = A_WIDTH + B_WIDTH
DEEPNORM_ALPHA = (2 * DEPTH) ** 0.25
LN_EPS = 1e-5
RMS_EPS = 1e-6
L2_EPS = 1e-6

OFF_A_Z = A_QKV
OFF_A_BETA = OFF_A_Z + A_WIDTH
OFF_A_DECAY = OFF_A_BETA + A_HEADS
OFF_B_Q = OFF_A_DECAY + A_HEADS
OFF_B_K = OFF_B_Q + B_WIDTH
OFF_B_V = OFF_B_K + B_KV_WIDTH
OFF_B_Z = OFF_B_V + B_KV_WIDTH
PROJ_COLS = OFF_B_Z + B_WIDTH

LANES = 128
C_QKV = 0
C_ZA = C_QKV + A_QKV
C_QB = C_ZA + A_WIDTH
C_KB = C_QB + B_WIDTH
C_VB = C_KB + B_KV_WIDTH
C_ZB = C_VB + B_KV_WIDTH
C_BD = C_ZB + B_WIDTH
W_COLS = C_BD + LANES

VMEM_LIMIT = 56 * 1024 * 1024

PROJ_TM = 512
DELTA_CT = 256
SWA_TQ = 512
OUT_TM = 512
STEP_BT = 16


def _dot(a, b):
    return jnp.dot(a, b, preferred_element_type=F32)


def _dot_nt(a, b):
    return lax.dot_general(a, b, (((1,), (1,)), ((), ())), preferred_element_type=F32)


def _silu(x):
    return x * jax.nn.sigmoid(x)


def _softplus(x):
    return jnp.maximum(x, 0.0) + jnp.log1p(jnp.exp(-jnp.abs(x)))


def _lane(shape):
    return lax.broadcasted_iota(jnp.int32, shape, len(shape) - 1)


def _l2norm_heads(y, scale):
    outs = []
    for h in range(A_HEADS):
        xh = y[:, h * A_DK:(h + 1) * A_DK]
        ss = jnp.sum(xh * xh, axis=-1, keepdims=True)
        xn = xh * lax.rsqrt(ss + L2_EPS)
        outs.append(xn * scale if scale != 1.0 else xn)
    return jnp.concatenate(outs, axis=-1)


def _rotary_group(xg, cos, sin_signed):
    lane = _lane(xg.shape)
    swapped = jnp.where((lane % B_HD) < (B_HD // 2),
                        pltpu.roll(xg, LANES - B_HD // 2, axis=1),
                        pltpu.roll(xg, B_HD // 2, axis=1))
    return xg * cos + swapped * sin_signed


def _kv_layouts(kb, vb):
    low = _lane(kb.shape) < B_HD
    kbr = pltpu.roll(kb, B_HD, axis=1)
    vbr = pltpu.roll(vb, B_HD, axis=1)
    return kb, kbr, jnp.where(low, vb, vbr), jnp.where(low, vbr, vb)


def _gate_lanes(bd, alog_row, dt_row):
    lane = _lane(bd.shape)
    g = -jnp.exp(alog_row) * _softplus(bd + dt_row)
    return jnp.where(lane < A_HEADS, jax.nn.sigmoid(bd), g)


def _layer_norm(r, g, b):
    mu = jnp.mean(r, axis=-1, keepdims=True)
    d = r - mu
    var = jnp.mean(d * d, axis=-1, keepdims=True)
    return d * lax.rsqrt(var + LN_EPS) * g + b


def _ada_kernel(c_ref, w_ref, b_ref, o_ref):
    o_ref[...] = _dot(c_ref[...].astype(BF16), w_ref[...].astype(BF16)) + b_ref[...]


def _ada(c_all, w_ada, b_ada):
    rows = c_all.shape[0]
    tn = 768
    return pl.pallas_call(
        _ada_kernel,
        grid=(3 * D_MODEL // tn,),
        in_specs=[pl.BlockSpec((rows, D_MODEL), lambda j: (0, 0)),
                  pl.BlockSpec((D_MODEL, tn), lambda j: (0, j)),
                  pl.BlockSpec((1, tn), lambda j: (0, j))],
        out_specs=pl.BlockSpec((rows, tn), lambda j: (0, j)),
        out_shape=jax.ShapeDtypeStruct((rows, 3 * D_MODEL), F32),
        compiler_params=pltpu.CompilerParams(dimension_semantics=("arbitrary",),
                                             vmem_limit_bytes=VMEM_LIMIT),
        name="ada",
    )(c_all, w_ada, b_ada)


def _proj_kernel(x_ref, mod_ref, w_ref, cw_ref, alog_ref, dt_ref, cos_ref, sin_ref,
                 q_ref, k_ref, v_ref, za_ref, gb_ref, qb_ref, kb_ref, kbr_ref, vd0_ref, vd1_ref,
                 zb_ref, cst_ref, kbl_ref, vbl_ref, tail):
    tm = x_ref.shape[1]
    t = pl.program_id(1)

    @pl.when(t == 0)
    def _():
        tail[...] = jnp.zeros(tail.shape, F32)

    brow = pl.ds(pl.program_id(0), 1)
    shift = mod_ref[brow, 0:D_MODEL]
    scale = mod_ref[brow, D_MODEL:2 * D_MODEL]
    h = (x_ref[0] * (1.0 + scale) + shift).astype(BF16)

    row8 = lax.broadcasted_iota(jnp.int32, (8, A_WIDTH), 0)
    for gi, o_ref in enumerate((q_ref, k_ref, v_ref)):
        cs = slice(gi * A_WIDTH, (gi + 1) * A_WIDTH)
        u = _dot_nt(h, w_ref[cs, :])
        prev = tail[:, cs]
        acc = None
        head = None
        for j in range(CONV_W - 1, 0, -1):
            wj = cw_ref[CONV_W - 1 - j:CONV_W - j, cs]
            rolled = pltpu.roll(u, j, axis=0)
            first = jnp.where(row8 < j, pltpu.roll(prev, j, axis=0), rolled[0:8]) * wj
            acc = rolled * wj if acc is None else acc + rolled * wj
            head = first if head is None else head + first
        w_last = cw_ref[CONV_W - 1:CONV_W, cs]
        acc = acc + u * w_last
        head = head + u[0:8] * w_last
        y = _silu(jnp.concatenate([head, acc[8:]], axis=0))
        if gi == 0:
            y = _l2norm_heads(y, A_DK ** -0.5)
        elif gi == 1:
            y = _l2norm_heads(y, 1.0)
        o_ref[0] = y
        tail[:, cs] = u[tm - 8:tm]
        cst_ref[0, :, cs] = u[tm - (CONV_W - 1):tm]

    za_ref[0] = _dot_nt(h, w_ref[C_ZA:C_ZA + A_WIDTH, :])
    gb_ref[0] = _gate_lanes(_dot_nt(h, w_ref[C_BD:C_BD + LANES, :]), alog_ref[...], dt_ref[...])

    cos = cos_ref[...]
    sin = sin_ref[...]
    uq = _dot_nt(h, w_ref[C_QB:C_QB + B_WIDTH, :])
    for g in range(B_WIDTH // LANES):
        qb_ref[0, :, g * LANES:(g + 1) * LANES] = (
            _rotary_group(uq[:, g * LANES:(g + 1) * LANES], cos, sin) * (B_HD ** -0.5)).astype(BF16)
    kb = _rotary_group(_dot_nt(h, w_ref[C_KB:C_KB + LANES, :]), cos, sin)
    vb = _dot_nt(h, w_ref[C_VB:C_VB + LANES, :])
    for o_ref, val in zip((kb_ref, kbr_ref, vd0_ref, vd1_ref), _kv_layouts(kb, vb)):
        o_ref[0] = val.astype(BF16)
    zb_ref[0] = _dot_nt(h, w_ref[C_ZB:C_ZB + B_WIDTH, :])

    @pl.when(t == pl.num_programs(1) - 1)
    def _():
        kbl_ref[0] = kb[tm - WINDOW:tm]
        vbl_ref[0] = vb[tm - WINDOW:tm]


def _proj(x, mod, mod_row0, w_r, conv_w, alog_row, dt_row, cos_t, sin_t):
    bsz, t, _ = x.shape
    tm = PROJ_TM
    row = lambda w: pl.BlockSpec((1, tm, w), lambda b, i: (b, i, 0))
    const2 = lambda s: pl.BlockSpec(s, lambda b, i: (0, 0))
    per_b = lambda r, w: pl.BlockSpec((1, r, w), lambda b, i: (b, 0, 0))
    wide = lambda w, dt=F32: jax.ShapeDtypeStruct((bsz, t, w), dt)
    return pl.pallas_call(
        _proj_kernel,
        grid=(bsz, t // tm),
        in_specs=[row(D_MODEL),
                  pl.BlockSpec((8, 3 * D_MODEL), lambda b, i: (mod_row0 // 8, 0)),
                  const2((W_COLS, D_MODEL)),
                  const2((CONV_W, A_QKV)),
                  const2((1, LANES)), const2((1, LANES)),
                  pl.BlockSpec((tm, LANES), lambda b, i: (i, 0)),
                  pl.BlockSpec((tm, LANES), lambda b, i: (i, 0))],
        out_specs=[row(A_WIDTH), row(A_WIDTH), row(A_WIDTH), row(A_WIDTH), row(LANES),
                   row(B_WIDTH), row(LANES), row(LANES), row(LANES), row(LANES), row(B_WIDTH),
                   per_b(CONV_W - 1, A_QKV), per_b(WINDOW, LANES), per_b(WINDOW, LANES)],
        out_shape=[wide(A_WIDTH), wide(A_WIDTH), wide(A_WIDTH), wide(A_WIDTH), wide(LANES),
                   wide(B_WIDTH, BF16), wide(LANES, BF16), wide(LANES, BF16), wide(LANES, BF16),
                   wide(LANES, BF16), wide(B_WIDTH),
                   jax.ShapeDtypeStruct((bsz, CONV_W - 1, A_QKV), F32),
                   jax.ShapeDtypeStruct((bsz, WINDOW, LANES), F32),
                   jax.ShapeDtypeStruct((bsz, WINDOW, LANES), F32)],
        scratch_shapes=[pltpu.VMEM((8, A_QKV), F32)],
        compiler_params=pltpu.CompilerParams(dimension_semantics=("arbitrary", "arbitrary"),
                                             vmem_limit_bytes=VMEM_LIMIT),
        name="proj",
    )(x, mod, w_r, conv_w, alog_row, dt_row, cos_t, sin_t)


def _delta_kernel(q_ref, k_ref, v_ref, gb_ref, za_ref, na_ref, oa_ref, st_ref,
                  s_scr, wq_s, ut_s, akd_s, gl_s):
    bsz, ct = q_ref.shape[0], q_ref.shape[1]
    nch = ct // CHUNK
    t = pl.program_id(0)
    wslot = t % 2
    rslot = 1 - wslot

    @pl.when(t == 0)
    def _():
        s_scr[...] = jnp.zeros(s_scr.shape, F32)
        wq_s[...] = jnp.zeros(wq_s.shape, BF16)
        ut_s[...] = jnp.zeros(ut_s.shape, F32)
        akd_s[...] = jnp.zeros(akd_s.shape, BF16)
        gl_s[...] = jnp.zeros(gl_s.shape, F32)

    units = [(b, c, h) for b in range(bsz) for c in range(nch) for h in range(A_HEADS)]
    uid = {u_: i for i, u_ in enumerate(units)}
    rows = lambda c: slice(c * CHUNK, (c + 1) * CHUNK)
    lanes = lambda h: slice(h * A_DK, (h + 1) * A_DK)
    ti = lax.broadcasted_iota(jnp.int32, (CHUNK, CHUNK), 0)
    ii = lax.broadcasted_iota(jnp.int32, (CHUNK, CHUNK), 1)
    na = na_ref[...]

    s_cur = {(b, h): s_scr[b * A_HEADS + h] for b in range(bsz) for h in range(A_HEADS)}
    ws, uu = {}, {}

    def rec_ws(c):
        for b in range(bsz):
            for h in range(A_HEADS):
                i = uid[b, c, h]
                ws[b, h] = _dot(wq_s[rslot, i], s_cur[b, h].astype(BF16))
                uu[b, h] = (ut_s[rslot, i] - ws[b, h][:CHUNK]).astype(BF16)

    def rec_ou(c):
        for b in range(bsz):
            for h in range(A_HEADS):
                i = uid[b, c, h]
                ou = _dot(akd_s[rslot, i], uu[b, h])
                o = ws[b, h][CHUNK:] + ou[:CHUNK]
                s_cur[b, h] = gl_s[rslot, i] * s_cur[b, h] + ou[CHUNK:]
                on = o * lax.rsqrt(jnp.mean(o * o, axis=-1, keepdims=True) + RMS_EPS) * na
                oa_ref[b, rows(c), lanes(h)] = (on * _silu(za_ref[b, rows(c), lanes(h)])).astype(BF16)

    rec_stages = []
    for c in range(nch):
        rec_stages += [lambda c=c: rec_ws(c), lambda c=c: rec_ou(c)]

    def run_rec(n_left_after):
        while rec_stages and len(rec_stages) > n_left_after:
            rec_stages.pop(0)()

    beta, g_col, g_last, eg, dec_incl, nmat, aqk = {}, {}, {}, {}, {}, {}, {}
    for b in range(bsz):
        gbv = gb_ref[b]
        rin = lax.broadcasted_iota(jnp.int32, gbv.shape, 0) % CHUNK
        gcs = gbv
        s = 1
        while s < CHUNK:
            gcs = gcs + jnp.where(rin >= s, pltpu.roll(gcs, s, axis=0), 0.0)
            s *= 2
        gcs_t = gcs.T
        for c in range(nch):
            for h in range(A_HEADS):
                u_ = (b, c, h)
                r0 = c * CHUNK
                beta[u_] = jnp.broadcast_to(gbv[rows(c), h:h + 1], (CHUNK, A_DK))
                g_col[u_] = jnp.broadcast_to(gcs[rows(c), A_HEADS + h:A_HEADS + h + 1], (CHUNK, A_DK))
                g_last[u_] = gcs[r0 + CHUNK - 1:r0 + CHUNK, A_HEADS + h:A_HEADS + h + 1]
                g_row = gcs_t[A_HEADS + h:A_HEADS + h + 1, rows(c)]
                dec_incl[u_] = jnp.exp(jnp.where(ti >= ii, g_col[u_][:, :CHUNK] - g_row, -jnp.inf))
                eg[u_] = jnp.exp(g_col[u_])

    n_prep = 8
    for (b, c, h) in units:
        kcb = k_ref[b, rows(c), lanes(h)].astype(BF16)
        qcb = q_ref[b, rows(c), lanes(h)].astype(BF16)
        kq = _dot_nt(jnp.concatenate([kcb, qcb], axis=0), kcb)
        nmat[b, c, h] = -(beta[b, c, h][:, :CHUNK] * kq[:CHUNK] * jnp.where(ti > ii, dec_incl[b, c, h], 0.0))
        aqk[b, c, h] = kq[CHUNK:] * dec_incl[b, c, h]
    run_rec(n_prep - 1)

    rsum = dict(nmat)
    pwb = {u_: nmat[u_].astype(BF16) for u_ in units}
    pw = {u_: _dot(pwb[u_], pwb[u_]) for u_ in units}
    run_rec(n_prep - 2)
    for step in range(1, 6):
        last = step == 5
        pwb = {u_: pw[u_].astype(BF16) for u_ in units}
        rp = {}
        for u_ in units:
            rb = rsum[u_].astype(BF16)
            rp[u_] = _dot(rb if last else jnp.concatenate([rb, pwb[u_]], axis=0), pwb[u_])
        for u_ in units:
            rsum[u_] = rsum[u_] + pw[u_] + rp[u_][:CHUNK]
            if not last:
                pw[u_] = rp[u_][CHUNK:]
        run_rec(n_prep - 2 - step)

    for (b, c, h) in units:
        u_ = (b, c, h)
        i = uid[u_]
        kc = k_ref[b, rows(c), lanes(h)]
        rhs = jnp.concatenate([(beta[u_] * eg[u_]) * kc, beta[u_] * v_ref[b, rows(c), lanes(h)]],
                              axis=-1)
        sol = rhs + _dot(rsum[u_].astype(BF16), rhs.astype(BF16))
        wq_s[wslot, i] = jnp.concatenate([sol[:, :A_DK], eg[u_] * q_ref[b, rows(c), lanes(h)]],
                                         axis=0).astype(BF16)
        ut_s[wslot, i] = sol[:, A_DK:]
        kd = jnp.exp(g_last[u_] - g_col[u_]) * kc
        akd_s[wslot, i] = jnp.concatenate([aqk[u_], kd.T], axis=0).astype(BF16)
        gl_s[wslot, i] = jnp.broadcast_to(jnp.exp(g_last[u_]), (1, A_DV))
    run_rec(0)

    for b in range(bsz):
        for h in range(A_HEADS):
            s_scr[b * A_HEADS + h] = s_cur[b, h]

    @pl.when(t == pl.num_programs(0) - 1)
    def _():
        for b in range(bsz):
            for h in range(A_HEADS):
                st_ref[b, h] = s_cur[b, h]


def _delta(q, k, v, gb, za, na_row):
    bsz, t, _ = q.shape
    ct = DELTA_CT
    nt = t // ct
    n_units = bsz * (ct // CHUNK) * A_HEADS
    prep = lambda w: pl.BlockSpec((bsz, ct, w), lambda i: (0, jnp.minimum(i, nt - 1), 0))
    rec = lambda w: pl.BlockSpec((bsz, ct, w), lambda i: (0, jnp.maximum(i - 1, 0), 0))
    return pl.pallas_call(
        _delta_kernel,
        grid=(nt + 1,),
        in_specs=[prep(A_WIDTH), prep(A_WIDTH), prep(A_WIDTH), prep(LANES), rec(A_WIDTH),
                  pl.BlockSpec((1, A_DV), lambda i: (0, 0))],
        out_specs=[rec(A_WIDTH),
                   pl.BlockSpec((bsz, A_HEADS, A_DK, A_DV), lambda i: (0, 0, 0, 0))],
        out_shape=[jax.ShapeDtypeStruct((bsz, t, A_WIDTH), BF16),
                   jax.ShapeDtypeStruct((bsz, A_HEADS, A_DK, A_DV), F32)],
        scratch_shapes=[pltpu.VMEM((bsz * A_HEADS, A_DK, A_DV), F32),
                        pltpu.VMEM((2, n_units, 2 * CHUNK, A_DK), BF16),
                        pltpu.VMEM((2, n_units, CHUNK, A_DV), F32),
                        pltpu.VMEM((2, n_units, 3 * CHUNK, CHUNK), BF16),
                        pltpu.VMEM((2, n_units, 1, A_DV), F32)],
        compiler_params=pltpu.CompilerParams(dimension_semantics=("arbitrary",),
                                             vmem_limit_bytes=VMEM_LIMIT),
        name="delta",
    )(q, k, v, gb, za, na_row)


def _swa_kernel(sink_ref, qb_ref, kc_ref, kp_ref, krc_ref, krp_ref, v0c_ref, v0p_ref, v1c_ref, v1p_ref,
                zb_ref, ob_ref):
    n = pl.program_id(1)
    tq = qb_ref.shape[1]
    blk = WINDOW
    kx = (jnp.concatenate([kp_ref[0], kc_ref[0]], axis=0), jnp.concatenate([krp_ref[0], krc_ref[0]], axis=0))
    vd = (jnp.concatenate([v0p_ref[0], v0c_ref[0]], axis=0), jnp.concatenate([v1p_ref[0], v1c_ref[0]], axis=0))

    a = lax.broadcasted_iota(jnp.int32, (2 * blk, 2 * blk), 0) % blk
    j = lax.broadcasted_iota(jnp.int32, (2 * blk, 2 * blk), 1)
    rel = a + blk - j
    band = (rel >= 0) & (rel <= WINDOW)
    band_first = band & ((n > 0) | (j >= blk))
    top = lax.broadcasted_iota(jnp.int32, (2 * blk, 1), 0) < blk
    low = _lane((blk, LANES)) < B_HD
    zero = jnp.zeros((blk, LANES), BF16)

    for i in range(tq // blk):
        valid = band_first if i == 0 else band
        qrows = slice(i * blk, (i + 1) * blk)
        krows = slice(i * blk, (i + 2) * blk)
        outs = {}
        for kh in range(B_KV_HEADS):
            for half in range(2):
                qs = []
                for g in range(2):
                    grp = kh * 2 + g
                    xg = qb_ref[0, qrows, grp * LANES:(grp + 1) * LANES]
                    qs.append(jnp.where(low if half == 0 else jnp.logical_not(low), xg, zero))
                qz = jnp.concatenate(qs, axis=0)
                sc = jnp.where(valid, _dot_nt(qz, kx[0 if kh == half else 1][krows]), -jnp.inf)
                sink = jnp.where(top, sink_ref[kh * B_GROUP + half], sink_ref[kh * B_GROUP + half + 2])
                m = jnp.maximum(jnp.max(sc, axis=-1, keepdims=True), sink)
                p = jnp.exp(sc - m)
                den = jnp.sum(p, axis=-1, keepdims=True) + jnp.exp(sink - m)
                outs[(kh, half)] = _dot(p.astype(BF16), vd[kh][krows]) / den
        for grp in range(B_WIDTH // LANES):
            kh, g = grp // 2, grp % 2
            og = jnp.where(low, outs[(kh, 0)][g * blk:(g + 1) * blk], outs[(kh, 1)][g * blk:(g + 1) * blk])
            gs = slice(grp * LANES, (grp + 1) * LANES)
            ob_ref[0, qrows, gs] = (og * _silu(zb_ref[0, qrows, gs])).astype(BF16)


def _swa(sinks, qb, kb, kbr, vd0, vd1, zb):
    bsz, t, _ = qb.shape
    tq = SWA_TQ
    per = tq // WINDOW
    cur = lambda w: pl.BlockSpec((1, tq, w), lambda b, i: (b, i, 0))
    prev = pl.BlockSpec((1, WINDOW, LANES), lambda b, i: (b, jnp.maximum(i * per - 1, 0), 0))
    return pl.pallas_call(
        _swa_kernel,
        grid=(bsz, t // tq),
        in_specs=[pl.BlockSpec(memory_space=pltpu.SMEM), cur(B_WIDTH),
                  cur(LANES), prev, cur(LANES), prev, cur(LANES), prev, cur(LANES), prev,
                  cur(B_WIDTH)],
        out_specs=cur(B_WIDTH),
        out_shape=jax.ShapeDtypeStruct((bsz, t, B_WIDTH), BF16),
        compiler_params=pltpu.CompilerParams(dimension_semantics=("arbitrary", "arbitrary"),
                                             vmem_limit_bytes=VMEM_LIMIT),
        name="swa",
    )(sinks, qb, kb, kb, kbr, kbr, vd0, vd0, vd1, vd1, zb)


def _out_kernel(oa_ref, ob_ref, x_ref, gate_ref, w_ref, g_ref, b_ref, y_ref, *, gate_per_batch):
    mix = _dot(oa_ref[0], w_ref[0:A_WIDTH, :]) + _dot(ob_ref[0], w_ref[A_WIDTH:MIX_WIDTH, :])
    gate = gate_ref[pl.ds(pl.program_id(0), 1), :] if gate_per_batch else gate_ref[...]
    r = DEEPNORM_ALPHA * x_ref[0] + (1.0 + gate) * mix
    y_ref[0] = _layer_norm(r, g_ref[...], b_ref[...])


def _out(oa, ob, x, mod, mod_row0, gate_per_batch, w_out, ln_g, ln_b, tm):
    bsz, t, _ = x.shape
    grows = 8 if gate_per_batch else tm
    row = lambda w: pl.BlockSpec((1, tm, w), lambda b, i: (b, i, 0))
    const2 = lambda s: pl.BlockSpec(s, lambda b, i: (0, 0))
    return pl.pallas_call(
        functools.partial(_out_kernel, gate_per_batch=gate_per_batch),
        grid=(bsz, t // tm),
        in_specs=[row(A_WIDTH), row(B_WIDTH), row(D_MODEL),
                  pl.BlockSpec((grows, D_MODEL), lambda b, i: (mod_row0 // grows, 2)),
                  const2((MIX_WIDTH, D_MODEL)), const2((1, D_MODEL)), const2((1, D_MODEL))],
        out_specs=row(D_MODEL),
        out_shape=jax.ShapeDtypeStruct((bsz, t, D_MODEL), F32),
        compiler_params=pltpu.CompilerParams(dimension_semantics=("arbitrary", "arbitrary"),
                                             vmem_limit_bytes=VMEM_LIMIT),
        name="out",
    )(oa, ob, x, mod, w_out, ln_g, ln_b)


def _sproj_kernel(x_ref, mod_ref, w_ref, cw_ref, cst_ref, alog_ref, dt_ref, cos_ref, sin_ref,
                  q_ref, k_ref, v_ref, za_ref, gb_ref, qb_ref, kb_ref, vb_ref, zb_ref, ncs_ref):
    shift = mod_ref[:, 0:D_MODEL]
    scale = mod_ref[:, D_MODEL:2 * D_MODEL]
    h = (x_ref[...] * (1.0 + scale) + shift).astype(BF16)

    for gi, o_ref in enumerate((q_ref, k_ref, v_ref)):
        c0 = gi * A_WIDTH
        cs = slice(c0, c0 + A_WIDTH)
        u = _dot_nt(h, w_ref[cs, :])
        acc = cst_ref[0, :, cs] * cw_ref[0:1, cs]
        acc = acc + cst_ref[1, :, cs] * cw_ref[1:2, cs]
        acc = acc + cst_ref[2, :, cs] * cw_ref[2:3, cs]
        acc = acc + u * cw_ref[3:4, cs]
        y = _silu(acc)
        if gi == 0:
            y = _l2norm_heads(y, A_DK ** -0.5)
        elif gi == 1:
            y = _l2norm_heads(y, 1.0)
        o_ref[...] = y
        ncs_ref[0, :, cs] = cst_ref[1, :, cs]
        ncs_ref[1, :, cs] = cst_ref[2, :, cs]
        ncs_ref[2, :, cs] = u

    za_ref[...] = _dot_nt(h, w_ref[C_ZA:C_ZA + A_WIDTH, :])
    gb_ref[...] = _gate_lanes(_dot_nt(h, w_ref[C_BD:C_BD + LANES, :]), alog_ref[...], dt_ref[...])

    cos = cos_ref[...]
    sin = sin_ref[...]
    uq = _dot_nt(h, w_ref[C_QB:C_QB + B_WIDTH, :])
    for g in range(B_WIDTH // LANES):
        qb_ref[:, g * LANES:(g + 1) * LANES] = (
            _rotary_group(uq[:, g * LANES:(g + 1) * LANES], cos, sin) * (B_HD ** -0.5))
    kb_ref[...] = _rotary_group(_dot_nt(h, w_ref[C_KB:C_KB + LANES, :]), cos, sin)
    vb_ref[...] = _dot_nt(h, w_ref[C_VB:C_VB + LANES, :])
    zb_ref[...] = _dot_nt(h, w_ref[C_ZB:C_ZB + B_WIDTH, :])


def _sproj(x, mod_s, w_r, conv_w, cst, alog_row, dt_row, cos_row, sin_row):
    n = x.shape[0]
    full = lambda s: pl.BlockSpec(s, lambda i: (0,) * len(s))
    wide = lambda w: jax.ShapeDtypeStruct((n, w), F32)
    return pl.pallas_call(
        _sproj_kernel,
        grid=(1,),
        in_specs=[full((n, D_MODEL)), pl.BlockSpec((n, 3 * D_MODEL), lambda i: (0, 0)),
                  full((W_COLS, D_MODEL)),
                  full((CONV_W, A_QKV)), full((CONV_W - 1, n, A_QKV)),
                  full((1, LANES)), full((1, LANES)), full((1, LANES)), full((1, LANES))],
        out_specs=[full((n, A_WIDTH)), full((n, A_WIDTH)), full((n, A_WIDTH)), full((n, A_WIDTH)),
                   full((n, LANES)), full((n, B_WIDTH)), full((n, LANES)), full((n, LANES)),
                   full((n, B_WIDTH)), full((CONV_W - 1, n, A_QKV))],
        out_shape=[wide(A_WIDTH), wide(A_WIDTH), wide(A_WIDTH), wide(A_WIDTH), wide(LANES),
                   wide(B_WIDTH), wide(LANES), wide(LANES), wide(B_WIDTH),
                   jax.ShapeDtypeStruct((CONV_W - 1, n, A_QKV), F32)],
        compiler_params=pltpu.CompilerParams(dimension_semantics=("arbitrary",),
                                             vmem_limit_bytes=VMEM_LIMIT),
        name="sproj",
    )(x, mod_s, w_r, conv_w, cst, alog_row, dt_row, cos_row, sin_row)


def _sstep_kernel(sink_ref, q_ref, k_ref, v_ref, gb_ref, za_ref, na_ref, st_ref,
                  qb_ref, kn_ref, vn_ref, zb_ref, ck_ref, cv_ref,
                  oa_ref, ob_ref, nst_ref, nck_ref, ncv_ref,
                  o_scr, ob_scr):
    bt = q_ref.shape[0]
    gbv = gb_ref[...]

    for h in range(A_HEADS):
        hs = slice(h * A_DK, (h + 1) * A_DK)
        q_t = q_ref[:, hs].T
        k_t = k_ref[:, hs].T
        for bb in range(bt):
            eg = jnp.exp(gbv[bb:bb + 1, A_HEADS + h:A_HEADS + h + 1])
            beta = gbv[bb:bb + 1, h:h + 1]
            kcol = k_t[:, bb:bb + 1]
            qcol = q_t[:, bb:bb + 1]
            s1 = eg * st_ref[bb, h]
            pred = jnp.sum(kcol * s1, axis=0, keepdims=True)
            upd = beta * (v_ref[bb:bb + 1, hs] - pred)
            s2 = s1 + kcol * upd
            nst_ref[bb, h] = s2
            o_scr[bb:bb + 1, hs] = jnp.sum(qcol * s2, axis=0, keepdims=True)
    na = na_ref[...]
    for h in range(A_HEADS):
        hs = slice(h * A_DK, (h + 1) * A_DK)
        o = o_scr[:, hs]
        on = o * lax.rsqrt(jnp.mean(o * o, axis=-1, keepdims=True) + RMS_EPS) * na
        oa_ref[:, hs] = (on * _silu(za_ref[:, hs])).astype(BF16)

    row8 = lax.broadcasted_iota(jnp.int32, (B_HEADS, LANES), 0)
    lane8 = _lane((B_HEADS, LANES))
    own_half = (lane8 >= B_HD) == (row8 >= B_GROUP)
    rcol = lax.broadcasted_iota(jnp.int32, (B_HEADS, 1), 0)
    sink = jnp.zeros((B_HEADS, 1), F32)
    for r in range(B_HEADS):
        sink = jnp.where(rcol == r, sink_ref[r], sink)
    qv = qb_ref[...]
    qv_r = jnp.concatenate([pltpu.roll(qv[:, g * LANES:(g + 1) * LANES], B_HD, axis=1)
                            for g in range(B_WIDTH // LANES)], axis=-1)
    kn_t = kn_ref[...].T
    vn_t = vn_ref[...].T
    newest = _lane((LANES, WINDOW)) == WINDOW - 1
    for bb in range(bt):
        qz = jnp.zeros((B_HEADS, LANES), F32)
        for r in range(B_HEADS):
            grp, half, kh = r // 2, r % 2, r // B_GROUP
            src = qv if half == kh else qv_r
            qz = jnp.where(row8 == r, src[bb:bb + 1, grp * LANES:(grp + 1) * LANES], qz)
        qz = jnp.where(own_half, qz, 0.0)
        kc = ck_ref[bb]
        vc = cv_ref[bb]
        knew = kn_ref[bb:bb + 1, :]
        vnew = vn_ref[bb:bb + 1, :]
        sc = _dot(qz, kc)
        sc_new = jnp.sum(qz * knew, axis=-1, keepdims=True)
        m = jnp.maximum(jnp.maximum(jnp.max(sc, axis=-1, keepdims=True), sc_new), sink)
        p = jnp.exp(sc - m)
        p_new = jnp.exp(sc_new - m)
        den = jnp.sum(p, axis=-1, keepdims=True) + p_new + jnp.exp(sink - m)
        o = (_dot_nt(p, vc) + p_new * vnew) / den
        o = jnp.where(own_half, o, 0.0)
        ob_scr[bb * B_HEADS:(bb + 1) * B_HEADS, :] = o + pltpu.roll(o, B_HD, axis=1)
        nck_ref[bb] = jnp.where(newest, kn_t[:, bb:bb + 1], pltpu.roll(kc, WINDOW - 1, axis=1))
        ncv_ref[bb] = jnp.where(newest, vn_t[:, bb:bb + 1], pltpu.roll(vc, WINDOW - 1, axis=1))
    low = _lane((bt, LANES)) < B_HD
    for grp in range(B_WIDTH // LANES):
        even = ob_scr[pl.ds(2 * grp, bt, stride=B_HEADS), :]
        odd = ob_scr[pl.ds(2 * grp + 1, bt, stride=B_HEADS), :]
        gs = slice(grp * LANES, (grp + 1) * LANES)
        ob_ref[:, gs] = (jnp.where(low, even, odd) * _silu(zb_ref[:, gs])).astype(BF16)


def _sstep(sinks, q, k, v, gb, za, na_row, state, qb, kn, vn, zb, ck, cv):
    n = q.shape[0]
    bt = STEP_BT
    row = lambda w: pl.BlockSpec((bt, w), lambda i: (i, 0))
    st_spec = pl.BlockSpec((bt, A_HEADS, A_DK, A_DV), lambda i: (i, 0, 0, 0))
    c_spec = pl.BlockSpec((bt, WINDOW, LANES), lambda i: (i, 0, 0))
    return pl.pallas_call(
        _sstep_kernel,
        grid=(n // bt,),
        in_specs=[pl.BlockSpec(memory_space=pltpu.SMEM),
                  row(A_WIDTH), row(A_WIDTH), row(A_WIDTH), row(LANES), row(A_WIDTH),
                  pl.BlockSpec((1, A_DV), lambda i: (0, 0)), st_spec,
                  row(B_WIDTH), row(LANES), row(LANES), row(B_WIDTH), c_spec, c_spec],
        out_specs=[row(A_WIDTH), row(B_WIDTH), st_spec, c_spec, c_spec],
        out_shape=[jax.ShapeDtypeStruct((n, A_WIDTH), BF16),
                   jax.ShapeDtypeStruct((n, B_WIDTH), BF16),
                   jax.ShapeDtypeStruct((n, A_HEADS, A_DK, A_DV), F32),
                   jax.ShapeDtypeStruct((n, WINDOW, LANES), F32),
                   jax.ShapeDtypeStruct((n, WINDOW, LANES), F32)],
        scratch_shapes=[pltpu.VMEM((bt, A_WIDTH), F32), pltpu.VMEM((bt * B_HEADS, LANES), F32)],
        compiler_params=pltpu.CompilerParams(dimension_semantics=("arbitrary",),
                                             vmem_limit_bytes=VMEM_LIMIT),
        name="sstep",
    )(sinks, q, k, v, gb, za, na_row, state, qb, kn, vn, zb, ck, cv)


def _rope_tables(pos):
    half = B_HD // 2
    inv = 1.0 / (ROPE_THETA ** (np.arange(half, dtype=np.float64) / half))
    ang = np.asarray(pos, np.float64)[:, None] * inv[None, :]
    cos, sin = np.cos(ang), np.sin(ang)
    reps = LANES // B_HD
    return (jnp.asarray(np.tile(np.concatenate([cos, cos], -1), (1, reps)), F32),
            jnp.asarray(np.tile(np.concatenate([-sin, sin], -1), (1, reps)), F32))


def _pad_row(vec, offset):
    return jnp.zeros((1, LANES), F32).at[0, offset:offset + vec.shape[0]].set(vec.astype(F32))


def _layer(x_prompt, x_sample, state_conv, state_delta, cache_k, cache_v, c_prompt, c_sample,
           w_ada, b_ada, w_in, conv_w, a_log, dt_bias, norm_a, sinks, w_out, ln_g, ln_b):
    bsz, seq, _ = x_prompt.shape
    n_s = x_sample.shape[0]

    w_t = jnp.swapaxes(w_in, 0, 1)
    w_r = jnp.concatenate([
        w_t[0:OFF_A_BETA], w_t[OFF_B_Q:PROJ_COLS], w_t[OFF_A_BETA:OFF_B_Q],
        jnp.zeros((LANES - 2 * A_HEADS, D_MODEL), w_in.dtype)], axis=0).astype(BF16)
    w_o = w_out.astype(BF16)
    alog_row = _pad_row(a_log, A_HEADS)
    dt_row = _pad_row(dt_bias, A_HEADS)
    na_row = norm_a.reshape(1, A_DV)
    g_row = ln_g.reshape(1, D_MODEL)
    b_row = ln_b.reshape(1, D_MODEL)

    assert n_s % 8 == 0 and bsz <= 8
    c_all = jnp.concatenate([c_sample, c_prompt, jnp.zeros((8 - bsz, D_MODEL), F32)], axis=0)
    mod = _ada(c_all, w_ada, b_ada.reshape(1, 3 * D_MODEL))

    cos_p, sin_p = _rope_tables(np.arange(seq))
    (q, k, v, za, gb, qb, kb, kbr, vd0, vd1, zb, conv_p, kb_last, vb_last) = _proj(
        x_prompt, mod, n_s, w_r, conv_w, alog_row, dt_row, cos_p, sin_p)
    oa, delta_p = _delta(q, k, v, gb, za, na_row)
    ob = _swa(sinks, qb, kb, kbr, vd0, vd1, zb)
    y_p = _out(oa, ob, x_prompt, mod, n_s, True, w_o, g_row, b_row, OUT_TM)
    swa_k_p = kb_last.reshape(bsz, WINDOW, B_KV_HEADS, B_HD)
    swa_v_p = vb_last.reshape(bsz, WINDOW, B_KV_HEADS, B_HD)

    cos_s, sin_s = _rope_tables(np.array([PAST_LEN]))
    xs = x_sample.reshape(n_s, D_MODEL)
    cst = jnp.transpose(state_conv, (1, 0, 2))
    sq, sk, sv, sza, sgb, sqb, skn, svn, szb, ncs = _sproj(xs, mod, w_r, conv_w, cst, alog_row, dt_row,
                                                           cos_s, sin_s)
    soa, sob, delta_s, nck, ncv = _sstep(sinks, sq, sk, sv, sgb, sza, na_row, state_delta,
                                         sqb, skn, svn, szb,
                                         jnp.swapaxes(cache_k.reshape(n_s, WINDOW, LANES), 1, 2),
                                         jnp.swapaxes(cache_v.reshape(n_s, WINDOW, LANES), 1, 2))
    y_s = _out(soa[None], sob[None], xs[None], mod, 0, False, w_o, g_row, b_row, n_s)
    conv_s = jnp.transpose(ncs, (1, 0, 2))
    unpack = lambda c: jnp.swapaxes(c, 1, 2).reshape(n_s, WINDOW, B_KV_HEADS, B_HD)
    return (y_p, y_s.reshape(n_s, 1, D_MODEL), conv_p, delta_p, swa_k_p, swa_v_p,
            conv_s, delta_s, unpack(nck), unpack(ncv))


def kernel(x_prompt, x_sample, state_conv, state_delta, cache_swa_k, cache_swa_v, c_prompt, c_sample,
           w_ada, b_ada, w_in, conv_w, a_log, dt_bias, norm_a, sinks, w_out, ln_g, ln_b):
    assert w_ada.shape[0] == DEPTH == 1
    outs = _layer(x_prompt, x_sample, state_conv[0], state_delta[0], cache_swa_k[0], cache_swa_v[0],
                  c_prompt, c_sample, w_ada[0], b_ada[0], w_in[0], conv_w[0], a_log[0], dt_bias[0],
                  norm_a[0], sinks[0], w_out[0], ln_g[0], ln_b[0])
    y_p, y_s = outs[0], outs[1]
    return (y_p, y_s) + tuple(o[None] for o in outs[2:])
```

```python
import functools

import jax
import jax.numpy as jnp
import numpy as np
from jax import lax
from jax.experimental import pallas as pl
from jax.experimental.pallas import tpu as pltpu

F32 = jnp.float32
BF16 = jnp.bfloat16

D_MODEL = 1024
DEPTH = 1
PAST_LEN = 8192
A_HEADS = 4
A_DK = 128
A_DV = 128
A_WIDTH = A_HEADS * A_DV
A_QKV = 3 * A_WIDTH
CONV_W = 4
CHUNK = 64
B_HEADS = 8
B_KV_HEADS = 2
B_HD = 64
B_GROUP = B_HEADS // B_KV_HEADS
B_WIDTH = B_HEADS * B_HD
B_KV_WIDTH = B_KV_HEADS * B_HD
WINDOW = 128
ROPE_THETA = 10000.0
MIX_WIDTH = A_WIDTH + B_WIDTH
DEEPNORM_ALPHA = (2 * DEPTH) ** 0.25
LN_EPS = 1e-5
RMS_EPS = 1e-6
L2_EPS = 1e-6

OFF_A_Z = A_QKV
OFF_A_BETA = OFF_A_Z + A_WIDTH
OFF_A_DECAY = OFF_A_BETA + A_HEADS
OFF_B_Q = OFF_A_DECAY + A_HEADS
OFF_B_K = OFF_B_Q + B_WIDTH
OFF_B_V = OFF_B_K + B_KV_WIDTH
OFF_B_Z = OFF_B_V + B_KV_WIDTH
PROJ_COLS = OFF_B_Z + B_WIDTH

LANES = 128
C_QKV = 0
C_ZA = C_QKV + A_QKV
C_QB = C_ZA + A_WIDTH
C_KB = C_QB + B_WIDTH
C_VB = C_KB + B_KV_WIDTH
C_ZB = C_VB + B_KV_WIDTH
C_BD = C_ZB + B_WIDTH
W_COLS = C_BD + LANES

VMEM_LIMIT = 56 * 1024 * 1024

PROJ_TM = 512
DELTA_CT = 256
SWA_TQ = 512
OUT_TM = 512
STEP_BT = 16


def _dot(a, b):
    return jnp.dot(a, b, preferred_element_type=F32)


def _dot_nt(a, b):
    return lax.dot_general(a, b, (((1,), (1,)), ((), ())), preferred_element_type=F32)


def _silu(x):
    return x * jax.nn.sigmoid(x)


def _softplus(x):
    return jnp.maximum(x, 0.0) + jnp.log1p(jnp.exp(-jnp.abs(x)))


def _lane(shape):
    return lax.broadcasted_iota(jnp.int32, shape, len(shape) - 1)


def _l2norm_heads(y, scale):
    outs = []
    for h in range(A_HEADS):
        xh = y[:, h * A_DK:(h + 1) * A_DK]
        ss = jnp.sum(xh * xh, axis=-1, keepdims=True)
        xn = xh * lax.rsqrt(ss + L2_EPS)
        outs.append(xn * scale if scale != 1.0 else xn)
    return jnp.concatenate(outs, axis=-1)


def _rotary_group(xg, cos, sin_signed):
    lane = _lane(xg.shape)
    swapped = jnp.where((lane % B_HD) < (B_HD // 2),
                        pltpu.roll(xg, LANES - B_HD // 2, axis=1),
                        pltpu.roll(xg, B_HD // 2, axis=1))
    return xg * cos + swapped * sin_signed


def _kv_layouts(kb, vb):
    low = _lane(kb.shape) < B_HD
    kbr = pltpu.roll(kb, B_HD, axis=1)
    vbr = pltpu.roll(vb, B_HD, axis=1)
    return kb, kbr, jnp.where(low, vb, vbr), jnp.where(low, vbr, vb)


def _gate_lanes(bd, alog_row, dt_row):
    lane = _lane(bd.shape)
    g = -jnp.exp(alog_row) * _softplus(bd + dt_row)
    return jnp.where(lane < A_HEADS, jax.nn.sigmoid(bd), g)


def _layer_norm(r, g, b):
    mu = jnp.mean(r, axis=-1, keepdims=True)
    d = r - mu
    var = jnp.mean(d * d, axis=-1, keepdims=True)
    return d * lax.rsqrt(var + LN_EPS) * g + b


def _ada_kernel(c_ref, w_ref, b_ref, o_ref):
    o_ref[...] = _dot(c_ref[...].astype(BF16), w_ref[...].astype(BF16)) + b_ref[...]


def _ada(c_all, w_ada, b_ada):
    rows = c_all.shape[0]
    tn = 768
    return pl.pallas_call(
        _ada_kernel,
        grid=(3 * D_MODEL // tn,),
        in_specs=[pl.BlockSpec((rows, D_MODEL), lambda j: (0, 0)),
                  pl.BlockSpec((D_MODEL, tn), lambda j: (0, j)),
                  pl.BlockSpec((1, tn), lambda j: (0, j))],
        out_specs=pl.BlockSpec((rows, tn), lambda j: (0, j)),
        out_shape=jax.ShapeDtypeStruct((rows, 3 * D_MODEL), F32),
        compiler_params=pltpu.CompilerParams(dimension_semantics=("arbitrary",),
                                             vmem_limit_bytes=VMEM_LIMIT),
        name="ada",
    )(c_all, w_ada, b_ada)


def _proj_kernel(x_ref, mod_ref, w_ref, cw_ref, alog_ref, dt_ref, cos_ref, sin_ref,
                 q_ref, k_ref, v_ref, za_ref, gb_ref, qb_ref, kb_ref, kbr_ref, vd0_ref, vd1_ref,
                 zb_ref, cst_ref, kbl_ref, vbl_ref, tail):
    tm = x_ref.shape[1]
    t = pl.program_id(1)

    @pl.when(t == 0)
    def _():
        tail[...] = jnp.zeros(tail.shape, F32)

    brow = pl.ds(pl.program_id(0), 1)
    shift = mod_ref[brow, 0:D_MODEL]
    scale = mod_ref[brow, D_MODEL:2 * D_MODEL]
    h = (x_ref[0] * (1.0 + scale) + shift).astype(BF16)

    row8 = lax.broadcasted_iota(jnp.int32, (8, A_WIDTH), 0)
    for gi, o_ref in enumerate((q_ref, k_ref, v_ref)):
        cs = slice(gi * A_WIDTH, (gi + 1) * A_WIDTH)
        u = _dot(h, w_ref[:, cs])
        prev = tail[:, cs]
        acc = None
        head = None
        for j in range(CONV_W - 1, 0, -1):
            wj = cw_ref[CONV_W - 1 - j:CONV_W - j, cs]
            rolled = pltpu.roll(u, j, axis=0)
            first = jnp.where(row8 < j, pltpu.roll(prev, j, axis=0), rolled[0:8]) * wj
            acc = rolled * wj if acc is None else acc + rolled * wj
            head = first if head is None else head + first
        w_last = cw_ref[CONV_W - 1:CONV_W, cs]
        acc = acc + u * w_last
        head = head + u[0:8] * w_last
        y = _silu(jnp.concatenate([head, acc[8:]], axis=0))
        if gi == 0:
            y = _l2norm_heads(y, A_DK ** -0.5)
        elif gi == 1:
            y = _l2norm_heads(y, 1.0)
        o_ref[0] = y
        tail[:, cs] = u[tm - 8:tm]
        cst_ref[0, :, cs] = u[tm - (CONV_W - 1):tm]

    za_ref[0] = _dot(h, w_ref[:, C_ZA:C_ZA + A_WIDTH])
    gb_ref[0] = _gate_lanes(_dot(h, w_ref[:, C_BD:C_BD + LANES]), alog_ref[...], dt_ref[...])

    cos = cos_ref[...]
    sin = sin_ref[...]
    uq = _dot(h, w_ref[:, C_QB:C_QB + B_WIDTH])
    for g in range(B_WIDTH // LANES):
        qb_ref[0, :, g * LANES:(g + 1) * LANES] = (
            _rotary_group(uq[:, g * LANES:(g + 1) * LANES], cos, sin) * (B_HD ** -0.5)).astype(BF16)
    kb = _rotary_group(_dot(h, w_ref[:, C_KB:C_KB + LANES]), cos, sin)
    vb = _dot(h, w_ref[:, C_VB:C_VB + LANES])
    for o_ref, val in zip((kb_ref, kbr_ref, vd0_ref, vd1_ref), _kv_layouts(kb, vb)):
        o_ref[0] = val.astype(BF16)
    zb_ref[0] = _dot(h, w_ref[:, C_ZB:C_ZB + B_WIDTH])

    @pl.when(t == pl.num_programs(1) - 1)
    def _():
        kbl_ref[0] = kb[tm - WINDOW:tm]
        vbl_ref[0] = vb[tm - WINDOW:tm]


def _proj(x, mod, mod_row0, w_r, conv_w, alog_row, dt_row, cos_t, sin_t):
    bsz, t, _ = x.shape
    tm = PROJ_TM
    row = lambda w: pl.BlockSpec((1, tm, w), lambda b, i: (b, i, 0))
    const2 = lambda s: pl.BlockSpec(s, lambda b, i: (0, 0))
    per_b = lambda r, w: pl.BlockSpec((1, r, w), lambda b, i: (b, 0, 0))
    wide = lambda w, dt=F32: jax.ShapeDtypeStruct((bsz, t, w), dt)
    return pl.pallas_call(
        _proj_kernel,
        grid=(bsz, t // tm),
        in_specs=[row(D_MODEL),
                  pl.BlockSpec((8, 3 * D_MODEL), lambda b, i: (mod_row0 // 8, 0)),
                  const2((D_MODEL, W_COLS)),
                  const2((CONV_W, A_QKV)),
                  const2((1, LANES)), const2((1, LANES)),
                  pl.BlockSpec((tm, LANES), lambda b, i: (i, 0)),
                  pl.BlockSpec((tm, LANES), lambda b, i: (i, 0))],
        out_specs=[row(A_WIDTH), row(A_WIDTH), row(A_WIDTH), row(A_WIDTH), row(LANES),
                   row(B_WIDTH), row(LANES), row(LANES), row(LANES), row(LANES), row(B_WIDTH),
                   per_b(CONV_W - 1, A_QKV), per_b(WINDOW, LANES), per_b(WINDOW, LANES)],
        out_shape=[wide(A_WIDTH), wide(A_WIDTH), wide(A_WIDTH), wide(A_WIDTH), wide(LANES),
                   wide(B_WIDTH, BF16), wide(LANES, BF16), wide(LANES, BF16), wide(LANES, BF16),
                   wide(LANES, BF16), wide(B_WIDTH),
                   jax.ShapeDtypeStruct((bsz, CONV_W - 1, A_QKV), F32),
                   jax.ShapeDtypeStruct((bsz, WINDOW, LANES), F32),
                   jax.ShapeDtypeStruct((bsz, WINDOW, LANES), F32)],
        scratch_shapes=[pltpu.VMEM((8, A_QKV), F32)],
        compiler_params=pltpu.CompilerParams(dimension_semantics=("arbitrary", "arbitrary"),
                                             vmem_limit_bytes=VMEM_LIMIT),
        name="proj",
    )(x, mod, w_r, conv_w, alog_row, dt_row, cos_t, sin_t)


def _delta_kernel(q_ref, k_ref, v_ref, gb_ref, za_ref, na_ref, oa_ref, st_ref,
                  s_scr, wq_s, ut_s, akd_s, gl_s):
    bsz, ct = q_ref.shape[0], q_ref.shape[1]
    nch = ct // CHUNK
    t = pl.program_id(0)
    wslot = t % 2
    rslot = 1 - wslot

    @pl.when(t == 0)
    def _():
        s_scr[...] = jnp.zeros(s_scr.shape, F32)
        wq_s[...] = jnp.zeros(wq_s.shape, BF16)
        ut_s[...] = jnp.zeros(ut_s.shape, F32)
        akd_s[...] = jnp.zeros(akd_s.shape, BF16)
        gl_s[...] = jnp.zeros(gl_s.shape, F32)

    units = [(b, c, h) for b in range(bsz) for c in range(nch) for h in range(A_HEADS)]
    uid = {u_: i for i, u_ in enumerate(units)}
    rows = lambda c: slice(c * CHUNK, (c + 1) * CHUNK)
    lanes = lambda h: slice(h * A_DK, (h + 1) * A_DK)
    ti = lax.broadcasted_iota(jnp.int32, (CHUNK, CHUNK), 0)
    ii = lax.broadcasted_iota(jnp.int32, (CHUNK, CHUNK), 1)
    na = na_ref[...]

    s_cur = {(b, h): s_scr[b * A_HEADS + h] for b in range(bsz) for h in range(A_HEADS)}
    ws, uu = {}, {}

    def rec_ws(c):
        for b in range(bsz):
            for h in range(A_HEADS):
                i = uid[b, c, h]
                ws[b, h] = _dot(wq_s[rslot, i], s_cur[b, h].astype(BF16))
                uu[b, h] = (ut_s[rslot, i] - ws[b, h][:CHUNK]).astype(BF16)

    def rec_ou(c):
        for b in range(bsz):
            for h in range(A_HEADS):
                i = uid[b, c, h]
                ou = _dot(akd_s[rslot, i], uu[b, h])
                o = ws[b, h][CHUNK:] + ou[:CHUNK]
                s_cur[b, h] = gl_s[rslot, i] * s_cur[b, h] + ou[CHUNK:]
                on = o * lax.rsqrt(jnp.mean(o * o, axis=-1, keepdims=True) + RMS_EPS) * na
                oa_ref[b, rows(c), lanes(h)] = (on * _silu(za_ref[b, rows(c), lanes(h)])).astype(BF16)

    rec_stages = []
    for c in range(nch):
        rec_stages += [lambda c=c: rec_ws(c), lambda c=c: rec_ou(c)]

    def run_rec(n_left_after):
        while rec_stages and len(rec_stages) > n_left_after:
            rec_stages.pop(0)()

    beta, g_col, g_last, eg, dec_incl, nmat, aqk = {}, {}, {}, {}, {}, {}, {}
    for b in range(bsz):
        gbv = gb_ref[b]
        rin = lax.broadcasted_iota(jnp.int32, gbv.shape, 0) % CHUNK
        gcs = gbv
        s = 1
        while s < CHUNK:
            gcs = gcs + jnp.where(rin >= s, pltpu.roll(gcs, s, axis=0), 0.0)
            s *= 2
        gcs_t = gcs.T
        for c in range(nch):
            for h in range(A_HEADS):
                u_ = (b, c, h)
                r0 = c * CHUNK
                beta[u_] = jnp.broadcast_to(gbv[rows(c), h:h + 1], (CHUNK, A_DK))
                g_col[u_] = jnp.broadcast_to(gcs[rows(c), A_HEADS + h:A_HEADS + h + 1], (CHUNK, A_DK))
                g_last[u_] = gcs[r0 + CHUNK - 1:r0 + CHUNK, A_HEADS + h:A_HEADS + h + 1]
                g_row = gcs_t[A_HEADS + h:A_HEADS + h + 1, rows(c)]
                dec_incl[u_] = jnp.exp(jnp.where(ti >= ii, g_col[u_][:, :CHUNK] - g_row, -jnp.inf))
                eg[u_] = jnp.exp(g_col[u_])

    n_prep = 8
    for (b, c, h) in units:
        kcb = k_ref[b, rows(c), lanes(h)].astype(BF16)
        qcb = q_ref[b, rows(c), lanes(h)].astype(BF16)
        kq = _dot_nt(jnp.concatenate([kcb, qcb], axis=0), kcb)
        nmat[b, c, h] = -(beta[b, c, h][:, :CHUNK] * kq[:CHUNK] * jnp.where(ti > ii, dec_incl[b, c, h], 0.0))
        aqk[b, c, h] = kq[CHUNK:] * dec_incl[b, c, h]
    run_rec(n_prep - 1)

    rsum = dict(nmat)
    pwb = {u_: nmat[u_].astype(BF16) for u_ in units}
    pw = {u_: _dot(pwb[u_], pwb[u_]) for u_ in units}
    run_rec(n_prep - 2)
    for step in range(1, 6):
        last = step == 5
        pwb = {u_: pw[u_].astype(BF16) for u_ in units}
        rp = {}
        for u_ in units:
            rb = rsum[u_].astype(BF16)
            rp[u_] = _dot(rb if last else jnp.concatenate([rb, pwb[u_]], axis=0), pwb[u_])
        for u_ in units:
            rsum[u_] = rsum[u_] + pw[u_] + rp[u_][:CHUNK]
            if not last:
                pw[u_] = rp[u_][CHUNK:]
        run_rec(n_prep - 2 - step)

    for (b, c, h) in units:
        u_ = (b, c, h)
        i = uid[u_]
        kc = k_ref[b, rows(c), lanes(h)]
        rhs = jnp.concatenate([(beta[u_] * eg[u_]) * kc, beta[u_] * v_ref[b, rows(c), lanes(h)]],
                              axis=-1)
        sol = rhs + _dot(rsum[u_].astype(BF16), rhs.astype(BF16))
        wq_s[wslot, i] = jnp.concatenate([sol[:, :A_DK], eg[u_] * q_ref[b, rows(c), lanes(h)]],
                                         axis=0).astype(BF16)
        ut_s[wslot, i] = sol[:, A_DK:]
        kd = jnp.exp(g_last[u_] - g_col[u_]) * kc
        akd_s[wslot, i] = jnp.concatenate([aqk[u_], kd.T], axis=0).astype(BF16)
        gl_s[wslot, i] = jnp.broadcast_to(jnp.exp(g_last[u_]), (1, A_DV))
    run_rec(0)

    for b in range(bsz):
        for h in range(A_HEADS):
            s_scr[b * A_HEADS + h] = s_cur[b, h]

    @pl.when(t == pl.num_programs(0) - 1)
    def _():
        for b in range(bsz):
            for h in range(A_HEADS):
                st_ref[b, h] = s_cur[b, h]


def _delta(q, k, v, gb, za, na_row):
    bsz, t, _ = q.shape
    ct = DELTA_CT
    nt = t // ct
    n_units = bsz * (ct // CHUNK) * A_HEADS
    prep = lambda w: pl.BlockSpec((bsz, ct, w), lambda i: (0, jnp.minimum(i, nt - 1), 0))
    rec = lambda w: pl.BlockSpec((bsz, ct, w), lambda i: (0, jnp.maximum(i - 1, 0), 0))
    return pl.pallas_call(
        _delta_kernel,
        grid=(nt + 1,),
        in_specs=[prep(A_WIDTH), prep(A_WIDTH), prep(A_WIDTH), prep(LANES), rec(A_WIDTH),
                  pl.BlockSpec((1, A_DV), lambda i: (0, 0))],
        out_specs=[rec(A_WIDTH),
                   pl.BlockSpec((bsz, A_HEADS, A_DK, A_DV), lambda i: (0, 0, 0, 0))],
        out_shape=[jax.ShapeDtypeStruct((bsz, t, A_WIDTH), BF16),
                   jax.ShapeDtypeStruct((bsz, A_HEADS, A_DK, A_DV), F32)],
        scratch_shapes=[pltpu.VMEM((bsz * A_HEADS, A_DK, A_DV), F32),
                        pltpu.VMEM((2, n_units, 2 * CHUNK, A_DK), BF16),
                        pltpu.VMEM((2, n_units, CHUNK, A_DV), F32),
                        pltpu.VMEM((2, n_units, 3 * CHUNK, CHUNK), BF16),
                        pltpu.VMEM((2, n_units, 1, A_DV), F32)],
        compiler_params=pltpu.CompilerParams(dimension_semantics=("arbitrary",),
                                             vmem_limit_bytes=VMEM_LIMIT),
        name="delta",
    )(q, k, v, gb, za, na_row)


def _swa_kernel(sink_ref, qb_ref, kc_ref, kp_ref, krc_ref, krp_ref, v0c_ref, v0p_ref, v1c_ref, v1p_ref,
                zb_ref, ob_ref):
    n = pl.program_id(1)
    tq = qb_ref.shape[1]
    blk = WINDOW
    kx = (jnp.concatenate([kp_ref[0], kc_ref[0]], axis=0), jnp.concatenate([krp_ref[0], krc_ref[0]], axis=0))
    vd = (jnp.concatenate([v0p_ref[0], v0c_ref[0]], axis=0), jnp.concatenate([v1p_ref[0], v1c_ref[0]], axis=0))

    a = lax.broadcasted_iota(jnp.int32, (2 * blk, 2 * blk), 0) % blk
    j = lax.broadcasted_iota(jnp.int32, (2 * blk, 2 * blk), 1)
    rel = a + blk - j
    band = (rel >= 0) & (rel <= WINDOW)
    band_first = band & ((n > 0) | (j >= blk))
    top = lax.broadcasted_iota(jnp.int32, (2 * blk, 1), 0) < blk
    low = _lane((blk, LANES)) < B_HD
    zero = jnp.zeros((blk, LANES), BF16)

    for i in range(tq // blk):
        valid = band_first if i == 0 else band
        qrows = slice(i * blk, (i + 1) * blk)
        krows = slice(i * blk, (i + 2) * blk)
        outs = {}
        for kh in range(B_KV_HEADS):
            for half in range(2):
                qs = []
                for g in range(2):
                    grp = kh * 2 + g
                    xg = qb_ref[0, qrows, grp * LANES:(grp + 1) * LANES]
                    qs.append(jnp.where(low if half == 0 else jnp.logical_not(low), xg, zero))
                qz = jnp.concatenate(qs, axis=0)
                sc = jnp.where(valid, _dot_nt(qz, kx[0 if kh == half else 1][krows]), -jnp.inf)
                sink = jnp.where(top, sink_ref[kh * B_GROUP + half], sink_ref[kh * B_GROUP + half + 2])
                m = jnp.maximum(jnp.max(sc, axis=-1, keepdims=True), sink)
                p = jnp.exp(sc - m)
                den = jnp.sum(p, axis=-1, keepdims=True) + jnp.exp(sink - m)
                outs[(kh, half)] = _dot(p.astype(BF16), vd[kh][krows]) / den
        for grp in range(B_WIDTH // LANES):
            kh, g = grp // 2, grp % 2
            og = jnp.where(low, outs[(kh, 0)][g * blk:(g + 1) * blk], outs[(kh, 1)][g * blk:(g + 1) * blk])
            gs = slice(grp * LANES, (grp + 1) * LANES)
            ob_ref[0, qrows, gs] = (og * _silu(zb_ref[0, qrows, gs])).astype(BF16)


def _swa(sinks, qb, kb, kbr, vd0, vd1, zb):
    bsz, t, _ = qb.shape
    tq = SWA_TQ
    per = tq // WINDOW
    cur = lambda w: pl.BlockSpec((1, tq, w), lambda b, i: (b, i, 0))
    prev = pl.BlockSpec((1, WINDOW, LANES), lambda b, i: (b, jnp.maximum(i * per - 1, 0), 0))
    return pl.pallas_call(
        _swa_kernel,
        grid=(bsz, t // tq),
        in_specs=[pl.BlockSpec(memory_space=pltpu.SMEM), cur(B_WIDTH),
                  cur(LANES), prev, cur(LANES), prev, cur(LANES), prev, cur(LANES), prev,
                  cur(B_WIDTH)],
        out_specs=cur(B_WIDTH),
        out_shape=jax.ShapeDtypeStruct((bsz, t, B_WIDTH), BF16),
        compiler_params=pltpu.CompilerParams(dimension_semantics=("arbitrary", "arbitrary"),
                                             vmem_limit_bytes=VMEM_LIMIT),
        name="swa",
    )(sinks, qb, kb, kb, kbr, kbr, vd0, vd0, vd1, vd1, zb)


def _out_kernel(oa_ref, ob_ref, x_ref, gate_ref, w_ref, g_ref, b_ref, y_ref, *, gate_per_batch):
    mix = _dot(oa_ref[0], w_ref[0:A_WIDTH, :]) + _dot(ob_ref[0], w_ref[A_WIDTH:MIX_WIDTH, :])
    gate = gate_ref[pl.ds(pl.program_id(0), 1), :] if gate_per_batch else gate_ref[...]
    r = DEEPNORM_ALPHA * x_ref[0] + (1.0 + gate) * mix
    y_ref[0] = _layer_norm(r, g_ref[...], b_ref[...])


def _out(oa, ob, x, mod, mod_row0, gate_per_batch, w_out, ln_g, ln_b, tm):
    bsz, t, _ = x.shape
    grows = 8 if gate_per_batch else tm
    row = lambda w: pl.BlockSpec((1, tm, w), lambda b, i: (b, i, 0))
    const2 = lambda s: pl.BlockSpec(s, lambda b, i: (0, 0))
    return pl.pallas_call(
        functools.partial(_out_kernel, gate_per_batch=gate_per_batch),
        grid=(bsz, t // tm),
        in_specs=[row(A_WIDTH), row(B_WIDTH), row(D_MODEL),
                  pl.BlockSpec((grows, D_MODEL), lambda b, i: (mod_row0 // grows, 2)),
                  const2((MIX_WIDTH, D_MODEL)), const2((1, D_MODEL)), const2((1, D_MODEL))],
        out_specs=row(D_MODEL),
        out_shape=jax.ShapeDtypeStruct((bsz, t, D_MODEL), F32),
        compiler_params=pltpu.CompilerParams(dimension_semantics=("arbitrary", "arbitrary"),
                                             vmem_limit_bytes=VMEM_LIMIT),
        name="out",
    )(oa, ob, x, mod, w_out, ln_g, ln_b)


def _sproj_kernel(x_ref, mod_ref, w_ref, cw_ref, cst_ref, alog_ref, dt_ref, cos_ref, sin_ref,
                  q_ref, k_ref, v_ref, za_ref, gb_ref, qb_ref, kb_ref, vb_ref, zb_ref, ncs_ref):
    shift = mod_ref[:, 0:D_MODEL]
    scale = mod_ref[:, D_MODEL:2 * D_MODEL]
    h = (x_ref[...] * (1.0 + scale) + shift).astype(BF16)

    for gi, o_ref in enumerate((q_ref, k_ref, v_ref)):
        c0 = gi * A_WIDTH
        cs = slice(c0, c0 + A_WIDTH)
        u = _dot(h, w_ref[:, cs])
        acc = cst_ref[0, :, cs] * cw_ref[0:1, cs]
        acc = acc + cst_ref[1, :, cs] * cw_ref[1:2, cs]
        acc = acc + cst_ref[2, :, cs] * cw_ref[2:3, cs]
        acc = acc + u * cw_ref[3:4, cs]
        y = _silu(acc)
        if gi == 0:
            y = _l2norm_heads(y, A_DK ** -0.5)
        elif gi == 1:
            y = _l2norm_heads(y, 1.0)
        o_ref[...] = y
        ncs_ref[0, :, cs] = cst_ref[1, :, cs]
        ncs_ref[1, :, cs] = cst_ref[2, :, cs]
        ncs_ref[2, :, cs] = u

    za_ref[...] = _dot(h, w_ref[:, C_ZA:C_ZA + A_WIDTH])
    gb_ref[...] = _gate_lanes(_dot(h, w_ref[:, C_BD:C_BD + LANES]), alog_ref[...], dt_ref[...])

    cos = cos_ref[...]
    sin = sin_ref[...]
    uq = _dot(h, w_ref[:, C_QB:C_QB + B_WIDTH])
    for g in range(B_WIDTH // LANES):
        qb_ref[:, g * LANES:(g + 1) * LANES] = (
            _rotary_group(uq[:, g * LANES:(g + 1) * LANES], cos, sin) * (B_HD ** -0.5))
    kb_ref[...] = _rotary_group(_dot(h, w_ref[:, C_KB:C_KB + LANES]), cos, sin)
    vb_ref[...] = _dot(h, w_ref[:, C_VB:C_VB + LANES])
    zb_ref[...] = _dot(h, w_ref[:, C_ZB:C_ZB + B_WIDTH])


def _sproj(x, mod_s, w_r, conv_w, cst, alog_row, dt_row, cos_row, sin_row):
    n = x.shape[0]
    full = lambda s: pl.BlockSpec(s, lambda i: (0,) * len(s))
    wide = lambda w: jax.ShapeDtypeStruct((n, w), F32)
    return pl.pallas_call(
        _sproj_kernel,
        grid=(1,),
        in_specs=[full((n, D_MODEL)), pl.BlockSpec((n, 3 * D_MODEL), lambda i: (0, 0)),
                  full((D_MODEL, W_COLS)),
                  full((CONV_W, A_QKV)), full((CONV_W - 1, n, A_QKV)),
                  full((1, LANES)), full((1, LANES)), full((1, LANES)), full((1, LANES))],
        out_specs=[full((n, A_WIDTH)), full((n, A_WIDTH)), full((n, A_WIDTH)), full((n, A_WIDTH)),
                   full((n, LANES)), full((n, B_WIDTH)), full((n, LANES)), full((n, LANES)),
                   full((n, B_WIDTH)), full((CONV_W - 1, n, A_QKV))],
        out_shape=[wide(A_WIDTH), wide(A_WIDTH), wide(A_WIDTH), wide(A_WIDTH), wide(LANES),
                   wide(B_WIDTH), wide(LANES), wide(LANES), wide(B_WIDTH),
                   jax.ShapeDtypeStruct((CONV_W - 1, n, A_QKV), F32)],
        compiler_params=pltpu.CompilerParams(dimension_semantics=("arbitrary",),
                                             vmem_limit_bytes=VMEM_LIMIT),
        name="sproj",
    )(x, mod_s, w_r, conv_w, cst, alog_row, dt_row, cos_row, sin_row)


def _sstep_kernel(sink_ref, q_ref, k_ref, v_ref, gb_ref, za_ref, na_ref, st_ref,
                  qb_ref, kn_ref, vn_ref, zb_ref, ck_ref, cv_ref,
                  oa_ref, ob_ref, nst_ref, nck_ref, ncv_ref,
                  o_scr, ob_scr):
    bt = q_ref.shape[0]
    gbv = gb_ref[...]

    for h in range(A_HEADS):
        hs = slice(h * A_DK, (h + 1) * A_DK)
        q_t = q_ref[:, hs].T
        k_t = k_ref[:, hs].T
        for bb in range(bt):
            eg = jnp.exp(gbv[bb:bb + 1, A_HEADS + h:A_HEADS + h + 1])
            beta = gbv[bb:bb + 1, h:h + 1]
            kcol = k_t[:, bb:bb + 1]
            qcol = q_t[:, bb:bb + 1]
            s1 = eg * st_ref[bb, h]
            pred = jnp.sum(kcol * s1, axis=0, keepdims=True)
            upd = beta * (v_ref[bb:bb + 1, hs] - pred)
            s2 = s1 + kcol * upd
            nst_ref[bb, h] = s2
            o_scr[bb:bb + 1, hs] = jnp.sum(qcol * s2, axis=0, keepdims=True)
    na = na_ref[...]
    for h in range(A_HEADS):
        hs = slice(h * A_DK, (h + 1) * A_DK)
        o = o_scr[:, hs]
        on = o * lax.rsqrt(jnp.mean(o * o, axis=-1, keepdims=True) + RMS_EPS) * na
        oa_ref[:, hs] = (on * _silu(za_ref[:, hs])).astype(BF16)

    row8 = lax.broadcasted_iota(jnp.int32, (B_HEADS, LANES), 0)
    lane8 = _lane((B_HEADS, LANES))
    own_half = (lane8 >= B_HD) == (row8 >= B_GROUP)
    rcol = lax.broadcasted_iota(jnp.int32, (B_HEADS, 1), 0)
    sink = jnp.zeros((B_HEADS, 1), F32)
    for r in range(B_HEADS):
        sink = jnp.where(rcol == r, sink_ref[r], sink)
    qv = qb_ref[...]
    qv_r = jnp.concatenate([pltpu.roll(qv[:, g * LANES:(g + 1) * LANES], B_HD, axis=1)
                            for g in range(B_WIDTH // LANES)], axis=-1)
    kn_t = kn_ref[...].T
    vn_t = vn_ref[...].T
    newest = _lane((LANES, WINDOW)) == WINDOW - 1
    qzs, scs = [], []
    for bb in range(bt):
        qz = jnp.zeros((B_HEADS, LANES), F32)
        for r in range(B_HEADS):
            grp, half, kh = r // 2, r % 2, r // B_GROUP
            src = qv if half == kh else qv_r
            qz = jnp.where(row8 == r, src[bb:bb + 1, grp * LANES:(grp + 1) * LANES], qz)
        qzs.append(jnp.where(own_half, qz, 0.0))
    for bb in range(bt):
        scs.append(_dot(qzs[bb], ck_ref[bb]))
    ps, pnews, dens = [], [], []
    for bb in range(bt):
        sc_new = jnp.sum(qzs[bb] * kn_ref[bb:bb + 1, :], axis=-1, keepdims=True)
        m = jnp.maximum(jnp.maximum(jnp.max(scs[bb], axis=-1, keepdims=True), sc_new), sink)
        p = jnp.exp(scs[bb] - m)
        p_new = jnp.exp(sc_new - m)
        ps.append(p)
        pnews.append(p_new)
        dens.append(jnp.sum(p, axis=-1, keepdims=True) + p_new + jnp.exp(sink - m))
    pvs = [_dot_nt(ps[bb], cv_ref[bb]) for bb in range(bt)]
    for bb in range(bt):
        o = (pvs[bb] + pnews[bb] * vn_ref[bb:bb + 1, :]) / dens[bb]
        o = jnp.where(own_half, o, 0.0)
        ob_scr[bb * B_HEADS:(bb + 1) * B_HEADS, :] = o + pltpu.roll(o, B_HD, axis=1)
    for bb in range(bt):
        nck_ref[bb] = jnp.where(newest, kn_t[:, bb:bb + 1], pltpu.roll(ck_ref[bb], WINDOW - 1, axis=1))
        ncv_ref[bb] = jnp.where(newest, vn_t[:, bb:bb + 1], pltpu.roll(cv_ref[bb], WINDOW - 1, axis=1))
    low = _lane((bt, LANES)) < B_HD
    for grp in range(B_WIDTH // LANES):
        even = ob_scr[pl.ds(2 * grp, bt, stride=B_HEADS), :]
        odd = ob_scr[pl.ds(2 * grp + 1, bt, stride=B_HEADS), :]
        gs = slice(grp * LANES, (grp + 1) * LANES)
        ob_ref[:, gs] = (jnp.where(low, even, odd) * _silu(zb_ref[:, gs])).astype(BF16)


def _sstep(sinks, q, k, v, gb, za, na_row, state, qb, kn, vn, zb, ck, cv):
    n = q.shape[0]
    bt = STEP_BT
    row = lambda w: pl.BlockSpec((bt, w), lambda i: (i, 0))
    st_spec = pl.BlockSpec((bt, A_HEADS, A_DK, A_DV), lambda i: (i, 0, 0, 0))
    c_spec = pl.BlockSpec((bt, WINDOW, LANES), lambda i: (i, 0, 0))
    return pl.pallas_call(
        _sstep_kernel,
        grid=(n // bt,),
        in_specs=[pl.BlockSpec(memory_space=pltpu.SMEM),
                  row(A_WIDTH), row(A_WIDTH), row(A_WIDTH), row(LANES), row(A_WIDTH),
                  pl.BlockSpec((1, A_DV), lambda i: (0, 0)), st_spec,
                  row(B_WIDTH), row(LANES), row(LANES), row(B_WIDTH), c_spec, c_spec],
        out_specs=[row(A_WIDTH), row(B_WIDTH), st_spec, c_spec, c_spec],
        out_shape=[jax.ShapeDtypeStruct((n, A_WIDTH), BF16),
                   jax.ShapeDtypeStruct((n, B_WIDTH), BF16),
                   jax.ShapeDtypeStruct((n, A_HEADS, A_DK, A_DV), F32),
                   jax.ShapeDtypeStruct((n, WINDOW, LANES), F32),
                   jax.ShapeDtypeStruct((n, WINDOW, LANES), F32)],
        scratch_shapes=[pltpu.VMEM((bt, A_WIDTH), F32), pltpu.VMEM((bt * B_HEADS, LANES), F32)],
        compiler_params=pltpu.CompilerParams(dimension_semantics=("arbitrary",),
                                             vmem_limit_bytes=VMEM_LIMIT),
        name="sstep",
    )(sinks, q, k, v, gb, za, na_row, state, qb, kn, vn, zb, ck, cv)


def _rope_tables(pos):
    half = B_HD // 2
    inv = 1.0 / (ROPE_THETA ** (np.arange(half, dtype=np.float64) / half))
    ang = np.asarray(pos, np.float64)[:, None] * inv[None, :]
    cos, sin = np.cos(ang), np.sin(ang)
    reps = LANES // B_HD
    return (jnp.asarray(np.tile(np.concatenate([cos, cos], -1), (1, reps)), F32),
            jnp.asarray(np.tile(np.concatenate([-sin, sin], -1), (1, reps)), F32))


def _pad_row(vec, offset):
    return jnp.zeros((1, LANES), F32).at[0, offset:offset + vec.shape[0]].set(vec.astype(F32))


def _layer(x_prompt, x_sample, state_conv, state_delta, cache_k, cache_v, c_prompt, c_sample,
           w_ada, b_ada, w_in, conv_w, a_log, dt_bias, norm_a, sinks, w_out, ln_g, ln_b):
    bsz, seq, _ = x_prompt.shape
    n_s = x_sample.shape[0]

    w_t = jnp.swapaxes(w_in, 0, 1)
    w_r = jnp.swapaxes(jnp.concatenate([
        w_t[0:OFF_A_BETA], w_t[OFF_B_Q:PROJ_COLS], w_t[OFF_A_BETA:OFF_B_Q],
        jnp.zeros((LANES - 2 * A_HEADS, D_MODEL), w_in.dtype)], axis=0).astype(BF16), 0, 1)
    w_o = w_out.astype(BF16)
    alog_row = _pad_row(a_log, A_HEADS)
    dt_row = _pad_row(dt_bias, A_HEADS)
    na_row = norm_a.reshape(1, A_DV)
    g_row = ln_g.reshape(1, D_MODEL)
    b_row = ln_b.reshape(1, D_MODEL)

    assert n_s % 8 == 0 and bsz <= 8
    c_all = jnp.concatenate([c_sample, c_prompt, jnp.zeros((8 - bsz, D_MODEL), F32)], axis=0)
    mod = _ada(c_all, w_ada, b_ada.reshape(1, 3 * D_MODEL))

    cos_p, sin_p = _rope_tables(np.arange(seq))
    (q, k, v, za, gb, qb, kb, kbr, vd0, vd1, zb, conv_p, kb_last, vb_last) = _proj(
        x_prompt, mod, n_s, w_r, conv_w, alog_row, dt_row, cos_p, sin_p)
    oa, delta_p = _delta(q, k, v, gb, za, na_row)
    ob = _swa(sinks, qb, kb, kbr, vd0, vd1, zb)
    y_p = _out(oa, ob, x_prompt, mod, n_s, True, w_o, g_row, b_row, OUT_TM)
    swa_k_p = kb_last.reshape(bsz, WINDOW, B_KV_HEADS, B_HD)
    swa_v_p = vb_last.reshape(bsz, WINDOW, B_KV_HEADS, B_HD)

    cos_s, sin_s = _rope_tables(np.array([PAST_LEN]))
    xs = x_sample.reshape(n_s, D_MODEL)
    cst = jnp.transpose(state_conv, (1, 0, 2))
    sq, sk, sv, sza, sgb, sqb, skn, svn, szb, ncs = _sproj(xs, mod, w_r, conv_w, cst, alog_row, dt_row,
                                                           cos_s, sin_s)
    soa, sob, delta_s, nck, ncv = _sstep(sinks, sq, sk, sv, sgb, sza, na_row, state_delta,
                                         sqb, skn, svn, szb,
                                         jnp.swapaxes(cache_k.reshape(n_s, WINDOW, LANES), 1, 2),
                                         jnp.swapaxes(cache_v.reshape(n_s, WINDOW, LANES), 1, 2))
    y_s = _out(soa[None], sob[None], xs[None], mod, 0, False, w_o, g_row, b_row, n_s)
    conv_s = jnp.transpose(ncs, (1, 0, 2))
    unpack = lambda c: jnp.swapaxes(c, 1, 2).reshape(n_s, WINDOW, B_KV_HEADS, B_HD)
    return (y_p, y_s.reshape(n_s, 1, D_MODEL), conv_p, delta_p, swa_k_p, swa_v_p,
            conv_s, delta_s, unpack(nck), unpack(ncv))


def kernel(x_prompt, x_sample, state_conv, state_delta, cache_swa_k, cache_swa_v, c_prompt, c_sample,
           w_ada, b_ada, w_in, conv_w, a_log, dt_bias, norm_a, sinks, w_out, ln_g, ln_b):
    assert w_ada.shape[0] == DEPTH == 1
    outs = _layer(x_prompt, x_sample, state_conv[0], state_delta[0], cache_swa_k[0], cache_swa_v[0],
                  c_prompt, c_sample, w_ada[0], b_ada[0], w_in[0], conv_w[0], a_log[0], dt_bias[0],
                  norm_a[0], sinks[0], w_out[0], ln_g[0], ln_b[0])
    y_p, y_s = outs[0], outs[1]
    return (y_p, y_s) + tuple(o[None] for o in outs[2:])
```

```python
import functools

import jax
import jax.numpy as jnp
import numpy as np
from jax import lax
from jax.experimental import pallas as pl
from jax.experimental.pallas import tpu as pltpu

F32 = jnp.float32
BF16 = jnp.bfloat16

D_MODEL = 1024
DEPTH = 1
PAST_LEN = 8192
A_HEADS = 4
A_DK = 128
A_DV = 128
A_WIDTH = A_HEADS * A_DV
A_QKV = 3 * A_WIDTH
CONV_W = 4
CHUNK = 64
B_HEADS = 8
B_KV_HEADS = 2
B_HD = 64
B_GROUP = B_HEADS // B_KV_HEADS
B_WIDTH = B_HEADS * B_HD
B_KV_WIDTH = B_KV_HEADS * B_HD
WINDOW = 128
ROPE_THETA = 10000.0
MIX_WIDTH = A_WIDTH + B_WIDTH
DEEPNORM_ALPHA = (2 * DEPTH) ** 0.25
LN_EPS = 1e-5
RMS_EPS = 1e-6
L2_EPS = 1e-6

OFF_A_Z = A_QKV
OFF_A_BETA = OFF_A_Z + A_WIDTH
OFF_A_DECAY = OFF_A_BETA + A_HEADS
OFF_B_Q = OFF_A_DECAY + A_HEADS
OFF_B_K = OFF_B_Q + B_WIDTH
OFF_B_V = OFF_B_K + B_KV_WIDTH
OFF_B_Z = OFF_B_V + B_KV_WIDTH
PROJ_COLS = OFF_B_Z + B_WIDTH

LANES = 128
C_QKV = 0
C_ZA = C_QKV + A_QKV
C_QB = C_ZA + A_WIDTH
C_KB = C_QB + B_WIDTH
C_VB = C_KB + B_KV_WIDTH
C_ZB = C_VB + B_KV_WIDTH
C_BD = C_ZB + B_WIDTH
W_COLS = C_BD + LANES

VMEM_LIMIT = 56 * 1024 * 1024

PROJ_TM = 512
DELTA_CT = 256
SWA_TQ = 512
OUT_TM = 1024
STEP_BT = 16


def _dot(a, b):
    return jnp.dot(a, b, preferred_element_type=F32)


def _dot_nt(a, b):
    return lax.dot_general(a, b, (((1,), (1,)), ((), ())), preferred_element_type=F32)


def _silu(x):
    return x * jax.nn.sigmoid(x)


def _softplus(x):
    return jnp.maximum(x, 0.0) + jnp.log1p(jnp.exp(-jnp.abs(x)))


def _lane(shape):
    return lax.broadcasted_iota(jnp.int32, shape, len(shape) - 1)


def _l2norm_heads(y, scale):
    outs = []
    for h in range(A_HEADS):
        xh = y[:, h * A_DK:(h + 1) * A_DK]
        ss = jnp.sum(xh * xh, axis=-1, keepdims=True)
        xn = xh * lax.rsqrt(ss + L2_EPS)
        outs.append(xn * scale if scale != 1.0 else xn)
    return jnp.concatenate(outs, axis=-1)


def _rotary_group(xg, cos, sin_signed):
    lane = _lane(xg.shape)
    swapped = jnp.where((lane % B_HD) < (B_HD // 2),
                        pltpu.roll(xg, LANES - B_HD // 2, axis=1),
                        pltpu.roll(xg, B_HD // 2, axis=1))
    return xg * cos + swapped * sin_signed


def _kv_layouts(kb, vb):
    low = _lane(kb.shape) < B_HD
    kbr = pltpu.roll(kb, B_HD, axis=1)
    vbr = pltpu.roll(vb, B_HD, axis=1)
    return kb, kbr, jnp.where(low, vb, vbr), jnp.where(low, vbr, vb)


def _gate_lanes(bd, alog_row, dt_row):
    lane = _lane(bd.shape)
    g = -jnp.exp(alog_row) * _softplus(bd + dt_row)
    return jnp.where(lane < A_HEADS, jax.nn.sigmoid(bd), g)


def _layer_norm(r, g, b):
    mu = jnp.mean(r, axis=-1, keepdims=True)
    d = r - mu
    var = jnp.mean(d * d, axis=-1, keepdims=True)
    return d * lax.rsqrt(var + LN_EPS) * g + b


def _ada_kernel(c_ref, w_ref, b_ref, o_ref):
    o_ref[...] = _dot(c_ref[...].astype(BF16), w_ref[...].astype(BF16)) + b_ref[...]


def _ada(c_all, w_ada, b_ada):
    rows = c_all.shape[0]
    tn = 768
    return pl.pallas_call(
        _ada_kernel,
        grid=(3 * D_MODEL // tn,),
        in_specs=[pl.BlockSpec((rows, D_MODEL), lambda j: (0, 0)),
                  pl.BlockSpec((D_MODEL, tn), lambda j: (0, j)),
                  pl.BlockSpec((1, tn), lambda j: (0, j))],
        out_specs=pl.BlockSpec((rows, tn), lambda j: (0, j)),
        out_shape=jax.ShapeDtypeStruct((rows, 3 * D_MODEL), F32),
        compiler_params=pltpu.CompilerParams(dimension_semantics=("arbitrary",),
                                             vmem_limit_bytes=VMEM_LIMIT),
        name="ada",
    )(c_all, w_ada, b_ada)


def _proj_kernel(x_ref, mod_ref, w_ref, cw_ref, alog_ref, dt_ref, cos_ref, sin_ref,
                 q_ref, k_ref, v_ref, za_ref, gb_ref, qb_ref, kb_ref, kbr_ref, vd0_ref, vd1_ref,
                 zb_ref, cst_ref, kbl_ref, vbl_ref, ubuf):
    tm = x_ref.shape[1]
    t = pl.program_id(1)

    @pl.when(t == 0)
    def _():
        ubuf[...] = jnp.zeros(ubuf.shape, F32)

    brow = pl.ds(pl.program_id(0), 1)
    shift = mod_ref[brow, 0:D_MODEL]
    scale = mod_ref[brow, D_MODEL:2 * D_MODEL]
    h = (x_ref[0] * (1.0 + scale) + shift).astype(BF16)

    sub = lax.broadcasted_iota(jnp.int32, (tm // 8, 8, A_WIDTH), 1)
    for gi, o_ref in enumerate((q_ref, k_ref, v_ref)):
        cs = slice(gi * A_WIDTH, (gi + 1) * A_WIDTH)
        u = _dot(h, w_ref[:, cs])
        groups = jnp.concatenate([ubuf[:, cs], u], axis=0).reshape(tm // 8 + 1, 8, A_WIDTH)
        acc = None
        for j in range(CONV_W - 1, 0, -1):
            rot = pltpu.roll(groups, j, axis=1)
            term = (jnp.where(sub < j, rot[:-1], rot[1:]).reshape(tm, A_WIDTH)
                    * cw_ref[CONV_W - 1 - j:CONV_W - j, cs])
            acc = term if acc is None else acc + term
        y = _silu(acc + u * cw_ref[CONV_W - 1:CONV_W, cs])
        if gi == 0:
            y = _l2norm_heads(y, A_DK ** -0.5)
        elif gi == 1:
            y = _l2norm_heads(y, 1.0)
        o_ref[0] = y
        ubuf[:, cs] = u[tm - 8:tm]
        cst_ref[0, :, cs] = u[tm - (CONV_W - 1):tm]

    za_ref[0] = _dot(h, w_ref[:, C_ZA:C_ZA + A_WIDTH])
    gb_ref[0] = _gate_lanes(_dot(h, w_ref[:, C_BD:C_BD + LANES]), alog_ref[...], dt_ref[...])
    cos = cos_ref[...]
    sin = sin_ref[...]
    uq = _dot(h, w_ref[:, C_QB:C_QB + B_WIDTH])
    for g in range(B_WIDTH // LANES):
        qb_ref[0, :, g * LANES:(g + 1) * LANES] = (
            _rotary_group(uq[:, g * LANES:(g + 1) * LANES], cos, sin) * (B_HD ** -0.5)).astype(BF16)
    ukv = _dot(h, w_ref[:, C_KB:C_KB + 2 * LANES])
    kb = _rotary_group(ukv[:, 0:LANES], cos, sin)
    vb = ukv[:, LANES:2 * LANES]
    for o_ref, val in zip((kb_ref, kbr_ref, vd0_ref, vd1_ref), _kv_layouts(kb, vb)):
        o_ref[0] = val.astype(BF16)
    zb_ref[0] = _dot(h, w_ref[:, C_ZB:C_ZB + B_WIDTH])

    @pl.when(t == pl.num_programs(1) - 1)
    def _():
        kbl_ref[0] = kb[tm - WINDOW:tm]
        vbl_ref[0] = vb[tm - WINDOW:tm]


def _proj(x, mod, mod_row0, w_r, conv_w, alog_row, dt_row, cos_t, sin_t):
    bsz, t, _ = x.shape
    tm = PROJ_TM
    row = lambda w: pl.BlockSpec((1, tm, w), lambda b, i: (b, i, 0))
    const2 = lambda s: pl.BlockSpec(s, lambda b, i: (0, 0))
    per_b = lambda r, w: pl.BlockSpec((1, r, w), lambda b, i: (b, 0, 0))
    wide = lambda w, dt=F32: jax.ShapeDtypeStruct((bsz, t, w), dt)
    return pl.pallas_call(
        _proj_kernel,
        grid=(bsz, t // tm),
        in_specs=[row(D_MODEL),
                  pl.BlockSpec((8, 3 * D_MODEL), lambda b, i: (mod_row0 // 8, 0)),
                  const2((D_MODEL, W_COLS)),
                  const2((CONV_W, A_QKV)),
                  const2((1, LANES)), const2((1, LANES)),
                  pl.BlockSpec((tm, LANES), lambda b, i: (i, 0)),
                  pl.BlockSpec((tm, LANES), lambda b, i: (i, 0))],
        out_specs=[row(A_WIDTH), row(A_WIDTH), row(A_WIDTH), row(A_WIDTH), row(LANES),
                   row(B_WIDTH), row(LANES), row(LANES), row(LANES), row(LANES), row(B_WIDTH),
                   per_b(CONV_W - 1, A_QKV), per_b(WINDOW, LANES), per_b(WINDOW, LANES)],
        out_shape=[wide(A_WIDTH), wide(A_WIDTH), wide(A_WIDTH), wide(A_WIDTH), wide(LANES),
                   wide(B_WIDTH, BF16), wide(LANES, BF16), wide(LANES, BF16), wide(LANES, BF16),
                   wide(LANES, BF16), wide(B_WIDTH),
                   jax.ShapeDtypeStruct((bsz, CONV_W - 1, A_QKV), F32),
                   jax.ShapeDtypeStruct((bsz, WINDOW, LANES), F32),
                   jax.ShapeDtypeStruct((bsz, WINDOW, LANES), F32)],
        scratch_shapes=[pltpu.VMEM((8, A_QKV), F32)],
        compiler_params=pltpu.CompilerParams(dimension_semantics=("arbitrary", "arbitrary"),
                                             vmem_limit_bytes=VMEM_LIMIT),
        name="proj",
    )(x, mod, w_r, conv_w, alog_row, dt_row, cos_t, sin_t)


def _delta_kernel(q_ref, k_ref, v_ref, gb_ref, za_ref, na_ref, oa_ref, st_ref,
                  s_scr, wq_s, ut_s, akd_s, gl_s):
    bsz, ct = q_ref.shape[0], q_ref.shape[1]
    nch = ct // CHUNK
    t = pl.program_id(0)
    wslot = t % 2
    rslot = 1 - wslot

    @pl.when(t == 0)
    def _():
        s_scr[...] = jnp.zeros(s_scr.shape, F32)
        wq_s[...] = jnp.zeros(wq_s.shape, BF16)
        ut_s[...] = jnp.zeros(ut_s.shape, F32)
        akd_s[...] = jnp.zeros(akd_s.shape, BF16)
        gl_s[...] = jnp.zeros(gl_s.shape, F32)

    units = [(b, c, h) for b in range(bsz) for c in range(nch) for h in range(A_HEADS)]
    uid = {u_: i for i, u_ in enumerate(units)}
    rows = lambda c: slice(c * CHUNK, (c + 1) * CHUNK)
    lanes = lambda h: slice(h * A_DK, (h + 1) * A_DK)
    ti = lax.broadcasted_iota(jnp.int32, (CHUNK, CHUNK), 0)
    ii = lax.broadcasted_iota(jnp.int32, (CHUNK, CHUNK), 1)
    na = na_ref[...]

    s_cur = {(b, h): s_scr[b * A_HEADS + h] for b in range(bsz) for h in range(A_HEADS)}
    ws, uu = {}, {}

    def rec_ws(c):
        for b in range(bsz):
            for h in range(A_HEADS):
                i = uid[b, c, h]
                ws[b, h] = _dot(wq_s[rslot, i], s_cur[b, h].astype(BF16))
                uu[b, h] = (ut_s[rslot, i] - ws[b, h][:CHUNK]).astype(BF16)

    def rec_ou(c):
        for b in range(bsz):
            for h in range(A_HEADS):
                i = uid[b, c, h]
                ou = _dot(akd_s[rslot, i], uu[b, h])
                o = ws[b, h][CHUNK:] + ou[:CHUNK]
                s_cur[b, h] = gl_s[rslot, i] * s_cur[b, h] + ou[CHUNK:]
                on = o * lax.rsqrt(jnp.mean(o * o, axis=-1, keepdims=True) + RMS_EPS) * na
                oa_ref[b, rows(c), lanes(h)] = (on * _silu(za_ref[b, rows(c), lanes(h)])).astype(BF16)

    rec_stages = []
    for c in range(nch):
        rec_stages += [lambda c=c: rec_ws(c), lambda c=c: rec_ou(c)]

    def run_rec(n_left_after):
        while rec_stages and len(rec_stages) > n_left_after:
            rec_stages.pop(0)()

    beta, g_col, g_last, eg, dec_incl, nmat, aqk = {}, {}, {}, {}, {}, {}, {}
    for b in range(bsz):
        gbv = gb_ref[b]
        rin = lax.broadcasted_iota(jnp.int32, gbv.shape, 0) % CHUNK
        gcs = gbv
        s = 1
        while s < CHUNK:
            gcs = gcs + jnp.where(rin >= s, pltpu.roll(gcs, s, axis=0), 0.0)
            s *= 2
        gcs_t = gcs.T
        for c in range(nch):
            for h in range(A_HEADS):
                u_ = (b, c, h)
                r0 = c * CHUNK
                beta[u_] = jnp.broadcast_to(gbv[rows(c), h:h + 1], (CHUNK, A_DK))
                g_col[u_] = jnp.broadcast_to(gcs[rows(c), A_HEADS + h:A_HEADS + h + 1], (CHUNK, A_DK))
                g_last[u_] = gcs[r0 + CHUNK - 1:r0 + CHUNK, A_HEADS + h:A_HEADS + h + 1]
                g_row = gcs_t[A_HEADS + h:A_HEADS + h + 1, rows(c)]
                dec_incl[u_] = jnp.exp(jnp.where(ti >= ii, g_col[u_][:, :CHUNK] - g_row, -jnp.inf))
                eg[u_] = jnp.exp(g_col[u_])

    n_prep = 8
    for (b, c, h) in units:
        kcb = k_ref[b, rows(c), lanes(h)].astype(BF16)
        qcb = q_ref[b, rows(c), lanes(h)].astype(BF16)
        kq = _dot_nt(jnp.concatenate([kcb, qcb], axis=0), kcb)
        nmat[b, c, h] = -(beta[b, c, h][:, :CHUNK] * kq[:CHUNK] * jnp.where(ti > ii, dec_incl[b, c, h], 0.0))
        aqk[b, c, h] = kq[CHUNK:] * dec_incl[b, c, h]
    run_rec(n_prep - 1)

    rsum = dict(nmat)
    pwb = {u_: nmat[u_].astype(BF16) for u_ in units}
    pw = {u_: _dot(pwb[u_], pwb[u_]) for u_ in units}
    run_rec(n_prep - 2)
    for step in range(1, 6):
        last = step == 5
        pwb = {u_: pw[u_].astype(BF16) for u_ in units}
        rp = {}
        for u_ in units:
            rb = rsum[u_].astype(BF16)
            rp[u_] = _dot(rb if last else jnp.concatenate([rb, pwb[u_]], axis=0), pwb[u_])
        for u_ in units:
            rsum[u_] = rsum[u_] + pw[u_] + rp[u_][:CHUNK]
            if not last:
                pw[u_] = rp[u_][CHUNK:]
        run_rec(n_prep - 2 - step)

    for (b, c, h) in units:
        u_ = (b, c, h)
        i = uid[u_]
        kc = k_ref[b, rows(c), lanes(h)]
        rhs = jnp.concatenate([(beta[u_] * eg[u_]) * kc, beta[u_] * v_ref[b, rows(c), lanes(h)]],
                              axis=-1)
        sol = rhs + _dot(rsum[u_].astype(BF16), rhs.astype(BF16))
        wq_s[wslot, i] = jnp.concatenate([sol[:, :A_DK], eg[u_] * q_ref[b, rows(c), lanes(h)]],
                                         axis=0).astype(BF16)
        ut_s[wslot, i] = sol[:, A_DK:]
        kd = jnp.exp(g_last[u_] - g_col[u_]) * kc
        akd_s[wslot, i] = jnp.concatenate([aqk[u_], kd.T], axis=0).astype(BF16)
        gl_s[wslot, i] = jnp.broadcast_to(jnp.exp(g_last[u_]), (1, A_DV))
    run_rec(0)

    for b in range(bsz):
        for h in range(A_HEADS):
            s_scr[b * A_HEADS + h] = s_cur[b, h]

    @pl.when(t == pl.num_programs(0) - 1)
    def _():
        for b in range(bsz):
            for h in range(A_HEADS):
                st_ref[b, h] = s_cur[b, h]


def _delta(q, k, v, gb, za, na_row):
    bsz, t, _ = q.shape
    ct = DELTA_CT
    nt = t // ct
    n_units = bsz * (ct // CHUNK) * A_HEADS
    prep = lambda w: pl.BlockSpec((bsz, ct, w), lambda i: (0, jnp.minimum(i, nt - 1), 0))
    rec = lambda w: pl.BlockSpec((bsz, ct, w), lambda i: (0, jnp.maximum(i - 1, 0), 0))
    return pl.pallas_call(
        _delta_kernel,
        grid=(nt + 1,),
        in_specs=[prep(A_WIDTH), prep(A_WIDTH), prep(A_WIDTH), prep(LANES), rec(A_WIDTH),
                  pl.BlockSpec((1, A_DV), lambda i: (0, 0))],
        out_specs=[rec(A_WIDTH),
                   pl.BlockSpec((bsz, A_HEADS, A_DK, A_DV), lambda i: (0, 0, 0, 0))],
        out_shape=[jax.ShapeDtypeStruct((bsz, t, A_WIDTH), BF16),
                   jax.ShapeDtypeStruct((bsz, A_HEADS, A_DK, A_DV), F32)],
        scratch_shapes=[pltpu.VMEM((bsz * A_HEADS, A_DK, A_DV), F32),
                        pltpu.VMEM((2, n_units, 2 * CHUNK, A_DK), BF16),
                        pltpu.VMEM((2, n_units, CHUNK, A_DV), F32),
                        pltpu.VMEM((2, n_units, 3 * CHUNK, CHUNK), BF16),
                        pltpu.VMEM((2, n_units, 1, A_DV), F32)],
        compiler_params=pltpu.CompilerParams(dimension_semantics=("arbitrary",),
                                             vmem_limit_bytes=VMEM_LIMIT),
        name="delta",
    )(q, k, v, gb, za, na_row)


def _swa_kernel(sink_ref, qb_ref, kc_ref, kp_ref, krc_ref, krp_ref, v0c_ref, v0p_ref, v1c_ref, v1p_ref,
                zb_ref, ob_ref):
    n = pl.program_id(1)
    tq = qb_ref.shape[1]
    blk = WINDOW
    kx = (jnp.concatenate([kp_ref[0], kc_ref[0]], axis=0), jnp.concatenate([krp_ref[0], krc_ref[0]], axis=0))
    vd = (jnp.concatenate([v0p_ref[0], v0c_ref[0]], axis=0), jnp.concatenate([v1p_ref[0], v1c_ref[0]], axis=0))

    a = lax.broadcasted_iota(jnp.int32, (2 * blk, 2 * blk), 0) % blk
    j = lax.broadcasted_iota(jnp.int32, (2 * blk, 2 * blk), 1)
    rel = a + blk - j
    band = (rel >= 0) & (rel <= WINDOW)
    band_first = band & ((n > 0) | (j >= blk))
    top = lax.broadcasted_iota(jnp.int32, (2 * blk, 1), 0) < blk
    low = _lane((blk, LANES)) < B_HD
    zero = jnp.zeros((blk, LANES), BF16)

    for i in range(tq // blk):
        valid = band_first if i == 0 else band
        qrows = slice(i * blk, (i + 1) * blk)
        krows = slice(i * blk, (i + 2) * blk)
        outs = {}
        for kh in range(B_KV_HEADS):
            for half in range(2):
                qs = []
                for g in range(2):
                    grp = kh * 2 + g
                    xg = qb_ref[0, qrows, grp * LANES:(grp + 1) * LANES]
                    qs.append(jnp.where(low if half == 0 else jnp.logical_not(low), xg, zero))
                qz = jnp.concatenate(qs, axis=0)
                sc = jnp.where(valid, _dot_nt(qz, kx[0 if kh == half else 1][krows]), -jnp.inf)
                sink = jnp.where(top, sink_ref[kh * B_GROUP + half], sink_ref[kh * B_GROUP + half + 2])
                m = jnp.maximum(jnp.max(sc, axis=-1, keepdims=True), sink)
                p = jnp.exp(sc - m)
                den = jnp.sum(p, axis=-1, keepdims=True) + jnp.exp(sink - m)
                outs[(kh, half)] = _dot(p.astype(BF16), vd[kh][krows]) / den
        for grp in range(B_WIDTH // LANES):
            kh, g = grp // 2, grp % 2
            og = jnp.where(low, outs[(kh, 0)][g * blk:(g + 1) * blk], outs[(kh, 1)][g * blk:(g + 1) * blk])
            gs = slice(grp * LANES, (grp + 1) * LANES)
            ob_ref[0, qrows, gs] = (og * _silu(zb_ref[0, qrows, gs])).astype(BF16)


def _swa(sinks, qb, kb, kbr, vd0, vd1, zb):
    bsz, t, _ = qb.shape
    tq = SWA_TQ
    per = tq // WINDOW
    cur = lambda w: pl.BlockSpec((1, tq, w), lambda b, i: (b, i, 0))
    prev = pl.BlockSpec((1, WINDOW, LANES), lambda b, i: (b, jnp.maximum(i * per - 1, 0), 0))
    return pl.pallas_call(
        _swa_kernel,
        grid=(bsz, t // tq),
        in_specs=[pl.BlockSpec(memory_space=pltpu.SMEM), cur(B_WIDTH),
                  cur(LANES), prev, cur(LANES), prev, cur(LANES), prev, cur(LANES), prev,
                  cur(B_WIDTH)],
        out_specs=cur(B_WIDTH),
        out_shape=jax.ShapeDtypeStruct((bsz, t, B_WIDTH), BF16),
        compiler_params=pltpu.CompilerParams(dimension_semantics=("arbitrary", "arbitrary"),
                                             vmem_limit_bytes=VMEM_LIMIT),
        name="swa",
    )(sinks, qb, kb, kb, kbr, kbr, vd0, vd0, vd1, vd1, zb)


def _out_kernel(oa_ref, ob_ref, x_ref, gate_ref, w_ref, g_ref, b_ref, y_ref, *, gate_per_batch):
    mix = _dot(oa_ref[0], w_ref[0:A_WIDTH, :]) + _dot(ob_ref[0], w_ref[A_WIDTH:MIX_WIDTH, :])
    gate = gate_ref[pl.ds(pl.program_id(0), 1), :] if gate_per_batch else gate_ref[...]
    r = DEEPNORM_ALPHA * x_ref[0] + (1.0 + gate) * mix
    y_ref[0] = _layer_norm(r, g_ref[...], b_ref[...])


def _out(oa, ob, x, mod, mod_row0, gate_per_batch, w_out, ln_g, ln_b, tm):
    bsz, t, _ = x.shape
    grows = 8 if gate_per_batch else tm
    row = lambda w: pl.BlockSpec((1, tm, w), lambda b, i: (b, i, 0))
    const2 = lambda s: pl.BlockSpec(s, lambda b, i: (0, 0))
    return pl.pallas_call(
        functools.partial(_out_kernel, gate_per_batch=gate_per_batch),
        grid=(bsz, t // tm),
        in_specs=[row(A_WIDTH), row(B_WIDTH), row(D_MODEL),
                  pl.BlockSpec((grows, D_MODEL), lambda b, i: (mod_row0 // grows, 2)),
                  const2((MIX_WIDTH, D_MODEL)), const2((1, D_MODEL)), const2((1, D_MODEL))],
        out_specs=row(D_MODEL),
        out_shape=jax.ShapeDtypeStruct((bsz, t, D_MODEL), F32),
        compiler_params=pltpu.CompilerParams(dimension_semantics=("arbitrary", "arbitrary"),
                                             vmem_limit_bytes=VMEM_LIMIT),
        name="out",
    )(oa, ob, x, mod, w_out, ln_g, ln_b)


def _sproj_kernel(x_ref, mod_ref, w_ref, cw_ref, cst_ref, alog_ref, dt_ref, cos_ref, sin_ref,
                  q_ref, k_ref, v_ref, za_ref, gb_ref, qb_ref, kb_ref, vb_ref, zb_ref, ncs_ref):
    shift = mod_ref[:, 0:D_MODEL]
    scale = mod_ref[:, D_MODEL:2 * D_MODEL]
    h = (x_ref[...] * (1.0 + scale) + shift).astype(BF16)

    for gi, o_ref in enumerate((q_ref, k_ref, v_ref)):
        c0 = gi * A_WIDTH
        cs = slice(c0, c0 + A_WIDTH)
        u = _dot(h, w_ref[:, cs])
        acc = cst_ref[0, :, cs] * cw_ref[0:1, cs]
        acc = acc + cst_ref[1, :, cs] * cw_ref[1:2, cs]
        acc = acc + cst_ref[2, :, cs] * cw_ref[2:3, cs]
        acc = acc + u * cw_ref[3:4, cs]
        y = _silu(acc)
        if gi == 0:
            y = _l2norm_heads(y, A_DK ** -0.5)
        elif gi == 1:
            y = _l2norm_heads(y, 1.0)
        o_ref[...] = y
        ncs_ref[0, :, cs] = cst_ref[1, :, cs]
        ncs_ref[1, :, cs] = cst_ref[2, :, cs]
        ncs_ref[2, :, cs] = u

    za_ref[...] = _dot(h, w_ref[:, C_ZA:C_ZA + A_WIDTH])
    gb_ref[...] = _gate_lanes(_dot(h, w_ref[:, C_BD:C_BD + LANES]), alog_ref[...], dt_ref[...])

    cos = cos_ref[...]
    sin = sin_ref[...]
    uq = _dot(h, w_ref[:, C_QB:C_QB + B_WIDTH])
    for g in range(B_WIDTH // LANES):
        qb_ref[:, g * LANES:(g + 1) * LANES] = (
            _rotary_group(uq[:, g * LANES:(g + 1) * LANES], cos, sin) * (B_HD ** -0.5))
    kb_ref[...] = _rotary_group(_dot(h, w_ref[:, C_KB:C_KB + LANES]), cos, sin)
    vb_ref[...] = _dot(h, w_ref[:, C_VB:C_VB + LANES])
    zb_ref[...] = _dot(h, w_ref[:, C_ZB:C_ZB + B_WIDTH])


def _sproj(x, mod_s, w_r, conv_w, cst, alog_row, dt_row, cos_row, sin_row):
    n = x.shape[0]
    full = lambda s: pl.BlockSpec(s, lambda i: (0,) * len(s))
    wide = lambda w: jax.ShapeDtypeStruct((n, w), F32)
    return pl.pallas_call(
        _sproj_kernel,
        grid=(1,),
        in_specs=[full((n, D_MODEL)), pl.BlockSpec((n, 3 * D_MODEL), lambda i: (0, 0)),
                  full((D_MODEL, W_COLS)),
                  full((CONV_W, A_QKV)), full((CONV_W - 1, n, A_QKV)),
                  full((1, LANES)), full((1, LANES)), full((1, LANES)), full((1, LANES))],
        out_specs=[full((n, A_WIDTH)), full((n, A_WIDTH)), full((n, A_WIDTH)), full((n, A_WIDTH)),
                   full((n, LANES)), full((n, B_WIDTH)), full((n, LANES)), full((n, LANES)),
                   full((n, B_WIDTH)), full((CONV_W - 1, n, A_QKV))],
        out_shape=[wide(A_WIDTH), wide(A_WIDTH), wide(A_WIDTH), wide(A_WIDTH), wide(LANES),
                   wide(B_WIDTH), wide(LANES), wide(LANES), wide(B_WIDTH),
                   jax.ShapeDtypeStruct((CONV_W - 1, n, A_QKV), F32)],
        compiler_params=pltpu.CompilerParams(dimension_semantics=("arbitrary",),
                                             vmem_limit_bytes=VMEM_LIMIT),
        name="sproj",
    )(x, mod_s, w_r, conv_w, cst, alog_row, dt_row, cos_row, sin_row)


def _sstep_kernel(sink_ref, q_ref, k_ref, v_ref, gb_ref, za_ref, na_ref, st_ref,
                  qb_ref, kn_ref, vn_ref, zb_ref, ck_ref, cv_ref,
                  oa_ref, ob_ref, nst_ref, nck_ref, ncv_ref,
                  o_scr, ob_scr):
    bt = q_ref.shape[0]
    gbv = gb_ref[...]

    for h in range(A_HEADS):
        hs = slice(h * A_DK, (h + 1) * A_DK)
        q_t = q_ref[:, hs].T
        k_t = k_ref[:, hs].T
        for bb in range(bt):
            eg = jnp.exp(gbv[bb:bb + 1, A_HEADS + h:A_HEADS + h + 1])
            beta = gbv[bb:bb + 1, h:h + 1]
            kcol = k_t[:, bb:bb + 1]
            qcol = q_t[:, bb:bb + 1]
            s1 = eg * st_ref[bb, h]
            pred = jnp.sum(kcol * s1, axis=0, keepdims=True)
            upd = beta * (v_ref[bb:bb + 1, hs] - pred)
            s2 = s1 + kcol * upd
            nst_ref[bb, h] = s2
            o_scr[bb:bb + 1, hs] = jnp.sum(qcol * s2, axis=0, keepdims=True)
    na = na_ref[...]
    for h in range(A_HEADS):
        hs = slice(h * A_DK, (h + 1) * A_DK)
        o = o_scr[:, hs]
        on = o * lax.rsqrt(jnp.mean(o * o, axis=-1, keepdims=True) + RMS_EPS) * na
        oa_ref[:, hs] = (on * _silu(za_ref[:, hs])).astype(BF16)

    row8 = lax.broadcasted_iota(jnp.int32, (B_HEADS, LANES), 0)
    lane8 = _lane((B_HEADS, LANES))
    own_half = (lane8 >= B_HD) == (row8 >= B_GROUP)
    rcol = lax.broadcasted_iota(jnp.int32, (B_HEADS, 1), 0)
    sink = jnp.zeros((B_HEADS, 1), F32)
    for r in range(B_HEADS):
        sink = jnp.where(rcol == r, sink_ref[r], sink)
    qv = qb_ref[...]
    qv_r = jnp.concatenate([pltpu.roll(qv[:, g * LANES:(g + 1) * LANES], B_HD, axis=1)
                            for g in range(B_WIDTH // LANES)], axis=-1)
    kn_t = kn_ref[...].T
    vn_t = vn_ref[...].T
    newest = _lane((LANES, WINDOW)) == WINDOW - 1
    qzs, scs = [], []
    for bb in range(bt):
        qz = jnp.zeros((B_HEADS, LANES), F32)
        for r in range(B_HEADS):
            grp, half, kh = r // 2, r % 2, r // B_GROUP
            src = qv if half == kh else qv_r
            qz = jnp.where(row8 == r, src[bb:bb + 1, grp * LANES:(grp + 1) * LANES], qz)
        qzs.append(jnp.where(own_half, qz, 0.0))
    for bb in range(bt):
        scs.append(_dot(qzs[bb], ck_ref[bb]))
    ps, pnews, dens = [], [], []
    for bb in range(bt):
        sc_new = jnp.sum(qzs[bb] * kn_ref[bb:bb + 1, :], axis=-1, keepdims=True)
        m = jnp.maximum(jnp.maximum(jnp.max(scs[bb], axis=-1, keepdims=True), sc_new), sink)
        p = jnp.exp(scs[bb] - m)
        p_new = jnp.exp(sc_new - m)
        ps.append(p)
        pnews.append(p_new)
        dens.append(jnp.sum(p, axis=-1, keepdims=True) + p_new + jnp.exp(sink - m))
    pvs = [_dot_nt(ps[bb], cv_ref[bb]) for bb in range(bt)]
    for bb in range(bt):
        o = (pvs[bb] + pnews[bb] * vn_ref[bb:bb + 1, :]) / dens[bb]
        o = jnp.where(own_half, o, 0.0)
        ob_scr[bb * B_HEADS:(bb + 1) * B_HEADS, :] = o + pltpu.roll(o, B_HD, axis=1)
    for bb in range(bt):
        nck_ref[bb] = jnp.where(newest, kn_t[:, bb:bb + 1], pltpu.roll(ck_ref[bb], WINDOW - 1, axis=1))
        ncv_ref[bb] = jnp.where(newest, vn_t[:, bb:bb + 1], pltpu.roll(cv_ref[bb], WINDOW - 1, axis=1))
    low = _lane((bt, LANES)) < B_HD
    for grp in range(B_WIDTH // LANES):
        even = ob_scr[pl.ds(2 * grp, bt, stride=B_HEADS), :]
        odd = ob_scr[pl.ds(2 * grp + 1, bt, stride=B_HEADS), :]
        gs = slice(grp * LANES, (grp + 1) * LANES)
        ob_ref[:, gs] = (jnp.where(low, even, odd) * _silu(zb_ref[:, gs])).astype(BF16)


def _sstep(sinks, q, k, v, gb, za, na_row, state, qb, kn, vn, zb, ck, cv):
    n = q.shape[0]
    bt = STEP_BT
    row = lambda w: pl.BlockSpec((bt, w), lambda i: (i, 0))
    st_spec = pl.BlockSpec((bt, A_HEADS, A_DK, A_DV), lambda i: (i, 0, 0, 0))
    c_spec = pl.BlockSpec((bt, WINDOW, LANES), lambda i: (i, 0, 0))
    return pl.pallas_call(
        _sstep_kernel,
        grid=(n // bt,),
        in_specs=[pl.BlockSpec(memory_space=pltpu.SMEM),
                  row(A_WIDTH), row(A_WIDTH), row(A_WIDTH), row(LANES), row(A_WIDTH),
                  pl.BlockSpec((1, A_DV), lambda i: (0, 0)), st_spec,
                  row(B_WIDTH), row(LANES), row(LANES), row(B_WIDTH), c_spec, c_spec],
        out_specs=[row(A_WIDTH), row(B_WIDTH), st_spec, c_spec, c_spec],
        out_shape=[jax.ShapeDtypeStruct((n, A_WIDTH), BF16),
                   jax.ShapeDtypeStruct((n, B_WIDTH), BF16),
                   jax.ShapeDtypeStruct((n, A_HEADS, A_DK, A_DV), F32),
                   jax.ShapeDtypeStruct((n, WINDOW, LANES), F32),
                   jax.ShapeDtypeStruct((n, WINDOW, LANES), F32)],
        scratch_shapes=[pltpu.VMEM((bt, A_WIDTH), F32), pltpu.VMEM((bt * B_HEADS, LANES), F32)],
        compiler_params=pltpu.CompilerParams(dimension_semantics=("arbitrary",),
                                             vmem_limit_bytes=VMEM_LIMIT),
        name="sstep",
    )(sinks, q, k, v, gb, za, na_row, state, qb, kn, vn, zb, ck, cv)


def _rope_tables(pos):
    half = B_HD // 2
    inv = 1.0 / (ROPE_THETA ** (np.arange(half, dtype=np.float64) / half))
    ang = np.asarray(pos, np.float64)[:, None] * inv[None, :]
    cos, sin = np.cos(ang), np.sin(ang)
    reps = LANES // B_HD
    return (jnp.asarray(np.tile(np.concatenate([cos, cos], -1), (1, reps)), F32),
            jnp.asarray(np.tile(np.concatenate([-sin, sin], -1), (1, reps)), F32))


def _pad_row(vec, offset):
    return jnp.zeros((1, LANES), F32).at[0, offset:offset + vec.shape[0]].set(vec.astype(F32))


def _layer(x_prompt, x_sample, state_conv, state_delta, cache_k, cache_v, c_prompt, c_sample,
           w_ada, b_ada, w_in, conv_w, a_log, dt_bias, norm_a, sinks, w_out, ln_g, ln_b):
    bsz, seq, _ = x_prompt.shape
    n_s = x_sample.shape[0]

    w_t = jnp.swapaxes(w_in, 0, 1)
    w_r = jnp.swapaxes(jnp.concatenate([
        w_t[0:OFF_A_BETA], w_t[OFF_B_Q:PROJ_COLS], w_t[OFF_A_BETA:OFF_B_Q],
        jnp.zeros((LANES - 2 * A_HEADS, D_MODEL), w_in.dtype)], axis=0).astype(BF16), 0, 1)
    w_o = w_out.astype(BF16)
    alog_row = _pad_row(a_log, A_HEADS)
    dt_row = _pad_row(dt_bias, A_HEADS)
    na_row = norm_a.reshape(1, A_DV)
    g_row = ln_g.reshape(1, D_MODEL)
    b_row = ln_b.reshape(1, D_MODEL)

    assert n_s % 8 == 0 and bsz <= 8
    c_all = jnp.concatenate([c_sample, c_prompt, jnp.zeros((8 - bsz, D_MODEL), F32)], axis=0)
    mod = _ada(c_all, w_ada, b_ada.reshape(1, 3 * D_MODEL))

    cos_p, sin_p = _rope_tables(np.arange(seq))
    (q, k, v, za, gb, qb, kb, kbr, vd0, vd1, zb, conv_p, kb_last, vb_last) = _proj(
        x_prompt, mod, n_s, w_r, conv_w, alog_row, dt_row, cos_p, sin_p)
    oa, delta_p = _delta(q, k, v, gb, za, na_row)
    ob = _swa(sinks, qb, kb, kbr, vd0, vd1, zb)
    y_p = _out(oa, ob, x_prompt, mod, n_s, True, w_o, g_row, b_row, OUT_TM)
    swa_k_p = kb_last.reshape(bsz, WINDOW, B_KV_HEADS, B_HD)
    swa_v_p = vb_last.reshape(bsz, WINDOW, B_KV_HEADS, B_HD)

    cos_s, sin_s = _rope_tables(np.array([PAST_LEN]))
    xs = x_sample.reshape(n_s, D_MODEL)
    cst = jnp.transpose(state_conv, (1, 0, 2))
    sq, sk, sv, sza, sgb, sqb, skn, svn, szb, ncs = _sproj(xs, mod, w_r, conv_w, cst, alog_row, dt_row,
                                                           cos_s, sin_s)
    soa, sob, delta_s, nck, ncv = _sstep(sinks, sq, sk, sv, sgb, sza, na_row, state_delta,
                                         sqb, skn, svn, szb,
                                         jnp.swapaxes(cache_k.reshape(n_s, WINDOW, LANES), 1, 2),
                                         jnp.swapaxes(cache_v.reshape(n_s, WINDOW, LANES), 1, 2))
    y_s = _out(soa[None], sob[None], xs[None], mod, 0, False, w_o, g_row, b_row, n_s)
    conv_s = jnp.transpose(ncs, (1, 0, 2))
    unpack = lambda c: jnp.swapaxes(c, 1, 2).reshape(n_s, WINDOW, B_KV_HEADS, B_HD)
    return (y_p, y_s.reshape(n_s, 1, D_MODEL), conv_p, delta_p, swa_k_p, swa_v_p,
            conv_s, delta_s, unpack(nck), unpack(ncv))


def kernel(x_prompt, x_sample, state_conv, state_delta, cache_swa_k, cache_swa_v, c_prompt, c_sample,
           w_ada, b_ada, w_in, conv_w, a_log, dt_bias, norm_a, sinks, w_out, ln_g, ln_b):
    assert w_ada.shape[0] == DEPTH == 1
    outs = _layer(x_prompt, x_sample, state_conv[0], state_delta[0], cache_swa_k[0], cache_swa_v[0],
                  c_prompt, c_sample, w_ada[0], b_ada[0], w_in[0], conv_w[0], a_log[0], dt_bias[0],
                  norm_a[0], sinks[0], w_out[0], ln_g[0], ln_b[0])
    y_p, y_s = outs[0], outs[1]
    return (y_p, y_s) + tuple(o[None] for o in outs[2:])
```

```python
import functools

import jax
import jax.numpy as jnp
import numpy as np
from jax import lax
from jax.experimental import pallas as pl
from jax.experimental.pallas import tpu as pltpu

F32 = jnp.float32
BF16 = jnp.bfloat16

D_MODEL = 1024
DEPTH = 1
PAST_LEN = 8192
A_HEADS = 4
A_DK = 128
A_DV = 128
A_WIDTH = A_HEADS * A_DV
A_QKV = 3 * A_WIDTH
CONV_W = 4
CHUNK = 64
B_HEADS = 8
B_KV_HEADS = 2
B_HD = 64
B_GROUP = B_HEADS // B_KV_HEADS
B_WIDTH = B_HEADS * B_HD
B_KV_WIDTH = B_KV_HEADS * B_HD
WINDOW = 128
ROPE_THETA = 10000.0
MIX_WIDTH = A_WIDTH + B_WIDTH
DEEPNORM_ALPHA = (2 * DEPTH) ** 0.25
LOG2E = 1.4426950408889634
LN_EPS = 1e-5
RMS_EPS = 1e-6
L2_EPS = 1e-6

OFF_A_Z = A_QKV
OFF_A_BETA = OFF_A_Z + A_WIDTH
OFF_A_DECAY = OFF_A_BETA + A_HEADS
OFF_B_Q = OFF_A_DECAY + A_HEADS
OFF_B_K = OFF_B_Q + B_WIDTH
OFF_B_V = OFF_B_K + B_KV_WIDTH
OFF_B_Z = OFF_B_V + B_KV_WIDTH
PROJ_COLS = OFF_B_Z + B_WIDTH

LANES = 128
C_QKV = 0
C_ZA = C_QKV + A_QKV
C_QB = C_ZA + A_WIDTH
C_KB = C_QB + B_WIDTH
C_VB = C_KB + B_KV_WIDTH
C_ZB = C_VB + B_KV_WIDTH
C_BD = C_ZB + B_WIDTH
W_COLS = C_BD + LANES

VMEM_LIMIT = 56 * 1024 * 1024

PROJ_TM = 512
DELTA_CT = 256
SWA_TQ = 512
SWA_WAVE = 1
OUT_TM = 1024
STEP_BT = 16


def _dot(a, b):
    return jnp.dot(a, b, preferred_element_type=F32)


def _dot_nt(a, b):
    return lax.dot_general(a, b, (((1,), (1,)), ((), ())), preferred_element_type=F32)


def _silu(x):
    return x * jax.nn.sigmoid(x)


def _softplus(x):
    return jnp.maximum(x, 0.0) + jnp.log1p(jnp.exp(-jnp.abs(x)))


def _lane(shape):
    return lax.broadcasted_iota(jnp.int32, shape, len(shape) - 1)


def _l2norm_heads(y, scale):
    outs = []
    for h in range(A_HEADS):
        xh = y[:, h * A_DK:(h + 1) * A_DK]
        ss = jnp.sum(xh * xh, axis=-1, keepdims=True)
        xn = xh * lax.rsqrt(ss + L2_EPS)
        outs.append(xn * scale if scale != 1.0 else xn)
    return jnp.concatenate(outs, axis=-1)


def _rotary_group(xg, cos, sin_signed):
    lane = _lane(xg.shape)
    swapped = jnp.where((lane % B_HD) < (B_HD // 2),
                        pltpu.roll(xg, LANES - B_HD // 2, axis=1),
                        pltpu.roll(xg, B_HD // 2, axis=1))
    return xg * cos + swapped * sin_signed


def _kv_layouts(kb, vb):
    low = _lane(kb.shape) < B_HD
    kbr = pltpu.roll(kb, B_HD, axis=1)
    vbr = pltpu.roll(vb, B_HD, axis=1)
    return kb, kbr, jnp.where(low, vb, vbr), jnp.where(low, vbr, vb)


def _gate_lanes(bd, alog_row, dt_row):
    lane = _lane(bd.shape)
    g = -jnp.exp(alog_row) * _softplus(bd + dt_row)
    return jnp.where(lane < A_HEADS, jax.nn.sigmoid(bd), g)


def _layer_norm(r, g, b):
    mu = jnp.mean(r, axis=-1, keepdims=True)
    d = r - mu
    var = jnp.mean(d * d, axis=-1, keepdims=True)
    return d * lax.rsqrt(var + LN_EPS) * g + b


def _ada_kernel(c_ref, w_ref, b_ref, o_ref):
    o_ref[...] = _dot(c_ref[...].astype(BF16), w_ref[...].astype(BF16)) + b_ref[...]


def _ada(c_all, w_ada, b_ada):
    rows = c_all.shape[0]
    tn = 768
    return pl.pallas_call(
        _ada_kernel,
        grid=(3 * D_MODEL // tn,),
        in_specs=[pl.BlockSpec((rows, D_MODEL), lambda j: (0, 0)),
                  pl.BlockSpec((D_MODEL, tn), lambda j: (0, j)),
                  pl.BlockSpec((1, tn), lambda j: (0, j))],
        out_specs=pl.BlockSpec((rows, tn), lambda j: (0, j)),
        out_shape=jax.ShapeDtypeStruct((rows, 3 * D_MODEL), F32),
        compiler_params=pltpu.CompilerParams(dimension_semantics=("arbitrary",),
                                             vmem_limit_bytes=VMEM_LIMIT),
        name="ada",
    )(c_all, w_ada, b_ada)


def _proj_kernel(x_ref, mod_ref, w_ref, cw_ref, alog_ref, dt_ref, cos_ref, sin_ref,
                 q_ref, k_ref, v_ref, za_ref, gb_ref, qb_ref, kb_ref, kbr_ref, vd0_ref, vd1_ref,
                 zb_ref, cst_ref, kbl_ref, vbl_ref, ubuf):
    tm = x_ref.shape[1]
    t = pl.program_id(1)

    @pl.when(t == 0)
    def _():
        ubuf[...] = jnp.zeros(ubuf.shape, F32)

    brow = pl.ds(pl.program_id(0), 1)
    shift = mod_ref[brow, 0:D_MODEL]
    scale = mod_ref[brow, D_MODEL:2 * D_MODEL]
    h = (x_ref[0] * (1.0 + scale) + shift).astype(BF16)

    sub = lax.broadcasted_iota(jnp.int32, (tm // 8, 8, A_WIDTH), 1)
    for gi, o_ref in enumerate((q_ref, k_ref, v_ref)):
        cs = slice(gi * A_WIDTH, (gi + 1) * A_WIDTH)
        u = _dot(h, w_ref[:, cs])
        groups = jnp.concatenate([ubuf[:, cs], u], axis=0).reshape(tm // 8 + 1, 8, A_WIDTH)
        acc = None
        for j in range(CONV_W - 1, 0, -1):
            rot = pltpu.roll(groups, j, axis=1)
            term = (jnp.where(sub < j, rot[:-1], rot[1:]).reshape(tm, A_WIDTH)
                    * cw_ref[CONV_W - 1 - j:CONV_W - j, cs])
            acc = term if acc is None else acc + term
        y = _silu(acc + u * cw_ref[CONV_W - 1:CONV_W, cs])
        if gi == 0:
            y = _l2norm_heads(y, A_DK ** -0.5)
        elif gi == 1:
            y = _l2norm_heads(y, 1.0)
        o_ref[0] = y
        ubuf[:, cs] = u[tm - 8:tm]
        cst_ref[0, :, cs] = u[tm - (CONV_W - 1):tm]

    za_ref[0] = _dot(h, w_ref[:, C_ZA:C_ZA + A_WIDTH])
    gb_ref[0] = _gate_lanes(_dot(h, w_ref[:, C_BD:C_BD + LANES]), alog_ref[...], dt_ref[...])
    cos = cos_ref[...]
    sin = sin_ref[...]
    uq = _dot(h, w_ref[:, C_QB:C_QB + B_WIDTH])
    for g in range(B_WIDTH // LANES):
        qb_ref[0, :, g * LANES:(g + 1) * LANES] = (
            _rotary_group(uq[:, g * LANES:(g + 1) * LANES], cos, sin) * (B_HD ** -0.5 * LOG2E)).astype(BF16)
    ukv = _dot(h, w_ref[:, C_KB:C_KB + 2 * LANES])
    kb = _rotary_group(ukv[:, 0:LANES], cos, sin)
    vb = ukv[:, LANES:2 * LANES]
    for o_ref, val in zip((kb_ref, kbr_ref, vd0_ref, vd1_ref), _kv_layouts(kb, vb)):
        o_ref[0] = val.astype(BF16)
    zb_ref[0] = _dot(h, w_ref[:, C_ZB:C_ZB + B_WIDTH])

    @pl.when(t == pl.num_programs(1) - 1)
    def _():
        kbl_ref[0] = kb[tm - WINDOW:tm]
        vbl_ref[0] = vb[tm - WINDOW:tm]


def _proj(x, mod, mod_row0, w_r, conv_w, alog_row, dt_row, cos_t, sin_t):
    bsz, t, _ = x.shape
    tm = PROJ_TM
    row = lambda w: pl.BlockSpec((1, tm, w), lambda b, i: (b, i, 0))
    const2 = lambda s: pl.BlockSpec(s, lambda b, i: (0, 0))
    per_b = lambda r, w: pl.BlockSpec((1, r, w), lambda b, i: (b, 0, 0))
    wide = lambda w, dt=F32: jax.ShapeDtypeStruct((bsz, t, w), dt)
    return pl.pallas_call(
        _proj_kernel,
        grid=(bsz, t // tm),
        in_specs=[row(D_MODEL),
                  pl.BlockSpec((8, 3 * D_MODEL), lambda b, i: (mod_row0 // 8, 0)),
                  const2((D_MODEL, W_COLS)),
                  const2((CONV_W, A_QKV)),
                  const2((1, LANES)), const2((1, LANES)),
                  pl.BlockSpec((tm, LANES), lambda b, i: (i, 0)),
                  pl.BlockSpec((tm, LANES), lambda b, i: (i, 0))],
        out_specs=[row(A_WIDTH), row(A_WIDTH), row(A_WIDTH), row(A_WIDTH), row(LANES),
                   row(B_WIDTH), row(LANES), row(LANES), row(LANES), row(LANES), row(B_WIDTH),
                   per_b(CONV_W - 1, A_QKV), per_b(WINDOW, LANES), per_b(WINDOW, LANES)],
        out_shape=[wide(A_WIDTH), wide(A_WIDTH), wide(A_WIDTH), wide(A_WIDTH), wide(LANES),
                   wide(B_WIDTH, BF16), wide(LANES, BF16), wide(LANES, BF16), wide(LANES, BF16),
                   wide(LANES, BF16), wide(B_WIDTH),
                   jax.ShapeDtypeStruct((bsz, CONV_W - 1, A_QKV), F32),
                   jax.ShapeDtypeStruct((bsz, WINDOW, LANES), F32),
                   jax.ShapeDtypeStruct((bsz, WINDOW, LANES), F32)],
        scratch_shapes=[pltpu.VMEM((8, A_QKV), F32)],
        compiler_params=pltpu.CompilerParams(dimension_semantics=("arbitrary", "arbitrary"),
                                             vmem_limit_bytes=VMEM_LIMIT),
        name="proj",
    )(x, mod, w_r, conv_w, alog_row, dt_row, cos_t, sin_t)


def _delta_kernel(q_ref, k_ref, v_ref, gb_ref, za_ref, na_ref, oa_ref, st_ref,
                  s_scr, wq_s, ut_s, akd_s, gl_s):
    bsz, ct = q_ref.shape[0], q_ref.shape[1]
    nch = ct // CHUNK
    t = pl.program_id(0)
    wslot = t % 2
    rslot = 1 - wslot

    @pl.when(t == 0)
    def _():
        s_scr[...] = jnp.zeros(s_scr.shape, F32)
        wq_s[...] = jnp.zeros(wq_s.shape, BF16)
        ut_s[...] = jnp.zeros(ut_s.shape, F32)
        akd_s[...] = jnp.zeros(akd_s.shape, BF16)
        gl_s[...] = jnp.zeros(gl_s.shape, F32)

    units = [(b, c, h) for b in range(bsz) for c in range(nch) for h in range(A_HEADS)]
    uid = {u_: i for i, u_ in enumerate(units)}
    rows = lambda c: slice(c * CHUNK, (c + 1) * CHUNK)
    lanes = lambda h: slice(h * A_DK, (h + 1) * A_DK)
    ti = lax.broadcasted_iota(jnp.int32, (CHUNK, CHUNK), 0)
    ii = lax.broadcasted_iota(jnp.int32, (CHUNK, CHUNK), 1)
    na = na_ref[...]

    s_cur = {(b, h): s_scr[b * A_HEADS + h] for b in range(bsz) for h in range(A_HEADS)}
    ws, uu = {}, {}

    def rec_ws(c):
        for b in range(bsz):
            for h in range(A_HEADS):
                i = uid[b, c, h]
                ws[b, h] = _dot(wq_s[rslot, i], s_cur[b, h].astype(BF16))
                uu[b, h] = (ut_s[rslot, i] - ws[b, h][:CHUNK]).astype(BF16)

    def rec_ou(c):
        for b in range(bsz):
            for h in range(A_HEADS):
                i = uid[b, c, h]
                ou = _dot(akd_s[rslot, i], uu[b, h])
                o = ws[b, h][CHUNK:] + ou[:CHUNK]
                s_cur[b, h] = gl_s[rslot, i] * s_cur[b, h] + ou[CHUNK:]
                on = o * lax.rsqrt(jnp.mean(o * o, axis=-1, keepdims=True) + RMS_EPS) * na
                oa_ref[b, rows(c), lanes(h)] = (on * _silu(za_ref[b, rows(c), lanes(h)])).astype(BF16)

    rec_stages = []
    for c in range(nch):
        rec_stages += [lambda c=c: rec_ws(c), lambda c=c: rec_ou(c)]

    def run_rec(n_left_after):
        while rec_stages and len(rec_stages) > n_left_after:
            rec_stages.pop(0)()

    beta, g_col, g_last, eg, dec_incl, nmat, aqk = {}, {}, {}, {}, {}, {}, {}
    for b in range(bsz):
        gbv = gb_ref[b]
        rin = lax.broadcasted_iota(jnp.int32, gbv.shape, 0) % CHUNK
        gcs = gbv
        s = 1
        while s < CHUNK:
            gcs = gcs + jnp.where(rin >= s, pltpu.roll(gcs, s, axis=0), 0.0)
            s *= 2
        gcs_t = gcs.T
        for c in range(nch):
            for h in range(A_HEADS):
                u_ = (b, c, h)
                r0 = c * CHUNK
                beta[u_] = jnp.broadcast_to(gbv[rows(c), h:h + 1], (CHUNK, A_DK))
                g_col[u_] = jnp.broadcast_to(gcs[rows(c), A_HEADS + h:A_HEADS + h + 1], (CHUNK, A_DK))
                g_last[u_] = gcs[r0 + CHUNK - 1:r0 + CHUNK, A_HEADS + h:A_HEADS + h + 1]
                g_row = gcs_t[A_HEADS + h:A_HEADS + h + 1, rows(c)]
                dec_incl[u_] = jnp.exp(jnp.where(ti >= ii, g_col[u_][:, :CHUNK] - g_row, -jnp.inf))
                eg[u_] = jnp.exp(g_col[u_])

    n_prep = 8
    for (b, c, h) in units:
        kcb = k_ref[b, rows(c), lanes(h)].astype(BF16)
        qcb = q_ref[b, rows(c), lanes(h)].astype(BF16)
        kq = _dot_nt(jnp.concatenate([kcb, qcb], axis=0), kcb)
        nmat[b, c, h] = -(beta[b, c, h][:, :CHUNK] * kq[:CHUNK] * jnp.where(ti > ii, dec_incl[b, c, h], 0.0))
        aqk[b, c, h] = kq[CHUNK:] * dec_incl[b, c, h]
    run_rec(n_prep - 1)

    rsum = dict(nmat)
    pwb = {u_: nmat[u_].astype(BF16) for u_ in units}
    pw = {u_: _dot(pwb[u_], pwb[u_]) for u_ in units}
    run_rec(n_prep - 2)
    for step in range(1, 6):
        last = step == 5
        pwb = {u_: pw[u_].astype(BF16) for u_ in units}
        rp = {}
        for u_ in units:
            rb = rsum[u_].astype(BF16)
            rp[u_] = _dot(rb if last else jnp.concatenate([rb, pwb[u_]], axis=0), pwb[u_])
        for u_ in units:
            rsum[u_] = rsum[u_] + pw[u_] + rp[u_][:CHUNK]
            if not last:
                pw[u_] = rp[u_][CHUNK:]
        run_rec(n_prep - 2 - step)

    for (b, c, h) in units:
        u_ = (b, c, h)
        i = uid[u_]
        kc = k_ref[b, rows(c), lanes(h)]
        rhs = jnp.concatenate([(beta[u_] * eg[u_]) * kc, beta[u_] * v_ref[b, rows(c), lanes(h)]],
                              axis=-1)
        sol = rhs + _dot(rsum[u_].astype(BF16), rhs.astype(BF16))
        wq_s[wslot, i] = jnp.concatenate([sol[:, :A_DK], eg[u_] * q_ref[b, rows(c), lanes(h)]],
                                         axis=0).astype(BF16)
        ut_s[wslot, i] = sol[:, A_DK:]
        kd = jnp.exp(g_last[u_] - g_col[u_]) * kc
        akd_s[wslot, i] = jnp.concatenate([aqk[u_], kd.T], axis=0).astype(BF16)
        gl_s[wslot, i] = jnp.broadcast_to(jnp.exp(g_last[u_]), (1, A_DV))
    run_rec(0)

    for b in range(bsz):
        for h in range(A_HEADS):
            s_scr[b * A_HEADS + h] = s_cur[b, h]

    @pl.when(t == pl.num_programs(0) - 1)
    def _():
        for b in range(bsz):
            for h in range(A_HEADS):
                st_ref[b, h] = s_cur[b, h]


def _delta(q, k, v, gb, za, na_row):
    bsz, t, _ = q.shape
    ct = DELTA_CT
    nt = t // ct
    n_units = bsz * (ct // CHUNK) * A_HEADS
    prep = lambda w: pl.BlockSpec((bsz, ct, w), lambda i: (0, jnp.minimum(i, nt - 1), 0))
    rec = lambda w: pl.BlockSpec((bsz, ct, w), lambda i: (0, jnp.maximum(i - 1, 0), 0))
    return pl.pallas_call(
        _delta_kernel,
        grid=(nt + 1,),
        in_specs=[prep(A_WIDTH), prep(A_WIDTH), prep(A_WIDTH), prep(LANES), rec(A_WIDTH),
                  pl.BlockSpec((1, A_DV), lambda i: (0, 0))],
        out_specs=[rec(A_WIDTH),
                   pl.BlockSpec((bsz, A_HEADS, A_DK, A_DV), lambda i: (0, 0, 0, 0))],
        out_shape=[jax.ShapeDtypeStruct((bsz, t, A_WIDTH), BF16),
                   jax.ShapeDtypeStruct((bsz, A_HEADS, A_DK, A_DV), F32)],
        scratch_shapes=[pltpu.VMEM((bsz * A_HEADS, A_DK, A_DV), F32),
                        pltpu.VMEM((2, n_units, 2 * CHUNK, A_DK), BF16),
                        pltpu.VMEM((2, n_units, CHUNK, A_DV), F32),
                        pltpu.VMEM((2, n_units, 3 * CHUNK, CHUNK), BF16),
                        pltpu.VMEM((2, n_units, 1, A_DV), F32)],
        compiler_params=pltpu.CompilerParams(dimension_semantics=("arbitrary",),
                                             vmem_limit_bytes=VMEM_LIMIT),
        name="delta",
    )(q, k, v, gb, za, na_row)


def _swa_kernel(sink_ref, qb_ref, kc_ref, kp_ref, krc_ref, krp_ref, v0c_ref, v0p_ref, v1c_ref, v1p_ref,
                zb_ref, ob_ref):
    n = pl.program_id(1)
    tq = qb_ref.shape[1]
    blk = WINDOW
    kx = (jnp.concatenate([kp_ref[0], kc_ref[0]], axis=0), jnp.concatenate([krp_ref[0], krc_ref[0]], axis=0))
    vd = (jnp.concatenate([v0p_ref[0], v0c_ref[0]], axis=0), jnp.concatenate([v1p_ref[0], v1c_ref[0]], axis=0))

    a = lax.broadcasted_iota(jnp.int32, (2 * blk, 2 * blk), 0) % blk
    j = lax.broadcasted_iota(jnp.int32, (2 * blk, 2 * blk), 1)
    rel = a + blk - j
    band = (rel >= 0) & (rel <= WINDOW)
    band_first = band & ((n > 0) | (j >= blk))
    top = lax.broadcasted_iota(jnp.int32, (2 * blk, 1), 0) < blk
    low = _lane((blk, LANES)) < B_HD
    zero = jnp.zeros((blk, LANES), BF16)

    qrows = lambda i: slice(i * blk, (i + 1) * blk)
    krows = lambda i: slice(i * blk, (i + 2) * blk)
    sink = {(kh, half): jnp.where(top, sink_ref[kh * B_GROUP + half] * LOG2E,
                                  sink_ref[kh * B_GROUP + half + 2] * LOG2E)
            for kh in range(B_KV_HEADS) for half in range(2)}
    for i0 in range(0, tq // blk, SWA_WAVE):
        blocks = range(i0, i0 + SWA_WAVE)
        units = [(i, kh, half) for i in blocks for kh in range(B_KV_HEADS) for half in range(2)]
        sc = {}
        for (i, kh, half) in units:
            qs = []
            for g in range(2):
                grp = kh * 2 + g
                xg = qb_ref[0, qrows(i), grp * LANES:(grp + 1) * LANES]
                qs.append(jnp.where(low if half == 0 else jnp.logical_not(low), xg, zero))
            qz = jnp.concatenate(qs, axis=0)
            sc[i, kh, half] = _dot_nt(qz, kx[0 if kh == half else 1][krows(i)])
        p, den = {}, {}
        for u_ in units:
            i, kh, half = u_
            s_m = jnp.where(band_first if i == 0 else band, sc[u_], -jnp.inf)
            m = jnp.maximum(jnp.max(s_m, axis=-1, keepdims=True), sink[kh, half])
            e = jnp.exp2(s_m - m)
            den[u_] = jnp.sum(e, axis=-1, keepdims=True) + jnp.exp2(sink[kh, half] - m)
            p[u_] = e.astype(BF16)
        pv = {u_: _dot(p[u_], vd[u_[1]][krows(u_[0])]) for u_ in units}
        outs = {u_: pv[u_] / den[u_] for u_ in units}
        for i in blocks:
            for grp in range(B_WIDTH // LANES):
                kh, g = grp // 2, grp % 2
                og = jnp.where(low, outs[i, kh, 0][g * blk:(g + 1) * blk], outs[i, kh, 1][g * blk:(g + 1) * blk])
                gs = slice(grp * LANES, (grp + 1) * LANES)
                ob_ref[0, qrows(i), gs] = (og * _silu(zb_ref[0, qrows(i), gs])).astype(BF16)


def _swa(sinks, qb, kb, kbr, vd0, vd1, zb):
    bsz, t, _ = qb.shape
    tq = SWA_TQ
    per = tq // WINDOW
    cur = lambda w: pl.BlockSpec((1, tq, w), lambda b, i: (b, i, 0))
    prev = pl.BlockSpec((1, WINDOW, LANES), lambda b, i: (b, jnp.maximum(i * per - 1, 0), 0))
    return pl.pallas_call(
        _swa_kernel,
        grid=(bsz, t // tq),
        in_specs=[pl.BlockSpec(memory_space=pltpu.SMEM), cur(B_WIDTH),
                  cur(LANES), prev, cur(LANES), prev, cur(LANES), prev, cur(LANES), prev,
                  cur(B_WIDTH)],
        out_specs=cur(B_WIDTH),
        out_shape=jax.ShapeDtypeStruct((bsz, t, B_WIDTH), BF16),
        compiler_params=pltpu.CompilerParams(dimension_semantics=("arbitrary", "arbitrary"),
                                             vmem_limit_bytes=VMEM_LIMIT),
        name="swa",
    )(sinks, qb, kb, kb, kbr, kbr, vd0, vd0, vd1, vd1, zb)


def _out_kernel(oa_ref, ob_ref, x_ref, gate_ref, w_ref, g_ref, b_ref, y_ref, *, gate_per_batch):
    mix = _dot(oa_ref[0], w_ref[0:A_WIDTH, :]) + _dot(ob_ref[0], w_ref[A_WIDTH:MIX_WIDTH, :])
    gate = gate_ref[pl.ds(pl.program_id(0), 1), :] if gate_per_batch else gate_ref[...]
    r = DEEPNORM_ALPHA * x_ref[0] + (1.0 + gate) * mix
    y_ref[0] = _layer_norm(r, g_ref[...], b_ref[...])


def _out(oa, ob, x, mod, mod_row0, gate_per_batch, w_out, ln_g, ln_b, tm):
    bsz, t, _ = x.shape
    grows = 8 if gate_per_batch else tm
    row = lambda w: pl.BlockSpec((1, tm, w), lambda b, i: (b, i, 0))
    const2 = lambda s: pl.BlockSpec(s, lambda b, i: (0, 0))
    return pl.pallas_call(
        functools.partial(_out_kernel, gate_per_batch=gate_per_batch),
        grid=(bsz, t // tm),
        in_specs=[row(A_WIDTH), row(B_WIDTH), row(D_MODEL),
                  pl.BlockSpec((grows, D_MODEL), lambda b, i: (mod_row0 // grows, 2)),
                  const2((MIX_WIDTH, D_MODEL)), const2((1, D_MODEL)), const2((1, D_MODEL))],
        out_specs=row(D_MODEL),
        out_shape=jax.ShapeDtypeStruct((bsz, t, D_MODEL), F32),
        compiler_params=pltpu.CompilerParams(dimension_semantics=("arbitrary", "arbitrary"),
                                             vmem_limit_bytes=VMEM_LIMIT),
        name="out",
    )(oa, ob, x, mod, w_out, ln_g, ln_b)


def _sproj_kernel(x_ref, mod_ref, w_ref, cw_ref, cst_ref, alog_ref, dt_ref, cos_ref, sin_ref,
                  q_ref, k_ref, v_ref, za_ref, gb_ref, qb_ref, kb_ref, vb_ref, zb_ref, ncs_ref):
    shift = mod_ref[:, 0:D_MODEL]
    scale = mod_ref[:, D_MODEL:2 * D_MODEL]
    h = (x_ref[...] * (1.0 + scale) + shift).astype(BF16)

    for gi, o_ref in enumerate((q_ref, k_ref, v_ref)):
        c0 = gi * A_WIDTH
        cs = slice(c0, c0 + A_WIDTH)
        u = _dot(h, w_ref[:, cs])
        acc = cst_ref[0, :, cs] * cw_ref[0:1, cs]
        acc = acc + cst_ref[1, :, cs] * cw_ref[1:2, cs]
        acc = acc + cst_ref[2, :, cs] * cw_ref[2:3, cs]
        acc = acc + u * cw_ref[3:4, cs]
        y = _silu(acc)
        if gi == 0:
            y = _l2norm_heads(y, A_DK ** -0.5)
        elif gi == 1:
            y = _l2norm_heads(y, 1.0)
        o_ref[...] = y
        ncs_ref[0, :, cs] = cst_ref[1, :, cs]
        ncs_ref[1, :, cs] = cst_ref[2, :, cs]
        ncs_ref[2, :, cs] = u

    za_ref[...] = _dot(h, w_ref[:, C_ZA:C_ZA + A_WIDTH])
    gb_ref[...] = _gate_lanes(_dot(h, w_ref[:, C_BD:C_BD + LANES]), alog_ref[...], dt_ref[...])

    cos = cos_ref[...]
    sin = sin_ref[...]
    uq = _dot(h, w_ref[:, C_QB:C_QB + B_WIDTH])
    for g in range(B_WIDTH // LANES):
        qb_ref[:, g * LANES:(g + 1) * LANES] = (
            _rotary_group(uq[:, g * LANES:(g + 1) * LANES], cos, sin) * (B_HD ** -0.5))
    kb_ref[...] = _rotary_group(_dot(h, w_ref[:, C_KB:C_KB + LANES]), cos, sin)
    vb_ref[...] = _dot(h, w_ref[:, C_VB:C_VB + LANES])
    zb_ref[...] = _dot(h, w_ref[:, C_ZB:C_ZB + B_WIDTH])


def _sproj(x, mod_s, w_r, conv_w, cst, alog_row, dt_row, cos_row, sin_row):
    n = x.shape[0]
    full = lambda s: pl.BlockSpec(s, lambda i: (0,) * len(s))
    wide = lambda w: jax.ShapeDtypeStruct((n, w), F32)
    return pl.pallas_call(
        _sproj_kernel,
        grid=(1,),
        in_specs=[full((n, D_MODEL)), pl.BlockSpec((n, 3 * D_MODEL), lambda i: (0, 0)),
                  full((D_MODEL, W_COLS)),
                  full((CONV_W, A_QKV)), full((CONV_W - 1, n, A_QKV)),
                  full((1, LANES)), full((1, LANES)), full((1, LANES)), full((1, LANES))],
        out_specs=[full((n, A_WIDTH)), full((n, A_WIDTH)), full((n, A_WIDTH)), full((n, A_WIDTH)),
                   full((n, LANES)), full((n, B_WIDTH)), full((n, LANES)), full((n, LANES)),
                   full((n, B_WIDTH)), full((CONV_W - 1, n, A_QKV))],
        out_shape=[wide(A_WIDTH), wide(A_WIDTH), wide(A_WIDTH), wide(A_WIDTH), wide(LANES),
                   wide(B_WIDTH), wide(LANES), wide(LANES), wide(B_WIDTH),
                   jax.ShapeDtypeStruct((CONV_W - 1, n, A_QKV), F32)],
        compiler_params=pltpu.CompilerParams(dimension_semantics=("arbitrary",),
                                             vmem_limit_bytes=VMEM_LIMIT),
        name="sproj",
    )(x, mod_s, w_r, conv_w, cst, alog_row, dt_row, cos_row, sin_row)


def _sstep_kernel(sink_ref, q_ref, k_ref, v_ref, gb_ref, za_ref, na_ref, st_ref,
                  qb_ref, kn_ref, vn_ref, zb_ref, ck_ref, cv_ref,
                  oa_ref, ob_ref, nst_ref, nck_ref, ncv_ref,
                  o_scr, ob_scr):
    bt = q_ref.shape[0]
    gbv = gb_ref[...]

    for h in range(A_HEADS):
        hs = slice(h * A_DK, (h + 1) * A_DK)
        q_t = q_ref[:, hs].T
        k_t = k_ref[:, hs].T
        for bb in range(bt):
            eg = jnp.exp(gbv[bb:bb + 1, A_HEADS + h:A_HEADS + h + 1])
            beta = gbv[bb:bb + 1, h:h + 1]
            kcol = k_t[:, bb:bb + 1]
            qcol = q_t[:, bb:bb + 1]
            s1 = eg * st_ref[bb, h]
            pred = jnp.sum(kcol * s1, axis=0, keepdims=True)
            upd = beta * (v_ref[bb:bb + 1, hs] - pred)
            s2 = s1 + kcol * upd
            nst_ref[bb, h] = s2
            o_scr[bb:bb + 1, hs] = jnp.sum(qcol * s2, axis=0, keepdims=True)
    na = na_ref[...]
    for h in range(A_HEADS):
        hs = slice(h * A_DK, (h + 1) * A_DK)
        o = o_scr[:, hs]
        on = o * lax.rsqrt(jnp.mean(o * o, axis=-1, keepdims=True) + RMS_EPS) * na
        oa_ref[:, hs] = (on * _silu(za_ref[:, hs])).astype(BF16)

    row8 = lax.broadcasted_iota(jnp.int32, (B_HEADS, LANES), 0)
    lane8 = _lane((B_HEADS, LANES))
    own_half = (lane8 >= B_HD) == (row8 >= B_GROUP)
    rcol = lax.broadcasted_iota(jnp.int32, (B_HEADS, 1), 0)
    sink = jnp.zeros((B_HEADS, 1), F32)
    for r in range(B_HEADS):
        sink = jnp.where(rcol == r, sink_ref[r], sink)
    qv = qb_ref[...]
    qv_r = jnp.concatenate([pltpu.roll(qv[:, g * LANES:(g + 1) * LANES], B_HD, axis=1)
                            for g in range(B_WIDTH // LANES)], axis=-1)
    kn_t = kn_ref[...].T
    vn_t = vn_ref[...].T
    newest = _lane((LANES, WINDOW)) == WINDOW - 1
    qzs, scs = [], []
    for bb in range(bt):
        qz = jnp.zeros((B_HEADS, LANES), F32)
        for r in range(B_HEADS):
            grp, half, kh = r // 2, r % 2, r // B_GROUP
            src = qv if half == kh else qv_r
            qz = jnp.where(row8 == r, src[bb:bb + 1, grp * LANES:(grp + 1) * LANES], qz)
        qzs.append(jnp.where(own_half, qz, 0.0))
    for bb in range(bt):
        scs.append(_dot(qzs[bb], ck_ref[bb]))
    ps, pnews, dens = [], [], []
    for bb in range(bt):
        sc_new = jnp.sum(qzs[bb] * kn_ref[bb:bb + 1, :], axis=-1, keepdims=True)
        m = jnp.maximum(jnp.maximum(jnp.max(scs[bb], axis=-1, keepdims=True), sc_new), sink)
        p = jnp.exp(scs[bb] - m)
        p_new = jnp.exp(sc_new - m)
        ps.append(p)
        pnews.append(p_new)
        dens.append(jnp.sum(p, axis=-1, keepdims=True) + p_new + jnp.exp(sink - m))
    pvs = [_dot_nt(ps[bb], cv_ref[bb]) for bb in range(bt)]
    for bb in range(bt):
        o = (pvs[bb] + pnews[bb] * vn_ref[bb:bb + 1, :]) / dens[bb]
        o = jnp.where(own_half, o, 0.0)
        ob_scr[bb * B_HEADS:(bb + 1) * B_HEADS, :] = o + pltpu.roll(o, B_HD, axis=1)
    for bb in range(bt):
        nck_ref[bb] = jnp.where(newest, kn_t[:, bb:bb + 1], pltpu.roll(ck_ref[bb], WINDOW - 1, axis=1))
        ncv_ref[bb] = jnp.where(newest, vn_t[:, bb:bb + 1], pltpu.roll(cv_ref[bb], WINDOW - 1, axis=1))
    low = _lane((bt, LANES)) < B_HD
    for grp in range(B_WIDTH // LANES):
        even = ob_scr[pl.ds(2 * grp, bt, stride=B_HEADS), :]
        odd = ob_scr[pl.ds(2 * grp + 1, bt, stride=B_HEADS), :]
        gs = slice(grp * LANES, (grp + 1) * LANES)
        ob_ref[:, gs] = (jnp.where(low, even, odd) * _silu(zb_ref[:, gs])).astype(BF16)


def _sstep(sinks, q, k, v, gb, za, na_row, state, qb, kn, vn, zb, ck, cv):
    n = q.shape[0]
    bt = STEP_BT
    row = lambda w: pl.BlockSpec((bt, w), lambda i: (i, 0))
    st_spec = pl.BlockSpec((bt, A_HEADS, A_DK, A_DV), lambda i: (i, 0, 0, 0))
    c_spec = pl.BlockSpec((bt, WINDOW, LANES), lambda i: (i, 0, 0))
    return pl.pallas_call(
        _sstep_kernel,
        grid=(n // bt,),
        in_specs=[pl.BlockSpec(memory_space=pltpu.SMEM),
                  row(A_WIDTH), row(A_WIDTH), row(A_WIDTH), row(LANES), row(A_WIDTH),
                  pl.BlockSpec((1, A_DV), lambda i: (0, 0)), st_spec,
                  row(B_WIDTH), row(LANES), row(LANES), row(B_WIDTH), c_spec, c_spec],
        out_specs=[row(A_WIDTH), row(B_WIDTH), st_spec, c_spec, c_spec],
        out_shape=[jax.ShapeDtypeStruct((n, A_WIDTH), BF16),
                   jax.ShapeDtypeStruct((n, B_WIDTH), BF16),
                   jax.ShapeDtypeStruct((n, A_HEADS, A_DK, A_DV), F32),
                   jax.ShapeDtypeStruct((n, WINDOW, LANES), F32),
                   jax.ShapeDtypeStruct((n, WINDOW, LANES), F32)],
        scratch_shapes=[pltpu.VMEM((bt, A_WIDTH), F32), pltpu.VMEM((bt * B_HEADS, LANES), F32)],
        compiler_params=pltpu.CompilerParams(dimension_semantics=("arbitrary",),
                                             vmem_limit_bytes=VMEM_LIMIT),
        name="sstep",
    )(sinks, q, k, v, gb, za, na_row, state, qb, kn, vn, zb, ck, cv)


def _rope_tables(pos):
    half = B_HD // 2
    inv = 1.0 / (ROPE_THETA ** (np.arange(half, dtype=np.float64) / half))
    ang = np.asarray(pos, np.float64)[:, None] * inv[None, :]
    cos, sin = np.cos(ang), np.sin(ang)
    reps = LANES // B_HD
    return (jnp.asarray(np.tile(np.concatenate([cos, cos], -1), (1, reps)), F32),
            jnp.asarray(np.tile(np.concatenate([-sin, sin], -1), (1, reps)), F32))


def _pad_row(vec, offset):
    return jnp.zeros((1, LANES), F32).at[0, offset:offset + vec.shape[0]].set(vec.astype(F32))


def _layer(x_prompt, x_sample, state_conv, state_delta, cache_k, cache_v, c_prompt, c_sample,
           w_ada, b_ada, w_in, conv_w, a_log, dt_bias, norm_a, sinks, w_out, ln_g, ln_b):
    bsz, seq, _ = x_prompt.shape
    n_s = x_sample.shape[0]

    w_t = jnp.swapaxes(w_in, 0, 1)
    w_r = jnp.swapaxes(jnp.concatenate([
        w_t[0:OFF_A_BETA], w_t[OFF_B_Q:PROJ_COLS], w_t[OFF_A_BETA:OFF_B_Q],
        jnp.zeros((LANES - 2 * A_HEADS, D_MODEL), w_in.dtype)], axis=0).astype(BF16), 0, 1)
    w_o = w_out.astype(BF16)
    alog_row = _pad_row(a_log, A_HEADS)
    dt_row = _pad_row(dt_bias, A_HEADS)
    na_row = norm_a.reshape(1, A_DV)
    g_row = ln_g.reshape(1, D_MODEL)
    b_row = ln_b.reshape(1, D_MODEL)

    assert n_s % 8 == 0 and bsz <= 8
    c_all = jnp.concatenate([c_sample, c_prompt, jnp.zeros((8 - bsz, D_MODEL), F32)], axis=0)
    mod = _ada(c_all, w_ada, b_ada.reshape(1, 3 * D_MODEL))

    cos_p, sin_p = _rope_tables(np.arange(seq))
    (q, k, v, za, gb, qb, kb, kbr, vd0, vd1, zb, conv_p, kb_last, vb_last) = _proj(
        x_prompt, mod, n_s, w_r, conv_w, alog_row, dt_row, cos_p, sin_p)
    oa, delta_p = _delta(q, k, v, gb, za, na_row)
    ob = _swa(sinks, qb, kb, kbr, vd0, vd1, zb)
    y_p = _out(oa, ob, x_prompt, mod, n_s, True, w_o, g_row, b_row, OUT_TM)
    swa_k_p = kb_last.reshape(bsz, WINDOW, B_KV_HEADS, B_HD)
    swa_v_p = vb_last.reshape(bsz, WINDOW, B_KV_HEADS, B_HD)

    cos_s, sin_s = _rope_tables(np.array([PAST_LEN]))
    xs = x_sample.reshape(n_s, D_MODEL)
    cst = jnp.transpose(state_conv, (1, 0, 2))
    sq, sk, sv, sza, sgb, sqb, skn, svn, szb, ncs = _sproj(xs, mod, w_r, conv_w, cst, alog_row, dt_row,
                                                           cos_s, sin_s)
    soa, sob, delta_s, nck, ncv = _sstep(sinks, sq, sk, sv, sgb, sza, na_row, state_delta,
                                         sqb, skn, svn, szb,
                                         jnp.swapaxes(cache_k.reshape(n_s, WINDOW, LANES), 1, 2),
                                         jnp.swapaxes(cache_v.reshape(n_s, WINDOW, LANES), 1, 2))
    y_s = _out(soa[None], sob[None], xs[None], mod, 0, False, w_o, g_row, b_row, n_s)
    conv_s = jnp.transpose(ncs, (1, 0, 2))
    unpack = lambda c: jnp.swapaxes(c, 1, 2).reshape(n_s, WINDOW, B_KV_HEADS, B_HD)
    return (y_p, y_s.reshape(n_s, 1, D_MODEL), conv_p, delta_p, swa_k_p, swa_v_p,
            conv_s, delta_s, unpack(nck), unpack(ncv))


def kernel(x_prompt, x_sample, state_conv, state_delta, cache_swa_k, cache_swa_v, c_prompt, c_sample,
           w_ada, b_ada, w_in, conv_w, a_log, dt_bias, norm_a, sinks, w_out, ln_g, ln_b):
    assert w_ada.shape[0] == DEPTH == 1
    outs = _layer(x_prompt, x_sample, state_conv[0], state_delta[0], cache_swa_k[0], cache_swa_v[0],
                  c_prompt, c_sample, w_ada[0], b_ada[0], w_in[0], conv_w[0], a_log[0], dt_bias[0],
                  norm_a[0], sinks[0], w_out[0], ln_g[0], ln_b[0])
    y_p, y_s = outs[0], outs[1]
    return (y_p, y_s) + tuple(o[None] for o in outs[2:])
```

```python
import functools

import jax
import jax.numpy as jnp
import numpy as np
from jax import lax
from jax.experimental import pallas as pl
from jax.experimental.pallas import tpu as pltpu

F32 = jnp.float32
BF16 = jnp.bfloat16

D_MODEL = 1024
DEPTH = 1
PAST_LEN = 8192
A_HEADS = 4
A_DK = 128
A_DV = 128
A_WIDTH = A_HEADS * A_DV
A_QKV = 3 * A_WIDTH
CONV_W = 4
CHUNK = 64
B_HEADS = 8
B_KV_HEADS = 2
B_HD = 64
B_GROUP = B_HEADS // B_KV_HEADS
B_WIDTH = B_HEADS * B_HD
B_KV_WIDTH = B_KV_HEADS * B_HD
WINDOW = 128
ROPE_THETA = 10000.0
MIX_WIDTH = A_WIDTH + B_WIDTH
DEEPNORM_ALPHA = (2 * DEPTH) ** 0.25
LOG2E = 1.4426950408889634
LN_EPS = 1e-5
RMS_EPS = 1e-6
L2_EPS = 1e-6

OFF_A_Z = A_QKV
OFF_A_BETA = OFF_A_Z + A_WIDTH
OFF_A_DECAY = OFF_A_BETA + A_HEADS
OFF_B_Q = OFF_A_DECAY + A_HEADS
OFF_B_K = OFF_B_Q + B_WIDTH
OFF_B_V = OFF_B_K + B_KV_WIDTH
OFF_B_Z = OFF_B_V + B_KV_WIDTH
PROJ_COLS = OFF_B_Z + B_WIDTH

LANES = 128
C_QKV = 0
C_ZA = C_QKV + A_QKV
C_QB = C_ZA + A_WIDTH
C_KB = C_QB + B_WIDTH
C_VB = C_KB + B_KV_WIDTH
C_ZB = C_VB + B_KV_WIDTH
C_BD = C_ZB + B_WIDTH
W_COLS = C_BD + LANES

VMEM_LIMIT = 56 * 1024 * 1024

PROJ_TM = 512
DELTA_CT = 256
SWA_TQ = 512
SWA_WAVE = 1
OUT_TM = 1024
STEP_BT = 16


def _dot(a, b):
    return jnp.dot(a, b, preferred_element_type=F32)


def _dot_nt(a, b):
    return lax.dot_general(a, b, (((1,), (1,)), ((), ())), preferred_element_type=F32)


def _silu(x):
    return x * jax.nn.sigmoid(x)


def _softplus(x):
    return jnp.maximum(x, 0.0) + jnp.log1p(jnp.exp(-jnp.abs(x)))


def _lane(shape):
    return lax.broadcasted_iota(jnp.int32, shape, len(shape) - 1)


def _l2norm_heads(y, scale):
    outs = []
    for h in range(A_HEADS):
        xh = y[:, h * A_DK:(h + 1) * A_DK]
        ss = jnp.sum(xh * xh, axis=-1, keepdims=True)
        xn = xh * lax.rsqrt(ss + L2_EPS)
        outs.append(xn * scale if scale != 1.0 else xn)
    return jnp.concatenate(outs, axis=-1)


def _rotary_group(xg, cos, sin_signed):
    lane = _lane(xg.shape)
    swapped = jnp.where((lane % B_HD) < (B_HD // 2),
                        pltpu.roll(xg, LANES - B_HD // 2, axis=1),
                        pltpu.roll(xg, B_HD // 2, axis=1))
    return xg * cos + swapped * sin_signed


def _kv_layouts(kb, vb):
    low = _lane(kb.shape) < B_HD
    kbr = pltpu.roll(kb, B_HD, axis=1)
    vbr = pltpu.roll(vb, B_HD, axis=1)
    return kb, kbr, jnp.where(low, vb, vbr), jnp.where(low, vbr, vb)


def _gate_lanes(bd, alog_row, dt_row):
    lane = _lane(bd.shape)
    g = -jnp.exp(alog_row) * _softplus(bd + dt_row)
    return jnp.where(lane < A_HEADS, jax.nn.sigmoid(bd), g)


def _layer_norm(r, g, b):
    mu = jnp.mean(r, axis=-1, keepdims=True)
    d = r - mu
    var = jnp.mean(d * d, axis=-1, keepdims=True)
    return d * lax.rsqrt(var + LN_EPS) * g + b


def _ada_kernel(c_ref, w_ref, b_ref, o_ref):
    o_ref[...] = _dot(c_ref[...].astype(BF16), w_ref[...].astype(BF16)) + b_ref[...]


def _ada(c_all, w_ada, b_ada):
    rows = c_all.shape[0]
    tn = 768
    return pl.pallas_call(
        _ada_kernel,
        grid=(3 * D_MODEL // tn,),
        in_specs=[pl.BlockSpec((rows, D_MODEL), lambda j: (0, 0)),
                  pl.BlockSpec((D_MODEL, tn), lambda j: (0, j)),
                  pl.BlockSpec((1, tn), lambda j: (0, j))],
        out_specs=pl.BlockSpec((rows, tn), lambda j: (0, j)),
        out_shape=jax.ShapeDtypeStruct((rows, 3 * D_MODEL), F32),
        compiler_params=pltpu.CompilerParams(dimension_semantics=("arbitrary",),
                                             vmem_limit_bytes=VMEM_LIMIT),
        name="ada",
    )(c_all, w_ada, b_ada)


def _proj_kernel(x_ref, mod_ref, w_ref, cw_ref, alog_ref, dt_ref, cos_ref, sin_ref,
                 q_ref, k_ref, v_ref, za_ref, gb_ref, qb_ref, kb_ref, kbr_ref, vd0_ref, vd1_ref,
                 zb_ref, cst_ref, kbl_ref, vbl_ref, ubuf):
    tm = x_ref.shape[1]
    t = pl.program_id(1)

    @pl.when(t == 0)
    def _():
        ubuf[...] = jnp.zeros(ubuf.shape, F32)

    brow = pl.ds(pl.program_id(0), 1)
    shift = mod_ref[brow, 0:D_MODEL]
    scale = mod_ref[brow, D_MODEL:2 * D_MODEL]
    h = (x_ref[0] * (1.0 + scale) + shift).astype(BF16)

    sub = lax.broadcasted_iota(jnp.int32, (tm // 8, 8, A_WIDTH), 1)
    for gi, o_ref in enumerate((q_ref, k_ref, v_ref)):
        cs = slice(gi * A_WIDTH, (gi + 1) * A_WIDTH)
        u = _dot(h, w_ref[:, cs])
        groups = jnp.concatenate([ubuf[:, cs], u], axis=0).reshape(tm // 8 + 1, 8, A_WIDTH)
        acc = None
        for j in range(CONV_W - 1, 0, -1):
            rot = pltpu.roll(groups, j, axis=1)
            term = (jnp.where(sub < j, rot[:-1], rot[1:]).reshape(tm, A_WIDTH)
                    * cw_ref[CONV_W - 1 - j:CONV_W - j, cs])
            acc = term if acc is None else acc + term
        y = _silu(acc + u * cw_ref[CONV_W - 1:CONV_W, cs])
        if gi == 0:
            y = _l2norm_heads(y, A_DK ** -0.5)
        elif gi == 1:
            y = _l2norm_heads(y, 1.0)
        o_ref[0] = y
        ubuf[:, cs] = u[tm - 8:tm]
        cst_ref[0, :, cs] = u[tm - (CONV_W - 1):tm]

    za_ref[0] = _dot(h, w_ref[:, C_ZA:C_ZA + A_WIDTH])
    gb_ref[0] = _gate_lanes(_dot(h, w_ref[:, C_BD:C_BD + LANES]), alog_ref[...], dt_ref[...])
    cos = cos_ref[...]
    sin = sin_ref[...]
    uq = _dot(h, w_ref[:, C_QB:C_QB + B_WIDTH])
    for g in range(B_WIDTH // LANES):
        qb_ref[0, :, g * LANES:(g + 1) * LANES] = (
            _rotary_group(uq[:, g * LANES:(g + 1) * LANES], cos, sin) * (B_HD ** -0.5 * LOG2E)).astype(BF16)
    ukv = _dot(h, w_ref[:, C_KB:C_KB + 2 * LANES])
    kb = _rotary_group(ukv[:, 0:LANES], cos, sin)
    vb = ukv[:, LANES:2 * LANES]
    for o_ref, val in zip((kb_ref, kbr_ref, vd0_ref, vd1_ref), _kv_layouts(kb, vb)):
        o_ref[0] = val.astype(BF16)
    zb_ref[0] = _dot(h, w_ref[:, C_ZB:C_ZB + B_WIDTH])

    @pl.when(t == pl.num_programs(1) - 1)
    def _():
        kbl_ref[0] = kb[tm - WINDOW:tm]
        vbl_ref[0] = vb[tm - WINDOW:tm]


def _proj(x, mod, mod_row0, w_r, conv_w, alog_row, dt_row, cos_t, sin_t):
    bsz, t, _ = x.shape
    tm = PROJ_TM
    row = lambda w: pl.BlockSpec((1, tm, w), lambda b, i: (b, i, 0))
    const2 = lambda s: pl.BlockSpec(s, lambda b, i: (0, 0))
    per_b = lambda r, w: pl.BlockSpec((1, r, w), lambda b, i: (b, 0, 0))
    wide = lambda w, dt=F32: jax.ShapeDtypeStruct((bsz, t, w), dt)
    return pl.pallas_call(
        _proj_kernel,
        grid=(bsz, t // tm),
        in_specs=[row(D_MODEL),
                  pl.BlockSpec((8, 3 * D_MODEL), lambda b, i: (mod_row0 // 8, 0)),
                  const2((D_MODEL, W_COLS)),
                  const2((CONV_W, A_QKV)),
                  const2((1, LANES)), const2((1, LANES)),
                  pl.BlockSpec((tm, LANES), lambda b, i: (i, 0)),
                  pl.BlockSpec((tm, LANES), lambda b, i: (i, 0))],
        out_specs=[row(A_WIDTH), row(A_WIDTH), row(A_WIDTH), row(A_WIDTH), row(LANES),
                   row(B_WIDTH), row(LANES), row(LANES), row(LANES), row(LANES), row(B_WIDTH),
                   per_b(CONV_W - 1, A_QKV), per_b(WINDOW, LANES), per_b(WINDOW, LANES)],
        out_shape=[wide(A_WIDTH), wide(A_WIDTH), wide(A_WIDTH), wide(A_WIDTH), wide(LANES),
                   wide(B_WIDTH, BF16), wide(LANES, BF16), wide(LANES, BF16), wide(LANES, BF16),
                   wide(LANES, BF16), wide(B_WIDTH),
                   jax.ShapeDtypeStruct((bsz, CONV_W - 1, A_QKV), F32),
                   jax.ShapeDtypeStruct((bsz, WINDOW, LANES), F32),
                   jax.ShapeDtypeStruct((bsz, WINDOW, LANES), F32)],
        scratch_shapes=[pltpu.VMEM((8, A_QKV), F32)],
        compiler_params=pltpu.CompilerParams(dimension_semantics=("arbitrary", "arbitrary"),
                                             vmem_limit_bytes=VMEM_LIMIT),
        name="proj",
    )(x, mod, w_r, conv_w, alog_row, dt_row, cos_t, sin_t)


def _delta_kernel(q_ref, k_ref, v_ref, gb_ref, za_ref, na_ref, oa_ref, st_ref,
                  s_scr, wq_s, ut_s, aqk_s, kdt_s, gl_s):
    bsz, ct = q_ref.shape[0], q_ref.shape[1]
    nch = ct // CHUNK
    t = pl.program_id(0)
    wslot = t % 2
    rslot = 1 - wslot

    @pl.when(t == 0)
    def _():
        s_scr[...] = jnp.zeros(s_scr.shape, F32)
        wq_s[...] = jnp.zeros(wq_s.shape, BF16)
        ut_s[...] = jnp.zeros(ut_s.shape, F32)
        aqk_s[...] = jnp.zeros(aqk_s.shape, BF16)
        kdt_s[...] = jnp.zeros(kdt_s.shape, BF16)
        gl_s[...] = jnp.zeros(gl_s.shape, F32)

    units = [(b, c, h) for b in range(bsz) for c in range(nch) for h in range(A_HEADS)]
    uid = {u_: i for i, u_ in enumerate(units)}
    rows = lambda c: slice(c * CHUNK, (c + 1) * CHUNK)
    lanes = lambda h: slice(h * A_DK, (h + 1) * A_DK)
    na = na_ref[...]

    s_cur = {(b, h): s_scr[b * A_HEADS + h] for b in range(bsz) for h in range(A_HEADS)}
    ws, uu = {}, {}

    def rec_ws(c):
        for b in range(bsz):
            for h in range(A_HEADS):
                i = uid[b, c, h]
                ws[b, h] = _dot(wq_s[rslot, i], s_cur[b, h].astype(BF16))
                uu[b, h] = (ut_s[rslot, i] - ws[b, h][:CHUNK]).astype(BF16)

    def rec_ou(c):
        for b in range(bsz):
            for h in range(A_HEADS):
                i = uid[b, c, h]
                zpad = jnp.zeros((CHUNK, A_DV), BF16)
                u_rows = jnp.concatenate([uu[b, h] if hh == h else zpad for hh in range(A_HEADS)], axis=0)
                o = ws[b, h][CHUNK:] + _dot(aqk_s[rslot, b * nch + c], u_rows)
                s_cur[b, h] = gl_s[rslot, i] * s_cur[b, h] + _dot(kdt_s[rslot, i], uu[b, h])
                on = o * lax.rsqrt(jnp.mean(o * o, axis=-1, keepdims=True) + RMS_EPS) * na
                oa_ref[b, rows(c), lanes(h)] = (on * _silu(za_ref[b, rows(c), lanes(h)])).astype(BF16)

    rec_stages = []
    for c in range(nch):
        rec_stages += [lambda c=c: rec_ws(c), lambda c=c: rec_ou(c)]

    def run_rec(n_left_after):
        while rec_stages and len(rec_stages) > n_left_after:
            rec_stages.pop(0)()

    groups = [(b, c) for b in range(bsz) for c in range(nch)]
    pk = A_HEADS * CHUNK
    low = _lane((CHUNK, LANES)) < CHUNK
    low_row = _lane((1, LANES)) < CHUNK
    ti_p = lax.broadcasted_iota(jnp.int32, (CHUNK, pk), 0)
    ii_p = _lane((CHUNK, pk)) % CHUNK
    zero64 = jnp.zeros((CHUNK, LANES), BF16)

    def pack(parts):
        return jnp.concatenate([jnp.where(low, parts[0], parts[1]), jnp.where(low, parts[2], parts[3])], axis=-1)

    def block_diag(x16):
        blocks = []
        for h in range(A_HEADS):
            pair, first = h // 2, h % 2 == 0
            piece = jnp.where(low if first else jnp.logical_not(low), x16[:, pair * LANES:(pair + 1) * LANES], zero64)
            blocks.append(jnp.concatenate([piece, zero64] if pair == 0 else [zero64, piece], axis=-1))
        return jnp.concatenate(blocks, axis=0)

    beta, g_col, g_last, eg, dec_p, beta_p = {}, {}, {}, {}, {}, {}
    for b in range(bsz):
        gbv = gb_ref[b]
        rin = lax.broadcasted_iota(jnp.int32, gbv.shape, 0) % CHUNK
        gcs = gbv
        s = 1
        while s < CHUNK:
            gcs = gcs + jnp.where(rin >= s, pltpu.roll(gcs, s, axis=0), 0.0)
            s *= 2
        gcs_t = gcs.T
        for c in range(nch):
            r0 = c * CHUNK
            pair_lanes = slice((c // 2) * LANES, (c // 2 + 1) * LANES)
            g_rows = []
            for h in range(A_HEADS):
                u_ = (b, c, h)
                beta[u_] = jnp.broadcast_to(gbv[rows(c), h:h + 1], (CHUNK, A_DK))
                g_col[u_] = jnp.broadcast_to(gcs[rows(c), A_HEADS + h:A_HEADS + h + 1], (CHUNK, A_DK))
                g_last[u_] = gcs[r0 + CHUNK - 1:r0 + CHUNK, A_HEADS + h:A_HEADS + h + 1]
                eg[u_] = jnp.exp(g_col[u_])
                g_row = gcs_t[A_HEADS + h:A_HEADS + h + 1, pair_lanes]
                g_rows.append(g_row if c % 2 == h % 2 else pltpu.roll(g_row, CHUNK, axis=1))
            g_row_p = jnp.concatenate([jnp.where(low_row, g_rows[0], g_rows[1]),
                                       jnp.where(low_row, g_rows[2], g_rows[3])], axis=-1)
            g_col_p = pack([g_col[b, c, h] for h in range(A_HEADS)])
            dec_p[b, c] = jnp.exp(jnp.where(ti_p >= ii_p, g_col_p - g_row_p, -jnp.inf))
            beta_p[b, c] = pack([beta[b, c, h] for h in range(A_HEADS)])

    n_prep = 8
    nmat = {}
    for (b, c) in groups:
        k16 = k_ref[b, rows(c), :].astype(BF16)
        q16 = q_ref[b, rows(c), :].astype(BF16)
        k_heads = jnp.concatenate(
            [jnp.concatenate([k16[:, lanes(h)] if hh == h else zero64 for hh in range(A_HEADS)], axis=-1)
             for h in range(A_HEADS)], axis=0)
        kq = _dot_nt(jnp.concatenate([k16, q16], axis=0), k_heads)
        nmat[b, c] = -(beta_p[b, c] * kq[:CHUNK] * jnp.where(ti_p > ii_p, dec_p[b, c], 0.0))
        aqk_s[wslot, b * nch + c] = (kq[CHUNK:] * dec_p[b, c]).astype(BF16)
    run_rec(n_prep - 1)

    rsum = dict(nmat)
    pw16 = {g_: nmat[g_].astype(BF16) for g_ in groups}
    pw = {g_: _dot(pw16[g_], block_diag(pw16[g_])) for g_ in groups}
    run_rec(n_prep - 2)
    for step in range(1, 6):
        last = step == 5
        pw16 = {g_: pw[g_].astype(BF16) for g_ in groups}
        rp = {}
        for g_ in groups:
            r16 = rsum[g_].astype(BF16)
            rp[g_] = _dot(r16 if last else jnp.concatenate([r16, pw16[g_]], axis=0), block_diag(pw16[g_]))
        for g_ in groups:
            rsum[g_] = rsum[g_] + pw[g_] + rp[g_][:CHUNK]
            if not last:
                pw[g_] = rp[g_][CHUNK:]
        run_rec(n_prep - 2 - step)

    zrhs = jnp.zeros((CHUNK, 2 * A_DK), BF16)
    for (b, c, h) in units:
        u_ = (b, c, h)
        i = uid[u_]
        kc = k_ref[b, rows(c), lanes(h)]
        rhs = jnp.concatenate([(beta[u_] * eg[u_]) * kc, beta[u_] * v_ref[b, rows(c), lanes(h)]],
                              axis=-1)
        rhs16 = rhs.astype(BF16)
        rhs_rows = jnp.concatenate([rhs16 if hh == h else zrhs for hh in range(A_HEADS)], axis=0)
        sol = rhs + _dot(rsum[b, c].astype(BF16), rhs_rows)
        wq_s[wslot, i] = jnp.concatenate([sol[:, :A_DK], eg[u_] * q_ref[b, rows(c), lanes(h)]],
                                         axis=0).astype(BF16)
        ut_s[wslot, i] = sol[:, A_DK:]
        kd = jnp.exp(g_last[u_] - g_col[u_]) * kc
        kdt_s[wslot, i] = kd.T.astype(BF16)
        gl_s[wslot, i] = jnp.broadcast_to(jnp.exp(g_last[u_]), (1, A_DV))
    run_rec(0)

    for b in range(bsz):
        for h in range(A_HEADS):
            s_scr[b * A_HEADS + h] = s_cur[b, h]

    @pl.when(t == pl.num_programs(0) - 1)
    def _():
        for b in range(bsz):
            for h in range(A_HEADS):
                st_ref[b, h] = s_cur[b, h]


def _delta(q, k, v, gb, za, na_row):
    bsz, t, _ = q.shape
    ct = DELTA_CT
    nt = t // ct
    n_units = bsz * (ct // CHUNK) * A_HEADS
    prep = lambda w: pl.BlockSpec((bsz, ct, w), lambda i: (0, jnp.minimum(i, nt - 1), 0))
    rec = lambda w: pl.BlockSpec((bsz, ct, w), lambda i: (0, jnp.maximum(i - 1, 0), 0))
    return pl.pallas_call(
        _delta_kernel,
        grid=(nt + 1,),
        in_specs=[prep(A_WIDTH), prep(A_WIDTH), prep(A_WIDTH), prep(LANES), rec(A_WIDTH),
                  pl.BlockSpec((1, A_DV), lambda i: (0, 0))],
        out_specs=[rec(A_WIDTH),
                   pl.BlockSpec((bsz, A_HEADS, A_DK, A_DV), lambda i: (0, 0, 0, 0))],
        out_shape=[jax.ShapeDtypeStruct((bsz, t, A_WIDTH), BF16),
                   jax.ShapeDtypeStruct((bsz, A_HEADS, A_DK, A_DV), F32)],
        scratch_shapes=[pltpu.VMEM((bsz * A_HEADS, A_DK, A_DV), F32),
                        pltpu.VMEM((2, n_units, 2 * CHUNK, A_DK), BF16),
                        pltpu.VMEM((2, n_units, CHUNK, A_DV), F32),
                        pltpu.VMEM((2, n_units // A_HEADS, CHUNK, A_HEADS * CHUNK), BF16),
                        pltpu.VMEM((2, n_units, A_DK, CHUNK), BF16),
                        pltpu.VMEM((2, n_units, 1, A_DV), F32)],
        compiler_params=pltpu.CompilerParams(dimension_semantics=("arbitrary",),
                                             vmem_limit_bytes=VMEM_LIMIT),
        name="delta",
    )(q, k, v, gb, za, na_row)


def _swa_kernel(sink_ref, qb_ref, kc_ref, kp_ref, krc_ref, krp_ref, v0c_ref, v0p_ref, v1c_ref, v1p_ref,
                zb_ref, ob_ref):
    n = pl.program_id(1)
    tq = qb_ref.shape[1]
    blk = WINDOW
    kx = (jnp.concatenate([kp_ref[0], kc_ref[0]], axis=0), jnp.concatenate([krp_ref[0], krc_ref[0]], axis=0))
    vd = (jnp.concatenate([v0p_ref[0], v0c_ref[0]], axis=0), jnp.concatenate([v1p_ref[0], v1c_ref[0]], axis=0))

    a = lax.broadcasted_iota(jnp.int32, (2 * blk, 2 * blk), 0) % blk
    j = lax.broadcasted_iota(jnp.int32, (2 * blk, 2 * blk), 1)
    rel = a + blk - j
    band = (rel >= 0) & (rel <= WINDOW)
    band_first = band & ((n > 0) | (j >= blk))
    top = lax.broadcasted_iota(jnp.int32, (2 * blk, 1), 0) < blk
    low = _lane((blk, LANES)) < B_HD
    zero = jnp.zeros((blk, LANES), BF16)

    qrows = lambda i: slice(i * blk, (i + 1) * blk)
    krows = lambda i: slice(i * blk, (i + 2) * blk)
    sink = {(kh, half): jnp.where(top, sink_ref[kh * B_GROUP + half] * LOG2E,
                                  sink_ref[kh * B_GROUP + half + 2] * LOG2E)
            for kh in range(B_KV_HEADS) for half in range(2)}
    for i0 in range(0, tq // blk, SWA_WAVE):
        blocks = range(i0, i0 + SWA_WAVE)
        units = [(i, kh, half) for i in blocks for kh in range(B_KV_HEADS) for half in range(2)]
        sc = {}
        for (i, kh, half) in units:
            qs = []
            for g in range(2):
                grp = kh * 2 + g
                xg = qb_ref[0, qrows(i), grp * LANES:(grp + 1) * LANES]
                qs.append(jnp.where(low if half == 0 else jnp.logical_not(low), xg, zero))
            qz = jnp.concatenate(qs, axis=0)
            sc[i, kh, half] = _dot_nt(qz, kx[0 if kh == half else 1][krows(i)])
        p, den = {}, {}
        for u_ in units:
            i, kh, half = u_
            s_m = jnp.where(band_first if i == 0 else band, sc[u_], -jnp.inf)
            m = jnp.maximum(jnp.max(s_m, axis=-1, keepdims=True), sink[kh, half])
            e = jnp.exp2(s_m - m)
            den[u_] = jnp.sum(e, axis=-1, keepdims=True) + jnp.exp2(sink[kh, half] - m)
            p[u_] = e.astype(BF16)
        pv = {u_: _dot(p[u_], vd[u_[1]][krows(u_[0])]) for u_ in units}
        outs = {u_: pv[u_] / den[u_] for u_ in units}
        for i in blocks:
            for grp in range(B_WIDTH // LANES):
                kh, g = grp // 2, grp % 2
                og = jnp.where(low, outs[i, kh, 0][g * blk:(g + 1) * blk], outs[i, kh, 1][g * blk:(g + 1) * blk])
                gs = slice(grp * LANES, (grp + 1) * LANES)
                ob_ref[0, qrows(i), gs] = (og * _silu(zb_ref[0, qrows(i), gs])).astype(BF16)


def _swa(sinks, qb, kb, kbr, vd0, vd1, zb):
    bsz, t, _ = qb.shape
    tq = SWA_TQ
    per = tq // WINDOW
    cur = lambda w: pl.BlockSpec((1, tq, w), lambda b, i: (b, i, 0))
    prev = pl.BlockSpec((1, WINDOW, LANES), lambda b, i: (b, jnp.maximum(i * per - 1, 0), 0))
    return pl.pallas_call(
        _swa_kernel,
        grid=(bsz, t // tq),
        in_specs=[pl.BlockSpec(memory_space=pltpu.SMEM), cur(B_WIDTH),
                  cur(LANES), prev, cur(LANES), prev, cur(LANES), prev, cur(LANES), prev,
                  cur(B_WIDTH)],
        out_specs=cur(B_WIDTH),
        out_shape=jax.ShapeDtypeStruct((bsz, t, B_WIDTH), BF16),
        compiler_params=pltpu.CompilerParams(dimension_semantics=("arbitrary", "arbitrary"),
                                             vmem_limit_bytes=VMEM_LIMIT),
        name="swa",
    )(sinks, qb, kb, kb, kbr, kbr, vd0, vd0, vd1, vd1, zb)


def _out_kernel(oa_ref, ob_ref, x_ref, gate_ref, w_ref, g_ref, b_ref, y_ref, *, gate_per_batch):
    mix = _dot(oa_ref[0], w_ref[0:A_WIDTH, :]) + _dot(ob_ref[0], w_ref[A_WIDTH:MIX_WIDTH, :])
    gate = gate_ref[pl.ds(pl.program_id(0), 1), :] if gate_per_batch else gate_ref[...]
    r = DEEPNORM_ALPHA * x_ref[0] + (1.0 + gate) * mix
    y_ref[0] = _layer_norm(r, g_ref[...], b_ref[...])


def _out(oa, ob, x, mod, mod_row0, gate_per_batch, w_out, ln_g, ln_b, tm):
    bsz, t, _ = x.shape
    grows = 8 if gate_per_batch else tm
    row = lambda w: pl.BlockSpec((1, tm, w), lambda b, i: (b, i, 0))
    const2 = lambda s: pl.BlockSpec(s, lambda b, i: (0, 0))
    return pl.pallas_call(
        functools.partial(_out_kernel, gate_per_batch=gate_per_batch),
        grid=(bsz, t // tm),
        in_specs=[row(A_WIDTH), row(B_WIDTH), row(D_MODEL),
                  pl.BlockSpec((grows, D_MODEL), lambda b, i: (mod_row0 // grows, 2)),
                  const2((MIX_WIDTH, D_MODEL)), const2((1, D_MODEL)), const2((1, D_MODEL))],
        out_specs=row(D_MODEL),
        out_shape=jax.ShapeDtypeStruct((bsz, t, D_MODEL), F32),
        compiler_params=pltpu.CompilerParams(dimension_semantics=("arbitrary", "arbitrary"),
                                             vmem_limit_bytes=VMEM_LIMIT),
        name="out",
    )(oa, ob, x, mod, w_out, ln_g, ln_b)


def _sproj_kernel(x_ref, mod_ref, w_ref, cw_ref, cst_ref, alog_ref, dt_ref, cos_ref, sin_ref,
                  q_ref, k_ref, v_ref, za_ref, gb_ref, qb_ref, kb_ref, vb_ref, zb_ref, ncs_ref):
    shift = mod_ref[:, 0:D_MODEL]
    scale = mod_ref[:, D_MODEL:2 * D_MODEL]
    h = (x_ref[...] * (1.0 + scale) + shift).astype(BF16)

    for gi, o_ref in enumerate((q_ref, k_ref, v_ref)):
        c0 = gi * A_WIDTH
        cs = slice(c0, c0 + A_WIDTH)
        u = _dot(h, w_ref[:, cs])
        acc = cst_ref[0, :, cs] * cw_ref[0:1, cs]
        acc = acc + cst_ref[1, :, cs] * cw_ref[1:2, cs]
        acc = acc + cst_ref[2, :, cs] * cw_ref[2:3, cs]
        acc = acc + u * cw_ref[3:4, cs]
        y = _silu(acc)
        if gi == 0:
            y = _l2norm_heads(y, A_DK ** -0.5)
        elif gi == 1:
            y = _l2norm_heads(y, 1.0)
        o_ref[...] = y
        ncs_ref[0, :, cs] = cst_ref[1, :, cs]
        ncs_ref[1, :, cs] = cst_ref[2, :, cs]
        ncs_ref[2, :, cs] = u

    za_ref[...] = _dot(h, w_ref[:, C_ZA:C_ZA + A_WIDTH])
    gb_ref[...] = _gate_lanes(_dot(h, w_ref[:, C_BD:C_BD + LANES]), alog_ref[...], dt_ref[...])

    cos = cos_ref[...]
    sin = sin_ref[...]
    uq = _dot(h, w_ref[:, C_QB:C_QB + B_WIDTH])
    for g in range(B_WIDTH // LANES):
        qb_ref[:, g * LANES:(g + 1) * LANES] = (
            _rotary_group(uq[:, g * LANES:(g + 1) * LANES], cos, sin) * (B_HD ** -0.5))
    kb_ref[...] = _rotary_group(_dot(h, w_ref[:, C_KB:C_KB + LANES]), cos, sin)
    vb_ref[...] = _dot(h, w_ref[:, C_VB:C_VB + LANES])
    zb_ref[...] = _dot(h, w_ref[:, C_ZB:C_ZB + B_WIDTH])


def _sproj(x, mod_s, w_r, conv_w, cst, alog_row, dt_row, cos_row, sin_row):
    n = x.shape[0]
    full = lambda s: pl.BlockSpec(s, lambda i: (0,) * len(s))
    wide = lambda w: jax.ShapeDtypeStruct((n, w), F32)
    return pl.pallas_call(
        _sproj_kernel,
        grid=(1,),
        in_specs=[full((n, D_MODEL)), pl.BlockSpec((n, 3 * D_MODEL), lambda i: (0, 0)),
                  full((D_MODEL, W_COLS)),
                  full((CONV_W, A_QKV)), full((CONV_W - 1, n, A_QKV)),
                  full((1, LANES)), full((1, LANES)), full((1, LANES)), full((1, LANES))],
        out_specs=[full((n, A_WIDTH)), full((n, A_WIDTH)), full((n, A_WIDTH)), full((n, A_WIDTH)),
                   full((n, LANES)), full((n, B_WIDTH)), full((n, LANES)), full((n, LANES)),
                   full((n, B_WIDTH)), full((CONV_W - 1, n, A_QKV))],
        out_shape=[wide(A_WIDTH), wide(A_WIDTH), wide(A_WIDTH), wide(A_WIDTH), wide(LANES),
                   wide(B_WIDTH), wide(LANES), wide(LANES), wide(B_WIDTH),
                   jax.ShapeDtypeStruct((CONV_W - 1, n, A_QKV), F32)],
        compiler_params=pltpu.CompilerParams(dimension_semantics=("arbitrary",),
                                             vmem_limit_bytes=VMEM_LIMIT),
        name="sproj",
    )(x, mod_s, w_r, conv_w, cst, alog_row, dt_row, cos_row, sin_row)


def _sstep_kernel(sink_ref, q_ref, k_ref, v_ref, gb_ref, za_ref, na_ref, st_ref,
                  qb_ref, kn_ref, vn_ref, zb_ref, ck_ref, cv_ref,
                  oa_ref, ob_ref, nst_ref, nck_ref, ncv_ref,
                  o_scr, ob_scr):
    bt = q_ref.shape[0]
    gbv = gb_ref[...]

    for h in range(A_HEADS):
        hs = slice(h * A_DK, (h + 1) * A_DK)
        q_t = q_ref[:, hs].T
        k_t = k_ref[:, hs].T
        for bb in range(bt):
            eg = jnp.exp(gbv[bb:bb + 1, A_HEADS + h:A_HEADS + h + 1])
            beta = gbv[bb:bb + 1, h:h + 1]
            kcol = k_t[:, bb:bb + 1]
            qcol = q_t[:, bb:bb + 1]
            s1 = eg * st_ref[bb, h]
            pred = jnp.sum(kcol * s1, axis=0, keepdims=True)
            upd = beta * (v_ref[bb:bb + 1, hs] - pred)
            s2 = s1 + kcol * upd
            nst_ref[bb, h] = s2
            o_scr[bb:bb + 1, hs] = jnp.sum(qcol * s2, axis=0, keepdims=True)
    na = na_ref[...]
    for h in range(A_HEADS):
        hs = slice(h * A_DK, (h + 1) * A_DK)
        o = o_scr[:, hs]
        on = o * lax.rsqrt(jnp.mean(o * o, axis=-1, keepdims=True) + RMS_EPS) * na
        oa_ref[:, hs] = (on * _silu(za_ref[:, hs])).astype(BF16)

    row8 = lax.broadcasted_iota(jnp.int32, (B_HEADS, LANES), 0)
    lane8 = _lane((B_HEADS, LANES))
    own_half = (lane8 >= B_HD) == (row8 >= B_GROUP)
    rcol = lax.broadcasted_iota(jnp.int32, (B_HEADS, 1), 0)
    sink = jnp.zeros((B_HEADS, 1), F32)
    for r in range(B_HEADS):
        sink = jnp.where(rcol == r, sink_ref[r], sink)
    qv = qb_ref[...]
    qv_r = jnp.concatenate([pltpu.roll(qv[:, g * LANES:(g + 1) * LANES], B_HD, axis=1)
                            for g in range(B_WIDTH // LANES)], axis=-1)
    kn_t = kn_ref[...].T
    vn_t = vn_ref[...].T
    newest = _lane((LANES, WINDOW)) == WINDOW - 1
    qzs, scs = [], []
    for bb in range(bt):
        qz = jnp.zeros((B_HEADS, LANES), F32)
        for r in range(B_HEADS):
            grp, half, kh = r // 2, r % 2, r // B_GROUP
            src = qv if half == kh else qv_r
            qz = jnp.where(row8 == r, src[bb:bb + 1, grp * LANES:(grp + 1) * LANES], qz)
        qzs.append(jnp.where(own_half, qz, 0.0))
    for bb in range(bt):
        scs.append(_dot(qzs[bb], ck_ref[bb]))
    ps, pnews, dens = [], [], []
    for bb in range(bt):
        sc_new = jnp.sum(qzs[bb] * kn_ref[bb:bb + 1, :], axis=-1, keepdims=True)
        m = jnp.maximum(jnp.maximum(jnp.max(scs[bb], axis=-1, keepdims=True), sc_new), sink)
        p = jnp.exp(scs[bb] - m)
        p_new = jnp.exp(sc_new - m)
        ps.append(p)
        pnews.append(p_new)
        dens.append(jnp.sum(p, axis=-1, keepdims=True) + p_new + jnp.exp(sink - m))
    pvs = [_dot_nt(ps[bb], cv_ref[bb]) for bb in range(bt)]
    for bb in range(bt):
        o = (pvs[bb] + pnews[bb] * vn_ref[bb:bb + 1, :]) / dens[bb]
        o = jnp.where(own_half, o, 0.0)
        ob_scr[bb * B_HEADS:(bb + 1) * B_HEADS, :] = o + pltpu.roll(o, B_HD, axis=1)
    for bb in range(bt):
        nck_ref[bb] = jnp.where(newest, kn_t[:, bb:bb + 1], pltpu.roll(ck_ref[bb], WINDOW - 1, axis=1))
        ncv_ref[bb] = jnp.where(newest, vn_t[:, bb:bb + 1], pltpu.roll(cv_ref[bb], WINDOW - 1, axis=1))
    low = _lane((bt, LANES)) < B_HD
    for grp in range(B_WIDTH // LANES):
        even = ob_scr[pl.ds(2 * grp, bt, stride=B_HEADS), :]
        odd = ob_scr[pl.ds(2 * grp + 1, bt, stride=B_HEADS), :]
        gs = slice(grp * LANES, (grp + 1) * LANES)
        ob_ref[:, gs] = (jnp.where(low, even, odd) * _silu(zb_ref[:, gs])).astype(BF16)


def _sstep(sinks, q, k, v, gb, za, na_row, state, qb, kn, vn, zb, ck, cv):
    n = q.shape[0]
    bt = STEP_BT
    row = lambda w: pl.BlockSpec((bt, w), lambda i: (i, 0))
    st_spec = pl.BlockSpec((bt, A_HEADS, A_DK, A_DV), lambda i: (i, 0, 0, 0))
    c_spec = pl.BlockSpec((bt, WINDOW, LANES), lambda i: (i, 0, 0))
    return pl.pallas_call(
        _sstep_kernel,
        grid=(n // bt,),
        in_specs=[pl.BlockSpec(memory_space=pltpu.SMEM),
                  row(A_WIDTH), row(A_WIDTH), row(A_WIDTH), row(LANES), row(A_WIDTH),
                  pl.BlockSpec((1, A_DV), lambda i: (0, 0)), st_spec,
                  row(B_WIDTH), row(LANES), row(LANES), row(B_WIDTH), c_spec, c_spec],
        out_specs=[row(A_WIDTH), row(B_WIDTH), st_spec, c_spec, c_spec],
        out_shape=[jax.ShapeDtypeStruct((n, A_WIDTH), BF16),
                   jax.ShapeDtypeStruct((n, B_WIDTH), BF16),
                   jax.ShapeDtypeStruct((n, A_HEADS, A_DK, A_DV), F32),
                   jax.ShapeDtypeStruct((n, WINDOW, LANES), F32),
                   jax.ShapeDtypeStruct((n, WINDOW, LANES), F32)],
        scratch_shapes=[pltpu.VMEM((bt, A_WIDTH), F32), pltpu.VMEM((bt * B_HEADS, LANES), F32)],
        compiler_params=pltpu.CompilerParams(dimension_semantics=("arbitrary",),
                                             vmem_limit_bytes=VMEM_LIMIT),
        name="sstep",
    )(sinks, q, k, v, gb, za, na_row, state, qb, kn, vn, zb, ck, cv)


def _rope_tables(pos):
    half = B_HD // 2
    inv = 1.0 / (ROPE_THETA ** (np.arange(half, dtype=np.float64) / half))
    ang = np.asarray(pos, np.float64)[:, None] * inv[None, :]
    cos, sin = np.cos(ang), np.sin(ang)
    reps = LANES // B_HD
    return (jnp.asarray(np.tile(np.concatenate([cos, cos], -1), (1, reps)), F32),
            jnp.asarray(np.tile(np.concatenate([-sin, sin], -1), (1, reps)), F32))


def _pad_row(vec, offset):
    return jnp.zeros((1, LANES), F32).at[0, offset:offset + vec.shape[0]].set(vec.astype(F32))


def _layer(x_prompt, x_sample, state_conv, state_delta, cache_k, cache_v, c_prompt, c_sample,
           w_ada, b_ada, w_in, conv_w, a_log, dt_bias, norm_a, sinks, w_out, ln_g, ln_b):
    bsz, seq, _ = x_prompt.shape
    n_s = x_sample.shape[0]

    w_t = jnp.swapaxes(w_in, 0, 1)
    w_r = jnp.swapaxes(jnp.concatenate([
        w_t[0:OFF_A_BETA], w_t[OFF_B_Q:PROJ_COLS], w_t[OFF_A_BETA:OFF_B_Q],
        jnp.zeros((LANES - 2 * A_HEADS, D_MODEL), w_in.dtype)], axis=0).astype(BF16), 0, 1)
    w_o = w_out.astype(BF16)
    alog_row = _pad_row(a_log, A_HEADS)
    dt_row = _pad_row(dt_bias, A_HEADS)
    na_row = norm_a.reshape(1, A_DV)
    g_row = ln_g.reshape(1, D_MODEL)
    b_row = ln_b.reshape(1, D_MODEL)

    assert n_s % 8 == 0 and bsz <= 8
    c_all = jnp.concatenate([c_sample, c_prompt, jnp.zeros((8 - bsz, D_MODEL), F32)], axis=0)
    mod = _ada(c_all, w_ada, b_ada.reshape(1, 3 * D_MODEL))

    cos_p, sin_p = _rope_tables(np.arange(seq))
    (q, k, v, za, gb, qb, kb, kbr, vd0, vd1, zb, conv_p, kb_last, vb_last) = _proj(
        x_prompt, mod, n_s, w_r, conv_w, alog_row, dt_row, cos_p, sin_p)
    oa, delta_p = _delta(q, k, v, gb, za, na_row)
    ob = _swa(sinks, qb, kb, kbr, vd0, vd1, zb)
    y_p = _out(oa, ob, x_prompt, mod, n_s, True, w_o, g_row, b_row, OUT_TM)
    swa_k_p = kb_last.reshape(bsz, WINDOW, B_KV_HEADS, B_HD)
    swa_v_p = vb_last.reshape(bsz, WINDOW, B_KV_HEADS, B_HD)

    cos_s, sin_s = _rope_tables(np.array([PAST_LEN]))
    xs = x_sample.reshape(n_s, D_MODEL)
    cst = jnp.transpose(state_conv, (1, 0, 2))
    sq, sk, sv, sza, sgb, sqb, skn, svn, szb, ncs = _sproj(xs, mod, w_r, conv_w, cst, alog_row, dt_row,
                                                           cos_s, sin_s)
    soa, sob, delta_s, nck, ncv = _sstep(sinks, sq, sk, sv, sgb, sza, na_row, state_delta,
                                         sqb, skn, svn, szb,
                                         jnp.swapaxes(cache_k.reshape(n_s, WINDOW, LANES), 1, 2),
                                         jnp.swapaxes(cache_v.reshape(n_s, WINDOW, LANES), 1, 2))
    y_s = _out(soa[None], sob[None], xs[None], mod, 0, False, w_o, g_row, b_row, n_s)
    conv_s = jnp.transpose(ncs, (1, 0, 2))
    unpack = lambda c: jnp.swapaxes(c, 1, 2).reshape(n_s, WINDOW, B_KV_HEADS, B_HD)
    return (y_p, y_s.reshape(n_s, 1, D_MODEL), conv_p, delta_p, swa_k_p, swa_v_p,
            conv_s, delta_s, unpack(nck), unpack(ncv))


def kernel(x_prompt, x_sample, state_conv, state_delta, cache_swa_k, cache_swa_v, c_prompt, c_sample,
           w_ada, b_ada, w_in, conv_w, a_log, dt_bias, norm_a, sinks, w_out, ln_g, ln_b):
    assert w_ada.shape[0] == DEPTH == 1
    outs = _layer(x_prompt, x_sample, state_conv[0], state_delta[0], cache_swa_k[0], cache_swa_v[0],
                  c_prompt, c_sample, w_ada[0], b_ada[0], w_in[0], conv_w[0], a_log[0], dt_bias[0],
                  norm_a[0], sinks[0], w_out[0], ln_g[0], ln_b[0])
    y_p, y_s = outs[0], outs[1]
    return (y_p, y_s) + tuple(o[None] for o in outs[2:])
```

```python
import functools

import jax
import jax.numpy as jnp
import numpy as np
from jax import lax
from jax.experimental import pallas as pl
from jax.experimental.pallas import tpu as pltpu

F32 = jnp.float32
BF16 = jnp.bfloat16

D_MODEL = 1024
DEPTH = 1
PAST_LEN = 8192
A_HEADS = 4
A_DK = 128
A_DV = 128
A_WIDTH = A_HEADS * A_DV
A_QKV = 3 * A_WIDTH
CONV_W = 4
CHUNK = 64
B_HEADS = 8
B_KV_HEADS = 2
B_HD = 64
B_GROUP = B_HEADS // B_KV_HEADS
B_WIDTH = B_HEADS * B_HD
B_KV_WIDTH = B_KV_HEADS * B_HD
WINDOW = 128
ROPE_THETA = 10000.0
MIX_WIDTH = A_WIDTH + B_WIDTH
DEEPNORM_ALPHA = (2 * DEPTH) ** 0.25
LOG2E = 1.4426950408889634
LN_EPS = 1e-5
RMS_EPS = 1e-6
L2_EPS = 1e-6

OFF_A_Z = A_QKV
OFF_A_BETA = OFF_A_Z + A_WIDTH
OFF_A_DECAY = OFF_A_BETA + A_HEADS
OFF_B_Q = OFF_A_DECAY + A_HEADS
OFF_B_K = OFF_B_Q + B_WIDTH
OFF_B_V = OFF_B_K + B_KV_WIDTH
OFF_B_Z = OFF_B_V + B_KV_WIDTH
PROJ_COLS = OFF_B_Z + B_WIDTH

LANES = 128
C_QKV = 0
C_ZA = C_QKV + A_QKV
C_QB = C_ZA + A_WIDTH
C_KB = C_QB + B_WIDTH
C_VB = C_KB + B_KV_WIDTH
C_ZB = C_VB + B_KV_WIDTH
C_BD = C_ZB + B_WIDTH
WPREP_TN = 256
W_COLS = C_BD + WPREP_TN

VMEM_LIMIT = 56 * 1024 * 1024

PROJ_TM = 512
DELTA_CT = 256
SWA_TQ = 512
SWA_WAVE = 1
OUT_TM = 1024
STEP_BT = 16


def _dot(a, b):
    return jnp.dot(a, b, preferred_element_type=F32)


def _dot_nt(a, b):
    return lax.dot_general(a, b, (((1,), (1,)), ((), ())), preferred_element_type=F32)


def _silu(x):
    return x * jax.nn.sigmoid(x)


def _softplus(x):
    return jnp.maximum(x, 0.0) + jnp.log1p(jnp.exp(-jnp.abs(x)))


def _lane(shape):
    return lax.broadcasted_iota(jnp.int32, shape, len(shape) - 1)


def _l2norm_heads(y, scale):
    outs = []
    for h in range(A_HEADS):
        xh = y[:, h * A_DK:(h + 1) * A_DK]
        ss = jnp.sum(xh * xh, axis=-1, keepdims=True)
        xn = xh * lax.rsqrt(ss + L2_EPS)
        outs.append(xn * scale if scale != 1.0 else xn)
    return jnp.concatenate(outs, axis=-1)


def _rotary_group(xg, cos, sin_signed):
    lane = _lane(xg.shape)
    swapped = jnp.where((lane % B_HD) < (B_HD // 2),
                        pltpu.roll(xg, LANES - B_HD // 2, axis=1),
                        pltpu.roll(xg, B_HD // 2, axis=1))
    return xg * cos + swapped * sin_signed


def _kv_layouts(kb, vb):
    low = _lane(kb.shape) < B_HD
    kbr = pltpu.roll(kb, B_HD, axis=1)
    vbr = pltpu.roll(vb, B_HD, axis=1)
    return kb, kbr, jnp.where(low, vb, vbr), jnp.where(low, vbr, vb)


def _gate_lanes(bd, alog_row, dt_row):
    lane = _lane(bd.shape)
    g = -jnp.exp(alog_row) * _softplus(bd + dt_row)
    return jnp.where(lane < A_HEADS, jax.nn.sigmoid(bd), g)


def _layer_norm(r, g, b):
    mu = jnp.mean(r, axis=-1, keepdims=True)
    d = r - mu
    var = jnp.mean(d * d, axis=-1, keepdims=True)
    return d * lax.rsqrt(var + LN_EPS) * g + b


def _wprep_kernel(wt_ref, o_ref):
    x = wt_ref[...]
    tail = pl.program_id(0) == pl.num_programs(0) - 1
    row = lax.broadcasted_iota(jnp.int32, x.shape, 0)
    x = jnp.where(jnp.logical_and(tail, row >= 2 * A_HEADS), 0.0, x)
    o_ref[...] = x.T.astype(BF16)


def _wprep(w_t):
    tn = WPREP_TN
    n_a, n_b = OFF_A_BETA // tn, (PROJ_COLS - OFF_B_Q) // tn
    assert n_a * tn == OFF_A_BETA and n_b * tn == PROJ_COLS - OFF_B_Q and OFF_A_BETA + tn <= PROJ_COLS

    def src_row(j):
        return jnp.where(j < n_a, j * tn, jnp.where(j < n_a + n_b, OFF_B_Q + (j - n_a) * tn, OFF_A_BETA))

    return pl.pallas_call(
        _wprep_kernel,
        grid=(n_a + n_b + 1,),
        in_specs=[pl.BlockSpec((pl.Element(tn), pl.Element(D_MODEL)),
                               lambda j: (pl.multiple_of(src_row(j), 8), 0))],
        out_specs=pl.BlockSpec((D_MODEL, tn), lambda j: (0, j)),
        out_shape=jax.ShapeDtypeStruct((D_MODEL, W_COLS), BF16),
        compiler_params=pltpu.CompilerParams(dimension_semantics=("arbitrary",),
                                             vmem_limit_bytes=VMEM_LIMIT),
        name="wprep",
    )(w_t)


def _ada_kernel(c_ref, w_ref, b_ref, o_ref):
    o_ref[...] = _dot(c_ref[...].astype(BF16), w_ref[...].astype(BF16)) + b_ref[...]


def _ada(c_all, w_ada, b_ada):
    rows = c_all.shape[0]
    tn = 768
    return pl.pallas_call(
        _ada_kernel,
        grid=(3 * D_MODEL // tn,),
        in_specs=[pl.BlockSpec((rows, D_MODEL), lambda j: (0, 0)),
                  pl.BlockSpec((D_MODEL, tn), lambda j: (0, j)),
                  pl.BlockSpec((1, tn), lambda j: (0, j))],
        out_specs=pl.BlockSpec((rows, tn), lambda j: (0, j)),
        out_shape=jax.ShapeDtypeStruct((rows, 3 * D_MODEL), F32),
        compiler_params=pltpu.CompilerParams(dimension_semantics=("arbitrary",),
                                             vmem_limit_bytes=VMEM_LIMIT),
        name="ada",
    )(c_all, w_ada, b_ada)


def _proj_kernel(x_ref, mod_ref, w_ref, cw_ref, alog_ref, dt_ref, cos_ref, sin_ref,
                 q_ref, k_ref, v_ref, za_ref, gb_ref, qb_ref, kb_ref, kbr_ref, vd0_ref, vd1_ref,
                 zb_ref, cst_ref, kbl_ref, vbl_ref, ubuf):
    tm = x_ref.shape[1]
    t = pl.program_id(1)

    @pl.when(t == 0)
    def _():
        ubuf[...] = jnp.zeros(ubuf.shape, F32)

    brow = pl.ds(pl.program_id(0), 1)
    shift = mod_ref[brow, 0:D_MODEL]
    scale = mod_ref[brow, D_MODEL:2 * D_MODEL]
    h = (x_ref[0] * (1.0 + scale) + shift).astype(BF16)

    sub = lax.broadcasted_iota(jnp.int32, (tm // 8, 8, A_WIDTH), 1)
    for gi, o_ref in enumerate((q_ref, k_ref, v_ref)):
        cs = slice(gi * A_WIDTH, (gi + 1) * A_WIDTH)
        u = _dot(h, w_ref[:, cs])
        groups = jnp.concatenate([ubuf[:, cs], u], axis=0).reshape(tm // 8 + 1, 8, A_WIDTH)
        acc = None
        for j in range(CONV_W - 1, 0, -1):
            rot = pltpu.roll(groups, j, axis=1)
            term = (jnp.where(sub < j, rot[:-1], rot[1:]).reshape(tm, A_WIDTH)
                    * cw_ref[CONV_W - 1 - j:CONV_W - j, cs])
            acc = term if acc is None else acc + term
        y = _silu(acc + u * cw_ref[CONV_W - 1:CONV_W, cs])
        if gi == 0:
            y = _l2norm_heads(y, A_DK ** -0.5)
        elif gi == 1:
            y = _l2norm_heads(y, 1.0)
        o_ref[0] = y
        ubuf[:, cs] = u[tm - 8:tm]
        cst_ref[0, :, cs] = u[tm - (CONV_W - 1):tm]

    za_ref[0] = _dot(h, w_ref[:, C_ZA:C_ZA + A_WIDTH])
    gb_ref[0] = _gate_lanes(_dot(h, w_ref[:, C_BD:C_BD + LANES]), alog_ref[...], dt_ref[...])
    cos = cos_ref[...]
    sin = sin_ref[...]
    uq = _dot(h, w_ref[:, C_QB:C_QB + B_WIDTH])
    for g in range(B_WIDTH // LANES):
        qb_ref[0, :, g * LANES:(g + 1) * LANES] = (
            _rotary_group(uq[:, g * LANES:(g + 1) * LANES], cos, sin) * (B_HD ** -0.5 * LOG2E)).astype(BF16)
    ukv = _dot(h, w_ref[:, C_KB:C_KB + 2 * LANES])
    kb = _rotary_group(ukv[:, 0:LANES], cos, sin)
    vb = ukv[:, LANES:2 * LANES]
    for o_ref, val in zip((kb_ref, kbr_ref, vd0_ref, vd1_ref), _kv_layouts(kb, vb)):
        o_ref[0] = val.astype(BF16)
    zb_ref[0] = _dot(h, w_ref[:, C_ZB:C_ZB + B_WIDTH])

    @pl.when(t == pl.num_programs(1) - 1)
    def _():
        kbl_ref[0] = kb[tm - WINDOW:tm]
        vbl_ref[0] = vb[tm - WINDOW:tm]


def _proj(x, mod, mod_row0, w_r, conv_w, alog_row, dt_row, cos_t, sin_t):
    bsz, t, _ = x.shape
    tm = PROJ_TM
    row = lambda w: pl.BlockSpec((1, tm, w), lambda b, i: (b, i, 0))
    const2 = lambda s: pl.BlockSpec(s, lambda b, i: (0, 0))
    per_b = lambda r, w: pl.BlockSpec((1, r, w), lambda b, i: (b, 0, 0))
    wide = lambda w, dt=F32: jax.ShapeDtypeStruct((bsz, t, w), dt)
    return pl.pallas_call(
        _proj_kernel,
        grid=(bsz, t // tm),
        in_specs=[row(D_MODEL),
                  pl.BlockSpec((8, 3 * D_MODEL), lambda b, i: (mod_row0 // 8, 0)),
                  const2((D_MODEL, W_COLS)),
                  const2((CONV_W, A_QKV)),
                  const2((1, LANES)), const2((1, LANES)),
                  pl.BlockSpec((tm, LANES), lambda b, i: (i, 0)),
                  pl.BlockSpec((tm, LANES), lambda b, i: (i, 0))],
        out_specs=[row(A_WIDTH), row(A_WIDTH), row(A_WIDTH), row(A_WIDTH), row(LANES),
                   row(B_WIDTH), row(LANES), row(LANES), row(LANES), row(LANES), row(B_WIDTH),
                   per_b(CONV_W - 1, A_QKV), per_b(WINDOW, LANES), per_b(WINDOW, LANES)],
        out_shape=[wide(A_WIDTH), wide(A_WIDTH), wide(A_WIDTH), wide(A_WIDTH), wide(LANES),
                   wide(B_WIDTH, BF16), wide(LANES, BF16), wide(LANES, BF16), wide(LANES, BF16),
                   wide(LANES, BF16), wide(B_WIDTH),
                   jax.ShapeDtypeStruct((bsz, CONV_W - 1, A_QKV), F32),
                   jax.ShapeDtypeStruct((bsz, WINDOW, LANES), F32),
                   jax.ShapeDtypeStruct((bsz, WINDOW, LANES), F32)],
        scratch_shapes=[pltpu.VMEM((8, A_QKV), F32)],
        compiler_params=pltpu.CompilerParams(dimension_semantics=("arbitrary", "arbitrary"),
                                             vmem_limit_bytes=VMEM_LIMIT),
        name="proj",
    )(x, mod, w_r, conv_w, alog_row, dt_row, cos_t, sin_t)


def _delta_kernel(q_ref, k_ref, v_ref, gb_ref, za_ref, na_ref, oa_ref, st_ref,
                  s_scr, wq_s, ut_s, akd_s, gl_s):
    bsz, ct = q_ref.shape[0], q_ref.shape[1]
    nch = ct // CHUNK
    t = pl.program_id(0)
    wslot = t % 2
    rslot = 1 - wslot

    @pl.when(t == 0)
    def _():
        s_scr[...] = jnp.zeros(s_scr.shape, F32)
        wq_s[...] = jnp.zeros(wq_s.shape, BF16)
        ut_s[...] = jnp.zeros(ut_s.shape, F32)
        akd_s[...] = jnp.zeros(akd_s.shape, BF16)
        gl_s[...] = jnp.zeros(gl_s.shape, F32)

    units = [(b, c, h) for b in range(bsz) for c in range(nch) for h in range(A_HEADS)]
    uid = {u_: i for i, u_ in enumerate(units)}
    rows = lambda c: slice(c * CHUNK, (c + 1) * CHUNK)
    lanes = lambda h: slice(h * A_DK, (h + 1) * A_DK)
    na = na_ref[...]

    s_cur = {(b, h): s_scr[b * A_HEADS + h] for b in range(bsz) for h in range(A_HEADS)}
    ws, uu = {}, {}

    def rec_ws(c):
        for b in range(bsz):
            for h in range(A_HEADS):
                i = uid[b, c, h]
                ws[b, h] = _dot(wq_s[rslot, i], s_cur[b, h].astype(BF16))
                uu[b, h] = (ut_s[rslot, i] - ws[b, h][:CHUNK]).astype(BF16)

    def rec_ou(c):
        zpad = jnp.zeros((CHUNK, A_DV), BF16)
        for b in range(bsz):
            u_bd = jnp.concatenate(
                [jnp.concatenate([uu[b, h] if hh == h else zpad for hh in range(A_HEADS)], axis=-1)
                 for h in range(A_HEADS)], axis=0)
            ou = _dot(akd_s[rslot, b * nch + c], u_bd)
            for h in range(A_HEADS):
                o = ws[b, h][CHUNK:] + ou[:CHUNK, lanes(h)]
                s_cur[b, h] = gl_s[rslot, uid[b, c, h]] * s_cur[b, h] + ou[CHUNK:, lanes(h)]
                on = o * lax.rsqrt(jnp.mean(o * o, axis=-1, keepdims=True) + RMS_EPS) * na
                oa_ref[b, rows(c), lanes(h)] = (on * _silu(za_ref[b, rows(c), lanes(h)])).astype(BF16)

    rec_stages = []
    for c in range(nch):
        rec_stages += [lambda c=c: rec_ws(c), lambda c=c: rec_ou(c)]

    def run_rec(n_left_after):
        while rec_stages and len(rec_stages) > n_left_after:
            rec_stages.pop(0)()

    groups = [(b, c) for b in range(bsz) for c in range(nch)]
    pk = A_HEADS * CHUNK
    low = _lane((CHUNK, LANES)) < CHUNK
    low_row = _lane((1, LANES)) < CHUNK
    ti_p = lax.broadcasted_iota(jnp.int32, (CHUNK, pk), 0)
    ii_p = _lane((CHUNK, pk)) % CHUNK
    zero64 = jnp.zeros((CHUNK, LANES), BF16)

    def pack(parts):
        return jnp.concatenate([jnp.where(low, parts[0], parts[1]), jnp.where(low, parts[2], parts[3])], axis=-1)

    def block_diag(x16):
        blocks = []
        for h in range(A_HEADS):
            pair, first = h // 2, h % 2 == 0
            piece = jnp.where(low if first else jnp.logical_not(low), x16[:, pair * LANES:(pair + 1) * LANES], zero64)
            blocks.append(jnp.concatenate([piece, zero64] if pair == 0 else [zero64, piece], axis=-1))
        return jnp.concatenate(blocks, axis=0)

    zrhs = jnp.zeros((CHUNK, 2 * A_DK), BF16)
    n_rec = len(rec_stages)
    n_slots = 8 * bsz
    done = [0]

    def stage_done():
        done[0] += 1
        run_rec(n_rec - (done[0] * n_rec) // n_slots)

    def prepare(b):
        groups_b = [(b, c) for c in range(nch)]
        beta, g_col, g_last, eg, dec_p, beta_p = {}, {}, {}, {}, {}, {}
        gbv = gb_ref[b]
        rin = lax.broadcasted_iota(jnp.int32, gbv.shape, 0) % CHUNK
        gcs = gbv
        s = 1
        while s < CHUNK:
            gcs = gcs + jnp.where(rin >= s, pltpu.roll(gcs, s, axis=0), 0.0)
            s *= 2
        gcs_t = gcs.T
        for c in range(nch):
            r0 = c * CHUNK
            pair_lanes = slice((c // 2) * LANES, (c // 2 + 1) * LANES)
            g_rows = []
            for h in range(A_HEADS):
                u_ = (b, c, h)
                beta[u_] = jnp.broadcast_to(gbv[rows(c), h:h + 1], (CHUNK, A_DK))
                g_col[u_] = jnp.broadcast_to(gcs[rows(c), A_HEADS + h:A_HEADS + h + 1], (CHUNK, A_DK))
                g_last[u_] = gcs[r0 + CHUNK - 1:r0 + CHUNK, A_HEADS + h:A_HEADS + h + 1]
                eg[u_] = jnp.exp(g_col[u_])
                g_row = gcs_t[A_HEADS + h:A_HEADS + h + 1, pair_lanes]
                g_rows.append(g_row if c % 2 == h % 2 else pltpu.roll(g_row, CHUNK, axis=1))
            g_row_p = jnp.concatenate([jnp.where(low_row, g_rows[0], g_rows[1]),
                                       jnp.where(low_row, g_rows[2], g_rows[3])], axis=-1)
            g_col_p = pack([g_col[b, c, h] for h in range(A_HEADS)])
            dec_p[b, c] = jnp.exp(jnp.where(ti_p >= ii_p, g_col_p - g_row_p, -jnp.inf))
            beta_p[b, c] = pack([beta[b, c, h] for h in range(A_HEADS)])

        nmat = {}
        for (_, c) in groups_b:
            k16 = k_ref[b, rows(c), :].astype(BF16)
            q16 = q_ref[b, rows(c), :].astype(BF16)
            k_heads = jnp.concatenate(
                [jnp.concatenate([k16[:, lanes(h)] if hh == h else zero64 for hh in range(A_HEADS)], axis=-1)
                 for h in range(A_HEADS)], axis=0)
            kq = _dot_nt(jnp.concatenate([k16, q16], axis=0), k_heads)
            nmat[b, c] = -(beta_p[b, c] * kq[:CHUNK] * jnp.where(ti_p > ii_p, dec_p[b, c], 0.0))
            akd_s[wslot, b * nch + c, 0:CHUNK, :] = (kq[CHUNK:] * dec_p[b, c]).astype(BF16)
        stage_done()

        rsum = dict(nmat)
        pw16 = {g_: nmat[g_].astype(BF16) for g_ in groups_b}
        pw = {g_: _dot(pw16[g_], block_diag(pw16[g_])) for g_ in groups_b}
        stage_done()
        for step in range(1, 6):
            last = step == 5
            pw16 = {g_: pw[g_].astype(BF16) for g_ in groups_b}
            rp = {}
            for g_ in groups_b:
                r16 = rsum[g_].astype(BF16)
                rp[g_] = _dot(r16 if last else jnp.concatenate([r16, pw16[g_]], axis=0), block_diag(pw16[g_]))
            for g_ in groups_b:
                rsum[g_] = rsum[g_] + pw[g_] + rp[g_][:CHUNK]
                if not last:
                    pw[g_] = rp[g_][CHUNK:]
            stage_done()

        for (_, c) in groups_b:
            for h in range(A_HEADS):
                u_ = (b, c, h)
                i = uid[u_]
                kc = k_ref[b, rows(c), lanes(h)]
                rhs = jnp.concatenate([(beta[u_] * eg[u_]) * kc, beta[u_] * v_ref[b, rows(c), lanes(h)]],
                                      axis=-1)
                rhs16 = rhs.astype(BF16)
                rhs_rows = jnp.concatenate([rhs16 if hh == h else zrhs for hh in range(A_HEADS)], axis=0)
                sol = rhs + _dot(rsum[b, c].astype(BF16), rhs_rows)
                wq_s[wslot, i] = jnp.concatenate([sol[:, :A_DK], eg[u_] * q_ref[b, rows(c), lanes(h)]],
                                                 axis=0).astype(BF16)
                ut_s[wslot, i] = sol[:, A_DK:]
                gl_s[wslot, i] = jnp.broadcast_to(jnp.exp(g_last[u_]), (1, A_DV))
        for (_, c) in groups_b:
            kd = [jnp.exp(g_last[b, c, h] - g_col[b, c, h]) * k_ref[b, rows(c), lanes(h)]
                  for h in range(A_HEADS)]
            for p in range(A_HEADS // 2):
                akd_s[wslot, b * nch + c, CHUNK:, p * LANES:(p + 1) * LANES] = (
                    jnp.concatenate([kd[2 * p], kd[2 * p + 1]], axis=0).T.astype(BF16))
        stage_done()

    for b in range(bsz):
        prepare(b)
    run_rec(0)

    for b in range(bsz):
        for h in range(A_HEADS):
            s_scr[b * A_HEADS + h] = s_cur[b, h]

    @pl.when(t == pl.num_programs(0) - 1)
    def _():
        for b in range(bsz):
            for h in range(A_HEADS):
                st_ref[b, h] = s_cur[b, h]


def _delta(q, k, v, gb, za, na_row):
    bsz, t, _ = q.shape
    ct = DELTA_CT
    nt = t // ct
    n_units = bsz * (ct // CHUNK) * A_HEADS
    prep = lambda w: pl.BlockSpec((bsz, ct, w), lambda i: (0, jnp.minimum(i, nt - 1), 0))
    rec = lambda w: pl.BlockSpec((bsz, ct, w), lambda i: (0, jnp.maximum(i - 1, 0), 0))
    return pl.pallas_call(
        _delta_kernel,
        grid=(nt + 1,),
        in_specs=[prep(A_WIDTH), prep(A_WIDTH), prep(A_WIDTH), prep(LANES), rec(A_WIDTH),
                  pl.BlockSpec((1, A_DV), lambda i: (0, 0))],
        out_specs=[rec(A_WIDTH),
                   pl.BlockSpec((bsz, A_HEADS, A_DK, A_DV), lambda i: (0, 0, 0, 0))],
        out_shape=[jax.ShapeDtypeStruct((bsz, t, A_WIDTH), BF16),
                   jax.ShapeDtypeStruct((bsz, A_HEADS, A_DK, A_DV), F32)],
        scratch_shapes=[pltpu.VMEM((bsz * A_HEADS, A_DK, A_DV), F32),
                        pltpu.VMEM((2, n_units, 2 * CHUNK, A_DK), BF16),
                        pltpu.VMEM((2, n_units, CHUNK, A_DV), F32),
                        pltpu.VMEM((2, n_units // A_HEADS, CHUNK + A_DK, A_HEADS * CHUNK), BF16),
                        pltpu.VMEM((2, n_units, 1, A_DV), F32)],
        compiler_params=pltpu.CompilerParams(dimension_semantics=("arbitrary",),
                                             vmem_limit_bytes=VMEM_LIMIT),
        name="delta",
    )(q, k, v, gb, za, na_row)


def _swa_kernel(sink_ref, qb_ref, kc_ref, kp_ref, krc_ref, krp_ref, v0c_ref, v0p_ref, v1c_ref, v1p_ref,
                zb_ref, ob_ref):
    n = pl.program_id(1)
    tq = qb_ref.shape[1]
    blk = WINDOW
    kx = (jnp.concatenate([kp_ref[0], kc_ref[0]], axis=0), jnp.concatenate([krp_ref[0], krc_ref[0]], axis=0))
    vd = (jnp.concatenate([v0p_ref[0], v0c_ref[0]], axis=0), jnp.concatenate([v1p_ref[0], v1c_ref[0]], axis=0))

    a = lax.broadcasted_iota(jnp.int32, (2 * blk, 2 * blk), 0) % blk
    j = lax.broadcasted_iota(jnp.int32, (2 * blk, 2 * blk), 1)
    rel = a + blk - j
    band = (rel >= 0) & (rel <= WINDOW)
    band_first = band & ((n > 0) | (j >= blk))
    top = lax.broadcasted_iota(jnp.int32, (2 * blk, 1), 0) < blk
    low = _lane((blk, LANES)) < B_HD
    zero = jnp.zeros((blk, LANES), BF16)

    qrows = lambda i: slice(i * blk, (i + 1) * blk)
    krows = lambda i: slice(i * blk, (i + 2) * blk)
    sink = {(kh, half): jnp.where(top, sink_ref[kh * B_GROUP + half] * LOG2E,
                                  sink_ref[kh * B_GROUP + half + 2] * LOG2E)
            for kh in range(B_KV_HEADS) for half in range(2)}
    for i0 in range(0, tq // blk, SWA_WAVE):
        blocks = range(i0, i0 + SWA_WAVE)
        units = [(i, kh, half) for i in blocks for kh in range(B_KV_HEADS) for half in range(2)]
        sc = {}
        for (i, kh, half) in units:
            qs = []
            for g in range(2):
                grp = kh * 2 + g
                xg = qb_ref[0, qrows(i), grp * LANES:(grp + 1) * LANES]
                qs.append(jnp.where(low if half == 0 else jnp.logical_not(low), xg, zero))
            qz = jnp.concatenate(qs, axis=0)
            sc[i, kh, half] = _dot_nt(qz, kx[0 if kh == half else 1][krows(i)])
        p, den = {}, {}
        for u_ in units:
            i, kh, half = u_
            s_m = jnp.where(band_first if i == 0 else band, sc[u_], -jnp.inf)
            m = jnp.maximum(jnp.max(s_m, axis=-1, keepdims=True), sink[kh, half])
            e = jnp.exp2(s_m - m)
            den[u_] = jnp.sum(e, axis=-1, keepdims=True) + jnp.exp2(sink[kh, half] - m)
            p[u_] = e.astype(BF16)
        pv = {u_: _dot(p[u_], vd[u_[1]][krows(u_[0])]) for u_ in units}
        outs = {u_: pv[u_] / den[u_] for u_ in units}
        for i in blocks:
            for grp in range(B_WIDTH // LANES):
                kh, g = grp // 2, grp % 2
                og = jnp.where(low, outs[i, kh, 0][g * blk:(g + 1) * blk], outs[i, kh, 1][g * blk:(g + 1) * blk])
                gs = slice(grp * LANES, (grp + 1) * LANES)
                ob_ref[0, qrows(i), gs] = (og * _silu(zb_ref[0, qrows(i), gs])).astype(BF16)


def _swa(sinks, qb, kb, kbr, vd0, vd1, zb):
    bsz, t, _ = qb.shape
    tq = SWA_TQ
    per = tq // WINDOW
    cur = lambda w: pl.BlockSpec((1, tq, w), lambda b, i: (b, i, 0))
    prev = pl.BlockSpec((1, WINDOW, LANES), lambda b, i: (b, jnp.maximum(i * per - 1, 0), 0))
    return pl.pallas_call(
        _swa_kernel,
        grid=(bsz, t // tq),
        in_specs=[pl.BlockSpec(memory_space=pltpu.SMEM), cur(B_WIDTH),
                  cur(LANES), prev, cur(LANES), prev, cur(LANES), prev, cur(LANES), prev,
                  cur(B_WIDTH)],
        out_specs=cur(B_WIDTH),
        out_shape=jax.ShapeDtypeStruct((bsz, t, B_WIDTH), BF16),
        compiler_params=pltpu.CompilerParams(dimension_semantics=("arbitrary", "arbitrary"),
                                             vmem_limit_bytes=VMEM_LIMIT),
        name="swa",
    )(sinks, qb, kb, kb, kbr, kbr, vd0, vd0, vd1, vd1, zb)


def _out_kernel(oa_ref, ob_ref, x_ref, gate_ref, w_ref, g_ref, b_ref, y_ref, *, gate_per_batch):
    mix = _dot(oa_ref[0], w_ref[0:A_WIDTH, :]) + _dot(ob_ref[0], w_ref[A_WIDTH:MIX_WIDTH, :])
    gate = gate_ref[pl.ds(pl.program_id(0), 1), :] if gate_per_batch else gate_ref[...]
    r = DEEPNORM_ALPHA * x_ref[0] + (1.0 + gate) * mix
    y_ref[0] = _layer_norm(r, g_ref[...], b_ref[...])


def _out(oa, ob, x, mod, mod_row0, gate_per_batch, w_out, ln_g, ln_b, tm):
    bsz, t, _ = x.shape
    grows = 8 if gate_per_batch else tm
    row = lambda w: pl.BlockSpec((1, tm, w), lambda b, i: (b, i, 0))
    const2 = lambda s: pl.BlockSpec(s, lambda b, i: (0, 0))
    return pl.pallas_call(
        functools.partial(_out_kernel, gate_per_batch=gate_per_batch),
        grid=(bsz, t // tm),
        in_specs=[row(A_WIDTH), row(B_WIDTH), row(D_MODEL),
                  pl.BlockSpec((grows, D_MODEL), lambda b, i: (mod_row0 // grows, 2)),
                  const2((MIX_WIDTH, D_MODEL)), const2((1, D_MODEL)), const2((1, D_MODEL))],
        out_specs=row(D_MODEL),
        out_shape=jax.ShapeDtypeStruct((bsz, t, D_MODEL), F32),
        compiler_params=pltpu.CompilerParams(dimension_semantics=("arbitrary", "arbitrary"),
                                             vmem_limit_bytes=VMEM_LIMIT),
        name="out",
    )(oa, ob, x, mod, w_out, ln_g, ln_b)


def _sproj_kernel(x_ref, mod_ref, w_ref, cw_ref, cst_ref, alog_ref, dt_ref, cos_ref, sin_ref,
                  q_ref, k_ref, v_ref, za_ref, gb_ref, qb_ref, kb_ref, vb_ref, zb_ref, ncs_ref):
    shift = mod_ref[:, 0:D_MODEL]
    scale = mod_ref[:, D_MODEL:2 * D_MODEL]
    h = (x_ref[...] * (1.0 + scale) + shift).astype(BF16)

    for gi, o_ref in enumerate((q_ref, k_ref, v_ref)):
        c0 = gi * A_WIDTH
        cs = slice(c0, c0 + A_WIDTH)
        u = _dot(h, w_ref[:, cs])
        acc = cst_ref[0, :, cs] * cw_ref[0:1, cs]
        acc = acc + cst_ref[1, :, cs] * cw_ref[1:2, cs]
        acc = acc + cst_ref[2, :, cs] * cw_ref[2:3, cs]
        acc = acc + u * cw_ref[3:4, cs]
        y = _silu(acc)
        if gi == 0:
            y = _l2norm_heads(y, A_DK ** -0.5)
        elif gi == 1:
            y = _l2norm_heads(y, 1.0)
        o_ref[...] = y
        ncs_ref[0, :, cs] = cst_ref[1, :, cs]
        ncs_ref[1, :, cs] = cst_ref[2, :, cs]
        ncs_ref[2, :, cs] = u

    za_ref[...] = _dot(h, w_ref[:, C_ZA:C_ZA + A_WIDTH])
    gb_ref[...] = _gate_lanes(_dot(h, w_ref[:, C_BD:C_BD + LANES]), alog_ref[...], dt_ref[...])

    cos = cos_ref[...]
    sin = sin_ref[...]
    uq = _dot(h, w_ref[:, C_QB:C_QB + B_WIDTH])
    for g in range(B_WIDTH // LANES):
        qb_ref[:, g * LANES:(g + 1) * LANES] = (
            _rotary_group(uq[:, g * LANES:(g + 1) * LANES], cos, sin) * (B_HD ** -0.5))
    kb_ref[...] = _rotary_group(_dot(h, w_ref[:, C_KB:C_KB + LANES]), cos, sin)
    vb_ref[...] = _dot(h, w_ref[:, C_VB:C_VB + LANES])
    zb_ref[...] = _dot(h, w_ref[:, C_ZB:C_ZB + B_WIDTH])


def _sproj(x, mod_s, w_r, conv_w, cst, alog_row, dt_row, cos_row, sin_row):
    n = x.shape[0]
    full = lambda s: pl.BlockSpec(s, lambda i: (0,) * len(s))
    wide = lambda w: jax.ShapeDtypeStruct((n, w), F32)
    return pl.pallas_call(
        _sproj_kernel,
        grid=(1,),
        in_specs=[full((n, D_MODEL)), pl.BlockSpec((n, 3 * D_MODEL), lambda i: (0, 0)),
                  full((D_MODEL, W_COLS)),
                  full((CONV_W, A_QKV)), full((CONV_W - 1, n, A_QKV)),
                  full((1, LANES)), full((1, LANES)), full((1, LANES)), full((1, LANES))],
        out_specs=[full((n, A_WIDTH)), full((n, A_WIDTH)), full((n, A_WIDTH)), full((n, A_WIDTH)),
                   full((n, LANES)), full((n, B_WIDTH)), full((n, LANES)), full((n, LANES)),
                   full((n, B_WIDTH)), full((CONV_W - 1, n, A_QKV))],
        out_shape=[wide(A_WIDTH), wide(A_WIDTH), wide(A_WIDTH), wide(A_WIDTH), wide(LANES),
                   wide(B_WIDTH), wide(LANES), wide(LANES), wide(B_WIDTH),
                   jax.ShapeDtypeStruct((CONV_W - 1, n, A_QKV), F32)],
        compiler_params=pltpu.CompilerParams(dimension_semantics=("arbitrary",),
                                             vmem_limit_bytes=VMEM_LIMIT),
        name="sproj",
    )(x, mod_s, w_r, conv_w, cst, alog_row, dt_row, cos_row, sin_row)


def _sstep_kernel(sink_ref, q_ref, k_ref, v_ref, gb_ref, za_ref, na_ref, st_ref,
                  qb_ref, kn_ref, vn_ref, zb_ref, ck_ref, cv_ref,
                  oa_ref, ob_ref, nst_ref, nck_ref, ncv_ref,
                  o_scr, ob_scr):
    bt = q_ref.shape[0]
    gbv = gb_ref[...]

    for h in range(A_HEADS):
        hs = slice(h * A_DK, (h + 1) * A_DK)
        q_t = q_ref[:, hs].T
        k_t = k_ref[:, hs].T
        for bb in range(bt):
            eg = jnp.exp(gbv[bb:bb + 1, A_HEADS + h:A_HEADS + h + 1])
            beta = gbv[bb:bb + 1, h:h + 1]
            kcol = k_t[:, bb:bb + 1]
            qcol = q_t[:, bb:bb + 1]
            s1 = eg * st_ref[bb, h]
            pred = jnp.sum(kcol * s1, axis=0, keepdims=True)
            upd = beta * (v_ref[bb:bb + 1, hs] - pred)
            s2 = s1 + kcol * upd
            nst_ref[bb, h] = s2
            o_scr[bb:bb + 1, hs] = jnp.sum(qcol * s2, axis=0, keepdims=True)
    na = na_ref[...]
    for h in range(A_HEADS):
        hs = slice(h * A_DK, (h + 1) * A_DK)
        o = o_scr[:, hs]
        on = o * lax.rsqrt(jnp.mean(o * o, axis=-1, keepdims=True) + RMS_EPS) * na
        oa_ref[:, hs] = (on * _silu(za_ref[:, hs])).astype(BF16)

    row8 = lax.broadcasted_iota(jnp.int32, (B_HEADS, LANES), 0)
    lane8 = _lane((B_HEADS, LANES))
    own_half = (lane8 >= B_HD) == (row8 >= B_GROUP)
    rcol = lax.broadcasted_iota(jnp.int32, (B_HEADS, 1), 0)
    sink = jnp.zeros((B_HEADS, 1), F32)
    for r in range(B_HEADS):
        sink = jnp.where(rcol == r, sink_ref[r], sink)
    qv = qb_ref[...]
    qv_r = jnp.concatenate([pltpu.roll(qv[:, g * LANES:(g + 1) * LANES], B_HD, axis=1)
                            for g in range(B_WIDTH // LANES)], axis=-1)
    kn_t = kn_ref[...].T
    vn_t = vn_ref[...].T
    newest = _lane((LANES, WINDOW)) == WINDOW - 1
    qzs, scs = [], []
    for bb in range(bt):
        qz = jnp.zeros((B_HEADS, LANES), F32)
        for r in range(B_HEADS):
            grp, half, kh = r // 2, r % 2, r // B_GROUP
            src = qv if half == kh else qv_r
            qz = jnp.where(row8 == r, src[bb:bb + 1, grp * LANES:(grp + 1) * LANES], qz)
        qzs.append(jnp.where(own_half, qz, 0.0))
    for bb in range(bt):
        scs.append(_dot(qzs[bb], ck_ref[bb]))
    ps, pnews, dens = [], [], []
    for bb in range(bt):
        sc_new = jnp.sum(qzs[bb] * kn_ref[bb:bb + 1, :], axis=-1, keepdims=True)
        m = jnp.maximum(jnp.maximum(jnp.max(scs[bb], axis=-1, keepdims=True), sc_new), sink)
        p = jnp.exp(scs[bb] - m)
        p_new = jnp.exp(sc_new - m)
        ps.append(p)
        pnews.append(p_new)
        dens.append(jnp.sum(p, axis=-1, keepdims=True) + p_new + jnp.exp(sink - m))
    pvs = [_dot_nt(ps[bb], cv_ref[bb]) for bb in range(bt)]
    for bb in range(bt):
        o = (pvs[bb] + pnews[bb] * vn_ref[bb:bb + 1, :]) / dens[bb]
        o = jnp.where(own_half, o, 0.0)
        ob_scr[bb * B_HEADS:(bb + 1) * B_HEADS, :] = o + pltpu.roll(o, B_HD, axis=1)
    for bb in range(bt):
        nck_ref[bb] = jnp.where(newest, kn_t[:, bb:bb + 1], pltpu.roll(ck_ref[bb], WINDOW - 1, axis=1))
        ncv_ref[bb] = jnp.where(newest, vn_t[:, bb:bb + 1], pltpu.roll(cv_ref[bb], WINDOW - 1, axis=1))
    low = _lane((bt, LANES)) < B_HD
    for grp in range(B_WIDTH // LANES):
        even = ob_scr[pl.ds(2 * grp, bt, stride=B_HEADS), :]
        odd = ob_scr[pl.ds(2 * grp + 1, bt, stride=B_HEADS), :]
        gs = slice(grp * LANES, (grp + 1) * LANES)
        ob_ref[:, gs] = (jnp.where(low, even, odd) * _silu(zb_ref[:, gs])).astype(BF16)


def _sstep(sinks, q, k, v, gb, za, na_row, state, qb, kn, vn, zb, ck, cv):
    n = q.shape[0]
    bt = STEP_BT
    row = lambda w: pl.BlockSpec((bt, w), lambda i: (i, 0))
    st_spec = pl.BlockSpec((bt, A_HEADS, A_DK, A_DV), lambda i: (i, 0, 0, 0))
    c_spec = pl.BlockSpec((bt, WINDOW, LANES), lambda i: (i, 0, 0))
    return pl.pallas_call(
        _sstep_kernel,
        grid=(n // bt,),
        in_specs=[pl.BlockSpec(memory_space=pltpu.SMEM),
                  row(A_WIDTH), row(A_WIDTH), row(A_WIDTH), row(LANES), row(A_WIDTH),
                  pl.BlockSpec((1, A_DV), lambda i: (0, 0)), st_spec,
                  row(B_WIDTH), row(LANES), row(LANES), row(B_WIDTH), c_spec, c_spec],
        out_specs=[row(A_WIDTH), row(B_WIDTH), st_spec, c_spec, c_spec],
        out_shape=[jax.ShapeDtypeStruct((n, A_WIDTH), BF16),
                   jax.ShapeDtypeStruct((n, B_WIDTH), BF16),
                   jax.ShapeDtypeStruct((n, A_HEADS, A_DK, A_DV), F32),
                   jax.ShapeDtypeStruct((n, WINDOW, LANES), F32),
                   jax.ShapeDtypeStruct((n, WINDOW, LANES), F32)],
        scratch_shapes=[pltpu.VMEM((bt, A_WIDTH), F32), pltpu.VMEM((bt * B_HEADS, LANES), F32)],
        compiler_params=pltpu.CompilerParams(dimension_semantics=("arbitrary",),
                                             vmem_limit_bytes=VMEM_LIMIT),
        name="sstep",
    )(sinks, q, k, v, gb, za, na_row, state, qb, kn, vn, zb, ck, cv)


def _rope_tables(pos):
    half = B_HD // 2
    inv = 1.0 / (ROPE_THETA ** (np.arange(half, dtype=np.float64) / half))
    ang = np.asarray(pos, np.float64)[:, None] * inv[None, :]
    cos, sin = np.cos(ang), np.sin(ang)
    reps = LANES // B_HD
    return (jnp.asarray(np.tile(np.concatenate([cos, cos], -1), (1, reps)), F32),
            jnp.asarray(np.tile(np.concatenate([-sin, sin], -1), (1, reps)), F32))


def _pad_row(vec, offset):
    return jnp.zeros((1, LANES), F32).at[0, offset:offset + vec.shape[0]].set(vec.astype(F32))


def _layer(x_prompt, x_sample, state_conv, state_delta, cache_k, cache_v, c_prompt, c_sample,
           w_ada, b_ada, w_in, conv_w, a_log, dt_bias, norm_a, sinks, w_out, ln_g, ln_b):
    bsz, seq, _ = x_prompt.shape
    n_s = x_sample.shape[0]

    w_r = _wprep(jnp.swapaxes(w_in, 0, 1))
    w_o = w_out.astype(BF16)
    alog_row = _pad_row(a_log, A_HEADS)
    dt_row = _pad_row(dt_bias, A_HEADS)
    na_row = norm_a.reshape(1, A_DV)
    g_row = ln_g.reshape(1, D_MODEL)
    b_row = ln_b.reshape(1, D_MODEL)

    assert n_s % 8 == 0 and bsz <= 8
    c_all = jnp.concatenate([c_sample, c_prompt, jnp.zeros((8 - bsz, D_MODEL), F32)], axis=0)
    mod = _ada(c_all, w_ada, b_ada.reshape(1, 3 * D_MODEL))

    cos_p, sin_p = _rope_tables(np.arange(seq))
    (q, k, v, za, gb, qb, kb, kbr, vd0, vd1, zb, conv_p, kb_last, vb_last) = _proj(
        x_prompt, mod, n_s, w_r, conv_w, alog_row, dt_row, cos_p, sin_p)
    oa, delta_p = _delta(q, k, v, gb, za, na_row)
    ob = _swa(sinks, qb, kb, kbr, vd0, vd1, zb)
    y_p = _out(oa, ob, x_prompt, mod, n_s, True, w_o, g_row, b_row, OUT_TM)
    swa_k_p = kb_last.reshape(bsz, WINDOW, B_KV_HEADS, B_HD)
    swa_v_p = vb_last.reshape(bsz, WINDOW, B_KV_HEADS, B_HD)

    cos_s, sin_s = _rope_tables(np.array([PAST_LEN]))
    xs = x_sample.reshape(n_s, D_MODEL)
    cst = jnp.transpose(state_conv, (1, 0, 2))
    sq, sk, sv, sza, sgb, sqb, skn, svn, szb, ncs = _sproj(xs, mod, w_r, conv_w, cst, alog_row, dt_row,
                                                           cos_s, sin_s)
    soa, sob, delta_s, nck, ncv = _sstep(sinks, sq, sk, sv, sgb, sza, na_row, state_delta,
                                         sqb, skn, svn, szb,
                                         jnp.swapaxes(cache_k.reshape(n_s, WINDOW, LANES), 1, 2),
                                         jnp.swapaxes(cache_v.reshape(n_s, WINDOW, LANES), 1, 2))
    y_s = _out(soa[None], sob[None], xs[None], mod, 0, False, w_o, g_row, b_row, n_s)
    conv_s = jnp.transpose(ncs, (1, 0, 2))
    unpack = lambda c: jnp.swapaxes(c, 1, 2).reshape(n_s, WINDOW, B_KV_HEADS, B_HD)
    return (y_p, y_s.reshape(n_s, 1, D_MODEL), conv_p, delta_p, swa_k_p, swa_v_p,
            conv_s, delta_s, unpack(nck), unpack(ncv))


def kernel(x_prompt, x_sample, state_conv, state_delta, cache_swa_k, cache_swa_v, c_prompt, c_sample,
           w_ada, b_ada, w_in, conv_w, a_log, dt_bias, norm_a, sinks, w_out, ln_g, ln_b):
    assert w_ada.shape[0] == DEPTH == 1
    outs = _layer(x_prompt, x_sample, state_conv[0], state_delta[0], cache_swa_k[0], cache_swa_v[0],
                  c_prompt, c_sample, w_ada[0], b_ada[0], w_in[0], conv_w[0], a_log[0], dt_bias[0],
                  norm_a[0], sinks[0], w_out[0], ln_g[0], ln_b[0])
    y_p, y_s = outs[0], outs[1]
    return (y_p, y_s) + tuple(o[None] for o in outs[2:])
```

```python
import functools

import jax
import jax.numpy as jnp
import numpy as np
from jax import lax
from jax.experimental import pallas as pl
from jax.experimental.pallas import tpu as pltpu

F32 = jnp.float32
BF16 = jnp.bfloat16

D_MODEL = 1024
DEPTH = 1
PAST_LEN = 8192
A_HEADS = 4
A_DK = 128
A_DV = 128
A_WIDTH = A_HEADS * A_DV
A_QKV = 3 * A_WIDTH
CONV_W = 4
CHUNK = 64
B_HEADS = 8
B_KV_HEADS = 2
B_HD = 64
B_GROUP = B_HEADS // B_KV_HEADS
B_WIDTH = B_HEADS * B_HD
B_KV_WIDTH = B_KV_HEADS * B_HD
WINDOW = 128
ROPE_THETA = 10000.0
MIX_WIDTH = A_WIDTH + B_WIDTH
DEEPNORM_ALPHA = (2 * DEPTH) ** 0.25
LOG2E = 1.4426950408889634
LN_EPS = 1e-5
RMS_EPS = 1e-6
L2_EPS = 1e-6

OFF_A_Z = A_QKV
OFF_A_BETA = OFF_A_Z + A_WIDTH
OFF_A_DECAY = OFF_A_BETA + A_HEADS
OFF_B_Q = OFF_A_DECAY + A_HEADS
OFF_B_K = OFF_B_Q + B_WIDTH
OFF_B_V = OFF_B_K + B_KV_WIDTH
OFF_B_Z = OFF_B_V + B_KV_WIDTH
PROJ_COLS = OFF_B_Z + B_WIDTH

LANES = 128
C_QKV = 0
C_ZA = C_QKV + A_QKV
C_QB = C_ZA + A_WIDTH
C_KB = C_QB + B_WIDTH
C_VB = C_KB + B_KV_WIDTH
C_ZB = C_VB + B_KV_WIDTH
C_BD = C_ZB + B_WIDTH
WPREP_TN = 256
W_COLS = C_BD + WPREP_TN

VMEM_LIMIT = 56 * 1024 * 1024

PROJ_TM = 512
DELTA_CT = 256
DELTA_WAVE = 2
SWA_TQ = 512
SWA_WAVE = 1
OUT_TM = 1024
STEP_BT = 16


def _dot(a, b):
    return jnp.dot(a, b, preferred_element_type=F32)


def _dot_nt(a, b):
    return lax.dot_general(a, b, (((1,), (1,)), ((), ())), preferred_element_type=F32)


def _silu(x):
    return x * jax.nn.sigmoid(x)


def _softplus(x):
    return jnp.maximum(x, 0.0) + jnp.log1p(jnp.exp(-jnp.abs(x)))


def _lane(shape):
    return lax.broadcasted_iota(jnp.int32, shape, len(shape) - 1)


def _l2norm_heads(y, scale):
    outs = []
    for h in range(A_HEADS):
        xh = y[:, h * A_DK:(h + 1) * A_DK]
        ss = jnp.sum(xh * xh, axis=-1, keepdims=True)
        xn = xh * lax.rsqrt(ss + L2_EPS)
        outs.append(xn * scale if scale != 1.0 else xn)
    return jnp.concatenate(outs, axis=-1)


def _rotary_group(xg, cos, sin_signed):
    lane = _lane(xg.shape)
    swapped = jnp.where((lane % B_HD) < (B_HD // 2),
                        pltpu.roll(xg, LANES - B_HD // 2, axis=1),
                        pltpu.roll(xg, B_HD // 2, axis=1))
    return xg * cos + swapped * sin_signed


def _kv_layouts(kb, vb):
    low = _lane(kb.shape) < B_HD
    kbr = pltpu.roll(kb, B_HD, axis=1)
    vbr = pltpu.roll(vb, B_HD, axis=1)
    return kb, kbr, jnp.where(low, vb, vbr), jnp.where(low, vbr, vb)


def _gate_lanes(bd, alog_row, dt_row):
    lane = _lane(bd.shape)
    g = -jnp.exp(alog_row) * _softplus(bd + dt_row)
    return jnp.where(lane < A_HEADS, jax.nn.sigmoid(bd), g)


def _layer_norm(r, g, b):
    mu = jnp.mean(r, axis=-1, keepdims=True)
    d = r - mu
    var = jnp.mean(d * d, axis=-1, keepdims=True)
    return d * lax.rsqrt(var + LN_EPS) * g + b


def _wprep_kernel(wt_ref, o_ref):
    x = wt_ref[...]
    tail = pl.program_id(0) == pl.num_programs(0) - 1
    row = lax.broadcasted_iota(jnp.int32, x.shape, 0)
    x = jnp.where(jnp.logical_and(tail, row >= 2 * A_HEADS), 0.0, x)
    o_ref[...] = x.T.astype(BF16)


def _wprep(w_t):
    tn = WPREP_TN
    n_a, n_b = OFF_A_BETA // tn, (PROJ_COLS - OFF_B_Q) // tn
    assert n_a * tn == OFF_A_BETA and n_b * tn == PROJ_COLS - OFF_B_Q and OFF_A_BETA + tn <= PROJ_COLS

    def src_row(j):
        return jnp.where(j < n_a, j * tn, jnp.where(j < n_a + n_b, OFF_B_Q + (j - n_a) * tn, OFF_A_BETA))

    return pl.pallas_call(
        _wprep_kernel,
        grid=(n_a + n_b + 1,),
        in_specs=[pl.BlockSpec((pl.Element(tn), pl.Element(D_MODEL)),
                               lambda j: (pl.multiple_of(src_row(j), 8), 0))],
        out_specs=pl.BlockSpec((D_MODEL, tn), lambda j: (0, j)),
        out_shape=jax.ShapeDtypeStruct((D_MODEL, W_COLS), BF16),
        compiler_params=pltpu.CompilerParams(dimension_semantics=("arbitrary",),
                                             vmem_limit_bytes=VMEM_LIMIT),
        name="wprep",
    )(w_t)


def _ada_kernel(c_ref, w_ref, b_ref, o_ref):
    o_ref[...] = _dot(c_ref[...].astype(BF16), w_ref[...].astype(BF16)) + b_ref[...]


def _ada(c_all, w_ada, b_ada):
    rows = c_all.shape[0]
    tn = 768
    return pl.pallas_call(
        _ada_kernel,
        grid=(3 * D_MODEL // tn,),
        in_specs=[pl.BlockSpec((rows, D_MODEL), lambda j: (0, 0)),
                  pl.BlockSpec((D_MODEL, tn), lambda j: (0, j)),
                  pl.BlockSpec((1, tn), lambda j: (0, j))],
        out_specs=pl.BlockSpec((rows, tn), lambda j: (0, j)),
        out_shape=jax.ShapeDtypeStruct((rows, 3 * D_MODEL), F32),
        compiler_params=pltpu.CompilerParams(dimension_semantics=("arbitrary",),
                                             vmem_limit_bytes=VMEM_LIMIT),
        name="ada",
    )(c_all, w_ada, b_ada)


def _proj_kernel(x_ref, mod_ref, w_ref, cw_ref, alog_ref, dt_ref, cos_ref, sin_ref,
                 q_ref, k_ref, v_ref, za_ref, gb_ref, qb_ref, kb_ref, kbr_ref, vd0_ref, vd1_ref,
                 zb_ref, cst_ref, kbl_ref, vbl_ref, ubuf):
    tm = x_ref.shape[1]
    t = pl.program_id(1)

    @pl.when(t == 0)
    def _():
        ubuf[...] = jnp.zeros(ubuf.shape, F32)

    brow = pl.ds(pl.program_id(0), 1)
    shift = mod_ref[brow, 0:D_MODEL]
    scale = mod_ref[brow, D_MODEL:2 * D_MODEL]
    h = (x_ref[0] * (1.0 + scale) + shift).astype(BF16)

    sub = lax.broadcasted_iota(jnp.int32, (tm // 8, 8, A_WIDTH), 1)
    for gi, o_ref in enumerate((q_ref, k_ref, v_ref)):
        cs = slice(gi * A_WIDTH, (gi + 1) * A_WIDTH)
        u = _dot(h, w_ref[:, cs])
        groups = jnp.concatenate([ubuf[:, cs], u], axis=0).reshape(tm // 8 + 1, 8, A_WIDTH)
        acc = None
        for j in range(CONV_W - 1, 0, -1):
            rot = pltpu.roll(groups, j, axis=1)
            term = (jnp.where(sub < j, rot[:-1], rot[1:]).reshape(tm, A_WIDTH)
                    * cw_ref[CONV_W - 1 - j:CONV_W - j, cs])
            acc = term if acc is None else acc + term
        y = _silu(acc + u * cw_ref[CONV_W - 1:CONV_W, cs])
        if gi == 0:
            y = _l2norm_heads(y, A_DK ** -0.5)
        elif gi == 1:
            y = _l2norm_heads(y, 1.0)
        o_ref[0] = y
        ubuf[:, cs] = u[tm - 8:tm]
        cst_ref[0, :, cs] = u[tm - (CONV_W - 1):tm]

    za_ref[0] = _dot(h, w_ref[:, C_ZA:C_ZA + A_WIDTH])
    gb_ref[0] = _gate_lanes(_dot(h, w_ref[:, C_BD:C_BD + LANES]), alog_ref[...], dt_ref[...])
    cos = cos_ref[...]
    sin = sin_ref[...]
    uq = _dot(h, w_ref[:, C_QB:C_QB + B_WIDTH])
    for g in range(B_WIDTH // LANES):
        qb_ref[0, :, g * LANES:(g + 1) * LANES] = (
            _rotary_group(uq[:, g * LANES:(g + 1) * LANES], cos, sin) * (B_HD ** -0.5 * LOG2E)).astype(BF16)
    ukv = _dot(h, w_ref[:, C_KB:C_KB + 2 * LANES])
    kb = _rotary_group(ukv[:, 0:LANES], cos, sin)
    vb = ukv[:, LANES:2 * LANES]
    for o_ref, val in zip((kb_ref, kbr_ref, vd0_ref, vd1_ref), _kv_layouts(kb, vb)):
        o_ref[0] = val.astype(BF16)
    zb_ref[0] = _dot(h, w_ref[:, C_ZB:C_ZB + B_WIDTH])

    @pl.when(t == pl.num_programs(1) - 1)
    def _():
        kbl_ref[0] = kb[tm - WINDOW:tm]
        vbl_ref[0] = vb[tm - WINDOW:tm]


def _proj(x, mod, mod_row0, w_r, conv_w, alog_row, dt_row, cos_t, sin_t):
    bsz, t, _ = x.shape
    tm = PROJ_TM
    row = lambda w: pl.BlockSpec((1, tm, w), lambda b, i: (b, i, 0))
    const2 = lambda s: pl.BlockSpec(s, lambda b, i: (0, 0))
    per_b = lambda r, w: pl.BlockSpec((1, r, w), lambda b, i: (b, 0, 0))
    wide = lambda w, dt=F32: jax.ShapeDtypeStruct((bsz, t, w), dt)
    return pl.pallas_call(
        _proj_kernel,
        grid=(bsz, t // tm),
        in_specs=[row(D_MODEL),
                  pl.BlockSpec((8, 3 * D_MODEL), lambda b, i: (mod_row0 // 8, 0)),
                  const2((D_MODEL, W_COLS)),
                  const2((CONV_W, A_QKV)),
                  const2((1, LANES)), const2((1, LANES)),
                  pl.BlockSpec((tm, LANES), lambda b, i: (i, 0)),
                  pl.BlockSpec((tm, LANES), lambda b, i: (i, 0))],
        out_specs=[row(A_WIDTH), row(A_WIDTH), row(A_WIDTH), row(A_WIDTH), row(LANES),
                   row(B_WIDTH), row(LANES), row(LANES), row(LANES), row(LANES), row(B_WIDTH),
                   per_b(CONV_W - 1, A_QKV), per_b(WINDOW, LANES), per_b(WINDOW, LANES)],
        out_shape=[wide(A_WIDTH), wide(A_WIDTH), wide(A_WIDTH), wide(A_WIDTH), wide(LANES),
                   wide(B_WIDTH, BF16), wide(LANES, BF16), wide(LANES, BF16), wide(LANES, BF16),
                   wide(LANES, BF16), wide(B_WIDTH),
                   jax.ShapeDtypeStruct((bsz, CONV_W - 1, A_QKV), F32),
                   jax.ShapeDtypeStruct((bsz, WINDOW, LANES), F32),
                   jax.ShapeDtypeStruct((bsz, WINDOW, LANES), F32)],
        scratch_shapes=[pltpu.VMEM((8, A_QKV), F32)],
        compiler_params=pltpu.CompilerParams(dimension_semantics=("arbitrary", "arbitrary"),
                                             vmem_limit_bytes=VMEM_LIMIT),
        name="proj",
    )(x, mod, w_r, conv_w, alog_row, dt_row, cos_t, sin_t)


def _delta_kernel(q_ref, k_ref, v_ref, gb_ref, za_ref, na_ref, oa_ref, st_ref,
                  s_scr, wq_s, ut_s, akd_s, gl_s):
    bsz, ct = q_ref.shape[0], q_ref.shape[1]
    nch = ct // CHUNK
    t = pl.program_id(0)
    wslot = t % 2
    rslot = 1 - wslot

    @pl.when(t == 0)
    def _():
        s_scr[...] = jnp.zeros(s_scr.shape, F32)
        wq_s[...] = jnp.zeros(wq_s.shape, BF16)
        ut_s[...] = jnp.zeros(ut_s.shape, F32)
        akd_s[...] = jnp.zeros(akd_s.shape, BF16)
        gl_s[...] = jnp.zeros(gl_s.shape, F32)

    units = [(b, c, h) for b in range(bsz) for c in range(nch) for h in range(A_HEADS)]
    uid = {u_: i for i, u_ in enumerate(units)}
    rows = lambda c: slice(c * CHUNK, (c + 1) * CHUNK)
    lanes = lambda h: slice(h * A_DK, (h + 1) * A_DK)
    na = na_ref[...]

    s_cur = {(b, h): s_scr[b * A_HEADS + h] for b in range(bsz) for h in range(A_HEADS)}
    ws, uu = {}, {}

    def rec_ws(c):
        for b in range(bsz):
            for h in range(A_HEADS):
                i = uid[b, c, h]
                ws[b, h] = _dot(wq_s[rslot, i], s_cur[b, h].astype(BF16))
                uu[b, h] = (ut_s[rslot, i] - ws[b, h][:CHUNK]).astype(BF16)

    def rec_ou(c):
        zpad = jnp.zeros((CHUNK, A_DV), BF16)
        for b in range(bsz):
            u_bd = jnp.concatenate(
                [jnp.concatenate([uu[b, h] if hh == h else zpad for hh in range(A_HEADS)], axis=-1)
                 for h in range(A_HEADS)], axis=0)
            ou = _dot(akd_s[rslot, b * nch + c], u_bd)
            for h in range(A_HEADS):
                o = ws[b, h][CHUNK:] + ou[:CHUNK, lanes(h)]
                s_cur[b, h] = gl_s[rslot, uid[b, c, h]] * s_cur[b, h] + ou[CHUNK:, lanes(h)]
                on = o * lax.rsqrt(jnp.mean(o * o, axis=-1, keepdims=True) + RMS_EPS) * na
                oa_ref[b, rows(c), lanes(h)] = (on * _silu(za_ref[b, rows(c), lanes(h)])).astype(BF16)

    rec_stages = []
    for c in range(nch):
        rec_stages += [lambda c=c: rec_ws(c), lambda c=c: rec_ou(c)]

    def run_rec(n_left_after):
        while rec_stages and len(rec_stages) > n_left_after:
            rec_stages.pop(0)()

    pk = A_HEADS * CHUNK
    low = _lane((CHUNK, LANES)) < CHUNK
    low_row = _lane((1, LANES)) < CHUNK
    ti_p = lax.broadcasted_iota(jnp.int32, (CHUNK, pk), 0)
    ii_p = _lane((CHUNK, pk)) % CHUNK
    zero64 = jnp.zeros((CHUNK, LANES), BF16)

    def pack(parts):
        return jnp.concatenate([jnp.where(low, parts[0], parts[1]), jnp.where(low, parts[2], parts[3])], axis=-1)

    def block_diag(x16):
        blocks = []
        for h in range(A_HEADS):
            pair, first = h // 2, h % 2 == 0
            piece = jnp.where(low if first else jnp.logical_not(low), x16[:, pair * LANES:(pair + 1) * LANES], zero64)
            blocks.append(jnp.concatenate([piece, zero64] if pair == 0 else [zero64, piece], axis=-1))
        return jnp.concatenate(blocks, axis=0)

    zrhs = jnp.zeros((CHUNK, 2 * A_DK), BF16)
    n_rec = len(rec_stages)
    n_slots = 8 * (bsz // DELTA_WAVE)
    done = [0]

    def stage_done():
        done[0] += 1
        run_rec(n_rec - (done[0] * n_rec) // n_slots)

    def decay_terms(b, beta, g_col, g_last, eg, dec_p, beta_p):
        gbv = gb_ref[b]
        rin = lax.broadcasted_iota(jnp.int32, gbv.shape, 0) % CHUNK
        gcs = gbv
        s = 1
        while s < CHUNK:
            gcs = gcs + jnp.where(rin >= s, pltpu.roll(gcs, s, axis=0), 0.0)
            s *= 2
        gcs_t = gcs.T
        for c in range(nch):
            r0 = c * CHUNK
            pair_lanes = slice((c // 2) * LANES, (c // 2 + 1) * LANES)
            g_rows = []
            for h in range(A_HEADS):
                u_ = (b, c, h)
                beta[u_] = jnp.broadcast_to(gbv[rows(c), h:h + 1], (CHUNK, A_DK))
                g_col[u_] = jnp.broadcast_to(gcs[rows(c), A_HEADS + h:A_HEADS + h + 1], (CHUNK, A_DK))
                g_last[u_] = gcs[r0 + CHUNK - 1:r0 + CHUNK, A_HEADS + h:A_HEADS + h + 1]
                eg[u_] = jnp.exp(g_col[u_])
                g_row = gcs_t[A_HEADS + h:A_HEADS + h + 1, pair_lanes]
                g_rows.append(g_row if c % 2 == h % 2 else pltpu.roll(g_row, CHUNK, axis=1))
            g_row_p = jnp.concatenate([jnp.where(low_row, g_rows[0], g_rows[1]),
                                       jnp.where(low_row, g_rows[2], g_rows[3])], axis=-1)
            g_col_p = pack([g_col[b, c, h] for h in range(A_HEADS)])
            dec_p[b, c] = jnp.exp(jnp.where(ti_p >= ii_p, g_col_p - g_row_p, -jnp.inf))
            beta_p[b, c] = pack([beta[b, c, h] for h in range(A_HEADS)])

    def prepare(bs):
        groups_b = [(b, c) for b in bs for c in range(nch)]
        beta, g_col, g_last, eg, dec_p, beta_p = {}, {}, {}, {}, {}, {}
        for b in bs:
            decay_terms(b, beta, g_col, g_last, eg, dec_p, beta_p)

        nmat = {}
        for (b, c) in groups_b:
            k16 = k_ref[b, rows(c), :].astype(BF16)
            q16 = q_ref[b, rows(c), :].astype(BF16)
            k_heads = jnp.concatenate(
                [jnp.concatenate([k16[:, lanes(h)] if hh == h else zero64 for hh in range(A_HEADS)], axis=-1)
                 for h in range(A_HEADS)], axis=0)
            kq = _dot_nt(jnp.concatenate([k16, q16], axis=0), k_heads)
            nmat[b, c] = -(beta_p[b, c] * kq[:CHUNK] * jnp.where(ti_p > ii_p, dec_p[b, c], 0.0))
            akd_s[wslot, b * nch + c, 0:CHUNK, :] = (kq[CHUNK:] * dec_p[b, c]).astype(BF16)
        stage_done()

        rsum = dict(nmat)
        pw16 = {g_: nmat[g_].astype(BF16) for g_ in groups_b}
        pw = {g_: _dot(pw16[g_], block_diag(pw16[g_])) for g_ in groups_b}
        stage_done()
        for step in range(1, 6):
            last = step == 5
            pw16 = {g_: pw[g_].astype(BF16) for g_ in groups_b}
            rp = {}
            for g_ in groups_b:
                r16 = rsum[g_].astype(BF16)
                rp[g_] = _dot(r16 if last else jnp.concatenate([r16, pw16[g_]], axis=0), block_diag(pw16[g_]))
            for g_ in groups_b:
                rsum[g_] = rsum[g_] + pw[g_] + rp[g_][:CHUNK]
                if not last:
                    pw[g_] = rp[g_][CHUNK:]
            stage_done()

        for (b, c) in groups_b:
            for h in range(A_HEADS):
                u_ = (b, c, h)
                i = uid[u_]
                kc = k_ref[b, rows(c), lanes(h)]
                rhs = jnp.concatenate([(beta[u_] * eg[u_]) * kc, beta[u_] * v_ref[b, rows(c), lanes(h)]],
                                      axis=-1)
                rhs16 = rhs.astype(BF16)
                rhs_rows = jnp.concatenate([rhs16 if hh == h else zrhs for hh in range(A_HEADS)], axis=0)
                sol = rhs + _dot(rsum[b, c].astype(BF16), rhs_rows)
                wq_s[wslot, i] = jnp.concatenate([sol[:, :A_DK], eg[u_] * q_ref[b, rows(c), lanes(h)]],
                                                 axis=0).astype(BF16)
                ut_s[wslot, i] = sol[:, A_DK:]
                gl_s[wslot, i] = jnp.broadcast_to(jnp.exp(g_last[u_]), (1, A_DV))
        for (b, c) in groups_b:
            kd = [jnp.exp(g_last[b, c, h] - g_col[b, c, h]) * k_ref[b, rows(c), lanes(h)]
                  for h in range(A_HEADS)]
            for p in range(A_HEADS // 2):
                akd_s[wslot, b * nch + c, CHUNK:, p * LANES:(p + 1) * LANES] = (
                    jnp.concatenate([kd[2 * p], kd[2 * p + 1]], axis=0).T.astype(BF16))
        stage_done()

    for b0 in range(0, bsz, DELTA_WAVE):
        prepare(range(b0, b0 + DELTA_WAVE))
    run_rec(0)

    for b in range(bsz):
        for h in range(A_HEADS):
            s_scr[b * A_HEADS + h] = s_cur[b, h]

    @pl.when(t == pl.num_programs(0) - 1)
    def _():
        for b in range(bsz):
            for h in range(A_HEADS):
                st_ref[b, h] = s_cur[b, h]


def _delta(q, k, v, gb, za, na_row):
    bsz, t, _ = q.shape
    ct = DELTA_CT
    nt = t // ct
    n_units = bsz * (ct // CHUNK) * A_HEADS
    prep = lambda w: pl.BlockSpec((bsz, ct, w), lambda i: (0, jnp.minimum(i, nt - 1), 0))
    rec = lambda w: pl.BlockSpec((bsz, ct, w), lambda i: (0, jnp.maximum(i - 1, 0), 0))
    return pl.pallas_call(
        _delta_kernel,
        grid=(nt + 1,),
        in_specs=[prep(A_WIDTH), prep(A_WIDTH), prep(A_WIDTH), prep(LANES), rec(A_WIDTH),
                  pl.BlockSpec((1, A_DV), lambda i: (0, 0))],
        out_specs=[rec(A_WIDTH),
                   pl.BlockSpec((bsz, A_HEADS, A_DK, A_DV), lambda i: (0, 0, 0, 0))],
        out_shape=[jax.ShapeDtypeStruct((bsz, t, A_WIDTH), BF16),
                   jax.ShapeDtypeStruct((bsz, A_HEADS, A_DK, A_DV), F32)],
        scratch_shapes=[pltpu.VMEM((bsz * A_HEADS, A_DK, A_DV), F32),
                        pltpu.VMEM((2, n_units, 2 * CHUNK, A_DK), BF16),
                        pltpu.VMEM((2, n_units, CHUNK, A_DV), F32),
                        pltpu.VMEM((2, n_units // A_HEADS, CHUNK + A_DK, A_HEADS * CHUNK), BF16),
                        pltpu.VMEM((2, n_units, 1, A_DV), F32)],
        compiler_params=pltpu.CompilerParams(dimension_semantics=("arbitrary",),
                                             vmem_limit_bytes=VMEM_LIMIT),
        name="delta",
    )(q, k, v, gb, za, na_row)


def _swa_kernel(sink_ref, qb_ref, kc_ref, kp_ref, krc_ref, krp_ref, v0c_ref, v0p_ref, v1c_ref, v1p_ref,
                zb_ref, ob_ref):
    n = pl.program_id(1)
    tq = qb_ref.shape[1]
    blk = WINDOW
    kx = (jnp.concatenate([kp_ref[0], kc_ref[0]], axis=0), jnp.concatenate([krp_ref[0], krc_ref[0]], axis=0))
    vd = (jnp.concatenate([v0p_ref[0], v0c_ref[0]], axis=0), jnp.concatenate([v1p_ref[0], v1c_ref[0]], axis=0))

    a = lax.broadcasted_iota(jnp.int32, (2 * blk, 2 * blk), 0) % blk
    j = lax.broadcasted_iota(jnp.int32, (2 * blk, 2 * blk), 1)
    rel = a + blk - j
    band = (rel >= 0) & (rel <= WINDOW)
    band_first = band & ((n > 0) | (j >= blk))
    top = lax.broadcasted_iota(jnp.int32, (2 * blk, 1), 0) < blk
    low = _lane((blk, LANES)) < B_HD
    zero = jnp.zeros((blk, LANES), BF16)

    qrows = lambda i: slice(i * blk, (i + 1) * blk)
    krows = lambda i: slice(i * blk, (i + 2) * blk)
    sink = {(kh, half): jnp.where(top, sink_ref[kh * B_GROUP + half] * LOG2E,
                                  sink_ref[kh * B_GROUP + half + 2] * LOG2E)
            for kh in range(B_KV_HEADS) for half in range(2)}
    for i0 in range(0, tq // blk, SWA_WAVE):
        blocks = range(i0, i0 + SWA_WAVE)
        units = [(i, kh, half) for i in blocks for kh in range(B_KV_HEADS) for half in range(2)]
        sc = {}
        for (i, kh, half) in units:
            qs = []
            for g in range(2):
                grp = kh * 2 + g
                xg = qb_ref[0, qrows(i), grp * LANES:(grp + 1) * LANES]
                qs.append(jnp.where(low if half == 0 else jnp.logical_not(low), xg, zero))
            qz = jnp.concatenate(qs, axis=0)
            sc[i, kh, half] = _dot_nt(qz, kx[0 if kh == half else 1][krows(i)])
        p, den = {}, {}
        for u_ in units:
            i, kh, half = u_
            s_m = jnp.where(band_first if i == 0 else band, sc[u_], -jnp.inf)
            m = jnp.maximum(jnp.max(s_m, axis=-1, keepdims=True), sink[kh, half])
            e = jnp.exp2(s_m - m)
            den[u_] = jnp.sum(e, axis=-1, keepdims=True) + jnp.exp2(sink[kh, half] - m)
            p[u_] = e.astype(BF16)
        pv = {u_: _dot(p[u_], vd[u_[1]][krows(u_[0])]) for u_ in units}
        outs = {u_: pv[u_] / den[u_] for u_ in units}
        for i in blocks:
            for grp in range(B_WIDTH // LANES):
                kh, g = grp // 2, grp % 2
                og = jnp.where(low, outs[i, kh, 0][g * blk:(g + 1) * blk], outs[i, kh, 1][g * blk:(g + 1) * blk])
                gs = slice(grp * LANES, (grp + 1) * LANES)
                ob_ref[0, qrows(i), gs] = (og * _silu(zb_ref[0, qrows(i), gs])).astype(BF16)


def _swa(sinks, qb, kb, kbr, vd0, vd1, zb):
    bsz, t, _ = qb.shape
    tq = SWA_TQ
    per = tq // WINDOW
    cur = lambda w: pl.BlockSpec((1, tq, w), lambda b, i: (b, i, 0))
    prev = pl.BlockSpec((1, WINDOW, LANES), lambda b, i: (b, jnp.maximum(i * per - 1, 0), 0))
    return pl.pallas_call(
        _swa_kernel,
        grid=(bsz, t // tq),
        in_specs=[pl.BlockSpec(memory_space=pltpu.SMEM), cur(B_WIDTH),
                  cur(LANES), prev, cur(LANES), prev, cur(LANES), prev, cur(LANES), prev,
                  cur(B_WIDTH)],
        out_specs=cur(B_WIDTH),
        out_shape=jax.ShapeDtypeStruct((bsz, t, B_WIDTH), BF16),
        compiler_params=pltpu.CompilerParams(dimension_semantics=("arbitrary", "arbitrary"),
                                             vmem_limit_bytes=VMEM_LIMIT),
        name="swa",
    )(sinks, qb, kb, kb, kbr, kbr, vd0, vd0, vd1, vd1, zb)


def _out_kernel(oa_ref, ob_ref, x_ref, gate_ref, w_ref, g_ref, b_ref, y_ref, *, gate_per_batch):
    mix = _dot(oa_ref[0], w_ref[0:A_WIDTH, :]) + _dot(ob_ref[0], w_ref[A_WIDTH:MIX_WIDTH, :])
    gate = gate_ref[pl.ds(pl.program_id(0), 1), :] if gate_per_batch else gate_ref[...]
    r = DEEPNORM_ALPHA * x_ref[0] + (1.0 + gate) * mix
    y_ref[0] = _layer_norm(r, g_ref[...], b_ref[...])


def _out(oa, ob, x, mod, mod_row0, gate_per_batch, w_out, ln_g, ln_b, tm):
    bsz, t, _ = x.shape
    grows = 8 if gate_per_batch else tm
    row = lambda w: pl.BlockSpec((1, tm, w), lambda b, i: (b, i, 0))
    const2 = lambda s: pl.BlockSpec(s, lambda b, i: (0, 0))
    return pl.pallas_call(
        functools.partial(_out_kernel, gate_per_batch=gate_per_batch),
        grid=(bsz, t // tm),
        in_specs=[row(A_WIDTH), row(B_WIDTH), row(D_MODEL),
                  pl.BlockSpec((grows, D_MODEL), lambda b, i: (mod_row0 // grows, 2)),
                  const2((MIX_WIDTH, D_MODEL)), const2((1, D_MODEL)), const2((1, D_MODEL))],
        out_specs=row(D_MODEL),
        out_shape=jax.ShapeDtypeStruct((bsz, t, D_MODEL), F32),
        compiler_params=pltpu.CompilerParams(dimension_semantics=("arbitrary", "arbitrary"),
                                             vmem_limit_bytes=VMEM_LIMIT),
        name="out",
    )(oa, ob, x, mod, w_out, ln_g, ln_b)


def _sproj_kernel(x_ref, mod_ref, w_ref, cw_ref, cst_ref, alog_ref, dt_ref, cos_ref, sin_ref,
                  q_ref, k_ref, v_ref, za_ref, gb_ref, qb_ref, kb_ref, vb_ref, zb_ref, ncs_ref):
    shift = mod_ref[:, 0:D_MODEL]
    scale = mod_ref[:, D_MODEL:2 * D_MODEL]
    h = (x_ref[...] * (1.0 + scale) + shift).astype(BF16)

    for gi, o_ref in enumerate((q_ref, k_ref, v_ref)):
        c0 = gi * A_WIDTH
        cs = slice(c0, c0 + A_WIDTH)
        u = _dot(h, w_ref[:, cs])
        acc = cst_ref[0, :, cs] * cw_ref[0:1, cs]
        acc = acc + cst_ref[1, :, cs] * cw_ref[1:2, cs]
        acc = acc + cst_ref[2, :, cs] * cw_ref[2:3, cs]
        acc = acc + u * cw_ref[3:4, cs]
        y = _silu(acc)
        if gi == 0:
            y = _l2norm_heads(y, A_DK ** -0.5)
        elif gi == 1:
            y = _l2norm_heads(y, 1.0)
        o_ref[...] = y
        ncs_ref[0, :, cs] = cst_ref[1, :, cs]
        ncs_ref[1, :, cs] = cst_ref[2, :, cs]
        ncs_ref[2, :, cs] = u

    za_ref[...] = _dot(h, w_ref[:, C_ZA:C_ZA + A_WIDTH])
    gb_ref[...] = _gate_lanes(_dot(h, w_ref[:, C_BD:C_BD + LANES]), alog_ref[...], dt_ref[...])

    cos = cos_ref[...]
    sin = sin_ref[...]
    uq = _dot(h, w_ref[:, C_QB:C_QB + B_WIDTH])
    for g in range(B_WIDTH // LANES):
        qb_ref[:, g * LANES:(g + 1) * LANES] = (
            _rotary_group(uq[:, g * LANES:(g + 1) * LANES], cos, sin) * (B_HD ** -0.5))
    kb_ref[...] = _rotary_group(_dot(h, w_ref[:, C_KB:C_KB + LANES]), cos, sin)
    vb_ref[...] = _dot(h, w_ref[:, C_VB:C_VB + LANES])
    zb_ref[...] = _dot(h, w_ref[:, C_ZB:C_ZB + B_WIDTH])


def _sproj(x, mod_s, w_r, conv_w, cst, alog_row, dt_row, cos_row, sin_row):
    n = x.shape[0]
    full = lambda s: pl.BlockSpec(s, lambda i: (0,) * len(s))
    wide = lambda w: jax.ShapeDtypeStruct((n, w), F32)
    return pl.pallas_call(
        _sproj_kernel,
        grid=(1,),
        in_specs=[full((n, D_MODEL)), pl.BlockSpec((n, 3 * D_MODEL), lambda i: (0, 0)),
                  full((D_MODEL, W_COLS)),
                  full((CONV_W, A_QKV)), full((CONV_W - 1, n, A_QKV)),
                  full((1, LANES)), full((1, LANES)), full((1, LANES)), full((1, LANES))],
        out_specs=[full((n, A_WIDTH)), full((n, A_WIDTH)), full((n, A_WIDTH)), full((n, A_WIDTH)),
                   full((n, LANES)), full((n, B_WIDTH)), full((n, LANES)), full((n, LANES)),
                   full((n, B_WIDTH)), full((CONV_W - 1, n, A_QKV))],
        out_shape=[wide(A_WIDTH), wide(A_WIDTH), wide(A_WIDTH), wide(A_WIDTH), wide(LANES),
                   wide(B_WIDTH), wide(LANES), wide(LANES), wide(B_WIDTH),
                   jax.ShapeDtypeStruct((CONV_W - 1, n, A_QKV), F32)],
        compiler_params=pltpu.CompilerParams(dimension_semantics=("arbitrary",),
                                             vmem_limit_bytes=VMEM_LIMIT),
        name="sproj",
    )(x, mod_s, w_r, conv_w, cst, alog_row, dt_row, cos_row, sin_row)


def _sstep_kernel(sink_ref, q_ref, k_ref, v_ref, gb_ref, za_ref, na_ref, st_ref,
                  qb_ref, kn_ref, vn_ref, zb_ref, ck_ref, cv_ref,
                  oa_ref, ob_ref, nst_ref, nck_ref, ncv_ref,
                  o_scr, ob_scr):
    bt = q_ref.shape[0]
    gbv = gb_ref[...]

    for h in range(A_HEADS):
        hs = slice(h * A_DK, (h + 1) * A_DK)
        q_t = q_ref[:, hs].T
        k_t = k_ref[:, hs].T
        for bb in range(bt):
            eg = jnp.exp(gbv[bb:bb + 1, A_HEADS + h:A_HEADS + h + 1])
            beta = gbv[bb:bb + 1, h:h + 1]
            kcol = k_t[:, bb:bb + 1]
            qcol = q_t[:, bb:bb + 1]
            s1 = eg * st_ref[bb, h]
            pred = jnp.sum(kcol * s1, axis=0, keepdims=True)
            upd = beta * (v_ref[bb:bb + 1, hs] - pred)
            s2 = s1 + kcol * upd
            nst_ref[bb, h] = s2
            o_scr[bb:bb + 1, hs] = jnp.sum(qcol * s2, axis=0, keepdims=True)
    na = na_ref[...]
    for h in range(A_HEADS):
        hs = slice(h * A_DK, (h + 1) * A_DK)
        o = o_scr[:, hs]
        on = o * lax.rsqrt(jnp.mean(o * o, axis=-1, keepdims=True) + RMS_EPS) * na
        oa_ref[:, hs] = (on * _silu(za_ref[:, hs])).astype(BF16)

    row8 = lax.broadcasted_iota(jnp.int32, (B_HEADS, LANES), 0)
    lane8 = _lane((B_HEADS, LANES))
    own_half = (lane8 >= B_HD) == (row8 >= B_GROUP)
    rcol = lax.broadcasted_iota(jnp.int32, (B_HEADS, 1), 0)
    sink = jnp.zeros((B_HEADS, 1), F32)
    for r in range(B_HEADS):
        sink = jnp.where(rcol == r, sink_ref[r], sink)
    qv = qb_ref[...]
    qv_r = jnp.concatenate([pltpu.roll(qv[:, g * LANES:(g + 1) * LANES], B_HD, axis=1)
                            for g in range(B_WIDTH // LANES)], axis=-1)
    kn_t = kn_ref[...].T
    vn_t = vn_ref[...].T
    newest = _lane((LANES, WINDOW)) == WINDOW - 1
    qzs, scs = [], []
    for bb in range(bt):
        qz = jnp.zeros((B_HEADS, LANES), F32)
        for r in range(B_HEADS):
            grp, half, kh = r // 2, r % 2, r // B_GROUP
            src = qv if half == kh else qv_r
            qz = jnp.where(row8 == r, src[bb:bb + 1, grp * LANES:(grp + 1) * LANES], qz)
        qzs.append(jnp.where(own_half, qz, 0.0))
    for bb in range(bt):
        scs.append(_dot(qzs[bb], ck_ref[bb]))
    ps, pnews, dens = [], [], []
    for bb in range(bt):
        sc_new = jnp.sum(qzs[bb] * kn_ref[bb:bb + 1, :], axis=-1, keepdims=True)
        m = jnp.maximum(jnp.maximum(jnp.max(scs[bb], axis=-1, keepdims=True), sc_new), sink)
        p = jnp.exp(scs[bb] - m)
        p_new = jnp.exp(sc_new - m)
        ps.append(p)
        pnews.append(p_new)
        dens.append(jnp.sum(p, axis=-1, keepdims=True) + p_new + jnp.exp(sink - m))
    pvs = [_dot_nt(ps[bb], cv_ref[bb]) for bb in range(bt)]
    for bb in range(bt):
        o = (pvs[bb] + pnews[bb] * vn_ref[bb:bb + 1, :]) / dens[bb]
        o = jnp.where(own_half, o, 0.0)
        ob_scr[bb * B_HEADS:(bb + 1) * B_HEADS, :] = o + pltpu.roll(o, B_HD, axis=1)
    for bb in range(bt):
        nck_ref[bb] = jnp.where(newest, kn_t[:, bb:bb + 1], pltpu.roll(ck_ref[bb], WINDOW - 1, axis=1))
        ncv_ref[bb] = jnp.where(newest, vn_t[:, bb:bb + 1], pltpu.roll(cv_ref[bb], WINDOW - 1, axis=1))
    low = _lane((bt, LANES)) < B_HD
    for grp in range(B_WIDTH // LANES):
        even = ob_scr[pl.ds(2 * grp, bt, stride=B_HEADS), :]
        odd = ob_scr[pl.ds(2 * grp + 1, bt, stride=B_HEADS), :]
        gs = slice(grp * LANES, (grp + 1) * LANES)
        ob_ref[:, gs] = (jnp.where(low, even, odd) * _silu(zb_ref[:, gs])).astype(BF16)


def _sstep(sinks, q, k, v, gb, za, na_row, state, qb, kn, vn, zb, ck, cv):
    n = q.shape[0]
    bt = STEP_BT
    row = lambda w: pl.BlockSpec((bt, w), lambda i: (i, 0))
    st_spec = pl.BlockSpec((bt, A_HEADS, A_DK, A_DV), lambda i: (i, 0, 0, 0))
    c_spec = pl.BlockSpec((bt, WINDOW, LANES), lambda i: (i, 0, 0))
    return pl.pallas_call(
        _sstep_kernel,
        grid=(n // bt,),
        in_specs=[pl.BlockSpec(memory_space=pltpu.SMEM),
                  row(A_WIDTH), row(A_WIDTH), row(A_WIDTH), row(LANES), row(A_WIDTH),
                  pl.BlockSpec((1, A_DV), lambda i: (0, 0)), st_spec,
                  row(B_WIDTH), row(LANES), row(LANES), row(B_WIDTH), c_spec, c_spec],
        out_specs=[row(A_WIDTH), row(B_WIDTH), st_spec, c_spec, c_spec],
        out_shape=[jax.ShapeDtypeStruct((n, A_WIDTH), BF16),
                   jax.ShapeDtypeStruct((n, B_WIDTH), BF16),
                   jax.ShapeDtypeStruct((n, A_HEADS, A_DK, A_DV), F32),
                   jax.ShapeDtypeStruct((n, WINDOW, LANES), F32),
                   jax.ShapeDtypeStruct((n, WINDOW, LANES), F32)],
        scratch_shapes=[pltpu.VMEM((bt, A_WIDTH), F32), pltpu.VMEM((bt * B_HEADS, LANES), F32)],
        compiler_params=pltpu.CompilerParams(dimension_semantics=("arbitrary",),
                                             vmem_limit_bytes=VMEM_LIMIT),
        name="sstep",
    )(sinks, q, k, v, gb, za, na_row, state, qb, kn, vn, zb, ck, cv)


def _rope_tables(pos):
    half = B_HD // 2
    inv = 1.0 / (ROPE_THETA ** (np.arange(half, dtype=np.float64) / half))
    ang = np.asarray(pos, np.float64)[:, None] * inv[None, :]
    cos, sin = np.cos(ang), np.sin(ang)
    reps = LANES // B_HD
    return (jnp.asarray(np.tile(np.concatenate([cos, cos], -1), (1, reps)), F32),
            jnp.asarray(np.tile(np.concatenate([-sin, sin], -1), (1, reps)), F32))


def _pad_row(vec, offset):
    return jnp.zeros((1, LANES), F32).at[0, offset:offset + vec.shape[0]].set(vec.astype(F32))


def _layer(x_prompt, x_sample, state_conv, state_delta, cache_k, cache_v, c_prompt, c_sample,
           w_ada, b_ada, w_in, conv_w, a_log, dt_bias, norm_a, sinks, w_out, ln_g, ln_b):
    bsz, seq, _ = x_prompt.shape
    n_s = x_sample.shape[0]

    w_r = _wprep(jnp.swapaxes(w_in, 0, 1))
    w_o = w_out.astype(BF16)
    alog_row = _pad_row(a_log, A_HEADS)
    dt_row = _pad_row(dt_bias, A_HEADS)
    na_row = norm_a.reshape(1, A_DV)
    g_row = ln_g.reshape(1, D_MODEL)
    b_row = ln_b.reshape(1, D_MODEL)

    assert n_s % 8 == 0 and bsz <= 8
    c_all = jnp.concatenate([c_sample, c_prompt, jnp.zeros((8 - bsz, D_MODEL), F32)], axis=0)
    mod = _ada(c_all, w_ada, b_ada.reshape(1, 3 * D_MODEL))

    cos_p, sin_p = _rope_tables(np.arange(seq))
    (q, k, v, za, gb, qb, kb, kbr, vd0, vd1, zb, conv_p, kb_last, vb_last) = _proj(
        x_prompt, mod, n_s, w_r, conv_w, alog_row, dt_row, cos_p, sin_p)
    oa, delta_p = _delta(q, k, v, gb, za, na_row)
    ob = _swa(sinks, qb, kb, kbr, vd0, vd1, zb)
    y_p = _out(oa, ob, x_prompt, mod, n_s, True, w_o, g_row, b_row, OUT_TM)
    swa_k_p = kb_last.reshape(bsz, WINDOW, B_KV_HEADS, B_HD)
    swa_v_p = vb_last.reshape(bsz, WINDOW, B_KV_HEADS, B_HD)

    cos_s, sin_s = _rope_tables(np.array([PAST_LEN]))
    xs = x_sample.reshape(n_s, D_MODEL)
    cst = jnp.transpose(state_conv, (1, 0, 2))
    sq, sk, sv, sza, sgb, sqb, skn, svn, szb, ncs = _sproj(xs, mod, w_r, conv_w, cst, alog_row, dt_row,
                                                           cos_s, sin_s)
    soa, sob, delta_s, nck, ncv = _sstep(sinks, sq, sk, sv, sgb, sza, na_row, state_delta,
                                         sqb, skn, svn, szb,
                                         jnp.swapaxes(cache_k.reshape(n_s, WINDOW, LANES), 1, 2),
                                         jnp.swapaxes(cache_v.reshape(n_s, WINDOW, LANES), 1, 2))
    y_s = _out(soa[None], sob[None], xs[None], mod, 0, False, w_o, g_row, b_row, n_s)
    conv_s = jnp.transpose(ncs, (1, 0, 2))
    unpack = lambda c: jnp.swapaxes(c, 1, 2).reshape(n_s, WINDOW, B_KV_HEADS, B_HD)
    return (y_p, y_s.reshape(n_s, 1, D_MODEL), conv_p, delta_p, swa_k_p, swa_v_p,
            conv_s, delta_s, unpack(nck), unpack(ncv))


def kernel(x_prompt, x_sample, state_conv, state_delta, cache_swa_k, cache_swa_v, c_prompt, c_sample,
           w_ada, b_ada, w_in, conv_w, a_log, dt_bias, norm_a, sinks, w_out, ln_g, ln_b):
    assert w_ada.shape[0] == DEPTH == 1
    outs = _layer(x_prompt, x_sample, state_conv[0], state_delta[0], cache_swa_k[0], cache_swa_v[0],
                  c_prompt, c_sample, w_ada[0], b_ada[0], w_in[0], conv_w[0], a_log[0], dt_bias[0],
                  norm_a[0], sinks[0], w_out[0], ln_g[0], ln_b[0])
    y_p, y_s = outs[0], outs[1]
    return (y_p, y_s) + tuple(o[None] for o in outs[2:])
```

```python
import functools

import jax
import jax.numpy as jnp
import numpy as np
from jax import lax
from jax.experimental import pallas as pl
from jax.experimental.pallas import tpu as pltpu

F32 = jnp.float32
BF16 = jnp.bfloat16

D_MODEL = 1024
DEPTH = 1
PAST_LEN = 8192
A_HEADS = 4
A_DK = 128
A_DV = 128
A_WIDTH = A_HEADS * A_DV
A_QKV = 3 * A_WIDTH
CONV_W = 4
CHUNK = 64
B_HEADS = 8
B_KV_HEADS = 2
B_HD = 64
B_GROUP = B_HEADS // B_KV_HEADS
B_WIDTH = B_HEADS * B_HD
B_KV_WIDTH = B_KV_HEADS * B_HD
WINDOW = 128
ROPE_THETA = 10000.0
MIX_WIDTH = A_WIDTH + B_WIDTH
DEEPNORM_ALPHA = (2 * DEPTH) ** 0.25
LOG2E = 1.4426950408889634
LN_EPS = 1e-5
RMS_EPS = 1e-6
L2_EPS = 1e-6

OFF_A_Z = A_QKV
OFF_A_BETA = OFF_A_Z + A_WIDTH
OFF_A_DECAY = OFF_A_BETA + A_HEADS
OFF_B_Q = OFF_A_DECAY + A_HEADS
OFF_B_K = OFF_B_Q + B_WIDTH
OFF_B_V = OFF_B_K + B_KV_WIDTH
OFF_B_Z = OFF_B_V + B_KV_WIDTH
PROJ_COLS = OFF_B_Z + B_WIDTH

LANES = 128
C_QKV = 0
C_ZA = C_QKV + A_QKV
C_QB = C_ZA + A_WIDTH
C_KB = C_QB + B_WIDTH
C_VB = C_KB + B_KV_WIDTH
C_ZB = C_VB + B_KV_WIDTH
C_BD = C_ZB + B_WIDTH
WPREP_TN = 256
W_COLS = C_BD + WPREP_TN

VMEM_LIMIT = 56 * 1024 * 1024

PROJ_TM = 512
DELTA_CT = 256
DELTA_WAVE = 2
SWA_TQ = 512
SWA_WAVE = 1
OUT_TM = 1024
STEP_BT = 16


def _dot(a, b):
    return jnp.dot(a, b, preferred_element_type=F32)


def _dot_nt(a, b):
    return lax.dot_general(a, b, (((1,), (1,)), ((), ())), preferred_element_type=F32)


def _silu(x):
    return x * jax.nn.sigmoid(x)


def _softplus(x):
    return jnp.maximum(x, 0.0) + jnp.log1p(jnp.exp(-jnp.abs(x)))


def _lane(shape):
    return lax.broadcasted_iota(jnp.int32, shape, len(shape) - 1)


def _l2norm_heads(y, scale):
    outs = []
    for h in range(A_HEADS):
        xh = y[:, h * A_DK:(h + 1) * A_DK]
        ss = jnp.sum(xh * xh, axis=-1, keepdims=True)
        xn = xh * lax.rsqrt(ss + L2_EPS)
        outs.append(xn * scale if scale != 1.0 else xn)
    return jnp.concatenate(outs, axis=-1)


def _rotary_group(xg, cos, sin_signed):
    lane = _lane(xg.shape)
    swapped = jnp.where((lane % B_HD) < (B_HD // 2),
                        pltpu.roll(xg, LANES - B_HD // 2, axis=1),
                        pltpu.roll(xg, B_HD // 2, axis=1))
    return xg * cos + swapped * sin_signed


def _kv_layouts(kb, vb):
    low = _lane(kb.shape) < B_HD
    kbr = pltpu.roll(kb, B_HD, axis=1)
    vbr = pltpu.roll(vb, B_HD, axis=1)
    return kb, kbr, jnp.where(low, vb, vbr), jnp.where(low, vbr, vb)


def _gate_lanes(bd, alog_row, dt_row):
    lane = _lane(bd.shape)
    g = -jnp.exp(alog_row) * _softplus(bd + dt_row)
    return jnp.where(lane < A_HEADS, jax.nn.sigmoid(bd), g)


def _layer_norm(r, g, b):
    mu = jnp.mean(r, axis=-1, keepdims=True)
    d = r - mu
    var = jnp.mean(d * d, axis=-1, keepdims=True)
    return d * lax.rsqrt(var + LN_EPS) * g + b


def _wprep_kernel(wt_ref, o_ref):
    x = wt_ref[...]
    tail = pl.program_id(0) == pl.num_programs(0) - 1
    row = lax.broadcasted_iota(jnp.int32, x.shape, 0)
    x = jnp.where(jnp.logical_and(tail, row >= 2 * A_HEADS), 0.0, x)
    o_ref[...] = x.T.astype(BF16)


def _wprep(w_t):
    tn = WPREP_TN
    n_a, n_b = OFF_A_BETA // tn, (PROJ_COLS - OFF_B_Q) // tn
    assert n_a * tn == OFF_A_BETA and n_b * tn == PROJ_COLS - OFF_B_Q and OFF_A_BETA + tn <= PROJ_COLS

    def src_row(j):
        return jnp.where(j < n_a, j * tn, jnp.where(j < n_a + n_b, OFF_B_Q + (j - n_a) * tn, OFF_A_BETA))

    return pl.pallas_call(
        _wprep_kernel,
        grid=(n_a + n_b + 1,),
        in_specs=[pl.BlockSpec((pl.Element(tn), pl.Element(D_MODEL)),
                               lambda j: (pl.multiple_of(src_row(j), 8), 0))],
        out_specs=pl.BlockSpec((D_MODEL, tn), lambda j: (0, j)),
        out_shape=jax.ShapeDtypeStruct((D_MODEL, W_COLS), BF16),
        compiler_params=pltpu.CompilerParams(dimension_semantics=("arbitrary",),
                                             vmem_limit_bytes=VMEM_LIMIT),
        name="wprep",
    )(w_t)


def _ada_kernel(c_ref, w_ref, b_ref, o_ref):
    o_ref[...] = _dot(c_ref[...].astype(BF16), w_ref[...].astype(BF16)) + b_ref[...]


def _ada(c_all, w_ada, b_ada):
    rows = c_all.shape[0]
    tn = 768
    return pl.pallas_call(
        _ada_kernel,
        grid=(3 * D_MODEL // tn,),
        in_specs=[pl.BlockSpec((rows, D_MODEL), lambda j: (0, 0)),
                  pl.BlockSpec((D_MODEL, tn), lambda j: (0, j)),
                  pl.BlockSpec((1, tn), lambda j: (0, j))],
        out_specs=pl.BlockSpec((rows, tn), lambda j: (0, j)),
        out_shape=jax.ShapeDtypeStruct((rows, 3 * D_MODEL), F32),
        compiler_params=pltpu.CompilerParams(dimension_semantics=("arbitrary",),
                                             vmem_limit_bytes=VMEM_LIMIT),
        name="ada",
    )(c_all, w_ada, b_ada)


def _proj_kernel(x_ref, mod_ref, w_ref, cw_ref, alog_ref, dt_ref, cos_ref, sin_ref,
                 q_ref, k_ref, v_ref, za_ref, gb_ref, qb_ref, kb_ref, kbr_ref, vd0_ref, vd1_ref,
                 zb_ref, cst_ref, kbl_ref, vbl_ref, ubuf):
    tm = x_ref.shape[1]
    t = pl.program_id(1)

    @pl.when(t == 0)
    def _():
        ubuf[...] = jnp.zeros(ubuf.shape, F32)

    brow = pl.ds(pl.program_id(0), 1)
    shift = mod_ref[brow, 0:D_MODEL]
    scale = mod_ref[brow, D_MODEL:2 * D_MODEL]
    h = (x_ref[0] * (1.0 + scale) + shift).astype(BF16)

    sub = lax.broadcasted_iota(jnp.int32, (tm // 8, 8, A_WIDTH), 1)
    for gi, o_ref in enumerate((q_ref, k_ref, v_ref)):
        cs = slice(gi * A_WIDTH, (gi + 1) * A_WIDTH)
        u = _dot(h, w_ref[:, cs])
        groups = jnp.concatenate([ubuf[:, cs], u], axis=0).reshape(tm // 8 + 1, 8, A_WIDTH)
        acc = None
        for j in range(CONV_W - 1, 0, -1):
            rot = pltpu.roll(groups, j, axis=1)
            term = (jnp.where(sub < j, rot[:-1], rot[1:]).reshape(tm, A_WIDTH)
                    * cw_ref[CONV_W - 1 - j:CONV_W - j, cs])
            acc = term if acc is None else acc + term
        y = _silu(acc + u * cw_ref[CONV_W - 1:CONV_W, cs])
        if gi == 0:
            y = _l2norm_heads(y, A_DK ** -0.5)
        elif gi == 1:
            y = _l2norm_heads(y, 1.0)
        o_ref[0] = y
        ubuf[:, cs] = u[tm - 8:tm]
        cst_ref[0, :, cs] = u[tm - (CONV_W - 1):tm]

    za_ref[0] = _dot(h, w_ref[:, C_ZA:C_ZA + A_WIDTH])
    gb_ref[0] = _gate_lanes(_dot(h, w_ref[:, C_BD:C_BD + LANES]), alog_ref[...], dt_ref[...])
    cos = cos_ref[...]
    sin = sin_ref[...]
    uq = _dot(h, w_ref[:, C_QB:C_QB + B_WIDTH])
    for g in range(B_WIDTH // LANES):
        qb_ref[0, :, g * LANES:(g + 1) * LANES] = (
            _rotary_group(uq[:, g * LANES:(g + 1) * LANES], cos, sin) * (B_HD ** -0.5 * LOG2E)).astype(BF16)
    ukv = _dot(h, w_ref[:, C_KB:C_KB + 2 * LANES])
    kb = _rotary_group(ukv[:, 0:LANES], cos, sin)
    vb = ukv[:, LANES:2 * LANES]
    for o_ref, val in zip((kb_ref, kbr_ref, vd0_ref, vd1_ref), _kv_layouts(kb, vb)):
        o_ref[0] = val.astype(BF16)
    zb_ref[0] = _dot(h, w_ref[:, C_ZB:C_ZB + B_WIDTH])

    @pl.when(t == pl.num_programs(1) - 1)
    def _():
        kbl_ref[0] = kb[tm - WINDOW:tm]
        vbl_ref[0] = vb[tm - WINDOW:tm]


def _proj(x, mod, mod_row0, w_r, conv_w, alog_row, dt_row, cos_t, sin_t):
    bsz, t, _ = x.shape
    tm = PROJ_TM
    row = lambda w: pl.BlockSpec((1, tm, w), lambda b, i: (b, i, 0))
    const2 = lambda s: pl.BlockSpec(s, lambda b, i: (0, 0))
    per_b = lambda r, w: pl.BlockSpec((1, r, w), lambda b, i: (b, 0, 0))
    wide = lambda w, dt=F32: jax.ShapeDtypeStruct((bsz, t, w), dt)
    return pl.pallas_call(
        _proj_kernel,
        grid=(bsz, t // tm),
        in_specs=[row(D_MODEL),
                  pl.BlockSpec((8, 3 * D_MODEL), lambda b, i: (mod_row0 // 8, 0)),
                  const2((D_MODEL, W_COLS)),
                  const2((CONV_W, A_QKV)),
                  const2((1, LANES)), const2((1, LANES)),
                  pl.BlockSpec((tm, LANES), lambda b, i: (i, 0)),
                  pl.BlockSpec((tm, LANES), lambda b, i: (i, 0))],
        out_specs=[row(A_WIDTH), row(A_WIDTH), row(A_WIDTH), row(A_WIDTH), row(LANES),
                   row(B_WIDTH), row(LANES), row(LANES), row(LANES), row(LANES), row(B_WIDTH),
                   per_b(CONV_W - 1, A_QKV), per_b(WINDOW, LANES), per_b(WINDOW, LANES)],
        out_shape=[wide(A_WIDTH), wide(A_WIDTH), wide(A_WIDTH), wide(A_WIDTH), wide(LANES),
                   wide(B_WIDTH, BF16), wide(LANES, BF16), wide(LANES, BF16), wide(LANES, BF16),
                   wide(LANES, BF16), wide(B_WIDTH),
                   jax.ShapeDtypeStruct((bsz, CONV_W - 1, A_QKV), F32),
                   jax.ShapeDtypeStruct((bsz, WINDOW, LANES), F32),
                   jax.ShapeDtypeStruct((bsz, WINDOW, LANES), F32)],
        scratch_shapes=[pltpu.VMEM((8, A_QKV), F32)],
        compiler_params=pltpu.CompilerParams(dimension_semantics=("arbitrary", "arbitrary"),
                                             vmem_limit_bytes=VMEM_LIMIT),
        name="proj",
    )(x, mod, w_r, conv_w, alog_row, dt_row, cos_t, sin_t)


def _delta_kernel(q_ref, k_ref, v_ref, gb_ref, za_ref, na_ref, oa_ref, st_ref,
                  s_scr, wq_s, ut_s, akd_s, gl_s):
    bsz, ct = q_ref.shape[0], q_ref.shape[1]
    nch = ct // CHUNK
    t = pl.program_id(0)
    wslot = t % 2
    rslot = 1 - wslot

    @pl.when(t == 0)
    def _():
        s_scr[...] = jnp.zeros(s_scr.shape, F32)
        wq_s[...] = jnp.zeros(wq_s.shape, BF16)
        ut_s[...] = jnp.zeros(ut_s.shape, F32)
        akd_s[...] = jnp.zeros(akd_s.shape, BF16)
        gl_s[...] = jnp.zeros(gl_s.shape, F32)

    units = [(b, c, h) for b in range(bsz) for c in range(nch) for h in range(A_HEADS)]
    uid = {u_: i for i, u_ in enumerate(units)}
    rows = lambda c: slice(c * CHUNK, (c + 1) * CHUNK)
    lanes = lambda h: slice(h * A_DK, (h + 1) * A_DK)
    na = na_ref[...]

    s_cur = {(b, h): s_scr[b * A_HEADS + h] for b in range(bsz) for h in range(A_HEADS)}
    ws, uu = {}, {}

    def rec_ws(c):
        for b in range(bsz):
            for h in range(A_HEADS):
                i = uid[b, c, h]
                ws[b, h] = _dot(wq_s[rslot, i], s_cur[b, h].astype(BF16))
                uu[b, h] = (ut_s[rslot, i] - ws[b, h][:CHUNK]).astype(BF16)

    def rec_ou(c):
        zpad = jnp.zeros((CHUNK, A_DV), BF16)
        for b in range(bsz):
            u_bd = jnp.concatenate(
                [jnp.concatenate([uu[b, h] if hh == h else zpad for hh in range(A_HEADS)], axis=-1)
                 for h in range(A_HEADS)], axis=0)
            ou = _dot(akd_s[rslot, b * nch + c], u_bd)
            for h in range(A_HEADS):
                o = ws[b, h][CHUNK:] + ou[:CHUNK, lanes(h)]
                s_cur[b, h] = gl_s[rslot, uid[b, c, h]] * s_cur[b, h] + ou[CHUNK:, lanes(h)]
                on = o * lax.rsqrt(jnp.mean(o * o, axis=-1, keepdims=True) + RMS_EPS) * na
                oa_ref[b, rows(c), lanes(h)] = (on * _silu(za_ref[b, rows(c), lanes(h)])).astype(BF16)

    rec_stages = []
    for c in range(nch):
        rec_stages += [lambda c=c: rec_ws(c), lambda c=c: rec_ou(c)]

    def run_rec(n_left_after):
        while rec_stages and len(rec_stages) > n_left_after:
            rec_stages.pop(0)()

    pk = A_HEADS * CHUNK
    low = _lane((CHUNK, LANES)) < CHUNK
    low_row = _lane((1, LANES)) < CHUNK
    ti_p = lax.broadcasted_iota(jnp.int32, (CHUNK, pk), 0)
    ii_p = _lane((CHUNK, pk)) % CHUNK
    zero64 = jnp.zeros((CHUNK, LANES), BF16)

    def pack(parts):
        return jnp.concatenate([jnp.where(low, parts[0], parts[1]), jnp.where(low, parts[2], parts[3])], axis=-1)

    def block_diag(x16):
        blocks = []
        for h in range(A_HEADS):
            pair, first = h // 2, h % 2 == 0
            piece = jnp.where(low if first else jnp.logical_not(low), x16[:, pair * LANES:(pair + 1) * LANES], zero64)
            blocks.append(jnp.concatenate([piece, zero64] if pair == 0 else [zero64, piece], axis=-1))
        return jnp.concatenate(blocks, axis=0)

    zrhs = jnp.zeros((CHUNK, 2 * A_DK), BF16)
    n_rec = len(rec_stages)
    n_slots = 8 * (bsz // DELTA_WAVE)
    done = [0]

    def stage_done():
        done[0] += 1
        run_rec(n_rec - (done[0] * n_rec) // n_slots)

    def decay_terms(b, beta, g_col, g_last, eg, dec_p, beta_p):
        gbv = gb_ref[b]
        rin = lax.broadcasted_iota(jnp.int32, gbv.shape, 0) % CHUNK
        gcs = gbv
        s = 1
        while s < CHUNK:
            gcs = gcs + jnp.where(rin >= s, pltpu.roll(gcs, s, axis=0), 0.0)
            s *= 2
        gcs_t = gcs.T
        for c in range(nch):
            r0 = c * CHUNK
            pair_lanes = slice((c // 2) * LANES, (c // 2 + 1) * LANES)
            g_rows = []
            for h in range(A_HEADS):
                u_ = (b, c, h)
                beta[u_] = jnp.broadcast_to(gbv[rows(c), h:h + 1], (CHUNK, A_DK))
                g_col[u_] = jnp.broadcast_to(gcs[rows(c), A_HEADS + h:A_HEADS + h + 1], (CHUNK, A_DK))
                g_last[u_] = gcs[r0 + CHUNK - 1:r0 + CHUNK, A_HEADS + h:A_HEADS + h + 1]
                eg[u_] = jnp.exp(g_col[u_])
                g_row = gcs_t[A_HEADS + h:A_HEADS + h + 1, pair_lanes]
                g_rows.append(g_row if c % 2 == h % 2 else pltpu.roll(g_row, CHUNK, axis=1))
            g_row_p = jnp.concatenate([jnp.where(low_row, g_rows[0], g_rows[1]),
                                       jnp.where(low_row, g_rows[2], g_rows[3])], axis=-1)
            g_col_p = pack([g_col[b, c, h] for h in range(A_HEADS)])
            dec_p[b, c] = jnp.exp(jnp.where(ti_p >= ii_p, g_col_p - g_row_p, -jnp.inf))
            beta_p[b, c] = pack([beta[b, c, h] for h in range(A_HEADS)])

    def prepare(bs):
        groups_b = [(b, c) for b in bs for c in range(nch)]
        beta, g_col, g_last, eg, dec_p, beta_p = {}, {}, {}, {}, {}, {}
        for b in bs:
            decay_terms(b, beta, g_col, g_last, eg, dec_p, beta_p)

        nmat = {}
        for (b, c) in groups_b:
            k16 = k_ref[b, rows(c), :].astype(BF16)
            q16 = q_ref[b, rows(c), :].astype(BF16)
            k_heads = jnp.concatenate(
                [jnp.concatenate([k16[:, lanes(h)] if hh == h else zero64 for hh in range(A_HEADS)], axis=-1)
                 for h in range(A_HEADS)], axis=0)
            kq = _dot_nt(jnp.concatenate([k16, q16], axis=0), k_heads)
            nmat[b, c] = -(beta_p[b, c] * kq[:CHUNK] * jnp.where(ti_p > ii_p, dec_p[b, c], 0.0))
            akd_s[wslot, b * nch + c, 0:CHUNK, :] = (kq[CHUNK:] * dec_p[b, c]).astype(BF16)
        stage_done()

        rsum = dict(nmat)
        pw16 = {g_: nmat[g_].astype(BF16) for g_ in groups_b}
        pw = {g_: _dot(pw16[g_], block_diag(pw16[g_])) for g_ in groups_b}
        stage_done()
        for step in range(1, 6):
            last = step == 5
            pw16 = {g_: pw[g_].astype(BF16) for g_ in groups_b}
            rp = {}
            for g_ in groups_b:
                r16 = rsum[g_].astype(BF16)
                rp[g_] = _dot(r16 if last else jnp.concatenate([r16, pw16[g_]], axis=0), block_diag(pw16[g_]))
            for g_ in groups_b:
                rsum[g_] = rsum[g_] + pw[g_] + rp[g_][:CHUNK]
                if not last:
                    pw[g_] = rp[g_][CHUNK:]
            stage_done()

        for (b, c) in groups_b:
            for h in range(A_HEADS):
                u_ = (b, c, h)
                i = uid[u_]
                kc = k_ref[b, rows(c), lanes(h)]
                rhs = jnp.concatenate([(beta[u_] * eg[u_]) * kc, beta[u_] * v_ref[b, rows(c), lanes(h)]],
                                      axis=-1)
                rhs16 = rhs.astype(BF16)
                rhs_rows = jnp.concatenate([rhs16 if hh == h else zrhs for hh in range(A_HEADS)], axis=0)
                sol = rhs + _dot(rsum[b, c].astype(BF16), rhs_rows)
                wq_s[wslot, i] = jnp.concatenate([sol[:, :A_DK], eg[u_] * q_ref[b, rows(c), lanes(h)]],
                                                 axis=0).astype(BF16)
                ut_s[wslot, i] = sol[:, A_DK:]
                gl_s[wslot, i] = jnp.broadcast_to(jnp.exp(g_last[u_]), (1, A_DV))
        for (b, c) in groups_b:
            kd = [jnp.exp(g_last[b, c, h] - g_col[b, c, h]) * k_ref[b, rows(c), lanes(h)]
                  for h in range(A_HEADS)]
            for p in range(A_HEADS // 2):
                akd_s[wslot, b * nch + c, CHUNK:, p * LANES:(p + 1) * LANES] = (
                    jnp.concatenate([kd[2 * p], kd[2 * p + 1]], axis=0).T.astype(BF16))
        stage_done()

    for b0 in range(0, bsz, DELTA_WAVE):
        prepare(range(b0, b0 + DELTA_WAVE))
    run_rec(0)

    for b in range(bsz):
        for h in range(A_HEADS):
            s_scr[b * A_HEADS + h] = s_cur[b, h]

    @pl.when(t == pl.num_programs(0) - 1)
    def _():
        for b in range(bsz):
            for h in range(A_HEADS):
                st_ref[b, h] = s_cur[b, h]


def _delta(q, k, v, gb, za, na_row):
    bsz, t, _ = q.shape
    ct = DELTA_CT
    nt = t // ct
    n_units = bsz * (ct // CHUNK) * A_HEADS
    prep = lambda w: pl.BlockSpec((bsz, ct, w), lambda i: (0, jnp.minimum(i, nt - 1), 0))
    rec = lambda w: pl.BlockSpec((bsz, ct, w), lambda i: (0, jnp.maximum(i - 1, 0), 0))
    return pl.pallas_call(
        _delta_kernel,
        grid=(nt + 1,),
        in_specs=[prep(A_WIDTH), prep(A_WIDTH), prep(A_WIDTH), prep(LANES), rec(A_WIDTH),
                  pl.BlockSpec((1, A_DV), lambda i: (0, 0))],
        out_specs=[rec(A_WIDTH),
                   pl.BlockSpec((bsz, A_HEADS, A_DK, A_DV), lambda i: (0, 0, 0, 0))],
        out_shape=[jax.ShapeDtypeStruct((bsz, t, A_WIDTH), BF16),
                   jax.ShapeDtypeStruct((bsz, A_HEADS, A_DK, A_DV), F32)],
        scratch_shapes=[pltpu.VMEM((bsz * A_HEADS, A_DK, A_DV), F32),
                        pltpu.VMEM((2, n_units, 2 * CHUNK, A_DK), BF16),
                        pltpu.VMEM((2, n_units, CHUNK, A_DV), F32),
                        pltpu.VMEM((2, n_units // A_HEADS, CHUNK + A_DK, A_HEADS * CHUNK), BF16),
                        pltpu.VMEM((2, n_units, 1, A_DV), F32)],
        compiler_params=pltpu.CompilerParams(dimension_semantics=("arbitrary",),
                                             vmem_limit_bytes=VMEM_LIMIT),
        name="delta",
    )(q, k, v, gb, za, na_row)


def _swa_kernel(sink_ref, qb_ref, kc_ref, kp_ref, krc_ref, krp_ref, v0c_ref, v0p_ref, v1c_ref, v1p_ref,
                zb_ref, ob_ref):
    n = pl.program_id(1)
    tq = qb_ref.shape[1]
    blk = WINDOW
    kx = (jnp.concatenate([kp_ref[0], kc_ref[0]], axis=0), jnp.concatenate([krp_ref[0], krc_ref[0]], axis=0))
    vd = (jnp.concatenate([v0p_ref[0], v0c_ref[0]], axis=0), jnp.concatenate([v1p_ref[0], v1c_ref[0]], axis=0))

    a = lax.broadcasted_iota(jnp.int32, (2 * blk, 2 * blk), 0) % blk
    j = lax.broadcasted_iota(jnp.int32, (2 * blk, 2 * blk), 1)
    rel = a + blk - j
    band = (rel >= 0) & (rel <= WINDOW)
    band_first = band & ((n > 0) | (j >= blk))
    top = lax.broadcasted_iota(jnp.int32, (2 * blk, 1), 0) < blk
    low = _lane((blk, LANES)) < B_HD
    zero = jnp.zeros((blk, LANES), BF16)

    qrows = lambda i: slice(i * blk, (i + 1) * blk)
    krows = lambda i: slice(i * blk, (i + 2) * blk)
    sink = {(kh, half): jnp.where(top, sink_ref[kh * B_GROUP + half] * LOG2E,
                                  sink_ref[kh * B_GROUP + half + 2] * LOG2E)
            for kh in range(B_KV_HEADS) for half in range(2)}
    for i0 in range(0, tq // blk, SWA_WAVE):
        blocks = range(i0, i0 + SWA_WAVE)
        units = [(i, kh, half) for i in blocks for kh in range(B_KV_HEADS) for half in range(2)]
        sc = {}
        for (i, kh, half) in units:
            qs = []
            for g in range(2):
                grp = kh * 2 + g
                xg = qb_ref[0, qrows(i), grp * LANES:(grp + 1) * LANES]
                qs.append(jnp.where(low if half == 0 else jnp.logical_not(low), xg, zero))
            qz = jnp.concatenate(qs, axis=0)
            sc[i, kh, half] = _dot_nt(qz, kx[0 if kh == half else 1][krows(i)])
        p, den = {}, {}
        for u_ in units:
            i, kh, half = u_
            s_m = jnp.where(band_first if i == 0 else band, sc[u_], -jnp.inf)
            m = jnp.maximum(jnp.max(s_m, axis=-1, keepdims=True), sink[kh, half])
            e = jnp.exp2(s_m - m)
            den[u_] = jnp.sum(e, axis=-1, keepdims=True) + jnp.exp2(sink[kh, half] - m)
            p[u_] = e.astype(BF16)
        pv = {u_: _dot(p[u_], vd[u_[1]][krows(u_[0])]) for u_ in units}
        outs = {u_: pv[u_] / den[u_] for u_ in units}
        for i in blocks:
            for grp in range(B_WIDTH // LANES):
                kh, g = grp // 2, grp % 2
                og = jnp.where(low, outs[i, kh, 0][g * blk:(g + 1) * blk], outs[i, kh, 1][g * blk:(g + 1) * blk])
                gs = slice(grp * LANES, (grp + 1) * LANES)
                ob_ref[0, qrows(i), gs] = (og * _silu(zb_ref[0, qrows(i), gs])).astype(BF16)


def _swa(sinks, qb, kb, kbr, vd0, vd1, zb):
    bsz, t, _ = qb.shape
    tq = SWA_TQ
    per = tq // WINDOW
    cur = lambda w: pl.BlockSpec((1, tq, w), lambda b, i: (b, i, 0))
    prev = pl.BlockSpec((1, WINDOW, LANES), lambda b, i: (b, jnp.maximum(i * per - 1, 0), 0))
    return pl.pallas_call(
        _swa_kernel,
        grid=(bsz, t // tq),
        in_specs=[pl.BlockSpec(memory_space=pltpu.SMEM), cur(B_WIDTH),
                  cur(LANES), prev, cur(LANES), prev, cur(LANES), prev, cur(LANES), prev,
                  cur(B_WIDTH)],
        out_specs=cur(B_WIDTH),
        out_shape=jax.ShapeDtypeStruct((bsz, t, B_WIDTH), BF16),
        compiler_params=pltpu.CompilerParams(dimension_semantics=("arbitrary", "arbitrary"),
                                             vmem_limit_bytes=VMEM_LIMIT),
        name="swa",
    )(sinks, qb, kb, kb, kbr, kbr, vd0, vd0, vd1, vd1, zb)


def _out_kernel(oa_ref, ob_ref, x_ref, gate_ref, w_ref, g_ref, b_ref, y_ref, *, gate_per_batch):
    mix = _dot(oa_ref[0], w_ref[0:A_WIDTH, :]) + _dot(ob_ref[0], w_ref[A_WIDTH:MIX_WIDTH, :])
    gate = gate_ref[pl.ds(pl.program_id(0), 1), :] if gate_per_batch else gate_ref[...]
    r = DEEPNORM_ALPHA * x_ref[0] + (1.0 + gate) * mix
    y_ref[0] = _layer_norm(r, g_ref[...], b_ref[...])


def _out(oa, ob, x, mod, mod_row0, gate_per_batch, w_out, ln_g, ln_b, tm):
    bsz, t, _ = x.shape
    grows = 8 if gate_per_batch else tm
    row = lambda w: pl.BlockSpec((1, tm, w), lambda b, i: (b, i, 0))
    const2 = lambda s: pl.BlockSpec(s, lambda b, i: (0, 0))
    return pl.pallas_call(
        functools.partial(_out_kernel, gate_per_batch=gate_per_batch),
        grid=(bsz, t // tm),
        in_specs=[row(A_WIDTH), row(B_WIDTH), row(D_MODEL),
                  pl.BlockSpec((grows, D_MODEL), lambda b, i: (mod_row0 // grows, 2)),
                  const2((MIX_WIDTH, D_MODEL)), const2((1, D_MODEL)), const2((1, D_MODEL))],
        out_specs=row(D_MODEL),
        out_shape=jax.ShapeDtypeStruct((bsz, t, D_MODEL), F32),
        compiler_params=pltpu.CompilerParams(dimension_semantics=("arbitrary", "arbitrary"),
                                             vmem_limit_bytes=VMEM_LIMIT),
        name="out",
    )(oa, ob, x, mod, w_out, ln_g, ln_b)


def _sproj_kernel(x_ref, mod_ref, w_ref, cw_ref, cst_ref, alog_ref, dt_ref, cos_ref, sin_ref,
                  q_ref, k_ref, v_ref, za_ref, gb_ref, qb_ref, kb_ref, vb_ref, zb_ref, ncs_ref):
    shift = mod_ref[:, 0:D_MODEL]
    scale = mod_ref[:, D_MODEL:2 * D_MODEL]
    h = (x_ref[...] * (1.0 + scale) + shift).astype(BF16)

    for gi, o_ref in enumerate((q_ref, k_ref, v_ref)):
        c0 = gi * A_WIDTH
        cs = slice(c0, c0 + A_WIDTH)
        u = _dot(h, w_ref[:, cs])
        acc = cst_ref[0, :, cs] * cw_ref[0:1, cs]
        acc = acc + cst_ref[1, :, cs] * cw_ref[1:2, cs]
        acc = acc + cst_ref[2, :, cs] * cw_ref[2:3, cs]
        acc = acc + u * cw_ref[3:4, cs]
        y = _silu(acc)
        if gi == 0:
            y = _l2norm_heads(y, A_DK ** -0.5)
        elif gi == 1:
            y = _l2norm_heads(y, 1.0)
        o_ref[...] = y
        ncs_ref[0, :, cs] = cst_ref[1, :, cs]
        ncs_ref[1, :, cs] = cst_ref[2, :, cs]
        ncs_ref[2, :, cs] = u

    za_ref[...] = _dot(h, w_ref[:, C_ZA:C_ZA + A_WIDTH])
    gb_ref[...] = _gate_lanes(_dot(h, w_ref[:, C_BD:C_BD + LANES]), alog_ref[...], dt_ref[...])

    cos = cos_ref[...]
    sin = sin_ref[...]
    uq = _dot(h, w_ref[:, C_QB:C_QB + B_WIDTH])
    for g in range(B_WIDTH // LANES):
        qb_ref[:, g * LANES:(g + 1) * LANES] = (
            _rotary_group(uq[:, g * LANES:(g + 1) * LANES], cos, sin) * (B_HD ** -0.5))
    kb_ref[...] = _rotary_group(_dot(h, w_ref[:, C_KB:C_KB + LANES]), cos, sin)
    vb_ref[...] = _dot(h, w_ref[:, C_VB:C_VB + LANES])
    zb_ref[...] = _dot(h, w_ref[:, C_ZB:C_ZB + B_WIDTH])


def _sproj(x, mod_s, w_r, conv_w, cst, alog_row, dt_row, cos_row, sin_row):
    n = x.shape[0]
    full = lambda s: pl.BlockSpec(s, lambda i: (0,) * len(s))
    wide = lambda w: jax.ShapeDtypeStruct((n, w), F32)
    return pl.pallas_call(
        _sproj_kernel,
        grid=(1,),
        in_specs=[full((n, D_MODEL)), pl.BlockSpec((n, 3 * D_MODEL), lambda i: (0, 0)),
                  full((D_MODEL, W_COLS)),
                  full((CONV_W, A_QKV)), full((CONV_W - 1, n, A_QKV)),
                  full((1, LANES)), full((1, LANES)), full((1, LANES)), full((1, LANES))],
        out_specs=[full((n, A_WIDTH)), full((n, A_WIDTH)), full((n, A_WIDTH)), full((n, A_WIDTH)),
                   full((n, LANES)), full((n, B_WIDTH)), full((n, LANES)), full((n, LANES)),
                   full((n, B_WIDTH)), full((CONV_W - 1, n, A_QKV))],
        out_shape=[wide(A_WIDTH), wide(A_WIDTH), wide(A_WIDTH), wide(A_WIDTH), wide(LANES),
                   wide(B_WIDTH), wide(LANES), wide(LANES), wide(B_WIDTH),
                   jax.ShapeDtypeStruct((CONV_W - 1, n, A_QKV), F32)],
        compiler_params=pltpu.CompilerParams(dimension_semantics=("arbitrary",),
                                             vmem_limit_bytes=VMEM_LIMIT),
        name="sproj",
    )(x, mod_s, w_r, conv_w, cst, alog_row, dt_row, cos_row, sin_row)


def _sstep_kernel(sink_ref, q_ref, k_ref, v_ref, gb_ref, za_ref, na_ref, st_ref,
                  qb_ref, kn_ref, vn_ref, zb_ref, ck_ref, cv_ref,
                  oa_ref, ob_ref, nst_ref, nck_ref, ncv_ref,
                  o_scr, ob_scr):
    bt = q_ref.shape[0]
    gbv = gb_ref[...]

    pick = (lax.broadcasted_iota(jnp.int32, (bt, bt * A_DV), 1) // A_DV
            == lax.broadcasted_iota(jnp.int32, (bt, bt * A_DV), 0))
    pick = jnp.where(pick, 1.0, 0.0).astype(BF16)
    for h in range(A_HEADS):
        hs = slice(h * A_DK, (h + 1) * A_DK)
        q_rep = _dot(q_ref[:, hs].T.astype(BF16), pick)
        k_rep = _dot(k_ref[:, hs].T.astype(BF16), pick)
        for bb in range(bt):
            eg = jnp.exp(gbv[bb:bb + 1, A_HEADS + h:A_HEADS + h + 1])
            beta = gbv[bb:bb + 1, h:h + 1]
            kcol = k_rep[:, bb * A_DV:(bb + 1) * A_DV]
            qcol = q_rep[:, bb * A_DV:(bb + 1) * A_DV]
            s1 = eg * st_ref[bb, h]
            pred = jnp.sum(kcol * s1, axis=0, keepdims=True)
            upd = beta * (v_ref[bb:bb + 1, hs] - pred)
            s2 = s1 + kcol * upd
            nst_ref[bb, h] = s2
            o_scr[bb:bb + 1, hs] = jnp.sum(qcol * s2, axis=0, keepdims=True)
    na = na_ref[...]
    for h in range(A_HEADS):
        hs = slice(h * A_DK, (h + 1) * A_DK)
        o = o_scr[:, hs]
        on = o * lax.rsqrt(jnp.mean(o * o, axis=-1, keepdims=True) + RMS_EPS) * na
        oa_ref[:, hs] = (on * _silu(za_ref[:, hs])).astype(BF16)

    row8 = lax.broadcasted_iota(jnp.int32, (B_HEADS, LANES), 0)
    lane8 = _lane((B_HEADS, LANES))
    own_half = (lane8 >= B_HD) == (row8 >= B_GROUP)
    rcol = lax.broadcasted_iota(jnp.int32, (B_HEADS, 1), 0)
    sink = jnp.zeros((B_HEADS, 1), F32)
    for r in range(B_HEADS):
        sink = jnp.where(rcol == r, sink_ref[r], sink)
    qv = qb_ref[...]
    qv_r = jnp.concatenate([pltpu.roll(qv[:, g * LANES:(g + 1) * LANES], B_HD, axis=1)
                            for g in range(B_WIDTH // LANES)], axis=-1)
    kn_t = kn_ref[...].T
    vn_t = vn_ref[...].T
    newest = _lane((LANES, WINDOW)) == WINDOW - 1
    qzs, scs = [], []
    for bb in range(bt):
        qz = jnp.zeros((B_HEADS, LANES), F32)
        for r in range(B_HEADS):
            grp, half, kh = r // 2, r % 2, r // B_GROUP
            src = qv if half == kh else qv_r
            qz = jnp.where(row8 == r, src[bb:bb + 1, grp * LANES:(grp + 1) * LANES], qz)
        qzs.append(jnp.where(own_half, qz, 0.0))
    for bb in range(bt):
        scs.append(_dot(qzs[bb], ck_ref[bb]))
    ps, pnews, dens = [], [], []
    for bb in range(bt):
        sc_new = jnp.sum(qzs[bb] * kn_ref[bb:bb + 1, :], axis=-1, keepdims=True)
        m = jnp.maximum(jnp.maximum(jnp.max(scs[bb], axis=-1, keepdims=True), sc_new), sink)
        p = jnp.exp(scs[bb] - m)
        p_new = jnp.exp(sc_new - m)
        ps.append(p)
        pnews.append(p_new)
        dens.append(jnp.sum(p, axis=-1, keepdims=True) + p_new + jnp.exp(sink - m))
    pvs = [_dot_nt(ps[bb], cv_ref[bb]) for bb in range(bt)]
    for bb in range(bt):
        o = (pvs[bb] + pnews[bb] * vn_ref[bb:bb + 1, :]) / dens[bb]
        o = jnp.where(own_half, o, 0.0)
        ob_scr[bb * B_HEADS:(bb + 1) * B_HEADS, :] = o + pltpu.roll(o, B_HD, axis=1)
    for bb in range(bt):
        nck_ref[bb] = jnp.where(newest, kn_t[:, bb:bb + 1], pltpu.roll(ck_ref[bb], WINDOW - 1, axis=1))
        ncv_ref[bb] = jnp.where(newest, vn_t[:, bb:bb + 1], pltpu.roll(cv_ref[bb], WINDOW - 1, axis=1))
    low = _lane((bt, LANES)) < B_HD
    for grp in range(B_WIDTH // LANES):
        even = ob_scr[pl.ds(2 * grp, bt, stride=B_HEADS), :]
        odd = ob_scr[pl.ds(2 * grp + 1, bt, stride=B_HEADS), :]
        gs = slice(grp * LANES, (grp + 1) * LANES)
        ob_ref[:, gs] = (jnp.where(low, even, odd) * _silu(zb_ref[:, gs])).astype(BF16)


def _sstep(sinks, q, k, v, gb, za, na_row, state, qb, kn, vn, zb, ck, cv):
    n = q.shape[0]
    bt = STEP_BT
    row = lambda w: pl.BlockSpec((bt, w), lambda i: (i, 0))
    st_spec = pl.BlockSpec((bt, A_HEADS, A_DK, A_DV), lambda i: (i, 0, 0, 0))
    c_spec = pl.BlockSpec((bt, WINDOW, LANES), lambda i: (i, 0, 0))
    return pl.pallas_call(
        _sstep_kernel,
        grid=(n // bt,),
        in_specs=[pl.BlockSpec(memory_space=pltpu.SMEM),
                  row(A_WIDTH), row(A_WIDTH), row(A_WIDTH), row(LANES), row(A_WIDTH),
                  pl.BlockSpec((1, A_DV), lambda i: (0, 0)), st_spec,
                  row(B_WIDTH), row(LANES), row(LANES), row(B_WIDTH), c_spec, c_spec],
        out_specs=[row(A_WIDTH), row(B_WIDTH), st_spec, c_spec, c_spec],
        out_shape=[jax.ShapeDtypeStruct((n, A_WIDTH), BF16),
                   jax.ShapeDtypeStruct((n, B_WIDTH), BF16),
                   jax.ShapeDtypeStruct((n, A_HEADS, A_DK, A_DV), F32),
                   jax.ShapeDtypeStruct((n, WINDOW, LANES), F32),
                   jax.ShapeDtypeStruct((n, WINDOW, LANES), F32)],
        scratch_shapes=[pltpu.VMEM((bt, A_WIDTH), F32), pltpu.VMEM((bt * B_HEADS, LANES), F32)],
        compiler_params=pltpu.CompilerParams(dimension_semantics=("arbitrary",),
                                             vmem_limit_bytes=VMEM_LIMIT),
        name="sstep",
    )(sinks, q, k, v, gb, za, na_row, state, qb, kn, vn, zb, ck, cv)


def _rope_tables(pos):
    half = B_HD // 2
    inv = 1.0 / (ROPE_THETA ** (np.arange(half, dtype=np.float64) / half))
    ang = np.asarray(pos, np.float64)[:, None] * inv[None, :]
    cos, sin = np.cos(ang), np.sin(ang)
    reps = LANES // B_HD
    return (jnp.asarray(np.tile(np.concatenate([cos, cos], -1), (1, reps)), F32),
            jnp.asarray(np.tile(np.concatenate([-sin, sin], -1), (1, reps)), F32))


def _pad_row(vec, offset):
    return jnp.zeros((1, LANES), F32).at[0, offset:offset + vec.shape[0]].set(vec.astype(F32))


def _layer(x_prompt, x_sample, state_conv, state_delta, cache_k, cache_v, c_prompt, c_sample,
           w_ada, b_ada, w_in, conv_w, a_log, dt_bias, norm_a, sinks, w_out, ln_g, ln_b):
    bsz, seq, _ = x_prompt.shape
    n_s = x_sample.shape[0]

    w_r = _wprep(jnp.swapaxes(w_in, 0, 1))
    w_o = w_out.astype(BF16)
    alog_row = _pad_row(a_log, A_HEADS)
    dt_row = _pad_row(dt_bias, A_HEADS)
    na_row = norm_a.reshape(1, A_DV)
    g_row = ln_g.reshape(1, D_MODEL)
    b_row = ln_b.reshape(1, D_MODEL)

    assert n_s % 8 == 0 and bsz <= 8
    c_all = jnp.concatenate([c_sample, c_prompt, jnp.zeros((8 - bsz, D_MODEL), F32)], axis=0)
    mod = _ada(c_all, w_ada, b_ada.reshape(1, 3 * D_MODEL))

    cos_p, sin_p = _rope_tables(np.arange(seq))
    (q, k, v, za, gb, qb, kb, kbr, vd0, vd1, zb, conv_p, kb_last, vb_last) = _proj(
        x_prompt, mod, n_s, w_r, conv_w, alog_row, dt_row, cos_p, sin_p)
    oa, delta_p = _delta(q, k, v, gb, za, na_row)
    ob = _swa(sinks, qb, kb, kbr, vd0, vd1, zb)
    y_p = _out(oa, ob, x_prompt, mod, n_s, True, w_o, g_row, b_row, OUT_TM)
    swa_k_p = kb_last.reshape(bsz, WINDOW, B_KV_HEADS, B_HD)
    swa_v_p = vb_last.reshape(bsz, WINDOW, B_KV_HEADS, B_HD)

    cos_s, sin_s = _rope_tables(np.array([PAST_LEN]))
    xs = x_sample.reshape(n_s, D_MODEL)
    cst = jnp.transpose(state_conv, (1, 0, 2))
    sq, sk, sv, sza, sgb, sqb, skn, svn, szb, ncs = _sproj(xs, mod, w_r, conv_w, cst, alog_row, dt_row,
                                                           cos_s, sin_s)
    soa, sob, delta_s, nck, ncv = _sstep(sinks, sq, sk, sv, sgb, sza, na_row, state_delta,
                                         sqb, skn, svn, szb,
                                         jnp.swapaxes(cache_k.reshape(n_s, WINDOW, LANES), 1, 2),
                                         jnp.swapaxes(cache_v.reshape(n_s, WINDOW, LANES), 1, 2))
    y_s = _out(soa[None], sob[None], xs[None], mod, 0, False, w_o, g_row, b_row, n_s)
    conv_s = jnp.transpose(ncs, (1, 0, 2))
    unpack = lambda c: jnp.swapaxes(c, 1, 2).reshape(n_s, WINDOW, B_KV_HEADS, B_HD)
    return (y_p, y_s.reshape(n_s, 1, D_MODEL), conv_p, delta_p, swa_k_p, swa_v_p,
            conv_s, delta_s, unpack(nck), unpack(ncv))


def kernel(x_prompt, x_sample, state_conv, state_delta, cache_swa_k, cache_swa_v, c_prompt, c_sample,
           w_ada, b_ada, w_in, conv_w, a_log, dt_bias, norm_a, sinks, w_out, ln_g, ln_b):
    assert w_ada.shape[0] == DEPTH == 1
    outs = _layer(x_prompt, x_sample, state_conv[0], state_delta[0], cache_swa_k[0], cache_swa_v[0],
                  c_prompt, c_sample, w_ada[0], b_ada[0], w_in[0], conv_w[0], a_log[0], dt_bias[0],
                  norm_a[0], sinks[0], w_out[0], ln_g[0], ln_b[0])
    y_p, y_s = outs[0], outs[1]
    return (y_p, y_s) + tuple(o[None] for o in outs[2:])
```

```python
import functools

import jax
import jax.numpy as jnp
import numpy as np
from jax import lax
from jax.experimental import pallas as pl
from jax.experimental.pallas import tpu as pltpu

F32 = jnp.float32
BF16 = jnp.bfloat16

D_MODEL = 1024
DEPTH = 1
PAST_LEN = 8192
A_HEADS = 4
A_DK = 128
A_DV = 128
A_WIDTH = A_HEADS * A_DV
A_QKV = 3 * A_WIDTH
CONV_W = 4
CHUNK = 64
B_HEADS = 8
B_KV_HEADS = 2
B_HD = 64
B_GROUP = B_HEADS // B_KV_HEADS
B_WIDTH = B_HEADS * B_HD
B_KV_WIDTH = B_KV_HEADS * B_HD
WINDOW = 128
ROPE_THETA = 10000.0
MIX_WIDTH = A_WIDTH + B_WIDTH
DEEPNORM_ALPHA = (2 * DEPTH) ** 0.25
LOG2E = 1.4426950408889634
LN_EPS = 1e-5
RMS_EPS = 1e-6
L2_EPS = 1e-6

OFF_A_Z = A_QKV
OFF_A_BETA = OFF_A_Z + A_WIDTH
OFF_A_DECAY = OFF_A_BETA + A_HEADS
OFF_B_Q = OFF_A_DECAY + A_HEADS
OFF_B_K = OFF_B_Q + B_WIDTH
OFF_B_V = OFF_B_K + B_KV_WIDTH
OFF_B_Z = OFF_B_V + B_KV_WIDTH
PROJ_COLS = OFF_B_Z + B_WIDTH

LANES = 128
C_QKV = 0
C_ZA = C_QKV + A_QKV
C_QB = C_ZA + A_WIDTH
C_KB = C_QB + B_WIDTH
C_VB = C_KB + B_KV_WIDTH
C_ZB = C_VB + B_KV_WIDTH
C_BD = C_ZB + B_WIDTH
WPREP_TN = 256
W_COLS = C_BD + WPREP_TN

VMEM_LIMIT = 56 * 1024 * 1024

PROJ_TM = 512
PROJ_CW = 256
PROJ_PARTS = 4
DELTA_CT = 256
DELTA_WAVE = 2
SWA_TQ = 512
SWA_WAVE = 1
OUT_TM = 1024
OUT_PARTS = 4
STEP_BT = 16


def _dot(a, b):
    return jnp.dot(a, b, preferred_element_type=F32)


def _dot_nt(a, b):
    return lax.dot_general(a, b, (((1,), (1,)), ((), ())), preferred_element_type=F32)


def _silu(x):
    return x * jax.nn.sigmoid(x)


def _softplus(x):
    return jnp.maximum(x, 0.0) + jnp.log1p(jnp.exp(-jnp.abs(x)))


def _lane(shape):
    return lax.broadcasted_iota(jnp.int32, shape, len(shape) - 1)


def _l2norm_heads(y, scale):
    outs = []
    for h in range(y.shape[1] // A_DK):
        xh = y[:, h * A_DK:(h + 1) * A_DK]
        ss = jnp.sum(xh * xh, axis=-1, keepdims=True)
        xn = xh * lax.rsqrt(ss + L2_EPS)
        outs.append(xn * scale if scale != 1.0 else xn)
    return jnp.concatenate(outs, axis=-1)


def _rotary_group(xg, cos, sin_signed):
    lane = _lane(xg.shape)
    swapped = jnp.where((lane % B_HD) < (B_HD // 2),
                        pltpu.roll(xg, LANES - B_HD // 2, axis=1),
                        pltpu.roll(xg, B_HD // 2, axis=1))
    return xg * cos + swapped * sin_signed


def _kv_layouts(kb, vb):
    low = _lane(kb.shape) < B_HD
    kbr = pltpu.roll(kb, B_HD, axis=1)
    vbr = pltpu.roll(vb, B_HD, axis=1)
    return kb, kbr, jnp.where(low, vb, vbr), jnp.where(low, vbr, vb)


def _gate_lanes(bd, alog_row, dt_row):
    lane = _lane(bd.shape)
    g = -jnp.exp(alog_row) * _softplus(bd + dt_row)
    return jnp.where(lane < A_HEADS, jax.nn.sigmoid(bd), g)


def _layer_norm(r, g, b):
    mu = jnp.mean(r, axis=-1, keepdims=True)
    d = r - mu
    var = jnp.mean(d * d, axis=-1, keepdims=True)
    return d * lax.rsqrt(var + LN_EPS) * g + b


def _wprep_kernel(wt_ref, o_ref):
    x = wt_ref[...]
    tail = pl.program_id(0) == pl.num_programs(0) - 1
    row = lax.broadcasted_iota(jnp.int32, x.shape, 0)
    x = jnp.where(jnp.logical_and(tail, row >= 2 * A_HEADS), 0.0, x)
    o_ref[...] = x.T.astype(BF16)


def _wprep(w_t):
    tn = WPREP_TN
    n_a, n_b = OFF_A_BETA // tn, (PROJ_COLS - OFF_B_Q) // tn
    assert n_a * tn == OFF_A_BETA and n_b * tn == PROJ_COLS - OFF_B_Q and OFF_A_BETA + tn <= PROJ_COLS

    def src_row(j):
        return jnp.where(j < n_a, j * tn, jnp.where(j < n_a + n_b, OFF_B_Q + (j - n_a) * tn, OFF_A_BETA))

    return pl.pallas_call(
        _wprep_kernel,
        grid=(n_a + n_b + 1,),
        in_specs=[pl.BlockSpec((pl.Element(tn), pl.Element(D_MODEL)),
                               lambda j: (pl.multiple_of(src_row(j), 8), 0))],
        out_specs=pl.BlockSpec((D_MODEL, tn), lambda j: (0, j)),
        out_shape=jax.ShapeDtypeStruct((D_MODEL, W_COLS), BF16),
        compiler_params=pltpu.CompilerParams(dimension_semantics=("arbitrary",),
                                             vmem_limit_bytes=VMEM_LIMIT),
        name="wprep",
    )(w_t)


def _ada_kernel(c_ref, w_ref, b_ref, o_ref):
    o_ref[...] = _dot(c_ref[...].astype(BF16), w_ref[...].astype(BF16)) + b_ref[...]


def _ada(c_all, w_ada, b_ada):
    rows = c_all.shape[0]
    tn = 768
    return pl.pallas_call(
        _ada_kernel,
        grid=(3 * D_MODEL // tn,),
        in_specs=[pl.BlockSpec((rows, D_MODEL), lambda j: (0, 0)),
                  pl.BlockSpec((D_MODEL, tn), lambda j: (0, j)),
                  pl.BlockSpec((1, tn), lambda j: (0, j))],
        out_specs=pl.BlockSpec((rows, tn), lambda j: (0, j)),
        out_shape=jax.ShapeDtypeStruct((rows, 3 * D_MODEL), F32),
        compiler_params=pltpu.CompilerParams(dimension_semantics=("arbitrary",),
                                             vmem_limit_bytes=VMEM_LIMIT),
        name="ada",
    )(c_all, w_ada, b_ada)


def _proj_kernel(x_ref, mod_ref, w_ref, cw_ref, alog_ref, dt_ref, cos_ref, sin_ref,
                 q_ref, k_ref, v_ref, za_ref, gb_ref, qb_ref, kb_ref, kbr_ref, vd0_ref, vd1_ref,
                 zb_ref, cst_ref, kbl_ref, vbl_ref, ubuf):
    tm = x_ref.shape[1]
    t = pl.program_id(1)

    @pl.when(t == 0)
    def _():
        ubuf[...] = jnp.zeros(ubuf.shape, F32)

    brow = pl.ds(pl.program_id(0), 1)
    shift = mod_ref[brow, 0:D_MODEL]
    scale = mod_ref[brow, D_MODEL:2 * D_MODEL]

    rp = tm // PROJ_PARTS
    cw = PROJ_CW
    sub = lax.broadcasted_iota(jnp.int32, (rp // 8, 8, cw), 1)
    pieces = [slice(c0, c0 + cw) for c0 in range(0, A_QKV, cw)]
    outs = (q_ref, k_ref, v_ref)

    def part(r0):
        rs = slice(r0, r0 + rp)
        h = (x_ref[0, rs, :] * (1.0 + scale) + shift).astype(BF16)

        def conv_epilogue(cs, u):
            gi, c_in = cs.start // A_WIDTH, cs.start % A_WIDTH
            groups = jnp.concatenate([ubuf[:, cs], u], axis=0).reshape(rp // 8 + 1, 8, cw)
            acc = None
            for j in range(CONV_W - 1, 0, -1):
                rot = pltpu.roll(groups, j, axis=1)
                term = (jnp.where(sub < j, rot[:-1], rot[1:]).reshape(rp, cw)
                        * cw_ref[CONV_W - 1 - j:CONV_W - j, cs])
                acc = term if acc is None else acc + term
            y = _silu(acc + u * cw_ref[CONV_W - 1:CONV_W, cs])
            if gi == 0:
                y = _l2norm_heads(y, A_DK ** -0.5)
            elif gi == 1:
                y = _l2norm_heads(y, 1.0)
            outs[gi][0, rs, c_in:c_in + cw] = y
            ubuf[:, cs] = u[rp - 8:rp]
            if r0 + rp == tm:
                cst_ref[0, :, cs] = u[rp - (CONV_W - 1):rp]

        pending = _dot(h, w_ref[:, pieces[0]])
        for i, cs in enumerate(pieces):
            u = pending
            if i + 1 < len(pieces):
                pending = _dot(h, w_ref[:, pieces[i + 1]])
            conv_epilogue(cs, u)

        za_ref[0, rs, :] = _dot(h, w_ref[:, C_ZA:C_ZA + A_WIDTH])
        gb_ref[0, rs, :] = _gate_lanes(_dot(h, w_ref[:, C_BD:C_BD + LANES]), alog_ref[...], dt_ref[...])
        cos = cos_ref[rs, :]
        sin = sin_ref[rs, :]
        uq = _dot(h, w_ref[:, C_QB:C_QB + B_WIDTH])
        for g in range(B_WIDTH // LANES):
            qb_ref[0, rs, g * LANES:(g + 1) * LANES] = (
                _rotary_group(uq[:, g * LANES:(g + 1) * LANES], cos, sin) * (B_HD ** -0.5 * LOG2E)).astype(BF16)
        ukv = _dot(h, w_ref[:, C_KB:C_KB + 2 * LANES])
        kb = _rotary_group(ukv[:, 0:LANES], cos, sin)
        vb = ukv[:, LANES:2 * LANES]
        for o_ref, val in zip((kb_ref, kbr_ref, vd0_ref, vd1_ref), _kv_layouts(kb, vb)):
            o_ref[0, rs, :] = val.astype(BF16)
        zb_ref[0, rs, :] = _dot(h, w_ref[:, C_ZB:C_ZB + B_WIDTH])
        return kb, vb

    for r0 in range(0, tm, rp):
        kb, vb = part(r0)

    @pl.when(t == pl.num_programs(1) - 1)
    def _():
        kbl_ref[0] = kb[rp - WINDOW:rp]
        vbl_ref[0] = vb[rp - WINDOW:rp]


def _proj(x, mod, mod_row0, w_r, conv_w, alog_row, dt_row, cos_t, sin_t):
    bsz, t, _ = x.shape
    tm = PROJ_TM
    row = lambda w: pl.BlockSpec((1, tm, w), lambda b, i: (b, i, 0))
    const2 = lambda s: pl.BlockSpec(s, lambda b, i: (0, 0))
    per_b = lambda r, w: pl.BlockSpec((1, r, w), lambda b, i: (b, 0, 0))
    wide = lambda w, dt=F32: jax.ShapeDtypeStruct((bsz, t, w), dt)
    return pl.pallas_call(
        _proj_kernel,
        grid=(bsz, t // tm),
        in_specs=[row(D_MODEL),
                  pl.BlockSpec((8, 3 * D_MODEL), lambda b, i: (mod_row0 // 8, 0)),
                  const2((D_MODEL, W_COLS)),
                  const2((CONV_W, A_QKV)),
                  const2((1, LANES)), const2((1, LANES)),
                  pl.BlockSpec((tm, LANES), lambda b, i: (i, 0)),
                  pl.BlockSpec((tm, LANES), lambda b, i: (i, 0))],
        out_specs=[row(A_WIDTH), row(A_WIDTH), row(A_WIDTH), row(A_WIDTH), row(LANES),
                   row(B_WIDTH), row(LANES), row(LANES), row(LANES), row(LANES), row(B_WIDTH),
                   per_b(CONV_W - 1, A_QKV), per_b(WINDOW, LANES), per_b(WINDOW, LANES)],
        out_shape=[wide(A_WIDTH), wide(A_WIDTH), wide(A_WIDTH), wide(A_WIDTH), wide(LANES),
                   wide(B_WIDTH, BF16), wide(LANES, BF16), wide(LANES, BF16), wide(LANES, BF16),
                   wide(LANES, BF16), wide(B_WIDTH),
                   jax.ShapeDtypeStruct((bsz, CONV_W - 1, A_QKV), F32),
                   jax.ShapeDtypeStruct((bsz, WINDOW, LANES), F32),
                   jax.ShapeDtypeStruct((bsz, WINDOW, LANES), F32)],
        scratch_shapes=[pltpu.VMEM((8, A_QKV), F32)],
        compiler_params=pltpu.CompilerParams(dimension_semantics=("arbitrary", "arbitrary"),
                                             vmem_limit_bytes=VMEM_LIMIT),
        name="proj",
    )(x, mod, w_r, conv_w, alog_row, dt_row, cos_t, sin_t)


def _delta_kernel(q_ref, k_ref, v_ref, gb_ref, za_ref, na_ref, oa_ref, st_ref,
                  s_scr, wq_s, ut_s, akd_s, gl_s):
    bsz, ct = q_ref.shape[0], q_ref.shape[1]
    nch = ct // CHUNK
    t = pl.program_id(0)
    wslot = t % 2
    rslot = 1 - wslot

    @pl.when(t == 0)
    def _():
        s_scr[...] = jnp.zeros(s_scr.shape, F32)
        wq_s[...] = jnp.zeros(wq_s.shape, BF16)
        ut_s[...] = jnp.zeros(ut_s.shape, F32)
        akd_s[...] = jnp.zeros(akd_s.shape, BF16)
        gl_s[...] = jnp.zeros(gl_s.shape, F32)

    units = [(b, c, h) for b in range(bsz) for c in range(nch) for h in range(A_HEADS)]
    uid = {u_: i for i, u_ in enumerate(units)}
    rows = lambda c: slice(c * CHUNK, (c + 1) * CHUNK)
    lanes = lambda h: slice(h * A_DK, (h + 1) * A_DK)
    na = na_ref[...]

    s_cur = {(b, h): s_scr[b * A_HEADS + h] for b in range(bsz) for h in range(A_HEADS)}
    ws, uu = {}, {}

    def rec_ws(c):
        for b in range(bsz):
            for h in range(A_HEADS):
                i = uid[b, c, h]
                ws[b, h] = _dot(wq_s[rslot, i], s_cur[b, h].astype(BF16))
                uu[b, h] = (ut_s[rslot, i] - ws[b, h][:CHUNK]).astype(BF16)

    def rec_ou(c):
        zpad = jnp.zeros((CHUNK, A_DV), BF16)
        for b in range(bsz):
            u_bd = jnp.concatenate(
                [jnp.concatenate([uu[b, h] if hh == h else zpad for hh in range(A_HEADS)], axis=-1)
                 for h in range(A_HEADS)], axis=0)
            ou = _dot(akd_s[rslot, b * nch + c], u_bd)
            for h in range(A_HEADS):
                o = ws[b, h][CHUNK:] + ou[:CHUNK, lanes(h)]
                s_cur[b, h] = gl_s[rslot, uid[b, c, h]] * s_cur[b, h] + ou[CHUNK:, lanes(h)]
                on = o * lax.rsqrt(jnp.mean(o * o, axis=-1, keepdims=True) + RMS_EPS) * na
                oa_ref[b, rows(c), lanes(h)] = (on * _silu(za_ref[b, rows(c), lanes(h)])).astype(BF16)

    rec_stages = []
    for c in range(nch):
        rec_stages += [lambda c=c: rec_ws(c), lambda c=c: rec_ou(c)]

    def run_rec(n_left_after):
        while rec_stages and len(rec_stages) > n_left_after:
            rec_stages.pop(0)()

    pk = A_HEADS * CHUNK
    low = _lane((CHUNK, LANES)) < CHUNK
    low_row = _lane((1, LANES)) < CHUNK
    ti_p = lax.broadcasted_iota(jnp.int32, (CHUNK, pk), 0)
    ii_p = _lane((CHUNK, pk)) % CHUNK
    zero64 = jnp.zeros((CHUNK, LANES), BF16)

    def pack(parts):
        return jnp.concatenate([jnp.where(low, parts[0], parts[1]), jnp.where(low, parts[2], parts[3])], axis=-1)

    def block_diag(x16):
        blocks = []
        for h in range(A_HEADS):
            pair, first = h // 2, h % 2 == 0
            piece = jnp.where(low if first else jnp.logical_not(low), x16[:, pair * LANES:(pair + 1) * LANES], zero64)
            blocks.append(jnp.concatenate([piece, zero64] if pair == 0 else [zero64, piece], axis=-1))
        return jnp.concatenate(blocks, axis=0)

    zrhs = jnp.zeros((CHUNK, 2 * A_DK), BF16)
    n_rec = len(rec_stages)
    n_slots = 8 * (bsz // DELTA_WAVE)
    done = [0]

    def stage_done():
        done[0] += 1
        run_rec(n_rec - (done[0] * n_rec) // n_slots)

    def decay_terms(b, beta, g_col, g_last, eg, dec_p, beta_p):
        gbv = gb_ref[b]
        rin = lax.broadcasted_iota(jnp.int32, gbv.shape, 0) % CHUNK
        gcs = gbv
        s = 1
        while s < CHUNK:
            gcs = gcs + jnp.where(rin >= s, pltpu.roll(gcs, s, axis=0), 0.0)
            s *= 2
        gcs_t = gcs.T
        for c in range(nch):
            r0 = c * CHUNK
            pair_lanes = slice((c // 2) * LANES, (c // 2 + 1) * LANES)
            g_rows = []
            for h in range(A_HEADS):
                u_ = (b, c, h)
                beta[u_] = jnp.broadcast_to(gbv[rows(c), h:h + 1], (CHUNK, A_DK))
                g_col[u_] = jnp.broadcast_to(gcs[rows(c), A_HEADS + h:A_HEADS + h + 1], (CHUNK, A_DK))
                g_last[u_] = gcs[r0 + CHUNK - 1:r0 + CHUNK, A_HEADS + h:A_HEADS + h + 1]
                eg[u_] = jnp.exp(g_col[u_])
                g_row = gcs_t[A_HEADS + h:A_HEADS + h + 1, pair_lanes]
                g_rows.append(g_row if c % 2 == h % 2 else pltpu.roll(g_row, CHUNK, axis=1))
            g_row_p = jnp.concatenate([jnp.where(low_row, g_rows[0], g_rows[1]),
                                       jnp.where(low_row, g_rows[2], g_rows[3])], axis=-1)
            g_col_p = pack([g_col[b, c, h] for h in range(A_HEADS)])
            dec_p[b, c] = jnp.exp(jnp.where(ti_p >= ii_p, g_col_p - g_row_p, -jnp.inf))
            beta_p[b, c] = pack([beta[b, c, h] for h in range(A_HEADS)])

    def prepare(bs):
        groups_b = [(b, c) for b in bs for c in range(nch)]
        beta, g_col, g_last, eg, dec_p, beta_p = {}, {}, {}, {}, {}, {}
        for b in bs:
            decay_terms(b, beta, g_col, g_last, eg, dec_p, beta_p)

        nmat = {}
        for (b, c) in groups_b:
            k16 = k_ref[b, rows(c), :].astype(BF16)
            q16 = q_ref[b, rows(c), :].astype(BF16)
            k_heads = jnp.concatenate(
                [jnp.concatenate([k16[:, lanes(h)] if hh == h else zero64 for hh in range(A_HEADS)], axis=-1)
                 for h in range(A_HEADS)], axis=0)
            kq = _dot_nt(jnp.concatenate([k16, q16], axis=0), k_heads)
            nmat[b, c] = -(beta_p[b, c] * kq[:CHUNK] * jnp.where(ti_p > ii_p, dec_p[b, c], 0.0))
            akd_s[wslot, b * nch + c, 0:CHUNK, :] = (kq[CHUNK:] * dec_p[b, c]).astype(BF16)
        stage_done()

        rsum = dict(nmat)
        pw16 = {g_: nmat[g_].astype(BF16) for g_ in groups_b}
        pw = {g_: _dot(pw16[g_], block_diag(pw16[g_])) for g_ in groups_b}
        stage_done()
        for step in range(1, 6):
            last = step == 5
            pw16 = {g_: pw[g_].astype(BF16) for g_ in groups_b}
            rp = {}
            for g_ in groups_b:
                r16 = rsum[g_].astype(BF16)
                rp[g_] = _dot(r16 if last else jnp.concatenate([r16, pw16[g_]], axis=0), block_diag(pw16[g_]))
            for g_ in groups_b:
                rsum[g_] = rsum[g_] + pw[g_] + rp[g_][:CHUNK]
                if not last:
                    pw[g_] = rp[g_][CHUNK:]
            stage_done()

        for (b, c) in groups_b:
            for h in range(A_HEADS):
                u_ = (b, c, h)
                i = uid[u_]
                kc = k_ref[b, rows(c), lanes(h)]
                rhs = jnp.concatenate([(beta[u_] * eg[u_]) * kc, beta[u_] * v_ref[b, rows(c), lanes(h)]],
                                      axis=-1)
                rhs16 = rhs.astype(BF16)
                rhs_rows = jnp.concatenate([rhs16 if hh == h else zrhs for hh in range(A_HEADS)], axis=0)
                sol = rhs + _dot(rsum[b, c].astype(BF16), rhs_rows)
                wq_s[wslot, i] = jnp.concatenate([sol[:, :A_DK], eg[u_] * q_ref[b, rows(c), lanes(h)]],
                                                 axis=0).astype(BF16)
                ut_s[wslot, i] = sol[:, A_DK:]
                gl_s[wslot, i] = jnp.broadcast_to(jnp.exp(g_last[u_]), (1, A_DV))
        for (b, c) in groups_b:
            kd = [jnp.exp(g_last[b, c, h] - g_col[b, c, h]) * k_ref[b, rows(c), lanes(h)]
                  for h in range(A_HEADS)]
            for p in range(A_HEADS // 2):
                akd_s[wslot, b * nch + c, CHUNK:, p * LANES:(p + 1) * LANES] = (
                    jnp.concatenate([kd[2 * p], kd[2 * p + 1]], axis=0).T.astype(BF16))
        stage_done()

    for b0 in range(0, bsz, DELTA_WAVE):
        prepare(range(b0, b0 + DELTA_WAVE))
    run_rec(0)

    for b in range(bsz):
        for h in range(A_HEADS):
            s_scr[b * A_HEADS + h] = s_cur[b, h]

    @pl.when(t == pl.num_programs(0) - 1)
    def _():
        for b in range(bsz):
            for h in range(A_HEADS):
                st_ref[b, h] = s_cur[b, h]


def _delta(q, k, v, gb, za, na_row):
    bsz, t, _ = q.shape
    ct = DELTA_CT
    nt = t // ct
    n_units = bsz * (ct // CHUNK) * A_HEADS
    prep = lambda w: pl.BlockSpec((bsz, ct, w), lambda i: (0, jnp.minimum(i, nt - 1), 0))
    rec = lambda w: pl.BlockSpec((bsz, ct, w), lambda i: (0, jnp.maximum(i - 1, 0), 0))
    return pl.pallas_call(
        _delta_kernel,
        grid=(nt + 1,),
        in_specs=[prep(A_WIDTH), prep(A_WIDTH), prep(A_WIDTH), prep(LANES), rec(A_WIDTH),
                  pl.BlockSpec((1, A_DV), lambda i: (0, 0))],
        out_specs=[rec(A_WIDTH),
                   pl.BlockSpec((bsz, A_HEADS, A_DK, A_DV), lambda i: (0, 0, 0, 0))],
        out_shape=[jax.ShapeDtypeStruct((bsz, t, A_WIDTH), BF16),
                   jax.ShapeDtypeStruct((bsz, A_HEADS, A_DK, A_DV), F32)],
        scratch_shapes=[pltpu.VMEM((bsz * A_HEADS, A_DK, A_DV), F32),
                        pltpu.VMEM((2, n_units, 2 * CHUNK, A_DK), BF16),
                        pltpu.VMEM((2, n_units, CHUNK, A_DV), F32),
                        pltpu.VMEM((2, n_units // A_HEADS, CHUNK + A_DK, A_HEADS * CHUNK), BF16),
                        pltpu.VMEM((2, n_units, 1, A_DV), F32)],
        compiler_params=pltpu.CompilerParams(dimension_semantics=("arbitrary",),
                                             vmem_limit_bytes=VMEM_LIMIT),
        name="delta",
    )(q, k, v, gb, za, na_row)


def _swa_kernel(sink_ref, qb_ref, kc_ref, kp_ref, krc_ref, krp_ref, v0c_ref, v0p_ref, v1c_ref, v1p_ref,
                zb_ref, ob_ref):
    n = pl.program_id(1)
    tq = qb_ref.shape[1]
    blk = WINDOW
    kx = (jnp.concatenate([kp_ref[0], kc_ref[0]], axis=0), jnp.concatenate([krp_ref[0], krc_ref[0]], axis=0))
    vd = (jnp.concatenate([v0p_ref[0], v0c_ref[0]], axis=0), jnp.concatenate([v1p_ref[0], v1c_ref[0]], axis=0))

    a = lax.broadcasted_iota(jnp.int32, (2 * blk, 2 * blk), 0) % blk
    j = lax.broadcasted_iota(jnp.int32, (2 * blk, 2 * blk), 1)
    rel = a + blk - j
    band = (rel >= 0) & (rel <= WINDOW)
    band_first = band & ((n > 0) | (j >= blk))
    top = lax.broadcasted_iota(jnp.int32, (2 * blk, 1), 0) < blk
    low = _lane((blk, LANES)) < B_HD
    zero = jnp.zeros((blk, LANES), BF16)

    qrows = lambda i: slice(i * blk, (i + 1) * blk)
    krows = lambda i: slice(i * blk, (i + 2) * blk)
    sink = {(kh, half): jnp.where(top, sink_ref[kh * B_GROUP + half] * LOG2E,
                                  sink_ref[kh * B_GROUP + half + 2] * LOG2E)
            for kh in range(B_KV_HEADS) for half in range(2)}
    for i0 in range(0, tq // blk, SWA_WAVE):
        blocks = range(i0, i0 + SWA_WAVE)
        units = [(i, kh, half) for i in blocks for kh in range(B_KV_HEADS) for half in range(2)]
        sc = {}
        for (i, kh, half) in units:
            qs = []
            for g in range(2):
                grp = kh * 2 + g
                xg = qb_ref[0, qrows(i), grp * LANES:(grp + 1) * LANES]
                qs.append(jnp.where(low if half == 0 else jnp.logical_not(low), xg, zero))
            qz = jnp.concatenate(qs, axis=0)
            sc[i, kh, half] = _dot_nt(qz, kx[0 if kh == half else 1][krows(i)])
        p, den = {}, {}
        for u_ in units:
            i, kh, half = u_
            s_m = jnp.where(band_first if i == 0 else band, sc[u_], -jnp.inf)
            m = jnp.maximum(jnp.max(s_m, axis=-1, keepdims=True), sink[kh, half])
            e = jnp.exp2(s_m - m)
            den[u_] = jnp.sum(e, axis=-1, keepdims=True) + jnp.exp2(sink[kh, half] - m)
            p[u_] = e.astype(BF16)
        pv = {u_: _dot(p[u_], vd[u_[1]][krows(u_[0])]) for u_ in units}
        outs = {u_: pv[u_] / den[u_] for u_ in units}
        for i in blocks:
            for grp in range(B_WIDTH // LANES):
                kh, g = grp // 2, grp % 2
                og = jnp.where(low, outs[i, kh, 0][g * blk:(g + 1) * blk], outs[i, kh, 1][g * blk:(g + 1) * blk])
                gs = slice(grp * LANES, (grp + 1) * LANES)
                ob_ref[0, qrows(i), gs] = (og * _silu(zb_ref[0, qrows(i), gs])).astype(BF16)


def _swa(sinks, qb, kb, kbr, vd0, vd1, zb):
    bsz, t, _ = qb.shape
    tq = SWA_TQ
    per = tq // WINDOW
    cur = lambda w: pl.BlockSpec((1, tq, w), lambda b, i: (b, i, 0))
    prev = pl.BlockSpec((1, WINDOW, LANES), lambda b, i: (b, jnp.maximum(i * per - 1, 0), 0))
    return pl.pallas_call(
        _swa_kernel,
        grid=(bsz, t // tq),
        in_specs=[pl.BlockSpec(memory_space=pltpu.SMEM), cur(B_WIDTH),
                  cur(LANES), prev, cur(LANES), prev, cur(LANES), prev, cur(LANES), prev,
                  cur(B_WIDTH)],
        out_specs=cur(B_WIDTH),
        out_shape=jax.ShapeDtypeStruct((bsz, t, B_WIDTH), BF16),
        compiler_params=pltpu.CompilerParams(dimension_semantics=("arbitrary", "arbitrary"),
                                             vmem_limit_bytes=VMEM_LIMIT),
        name="swa",
    )(sinks, qb, kb, kb, kbr, kbr, vd0, vd0, vd1, vd1, zb)


def _out_kernel(oa_ref, ob_ref, x_ref, gate_ref, w_ref, g_ref, b_ref, y_ref, *, gate_per_batch):
    tm = x_ref.shape[1]
    parts = OUT_PARTS if tm % (8 * OUT_PARTS) == 0 and gate_per_batch else 1
    rows = [slice(p * (tm // parts), (p + 1) * (tm // parts)) for p in range(parts)]
    mix = [_dot(oa_ref[0, r_, :], w_ref[0:A_WIDTH, :]) + _dot(ob_ref[0, r_, :], w_ref[A_WIDTH:MIX_WIDTH, :])
           for r_ in rows]
    gate = gate_ref[pl.ds(pl.program_id(0), 1), :] if gate_per_batch else gate_ref[...]
    for r_, m_ in zip(rows, mix):
        r = DEEPNORM_ALPHA * x_ref[0, r_, :] + (1.0 + gate) * m_
        y_ref[0, r_, :] = _layer_norm(r, g_ref[...], b_ref[...])


def _out(oa, ob, x, mod, mod_row0, gate_per_batch, w_out, ln_g, ln_b, tm):
    bsz, t, _ = x.shape
    grows = 8 if gate_per_batch else tm
    row = lambda w: pl.BlockSpec((1, tm, w), lambda b, i: (b, i, 0))
    const2 = lambda s: pl.BlockSpec(s, lambda b, i: (0, 0))
    return pl.pallas_call(
        functools.partial(_out_kernel, gate_per_batch=gate_per_batch),
        grid=(bsz, t // tm),
        in_specs=[row(A_WIDTH), row(B_WIDTH), row(D_MODEL),
                  pl.BlockSpec((grows, D_MODEL), lambda b, i: (mod_row0 // grows, 2)),
                  const2((MIX_WIDTH, D_MODEL)), const2((1, D_MODEL)), const2((1, D_MODEL))],
        out_specs=row(D_MODEL),
        out_shape=jax.ShapeDtypeStruct((bsz, t, D_MODEL), F32),
        compiler_params=pltpu.CompilerParams(dimension_semantics=("arbitrary", "arbitrary"),
                                             vmem_limit_bytes=VMEM_LIMIT),
        name="out",
    )(oa, ob, x, mod, w_out, ln_g, ln_b)


def _sproj_kernel(x_ref, mod_ref, w_ref, cw_ref, cst_ref, alog_ref, dt_ref, cos_ref, sin_ref,
                  q_ref, k_ref, v_ref, za_ref, gb_ref, qb_ref, kb_ref, vb_ref, zb_ref, ncs_ref):
    shift = mod_ref[:, 0:D_MODEL]
    scale = mod_ref[:, D_MODEL:2 * D_MODEL]
    h = (x_ref[...] * (1.0 + scale) + shift).astype(BF16)

    for gi, o_ref in enumerate((q_ref, k_ref, v_ref)):
        c0 = gi * A_WIDTH
        cs = slice(c0, c0 + A_WIDTH)
        u = _dot(h, w_ref[:, cs])
        acc = cst_ref[0, :, cs] * cw_ref[0:1, cs]
        acc = acc + cst_ref[1, :, cs] * cw_ref[1:2, cs]
        acc = acc + cst_ref[2, :, cs] * cw_ref[2:3, cs]
        acc = acc + u * cw_ref[3:4, cs]
        y = _silu(acc)
        if gi == 0:
            y = _l2norm_heads(y, A_DK ** -0.5)
        elif gi == 1:
            y = _l2norm_heads(y, 1.0)
        o_ref[...] = y
        ncs_ref[0, :, cs] = cst_ref[1, :, cs]
        ncs_ref[1, :, cs] = cst_ref[2, :, cs]
        ncs_ref[2, :, cs] = u

    za_ref[...] = _dot(h, w_ref[:, C_ZA:C_ZA + A_WIDTH])
    gb_ref[...] = _gate_lanes(_dot(h, w_ref[:, C_BD:C_BD + LANES]), alog_ref[...], dt_ref[...])

    cos = cos_ref[...]
    sin = sin_ref[...]
    uq = _dot(h, w_ref[:, C_QB:C_QB + B_WIDTH])
    for g in range(B_WIDTH // LANES):
        qb_ref[:, g * LANES:(g + 1) * LANES] = (
            _rotary_group(uq[:, g * LANES:(g + 1) * LANES], cos, sin) * (B_HD ** -0.5))
    kb_ref[...] = _rotary_group(_dot(h, w_ref[:, C_KB:C_KB + LANES]), cos, sin)
    vb_ref[...] = _dot(h, w_ref[:, C_VB:C_VB + LANES])
    zb_ref[...] = _dot(h, w_ref[:, C_ZB:C_ZB + B_WIDTH])


def _sproj(x, mod_s, w_r, conv_w, cst, alog_row, dt_row, cos_row, sin_row):
    n = x.shape[0]
    full = lambda s: pl.BlockSpec(s, lambda i: (0,) * len(s))
    wide = lambda w: jax.ShapeDtypeStruct((n, w), F32)
    return pl.pallas_call(
        _sproj_kernel,
        grid=(1,),
        in_specs=[full((n, D_MODEL)), pl.BlockSpec((n, 3 * D_MODEL), lambda i: (0, 0)),
                  full((D_MODEL, W_COLS)),
                  full((CONV_W, A_QKV)), full((CONV_W - 1, n, A_QKV)),
                  full((1, LANES)), full((1, LANES)), full((1, LANES)), full((1, LANES))],
        out_specs=[full((n, A_WIDTH)), full((n, A_WIDTH)), full((n, A_WIDTH)), full((n, A_WIDTH)),
                   full((n, LANES)), full((n, B_WIDTH)), full((n, LANES)), full((n, LANES)),
                   full((n, B_WIDTH)), full((CONV_W - 1, n, A_QKV))],
        out_shape=[wide(A_WIDTH), wide(A_WIDTH), wide(A_WIDTH), wide(A_WIDTH), wide(LANES),
                   wide(B_WIDTH), wide(LANES), wide(LANES), wide(B_WIDTH),
                   jax.ShapeDtypeStruct((CONV_W - 1, n, A_QKV), F32)],
        compiler_params=pltpu.CompilerParams(dimension_semantics=("arbitrary",),
                                             vmem_limit_bytes=VMEM_LIMIT),
        name="sproj",
    )(x, mod_s, w_r, conv_w, cst, alog_row, dt_row, cos_row, sin_row)


def _sstep_kernel(sink_ref, q_ref, k_ref, v_ref, gb_ref, za_ref, na_ref, st_ref,
                  qb_ref, kn_ref, vn_ref, zb_ref, ck_ref, cv_ref,
                  oa_ref, ob_ref, nst_ref, nck_ref, ncv_ref,
                  o_scr, ob_scr):
    bt = q_ref.shape[0]
    gbv = gb_ref[...]

    pick = (lax.broadcasted_iota(jnp.int32, (bt, bt * A_DV), 1) // A_DV
            == lax.broadcasted_iota(jnp.int32, (bt, bt * A_DV), 0))
    pick = jnp.where(pick, 1.0, 0.0).astype(BF16)
    for h in range(A_HEADS):
        hs = slice(h * A_DK, (h + 1) * A_DK)
        q_rep = _dot(q_ref[:, hs].T.astype(BF16), pick)
        k_rep = _dot(k_ref[:, hs].T.astype(BF16), pick)
        for bb in range(bt):
            eg = jnp.exp(gbv[bb:bb + 1, A_HEADS + h:A_HEADS + h + 1])
            beta = gbv[bb:bb + 1, h:h + 1]
            kcol = k_rep[:, bb * A_DV:(bb + 1) * A_DV]
            qcol = q_rep[:, bb * A_DV:(bb + 1) * A_DV]
            s1 = eg * st_ref[bb, h]
            pred = jnp.sum(kcol * s1, axis=0, keepdims=True)
            upd = beta * (v_ref[bb:bb + 1, hs] - pred)
            s2 = s1 + kcol * upd
            nst_ref[bb, h] = s2
            o_scr[bb:bb + 1, hs] = jnp.sum(qcol * s2, axis=0, keepdims=True)
    na = na_ref[...]
    for h in range(A_HEADS):
        hs = slice(h * A_DK, (h + 1) * A_DK)
        o = o_scr[:, hs]
        on = o * lax.rsqrt(jnp.mean(o * o, axis=-1, keepdims=True) + RMS_EPS) * na
        oa_ref[:, hs] = (on * _silu(za_ref[:, hs])).astype(BF16)

    row8 = lax.broadcasted_iota(jnp.int32, (B_HEADS, LANES), 0)
    lane8 = _lane((B_HEADS, LANES))
    own_half = (lane8 >= B_HD) == (row8 >= B_GROUP)
    rcol = lax.broadcasted_iota(jnp.int32, (B_HEADS, 1), 0)
    sink = jnp.zeros((B_HEADS, 1), F32)
    for r in range(B_HEADS):
        sink = jnp.where(rcol == r, sink_ref[r], sink)
    qv = qb_ref[...]
    qv_r = jnp.concatenate([pltpu.roll(qv[:, g * LANES:(g + 1) * LANES], B_HD, axis=1)
                            for g in range(B_WIDTH // LANES)], axis=-1)
    kn_t = kn_ref[...].T
    vn_t = vn_ref[...].T
    newest = _lane((LANES, WINDOW)) == WINDOW - 1
    qzs, scs = [], []
    for bb in range(bt):
        qz = jnp.zeros((B_HEADS, LANES), F32)
        for r in range(B_HEADS):
            grp, half, kh = r // 2, r % 2, r // B_GROUP
            src = qv if half == kh else qv_r
            qz = jnp.where(row8 == r, src[bb:bb + 1, grp * LANES:(grp + 1) * LANES], qz)
        qzs.append(jnp.where(own_half, qz, 0.0))
    for bb in range(bt):
        scs.append(_dot(qzs[bb], ck_ref[bb]))
    ps, pnews, dens = [], [], []
    for bb in range(bt):
        sc_new = jnp.sum(qzs[bb] * kn_ref[bb:bb + 1, :], axis=-1, keepdims=True)
        m = jnp.maximum(jnp.maximum(jnp.max(scs[bb], axis=-1, keepdims=True), sc_new), sink)
        p = jnp.exp(scs[bb] - m)
        p_new = jnp.exp(sc_new - m)
        ps.append(p)
        pnews.append(p_new)
        dens.append(jnp.sum(p, axis=-1, keepdims=True) + p_new + jnp.exp(sink - m))
    pvs = [_dot_nt(ps[bb], cv_ref[bb]) for bb in range(bt)]
    for bb in range(bt):
        o = (pvs[bb] + pnews[bb] * vn_ref[bb:bb + 1, :]) / dens[bb]
        o = jnp.where(own_half, o, 0.0)
        ob_scr[bb * B_HEADS:(bb + 1) * B_HEADS, :] = o + pltpu.roll(o, B_HD, axis=1)
    for bb in range(bt):
        nck_ref[bb] = jnp.where(newest, kn_t[:, bb:bb + 1], pltpu.roll(ck_ref[bb], WINDOW - 1, axis=1))
        ncv_ref[bb] = jnp.where(newest, vn_t[:, bb:bb + 1], pltpu.roll(cv_ref[bb], WINDOW - 1, axis=1))
    low = _lane((bt, LANES)) < B_HD
    for grp in range(B_WIDTH // LANES):
        even = ob_scr[pl.ds(2 * grp, bt, stride=B_HEADS), :]
        odd = ob_scr[pl.ds(2 * grp + 1, bt, stride=B_HEADS), :]
        gs = slice(grp * LANES, (grp + 1) * LANES)
        ob_ref[:, gs] = (jnp.where(low, even, odd) * _silu(zb_ref[:, gs])).astype(BF16)


def _sstep(sinks, q, k, v, gb, za, na_row, state, qb, kn, vn, zb, ck, cv):
    n = q.shape[0]
    bt = STEP_BT
    row = lambda w: pl.BlockSpec((bt, w), lambda i: (i, 0))
    st_spec = pl.BlockSpec((bt, A_HEADS, A_DK, A_DV), lambda i: (i, 0, 0, 0))
    c_spec = pl.BlockSpec((bt, WINDOW, LANES), lambda i: (i, 0, 0))
    return pl.pallas_call(
        _sstep_kernel,
        grid=(n // bt,),
        in_specs=[pl.BlockSpec(memory_space=pltpu.SMEM),
                  row(A_WIDTH), row(A_WIDTH), row(A_WIDTH), row(LANES), row(A_WIDTH),
                  pl.BlockSpec((1, A_DV), lambda i: (0, 0)), st_spec,
                  row(B_WIDTH), row(LANES), row(LANES), row(B_WIDTH), c_spec, c_spec],
        out_specs=[row(A_WIDTH), row(B_WIDTH), st_spec, c_spec, c_spec],
        out_shape=[jax.ShapeDtypeStruct((n, A_WIDTH), BF16),
                   jax.ShapeDtypeStruct((n, B_WIDTH), BF16),
                   jax.ShapeDtypeStruct((n, A_HEADS, A_DK, A_DV), F32),
                   jax.ShapeDtypeStruct((n, WINDOW, LANES), F32),
                   jax.ShapeDtypeStruct((n, WINDOW, LANES), F32)],
        scratch_shapes=[pltpu.VMEM((bt, A_WIDTH), F32), pltpu.VMEM((bt * B_HEADS, LANES), F32)],
        compiler_params=pltpu.CompilerParams(dimension_semantics=("arbitrary",),
                                             vmem_limit_bytes=VMEM_LIMIT),
        name="sstep",
    )(sinks, q, k, v, gb, za, na_row, state, qb, kn, vn, zb, ck, cv)


def _rope_tables(pos):
    half = B_HD // 2
    inv = 1.0 / (ROPE_THETA ** (np.arange(half, dtype=np.float64) / half))
    ang = np.asarray(pos, np.float64)[:, None] * inv[None, :]
    cos, sin = np.cos(ang), np.sin(ang)
    reps = LANES // B_HD
    return (jnp.asarray(np.tile(np.concatenate([cos, cos], -1), (1, reps)), F32),
            jnp.asarray(np.tile(np.concatenate([-sin, sin], -1), (1, reps)), F32))


def _pad_row(vec, offset):
    return jnp.zeros((1, LANES), F32).at[0, offset:offset + vec.shape[0]].set(vec.astype(F32))


def _layer(x_prompt, x_sample, state_conv, state_delta, cache_k, cache_v, c_prompt, c_sample,
           w_ada, b_ada, w_in, conv_w, a_log, dt_bias, norm_a, sinks, w_out, ln_g, ln_b):
    bsz, seq, _ = x_prompt.shape
    n_s = x_sample.shape[0]

    w_r = _wprep(jnp.swapaxes(w_in, 0, 1))
    w_o = w_out.astype(BF16)
    alog_row = _pad_row(a_log, A_HEADS)
    dt_row = _pad_row(dt_bias, A_HEADS)
    na_row = norm_a.reshape(1, A_DV)
    g_row = ln_g.reshape(1, D_MODEL)
    b_row = ln_b.reshape(1, D_MODEL)

    assert n_s % 8 == 0 and bsz <= 8
    c_all = jnp.concatenate([c_sample, c_prompt, jnp.zeros((8 - bsz, D_MODEL), F32)], axis=0)
    mod = _ada(c_all, w_ada, b_ada.reshape(1, 3 * D_MODEL))

    cos_p, sin_p = _rope_tables(np.arange(seq))
    (q, k, v, za, gb, qb, kb, kbr, vd0, vd1, zb, conv_p, kb_last, vb_last) = _proj(
        x_prompt, mod, n_s, w_r, conv_w, alog_row, dt_row, cos_p, sin_p)
    oa, delta_p = _delta(q, k, v, gb, za, na_row)
    ob = _swa(sinks, qb, kb, kbr, vd0, vd1, zb)
    y_p = _out(oa, ob, x_prompt, mod, n_s, True, w_o, g_row, b_row, OUT_TM)
    swa_k_p = kb_last.reshape(bsz, WINDOW, B_KV_HEADS, B_HD)
    swa_v_p = vb_last.reshape(bsz, WINDOW, B_KV_HEADS, B_HD)

    cos_s, sin_s = _rope_tables(np.array([PAST_LEN]))
    xs = x_sample.reshape(n_s, D_MODEL)
    cst = jnp.transpose(state_conv, (1, 0, 2))
    sq, sk, sv, sza, sgb, sqb, skn, svn, szb, ncs = _sproj(xs, mod, w_r, conv_w, cst, alog_row, dt_row,
                                                           cos_s, sin_s)
    soa, sob, delta_s, nck, ncv = _sstep(sinks, sq, sk, sv, sgb, sza, na_row, state_delta,
                                         sqb, skn, svn, szb,
                                         jnp.swapaxes(cache_k.reshape(n_s, WINDOW, LANES), 1, 2),
                                         jnp.swapaxes(cache_v.reshape(n_s, WINDOW, LANES), 1, 2))
    y_s = _out(soa[None], sob[None], xs[None], mod, 0, False, w_o, g_row, b_row, n_s)
    conv_s = jnp.transpose(ncs, (1, 0, 2))
    unpack = lambda c: jnp.swapaxes(c, 1, 2).reshape(n_s, WINDOW, B_KV_HEADS, B_HD)
    return (y_p, y_s.reshape(n_s, 1, D_MODEL), conv_p, delta_p, swa_k_p, swa_v_p,
            conv_s, delta_s, unpack(nck), unpack(ncv))


def kernel(x_prompt, x_sample, state_conv, state_delta, cache_swa_k, cache_swa_v, c_prompt, c_sample,
           w_ada, b_ada, w_in, conv_w, a_log, dt_bias, norm_a, sinks, w_out, ln_g, ln_b):
    assert w_ada.shape[0] == DEPTH == 1
    outs = _layer(x_prompt, x_sample, state_conv[0], state_delta[0], cache_swa_k[0], cache_swa_v[0],
                  c_prompt, c_sample, w_ada[0], b_ada[0], w_in[0], conv_w[0], a_log[0], dt_bias[0],
                  norm_a[0], sinks[0], w_out[0], ln_g[0], ln_b[0])
    y_p, y_s = outs[0], outs[1]
    return (y_p, y_s) + tuple(o[None] for o in outs[2:])
```

```python
import functools

import jax
import jax.numpy as jnp
import numpy as np
from jax import lax
from jax.experimental import pallas as pl
from jax.experimental.pallas import tpu as pltpu

F32 = jnp.float32
BF16 = jnp.bfloat16

D_MODEL = 1024
DEPTH = 1
PAST_LEN = 8192
A_HEADS = 4
A_DK = 128
A_DV = 128
A_WIDTH = A_HEADS * A_DV
A_QKV = 3 * A_WIDTH
CONV_W = 4
CHUNK = 64
B_HEADS = 8
B_KV_HEADS = 2
B_HD = 64
B_GROUP = B_HEADS // B_KV_HEADS
B_WIDTH = B_HEADS * B_HD
B_KV_WIDTH = B_KV_HEADS * B_HD
WINDOW = 128
ROPE_THETA = 10000.0
MIX_WIDTH = A_WIDTH + B_WIDTH
DEEPNORM_ALPHA = (2 * DEPTH) ** 0.25
LOG2E = 1.4426950408889634
LN_EPS = 1e-5
RMS_EPS = 1e-6
L2_EPS = 1e-6

OFF_A_Z = A_QKV
OFF_A_BETA = OFF_A_Z + A_WIDTH
OFF_A_DECAY = OFF_A_BETA + A_HEADS
OFF_B_Q = OFF_A_DECAY + A_HEADS
OFF_B_K = OFF_B_Q + B_WIDTH
OFF_B_V = OFF_B_K + B_KV_WIDTH
OFF_B_Z = OFF_B_V + B_KV_WIDTH
PROJ_COLS = OFF_B_Z + B_WIDTH

LANES = 128
C_QKV = 0
C_ZA = C_QKV + A_QKV
C_QB = C_ZA + A_WIDTH
C_KB = C_QB + B_WIDTH
C_VB = C_KB + B_KV_WIDTH
C_ZB = C_VB + B_KV_WIDTH
C_BD = C_ZB + B_WIDTH
WPREP_TN = 256
W_COLS = C_BD + WPREP_TN

VMEM_LIMIT = 56 * 1024 * 1024

PROJ_TM = 512
PROJ_CW = 256
PROJ_PARTS = 4
DELTA_CT = 256
DELTA_WAVE = 2
SWA_TQ = 512
SWA_WAVE = 1
OUT_TM = 1024
OUT_PARTS = 4
STEP_BT = 16


def _dot(a, b):
    return jnp.dot(a, b, preferred_element_type=F32)


def _dot_nt(a, b):
    return lax.dot_general(a, b, (((1,), (1,)), ((), ())), preferred_element_type=F32)


def _silu(x):
    return x * jax.nn.sigmoid(x)


def _softplus(x):
    return jnp.maximum(x, 0.0) + jnp.log1p(jnp.exp(-jnp.abs(x)))


def _lane(shape):
    return lax.broadcasted_iota(jnp.int32, shape, len(shape) - 1)


def _l2norm_heads(y, scale):
    outs = []
    for h in range(y.shape[1] // A_DK):
        xh = y[:, h * A_DK:(h + 1) * A_DK]
        ss = jnp.sum(xh * xh, axis=-1, keepdims=True)
        xn = xh * lax.rsqrt(ss + L2_EPS)
        outs.append(xn * scale if scale != 1.0 else xn)
    return jnp.concatenate(outs, axis=-1)


def _rotary_group(xg, cos, sin_signed):
    lane = _lane(xg.shape)
    swapped = jnp.where((lane % B_HD) < (B_HD // 2),
                        pltpu.roll(xg, LANES - B_HD // 2, axis=1),
                        pltpu.roll(xg, B_HD // 2, axis=1))
    return xg * cos + swapped * sin_signed


def _kv_layouts(kb, vb):
    low = _lane(kb.shape) < B_HD
    kbr = pltpu.roll(kb, B_HD, axis=1)
    vbr = pltpu.roll(vb, B_HD, axis=1)
    return kb, kbr, jnp.where(low, vb, vbr), jnp.where(low, vbr, vb)


def _gate_lanes(bd, alog_row, dt_row):
    lane = _lane(bd.shape)
    g = -jnp.exp(alog_row) * _softplus(bd + dt_row)
    return jnp.where(lane < A_HEADS, jax.nn.sigmoid(bd), g)


def _layer_norm(r, g, b):
    mu = jnp.mean(r, axis=-1, keepdims=True)
    d = r - mu
    var = jnp.mean(d * d, axis=-1, keepdims=True)
    return d * lax.rsqrt(var + LN_EPS) * g + b


def _wprep_kernel(wt_ref, o_ref):
    x = wt_ref[...]
    tail = pl.program_id(0) == pl.num_programs(0) - 1
    row = lax.broadcasted_iota(jnp.int32, x.shape, 0)
    x = jnp.where(jnp.logical_and(tail, row >= 2 * A_HEADS), 0.0, x)
    o_ref[...] = x.T.astype(BF16)


def _wprep(w_t):
    tn = WPREP_TN
    n_a, n_b = OFF_A_BETA // tn, (PROJ_COLS - OFF_B_Q) // tn
    assert n_a * tn == OFF_A_BETA and n_b * tn == PROJ_COLS - OFF_B_Q and OFF_A_BETA + tn <= PROJ_COLS

    def src_row(j):
        return jnp.where(j < n_a, j * tn, jnp.where(j < n_a + n_b, OFF_B_Q + (j - n_a) * tn, OFF_A_BETA))

    return pl.pallas_call(
        _wprep_kernel,
        grid=(n_a + n_b + 1,),
        in_specs=[pl.BlockSpec((pl.Element(tn), pl.Element(D_MODEL)),
                               lambda j: (pl.multiple_of(src_row(j), 8), 0))],
        out_specs=pl.BlockSpec((D_MODEL, tn), lambda j: (0, j)),
        out_shape=jax.ShapeDtypeStruct((D_MODEL, W_COLS), BF16),
        compiler_params=pltpu.CompilerParams(dimension_semantics=("arbitrary",),
                                             vmem_limit_bytes=VMEM_LIMIT),
        name="wprep",
    )(w_t)


def _ada_kernel(c_ref, w_ref, b_ref, o_ref):
    o_ref[...] = _dot(c_ref[...].astype(BF16), w_ref[...].astype(BF16)) + b_ref[...]


def _ada(c_all, w_ada, b_ada):
    rows = c_all.shape[0]
    tn = 768
    return pl.pallas_call(
        _ada_kernel,
        grid=(3 * D_MODEL // tn,),
        in_specs=[pl.BlockSpec((rows, D_MODEL), lambda j: (0, 0)),
                  pl.BlockSpec((D_MODEL, tn), lambda j: (0, j)),
                  pl.BlockSpec((1, tn), lambda j: (0, j))],
        out_specs=pl.BlockSpec((rows, tn), lambda j: (0, j)),
        out_shape=jax.ShapeDtypeStruct((rows, 3 * D_MODEL), F32),
        compiler_params=pltpu.CompilerParams(dimension_semantics=("arbitrary",),
                                             vmem_limit_bytes=VMEM_LIMIT),
        name="ada",
    )(c_all, w_ada, b_ada)


def _proj_kernel(x_ref, mod_ref, w_ref, cw_ref, alog_ref, dt_ref, cos_ref, sin_ref,
                 q_ref, k_ref, v_ref, za_ref, gb_ref, qb_ref, kb_ref, kbr_ref, vd0_ref, vd1_ref,
                 zb_ref, cst_ref, kbl_ref, vbl_ref, ubuf):
    tm = x_ref.shape[1]
    t = pl.program_id(1)

    @pl.when(t == 0)
    def _():
        ubuf[...] = jnp.zeros(ubuf.shape, F32)

    brow = pl.ds(pl.program_id(0), 1)
    shift = mod_ref[brow, 0:D_MODEL]
    scale = mod_ref[brow, D_MODEL:2 * D_MODEL]

    rp = tm // PROJ_PARTS
    cw = PROJ_CW
    sub = lax.broadcasted_iota(jnp.int32, (rp // 8, 8, cw), 1)
    pieces = [slice(c0, c0 + cw) for c0 in range(0, A_QKV, cw)]
    outs = (q_ref, k_ref, v_ref)

    def part(r0):
        rs = slice(r0, r0 + rp)
        h = (x_ref[0, rs, :] * (1.0 + scale) + shift).astype(BF16)

        def conv_epilogue(cs, u):
            gi, c_in = cs.start // A_WIDTH, cs.start % A_WIDTH
            groups = jnp.concatenate([ubuf[:, cs], u], axis=0).reshape(rp // 8 + 1, 8, cw)
            acc = None
            for j in range(CONV_W - 1, 0, -1):
                rot = pltpu.roll(groups, j, axis=1)
                term = (jnp.where(sub < j, rot[:-1], rot[1:]).reshape(rp, cw)
                        * cw_ref[CONV_W - 1 - j:CONV_W - j, cs])
                acc = term if acc is None else acc + term
            y = _silu(acc + u * cw_ref[CONV_W - 1:CONV_W, cs])
            if gi == 0:
                y = _l2norm_heads(y, A_DK ** -0.5)
            elif gi == 1:
                y = _l2norm_heads(y, 1.0)
            outs[gi][0, rs, c_in:c_in + cw] = y.astype(BF16)
            ubuf[:, cs] = u[rp - 8:rp]
            if r0 + rp == tm:
                cst_ref[0, :, cs] = u[rp - (CONV_W - 1):rp]

        pending = _dot(h, w_ref[:, pieces[0]])
        for i, cs in enumerate(pieces):
            u = pending
            if i + 1 < len(pieces):
                pending = _dot(h, w_ref[:, pieces[i + 1]])
            conv_epilogue(cs, u)

        za_ref[0, rs, :] = _dot(h, w_ref[:, C_ZA:C_ZA + A_WIDTH]).astype(BF16)
        gb_ref[0, rs, :] = _gate_lanes(_dot(h, w_ref[:, C_BD:C_BD + LANES]), alog_ref[...], dt_ref[...])
        cos = cos_ref[rs, :]
        sin = sin_ref[rs, :]
        uq = _dot(h, w_ref[:, C_QB:C_QB + B_WIDTH])
        for g in range(B_WIDTH // LANES):
            qb_ref[0, rs, g * LANES:(g + 1) * LANES] = (
                _rotary_group(uq[:, g * LANES:(g + 1) * LANES], cos, sin) * (B_HD ** -0.5 * LOG2E)).astype(BF16)
        ukv = _dot(h, w_ref[:, C_KB:C_KB + 2 * LANES])
        kb = _rotary_group(ukv[:, 0:LANES], cos, sin)
        vb = ukv[:, LANES:2 * LANES]
        for o_ref, val in zip((kb_ref, kbr_ref, vd0_ref, vd1_ref), _kv_layouts(kb, vb)):
            o_ref[0, rs, :] = val.astype(BF16)
        zb_ref[0, rs, :] = _dot(h, w_ref[:, C_ZB:C_ZB + B_WIDTH]).astype(BF16)
        return kb, vb

    for r0 in range(0, tm, rp):
        kb, vb = part(r0)

    @pl.when(t == pl.num_programs(1) - 1)
    def _():
        kbl_ref[0] = kb[rp - WINDOW:rp]
        vbl_ref[0] = vb[rp - WINDOW:rp]


def _proj(x, mod, mod_row0, w_r, conv_w, alog_row, dt_row, cos_t, sin_t):
    bsz, t, _ = x.shape
    tm = PROJ_TM
    row = lambda w: pl.BlockSpec((1, tm, w), lambda b, i: (b, i, 0))
    const2 = lambda s: pl.BlockSpec(s, lambda b, i: (0, 0))
    per_b = lambda r, w: pl.BlockSpec((1, r, w), lambda b, i: (b, 0, 0))
    wide = lambda w, dt=F32: jax.ShapeDtypeStruct((bsz, t, w), dt)
    return pl.pallas_call(
        _proj_kernel,
        grid=(bsz, t // tm),
        in_specs=[row(D_MODEL),
                  pl.BlockSpec((8, 3 * D_MODEL), lambda b, i: (mod_row0 // 8, 0)),
                  const2((D_MODEL, W_COLS)),
                  const2((CONV_W, A_QKV)),
                  const2((1, LANES)), const2((1, LANES)),
                  pl.BlockSpec((tm, LANES), lambda b, i: (i, 0)),
                  pl.BlockSpec((tm, LANES), lambda b, i: (i, 0))],
        out_specs=[row(A_WIDTH), row(A_WIDTH), row(A_WIDTH), row(A_WIDTH), row(LANES),
                   row(B_WIDTH), row(LANES), row(LANES), row(LANES), row(LANES), row(B_WIDTH),
                   per_b(CONV_W - 1, A_QKV), per_b(WINDOW, LANES), per_b(WINDOW, LANES)],
        out_shape=[wide(A_WIDTH, BF16), wide(A_WIDTH, BF16), wide(A_WIDTH, BF16), wide(A_WIDTH, BF16),
                   wide(LANES),
                   wide(B_WIDTH, BF16), wide(LANES, BF16), wide(LANES, BF16), wide(LANES, BF16),
                   wide(LANES, BF16), wide(B_WIDTH, BF16),
                   jax.ShapeDtypeStruct((bsz, CONV_W - 1, A_QKV), F32),
                   jax.ShapeDtypeStruct((bsz, WINDOW, LANES), F32),
                   jax.ShapeDtypeStruct((bsz, WINDOW, LANES), F32)],
        scratch_shapes=[pltpu.VMEM((8, A_QKV), F32)],
        compiler_params=pltpu.CompilerParams(dimension_semantics=("arbitrary", "arbitrary"),
                                             vmem_limit_bytes=VMEM_LIMIT),
        name="proj",
    )(x, mod, w_r, conv_w, alog_row, dt_row, cos_t, sin_t)


def _delta_kernel(q_ref, k_ref, v_ref, gb_ref, za_ref, na_ref, oa_ref, st_ref,
                  s_scr, wq_s, ut_s, akd_s, gl_s):
    bsz, ct = q_ref.shape[0], q_ref.shape[1]
    nch = ct // CHUNK
    t = pl.program_id(0)
    wslot = t % 2
    rslot = 1 - wslot

    @pl.when(t == 0)
    def _():
        s_scr[...] = jnp.zeros(s_scr.shape, F32)
        wq_s[...] = jnp.zeros(wq_s.shape, BF16)
        ut_s[...] = jnp.zeros(ut_s.shape, F32)
        akd_s[...] = jnp.zeros(akd_s.shape, BF16)
        gl_s[...] = jnp.zeros(gl_s.shape, F32)

    units = [(b, c, h) for b in range(bsz) for c in range(nch) for h in range(A_HEADS)]
    uid = {u_: i for i, u_ in enumerate(units)}
    rows = lambda c: slice(c * CHUNK, (c + 1) * CHUNK)
    lanes = lambda h: slice(h * A_DK, (h + 1) * A_DK)
    na = na_ref[...]

    s_cur = {(b, h): s_scr[b * A_HEADS + h] for b in range(bsz) for h in range(A_HEADS)}
    ws, uu = {}, {}

    def rec_ws(c):
        for b in range(bsz):
            for h in range(A_HEADS):
                i = uid[b, c, h]
                ws[b, h] = _dot(wq_s[rslot, i], s_cur[b, h].astype(BF16))
                uu[b, h] = (ut_s[rslot, i] - ws[b, h][:CHUNK]).astype(BF16)

    def rec_ou(c):
        zpad = jnp.zeros((CHUNK, A_DV), BF16)
        for b in range(bsz):
            u_bd = jnp.concatenate(
                [jnp.concatenate([uu[b, h] if hh == h else zpad for hh in range(A_HEADS)], axis=-1)
                 for h in range(A_HEADS)], axis=0)
            ou = _dot(akd_s[rslot, b * nch + c], u_bd)
            for h in range(A_HEADS):
                o = ws[b, h][CHUNK:] + ou[:CHUNK, lanes(h)]
                s_cur[b, h] = gl_s[rslot, uid[b, c, h]] * s_cur[b, h] + ou[CHUNK:, lanes(h)]
                on = o * lax.rsqrt(jnp.mean(o * o, axis=-1, keepdims=True) + RMS_EPS) * na
                za = za_ref[b, rows(c), lanes(h)].astype(F32)
                oa_ref[b, rows(c), lanes(h)] = (on * _silu(za)).astype(BF16)

    rec_stages = []
    for c in range(nch):
        rec_stages += [lambda c=c: rec_ws(c), lambda c=c: rec_ou(c)]

    def run_rec(n_left_after):
        while rec_stages and len(rec_stages) > n_left_after:
            rec_stages.pop(0)()

    pk = A_HEADS * CHUNK
    low = _lane((CHUNK, LANES)) < CHUNK
    low_row = _lane((1, LANES)) < CHUNK
    ti_p = lax.broadcasted_iota(jnp.int32, (CHUNK, pk), 0)
    ii_p = _lane((CHUNK, pk)) % CHUNK
    zero64 = jnp.zeros((CHUNK, LANES), BF16)

    def pack(parts):
        return jnp.concatenate([jnp.where(low, parts[0], parts[1]), jnp.where(low, parts[2], parts[3])], axis=-1)

    def block_diag(x16):
        blocks = []
        for h in range(A_HEADS):
            pair, first = h // 2, h % 2 == 0
            piece = jnp.where(low if first else jnp.logical_not(low), x16[:, pair * LANES:(pair + 1) * LANES], zero64)
            blocks.append(jnp.concatenate([piece, zero64] if pair == 0 else [zero64, piece], axis=-1))
        return jnp.concatenate(blocks, axis=0)

    zrhs = jnp.zeros((CHUNK, 2 * A_DK), BF16)
    n_rec = len(rec_stages)
    n_slots = 8 * (bsz // DELTA_WAVE)
    done = [0]

    def stage_done():
        done[0] += 1
        run_rec(n_rec - (done[0] * n_rec) // n_slots)

    def decay_terms(b, beta, g_col, g_last, eg, dec_p, beta_p):
        gbv = gb_ref[b]
        rin = lax.broadcasted_iota(jnp.int32, gbv.shape, 0) % CHUNK
        gcs = gbv
        s = 1
        while s < CHUNK:
            gcs = gcs + jnp.where(rin >= s, pltpu.roll(gcs, s, axis=0), 0.0)
            s *= 2
        gcs_t = gcs.T
        for c in range(nch):
            r0 = c * CHUNK
            pair_lanes = slice((c // 2) * LANES, (c // 2 + 1) * LANES)
            g_rows = []
            for h in range(A_HEADS):
                u_ = (b, c, h)
                beta[u_] = jnp.broadcast_to(gbv[rows(c), h:h + 1], (CHUNK, A_DK))
                g_col[u_] = jnp.broadcast_to(gcs[rows(c), A_HEADS + h:A_HEADS + h + 1], (CHUNK, A_DK))
                g_last[u_] = gcs[r0 + CHUNK - 1:r0 + CHUNK, A_HEADS + h:A_HEADS + h + 1]
                eg[u_] = jnp.exp(g_col[u_])
                g_row = gcs_t[A_HEADS + h:A_HEADS + h + 1, pair_lanes]
                g_rows.append(g_row if c % 2 == h % 2 else pltpu.roll(g_row, CHUNK, axis=1))
            g_row_p = jnp.concatenate([jnp.where(low_row, g_rows[0], g_rows[1]),
                                       jnp.where(low_row, g_rows[2], g_rows[3])], axis=-1)
            g_col_p = pack([g_col[b, c, h] for h in range(A_HEADS)])
            dec_p[b, c] = jnp.exp(jnp.where(ti_p >= ii_p, g_col_p - g_row_p, -jnp.inf))
            beta_p[b, c] = pack([beta[b, c, h] for h in range(A_HEADS)])

    def prepare(bs):
        groups_b = [(b, c) for b in bs for c in range(nch)]
        beta, g_col, g_last, eg, dec_p, beta_p = {}, {}, {}, {}, {}, {}
        for b in bs:
            decay_terms(b, beta, g_col, g_last, eg, dec_p, beta_p)

        nmat = {}
        for (b, c) in groups_b:
            k16 = k_ref[b, rows(c), :].astype(BF16)
            q16 = q_ref[b, rows(c), :].astype(BF16)
            k_heads = jnp.concatenate(
                [jnp.concatenate([k16[:, lanes(h)] if hh == h else zero64 for hh in range(A_HEADS)], axis=-1)
                 for h in range(A_HEADS)], axis=0)
            kq = _dot_nt(jnp.concatenate([k16, q16], axis=0), k_heads)
            nmat[b, c] = -(beta_p[b, c] * kq[:CHUNK] * jnp.where(ti_p > ii_p, dec_p[b, c], 0.0))
            akd_s[wslot, b * nch + c, 0:CHUNK, :] = (kq[CHUNK:] * dec_p[b, c]).astype(BF16)
        stage_done()

        rsum = dict(nmat)
        pw16 = {g_: nmat[g_].astype(BF16) for g_ in groups_b}
        pw = {g_: _dot(pw16[g_], block_diag(pw16[g_])) for g_ in groups_b}
        stage_done()
        for step in range(1, 6):
            last = step == 5
            pw16 = {g_: pw[g_].astype(BF16) for g_ in groups_b}
            rp = {}
            for g_ in groups_b:
                r16 = rsum[g_].astype(BF16)
                rp[g_] = _dot(r16 if last else jnp.concatenate([r16, pw16[g_]], axis=0), block_diag(pw16[g_]))
            for g_ in groups_b:
                rsum[g_] = rsum[g_] + pw[g_] + rp[g_][:CHUNK]
                if not last:
                    pw[g_] = rp[g_][CHUNK:]
            stage_done()

        for (b, c) in groups_b:
            for h in range(A_HEADS):
                u_ = (b, c, h)
                i = uid[u_]
                kc = k_ref[b, rows(c), lanes(h)].astype(F32)
                vc = v_ref[b, rows(c), lanes(h)].astype(F32)
                rhs = jnp.concatenate([(beta[u_] * eg[u_]) * kc, beta[u_] * vc], axis=-1)
                rhs16 = rhs.astype(BF16)
                rhs_rows = jnp.concatenate([rhs16 if hh == h else zrhs for hh in range(A_HEADS)], axis=0)
                sol = rhs + _dot(rsum[b, c].astype(BF16), rhs_rows)
                qd = eg[u_] * q_ref[b, rows(c), lanes(h)].astype(F32)
                wq_s[wslot, i] = jnp.concatenate([sol[:, :A_DK], qd], axis=0).astype(BF16)
                ut_s[wslot, i] = sol[:, A_DK:]
                gl_s[wslot, i] = jnp.broadcast_to(jnp.exp(g_last[u_]), (1, A_DV))
        for (b, c) in groups_b:
            kd = [jnp.exp(g_last[b, c, h] - g_col[b, c, h]) * k_ref[b, rows(c), lanes(h)].astype(F32)
                  for h in range(A_HEADS)]
            for p in range(A_HEADS // 2):
                akd_s[wslot, b * nch + c, CHUNK:, p * LANES:(p + 1) * LANES] = (
                    jnp.concatenate([kd[2 * p], kd[2 * p + 1]], axis=0).T.astype(BF16))
        stage_done()

    for b0 in range(0, bsz, DELTA_WAVE):
        prepare(range(b0, b0 + DELTA_WAVE))
    run_rec(0)

    for b in range(bsz):
        for h in range(A_HEADS):
            s_scr[b * A_HEADS + h] = s_cur[b, h]

    @pl.when(t == pl.num_programs(0) - 1)
    def _():
        for b in range(bsz):
            for h in range(A_HEADS):
                st_ref[b, h] = s_cur[b, h]


def _delta(q, k, v, gb, za, na_row):
    bsz, t, _ = q.shape
    ct = DELTA_CT
    nt = t // ct
    n_units = bsz * (ct // CHUNK) * A_HEADS
    prep = lambda w: pl.BlockSpec((bsz, ct, w), lambda i: (0, jnp.minimum(i, nt - 1), 0))
    rec = lambda w: pl.BlockSpec((bsz, ct, w), lambda i: (0, jnp.maximum(i - 1, 0), 0))
    return pl.pallas_call(
        _delta_kernel,
        grid=(nt + 1,),
        in_specs=[prep(A_WIDTH), prep(A_WIDTH), prep(A_WIDTH), prep(LANES), rec(A_WIDTH),
                  pl.BlockSpec((1, A_DV), lambda i: (0, 0))],
        out_specs=[rec(A_WIDTH),
                   pl.BlockSpec((bsz, A_HEADS, A_DK, A_DV), lambda i: (0, 0, 0, 0))],
        out_shape=[jax.ShapeDtypeStruct((bsz, t, A_WIDTH), BF16),
                   jax.ShapeDtypeStruct((bsz, A_HEADS, A_DK, A_DV), F32)],
        scratch_shapes=[pltpu.VMEM((bsz * A_HEADS, A_DK, A_DV), F32),
                        pltpu.VMEM((2, n_units, 2 * CHUNK, A_DK), BF16),
                        pltpu.VMEM((2, n_units, CHUNK, A_DV), F32),
                        pltpu.VMEM((2, n_units // A_HEADS, CHUNK + A_DK, A_HEADS * CHUNK), BF16),
                        pltpu.VMEM((2, n_units, 1, A_DV), F32)],
        compiler_params=pltpu.CompilerParams(dimension_semantics=("arbitrary",),
                                             vmem_limit_bytes=VMEM_LIMIT),
        name="delta",
    )(q, k, v, gb, za, na_row)


def _swa_kernel(sink_ref, qb_ref, kc_ref, kp_ref, krc_ref, krp_ref, v0c_ref, v0p_ref, v1c_ref, v1p_ref,
                zb_ref, ob_ref):
    n = pl.program_id(1)
    tq = qb_ref.shape[1]
    blk = WINDOW
    kx = (jnp.concatenate([kp_ref[0], kc_ref[0]], axis=0), jnp.concatenate([krp_ref[0], krc_ref[0]], axis=0))
    vd = (jnp.concatenate([v0p_ref[0], v0c_ref[0]], axis=0), jnp.concatenate([v1p_ref[0], v1c_ref[0]], axis=0))

    a = lax.broadcasted_iota(jnp.int32, (2 * blk, 2 * blk), 0) % blk
    j = lax.broadcasted_iota(jnp.int32, (2 * blk, 2 * blk), 1)
    rel = a + blk - j
    band = (rel >= 0) & (rel <= WINDOW)
    band_first = band & ((n > 0) | (j >= blk))
    top = lax.broadcasted_iota(jnp.int32, (2 * blk, 1), 0) < blk
    low = _lane((blk, LANES)) < B_HD
    zero = jnp.zeros((blk, LANES), BF16)

    qrows = lambda i: slice(i * blk, (i + 1) * blk)
    krows = lambda i: slice(i * blk, (i + 2) * blk)
    sink = {(kh, half): jnp.where(top, sink_ref[kh * B_GROUP + half] * LOG2E,
                                  sink_ref[kh * B_GROUP + half + 2] * LOG2E)
            for kh in range(B_KV_HEADS) for half in range(2)}
    for i0 in range(0, tq // blk, SWA_WAVE):
        blocks = range(i0, i0 + SWA_WAVE)
        units = [(i, kh, half) for i in blocks for kh in range(B_KV_HEADS) for half in range(2)]
        sc = {}
        for (i, kh, half) in units:
            qs = []
            for g in range(2):
                grp = kh * 2 + g
                xg = qb_ref[0, qrows(i), grp * LANES:(grp + 1) * LANES]
                qs.append(jnp.where(low if half == 0 else jnp.logical_not(low), xg, zero))
            qz = jnp.concatenate(qs, axis=0)
            sc[i, kh, half] = _dot_nt(qz, kx[0 if kh == half else 1][krows(i)])
        p, den = {}, {}
        for u_ in units:
            i, kh, half = u_
            s_m = jnp.where(band_first if i == 0 else band, sc[u_], -jnp.inf)
            m = jnp.maximum(jnp.max(s_m, axis=-1, keepdims=True), sink[kh, half])
            e = jnp.exp2(s_m - m)
            den[u_] = jnp.sum(e, axis=-1, keepdims=True) + jnp.exp2(sink[kh, half] - m)
            p[u_] = e.astype(BF16)
        pv = {u_: _dot(p[u_], vd[u_[1]][krows(u_[0])]) for u_ in units}
        outs = {u_: pv[u_] / den[u_] for u_ in units}
        for i in blocks:
            for grp in range(B_WIDTH // LANES):
                kh, g = grp // 2, grp % 2
                og = jnp.where(low, outs[i, kh, 0][g * blk:(g + 1) * blk], outs[i, kh, 1][g * blk:(g + 1) * blk])
                gs = slice(grp * LANES, (grp + 1) * LANES)
                ob_ref[0, qrows(i), gs] = (og * _silu(zb_ref[0, qrows(i), gs].astype(F32))).astype(BF16)


def _swa(sinks, qb, kb, kbr, vd0, vd1, zb):
    bsz, t, _ = qb.shape
    tq = SWA_TQ
    per = tq // WINDOW
    cur = lambda w: pl.BlockSpec((1, tq, w), lambda b, i: (b, i, 0))
    prev = pl.BlockSpec((1, WINDOW, LANES), lambda b, i: (b, jnp.maximum(i * per - 1, 0), 0))
    return pl.pallas_call(
        _swa_kernel,
        grid=(bsz, t // tq),
        in_specs=[pl.BlockSpec(memory_space=pltpu.SMEM), cur(B_WIDTH),
                  cur(LANES), prev, cur(LANES), prev, cur(LANES), prev, cur(LANES), prev,
                  cur(B_WIDTH)],
        out_specs=cur(B_WIDTH),
        out_shape=jax.ShapeDtypeStruct((bsz, t, B_WIDTH), BF16),
        compiler_params=pltpu.CompilerParams(dimension_semantics=("arbitrary", "arbitrary"),
                                             vmem_limit_bytes=VMEM_LIMIT),
        name="swa",
    )(sinks, qb, kb, kb, kbr, kbr, vd0, vd0, vd1, vd1, zb)


def _out_kernel(oa_ref, ob_ref, x_ref, gate_ref, w_ref, g_ref, b_ref, y_ref, *, gate_per_batch):
    tm = x_ref.shape[1]
    parts = OUT_PARTS if tm % (8 * OUT_PARTS) == 0 and gate_per_batch else 1
    rows = [slice(p * (tm // parts), (p + 1) * (tm // parts)) for p in range(parts)]
    mix = [_dot(oa_ref[0, r_, :], w_ref[0:A_WIDTH, :]) + _dot(ob_ref[0, r_, :], w_ref[A_WIDTH:MIX_WIDTH, :])
           for r_ in rows]
    gate = gate_ref[pl.ds(pl.program_id(0), 1), :] if gate_per_batch else gate_ref[...]
    for r_, m_ in zip(rows, mix):
        r = DEEPNORM_ALPHA * x_ref[0, r_, :] + (1.0 + gate) * m_
        y_ref[0, r_, :] = _layer_norm(r, g_ref[...], b_ref[...])


def _out(oa, ob, x, mod, mod_row0, gate_per_batch, w_out, ln_g, ln_b, tm):
    bsz, t, _ = x.shape
    grows = 8 if gate_per_batch else tm
    row = lambda w: pl.BlockSpec((1, tm, w), lambda b, i: (b, i, 0))
    const2 = lambda s: pl.BlockSpec(s, lambda b, i: (0, 0))
    return pl.pallas_call(
        functools.partial(_out_kernel, gate_per_batch=gate_per_batch),
        grid=(bsz, t // tm),
        in_specs=[row(A_WIDTH), row(B_WIDTH), row(D_MODEL),
                  pl.BlockSpec((grows, D_MODEL), lambda b, i: (mod_row0 // grows, 2)),
                  const2((MIX_WIDTH, D_MODEL)), const2((1, D_MODEL)), const2((1, D_MODEL))],
        out_specs=row(D_MODEL),
        out_shape=jax.ShapeDtypeStruct((bsz, t, D_MODEL), F32),
        compiler_params=pltpu.CompilerParams(dimension_semantics=("arbitrary", "arbitrary"),
                                             vmem_limit_bytes=VMEM_LIMIT),
        name="out",
    )(oa, ob, x, mod, w_out, ln_g, ln_b)


def _sproj_kernel(x_ref, mod_ref, w_ref, cw_ref, cst_ref, alog_ref, dt_ref, cos_ref, sin_ref,
                  q_ref, k_ref, v_ref, za_ref, gb_ref, qb_ref, kb_ref, vb_ref, zb_ref, ncs_ref):
    shift = mod_ref[:, 0:D_MODEL]
    scale = mod_ref[:, D_MODEL:2 * D_MODEL]
    h = (x_ref[...] * (1.0 + scale) + shift).astype(BF16)

    for gi, o_ref in enumerate((q_ref, k_ref, v_ref)):
        c0 = gi * A_WIDTH
        cs = slice(c0, c0 + A_WIDTH)
        u = _dot(h, w_ref[:, cs])
        acc = cst_ref[0, :, cs] * cw_ref[0:1, cs]
        acc = acc + cst_ref[1, :, cs] * cw_ref[1:2, cs]
        acc = acc + cst_ref[2, :, cs] * cw_ref[2:3, cs]
        acc = acc + u * cw_ref[3:4, cs]
        y = _silu(acc)
        if gi == 0:
            y = _l2norm_heads(y, A_DK ** -0.5)
        elif gi == 1:
            y = _l2norm_heads(y, 1.0)
        o_ref[...] = y
        ncs_ref[0, :, cs] = cst_ref[1, :, cs]
        ncs_ref[1, :, cs] = cst_ref[2, :, cs]
        ncs_ref[2, :, cs] = u

    za_ref[...] = _dot(h, w_ref[:, C_ZA:C_ZA + A_WIDTH])
    gb_ref[...] = _gate_lanes(_dot(h, w_ref[:, C_BD:C_BD + LANES]), alog_ref[...], dt_ref[...])

    cos = cos_ref[...]
    sin = sin_ref[...]
    uq = _dot(h, w_ref[:, C_QB:C_QB + B_WIDTH])
    for g in range(B_WIDTH // LANES):
        qb_ref[:, g * LANES:(g + 1) * LANES] = (
            _rotary_group(uq[:, g * LANES:(g + 1) * LANES], cos, sin) * (B_HD ** -0.5))
    kb_ref[...] = _rotary_group(_dot(h, w_ref[:, C_KB:C_KB + LANES]), cos, sin)
    vb_ref[...] = _dot(h, w_ref[:, C_VB:C_VB + LANES])
    zb_ref[...] = _dot(h, w_ref[:, C_ZB:C_ZB + B_WIDTH])


def _sproj(x, mod_s, w_r, conv_w, cst, alog_row, dt_row, cos_row, sin_row):
    n = x.shape[0]
    full = lambda s: pl.BlockSpec(s, lambda i: (0,) * len(s))
    wide = lambda w: jax.ShapeDtypeStruct((n, w), F32)
    return pl.pallas_call(
        _sproj_kernel,
        grid=(1,),
        in_specs=[full((n, D_MODEL)), pl.BlockSpec((n, 3 * D_MODEL), lambda i: (0, 0)),
                  full((D_MODEL, W_COLS)),
                  full((CONV_W, A_QKV)), full((CONV_W - 1, n, A_QKV)),
                  full((1, LANES)), full((1, LANES)), full((1, LANES)), full((1, LANES))],
        out_specs=[full((n, A_WIDTH)), full((n, A_WIDTH)), full((n, A_WIDTH)), full((n, A_WIDTH)),
                   full((n, LANES)), full((n, B_WIDTH)), full((n, LANES)), full((n, LANES)),
                   full((n, B_WIDTH)), full((CONV_W - 1, n, A_QKV))],
        out_shape=[wide(A_WIDTH), wide(A_WIDTH), wide(A_WIDTH), wide(A_WIDTH), wide(LANES),
                   wide(B_WIDTH), wide(LANES), wide(LANES), wide(B_WIDTH),
                   jax.ShapeDtypeStruct((CONV_W - 1, n, A_QKV), F32)],
        compiler_params=pltpu.CompilerParams(dimension_semantics=("arbitrary",),
                                             vmem_limit_bytes=VMEM_LIMIT),
        name="sproj",
    )(x, mod_s, w_r, conv_w, cst, alog_row, dt_row, cos_row, sin_row)


def _sstep_kernel(sink_ref, q_ref, k_ref, v_ref, gb_ref, za_ref, na_ref, st_ref,
                  qb_ref, kn_ref, vn_ref, zb_ref, ck_ref, cv_ref,
                  oa_ref, ob_ref, nst_ref, nck_ref, ncv_ref,
                  o_scr, ob_scr):
    bt = q_ref.shape[0]
    gbv = gb_ref[...]

    pick = (lax.broadcasted_iota(jnp.int32, (bt, bt * A_DV), 1) // A_DV
            == lax.broadcasted_iota(jnp.int32, (bt, bt * A_DV), 0))
    pick = jnp.where(pick, 1.0, 0.0).astype(BF16)
    for h in range(A_HEADS):
        hs = slice(h * A_DK, (h + 1) * A_DK)
        q_rep = _dot(q_ref[:, hs].T.astype(BF16), pick)
        k_rep = _dot(k_ref[:, hs].T.astype(BF16), pick)
        for bb in range(bt):
            eg = jnp.exp(gbv[bb:bb + 1, A_HEADS + h:A_HEADS + h + 1])
            beta = gbv[bb:bb + 1, h:h + 1]
            kcol = k_rep[:, bb * A_DV:(bb + 1) * A_DV]
            qcol = q_rep[:, bb * A_DV:(bb + 1) * A_DV]
            s1 = eg * st_ref[bb, h]
            pred = jnp.sum(kcol * s1, axis=0, keepdims=True)
            upd = beta * (v_ref[bb:bb + 1, hs] - pred)
            s2 = s1 + kcol * upd
            nst_ref[bb, h] = s2
            o_scr[bb:bb + 1, hs] = jnp.sum(qcol * s2, axis=0, keepdims=True)
    na = na_ref[...]
    for h in range(A_HEADS):
        hs = slice(h * A_DK, (h + 1) * A_DK)
        o = o_scr[:, hs]
        on = o * lax.rsqrt(jnp.mean(o * o, axis=-1, keepdims=True) + RMS_EPS) * na
        oa_ref[:, hs] = (on * _silu(za_ref[:, hs])).astype(BF16)

    row8 = lax.broadcasted_iota(jnp.int32, (B_HEADS, LANES), 0)
    lane8 = _lane((B_HEADS, LANES))
    own_half = (lane8 >= B_HD) == (row8 >= B_GROUP)
    rcol = lax.broadcasted_iota(jnp.int32, (B_HEADS, 1), 0)
    sink = jnp.zeros((B_HEADS, 1), F32)
    for r in range(B_HEADS):
        sink = jnp.where(rcol == r, sink_ref[r], sink)
    qv = qb_ref[...]
    qv_r = jnp.concatenate([pltpu.roll(qv[:, g * LANES:(g + 1) * LANES], B_HD, axis=1)
                            for g in range(B_WIDTH // LANES)], axis=-1)
    kn_t = kn_ref[...].T
    vn_t = vn_ref[...].T
    newest = _lane((LANES, WINDOW)) == WINDOW - 1
    qzs, scs = [], []
    for bb in range(bt):
        qz = jnp.zeros((B_HEADS, LANES), F32)
        for r in range(B_HEADS):
            grp, half, kh = r // 2, r % 2, r // B_GROUP
            src = qv if half == kh else qv_r
            qz = jnp.where(row8 == r, src[bb:bb + 1, grp * LANES:(grp + 1) * LANES], qz)
        qzs.append(jnp.where(own_half, qz, 0.0))
    for bb in range(bt):
        scs.append(_dot(qzs[bb], ck_ref[bb]))
    ps, pnews, dens = [], [], []
    for bb in range(bt):
        sc_new = jnp.sum(qzs[bb] * kn_ref[bb:bb + 1, :], axis=-1, keepdims=True)
        m = jnp.maximum(jnp.maximum(jnp.max(scs[bb], axis=-1, keepdims=True), sc_new), sink)
        p = jnp.exp(scs[bb] - m)
        p_new = jnp.exp(sc_new - m)
        ps.append(p)
        pnews.append(p_new)
        dens.append(jnp.sum(p, axis=-1, keepdims=True) + p_new + jnp.exp(sink - m))
    pvs = [_dot_nt(ps[bb], cv_ref[bb]) for bb in range(bt)]
    for bb in range(bt):
        o = (pvs[bb] + pnews[bb] * vn_ref[bb:bb + 1, :]) / dens[bb]
        o = jnp.where(own_half, o, 0.0)
        ob_scr[bb * B_HEADS:(bb + 1) * B_HEADS, :] = o + pltpu.roll(o, B_HD, axis=1)
    for bb in range(bt):
        nck_ref[bb] = jnp.where(newest, kn_t[:, bb:bb + 1], pltpu.roll(ck_ref[bb], WINDOW - 1, axis=1))
        ncv_ref[bb] = jnp.where(newest, vn_t[:, bb:bb + 1], pltpu.roll(cv_ref[bb], WINDOW - 1, axis=1))
    low = _lane((bt, LANES)) < B_HD
    for grp in range(B_WIDTH // LANES):
        even = ob_scr[pl.ds(2 * grp, bt, stride=B_HEADS), :]
        odd = ob_scr[pl.ds(2 * grp + 1, bt, stride=B_HEADS), :]
        gs = slice(grp * LANES, (grp + 1) * LANES)
        ob_ref[:, gs] = (jnp.where(low, even, odd) * _silu(zb_ref[:, gs])).astype(BF16)


def _sstep(sinks, q, k, v, gb, za, na_row, state, qb, kn, vn, zb, ck, cv):
    n = q.shape[0]
    bt = STEP_BT
    row = lambda w: pl.BlockSpec((bt, w), lambda i: (i, 0))
    st_spec = pl.BlockSpec((bt, A_HEADS, A_DK, A_DV), lambda i: (i, 0, 0, 0))
    c_spec = pl.BlockSpec((bt, WINDOW, LANES), lambda i: (i, 0, 0))
    return pl.pallas_call(
        _sstep_kernel,
        grid=(n // bt,),
        in_specs=[pl.BlockSpec(memory_space=pltpu.SMEM),
                  row(A_WIDTH), row(A_WIDTH), row(A_WIDTH), row(LANES), row(A_WIDTH),
                  pl.BlockSpec((1, A_DV), lambda i: (0, 0)), st_spec,
                  row(B_WIDTH), row(LANES), row(LANES), row(B_WIDTH), c_spec, c_spec],
        out_specs=[row(A_WIDTH), row(B_WIDTH), st_spec, c_spec, c_spec],
        out_shape=[jax.ShapeDtypeStruct((n, A_WIDTH), BF16),
                   jax.ShapeDtypeStruct((n, B_WIDTH), BF16),
                   jax.ShapeDtypeStruct((n, A_HEADS, A_DK, A_DV), F32),
                   jax.ShapeDtypeStruct((n, WINDOW, LANES), F32),
                   jax.ShapeDtypeStruct((n, WINDOW, LANES), F32)],
        scratch_shapes=[pltpu.VMEM((bt, A_WIDTH), F32), pltpu.VMEM((bt * B_HEADS, LANES), F32)],
        compiler_params=pltpu.CompilerParams(dimension_semantics=("arbitrary",),
                                             vmem_limit_bytes=VMEM_LIMIT),
        name="sstep",
    )(sinks, q, k, v, gb, za, na_row, state, qb, kn, vn, zb, ck, cv)


def _rope_tables(pos):
    half = B_HD // 2
    inv = 1.0 / (ROPE_THETA ** (np.arange(half, dtype=np.float64) / half))
    ang = np.asarray(pos, np.float64)[:, None] * inv[None, :]
    cos, sin = np.cos(ang), np.sin(ang)
    reps = LANES // B_HD
    return (jnp.asarray(np.tile(np.concatenate([cos, cos], -1), (1, reps)), F32),
            jnp.asarray(np.tile(np.concatenate([-sin, sin], -1), (1, reps)), F32))


def _pad_row(vec, offset):
    return jnp.zeros((1, LANES), F32).at[0, offset:offset + vec.shape[0]].set(vec.astype(F32))


def _layer(x_prompt, x_sample, state_conv, state_delta, cache_k, cache_v, c_prompt, c_sample,
           w_ada, b_ada, w_in, conv_w, a_log, dt_bias, norm_a, sinks, w_out, ln_g, ln_b):
    bsz, seq, _ = x_prompt.shape
    n_s = x_sample.shape[0]

    w_r = _wprep(jnp.swapaxes(w_in, 0, 1))
    w_o = w_out.astype(BF16)
    alog_row = _pad_row(a_log, A_HEADS)
    dt_row = _pad_row(dt_bias, A_HEADS)
    na_row = norm_a.reshape(1, A_DV)
    g_row = ln_g.reshape(1, D_MODEL)
    b_row = ln_b.reshape(1, D_MODEL)

    assert n_s % 8 == 0 and bsz <= 8
    c_all = jnp.concatenate([c_sample, c_prompt, jnp.zeros((8 - bsz, D_MODEL), F32)], axis=0)
    mod = _ada(c_all, w_ada, b_ada.reshape(1, 3 * D_MODEL))

    cos_p, sin_p = _rope_tables(np.arange(seq))
    (q, k, v, za, gb, qb, kb, kbr, vd0, vd1, zb, conv_p, kb_last, vb_last) = _proj(
        x_prompt, mod, n_s, w_r, conv_w, alog_row, dt_row, cos_p, sin_p)
    oa, delta_p = _delta(q, k, v, gb, za, na_row)
    ob = _swa(sinks, qb, kb, kbr, vd0, vd1, zb)
    y_p = _out(oa, ob, x_prompt, mod, n_s, True, w_o, g_row, b_row, OUT_TM)
    swa_k_p = kb_last.reshape(bsz, WINDOW, B_KV_HEADS, B_HD)
    swa_v_p = vb_last.reshape(bsz, WINDOW, B_KV_HEADS, B_HD)

    cos_s, sin_s = _rope_tables(np.array([PAST_LEN]))
    xs = x_sample.reshape(n_s, D_MODEL)
    cst = jnp.transpose(state_conv, (1, 0, 2))
    sq, sk, sv, sza, sgb, sqb, skn, svn, szb, ncs = _sproj(xs, mod, w_r, conv_w, cst, alog_row, dt_row,
                                                           cos_s, sin_s)
    soa, sob, delta_s, nck, ncv = _sstep(sinks, sq, sk, sv, sgb, sza, na_row, state_delta,
                                         sqb, skn, svn, szb,
                                         jnp.swapaxes(cache_k.reshape(n_s, WINDOW, LANES), 1, 2),
                                         jnp.swapaxes(cache_v.reshape(n_s, WINDOW, LANES), 1, 2))
    y_s = _out(soa[None], sob[None], xs[None], mod, 0, False, w_o, g_row, b_row, n_s)
    conv_s = jnp.transpose(ncs, (1, 0, 2))
    unpack = lambda c: jnp.swapaxes(c, 1, 2).reshape(n_s, WINDOW, B_KV_HEADS, B_HD)
    return (y_p, y_s.reshape(n_s, 1, D_MODEL), conv_p, delta_p, swa_k_p, swa_v_p,
            conv_s, delta_s, unpack(nck), unpack(ncv))


def kernel(x_prompt, x_sample, state_conv, state_delta, cache_swa_k, cache_swa_v, c_prompt, c_sample,
           w_ada, b_ada, w_in, conv_w, a_log, dt_bias, norm_a, sinks, w_out, ln_g, ln_b):
    assert w_ada.shape[0] == DEPTH == 1
    outs = _layer(x_prompt, x_sample, state_conv[0], state_delta[0], cache_swa_k[0], cache_swa_v[0],
                  c_prompt, c_sample, w_ada[0], b_ada[0], w_in[0], conv_w[0], a_log[0], dt_bias[0],
                  norm_a[0], sinks[0], w_out[0], ln_g[0], ln_b[0])
    y_p, y_s = outs[0], outs[1]
    return (y_p, y_s) + tuple(o[None] for o in outs[2:])
```

```python
import jax
import jax.numpy as jnp
import numpy as np
from jax import lax
from jax.experimental import pallas as pl
from jax.experimental.pallas import tpu as pltpu

F32 = jnp.float32
BF16 = jnp.bfloat16

D_MODEL = 1024
DEPTH = 1
PAST_LEN = 8192
A_HEADS = 4
A_DK = 128
A_DV = 128
A_WIDTH = A_HEADS * A_DV
A_QKV = 3 * A_WIDTH
CONV_W = 4
CHUNK = 64
B_HEADS = 8
B_KV_HEADS = 2
B_HD = 64
B_GROUP = B_HEADS // B_KV_HEADS
B_WIDTH = B_HEADS * B_HD
B_KV_WIDTH = B_KV_HEADS * B_HD
WINDOW = 128
ROPE_THETA = 10000.0
MIX_WIDTH = A_WIDTH + B_WIDTH
DEEPNORM_ALPHA = (2 * DEPTH) ** 0.25
LOG2E = 1.4426950408889634
LN_EPS = 1e-5
RMS_EPS = 1e-6
L2_EPS = 1e-6

OFF_A_Z = A_QKV
OFF_A_BETA = OFF_A_Z + A_WIDTH
OFF_A_DECAY = OFF_A_BETA + A_HEADS
OFF_B_Q = OFF_A_DECAY + A_HEADS
OFF_B_K = OFF_B_Q + B_WIDTH
OFF_B_V = OFF_B_K + B_KV_WIDTH
OFF_B_Z = OFF_B_V + B_KV_WIDTH
PROJ_COLS = OFF_B_Z + B_WIDTH

LANES = 128
C_QKV = 0
C_ZA = C_QKV + A_QKV
C_QB = C_ZA + A_WIDTH
C_KB = C_QB + B_WIDTH
C_VB = C_KB + B_KV_WIDTH
C_ZB = C_VB + B_KV_WIDTH
C_BD = C_ZB + B_WIDTH
WPREP_TN = 256
W_COLS = C_BD + WPREP_TN

VMEM_LIMIT = 56 * 1024 * 1024

PROJ_TM = 512
PROJ_CW = 256
PROJ_PARTS = 4
DELTA_CT = 256
DELTA_WAVE = 2
SWA_TQ = 512
SWA_WAVE = 2
STEP_BT = 16


def _dot(a, b):
    return jnp.dot(a, b, preferred_element_type=F32)


def _dot_nt(a, b):
    return lax.dot_general(a, b, (((1,), (1,)), ((), ())), preferred_element_type=F32)


def _silu(x):
    return x * jax.nn.sigmoid(x)


def _softplus(x):
    return jnp.maximum(x, 0.0) + jnp.log1p(jnp.exp(-jnp.abs(x)))


def _lane(shape):
    return lax.broadcasted_iota(jnp.int32, shape, len(shape) - 1)


def _l2norm_heads(y, scale):
    outs = []
    for h in range(y.shape[1] // A_DK):
        xh = y[:, h * A_DK:(h + 1) * A_DK]
        ss = jnp.sum(xh * xh, axis=-1, keepdims=True)
        xn = xh * lax.rsqrt(ss + L2_EPS)
        outs.append(xn * scale if scale != 1.0 else xn)
    return jnp.concatenate(outs, axis=-1)


def _rotary_group(xg, cos, sin_signed):
    lane = _lane(xg.shape)
    swapped = jnp.where((lane % B_HD) < (B_HD // 2),
                        pltpu.roll(xg, LANES - B_HD // 2, axis=1),
                        pltpu.roll(xg, B_HD // 2, axis=1))
    return xg * cos + swapped * sin_signed


def _kv_layouts(kb, vb):
    low = _lane(kb.shape) < B_HD
    kbr = pltpu.roll(kb, B_HD, axis=1)
    vbr = pltpu.roll(vb, B_HD, axis=1)
    return kb, kbr, jnp.where(low, vb, vbr), jnp.where(low, vbr, vb)


def _gate_lanes(bd, alog_row, dt_row):
    lane = _lane(bd.shape)
    g = -jnp.exp(alog_row) * _softplus(bd + dt_row)
    return jnp.where(lane < A_HEADS, jax.nn.sigmoid(bd), g)


def _layer_norm(r, g, b):
    mu = jnp.mean(r, axis=-1, keepdims=True)
    d = r - mu
    var = jnp.mean(d * d, axis=-1, keepdims=True)
    return d * lax.rsqrt(var + LN_EPS) * g + b


def _wprep_kernel(wt_ref, o_ref):
    x = wt_ref[...]
    tail = pl.program_id(0) == pl.num_programs(0) - 1
    row = lax.broadcasted_iota(jnp.int32, x.shape, 0)
    x = jnp.where(jnp.logical_and(tail, row >= 2 * A_HEADS), 0.0, x)
    o_ref[...] = x.T.astype(BF16)


def _wprep(w_t):
    tn = WPREP_TN
    n_a, n_b = OFF_A_BETA // tn, (PROJ_COLS - OFF_B_Q) // tn
    assert n_a * tn == OFF_A_BETA and n_b * tn == PROJ_COLS - OFF_B_Q and OFF_A_BETA + tn <= PROJ_COLS

    def src_row(j):
        return jnp.where(j < n_a, j * tn, jnp.where(j < n_a + n_b, OFF_B_Q + (j - n_a) * tn, OFF_A_BETA))

    return pl.pallas_call(
        _wprep_kernel,
        grid=(n_a + n_b + 1,),
        in_specs=[pl.BlockSpec((pl.Element(tn), pl.Element(D_MODEL)),
                               lambda j: (pl.multiple_of(src_row(j), 8), 0))],
        out_specs=pl.BlockSpec((D_MODEL, tn), lambda j: (0, j)),
        out_shape=jax.ShapeDtypeStruct((D_MODEL, W_COLS), BF16),
        compiler_params=pltpu.CompilerParams(dimension_semantics=("arbitrary",),
                                             vmem_limit_bytes=VMEM_LIMIT),
        name="wprep",
    )(w_t)


def _ada_kernel(c_ref, w_ref, b_ref, o_ref):
    o_ref[...] = _dot(c_ref[...].astype(BF16), w_ref[...].astype(BF16)) + b_ref[...]


def _ada(c_all, w_ada, b_ada):
    rows = c_all.shape[0]
    tn = 768
    return pl.pallas_call(
        _ada_kernel,
        grid=(3 * D_MODEL // tn,),
        in_specs=[pl.BlockSpec((rows, D_MODEL), lambda j: (0, 0)),
                  pl.BlockSpec((D_MODEL, tn), lambda j: (0, j)),
                  pl.BlockSpec((1, tn), lambda j: (0, j))],
        out_specs=pl.BlockSpec((rows, tn), lambda j: (0, j)),
        out_shape=jax.ShapeDtypeStruct((rows, 3 * D_MODEL), F32),
        compiler_params=pltpu.CompilerParams(dimension_semantics=("arbitrary",),
                                             vmem_limit_bytes=VMEM_LIMIT),
        name="ada",
    )(c_all, w_ada, b_ada)


def _proj_kernel(x_ref, mod_ref, w_ref, cw_ref, alog_ref, dt_ref, cos_ref, sin_ref,
                 q_ref, k_ref, v_ref, za_ref, gb_ref, qb_ref, kb_ref, kbr_ref, vd0_ref, vd1_ref,
                 zb_ref, cst_ref, kbl_ref, vbl_ref, ubuf):
    tm = x_ref.shape[1]
    t = pl.program_id(1)

    @pl.when(t == 0)
    def _():
        ubuf[...] = jnp.zeros(ubuf.shape, F32)

    brow = pl.ds(pl.program_id(0), 1)
    shift = mod_ref[brow, 0:D_MODEL]
    scale = mod_ref[brow, D_MODEL:2 * D_MODEL]

    rp = tm // PROJ_PARTS
    cw = PROJ_CW
    sub = lax.broadcasted_iota(jnp.int32, (rp // 8, 8, cw), 1)
    pieces = [slice(c0, c0 + cw) for c0 in range(0, A_QKV, cw)]
    outs = (q_ref, k_ref, v_ref)

    def part(r0):
        rs = slice(r0, r0 + rp)
        h = (x_ref[0, rs, :] * (1.0 + scale) + shift).astype(BF16)

        def conv_epilogue(cs, u):
            gi, c_in = cs.start // A_WIDTH, cs.start % A_WIDTH
            groups = jnp.concatenate([ubuf[:, cs], u], axis=0).reshape(rp // 8 + 1, 8, cw)
            acc = None
            for j in range(CONV_W - 1, 0, -1):
                rot = pltpu.roll(groups, j, axis=1)
                term = (jnp.where(sub < j, rot[:-1], rot[1:]).reshape(rp, cw)
                        * cw_ref[CONV_W - 1 - j:CONV_W - j, cs])
                acc = term if acc is None else acc + term
            y = _silu(acc + u * cw_ref[CONV_W - 1:CONV_W, cs])
            if gi == 0:
                y = _l2norm_heads(y, A_DK ** -0.5)
            elif gi == 1:
                y = _l2norm_heads(y, 1.0)
            outs[gi][0, rs, c_in:c_in + cw] = y
            ubuf[:, cs] = u[rp - 8:rp]
            if r0 + rp == tm:
                cst_ref[0, :, cs] = u[rp - (CONV_W - 1):rp]

        pending = _dot(h, w_ref[:, pieces[0]])
        for i, cs in enumerate(pieces):
            u = pending
            if i + 1 < len(pieces):
                pending = _dot(h, w_ref[:, pieces[i + 1]])
            conv_epilogue(cs, u)

        za_ref[0, rs, :] = _dot(h, w_ref[:, C_ZA:C_ZA + A_WIDTH])
        gb_ref[0, rs, :] = _gate_lanes(_dot(h, w_ref[:, C_BD:C_BD + LANES]), alog_ref[...], dt_ref[...])
        cos = cos_ref[rs, :]
        sin = sin_ref[rs, :]
        uq = _dot(h, w_ref[:, C_QB:C_QB + B_WIDTH])
        for g in range(B_WIDTH // LANES):
            qb_ref[0, rs, g * LANES:(g + 1) * LANES] = (
                _rotary_group(uq[:, g * LANES:(g + 1) * LANES], cos, sin) * (B_HD ** -0.5 * LOG2E)).astype(BF16)
        ukv = _dot(h, w_ref[:, C_KB:C_KB + 2 * LANES])
        kb = _rotary_group(ukv[:, 0:LANES], cos, sin)
        vb = ukv[:, LANES:2 * LANES]
        for o_ref, val in zip((kb_ref, kbr_ref, vd0_ref, vd1_ref), _kv_layouts(kb, vb)):
            o_ref[0, rs, :] = val.astype(BF16)
        zb_ref[0, rs, :] = _dot(h, w_ref[:, C_ZB:C_ZB + B_WIDTH])
        return kb, vb

    for r0 in range(0, tm, rp):
        kb, vb = part(r0)

    @pl.when(t == pl.num_programs(1) - 1)
    def _():
        kbl_ref[0] = kb[rp - WINDOW:rp]
        vbl_ref[0] = vb[rp - WINDOW:rp]


def _proj(x, mod, mod_row0, w_r, conv_w, alog_row, dt_row, cos_t, sin_t):
    bsz, t, _ = x.shape
    tm = PROJ_TM
    row = lambda w: pl.BlockSpec((1, tm, w), lambda b, i: (b, i, 0))
    const2 = lambda s: pl.BlockSpec(s, lambda b, i: (0, 0))
    per_b = lambda r, w: pl.BlockSpec((1, r, w), lambda b, i: (b, 0, 0))
    wide = lambda w, dt=F32: jax.ShapeDtypeStruct((bsz, t, w), dt)
    return pl.pallas_call(
        _proj_kernel,
        grid=(bsz, t // tm),
        in_specs=[row(D_MODEL),
                  pl.BlockSpec((8, 3 * D_MODEL), lambda b, i: (mod_row0 // 8, 0)),
                  const2((D_MODEL, W_COLS)),
                  const2((CONV_W, A_QKV)),
                  const2((1, LANES)), const2((1, LANES)),
                  pl.BlockSpec((tm, LANES), lambda b, i: (i, 0)),
                  pl.BlockSpec((tm, LANES), lambda b, i: (i, 0))],
        out_specs=[row(A_WIDTH), row(A_WIDTH), row(A_WIDTH), row(A_WIDTH), row(LANES),
                   row(B_WIDTH), row(LANES), row(LANES), row(LANES), row(LANES), row(B_WIDTH),
                   per_b(CONV_W - 1, A_QKV), per_b(WINDOW, LANES), per_b(WINDOW, LANES)],
        out_shape=[wide(A_WIDTH), wide(A_WIDTH), wide(A_WIDTH), wide(A_WIDTH), wide(LANES),
                   wide(B_WIDTH, BF16), wide(LANES, BF16), wide(LANES, BF16), wide(LANES, BF16),
                   wide(LANES, BF16), wide(B_WIDTH),
                   jax.ShapeDtypeStruct((bsz, CONV_W - 1, A_QKV), F32),
                   jax.ShapeDtypeStruct((bsz, WINDOW, LANES), F32),
                   jax.ShapeDtypeStruct((bsz, WINDOW, LANES), F32)],
        scratch_shapes=[pltpu.VMEM((8, A_QKV), F32)],
        compiler_params=pltpu.CompilerParams(dimension_semantics=("arbitrary", "arbitrary"),
                                             vmem_limit_bytes=VMEM_LIMIT),
        name="proj",
    )(x, mod, w_r, conv_w, alog_row, dt_row, cos_t, sin_t)


def _delta_kernel(q_ref, k_ref, v_ref, gb_ref, za_ref, na_ref, oa_ref, st_ref,
                  s_scr, wq_s, ut_s, akd_s, gl_s):
    bsz, ct = q_ref.shape[0], q_ref.shape[1]
    nch = ct // CHUNK
    t = pl.program_id(0)
    wslot = t % 2
    rslot = 1 - wslot

    @pl.when(t == 0)
    def _():
        s_scr[...] = jnp.zeros(s_scr.shape, F32)
        wq_s[...] = jnp.zeros(wq_s.shape, BF16)
        ut_s[...] = jnp.zeros(ut_s.shape, F32)
        akd_s[...] = jnp.zeros(akd_s.shape, BF16)
        gl_s[...] = jnp.zeros(gl_s.shape, F32)

    units = [(b, c, h) for b in range(bsz) for c in range(nch) for h in range(A_HEADS)]
    uid = {u_: i for i, u_ in enumerate(units)}
    rows = lambda c: slice(c * CHUNK, (c + 1) * CHUNK)
    lanes = lambda h: slice(h * A_DK, (h + 1) * A_DK)
    na = na_ref[...]

    s_cur = {(b, h): s_scr[b * A_HEADS + h] for b in range(bsz) for h in range(A_HEADS)}
    ws, uu = {}, {}

    def rec_ws(c):
        for b in range(bsz):
            for h in range(A_HEADS):
                i = uid[b, c, h]
                ws[b, h] = _dot(wq_s[rslot, i], s_cur[b, h].astype(BF16))
                uu[b, h] = (ut_s[rslot, i] - ws[b, h][:CHUNK]).astype(BF16)

    def rec_ou(c):
        zpad = jnp.zeros((CHUNK, A_DV), BF16)
        for b in range(bsz):
            u_bd = jnp.concatenate(
                [jnp.concatenate([uu[b, h] if hh == h else zpad for hh in range(A_HEADS)], axis=-1)
                 for h in range(A_HEADS)], axis=0)
            ou = _dot(akd_s[rslot, b * nch + c], u_bd)
            for h in range(A_HEADS):
                o = ws[b, h][CHUNK:] + ou[:CHUNK, lanes(h)]
                s_cur[b, h] = gl_s[rslot, uid[b, c, h]] * s_cur[b, h] + ou[CHUNK:, lanes(h)]
                on = o * lax.rsqrt(jnp.mean(o * o, axis=-1, keepdims=True) + RMS_EPS) * na
                oa_ref[b, rows(c), lanes(h)] = (on * _silu(za_ref[b, rows(c), lanes(h)])).astype(BF16)

    rec_stages = []
    for c in range(nch):
        rec_stages += [lambda c=c: rec_ws(c), lambda c=c: rec_ou(c)]

    def run_rec(n_left_after):
        while rec_stages and len(rec_stages) > n_left_after:
            rec_stages.pop(0)()

    pk = A_HEADS * CHUNK
    low = _lane((CHUNK, LANES)) < CHUNK
    low_row = _lane((1, LANES)) < CHUNK
    ti_p = lax.broadcasted_iota(jnp.int32, (CHUNK, pk), 0)
    ii_p = _lane((CHUNK, pk)) % CHUNK
    zero64 = jnp.zeros((CHUNK, LANES), BF16)

    def pack(parts):
        return jnp.concatenate([jnp.where(low, parts[0], parts[1]), jnp.where(low, parts[2], parts[3])], axis=-1)

    def block_diag(x16):
        blocks = []
        for h in range(A_HEADS):
            pair, first = h // 2, h % 2 == 0
            piece = jnp.where(low if first else jnp.logical_not(low), x16[:, pair * LANES:(pair + 1) * LANES], zero64)
            blocks.append(jnp.concatenate([piece, zero64] if pair == 0 else [zero64, piece], axis=-1))
        return jnp.concatenate(blocks, axis=0)

    zrhs = jnp.zeros((CHUNK, 2 * A_DK), BF16)
    n_rec = len(rec_stages)
    n_slots = 8 * (bsz // DELTA_WAVE)
    done = [0]

    def stage_done():
        done[0] += 1
        run_rec(n_rec - (done[0] * n_rec) // n_slots)

    def decay_terms(b, beta, g_col, g_last, eg, dec_p, beta_p):
        gbv = gb_ref[b]
        rin = lax.broadcasted_iota(jnp.int32, gbv.shape, 0) % CHUNK
        gcs = gbv
        s = 1
        while s < CHUNK:
            gcs = gcs + jnp.where(rin >= s, pltpu.roll(gcs, s, axis=0), 0.0)
            s *= 2
        gcs_t = gcs.T
        for c in range(nch):
            r0 = c * CHUNK
            pair_lanes = slice((c // 2) * LANES, (c // 2 + 1) * LANES)
            g_rows = []
            for h in range(A_HEADS):
                u_ = (b, c, h)
                beta[u_] = jnp.broadcast_to(gbv[rows(c), h:h + 1], (CHUNK, A_DK))
                g_col[u_] = jnp.broadcast_to(gcs[rows(c), A_HEADS + h:A_HEADS + h + 1], (CHUNK, A_DK))
                g_last[u_] = gcs[r0 + CHUNK - 1:r0 + CHUNK, A_HEADS + h:A_HEADS + h + 1]
                eg[u_] = jnp.exp(g_col[u_])
                g_row = gcs_t[A_HEADS + h:A_HEADS + h + 1, pair_lanes]
                g_rows.append(g_row if c % 2 == h % 2 else pltpu.roll(g_row, CHUNK, axis=1))
            g_row_p = jnp.concatenate([jnp.where(low_row, g_rows[0], g_rows[1]),
                                       jnp.where(low_row, g_rows[2], g_rows[3])], axis=-1)
            g_col_p = pack([g_col[b, c, h] for h in range(A_HEADS)])
            dec_p[b, c] = jnp.exp(jnp.where(ti_p >= ii_p, g_col_p - g_row_p, -jnp.inf))
            beta_p[b, c] = pack([beta[b, c, h] for h in range(A_HEADS)])

    def prepare(bs):
        groups_b = [(b, c) for b in bs for c in range(nch)]
        beta, g_col, g_last, eg, dec_p, beta_p = {}, {}, {}, {}, {}, {}
        for b in bs:
            decay_terms(b, beta, g_col, g_last, eg, dec_p, beta_p)

        nmat = {}
        for (b, c) in groups_b:
            k16 = k_ref[b, rows(c), :].astype(BF16)
            q16 = q_ref[b, rows(c), :].astype(BF16)
            k_heads = jnp.concatenate(
                [jnp.concatenate([k16[:, lanes(h)] if hh == h else zero64 for hh in range(A_HEADS)], axis=-1)
                 for h in range(A_HEADS)], axis=0)
            kq = _dot_nt(jnp.concatenate([k16, q16], axis=0), k_heads)
            nmat[b, c] = -(beta_p[b, c] * kq[:CHUNK] * jnp.where(ti_p > ii_p, dec_p[b, c], 0.0))
            akd_s[wslot, b * nch + c, 0:CHUNK, :] = (kq[CHUNK:] * dec_p[b, c]).astype(BF16)
        stage_done()

        rsum = dict(nmat)
        pw16 = {g_: nmat[g_].astype(BF16) for g_ in groups_b}
        pw = {g_: _dot(pw16[g_], block_diag(pw16[g_])) for g_ in groups_b}
        stage_done()
        for step in range(1, 6):
            last = step == 5
            pw16 = {g_: pw[g_].astype(BF16) for g_ in groups_b}
            rp = {}
            for g_ in groups_b:
                r16 = rsum[g_].astype(BF16)
                rp[g_] = _dot(r16 if last else jnp.concatenate([r16, pw16[g_]], axis=0), block_diag(pw16[g_]))
            for g_ in groups_b:
                rsum[g_] = rsum[g_] + pw[g_] + rp[g_][:CHUNK]
                if not last:
                    pw[g_] = rp[g_][CHUNK:]
            stage_done()

        for (b, c) in groups_b:
            for h in range(A_HEADS):
                u_ = (b, c, h)
                i = uid[u_]
                kc = k_ref[b, rows(c), lanes(h)]
                rhs = jnp.concatenate([(beta[u_] * eg[u_]) * kc, beta[u_] * v_ref[b, rows(c), lanes(h)]],
                                      axis=-1)
                rhs16 = rhs.astype(BF16)
                rhs_rows = jnp.concatenate([rhs16 if hh == h else zrhs for hh in range(A_HEADS)], axis=0)
                sol = rhs + _dot(rsum[b, c].astype(BF16), rhs_rows)
                wq_s[wslot, i] = jnp.concatenate([sol[:, :A_DK], eg[u_] * q_ref[b, rows(c), lanes(h)]],
                                                 axis=0).astype(BF16)
                ut_s[wslot, i] = sol[:, A_DK:]
                gl_s[wslot, i] = jnp.broadcast_to(jnp.exp(g_last[u_]), (1, A_DV))
        for (b, c) in groups_b:
            kd = [jnp.exp(g_last[b, c, h] - g_col[b, c, h]) * k_ref[b, rows(c), lanes(h)]
                  for h in range(A_HEADS)]
            for p in range(A_HEADS // 2):
                akd_s[wslot, b * nch + c, CHUNK:, p * LANES:(p + 1) * LANES] = (
                    jnp.concatenate([kd[2 * p], kd[2 * p + 1]], axis=0).T.astype(BF16))
        stage_done()

    for b0 in range(0, bsz, DELTA_WAVE):
        prepare(range(b0, b0 + DELTA_WAVE))
    run_rec(0)

    for b in range(bsz):
        for h in range(A_HEADS):
            s_scr[b * A_HEADS + h] = s_cur[b, h]

    @pl.when(t == pl.num_programs(0) - 1)
    def _():
        for b in range(bsz):
            for h in range(A_HEADS):
                st_ref[b, h] = s_cur[b, h]


def _delta(q, k, v, gb, za, na_row):
    bsz, t, _ = q.shape
    ct = DELTA_CT
    nt = t // ct
    n_units = bsz * (ct // CHUNK) * A_HEADS
    prep = lambda w: pl.BlockSpec((bsz, ct, w), lambda i: (0, jnp.minimum(i, nt - 1), 0))
    rec = lambda w: pl.BlockSpec((bsz, ct, w), lambda i: (0, jnp.maximum(i - 1, 0), 0))
    return pl.pallas_call(
        _delta_kernel,
        grid=(nt + 1,),
        in_specs=[prep(A_WIDTH), prep(A_WIDTH), prep(A_WIDTH), prep(LANES), rec(A_WIDTH),
                  pl.BlockSpec((1, A_DV), lambda i: (0, 0))],
        out_specs=[rec(A_WIDTH),
                   pl.BlockSpec((bsz, A_HEADS, A_DK, A_DV), lambda i: (0, 0, 0, 0))],
        out_shape=[jax.ShapeDtypeStruct((bsz, t, A_WIDTH), BF16),
                   jax.ShapeDtypeStruct((bsz, A_HEADS, A_DK, A_DV), F32)],
        scratch_shapes=[pltpu.VMEM((bsz * A_HEADS, A_DK, A_DV), F32),
                        pltpu.VMEM((2, n_units, 2 * CHUNK, A_DK), BF16),
                        pltpu.VMEM((2, n_units, CHUNK, A_DV), F32),
                        pltpu.VMEM((2, n_units // A_HEADS, CHUNK + A_DK, A_HEADS * CHUNK), BF16),
                        pltpu.VMEM((2, n_units, 1, A_DV), F32)],
        compiler_params=pltpu.CompilerParams(dimension_semantics=("arbitrary",),
                                             vmem_limit_bytes=VMEM_LIMIT),
        name="delta",
    )(q, k, v, gb, za, na_row)


def _swa_out_kernel(sink_ref, qb_ref, kc_ref, kp_ref, krc_ref, krp_ref, v0c_ref, v0p_ref, v1c_ref, v1p_ref,
                    zb_ref, oa_ref, x_ref, gate_ref, w_ref, g_ref, b_ref, y_ref):
    n = pl.program_id(1)
    gate = gate_ref[pl.ds(pl.program_id(0), 1), :]
    tq = qb_ref.shape[1]
    blk = WINDOW
    kx = (jnp.concatenate([kp_ref[0], kc_ref[0]], axis=0), jnp.concatenate([krp_ref[0], krc_ref[0]], axis=0))
    vd = (jnp.concatenate([v0p_ref[0], v0c_ref[0]], axis=0), jnp.concatenate([v1p_ref[0], v1c_ref[0]], axis=0))

    a = lax.broadcasted_iota(jnp.int32, (2 * blk, 2 * blk), 0) % blk
    j = lax.broadcasted_iota(jnp.int32, (2 * blk, 2 * blk), 1)
    rel = a + blk - j
    band = (rel >= 0) & (rel <= WINDOW)
    band_first = band & ((n > 0) | (j >= blk))
    top = lax.broadcasted_iota(jnp.int32, (2 * blk, 1), 0) < blk
    low = _lane((blk, LANES)) < B_HD
    zero = jnp.zeros((blk, LANES), BF16)

    qrows = lambda i: slice(i * blk, (i + 1) * blk)
    krows = lambda i: slice(i * blk, (i + 2) * blk)
    sink = {(kh, half): jnp.where(top, sink_ref[kh * B_GROUP + half] * LOG2E,
                                  sink_ref[kh * B_GROUP + half + 2] * LOG2E)
            for kh in range(B_KV_HEADS) for half in range(2)}
    for i0 in range(0, tq // blk, SWA_WAVE):
        blocks = range(i0, i0 + SWA_WAVE)
        units = [(i, kh, half) for i in blocks for kh in range(B_KV_HEADS) for half in range(2)]
        mix_a = {i: _dot(oa_ref[0, qrows(i), :], w_ref[0:A_WIDTH, :]) for i in blocks}
        sc = {}
        for (i, kh, half) in units:
            qs = []
            for g in range(2):
                grp = kh * 2 + g
                xg = qb_ref[0, qrows(i), grp * LANES:(grp + 1) * LANES]
                qs.append(jnp.where(low if half == 0 else jnp.logical_not(low), xg, zero))
            qz = jnp.concatenate(qs, axis=0)
            sc[i, kh, half] = _dot_nt(qz, kx[0 if kh == half else 1][krows(i)])
        p, den = {}, {}
        for u_ in units:
            i, kh, half = u_
            s_m = jnp.where(band_first if i == 0 else band, sc[u_], -jnp.inf)
            m = jnp.maximum(jnp.max(s_m, axis=-1, keepdims=True), sink[kh, half])
            e = jnp.exp2(s_m - m)
            den[u_] = jnp.sum(e, axis=-1, keepdims=True) + jnp.exp2(sink[kh, half] - m)
            p[u_] = e.astype(BF16)
        pv = {u_: _dot(p[u_], vd[u_[1]][krows(u_[0])]) for u_ in units}
        outs = {u_: pv[u_] / den[u_] for u_ in units}
        for i in blocks:
            ob = []
            for grp in range(B_WIDTH // LANES):
                kh, g = grp // 2, grp % 2
                og = jnp.where(low, outs[i, kh, 0][g * blk:(g + 1) * blk], outs[i, kh, 1][g * blk:(g + 1) * blk])
                ob.append((og * _silu(zb_ref[0, qrows(i), grp * LANES:(grp + 1) * LANES])).astype(BF16))
            mix = mix_a[i] + _dot(jnp.concatenate(ob, axis=-1), w_ref[A_WIDTH:MIX_WIDTH, :])
            r = DEEPNORM_ALPHA * x_ref[0, qrows(i), :] + (1.0 + gate) * mix
            y_ref[0, qrows(i), :] = _layer_norm(r, g_ref[...], b_ref[...])


def _swa_out(sinks, qb, kb, kbr, vd0, vd1, zb, oa, x, mod, mod_row0, w_out, ln_g, ln_b):
    bsz, t, _ = qb.shape
    tq = SWA_TQ
    per = tq // WINDOW
    cur = lambda w: pl.BlockSpec((1, tq, w), lambda b, i: (b, i, 0))
    prev = pl.BlockSpec((1, WINDOW, LANES), lambda b, i: (b, jnp.maximum(i * per - 1, 0), 0))
    const2 = lambda s: pl.BlockSpec(s, lambda b, i: (0, 0))
    return pl.pallas_call(
        _swa_out_kernel,
        grid=(bsz, t // tq),
        in_specs=[pl.BlockSpec(memory_space=pltpu.SMEM), cur(B_WIDTH),
                  cur(LANES), prev, cur(LANES), prev, cur(LANES), prev, cur(LANES), prev,
                  cur(B_WIDTH), cur(A_WIDTH), cur(D_MODEL),
                  pl.BlockSpec((8, D_MODEL), lambda b, i: (mod_row0 // 8, 2)),
                  const2((MIX_WIDTH, D_MODEL)), const2((1, D_MODEL)), const2((1, D_MODEL))],
        out_specs=cur(D_MODEL),
        out_shape=jax.ShapeDtypeStruct((bsz, t, D_MODEL), F32),
        compiler_params=pltpu.CompilerParams(dimension_semantics=("arbitrary", "arbitrary"),
                                             vmem_limit_bytes=VMEM_LIMIT),
        name="swa_out",
    )(sinks, qb, kb, kb, kbr, kbr, vd0, vd0, vd1, vd1, zb, oa, x, mod, w_out, ln_g, ln_b)


def _out_kernel(oa_ref, ob_ref, x_ref, gate_ref, w_ref, g_ref, b_ref, y_ref):
    mix = _dot(oa_ref[...], w_ref[0:A_WIDTH, :]) + _dot(ob_ref[...], w_ref[A_WIDTH:MIX_WIDTH, :])
    r = DEEPNORM_ALPHA * x_ref[...] + (1.0 + gate_ref[...]) * mix
    y_ref[...] = _layer_norm(r, g_ref[...], b_ref[...])


def _out(oa, ob, x, mod, w_out, ln_g, ln_b):
    n = x.shape[0]
    full = lambda s: pl.BlockSpec(s, lambda i: (0, 0))
    return pl.pallas_call(
        _out_kernel,
        grid=(1,),
        in_specs=[full((n, A_WIDTH)), full((n, B_WIDTH)), full((n, D_MODEL)),
                  pl.BlockSpec((n, D_MODEL), lambda i: (0, 2)),
                  full((MIX_WIDTH, D_MODEL)), full((1, D_MODEL)), full((1, D_MODEL))],
        out_specs=full((n, D_MODEL)),
        out_shape=jax.ShapeDtypeStruct((n, D_MODEL), F32),
        compiler_params=pltpu.CompilerParams(dimension_semantics=("arbitrary",),
                                             vmem_limit_bytes=VMEM_LIMIT),
        name="out",
    )(oa, ob, x, mod, w_out, ln_g, ln_b)


def _sproj_kernel(x_ref, mod_ref, w_ref, cw_ref, cst_ref, alog_ref, dt_ref, cos_ref, sin_ref,
                  q_ref, k_ref, v_ref, za_ref, gb_ref, qb_ref, kb_ref, vb_ref, zb_ref, ncs_ref):
    shift = mod_ref[:, 0:D_MODEL]
    scale = mod_ref[:, D_MODEL:2 * D_MODEL]
    h = (x_ref[...] * (1.0 + scale) + shift).astype(BF16)

    for gi, o_ref in enumerate((q_ref, k_ref, v_ref)):
        c0 = gi * A_WIDTH
        cs = slice(c0, c0 + A_WIDTH)
        u = _dot(h, w_ref[:, cs])
        acc = cst_ref[0, :, cs] * cw_ref[0:1, cs]
        acc = acc + cst_ref[1, :, cs] * cw_ref[1:2, cs]
        acc = acc + cst_ref[2, :, cs] * cw_ref[2:3, cs]
        acc = acc + u * cw_ref[3:4, cs]
        y = _silu(acc)
        if gi == 0:
            y = _l2norm_heads(y, A_DK ** -0.5)
        elif gi == 1:
            y = _l2norm_heads(y, 1.0)
        o_ref[...] = y
        ncs_ref[0, :, cs] = cst_ref[1, :, cs]
        ncs_ref[1, :, cs] = cst_ref[2, :, cs]
        ncs_ref[2, :, cs] = u

    za_ref[...] = _dot(h, w_ref[:, C_ZA:C_ZA + A_WIDTH])
    gb_ref[...] = _gate_lanes(_dot(h, w_ref[:, C_BD:C_BD + LANES]), alog_ref[...], dt_ref[...])

    cos = cos_ref[...]
    sin = sin_ref[...]
    uq = _dot(h, w_ref[:, C_QB:C_QB + B_WIDTH])
    for g in range(B_WIDTH // LANES):
        qb_ref[:, g * LANES:(g + 1) * LANES] = (
            _rotary_group(uq[:, g * LANES:(g + 1) * LANES], cos, sin) * (B_HD ** -0.5))
    kb_ref[...] = _rotary_group(_dot(h, w_ref[:, C_KB:C_KB + LANES]), cos, sin)
    vb_ref[...] = _dot(h, w_ref[:, C_VB:C_VB + LANES])
    zb_ref[...] = _dot(h, w_ref[:, C_ZB:C_ZB + B_WIDTH])


def _sproj(x, mod_s, w_r, conv_w, cst, alog_row, dt_row, cos_row, sin_row):
    n = x.shape[0]
    full = lambda s: pl.BlockSpec(s, lambda i: (0,) * len(s))
    wide = lambda w: jax.ShapeDtypeStruct((n, w), F32)
    return pl.pallas_call(
        _sproj_kernel,
        grid=(1,),
        in_specs=[full((n, D_MODEL)), pl.BlockSpec((n, 3 * D_MODEL), lambda i: (0, 0)),
                  full((D_MODEL, W_COLS)),
                  full((CONV_W, A_QKV)), full((CONV_W - 1, n, A_QKV)),
                  full((1, LANES)), full((1, LANES)), full((1, LANES)), full((1, LANES))],
        out_specs=[full((n, A_WIDTH)), full((n, A_WIDTH)), full((n, A_WIDTH)), full((n, A_WIDTH)),
                   full((n, LANES)), full((n, B_WIDTH)), full((n, LANES)), full((n, LANES)),
                   full((n, B_WIDTH)), full((CONV_W - 1, n, A_QKV))],
        out_shape=[wide(A_WIDTH), wide(A_WIDTH), wide(A_WIDTH), wide(A_WIDTH), wide(LANES),
                   wide(B_WIDTH), wide(LANES), wide(LANES), wide(B_WIDTH),
                   jax.ShapeDtypeStruct((CONV_W - 1, n, A_QKV), F32)],
        compiler_params=pltpu.CompilerParams(dimension_semantics=("arbitrary",),
                                             vmem_limit_bytes=VMEM_LIMIT),
        name="sproj",
    )(x, mod_s, w_r, conv_w, cst, alog_row, dt_row, cos_row, sin_row)


def _sstep_kernel(sink_ref, q_ref, k_ref, v_ref, gb_ref, za_ref, na_ref, st_ref,
                  qb_ref, kn_ref, vn_ref, zb_ref, ck_ref, cv_ref,
                  oa_ref, ob_ref, nst_ref, nck_ref, ncv_ref,
                  o_scr, ob_scr):
    bt = q_ref.shape[0]
    gbv = gb_ref[...]

    pick = (lax.broadcasted_iota(jnp.int32, (bt, bt * A_DV), 1) // A_DV
            == lax.broadcasted_iota(jnp.int32, (bt, bt * A_DV), 0))
    pick = jnp.where(pick, 1.0, 0.0).astype(BF16)
    for h in range(A_HEADS):
        hs = slice(h * A_DK, (h + 1) * A_DK)
        q_rep = _dot(q_ref[:, hs].T.astype(BF16), pick)
        k_rep = _dot(k_ref[:, hs].T.astype(BF16), pick)
        for bb in range(bt):
            eg = jnp.exp(gbv[bb:bb + 1, A_HEADS + h:A_HEADS + h + 1])
            beta = gbv[bb:bb + 1, h:h + 1]
            kcol = k_rep[:, bb * A_DV:(bb + 1) * A_DV]
            qcol = q_rep[:, bb * A_DV:(bb + 1) * A_DV]
            s1 = eg * st_ref[bb, h]
            pred = jnp.sum(kcol * s1, axis=0, keepdims=True)
            upd = beta * (v_ref[bb:bb + 1, hs] - pred)
            s2 = s1 + kcol * upd
            nst_ref[bb, h] = s2
            o_scr[bb:bb + 1, hs] = jnp.sum(qcol * s2, axis=0, keepdims=True)
    na = na_ref[...]
    for h in range(A_HEADS):
        hs = slice(h * A_DK, (h + 1) * A_DK)
        o = o_scr[:, hs]
        on = o * lax.rsqrt(jnp.mean(o * o, axis=-1, keepdims=True) + RMS_EPS) * na
        oa_ref[:, hs] = (on * _silu(za_ref[:, hs])).astype(BF16)

    row8 = lax.broadcasted_iota(jnp.int32, (B_HEADS, LANES), 0)
    lane8 = _lane((B_HEADS, LANES))
    own_half = (lane8 >= B_HD) == (row8 >= B_GROUP)
    rcol = lax.broadcasted_iota(jnp.int32, (B_HEADS, 1), 0)
    sink = jnp.zeros((B_HEADS, 1), F32)
    for r in range(B_HEADS):
        sink = jnp.where(rcol == r, sink_ref[r], sink)
    qv = qb_ref[...]
    qv_r = jnp.concatenate([pltpu.roll(qv[:, g * LANES:(g + 1) * LANES], B_HD, axis=1)
                            for g in range(B_WIDTH // LANES)], axis=-1)
    kn_t = kn_ref[...].T
    vn_t = vn_ref[...].T
    newest = _lane((LANES, WINDOW)) == WINDOW - 1
    qzs, scs = [], []
    for bb in range(bt):
        qz = jnp.zeros((B_HEADS, LANES), F32)
        for r in range(B_HEADS):
            grp, half, kh = r // 2, r % 2, r // B_GROUP
            src = qv if half == kh else qv_r
            qz = jnp.where(row8 == r, src[bb:bb + 1, grp * LANES:(grp + 1) * LANES], qz)
        qzs.append(jnp.where(own_half, qz, 0.0))
    for bb in range(bt):
        scs.append(_dot(qzs[bb], ck_ref[bb]))
    ps, pnews, dens = [], [], []
    for bb in range(bt):
        sc_new = jnp.sum(qzs[bb] * kn_ref[bb:bb + 1, :], axis=-1, keepdims=True)
        m = jnp.maximum(jnp.maximum(jnp.max(scs[bb], axis=-1, keepdims=True), sc_new), sink)
        p = jnp.exp(scs[bb] - m)
        p_new = jnp.exp(sc_new - m)
        ps.append(p)
        pnews.append(p_new)
        dens.append(jnp.sum(p, axis=-1, keepdims=True) + p_new + jnp.exp(sink - m))
    pvs = [_dot_nt(ps[bb], cv_ref[bb]) for bb in range(bt)]
    for bb in range(bt):
        o = (pvs[bb] + pnews[bb] * vn_ref[bb:bb + 1, :]) / dens[bb]
        o = jnp.where(own_half, o, 0.0)
        ob_scr[bb * B_HEADS:(bb + 1) * B_HEADS, :] = o + pltpu.roll(o, B_HD, axis=1)
    for bb in range(bt):
        nck_ref[bb] = jnp.where(newest, kn_t[:, bb:bb + 1], pltpu.roll(ck_ref[bb], WINDOW - 1, axis=1))
        ncv_ref[bb] = jnp.where(newest, vn_t[:, bb:bb + 1], pltpu.roll(cv_ref[bb], WINDOW - 1, axis=1))
    low = _lane((bt, LANES)) < B_HD
    for grp in range(B_WIDTH // LANES):
        even = ob_scr[pl.ds(2 * grp, bt, stride=B_HEADS), :]
        odd = ob_scr[pl.ds(2 * grp + 1, bt, stride=B_HEADS), :]
        gs = slice(grp * LANES, (grp + 1) * LANES)
        ob_ref[:, gs] = (jnp.where(low, even, odd) * _silu(zb_ref[:, gs])).astype(BF16)


def _sstep(sinks, q, k, v, gb, za, na_row, state, qb, kn, vn, zb, ck, cv):
    n = q.shape[0]
    bt = STEP_BT
    row = lambda w: pl.BlockSpec((bt, w), lambda i: (i, 0))
    st_spec = pl.BlockSpec((bt, A_HEADS, A_DK, A_DV), lambda i: (i, 0, 0, 0))
    c_spec = pl.BlockSpec((bt, WINDOW, LANES), lambda i: (i, 0, 0))
    return pl.pallas_call(
        _sstep_kernel,
        grid=(n // bt,),
        in_specs=[pl.BlockSpec(memory_space=pltpu.SMEM),
                  row(A_WIDTH), row(A_WIDTH), row(A_WIDTH), row(LANES), row(A_WIDTH),
                  pl.BlockSpec((1, A_DV), lambda i: (0, 0)), st_spec,
                  row(B_WIDTH), row(LANES), row(LANES), row(B_WIDTH), c_spec, c_spec],
        out_specs=[row(A_WIDTH), row(B_WIDTH), st_spec, c_spec, c_spec],
        out_shape=[jax.ShapeDtypeStruct((n, A_WIDTH), BF16),
                   jax.ShapeDtypeStruct((n, B_WIDTH), BF16),
                   jax.ShapeDtypeStruct((n, A_HEADS, A_DK, A_DV), F32),
                   jax.ShapeDtypeStruct((n, WINDOW, LANES), F32),
                   jax.ShapeDtypeStruct((n, WINDOW, LANES), F32)],
        scratch_shapes=[pltpu.VMEM((bt, A_WIDTH), F32), pltpu.VMEM((bt * B_HEADS, LANES), F32)],
        compiler_params=pltpu.CompilerParams(dimension_semantics=("arbitrary",),
                                             vmem_limit_bytes=VMEM_LIMIT),
        name="sstep",
    )(sinks, q, k, v, gb, za, na_row, state, qb, kn, vn, zb, ck, cv)


def _rope_tables(pos):
    half = B_HD // 2
    inv = 1.0 / (ROPE_THETA ** (np.arange(half, dtype=np.float64) / half))
    ang = np.asarray(pos, np.float64)[:, None] * inv[None, :]
    cos, sin = np.cos(ang), np.sin(ang)
    reps = LANES // B_HD
    return (jnp.asarray(np.tile(np.concatenate([cos, cos], -1), (1, reps)), F32),
            jnp.asarray(np.tile(np.concatenate([-sin, sin], -1), (1, reps)), F32))


def _pad_row(vec, offset):
    return jnp.zeros((1, LANES), F32).at[0, offset:offset + vec.shape[0]].set(vec.astype(F32))


def _layer(x_prompt, x_sample, state_conv, state_delta, cache_k, cache_v, c_prompt, c_sample,
           w_ada, b_ada, w_in, conv_w, a_log, dt_bias, norm_a, sinks, w_out, ln_g, ln_b):
    bsz, seq, _ = x_prompt.shape
    n_s = x_sample.shape[0]

    w_r = _wprep(jnp.swapaxes(w_in, 0, 1))
    w_o = w_out.astype(BF16)
    alog_row = _pad_row(a_log, A_HEADS)
    dt_row = _pad_row(dt_bias, A_HEADS)
    na_row = norm_a.reshape(1, A_DV)
    g_row = ln_g.reshape(1, D_MODEL)
    b_row = ln_b.reshape(1, D_MODEL)

    assert n_s % 8 == 0 and bsz <= 8
    c_all = jnp.concatenate([c_sample, c_prompt, jnp.zeros((8 - bsz, D_MODEL), F32)], axis=0)
    mod = _ada(c_all, w_ada, b_ada.reshape(1, 3 * D_MODEL))

    cos_p, sin_p = _rope_tables(np.arange(seq))
    (q, k, v, za, gb, qb, kb, kbr, vd0, vd1, zb, conv_p, kb_last, vb_last) = _proj(
        x_prompt, mod, n_s, w_r, conv_w, alog_row, dt_row, cos_p, sin_p)
    oa, delta_p = _delta(q, k, v, gb, za, na_row)
    y_p = _swa_out(sinks, qb, kb, kbr, vd0, vd1, zb, oa, x_prompt, mod, n_s, w_o, g_row, b_row)
    swa_k_p = kb_last.reshape(bsz, WINDOW, B_KV_HEADS, B_HD)
    swa_v_p = vb_last.reshape(bsz, WINDOW, B_KV_HEADS, B_HD)

    cos_s, sin_s = _rope_tables(np.array([PAST_LEN]))
    xs = x_sample.reshape(n_s, D_MODEL)
    cst = jnp.transpose(state_conv, (1, 0, 2))
    sq, sk, sv, sza, sgb, sqb, skn, svn, szb, ncs = _sproj(xs, mod, w_r, conv_w, cst, alog_row, dt_row,
                                                           cos_s, sin_s)
    soa, sob, delta_s, nck, ncv = _sstep(sinks, sq, sk, sv, sgb, sza, na_row, state_delta,
                                         sqb, skn, svn, szb,
                                         jnp.swapaxes(cache_k.reshape(n_s, WINDOW, LANES), 1, 2),
                                         jnp.swapaxes(cache_v.reshape(n_s, WINDOW, LANES), 1, 2))
    y_s = _out(soa, sob, xs, mod, w_o, g_row, b_row)
    conv_s = jnp.transpose(ncs, (1, 0, 2))
    unpack = lambda c: jnp.swapaxes(c, 1, 2).reshape(n_s, WINDOW, B_KV_HEADS, B_HD)
    return (y_p, y_s.reshape(n_s, 1, D_MODEL), conv_p, delta_p, swa_k_p, swa_v_p,
            conv_s, delta_s, unpack(nck), unpack(ncv))


def kernel(x_prompt, x_sample, state_conv, state_delta, cache_swa_k, cache_swa_v, c_prompt, c_sample,
           w_ada, b_ada, w_in, conv_w, a_log, dt_bias, norm_a, sinks, w_out, ln_g, ln_b):
    assert w_ada.shape[0] == DEPTH == 1
    outs = _layer(x_prompt, x_sample, state_conv[0], state_delta[0], cache_swa_k[0], cache_swa_v[0],
                  c_prompt, c_sample, w_ada[0], b_ada[0], w_in[0], conv_w[0], a_log[0], dt_bias[0],
                  norm_a[0], sinks[0], w_out[0], ln_g[0], ln_b[0])
    y_p, y_s = outs[0], outs[1]
    return (y_p, y_s) + tuple(o[None] for o in outs[2:])
```

```python
import jax
import jax.numpy as jnp
import numpy as np
from jax import lax
from jax.experimental import pallas as pl
from jax.experimental.pallas import tpu as pltpu

F32 = jnp.float32
BF16 = jnp.bfloat16

D_MODEL = 1024
DEPTH = 1
PAST_LEN = 8192
A_HEADS = 4
A_DK = 128
A_DV = 128
A_WIDTH = A_HEADS * A_DV
A_QKV = 3 * A_WIDTH
CONV_W = 4
CHUNK = 64
B_HEADS = 8
B_KV_HEADS = 2
B_HD = 64
B_GROUP = B_HEADS // B_KV_HEADS
B_WIDTH = B_HEADS * B_HD
B_KV_WIDTH = B_KV_HEADS * B_HD
WINDOW = 128
ROPE_THETA = 10000.0
MIX_WIDTH = A_WIDTH + B_WIDTH
DEEPNORM_ALPHA = (2 * DEPTH) ** 0.25
LOG2E = 1.4426950408889634
LN_EPS = 1e-5
RMS_EPS = 1e-6
L2_EPS = 1e-6

OFF_A_Z = A_QKV
OFF_A_BETA = OFF_A_Z + A_WIDTH
OFF_A_DECAY = OFF_A_BETA + A_HEADS
OFF_B_Q = OFF_A_DECAY + A_HEADS
OFF_B_K = OFF_B_Q + B_WIDTH
OFF_B_V = OFF_B_K + B_KV_WIDTH
OFF_B_Z = OFF_B_V + B_KV_WIDTH
PROJ_COLS = OFF_B_Z + B_WIDTH

LANES = 128
C_QKV = 0
C_ZA = C_QKV + A_QKV
C_QB = C_ZA + A_WIDTH
C_KB = C_QB + B_WIDTH
C_VB = C_KB + B_KV_WIDTH
C_ZB = C_VB + B_KV_WIDTH
C_BD = C_ZB + B_WIDTH
WPREP_TN = 256
W_COLS = C_BD + WPREP_TN

VMEM_LIMIT = 56 * 1024 * 1024

PROJ_TM = 512
PROJ_CW = 256
PROJ_PARTS = 4
DELTA_CT = 256
DELTA_WAVE = 2
SWA_TQ = 512
SWA_WAVE = 2
STEP_BT = 16


def _dot(a, b):
    return jnp.dot(a, b, preferred_element_type=F32)


def _dot_nt(a, b):
    return lax.dot_general(a, b, (((1,), (1,)), ((), ())), preferred_element_type=F32)


def _silu(x):
    return x * jax.nn.sigmoid(x)


def _softplus(x):
    return jnp.maximum(x, 0.0) + jnp.log1p(jnp.exp(-jnp.abs(x)))


def _lane(shape):
    return lax.broadcasted_iota(jnp.int32, shape, len(shape) - 1)


def _l2norm_heads(y, scale):
    outs = []
    for h in range(y.shape[1] // A_DK):
        xh = y[:, h * A_DK:(h + 1) * A_DK]
        ss = jnp.sum(xh * xh, axis=-1, keepdims=True)
        xn = xh * lax.rsqrt(ss + L2_EPS)
        outs.append(xn * scale if scale != 1.0 else xn)
    return jnp.concatenate(outs, axis=-1)


def _rotary_group(xg, cos, sin_signed):
    lane = _lane(xg.shape)
    swapped = jnp.where((lane % B_HD) < (B_HD // 2),
                        pltpu.roll(xg, LANES - B_HD // 2, axis=1),
                        pltpu.roll(xg, B_HD // 2, axis=1))
    return xg * cos + swapped * sin_signed


def _kv_layouts(kb, vb):
    low = _lane(kb.shape) < B_HD
    kbr = pltpu.roll(kb, B_HD, axis=1)
    vbr = pltpu.roll(vb, B_HD, axis=1)
    return kb, kbr, jnp.where(low, vb, vbr), jnp.where(low, vbr, vb)


def _gate_lanes(bd, alog_row, dt_row):
    lane = _lane(bd.shape)
    g = -jnp.exp(alog_row) * _softplus(bd + dt_row)
    return jnp.where(lane < A_HEADS, jax.nn.sigmoid(bd), g)


def _layer_norm(r, g, b):
    mu = jnp.mean(r, axis=-1, keepdims=True)
    d = r - mu
    var = jnp.mean(d * d, axis=-1, keepdims=True)
    return d * lax.rsqrt(var + LN_EPS) * g + b


def _wprep_kernel(wt_ref, o_ref):
    x = wt_ref[...]
    tail = pl.program_id(0) == pl.num_programs(0) - 1
    row = lax.broadcasted_iota(jnp.int32, x.shape, 0)
    x = jnp.where(jnp.logical_and(tail, row >= 2 * A_HEADS), 0.0, x)
    o_ref[...] = x.T.astype(BF16)


def _wprep(w_t):
    tn = WPREP_TN
    n_a, n_b = OFF_A_BETA // tn, (PROJ_COLS - OFF_B_Q) // tn
    assert n_a * tn == OFF_A_BETA and n_b * tn == PROJ_COLS - OFF_B_Q and OFF_A_BETA + tn <= PROJ_COLS

    def src_row(j):
        return jnp.where(j < n_a, j * tn, jnp.where(j < n_a + n_b, OFF_B_Q + (j - n_a) * tn, OFF_A_BETA))

    return pl.pallas_call(
        _wprep_kernel,
        grid=(n_a + n_b + 1,),
        in_specs=[pl.BlockSpec((pl.Element(tn), pl.Element(D_MODEL)),
                               lambda j: (pl.multiple_of(src_row(j), 8), 0))],
        out_specs=pl.BlockSpec((D_MODEL, tn), lambda j: (0, j)),
        out_shape=jax.ShapeDtypeStruct((D_MODEL, W_COLS), BF16),
        compiler_params=pltpu.CompilerParams(dimension_semantics=("arbitrary",),
                                             vmem_limit_bytes=VMEM_LIMIT),
        name="wprep",
    )(w_t)


def _ada_kernel(c_ref, w_ref, b_ref, o_ref):
    o_ref[...] = _dot(c_ref[...].astype(BF16), w_ref[...].astype(BF16)) + b_ref[...]


def _ada(c_all, w_ada, b_ada):
    rows = c_all.shape[0]
    tn = 768
    return pl.pallas_call(
        _ada_kernel,
        grid=(3 * D_MODEL // tn,),
        in_specs=[pl.BlockSpec((rows, D_MODEL), lambda j: (0, 0)),
                  pl.BlockSpec((D_MODEL, tn), lambda j: (0, j)),
                  pl.BlockSpec((1, tn), lambda j: (0, j))],
        out_specs=pl.BlockSpec((rows, tn), lambda j: (0, j)),
        out_shape=jax.ShapeDtypeStruct((rows, 3 * D_MODEL), F32),
        compiler_params=pltpu.CompilerParams(dimension_semantics=("arbitrary",),
                                             vmem_limit_bytes=VMEM_LIMIT),
        name="ada",
    )(c_all, w_ada, b_ada)


def _proj_kernel(x_ref, mod_ref, w_ref, cw_ref, alog_ref, dt_ref, cos_ref, sin_ref,
                 qkv_ref, za_ref, gb_ref, qb_ref, kvl_ref, zb_ref, cst_ref, kbl_ref, vbl_ref, ubuf):
    tm = x_ref.shape[1]
    t = pl.program_id(1)

    @pl.when(t == 0)
    def _():
        ubuf[...] = jnp.zeros(ubuf.shape, F32)

    brow = pl.ds(pl.program_id(0), 1)
    shift = mod_ref[brow, 0:D_MODEL]
    scale = mod_ref[brow, D_MODEL:2 * D_MODEL]

    rp = tm // PROJ_PARTS
    cw = PROJ_CW
    sub = lax.broadcasted_iota(jnp.int32, (rp // 8, 8, cw), 1)
    pieces = [slice(c0, c0 + cw) for c0 in range(0, A_QKV, cw)]

    def part(r0):
        rs = slice(r0, r0 + rp)
        h = (x_ref[0, rs, :] * (1.0 + scale) + shift).astype(BF16)

        def conv_epilogue(cs, u):
            gi = cs.start // A_WIDTH
            groups = jnp.concatenate([ubuf[:, cs], u], axis=0).reshape(rp // 8 + 1, 8, cw)
            acc = None
            for j in range(CONV_W - 1, 0, -1):
                rot = pltpu.roll(groups, j, axis=1)
                term = (jnp.where(sub < j, rot[:-1], rot[1:]).reshape(rp, cw)
                        * cw_ref[CONV_W - 1 - j:CONV_W - j, cs])
                acc = term if acc is None else acc + term
            y = _silu(acc + u * cw_ref[CONV_W - 1:CONV_W, cs])
            if gi == 0:
                y = _l2norm_heads(y, A_DK ** -0.5)
            elif gi == 1:
                y = _l2norm_heads(y, 1.0)
            qkv_ref[0, rs, cs] = y
            ubuf[:, cs] = u[rp - 8:rp]
            if r0 + rp == tm:
                cst_ref[0, :, cs] = u[rp - (CONV_W - 1):rp]

        pending = _dot(h, w_ref[:, pieces[0]])
        for i, cs in enumerate(pieces):
            u = pending
            if i + 1 < len(pieces):
                pending = _dot(h, w_ref[:, pieces[i + 1]])
            conv_epilogue(cs, u)

        za_ref[0, rs, :] = _dot(h, w_ref[:, C_ZA:C_ZA + A_WIDTH])
        gb_ref[0, rs, :] = _gate_lanes(_dot(h, w_ref[:, C_BD:C_BD + LANES]), alog_ref[...], dt_ref[...])
        cos = cos_ref[rs, :]
        sin = sin_ref[rs, :]
        uq = _dot(h, w_ref[:, C_QB:C_QB + B_WIDTH])
        for g in range(B_WIDTH // LANES):
            qb_ref[0, rs, g * LANES:(g + 1) * LANES] = (
                _rotary_group(uq[:, g * LANES:(g + 1) * LANES], cos, sin) * (B_HD ** -0.5 * LOG2E)).astype(BF16)
        ukv = _dot(h, w_ref[:, C_KB:C_KB + 2 * LANES])
        kb = _rotary_group(ukv[:, 0:LANES], cos, sin)
        vb = ukv[:, LANES:2 * LANES]
        for j, val in enumerate(_kv_layouts(kb, vb)):
            kvl_ref[0, rs, j * LANES:(j + 1) * LANES] = val.astype(BF16)
        zb_ref[0, rs, :] = _dot(h, w_ref[:, C_ZB:C_ZB + B_WIDTH])
        return kb, vb

    for r0 in range(0, tm, rp):
        kb, vb = part(r0)

    @pl.when(t == pl.num_programs(1) - 1)
    def _():
        kbl_ref[0] = kb[rp - WINDOW:rp]
        vbl_ref[0] = vb[rp - WINDOW:rp]


def _proj(x, mod, mod_row0, w_r, conv_w, alog_row, dt_row, cos_t, sin_t):
    bsz, t, _ = x.shape
    tm = PROJ_TM
    row = lambda w: pl.BlockSpec((1, tm, w), lambda b, i: (b, i, 0))
    const2 = lambda s: pl.BlockSpec(s, lambda b, i: (0, 0))
    per_b = lambda r, w: pl.BlockSpec((1, r, w), lambda b, i: (b, 0, 0))
    wide = lambda w, dt=F32: jax.ShapeDtypeStruct((bsz, t, w), dt)
    return pl.pallas_call(
        _proj_kernel,
        grid=(bsz, t // tm),
        in_specs=[row(D_MODEL),
                  pl.BlockSpec((8, 3 * D_MODEL), lambda b, i: (mod_row0 // 8, 0)),
                  const2((D_MODEL, W_COLS)),
                  const2((CONV_W, A_QKV)),
                  const2((1, LANES)), const2((1, LANES)),
                  pl.BlockSpec((tm, LANES), lambda b, i: (i, 0)),
                  pl.BlockSpec((tm, LANES), lambda b, i: (i, 0))],
        out_specs=[row(A_QKV), row(A_WIDTH), row(LANES), row(B_WIDTH), row(4 * LANES), row(B_WIDTH),
                   per_b(CONV_W - 1, A_QKV), per_b(WINDOW, LANES), per_b(WINDOW, LANES)],
        out_shape=[wide(A_QKV), wide(A_WIDTH), wide(LANES), wide(B_WIDTH, BF16), wide(4 * LANES, BF16),
                   wide(B_WIDTH),
                   jax.ShapeDtypeStruct((bsz, CONV_W - 1, A_QKV), F32),
                   jax.ShapeDtypeStruct((bsz, WINDOW, LANES), F32),
                   jax.ShapeDtypeStruct((bsz, WINDOW, LANES), F32)],
        scratch_shapes=[pltpu.VMEM((8, A_QKV), F32)],
        compiler_params=pltpu.CompilerParams(dimension_semantics=("arbitrary", "arbitrary"),
                                             vmem_limit_bytes=VMEM_LIMIT),
        name="proj",
    )(x, mod, w_r, conv_w, alog_row, dt_row, cos_t, sin_t)


def _delta_kernel(q_ref, k_ref, v_ref, gb_ref, za_ref, na_ref, oa_ref, st_ref,
                  s_scr, wq_s, ut_s, akd_s, gl_s):
    bsz, ct = q_ref.shape[0], q_ref.shape[1]
    nch = ct // CHUNK
    t = pl.program_id(0)
    wslot = t % 2
    rslot = 1 - wslot

    @pl.when(t == 0)
    def _():
        s_scr[...] = jnp.zeros(s_scr.shape, F32)
        wq_s[...] = jnp.zeros(wq_s.shape, BF16)
        ut_s[...] = jnp.zeros(ut_s.shape, F32)
        akd_s[...] = jnp.zeros(akd_s.shape, BF16)
        gl_s[...] = jnp.zeros(gl_s.shape, F32)

    units = [(b, c, h) for b in range(bsz) for c in range(nch) for h in range(A_HEADS)]
    uid = {u_: i for i, u_ in enumerate(units)}
    rows = lambda c: slice(c * CHUNK, (c + 1) * CHUNK)
    lanes = lambda h: slice(h * A_DK, (h + 1) * A_DK)
    na = na_ref[...]

    s_cur = {(b, h): s_scr[b * A_HEADS + h] for b in range(bsz) for h in range(A_HEADS)}
    ws, uu = {}, {}

    def rec_ws(c):
        for b in range(bsz):
            for h in range(A_HEADS):
                i = uid[b, c, h]
                ws[b, h] = _dot(wq_s[rslot, i], s_cur[b, h].astype(BF16))
                uu[b, h] = (ut_s[rslot, i] - ws[b, h][:CHUNK]).astype(BF16)

    def rec_ou(c):
        zpad = jnp.zeros((CHUNK, A_DV), BF16)
        for b in range(bsz):
            u_bd = jnp.concatenate(
                [jnp.concatenate([uu[b, h] if hh == h else zpad for hh in range(A_HEADS)], axis=-1)
                 for h in range(A_HEADS)], axis=0)
            ou = _dot(akd_s[rslot, b * nch + c], u_bd)
            for h in range(A_HEADS):
                o = ws[b, h][CHUNK:] + ou[:CHUNK, lanes(h)]
                s_cur[b, h] = gl_s[rslot, uid[b, c, h]] * s_cur[b, h] + ou[CHUNK:, lanes(h)]
                on = o * lax.rsqrt(jnp.mean(o * o, axis=-1, keepdims=True) + RMS_EPS) * na
                oa_ref[b, rows(c), lanes(h)] = (on * _silu(za_ref[b, rows(c), lanes(h)])).astype(BF16)

    rec_stages = []
    for c in range(nch):
        rec_stages += [lambda c=c: rec_ws(c), lambda c=c: rec_ou(c)]

    def run_rec(n_left_after):
        while rec_stages and len(rec_stages) > n_left_after:
            rec_stages.pop(0)()

    pk = A_HEADS * CHUNK
    low = _lane((CHUNK, LANES)) < CHUNK
    low_row = _lane((1, LANES)) < CHUNK
    ti_p = lax.broadcasted_iota(jnp.int32, (CHUNK, pk), 0)
    ii_p = _lane((CHUNK, pk)) % CHUNK
    zero64 = jnp.zeros((CHUNK, LANES), BF16)

    def pack(parts):
        return jnp.concatenate([jnp.where(low, parts[0], parts[1]), jnp.where(low, parts[2], parts[3])], axis=-1)

    def block_diag(x16):
        blocks = []
        for h in range(A_HEADS):
            pair, first = h // 2, h % 2 == 0
            piece = jnp.where(low if first else jnp.logical_not(low), x16[:, pair * LANES:(pair + 1) * LANES], zero64)
            blocks.append(jnp.concatenate([piece, zero64] if pair == 0 else [zero64, piece], axis=-1))
        return jnp.concatenate(blocks, axis=0)

    zrhs = jnp.zeros((CHUNK, 2 * A_DK), BF16)
    n_rec = len(rec_stages)
    n_slots = 8 * (bsz // DELTA_WAVE)
    done = [0]

    def stage_done():
        done[0] += 1
        run_rec(n_rec - (done[0] * n_rec) // n_slots)

    def decay_terms(b, beta, g_col, g_last, eg, dec_p, beta_p):
        gbv = gb_ref[b]
        rin = lax.broadcasted_iota(jnp.int32, gbv.shape, 0) % CHUNK
        gcs = gbv
        s = 1
        while s < CHUNK:
            gcs = gcs + jnp.where(rin >= s, pltpu.roll(gcs, s, axis=0), 0.0)
            s *= 2
        gcs_t = gcs.T
        for c in range(nch):
            r0 = c * CHUNK
            pair_lanes = slice((c // 2) * LANES, (c // 2 + 1) * LANES)
            g_rows = []
            for h in range(A_HEADS):
                u_ = (b, c, h)
                beta[u_] = jnp.broadcast_to(gbv[rows(c), h:h + 1], (CHUNK, A_DK))
                g_col[u_] = jnp.broadcast_to(gcs[rows(c), A_HEADS + h:A_HEADS + h + 1], (CHUNK, A_DK))
                g_last[u_] = gcs[r0 + CHUNK - 1:r0 + CHUNK, A_HEADS + h:A_HEADS + h + 1]
                eg[u_] = jnp.exp(g_col[u_])
                g_row = gcs_t[A_HEADS + h:A_HEADS + h + 1, pair_lanes]
                g_rows.append(g_row if c % 2 == h % 2 else pltpu.roll(g_row, CHUNK, axis=1))
            g_row_p = jnp.concatenate([jnp.where(low_row, g_rows[0], g_rows[1]),
                                       jnp.where(low_row, g_rows[2], g_rows[3])], axis=-1)
            g_col_p = pack([g_col[b, c, h] for h in range(A_HEADS)])
            dec_p[b, c] = jnp.exp(jnp.where(ti_p >= ii_p, g_col_p - g_row_p, -jnp.inf))
            beta_p[b, c] = pack([beta[b, c, h] for h in range(A_HEADS)])

    def prepare(bs):
        groups_b = [(b, c) for b in bs for c in range(nch)]
        beta, g_col, g_last, eg, dec_p, beta_p = {}, {}, {}, {}, {}, {}
        for b in bs:
            decay_terms(b, beta, g_col, g_last, eg, dec_p, beta_p)

        nmat = {}
        for (b, c) in groups_b:
            k16 = k_ref[b, rows(c), :].astype(BF16)
            q16 = q_ref[b, rows(c), :].astype(BF16)
            k_heads = jnp.concatenate(
                [jnp.concatenate([k16[:, lanes(h)] if hh == h else zero64 for hh in range(A_HEADS)], axis=-1)
                 for h in range(A_HEADS)], axis=0)
            kq = _dot_nt(jnp.concatenate([k16, q16], axis=0), k_heads)
            nmat[b, c] = -(beta_p[b, c] * kq[:CHUNK] * jnp.where(ti_p > ii_p, dec_p[b, c], 0.0))
            akd_s[wslot, b * nch + c, 0:CHUNK, :] = (kq[CHUNK:] * dec_p[b, c]).astype(BF16)
        stage_done()

        rsum = dict(nmat)
        pw16 = {g_: nmat[g_].astype(BF16) for g_ in groups_b}
        pw = {g_: _dot(pw16[g_], block_diag(pw16[g_])) for g_ in groups_b}
        stage_done()
        for step in range(1, 6):
            last = step == 5
            pw16 = {g_: pw[g_].astype(BF16) for g_ in groups_b}
            rp = {}
            for g_ in groups_b:
                r16 = rsum[g_].astype(BF16)
                rp[g_] = _dot(r16 if last else jnp.concatenate([r16, pw16[g_]], axis=0), block_diag(pw16[g_]))
            for g_ in groups_b:
                rsum[g_] = rsum[g_] + pw[g_] + rp[g_][:CHUNK]
                if not last:
                    pw[g_] = rp[g_][CHUNK:]
            stage_done()

        for (b, c) in groups_b:
            for h in range(A_HEADS):
                u_ = (b, c, h)
                i = uid[u_]
                kc = k_ref[b, rows(c), lanes(h)]
                rhs = jnp.concatenate([(beta[u_] * eg[u_]) * kc, beta[u_] * v_ref[b, rows(c), lanes(h)]],
                                      axis=-1)
                rhs16 = rhs.astype(BF16)
                rhs_rows = jnp.concatenate([rhs16 if hh == h else zrhs for hh in range(A_HEADS)], axis=0)
                sol = rhs + _dot(rsum[b, c].astype(BF16), rhs_rows)
                wq_s[wslot, i] = jnp.concatenate([sol[:, :A_DK], eg[u_] * q_ref[b, rows(c), lanes(h)]],
                                                 axis=0).astype(BF16)
                ut_s[wslot, i] = sol[:, A_DK:]
                gl_s[wslot, i] = jnp.broadcast_to(jnp.exp(g_last[u_]), (1, A_DV))
        for (b, c) in groups_b:
            kd = [jnp.exp(g_last[b, c, h] - g_col[b, c, h]) * k_ref[b, rows(c), lanes(h)]
                  for h in range(A_HEADS)]
            for p in range(A_HEADS // 2):
                akd_s[wslot, b * nch + c, CHUNK:, p * LANES:(p + 1) * LANES] = (
                    jnp.concatenate([kd[2 * p], kd[2 * p + 1]], axis=0).T.astype(BF16))
        stage_done()

    for b0 in range(0, bsz, DELTA_WAVE):
        prepare(range(b0, b0 + DELTA_WAVE))
    run_rec(0)

    for b in range(bsz):
        for h in range(A_HEADS):
            s_scr[b * A_HEADS + h] = s_cur[b, h]

    @pl.when(t == pl.num_programs(0) - 1)
    def _():
        for b in range(bsz):
            for h in range(A_HEADS):
                st_ref[b, h] = s_cur[b, h]


def _delta(qkv, gb, za, na_row):
    bsz, t, _ = qkv.shape
    ct = DELTA_CT
    nt = t // ct
    n_units = bsz * (ct // CHUNK) * A_HEADS
    prep = lambda w, j=0: pl.BlockSpec((bsz, ct, w), lambda i: (0, jnp.minimum(i, nt - 1), j))
    rec = lambda w: pl.BlockSpec((bsz, ct, w), lambda i: (0, jnp.maximum(i - 1, 0), 0))
    return pl.pallas_call(
        _delta_kernel,
        grid=(nt + 1,),
        in_specs=[prep(A_WIDTH, 0), prep(A_WIDTH, 1), prep(A_WIDTH, 2), prep(LANES), rec(A_WIDTH),
                  pl.BlockSpec((1, A_DV), lambda i: (0, 0))],
        out_specs=[rec(A_WIDTH),
                   pl.BlockSpec((bsz, A_HEADS, A_DK, A_DV), lambda i: (0, 0, 0, 0))],
        out_shape=[jax.ShapeDtypeStruct((bsz, t, A_WIDTH), BF16),
                   jax.ShapeDtypeStruct((bsz, A_HEADS, A_DK, A_DV), F32)],
        scratch_shapes=[pltpu.VMEM((bsz * A_HEADS, A_DK, A_DV), F32),
                        pltpu.VMEM((2, n_units, 2 * CHUNK, A_DK), BF16),
                        pltpu.VMEM((2, n_units, CHUNK, A_DV), F32),
                        pltpu.VMEM((2, n_units // A_HEADS, CHUNK + A_DK, A_HEADS * CHUNK), BF16),
                        pltpu.VMEM((2, n_units, 1, A_DV), F32)],
        compiler_params=pltpu.CompilerParams(dimension_semantics=("arbitrary",),
                                             vmem_limit_bytes=VMEM_LIMIT),
        name="delta",
    )(qkv, qkv, qkv, gb, za, na_row)


def _swa_out_kernel(sink_ref, qb_ref, kc_ref, kp_ref, krc_ref, krp_ref, v0c_ref, v0p_ref, v1c_ref, v1p_ref,
                    zb_ref, oa_ref, x_ref, gate_ref, w_ref, g_ref, b_ref, y_ref):
    n = pl.program_id(1)
    gate = gate_ref[pl.ds(pl.program_id(0), 1), :]
    tq = qb_ref.shape[1]
    blk = WINDOW
    kx = (jnp.concatenate([kp_ref[0], kc_ref[0]], axis=0), jnp.concatenate([krp_ref[0], krc_ref[0]], axis=0))
    vd = (jnp.concatenate([v0p_ref[0], v0c_ref[0]], axis=0), jnp.concatenate([v1p_ref[0], v1c_ref[0]], axis=0))

    a = lax.broadcasted_iota(jnp.int32, (2 * blk, 2 * blk), 0) % blk
    j = lax.broadcasted_iota(jnp.int32, (2 * blk, 2 * blk), 1)
    rel = a + blk - j
    band = (rel >= 0) & (rel <= WINDOW)
    band_first = band & ((n > 0) | (j >= blk))
    top = lax.broadcasted_iota(jnp.int32, (2 * blk, 1), 0) < blk
    low = _lane((blk, LANES)) < B_HD
    zero = jnp.zeros((blk, LANES), BF16)

    qrows = lambda i: slice(i * blk, (i + 1) * blk)
    krows = lambda i: slice(i * blk, (i + 2) * blk)
    sink = {(kh, half): jnp.where(top, sink_ref[kh * B_GROUP + half] * LOG2E,
                                  sink_ref[kh * B_GROUP + half + 2] * LOG2E)
            for kh in range(B_KV_HEADS) for half in range(2)}
    for i0 in range(0, tq // blk, SWA_WAVE):
        blocks = range(i0, i0 + SWA_WAVE)
        units = [(i, kh, half) for i in blocks for kh in range(B_KV_HEADS) for half in range(2)]
        mix_a = {i: _dot(oa_ref[0, qrows(i), :], w_ref[0:A_WIDTH, :]) for i in blocks}
        sc = {}
        for (i, kh, half) in units:
            qs = []
            for g in range(2):
                grp = kh * 2 + g
                xg = qb_ref[0, qrows(i), grp * LANES:(grp + 1) * LANES]
                qs.append(jnp.where(low if half == 0 else jnp.logical_not(low), xg, zero))
            qz = jnp.concatenate(qs, axis=0)
            sc[i, kh, half] = _dot_nt(qz, kx[0 if kh == half else 1][krows(i)])
        p, den = {}, {}
        for u_ in units:
            i, kh, half = u_
            s_m = jnp.where(band_first if i == 0 else band, sc[u_], -jnp.inf)
            m = jnp.maximum(jnp.max(s_m, axis=-1, keepdims=True), sink[kh, half])
            e = jnp.exp2(s_m - m)
            den[u_] = jnp.sum(e, axis=-1, keepdims=True) + jnp.exp2(sink[kh, half] - m)
            p[u_] = e.astype(BF16)
        pv = {u_: _dot(p[u_], vd[u_[1]][krows(u_[0])]) for u_ in units}
        outs = {u_: pv[u_] / den[u_] for u_ in units}
        for i in blocks:
            ob = []
            for grp in range(B_WIDTH // LANES):
                kh, g = grp // 2, grp % 2
                og = jnp.where(low, outs[i, kh, 0][g * blk:(g + 1) * blk], outs[i, kh, 1][g * blk:(g + 1) * blk])
                ob.append((og * _silu(zb_ref[0, qrows(i), grp * LANES:(grp + 1) * LANES])).astype(BF16))
            mix = mix_a[i] + _dot(jnp.concatenate(ob, axis=-1), w_ref[A_WIDTH:MIX_WIDTH, :])
            r = DEEPNORM_ALPHA * x_ref[0, qrows(i), :] + (1.0 + gate) * mix
            y_ref[0, qrows(i), :] = _layer_norm(r, g_ref[...], b_ref[...])


def _swa_out(sinks, qb, kvl, zb, oa, x, mod, mod_row0, w_out, ln_g, ln_b):
    bsz, t, _ = qb.shape
    tq = SWA_TQ
    per = tq // WINDOW
    cur = lambda w: pl.BlockSpec((1, tq, w), lambda b, i: (b, i, 0))
    kv_cur = lambda j: pl.BlockSpec((1, tq, LANES), lambda b, i: (b, i, j))
    kv_prev = lambda j: pl.BlockSpec((1, WINDOW, LANES), lambda b, i: (b, jnp.maximum(i * per - 1, 0), j))
    const2 = lambda s: pl.BlockSpec(s, lambda b, i: (0, 0))
    return pl.pallas_call(
        _swa_out_kernel,
        grid=(bsz, t // tq),
        in_specs=[pl.BlockSpec(memory_space=pltpu.SMEM), cur(B_WIDTH),
                  kv_cur(0), kv_prev(0), kv_cur(1), kv_prev(1), kv_cur(2), kv_prev(2), kv_cur(3), kv_prev(3),
                  cur(B_WIDTH), cur(A_WIDTH), cur(D_MODEL),
                  pl.BlockSpec((8, D_MODEL), lambda b, i: (mod_row0 // 8, 2)),
                  const2((MIX_WIDTH, D_MODEL)), const2((1, D_MODEL)), const2((1, D_MODEL))],
        out_specs=cur(D_MODEL),
        out_shape=jax.ShapeDtypeStruct((bsz, t, D_MODEL), F32),
        compiler_params=pltpu.CompilerParams(dimension_semantics=("arbitrary", "arbitrary"),
                                             vmem_limit_bytes=VMEM_LIMIT),
        name="swa_out",
    )(sinks, qb, kvl, kvl, kvl, kvl, kvl, kvl, kvl, kvl, zb, oa, x, mod, w_out, ln_g, ln_b)


def _out_kernel(oa_ref, ob_ref, x_ref, gate_ref, w_ref, g_ref, b_ref, y_ref):
    mix = _dot(oa_ref[...], w_ref[0:A_WIDTH, :]) + _dot(ob_ref[...], w_ref[A_WIDTH:MIX_WIDTH, :])
    r = DEEPNORM_ALPHA * x_ref[...] + (1.0 + gate_ref[...]) * mix
    y_ref[...] = _layer_norm(r, g_ref[...], b_ref[...])


def _out(oa, ob, x, mod, w_out, ln_g, ln_b):
    n = x.shape[0]
    full = lambda s: pl.BlockSpec(s, lambda i: (0, 0))
    return pl.pallas_call(
        _out_kernel,
        grid=(1,),
        in_specs=[full((n, A_WIDTH)), full((n, B_WIDTH)), full((n, D_MODEL)),
                  pl.BlockSpec((n, D_MODEL), lambda i: (0, 2)),
                  full((MIX_WIDTH, D_MODEL)), full((1, D_MODEL)), full((1, D_MODEL))],
        out_specs=full((n, D_MODEL)),
        out_shape=jax.ShapeDtypeStruct((n, D_MODEL), F32),
        compiler_params=pltpu.CompilerParams(dimension_semantics=("arbitrary",),
                                             vmem_limit_bytes=VMEM_LIMIT),
        name="out",
    )(oa, ob, x, mod, w_out, ln_g, ln_b)


def _sproj_kernel(x_ref, mod_ref, w_ref, cw_ref, cst_ref, alog_ref, dt_ref, cos_ref, sin_ref,
                  q_ref, k_ref, v_ref, za_ref, gb_ref, qb_ref, kb_ref, vb_ref, zb_ref, ncs_ref):
    shift = mod_ref[:, 0:D_MODEL]
    scale = mod_ref[:, D_MODEL:2 * D_MODEL]
    h = (x_ref[...] * (1.0 + scale) + shift).astype(BF16)

    for gi, o_ref in enumerate((q_ref, k_ref, v_ref)):
        c0 = gi * A_WIDTH
        cs = slice(c0, c0 + A_WIDTH)
        u = _dot(h, w_ref[:, cs])
        acc = cst_ref[0, :, cs] * cw_ref[0:1, cs]
        acc = acc + cst_ref[1, :, cs] * cw_ref[1:2, cs]
        acc = acc + cst_ref[2, :, cs] * cw_ref[2:3, cs]
        acc = acc + u * cw_ref[3:4, cs]
        y = _silu(acc)
        if gi == 0:
            y = _l2norm_heads(y, A_DK ** -0.5)
        elif gi == 1:
            y = _l2norm_heads(y, 1.0)
        o_ref[...] = y
        ncs_ref[0, :, cs] = cst_ref[1, :, cs]
        ncs_ref[1, :, cs] = cst_ref[2, :, cs]
        ncs_ref[2, :, cs] = u

    za_ref[...] = _dot(h, w_ref[:, C_ZA:C_ZA + A_WIDTH])
    gb_ref[...] = _gate_lanes(_dot(h, w_ref[:, C_BD:C_BD + LANES]), alog_ref[...], dt_ref[...])

    cos = cos_ref[...]
    sin = sin_ref[...]
    uq = _dot(h, w_ref[:, C_QB:C_QB + B_WIDTH])
    for g in range(B_WIDTH // LANES):
        qb_ref[:, g * LANES:(g + 1) * LANES] = (
            _rotary_group(uq[:, g * LANES:(g + 1) * LANES], cos, sin) * (B_HD ** -0.5))
    kb_ref[...] = _rotary_group(_dot(h, w_ref[:, C_KB:C_KB + LANES]), cos, sin)
    vb_ref[...] = _dot(h, w_ref[:, C_VB:C_VB + LANES])
    zb_ref[...] = _dot(h, w_ref[:, C_ZB:C_ZB + B_WIDTH])


def _sproj(x, mod_s, w_r, conv_w, cst, alog_row, dt_row, cos_row, sin_row):
    n = x.shape[0]
    full = lambda s: pl.BlockSpec(s, lambda i: (0,) * len(s))
    wide = lambda w: jax.ShapeDtypeStruct((n, w), F32)
    return pl.pallas_call(
        _sproj_kernel,
        grid=(1,),
        in_specs=[full((n, D_MODEL)), pl.BlockSpec((n, 3 * D_MODEL), lambda i: (0, 0)),
                  full((D_MODEL, W_COLS)),
                  full((CONV_W, A_QKV)), full((CONV_W - 1, n, A_QKV)),
                  full((1, LANES)), full((1, LANES)), full((1, LANES)), full((1, LANES))],
        out_specs=[full((n, A_WIDTH)), full((n, A_WIDTH)), full((n, A_WIDTH)), full((n, A_WIDTH)),
                   full((n, LANES)), full((n, B_WIDTH)), full((n, LANES)), full((n, LANES)),
                   full((n, B_WIDTH)), full((CONV_W - 1, n, A_QKV))],
        out_shape=[wide(A_WIDTH), wide(A_WIDTH), wide(A_WIDTH), wide(A_WIDTH), wide(LANES),
                   wide(B_WIDTH), wide(LANES), wide(LANES), wide(B_WIDTH),
                   jax.ShapeDtypeStruct((CONV_W - 1, n, A_QKV), F32)],
        compiler_params=pltpu.CompilerParams(dimension_semantics=("arbitrary",),
                                             vmem_limit_bytes=VMEM_LIMIT),
        name="sproj",
    )(x, mod_s, w_r, conv_w, cst, alog_row, dt_row, cos_row, sin_row)


def _sstep_kernel(sink_ref, q_ref, k_ref, v_ref, gb_ref, za_ref, na_ref, st_ref,
                  qb_ref, kn_ref, vn_ref, zb_ref, ck_ref, cv_ref,
                  oa_ref, ob_ref, nst_ref, nck_ref, ncv_ref,
                  o_scr, ob_scr):
    bt = q_ref.shape[0]
    gbv = gb_ref[...]

    pick = (lax.broadcasted_iota(jnp.int32, (bt, bt * A_DV), 1) // A_DV
            == lax.broadcasted_iota(jnp.int32, (bt, bt * A_DV), 0))
    pick = jnp.where(pick, 1.0, 0.0).astype(BF16)
    for h in range(A_HEADS):
        hs = slice(h * A_DK, (h + 1) * A_DK)
        q_rep = _dot(q_ref[:, hs].T.astype(BF16), pick)
        k_rep = _dot(k_ref[:, hs].T.astype(BF16), pick)
        for bb in range(bt):
            eg = jnp.exp(gbv[bb:bb + 1, A_HEADS + h:A_HEADS + h + 1])
            beta = gbv[bb:bb + 1, h:h + 1]
            kcol = k_rep[:, bb * A_DV:(bb + 1) * A_DV]
            qcol = q_rep[:, bb * A_DV:(bb + 1) * A_DV]
            s1 = eg * st_ref[bb, h]
            pred = jnp.sum(kcol * s1, axis=0, keepdims=True)
            upd = beta * (v_ref[bb:bb + 1, hs] - pred)
            s2 = s1 + kcol * upd
            nst_ref[bb, h] = s2
            o_scr[bb:bb + 1, hs] = jnp.sum(qcol * s2, axis=0, keepdims=True)
    na = na_ref[...]
    for h in range(A_HEADS):
        hs = slice(h * A_DK, (h + 1) * A_DK)
        o = o_scr[:, hs]
        on = o * lax.rsqrt(jnp.mean(o * o, axis=-1, keepdims=True) + RMS_EPS) * na
        oa_ref[:, hs] = (on * _silu(za_ref[:, hs])).astype(BF16)

    row8 = lax.broadcasted_iota(jnp.int32, (B_HEADS, LANES), 0)
    lane8 = _lane((B_HEADS, LANES))
    own_half = (lane8 >= B_HD) == (row8 >= B_GROUP)
    rcol = lax.broadcasted_iota(jnp.int32, (B_HEADS, 1), 0)
    sink = jnp.zeros((B_HEADS, 1), F32)
    for r in range(B_HEADS):
        sink = jnp.where(rcol == r, sink_ref[r], sink)
    qv = qb_ref[...]
    qv_r = jnp.concatenate([pltpu.roll(qv[:, g * LANES:(g + 1) * LANES], B_HD, axis=1)
                            for g in range(B_WIDTH // LANES)], axis=-1)
    kn_t = kn_ref[...].T
    vn_t = vn_ref[...].T
    newest = _lane((LANES, WINDOW)) == WINDOW - 1
    qzs, scs = [], []
    for bb in range(bt):
        qz = jnp.zeros((B_HEADS, LANES), F32)
        for r in range(B_HEADS):
            grp, half, kh = r // 2, r % 2, r // B_GROUP
            src = qv if half == kh else qv_r
            qz = jnp.where(row8 == r, src[bb:bb + 1, grp * LANES:(grp + 1) * LANES], qz)
        qzs.append(jnp.where(own_half, qz, 0.0))
    for bb in range(bt):
        scs.append(_dot(qzs[bb], ck_ref[bb]))
    ps, pnews, dens = [], [], []
    for bb in range(bt):
        sc_new = jnp.sum(qzs[bb] * kn_ref[bb:bb + 1, :], axis=-1, keepdims=True)
        m = jnp.maximum(jnp.maximum(jnp.max(scs[bb], axis=-1, keepdims=True), sc_new), sink)
        p = jnp.exp(scs[bb] - m)
        p_new = jnp.exp(sc_new - m)
        ps.append(p)
        pnews.append(p_new)
        dens.append(jnp.sum(p, axis=-1, keepdims=True) + p_new + jnp.exp(sink - m))
    pvs = [_dot_nt(ps[bb], cv_ref[bb]) for bb in range(bt)]
    for bb in range(bt):
        o = (pvs[bb] + pnews[bb] * vn_ref[bb:bb + 1, :]) / dens[bb]
        o = jnp.where(own_half, o, 0.0)
        ob_scr[bb * B_HEADS:(bb + 1) * B_HEADS, :] = o + pltpu.roll(o, B_HD, axis=1)
    for bb in range(bt):
        nck_ref[bb] = jnp.where(newest, kn_t[:, bb:bb + 1], pltpu.roll(ck_ref[bb], WINDOW - 1, axis=1))
        ncv_ref[bb] = jnp.where(newest, vn_t[:, bb:bb + 1], pltpu.roll(cv_ref[bb], WINDOW - 1, axis=1))
    low = _lane((bt, LANES)) < B_HD
    for grp in range(B_WIDTH // LANES):
        even = ob_scr[pl.ds(2 * grp, bt, stride=B_HEADS), :]
        odd = ob_scr[pl.ds(2 * grp + 1, bt, stride=B_HEADS), :]
        gs = slice(grp * LANES, (grp + 1) * LANES)
        ob_ref[:, gs] = (jnp.where(low, even, odd) * _silu(zb_ref[:, gs])).astype(BF16)


def _sstep(sinks, q, k, v, gb, za, na_row, state, qb, kn, vn, zb, ck, cv):
    n = q.shape[0]
    bt = STEP_BT
    row = lambda w: pl.BlockSpec((bt, w), lambda i: (i, 0))
    st_spec = pl.BlockSpec((bt, A_HEADS, A_DK, A_DV), lambda i: (i, 0, 0, 0))
    c_spec = pl.BlockSpec((bt, WINDOW, LANES), lambda i: (i, 0, 0))
    return pl.pallas_call(
        _sstep_kernel,
        grid=(n // bt,),
        in_specs=[pl.BlockSpec(memory_space=pltpu.SMEM),
                  row(A_WIDTH), row(A_WIDTH), row(A_WIDTH), row(LANES), row(A_WIDTH),
                  pl.BlockSpec((1, A_DV), lambda i: (0, 0)), st_spec,
                  row(B_WIDTH), row(LANES), row(LANES), row(B_WIDTH), c_spec, c_spec],
        out_specs=[row(A_WIDTH), row(B_WIDTH), st_spec, c_spec, c_spec],
        out_shape=[jax.ShapeDtypeStruct((n, A_WIDTH), BF16),
                   jax.ShapeDtypeStruct((n, B_WIDTH), BF16),
                   jax.ShapeDtypeStruct((n, A_HEADS, A_DK, A_DV), F32),
                   jax.ShapeDtypeStruct((n, WINDOW, LANES), F32),
                   jax.ShapeDtypeStruct((n, WINDOW, LANES), F32)],
        scratch_shapes=[pltpu.VMEM((bt, A_WIDTH), F32), pltpu.VMEM((bt * B_HEADS, LANES), F32)],
        compiler_params=pltpu.CompilerParams(dimension_semantics=("arbitrary",),
                                             vmem_limit_bytes=VMEM_LIMIT),
        name="sstep",
    )(sinks, q, k, v, gb, za, na_row, state, qb, kn, vn, zb, ck, cv)


def _rope_tables(pos):
    half = B_HD // 2
    inv = 1.0 / (ROPE_THETA ** (np.arange(half, dtype=np.float64) / half))
    ang = np.asarray(pos, np.float64)[:, None] * inv[None, :]
    cos, sin = np.cos(ang), np.sin(ang)
    reps = LANES // B_HD
    return (jnp.asarray(np.tile(np.concatenate([cos, cos], -1), (1, reps)), F32),
            jnp.asarray(np.tile(np.concatenate([-sin, sin], -1), (1, reps)), F32))


def _pad_row(vec, offset):
    return jnp.zeros((1, LANES), F32).at[0, offset:offset + vec.shape[0]].set(vec.astype(F32))


def _layer(x_prompt, x_sample, state_conv, state_delta, cache_k, cache_v, c_prompt, c_sample,
           w_ada, b_ada, w_in, conv_w, a_log, dt_bias, norm_a, sinks, w_out, ln_g, ln_b):
    bsz, seq, _ = x_prompt.shape
    n_s = x_sample.shape[0]

    w_r = _wprep(jnp.swapaxes(w_in, 0, 1))
    w_o = w_out.astype(BF16)
    alog_row = _pad_row(a_log, A_HEADS)
    dt_row = _pad_row(dt_bias, A_HEADS)
    na_row = norm_a.reshape(1, A_DV)
    g_row = ln_g.reshape(1, D_MODEL)
    b_row = ln_b.reshape(1, D_MODEL)

    assert n_s % 8 == 0 and bsz <= 8
    c_all = jnp.concatenate([c_sample, c_prompt, jnp.zeros((8 - bsz, D_MODEL), F32)], axis=0)
    mod = _ada(c_all, w_ada, b_ada.reshape(1, 3 * D_MODEL))

    cos_p, sin_p = _rope_tables(np.arange(seq))
    (qkv, za, gb, qb, kvl, zb, conv_p, kb_last, vb_last) = _proj(
        x_prompt, mod, n_s, w_r, conv_w, alog_row, dt_row, cos_p, sin_p)
    oa, delta_p = _delta(qkv, gb, za, na_row)
    y_p = _swa_out(sinks, qb, kvl, zb, oa, x_prompt, mod, n_s, w_o, g_row, b_row)
    swa_k_p = kb_last.reshape(bsz, WINDOW, B_KV_HEADS, B_HD)
    swa_v_p = vb_last.reshape(bsz, WINDOW, B_KV_HEADS, B_HD)

    cos_s, sin_s = _rope_tables(np.array([PAST_LEN]))
    xs = x_sample.reshape(n_s, D_MODEL)
    cst = jnp.transpose(state_conv, (1, 0, 2))
    sq, sk, sv, sza, sgb, sqb, skn, svn, szb, ncs = _sproj(xs, mod, w_r, conv_w, cst, alog_row, dt_row,
                                                           cos_s, sin_s)
    soa, sob, delta_s, nck, ncv = _sstep(sinks, sq, sk, sv, sgb, sza, na_row, state_delta,
                                         sqb, skn, svn, szb,
                                         jnp.swapaxes(cache_k.reshape(n_s, WINDOW, LANES), 1, 2),
                                         jnp.swapaxes(cache_v.reshape(n_s, WINDOW, LANES), 1, 2))
    y_s = _out(soa, sob, xs, mod, w_o, g_row, b_row)
    conv_s = jnp.transpose(ncs, (1, 0, 2))
    unpack = lambda c: jnp.swapaxes(c, 1, 2).reshape(n_s, WINDOW, B_KV_HEADS, B_HD)
    return (y_p, y_s.reshape(n_s, 1, D_MODEL), conv_p, delta_p, swa_k_p, swa_v_p,
            conv_s, delta_s, unpack(nck), unpack(ncv))


def kernel(x_prompt, x_sample, state_conv, state_delta, cache_swa_k, cache_swa_v, c_prompt, c_sample,
           w_ada, b_ada, w_in, conv_w, a_log, dt_bias, norm_a, sinks, w_out, ln_g, ln_b):
    assert w_ada.shape[0] == DEPTH == 1
    outs = _layer(x_prompt, x_sample, state_conv[0], state_delta[0], cache_swa_k[0], cache_swa_v[0],
                  c_prompt, c_sample, w_ada[0], b_ada[0], w_in[0], conv_w[0], a_log[0], dt_bias[0],
                  norm_a[0], sinks[0], w_out[0], ln_g[0], ln_b[0])
    y_p, y_s = outs[0], outs[1]
    return (y_p, y_s) + tuple(o[None] for o in outs[2:])
```

```python
import jax
import jax.numpy as jnp
import numpy as np
from jax import lax
from jax.experimental import pallas as pl
from jax.experimental.pallas import tpu as pltpu

F32 = jnp.float32
BF16 = jnp.bfloat16

D_MODEL = 1024
DEPTH = 1
PAST_LEN = 8192
A_HEADS = 4
A_DK = 128
A_DV = 128
A_WIDTH = A_HEADS * A_DV
A_QKV = 3 * A_WIDTH
CONV_W = 4
CHUNK = 64
B_HEADS = 8
B_KV_HEADS = 2
B_HD = 64
B_GROUP = B_HEADS // B_KV_HEADS
B_WIDTH = B_HEADS * B_HD
B_KV_WIDTH = B_KV_HEADS * B_HD
WINDOW = 128
ROPE_THETA = 10000.0
MIX_WIDTH = A_WIDTH + B_WIDTH
DEEPNORM_ALPHA = (2 * DEPTH) ** 0.25
LOG2E = 1.4426950408889634
LN_EPS = 1e-5
RMS_EPS = 1e-6
L2_EPS = 1e-6

OFF_A_Z = A_QKV
OFF_A_BETA = OFF_A_Z + A_WIDTH
OFF_A_DECAY = OFF_A_BETA + A_HEADS
OFF_B_Q = OFF_A_DECAY + A_HEADS
OFF_B_K = OFF_B_Q + B_WIDTH
OFF_B_V = OFF_B_K + B_KV_WIDTH
OFF_B_Z = OFF_B_V + B_KV_WIDTH
PROJ_COLS = OFF_B_Z + B_WIDTH

LANES = 128
C_QKV = 0
C_ZA = C_QKV + A_QKV
C_QB = C_ZA + A_WIDTH
C_KB = C_QB + B_WIDTH
C_VB = C_KB + B_KV_WIDTH
C_ZB = C_VB + B_KV_WIDTH
C_BD = C_ZB + B_WIDTH
WPREP_TN = 256
W_COLS = C_BD + WPREP_TN

VMEM_LIMIT = 56 * 1024 * 1024

PROJ_TM = 1024
PROJ_CW = 256
PROJ_PARTS = 8
DELTA_CT = 256
DELTA_WAVE = 2
SWA_TQ = 512
SWA_WAVE = 2
STEP_BT = 16


def _dot(a, b):
    return jnp.dot(a, b, preferred_element_type=F32)


def _dot_nt(a, b):
    return lax.dot_general(a, b, (((1,), (1,)), ((), ())), preferred_element_type=F32)


def _silu(x):
    return x * jax.nn.sigmoid(x)


def _softplus(x):
    return jnp.maximum(x, 0.0) + jnp.log1p(jnp.exp(-jnp.abs(x)))


def _lane(shape):
    return lax.broadcasted_iota(jnp.int32, shape, len(shape) - 1)


def _l2norm_heads(y, scale):
    outs = []
    for h in range(y.shape[1] // A_DK):
        xh = y[:, h * A_DK:(h + 1) * A_DK]
        ss = jnp.sum(xh * xh, axis=-1, keepdims=True)
        xn = xh * lax.rsqrt(ss + L2_EPS)
        outs.append(xn * scale if scale != 1.0 else xn)
    return jnp.concatenate(outs, axis=-1)


def _rotary_group(xg, cos, sin_signed):
    lane = _lane(xg.shape)
    swapped = jnp.where((lane % B_HD) < (B_HD // 2),
                        pltpu.roll(xg, LANES - B_HD // 2, axis=1),
                        pltpu.roll(xg, B_HD // 2, axis=1))
    return xg * cos + swapped * sin_signed


def _kv_layouts(kb, vb):
    low = _lane(kb.shape) < B_HD
    kbr = pltpu.roll(kb, B_HD, axis=1)
    vbr = pltpu.roll(vb, B_HD, axis=1)
    return kb, kbr, jnp.where(low, vb, vbr), jnp.where(low, vbr, vb)


def _gate_lanes(bd, alog_row, dt_row):
    lane = _lane(bd.shape)
    g = -jnp.exp(alog_row) * _softplus(bd + dt_row)
    return jnp.where(lane < A_HEADS, jax.nn.sigmoid(bd), g)


def _layer_norm(r, g, b):
    mu = jnp.mean(r, axis=-1, keepdims=True)
    d = r - mu
    var = jnp.mean(d * d, axis=-1, keepdims=True)
    return d * lax.rsqrt(var + LN_EPS) * g + b


def _wprep_kernel(wt_ref, o_ref):
    x = wt_ref[...]
    tail = pl.program_id(0) == pl.num_programs(0) - 1
    row = lax.broadcasted_iota(jnp.int32, x.shape, 0)
    x = jnp.where(jnp.logical_and(tail, row >= 2 * A_HEADS), 0.0, x)
    o_ref[...] = x.T.astype(BF16)


def _wprep(w_t):
    tn = WPREP_TN
    n_a, n_b = OFF_A_BETA // tn, (PROJ_COLS - OFF_B_Q) // tn
    assert n_a * tn == OFF_A_BETA and n_b * tn == PROJ_COLS - OFF_B_Q and OFF_A_BETA + tn <= PROJ_COLS

    def src_row(j):
        return jnp.where(j < n_a, j * tn, jnp.where(j < n_a + n_b, OFF_B_Q + (j - n_a) * tn, OFF_A_BETA))

    return pl.pallas_call(
        _wprep_kernel,
        grid=(n_a + n_b + 1,),
        in_specs=[pl.BlockSpec((pl.Element(tn), pl.Element(D_MODEL)),
                               lambda j: (pl.multiple_of(src_row(j), 8), 0))],
        out_specs=pl.BlockSpec((D_MODEL, tn), lambda j: (0, j)),
        out_shape=jax.ShapeDtypeStruct((D_MODEL, W_COLS), BF16),
        compiler_params=pltpu.CompilerParams(dimension_semantics=("arbitrary",),
                                             vmem_limit_bytes=VMEM_LIMIT),
        name="wprep",
    )(w_t)


def _ada_kernel(c_ref, w_ref, b_ref, o_ref):
    o_ref[...] = _dot(c_ref[...].astype(BF16), w_ref[...].astype(BF16)) + b_ref[...]


def _ada(c_all, w_ada, b_ada):
    rows = c_all.shape[0]
    tn = 768
    return pl.pallas_call(
        _ada_kernel,
        grid=(3 * D_MODEL // tn,),
        in_specs=[pl.BlockSpec((rows, D_MODEL), lambda j: (0, 0)),
                  pl.BlockSpec((D_MODEL, tn), lambda j: (0, j)),
                  pl.BlockSpec((1, tn), lambda j: (0, j))],
        out_specs=pl.BlockSpec((rows, tn), lambda j: (0, j)),
        out_shape=jax.ShapeDtypeStruct((rows, 3 * D_MODEL), F32),
        compiler_params=pltpu.CompilerParams(dimension_semantics=("arbitrary",),
                                             vmem_limit_bytes=VMEM_LIMIT),
        name="ada",
    )(c_all, w_ada, b_ada)


def _proj_kernel(x_ref, mod_ref, w_ref, cw_ref, alog_ref, dt_ref, cos_ref, sin_ref,
                 qkv_ref, za_ref, gb_ref, qb_ref, kvl_ref, zb_ref, cst_ref, kbl_ref, vbl_ref, ubuf):
    tm = x_ref.shape[1]
    t = pl.program_id(1)

    @pl.when(t == 0)
    def _():
        ubuf[...] = jnp.zeros(ubuf.shape, F32)

    brow = pl.ds(pl.program_id(0), 1)
    shift = mod_ref[brow, 0:D_MODEL]
    scale = mod_ref[brow, D_MODEL:2 * D_MODEL]

    rp = tm // PROJ_PARTS
    cw = PROJ_CW
    sub = lax.broadcasted_iota(jnp.int32, (rp // 8, 8, cw), 1)
    pieces = [slice(c0, c0 + cw) for c0 in range(0, A_QKV, cw)]

    def part(r0):
        rs = slice(r0, r0 + rp)
        h = (x_ref[0, rs, :] * (1.0 + scale) + shift).astype(BF16)

        def conv_epilogue(cs, u):
            gi = cs.start // A_WIDTH
            groups = jnp.concatenate([ubuf[:, cs], u], axis=0).reshape(rp // 8 + 1, 8, cw)
            acc = None
            for j in range(CONV_W - 1, 0, -1):
                rot = pltpu.roll(groups, j, axis=1)
                term = (jnp.where(sub < j, rot[:-1], rot[1:]).reshape(rp, cw)
                        * cw_ref[CONV_W - 1 - j:CONV_W - j, cs])
                acc = term if acc is None else acc + term
            y = _silu(acc + u * cw_ref[CONV_W - 1:CONV_W, cs])
            if gi == 0:
                y = _l2norm_heads(y, A_DK ** -0.5)
            elif gi == 1:
                y = _l2norm_heads(y, 1.0)
            qkv_ref[0, rs, cs] = y
            ubuf[:, cs] = u[rp - 8:rp]
            if r0 + rp == tm:
                cst_ref[0, :, cs] = u[rp - (CONV_W - 1):rp]

        pending = _dot(h, w_ref[:, pieces[0]])
        for i, cs in enumerate(pieces):
            u = pending
            if i + 1 < len(pieces):
                pending = _dot(h, w_ref[:, pieces[i + 1]])
            conv_epilogue(cs, u)

        za_ref[0, rs, :] = _dot(h, w_ref[:, C_ZA:C_ZA + A_WIDTH])
        gb_ref[0, rs, :] = _gate_lanes(_dot(h, w_ref[:, C_BD:C_BD + LANES]), alog_ref[...], dt_ref[...])
        cos = cos_ref[rs, :]
        sin = sin_ref[rs, :]
        uq = _dot(h, w_ref[:, C_QB:C_QB + B_WIDTH])
        for g in range(B_WIDTH // LANES):
            qb_ref[0, rs, g * LANES:(g + 1) * LANES] = (
                _rotary_group(uq[:, g * LANES:(g + 1) * LANES], cos, sin) * (B_HD ** -0.5 * LOG2E)).astype(BF16)
        ukv = _dot(h, w_ref[:, C_KB:C_KB + 2 * LANES])
        kb = _rotary_group(ukv[:, 0:LANES], cos, sin)
        vb = ukv[:, LANES:2 * LANES]
        for j, val in enumerate(_kv_layouts(kb, vb)):
            kvl_ref[0, rs, j * LANES:(j + 1) * LANES] = val.astype(BF16)
        zb_ref[0, rs, :] = _dot(h, w_ref[:, C_ZB:C_ZB + B_WIDTH])
        return kb, vb

    for r0 in range(0, tm, rp):
        kb, vb = part(r0)

    @pl.when(t == pl.num_programs(1) - 1)
    def _():
        kbl_ref[0] = kb[rp - WINDOW:rp]
        vbl_ref[0] = vb[rp - WINDOW:rp]


def _proj(x, mod, mod_row0, w_r, conv_w, alog_row, dt_row, cos_t, sin_t):
    bsz, t, _ = x.shape
    tm = PROJ_TM
    row = lambda w: pl.BlockSpec((1, tm, w), lambda b, i: (b, i, 0))
    const2 = lambda s: pl.BlockSpec(s, lambda b, i: (0, 0))
    per_b = lambda r, w: pl.BlockSpec((1, r, w), lambda b, i: (b, 0, 0))
    wide = lambda w, dt=F32: jax.ShapeDtypeStruct((bsz, t, w), dt)
    return pl.pallas_call(
        _proj_kernel,
        grid=(bsz, t // tm),
        in_specs=[row(D_MODEL),
                  pl.BlockSpec((8, 3 * D_MODEL), lambda b, i: (mod_row0 // 8, 0)),
                  pl.BlockSpec((D_MODEL, W_COLS), lambda b, i: (0, 0), pipeline_mode=pl.Buffered(1)),
                  const2((CONV_W, A_QKV)),
                  const2((1, LANES)), const2((1, LANES)),
                  pl.BlockSpec((tm, LANES), lambda b, i: (i, 0)),
                  pl.BlockSpec((tm, LANES), lambda b, i: (i, 0))],
        out_specs=[row(A_QKV), row(A_WIDTH), row(LANES), row(B_WIDTH), row(4 * LANES), row(B_WIDTH),
                   per_b(CONV_W - 1, A_QKV), per_b(WINDOW, LANES), per_b(WINDOW, LANES)],
        out_shape=[wide(A_QKV), wide(A_WIDTH), wide(LANES), wide(B_WIDTH, BF16), wide(4 * LANES, BF16),
                   wide(B_WIDTH),
                   jax.ShapeDtypeStruct((bsz, CONV_W - 1, A_QKV), F32),
                   jax.ShapeDtypeStruct((bsz, WINDOW, LANES), F32),
                   jax.ShapeDtypeStruct((bsz, WINDOW, LANES), F32)],
        scratch_shapes=[pltpu.VMEM((8, A_QKV), F32)],
        compiler_params=pltpu.CompilerParams(dimension_semantics=("arbitrary", "arbitrary"),
                                             vmem_limit_bytes=VMEM_LIMIT),
        name="proj",
    )(x, mod, w_r, conv_w, alog_row, dt_row, cos_t, sin_t)


def _delta_kernel(q_ref, k_ref, v_ref, gb_ref, za_ref, na_ref, oa_ref, st_ref,
                  s_scr, wq_s, ut_s, akd_s, gl_s):
    bsz, ct = q_ref.shape[0], q_ref.shape[1]
    nch = ct // CHUNK
    t = pl.program_id(0)
    wslot = t % 2
    rslot = 1 - wslot

    @pl.when(t == 0)
    def _():
        s_scr[...] = jnp.zeros(s_scr.shape, F32)
        wq_s[...] = jnp.zeros(wq_s.shape, BF16)
        ut_s[...] = jnp.zeros(ut_s.shape, F32)
        akd_s[...] = jnp.zeros(akd_s.shape, BF16)
        gl_s[...] = jnp.zeros(gl_s.shape, F32)

    units = [(b, c, h) for b in range(bsz) for c in range(nch) for h in range(A_HEADS)]
    uid = {u_: i for i, u_ in enumerate(units)}
    rows = lambda c: slice(c * CHUNK, (c + 1) * CHUNK)
    lanes = lambda h: slice(h * A_DK, (h + 1) * A_DK)
    na = na_ref[...]

    s_cur = {(b, h): s_scr[b * A_HEADS + h] for b in range(bsz) for h in range(A_HEADS)}
    ws, uu = {}, {}

    def rec_ws(c):
        for b in range(bsz):
            for h in range(A_HEADS):
                i = uid[b, c, h]
                ws[b, h] = _dot(wq_s[rslot, i], s_cur[b, h].astype(BF16))
                uu[b, h] = (ut_s[rslot, i] - ws[b, h][:CHUNK]).astype(BF16)

    def rec_ou(c):
        zpad = jnp.zeros((CHUNK, A_DV), BF16)
        for b in range(bsz):
            u_bd = jnp.concatenate(
                [jnp.concatenate([uu[b, h] if hh == h else zpad for hh in range(A_HEADS)], axis=-1)
                 for h in range(A_HEADS)], axis=0)
            ou = _dot(akd_s[rslot, b * nch + c], u_bd)
            for h in range(A_HEADS):
                o = ws[b, h][CHUNK:] + ou[:CHUNK, lanes(h)]
                s_cur[b, h] = gl_s[rslot, uid[b, c, h]] * s_cur[b, h] + ou[CHUNK:, lanes(h)]
                on = o * lax.rsqrt(jnp.mean(o * o, axis=-1, keepdims=True) + RMS_EPS) * na
                oa_ref[b, rows(c), lanes(h)] = (on * _silu(za_ref[b, rows(c), lanes(h)])).astype(BF16)

    rec_stages = []
    for c in range(nch):
        rec_stages += [lambda c=c: rec_ws(c), lambda c=c: rec_ou(c)]

    def run_rec(n_left_after):
        while rec_stages and len(rec_stages) > n_left_after:
            rec_stages.pop(0)()

    pk = A_HEADS * CHUNK
    low = _lane((CHUNK, LANES)) < CHUNK
    low_row = _lane((1, LANES)) < CHUNK
    ti_p = lax.broadcasted_iota(jnp.int32, (CHUNK, pk), 0)
    ii_p = _lane((CHUNK, pk)) % CHUNK
    zero64 = jnp.zeros((CHUNK, LANES), BF16)

    def pack(parts):
        return jnp.concatenate([jnp.where(low, parts[0], parts[1]), jnp.where(low, parts[2], parts[3])], axis=-1)

    def block_diag(x16):
        blocks = []
        for h in range(A_HEADS):
            pair, first = h // 2, h % 2 == 0
            piece = jnp.where(low if first else jnp.logical_not(low), x16[:, pair * LANES:(pair + 1) * LANES], zero64)
            blocks.append(jnp.concatenate([piece, zero64] if pair == 0 else [zero64, piece], axis=-1))
        return jnp.concatenate(blocks, axis=0)

    zrhs = jnp.zeros((CHUNK, 2 * A_DK), BF16)
    n_rec = len(rec_stages)
    n_slots = 8 * (bsz // DELTA_WAVE)
    done = [0]

    def stage_done():
        done[0] += 1
        run_rec(n_rec - (done[0] * n_rec) // n_slots)

    def decay_terms(b, beta, g_col, g_last, eg, dec_p, beta_p):
        gbv = gb_ref[b]
        rin = lax.broadcasted_iota(jnp.int32, gbv.shape, 0) % CHUNK
        gcs = gbv
        s = 1
        while s < CHUNK:
            gcs = gcs + jnp.where(rin >= s, pltpu.roll(gcs, s, axis=0), 0.0)
            s *= 2
        gcs_t = gcs.T
        for c in range(nch):
            r0 = c * CHUNK
            pair_lanes = slice((c // 2) * LANES, (c // 2 + 1) * LANES)
            g_rows = []
            for h in range(A_HEADS):
                u_ = (b, c, h)
                beta[u_] = jnp.broadcast_to(gbv[rows(c), h:h + 1], (CHUNK, A_DK))
                g_col[u_] = jnp.broadcast_to(gcs[rows(c), A_HEADS + h:A_HEADS + h + 1], (CHUNK, A_DK))
                g_last[u_] = gcs[r0 + CHUNK - 1:r0 + CHUNK, A_HEADS + h:A_HEADS + h + 1]
                eg[u_] = jnp.exp(g_col[u_])
                g_row = gcs_t[A_HEADS + h:A_HEADS + h + 1, pair_lanes]
                g_rows.append(g_row if c % 2 == h % 2 else pltpu.roll(g_row, CHUNK, axis=1))
            g_row_p = jnp.concatenate([jnp.where(low_row, g_rows[0], g_rows[1]),
                                       jnp.where(low_row, g_rows[2], g_rows[3])], axis=-1)
            g_col_p = pack([g_col[b, c, h] for h in range(A_HEADS)])
            dec_p[b, c] = jnp.exp(jnp.where(ti_p >= ii_p, g_col_p - g_row_p, -jnp.inf))
            beta_p[b, c] = pack([beta[b, c, h] for h in range(A_HEADS)])

    def prepare(bs):
        groups_b = [(b, c) for b in bs for c in range(nch)]
        beta, g_col, g_last, eg, dec_p, beta_p = {}, {}, {}, {}, {}, {}
        for b in bs:
            decay_terms(b, beta, g_col, g_last, eg, dec_p, beta_p)

        nmat = {}
        for (b, c) in groups_b:
            k16 = k_ref[b, rows(c), :].astype(BF16)
            q16 = q_ref[b, rows(c), :].astype(BF16)
            k_heads = jnp.concatenate(
                [jnp.concatenate([k16[:, lanes(h)] if hh == h else zero64 for hh in range(A_HEADS)], axis=-1)
                 for h in range(A_HEADS)], axis=0)
            kq = _dot_nt(jnp.concatenate([k16, q16], axis=0), k_heads)
            nmat[b, c] = -(beta_p[b, c] * kq[:CHUNK] * jnp.where(ti_p > ii_p, dec_p[b, c], 0.0))
            akd_s[wslot, b * nch + c, 0:CHUNK, :] = (kq[CHUNK:] * dec_p[b, c]).astype(BF16)
        stage_done()

        rsum = dict(nmat)
        pw16 = {g_: nmat[g_].astype(BF16) for g_ in groups_b}
        pw = {g_: _dot(pw16[g_], block_diag(pw16[g_])) for g_ in groups_b}
        stage_done()
        for step in range(1, 6):
            last = step == 5
            pw16 = {g_: pw[g_].astype(BF16) for g_ in groups_b}
            rp = {}
            for g_ in groups_b:
                r16 = rsum[g_].astype(BF16)
                rp[g_] = _dot(r16 if last else jnp.concatenate([r16, pw16[g_]], axis=0), block_diag(pw16[g_]))
            for g_ in groups_b:
                rsum[g_] = rsum[g_] + pw[g_] + rp[g_][:CHUNK]
                if not last:
                    pw[g_] = rp[g_][CHUNK:]
            stage_done()

        for (b, c) in groups_b:
            for h in range(A_HEADS):
                u_ = (b, c, h)
                i = uid[u_]
                kc = k_ref[b, rows(c), lanes(h)]
                rhs = jnp.concatenate([(beta[u_] * eg[u_]) * kc, beta[u_] * v_ref[b, rows(c), lanes(h)]],
                                      axis=-1)
                rhs16 = rhs.astype(BF16)
                rhs_rows = jnp.concatenate([rhs16 if hh == h else zrhs for hh in range(A_HEADS)], axis=0)
                sol = rhs + _dot(rsum[b, c].astype(BF16), rhs_rows)
                wq_s[wslot, i] = jnp.concatenate([sol[:, :A_DK], eg[u_] * q_ref[b, rows(c), lanes(h)]],
                                                 axis=0).astype(BF16)
                ut_s[wslot, i] = sol[:, A_DK:]
                gl_s[wslot, i] = jnp.broadcast_to(jnp.exp(g_last[u_]), (1, A_DV))
        for (b, c) in groups_b:
            kd = [jnp.exp(g_last[b, c, h] - g_col[b, c, h]) * k_ref[b, rows(c), lanes(h)]
                  for h in range(A_HEADS)]
            for p in range(A_HEADS // 2):
                akd_s[wslot, b * nch + c, CHUNK:, p * LANES:(p + 1) * LANES] = (
                    jnp.concatenate([kd[2 * p], kd[2 * p + 1]], axis=0).T.astype(BF16))
        stage_done()

    for b0 in range(0, bsz, DELTA_WAVE):
        prepare(range(b0, b0 + DELTA_WAVE))
    run_rec(0)

    for b in range(bsz):
        for h in range(A_HEADS):
            s_scr[b * A_HEADS + h] = s_cur[b, h]

    @pl.when(t == pl.num_programs(0) - 1)
    def _():
        for b in range(bsz):
            for h in range(A_HEADS):
                st_ref[b, h] = s_cur[b, h]


def _delta(qkv, gb, za, na_row):
    bsz, t, _ = qkv.shape
    ct = DELTA_CT
    nt = t // ct
    n_units = bsz * (ct // CHUNK) * A_HEADS
    prep = lambda w, j=0: pl.BlockSpec((bsz, ct, w), lambda i: (0, jnp.minimum(i, nt - 1), j))
    rec = lambda w: pl.BlockSpec((bsz, ct, w), lambda i: (0, jnp.maximum(i - 1, 0), 0))
    return pl.pallas_call(
        _delta_kernel,
        grid=(nt + 1,),
        in_specs=[prep(A_WIDTH, 0), prep(A_WIDTH, 1), prep(A_WIDTH, 2), prep(LANES), rec(A_WIDTH),
                  pl.BlockSpec((1, A_DV), lambda i: (0, 0))],
        out_specs=[rec(A_WIDTH),
                   pl.BlockSpec((bsz, A_HEADS, A_DK, A_DV), lambda i: (0, 0, 0, 0))],
        out_shape=[jax.ShapeDtypeStruct((bsz, t, A_WIDTH), BF16),
                   jax.ShapeDtypeStruct((bsz, A_HEADS, A_DK, A_DV), F32)],
        scratch_shapes=[pltpu.VMEM((bsz * A_HEADS, A_DK, A_DV), F32),
                        pltpu.VMEM((2, n_units, 2 * CHUNK, A_DK), BF16),
                        pltpu.VMEM((2, n_units, CHUNK, A_DV), F32),
                        pltpu.VMEM((2, n_units // A_HEADS, CHUNK + A_DK, A_HEADS * CHUNK), BF16),
                        pltpu.VMEM((2, n_units, 1, A_DV), F32)],
        compiler_params=pltpu.CompilerParams(dimension_semantics=("arbitrary",),
                                             vmem_limit_bytes=VMEM_LIMIT),
        name="delta",
    )(qkv, qkv, qkv, gb, za, na_row)


def _swa_out_kernel(sink_ref, qb_ref, kc_ref, kp_ref, krc_ref, krp_ref, v0c_ref, v0p_ref, v1c_ref, v1p_ref,
                    zb_ref, oa_ref, x_ref, gate_ref, w_ref, g_ref, b_ref, y_ref):
    n = pl.program_id(1)
    gate = gate_ref[pl.ds(pl.program_id(0), 1), :]
    tq = qb_ref.shape[1]
    blk = WINDOW
    kx = (jnp.concatenate([kp_ref[0], kc_ref[0]], axis=0), jnp.concatenate([krp_ref[0], krc_ref[0]], axis=0))
    vd = (jnp.concatenate([v0p_ref[0], v0c_ref[0]], axis=0), jnp.concatenate([v1p_ref[0], v1c_ref[0]], axis=0))

    a = lax.broadcasted_iota(jnp.int32, (2 * blk, 2 * blk), 0) % blk
    j = lax.broadcasted_iota(jnp.int32, (2 * blk, 2 * blk), 1)
    rel = a + blk - j
    band = (rel >= 0) & (rel <= WINDOW)
    band_first = band & ((n > 0) | (j >= blk))
    top = lax.broadcasted_iota(jnp.int32, (2 * blk, 1), 0) < blk
    low = _lane((blk, LANES)) < B_HD
    zero = jnp.zeros((blk, LANES), BF16)

    qrows = lambda i: slice(i * blk, (i + 1) * blk)
    krows = lambda i: slice(i * blk, (i + 2) * blk)
    sink = {(kh, half): jnp.where(top, sink_ref[kh * B_GROUP + half] * LOG2E,
                                  sink_ref[kh * B_GROUP + half + 2] * LOG2E)
            for kh in range(B_KV_HEADS) for half in range(2)}
    for i0 in range(0, tq // blk, SWA_WAVE):
        blocks = range(i0, i0 + SWA_WAVE)
        units = [(i, kh, half) for i in blocks for kh in range(B_KV_HEADS) for half in range(2)]
        mix_a = {i: _dot(oa_ref[0, qrows(i), :], w_ref[0:A_WIDTH, :]) for i in blocks}
        sc = {}
        for (i, kh, half) in units:
            qs = []
            for g in range(2):
                grp = kh * 2 + g
                xg = qb_ref[0, qrows(i), grp * LANES:(grp + 1) * LANES]
                qs.append(jnp.where(low if half == 0 else jnp.logical_not(low), xg, zero))
            qz = jnp.concatenate(qs, axis=0)
            sc[i, kh, half] = _dot_nt(qz, kx[0 if kh == half else 1][krows(i)])
        p, den = {}, {}
        for u_ in units:
            i, kh, half = u_
            s_m = jnp.where(band_first if i == 0 else band, sc[u_], -jnp.inf)
            m = jnp.maximum(jnp.max(s_m, axis=-1, keepdims=True), sink[kh, half])
            e = jnp.exp2(s_m - m)
            den[u_] = jnp.sum(e, axis=-1, keepdims=True) + jnp.exp2(sink[kh, half] - m)
            p[u_] = e.astype(BF16)
        pv = {u_: _dot(p[u_], vd[u_[1]][krows(u_[0])]) for u_ in units}
        outs = {u_: pv[u_] / den[u_] for u_ in units}
        for i in blocks:
            ob = []
            for grp in range(B_WIDTH // LANES):
                kh, g = grp // 2, grp % 2
                og = jnp.where(low, outs[i, kh, 0][g * blk:(g + 1) * blk], outs[i, kh, 1][g * blk:(g + 1) * blk])
                ob.append((og * _silu(zb_ref[0, qrows(i), grp * LANES:(grp + 1) * LANES])).astype(BF16))
            mix = mix_a[i] + _dot(jnp.concatenate(ob, axis=-1), w_ref[A_WIDTH:MIX_WIDTH, :])
            r = DEEPNORM_ALPHA * x_ref[0, qrows(i), :] + (1.0 + gate) * mix
            y_ref[0, qrows(i), :] = _layer_norm(r, g_ref[...], b_ref[...])


def _swa_out(sinks, qb, kvl, zb, oa, x, mod, mod_row0, w_out, ln_g, ln_b):
    bsz, t, _ = qb.shape
    tq = SWA_TQ
    per = tq // WINDOW
    cur = lambda w: pl.BlockSpec((1, tq, w), lambda b, i: (b, i, 0))
    kv_cur = lambda j: pl.BlockSpec((1, tq, LANES), lambda b, i: (b, i, j))
    kv_prev = lambda j: pl.BlockSpec((1, WINDOW, LANES), lambda b, i: (b, jnp.maximum(i * per - 1, 0), j))
    const2 = lambda s: pl.BlockSpec(s, lambda b, i: (0, 0))
    return pl.pallas_call(
        _swa_out_kernel,
        grid=(bsz, t // tq),
        in_specs=[pl.BlockSpec(memory_space=pltpu.SMEM), cur(B_WIDTH),
                  kv_cur(0), kv_prev(0), kv_cur(1), kv_prev(1), kv_cur(2), kv_prev(2), kv_cur(3), kv_prev(3),
                  cur(B_WIDTH), cur(A_WIDTH), cur(D_MODEL),
                  pl.BlockSpec((8, D_MODEL), lambda b, i: (mod_row0 // 8, 2)),
                  const2((MIX_WIDTH, D_MODEL)), const2((1, D_MODEL)), const2((1, D_MODEL))],
        out_specs=cur(D_MODEL),
        out_shape=jax.ShapeDtypeStruct((bsz, t, D_MODEL), F32),
        compiler_params=pltpu.CompilerParams(dimension_semantics=("arbitrary", "arbitrary"),
                                             vmem_limit_bytes=VMEM_LIMIT),
        name="swa_out",
    )(sinks, qb, kvl, kvl, kvl, kvl, kvl, kvl, kvl, kvl, zb, oa, x, mod, w_out, ln_g, ln_b)


def _out_kernel(oa_ref, ob_ref, x_ref, gate_ref, w_ref, g_ref, b_ref, y_ref):
    mix = _dot(oa_ref[...], w_ref[0:A_WIDTH, :]) + _dot(ob_ref[...], w_ref[A_WIDTH:MIX_WIDTH, :])
    r = DEEPNORM_ALPHA * x_ref[...] + (1.0 + gate_ref[...]) * mix
    y_ref[...] = _layer_norm(r, g_ref[...], b_ref[...])


def _out(oa, ob, x, mod, w_out, ln_g, ln_b):
    n = x.shape[0]
    full = lambda s: pl.BlockSpec(s, lambda i: (0, 0))
    return pl.pallas_call(
        _out_kernel,
        grid=(1,),
        in_specs=[full((n, A_WIDTH)), full((n, B_WIDTH)), full((n, D_MODEL)),
                  pl.BlockSpec((n, D_MODEL), lambda i: (0, 2)),
                  full((MIX_WIDTH, D_MODEL)), full((1, D_MODEL)), full((1, D_MODEL))],
        out_specs=full((n, D_MODEL)),
        out_shape=jax.ShapeDtypeStruct((n, D_MODEL), F32),
        compiler_params=pltpu.CompilerParams(dimension_semantics=("arbitrary",),
                                             vmem_limit_bytes=VMEM_LIMIT),
        name="out",
    )(oa, ob, x, mod, w_out, ln_g, ln_b)


def _sproj_kernel(x_ref, mod_ref, w_ref, cw_ref, cst_ref, alog_ref, dt_ref, cos_ref, sin_ref,
                  q_ref, k_ref, v_ref, za_ref, gb_ref, qb_ref, kb_ref, vb_ref, zb_ref, ncs_ref):
    shift = mod_ref[:, 0:D_MODEL]
    scale = mod_ref[:, D_MODEL:2 * D_MODEL]
    h = (x_ref[...] * (1.0 + scale) + shift).astype(BF16)

    for gi, o_ref in enumerate((q_ref, k_ref, v_ref)):
        c0 = gi * A_WIDTH
        cs = slice(c0, c0 + A_WIDTH)
        u = _dot(h, w_ref[:, cs])
        acc = cst_ref[0, :, cs] * cw_ref[0:1, cs]
        acc = acc + cst_ref[1, :, cs] * cw_ref[1:2, cs]
        acc = acc + cst_ref[2, :, cs] * cw_ref[2:3, cs]
        acc = acc + u * cw_ref[3:4, cs]
        y = _silu(acc)
        if gi == 0:
            y = _l2norm_heads(y, A_DK ** -0.5)
        elif gi == 1:
            y = _l2norm_heads(y, 1.0)
        o_ref[...] = y
        ncs_ref[0, :, cs] = cst_ref[1, :, cs]
        ncs_ref[1, :, cs] = cst_ref[2, :, cs]
        ncs_ref[2, :, cs] = u

    za_ref[...] = _dot(h, w_ref[:, C_ZA:C_ZA + A_WIDTH])
    gb_ref[...] = _gate_lanes(_dot(h, w_ref[:, C_BD:C_BD + LANES]), alog_ref[...], dt_ref[...])

    cos = cos_ref[...]
    sin = sin_ref[...]
    uq = _dot(h, w_ref[:, C_QB:C_QB + B_WIDTH])
    for g in range(B_WIDTH // LANES):
        qb_ref[:, g * LANES:(g + 1) * LANES] = (
            _rotary_group(uq[:, g * LANES:(g + 1) * LANES], cos, sin) * (B_HD ** -0.5))
    kb_ref[...] = _rotary_group(_dot(h, w_ref[:, C_KB:C_KB + LANES]), cos, sin)
    vb_ref[...] = _dot(h, w_ref[:, C_VB:C_VB + LANES])
    zb_ref[...] = _dot(h, w_ref[:, C_ZB:C_ZB + B_WIDTH])


def _sproj(x, mod_s, w_r, conv_w, cst, alog_row, dt_row, cos_row, sin_row):
    n = x.shape[0]
    full = lambda s: pl.BlockSpec(s, lambda i: (0,) * len(s))
    wide = lambda w: jax.ShapeDtypeStruct((n, w), F32)
    return pl.pallas_call(
        _sproj_kernel,
        grid=(1,),
        in_specs=[full((n, D_MODEL)), pl.BlockSpec((n, 3 * D_MODEL), lambda i: (0, 0)),
                  full((D_MODEL, W_COLS)),
                  full((CONV_W, A_QKV)), full((CONV_W - 1, n, A_QKV)),
                  full((1, LANES)), full((1, LANES)), full((1, LANES)), full((1, LANES))],
        out_specs=[full((n, A_WIDTH)), full((n, A_WIDTH)), full((n, A_WIDTH)), full((n, A_WIDTH)),
                   full((n, LANES)), full((n, B_WIDTH)), full((n, LANES)), full((n, LANES)),
                   full((n, B_WIDTH)), full((CONV_W - 1, n, A_QKV))],
        out_shape=[wide(A_WIDTH), wide(A_WIDTH), wide(A_WIDTH), wide(A_WIDTH), wide(LANES),
                   wide(B_WIDTH), wide(LANES), wide(LANES), wide(B_WIDTH),
                   jax.ShapeDtypeStruct((CONV_W - 1, n, A_QKV), F32)],
        compiler_params=pltpu.CompilerParams(dimension_semantics=("arbitrary",),
                                             vmem_limit_bytes=VMEM_LIMIT),
        name="sproj",
    )(x, mod_s, w_r, conv_w, cst, alog_row, dt_row, cos_row, sin_row)


def _sstep_kernel(sink_ref, q_ref, k_ref, v_ref, gb_ref, za_ref, na_ref, st_ref,
                  qb_ref, kn_ref, vn_ref, zb_ref, ck_ref, cv_ref,
                  oa_ref, ob_ref, nst_ref, nck_ref, ncv_ref,
                  o_scr, ob_scr):
    bt = q_ref.shape[0]
    gbv = gb_ref[...]

    pick = (lax.broadcasted_iota(jnp.int32, (bt, bt * A_DV), 1) // A_DV
            == lax.broadcasted_iota(jnp.int32, (bt, bt * A_DV), 0))
    pick = jnp.where(pick, 1.0, 0.0).astype(BF16)
    for h in range(A_HEADS):
        hs = slice(h * A_DK, (h + 1) * A_DK)
        q_rep = _dot(q_ref[:, hs].T.astype(BF16), pick)
        k_rep = _dot(k_ref[:, hs].T.astype(BF16), pick)
        for bb in range(bt):
            eg = jnp.exp(gbv[bb:bb + 1, A_HEADS + h:A_HEADS + h + 1])
            beta = gbv[bb:bb + 1, h:h + 1]
            kcol = k_rep[:, bb * A_DV:(bb + 1) * A_DV]
            qcol = q_rep[:, bb * A_DV:(bb + 1) * A_DV]
            s1 = eg * st_ref[bb, h]
            pred = jnp.sum(kcol * s1, axis=0, keepdims=True)
            upd = beta * (v_ref[bb:bb + 1, hs] - pred)
            s2 = s1 + kcol * upd
            nst_ref[bb, h] = s2
            o_scr[bb:bb + 1, hs] = jnp.sum(qcol * s2, axis=0, keepdims=True)
    na = na_ref[...]
    for h in range(A_HEADS):
        hs = slice(h * A_DK, (h + 1) * A_DK)
        o = o_scr[:, hs]
        on = o * lax.rsqrt(jnp.mean(o * o, axis=-1, keepdims=True) + RMS_EPS) * na
        oa_ref[:, hs] = (on * _silu(za_ref[:, hs])).astype(BF16)

    row8 = lax.broadcasted_iota(jnp.int32, (B_HEADS, LANES), 0)
    lane8 = _lane((B_HEADS, LANES))
    own_half = (lane8 >= B_HD) == (row8 >= B_GROUP)
    rcol = lax.broadcasted_iota(jnp.int32, (B_HEADS, 1), 0)
    sink = jnp.zeros((B_HEADS, 1), F32)
    for r in range(B_HEADS):
        sink = jnp.where(rcol == r, sink_ref[r], sink)
    qv = qb_ref[...]
    qv_r = jnp.concatenate([pltpu.roll(qv[:, g * LANES:(g + 1) * LANES], B_HD, axis=1)
                            for g in range(B_WIDTH // LANES)], axis=-1)
    kn_t = kn_ref[...].T
    vn_t = vn_ref[...].T
    newest = _lane((LANES, WINDOW)) == WINDOW - 1
    qzs, scs = [], []
    for bb in range(bt):
        qz = jnp.zeros((B_HEADS, LANES), F32)
        for r in range(B_HEADS):
            grp, half, kh = r // 2, r % 2, r // B_GROUP
            src = qv if half == kh else qv_r
            qz = jnp.where(row8 == r, src[bb:bb + 1, grp * LANES:(grp + 1) * LANES], qz)
        qzs.append(jnp.where(own_half, qz, 0.0))
    for bb in range(bt):
        scs.append(_dot(qzs[bb], ck_ref[bb]))
    ps, pnews, dens = [], [], []
    for bb in range(bt):
        sc_new = jnp.sum(qzs[bb] * kn_ref[bb:bb + 1, :], axis=-1, keepdims=True)
        m = jnp.maximum(jnp.maximum(jnp.max(scs[bb], axis=-1, keepdims=True), sc_new), sink)
        p = jnp.exp(scs[bb] - m)
        p_new = jnp.exp(sc_new - m)
        ps.append(p)
        pnews.append(p_new)
        dens.append(jnp.sum(p, axis=-1, keepdims=True) + p_new + jnp.exp(sink - m))
    pvs = [_dot_nt(ps[bb], cv_ref[bb]) for bb in range(bt)]
    for bb in range(bt):
        o = (pvs[bb] + pnews[bb] * vn_ref[bb:bb + 1, :]) / dens[bb]
        o = jnp.where(own_half, o, 0.0)
        ob_scr[bb * B_HEADS:(bb + 1) * B_HEADS, :] = o + pltpu.roll(o, B_HD, axis=1)
    for bb in range(bt):
        nck_ref[bb] = jnp.where(newest, kn_t[:, bb:bb + 1], pltpu.roll(ck_ref[bb], WINDOW - 1, axis=1))
        ncv_ref[bb] = jnp.where(newest, vn_t[:, bb:bb + 1], pltpu.roll(cv_ref[bb], WINDOW - 1, axis=1))
    low = _lane((bt, LANES)) < B_HD
    for grp in range(B_WIDTH // LANES):
        even = ob_scr[pl.ds(2 * grp, bt, stride=B_HEADS), :]
        odd = ob_scr[pl.ds(2 * grp + 1, bt, stride=B_HEADS), :]
        gs = slice(grp * LANES, (grp + 1) * LANES)
        ob_ref[:, gs] = (jnp.where(low, even, odd) * _silu(zb_ref[:, gs])).astype(BF16)


def _sstep(sinks, q, k, v, gb, za, na_row, state, qb, kn, vn, zb, ck, cv):
    n = q.shape[0]
    bt = STEP_BT
    row = lambda w: pl.BlockSpec((bt, w), lambda i: (i, 0))
    st_spec = pl.BlockSpec((bt, A_HEADS, A_DK, A_DV), lambda i: (i, 0, 0, 0))
    c_spec = pl.BlockSpec((bt, WINDOW, LANES), lambda i: (i, 0, 0))
    return pl.pallas_call(
        _sstep_kernel,
        grid=(n // bt,),
        in_specs=[pl.BlockSpec(memory_space=pltpu.SMEM),
                  row(A_WIDTH), row(A_WIDTH), row(A_WIDTH), row(LANES), row(A_WIDTH),
                  pl.BlockSpec((1, A_DV), lambda i: (0, 0)), st_spec,
                  row(B_WIDTH), row(LANES), row(LANES), row(B_WIDTH), c_spec, c_spec],
        out_specs=[row(A_WIDTH), row(B_WIDTH), st_spec, c_spec, c_spec],
        out_shape=[jax.ShapeDtypeStruct((n, A_WIDTH), BF16),
                   jax.ShapeDtypeStruct((n, B_WIDTH), BF16),
                   jax.ShapeDtypeStruct((n, A_HEADS, A_DK, A_DV), F32),
                   jax.ShapeDtypeStruct((n, WINDOW, LANES), F32),
                   jax.ShapeDtypeStruct((n, WINDOW, LANES), F32)],
        scratch_shapes=[pltpu.VMEM((bt, A_WIDTH), F32), pltpu.VMEM((bt * B_HEADS, LANES), F32)],
        compiler_params=pltpu.CompilerParams(dimension_semantics=("arbitrary",),
                                             vmem_limit_bytes=VMEM_LIMIT),
        name="sstep",
    )(sinks, q, k, v, gb, za, na_row, state, qb, kn, vn, zb, ck, cv)


def _rope_tables(pos):
    half = B_HD // 2
    inv = 1.0 / (ROPE_THETA ** (np.arange(half, dtype=np.float64) / half))
    ang = np.asarray(pos, np.float64)[:, None] * inv[None, :]
    cos, sin = np.cos(ang), np.sin(ang)
    reps = LANES // B_HD
    return (jnp.asarray(np.tile(np.concatenate([cos, cos], -1), (1, reps)), F32),
            jnp.asarray(np.tile(np.concatenate([-sin, sin], -1), (1, reps)), F32))


def _pad_row(vec, offset):
    return jnp.zeros((1, LANES), F32).at[0, offset:offset + vec.shape[0]].set(vec.astype(F32))


def _layer(x_prompt, x_sample, state_conv, state_delta, cache_k, cache_v, c_prompt, c_sample,
           w_ada, b_ada, w_in, conv_w, a_log, dt_bias, norm_a, sinks, w_out, ln_g, ln_b):
    bsz, seq, _ = x_prompt.shape
    n_s = x_sample.shape[0]

    w_r = _wprep(jnp.swapaxes(w_in, 0, 1))
    w_o = w_out.astype(BF16)
    alog_row = _pad_row(a_log, A_HEADS)
    dt_row = _pad_row(dt_bias, A_HEADS)
    na_row = norm_a.reshape(1, A_DV)
    g_row = ln_g.reshape(1, D_MODEL)
    b_row = ln_b.reshape(1, D_MODEL)

    assert n_s % 8 == 0 and bsz <= 8
    c_all = jnp.concatenate([c_sample, c_prompt, jnp.zeros((8 - bsz, D_MODEL), F32)], axis=0)
    mod = _ada(c_all, w_ada, b_ada.reshape(1, 3 * D_MODEL))

    cos_p, sin_p = _rope_tables(np.arange(seq))
    (qkv, za, gb, qb, kvl, zb, conv_p, kb_last, vb_last) = _proj(
        x_prompt, mod, n_s, w_r, conv_w, alog_row, dt_row, cos_p, sin_p)
    oa, delta_p = _delta(qkv, gb, za, na_row)
    y_p = _swa_out(sinks, qb, kvl, zb, oa, x_prompt, mod, n_s, w_o, g_row, b_row)
    swa_k_p = kb_last.reshape(bsz, WINDOW, B_KV_HEADS, B_HD)
    swa_v_p = vb_last.reshape(bsz, WINDOW, B_KV_HEADS, B_HD)

    cos_s, sin_s = _rope_tables(np.array([PAST_LEN]))
    xs = x_sample.reshape(n_s, D_MODEL)
    cst = jnp.transpose(state_conv, (1, 0, 2))
    sq, sk, sv, sza, sgb, sqb, skn, svn, szb, ncs = _sproj(xs, mod, w_r, conv_w, cst, alog_row, dt_row,
                                                           cos_s, sin_s)
    soa, sob, delta_s, nck, ncv = _sstep(sinks, sq, sk, sv, sgb, sza, na_row, state_delta,
                                         sqb, skn, svn, szb,
                                         jnp.swapaxes(cache_k.reshape(n_s, WINDOW, LANES), 1, 2),
                                         jnp.swapaxes(cache_v.reshape(n_s, WINDOW, LANES), 1, 2))
    y_s = _out(soa, sob, xs, mod, w_o, g_row, b_row)
    conv_s = jnp.transpose(ncs, (1, 0, 2))
    unpack = lambda c: jnp.swapaxes(c, 1, 2).reshape(n_s, WINDOW, B_KV_HEADS, B_HD)
    return (y_p, y_s.reshape(n_s, 1, D_MODEL), conv_p, delta_p, swa_k_p, swa_v_p,
            conv_s, delta_s, unpack(nck), unpack(ncv))


def kernel(x_prompt, x_sample, state_conv, state_delta, cache_swa_k, cache_swa_v, c_prompt, c_sample,
           w_ada, b_ada, w_in, conv_w, a_log, dt_bias, norm_a, sinks, w_out, ln_g, ln_b):
    assert w_ada.shape[0] == DEPTH == 1
    outs = _layer(x_prompt, x_sample, state_conv[0], state_delta[0], cache_swa_k[0], cache_swa_v[0],
                  c_prompt, c_sample, w_ada[0], b_ada[0], w_in[0], conv_w[0], a_log[0], dt_bias[0],
                  norm_a[0], sinks[0], w_out[0], ln_g[0], ln_b[0])
    y_p, y_s = outs[0], outs[1]
    return (y_p, y_s) + tuple(o[None] for o in outs[2:])
```

```python
import jax
import jax.numpy as jnp
import numpy as np
from jax import lax
from jax.experimental import pallas as pl
from jax.experimental.pallas import tpu as pltpu

F32 = jnp.float32
BF16 = jnp.bfloat16

D_MODEL = 1024
DEPTH = 1
PAST_LEN = 8192
A_HEADS = 4
A_DK = 128
A_DV = 128
A_WIDTH = A_HEADS * A_DV
A_QKV = 3 * A_WIDTH
CONV_W = 4
CHUNK = 64
B_HEADS = 8
B_KV_HEADS = 2
B_HD = 64
B_GROUP = B_HEADS // B_KV_HEADS
B_WIDTH = B_HEADS * B_HD
B_KV_WIDTH = B_KV_HEADS * B_HD
WINDOW = 128
ROPE_THETA = 10000.0
MIX_WIDTH = A_WIDTH + B_WIDTH
DEEPNORM_ALPHA = (2 * DEPTH) ** 0.25
LOG2E = 1.4426950408889634
LN_EPS = 1e-5
RMS_EPS = 1e-6
L2_EPS = 1e-6

OFF_A_Z = A_QKV
OFF_A_BETA = OFF_A_Z + A_WIDTH
OFF_A_DECAY = OFF_A_BETA + A_HEADS
OFF_B_Q = OFF_A_DECAY + A_HEADS
OFF_B_K = OFF_B_Q + B_WIDTH
OFF_B_V = OFF_B_K + B_KV_WIDTH
OFF_B_Z = OFF_B_V + B_KV_WIDTH
PROJ_COLS = OFF_B_Z + B_WIDTH

LANES = 128
C_QKV = 0
C_ZA = C_QKV + A_QKV
C_QB = C_ZA + A_WIDTH
C_KB = C_QB + B_WIDTH
C_VB = C_KB + B_KV_WIDTH
C_ZB = C_VB + B_KV_WIDTH
C_BD = C_ZB + B_WIDTH
WPREP_TN = 256
W_COLS = C_BD + WPREP_TN

VMEM_LIMIT = 56 * 1024 * 1024

ADA_TN = 1536
PROJ_TM = 512
PROJ_CW = 256
PROJ_PARTS = 4
DELTA_CT = 256
DELTA_WAVE = 2
SWA_TQ = 512
SWA_WAVE = 2
STEP_BT = 16


def _dot(a, b):
    return jnp.dot(a, b, preferred_element_type=F32)


def _dot_nt(a, b):
    return lax.dot_general(a, b, (((1,), (1,)), ((), ())), preferred_element_type=F32)


def _silu(x):
    return x * jax.nn.sigmoid(x)


def _softplus(x):
    return jnp.maximum(x, 0.0) + jnp.log1p(jnp.exp(-jnp.abs(x)))


def _lane(shape):
    return lax.broadcasted_iota(jnp.int32, shape, len(shape) - 1)


def _l2norm_heads(y, scale):
    outs = []
    for h in range(y.shape[1] // A_DK):
        xh = y[:, h * A_DK:(h + 1) * A_DK]
        ss = jnp.sum(xh * xh, axis=-1, keepdims=True)
        xn = xh * lax.rsqrt(ss + L2_EPS)
        outs.append(xn * scale if scale != 1.0 else xn)
    return jnp.concatenate(outs, axis=-1)


def _rotary_group(xg, cos, sin_signed):
    lane = _lane(xg.shape)
    swapped = jnp.where((lane % B_HD) < (B_HD // 2),
                        pltpu.roll(xg, LANES - B_HD // 2, axis=1),
                        pltpu.roll(xg, B_HD // 2, axis=1))
    return xg * cos + swapped * sin_signed


def _kv_layouts(kb, vb):
    low = _lane(kb.shape) < B_HD
    kbr = pltpu.roll(kb, B_HD, axis=1)
    vbr = pltpu.roll(vb, B_HD, axis=1)
    return kb, kbr, jnp.where(low, vb, vbr), jnp.where(low, vbr, vb)


def _gate_lanes(bd, alog_row, dt_row):
    lane = _lane(bd.shape)
    g = -jnp.exp(alog_row) * _softplus(bd + dt_row)
    return jnp.where(lane < A_HEADS, jax.nn.sigmoid(bd), g)


def _layer_norm(r, g, b):
    mu = jnp.mean(r, axis=-1, keepdims=True)
    d = r - mu
    var = jnp.mean(d * d, axis=-1, keepdims=True)
    return d * lax.rsqrt(var + LN_EPS) * g + b


def _wprep_kernel(wt_ref, o_ref):
    x = wt_ref[...]
    tail = pl.program_id(0) == pl.num_programs(0) - 1
    row = lax.broadcasted_iota(jnp.int32, x.shape, 0)
    x = jnp.where(jnp.logical_and(tail, row >= 2 * A_HEADS), 0.0, x)
    o_ref[...] = x.T.astype(BF16)


def _wprep(w_t):
    tn = WPREP_TN
    n_a, n_b = OFF_A_BETA // tn, (PROJ_COLS - OFF_B_Q) // tn
    assert n_a * tn == OFF_A_BETA and n_b * tn == PROJ_COLS - OFF_B_Q and OFF_A_BETA + tn <= PROJ_COLS

    def src_row(j):
        return jnp.where(j < n_a, j * tn, jnp.where(j < n_a + n_b, OFF_B_Q + (j - n_a) * tn, OFF_A_BETA))

    return pl.pallas_call(
        _wprep_kernel,
        grid=(n_a + n_b + 1,),
        in_specs=[pl.BlockSpec((pl.Element(tn), pl.Element(D_MODEL)),
                               lambda j: (pl.multiple_of(src_row(j), 8), 0))],
        out_specs=pl.BlockSpec((D_MODEL, tn), lambda j: (0, j)),
        out_shape=jax.ShapeDtypeStruct((D_MODEL, W_COLS), BF16),
        compiler_params=pltpu.CompilerParams(dimension_semantics=("arbitrary",),
                                             vmem_limit_bytes=VMEM_LIMIT),
        name="wprep",
    )(w_t)


def _ada_kernel(cs_ref, cp_ref, w_ref, b_ref, o_ref):
    n_s, n_p = cs_ref.shape[0], cp_ref.shape[0]
    w = w_ref[...].astype(BF16)
    o_ref[0:n_s, :] = _dot(cs_ref[...].astype(BF16), w) + b_ref[...]
    o_ref[n_s:n_s + 8, :] = jnp.zeros((8, o_ref.shape[1]), F32)
    o_ref[n_s:n_s + n_p, :] = _dot(cp_ref[...].astype(BF16), w) + b_ref[...]


def _ada(c_sample, c_prompt, w_ada, b_ada):
    n_s, n_p = c_sample.shape[0], c_prompt.shape[0]
    rows = n_s + 8
    tn = ADA_TN
    return pl.pallas_call(
        _ada_kernel,
        grid=(3 * D_MODEL // tn,),
        in_specs=[pl.BlockSpec((n_s, D_MODEL), lambda j: (0, 0)),
                  pl.BlockSpec((n_p, D_MODEL), lambda j: (0, 0)),
                  pl.BlockSpec((D_MODEL, tn), lambda j: (0, j)),
                  pl.BlockSpec((1, tn), lambda j: (0, j))],
        out_specs=pl.BlockSpec((rows, tn), lambda j: (0, j)),
        out_shape=jax.ShapeDtypeStruct((rows, 3 * D_MODEL), F32),
        compiler_params=pltpu.CompilerParams(dimension_semantics=("arbitrary",),
                                             vmem_limit_bytes=VMEM_LIMIT),
        name="ada",
    )(c_sample, c_prompt, w_ada, b_ada)


def _proj_kernel(x_ref, mod_ref, w_ref, cw_ref, alog_ref, dt_ref, cos_ref, sin_ref,
                 qkv_ref, za_ref, gb_ref, qb_ref, kvl_ref, zb_ref, cst_ref, kbl_ref, vbl_ref, ubuf):
    tm = x_ref.shape[1]
    t = pl.program_id(1)

    @pl.when(t == 0)
    def _():
        ubuf[...] = jnp.zeros(ubuf.shape, F32)

    brow = pl.ds(pl.program_id(0), 1)
    shift = mod_ref[brow, 0:D_MODEL]
    scale = mod_ref[brow, D_MODEL:2 * D_MODEL]

    rp = tm // PROJ_PARTS
    cw = PROJ_CW
    sub = lax.broadcasted_iota(jnp.int32, (rp // 8, 8, cw), 1)
    pieces = [slice(c0, c0 + cw) for c0 in range(0, A_QKV, cw)]

    def part(r0):
        rs = slice(r0, r0 + rp)
        h = (x_ref[0, rs, :] * (1.0 + scale) + shift).astype(BF16)

        def conv_epilogue(cs, u):
            gi = cs.start // A_WIDTH
            groups = jnp.concatenate([ubuf[:, cs], u], axis=0).reshape(rp // 8 + 1, 8, cw)
            acc = None
            for j in range(CONV_W - 1, 0, -1):
                rot = pltpu.roll(groups, j, axis=1)
                term = (jnp.where(sub < j, rot[:-1], rot[1:]).reshape(rp, cw)
                        * cw_ref[CONV_W - 1 - j:CONV_W - j, cs])
                acc = term if acc is None else acc + term
            y = _silu(acc + u * cw_ref[CONV_W - 1:CONV_W, cs])
            if gi == 0:
                y = _l2norm_heads(y, A_DK ** -0.5)
            elif gi == 1:
                y = _l2norm_heads(y, 1.0)
            qkv_ref[0, rs, cs] = y
            ubuf[:, cs] = u[rp - 8:rp]
            if r0 + rp == tm:
                cst_ref[0, :, cs] = u[rp - (CONV_W - 1):rp]

        pending = _dot(h, w_ref[:, pieces[0]])
        for i, cs in enumerate(pieces):
            u = pending
            if i + 1 < len(pieces):
                pending = _dot(h, w_ref[:, pieces[i + 1]])
            conv_epilogue(cs, u)

        za_ref[0, rs, :] = _dot(h, w_ref[:, C_ZA:C_ZA + A_WIDTH])
        gb_ref[0, rs, :] = _gate_lanes(_dot(h, w_ref[:, C_BD:C_BD + LANES]), alog_ref[...], dt_ref[...])
        cos = cos_ref[rs, :]
        sin = sin_ref[rs, :]
        uq = _dot(h, w_ref[:, C_QB:C_QB + B_WIDTH])
        for g in range(B_WIDTH // LANES):
            qb_ref[0, rs, g * LANES:(g + 1) * LANES] = (
                _rotary_group(uq[:, g * LANES:(g + 1) * LANES], cos, sin) * (B_HD ** -0.5 * LOG2E)).astype(BF16)
        ukv = _dot(h, w_ref[:, C_KB:C_KB + 2 * LANES])
        kb = _rotary_group(ukv[:, 0:LANES], cos, sin)
        vb = ukv[:, LANES:2 * LANES]
        for j, val in enumerate(_kv_layouts(kb, vb)):
            kvl_ref[0, rs, j * LANES:(j + 1) * LANES] = val.astype(BF16)
        zb_ref[0, rs, :] = _dot(h, w_ref[:, C_ZB:C_ZB + B_WIDTH])
        return kb, vb

    for r0 in range(0, tm, rp):
        kb, vb = part(r0)

    @pl.when(t == pl.num_programs(1) - 1)
    def _():
        kbl_ref[0] = kb[rp - WINDOW:rp]
        vbl_ref[0] = vb[rp - WINDOW:rp]


def _proj(x, mod, mod_row0, w_r, conv_w, alog_row, dt_row, cos_t, sin_t):
    bsz, t, _ = x.shape
    tm = PROJ_TM
    row = lambda w: pl.BlockSpec((1, tm, w), lambda b, i: (b, i, 0))
    const2 = lambda s: pl.BlockSpec(s, lambda b, i: (0, 0))
    per_b = lambda r, w: pl.BlockSpec((1, r, w), lambda b, i: (b, 0, 0))
    wide = lambda w, dt=F32: jax.ShapeDtypeStruct((bsz, t, w), dt)
    return pl.pallas_call(
        _proj_kernel,
        grid=(bsz, t // tm),
        in_specs=[row(D_MODEL),
                  pl.BlockSpec((8, 3 * D_MODEL), lambda b, i: (mod_row0 // 8, 0)),
                  const2((D_MODEL, W_COLS)),
                  const2((CONV_W, A_QKV)),
                  const2((1, LANES)), const2((1, LANES)),
                  pl.BlockSpec((tm, LANES), lambda b, i: (i, 0)),
                  pl.BlockSpec((tm, LANES), lambda b, i: (i, 0))],
        out_specs=[row(A_QKV), row(A_WIDTH), row(LANES), row(B_WIDTH), row(4 * LANES), row(B_WIDTH),
                   per_b(CONV_W - 1, A_QKV), per_b(WINDOW, LANES), per_b(WINDOW, LANES)],
        out_shape=[wide(A_QKV), wide(A_WIDTH), wide(LANES), wide(B_WIDTH, BF16), wide(4 * LANES, BF16),
                   wide(B_WIDTH),
                   jax.ShapeDtypeStruct((bsz, CONV_W - 1, A_QKV), F32),
                   jax.ShapeDtypeStruct((bsz, WINDOW, LANES), F32),
                   jax.ShapeDtypeStruct((bsz, WINDOW, LANES), F32)],
        scratch_shapes=[pltpu.VMEM((8, A_QKV), F32)],
        compiler_params=pltpu.CompilerParams(dimension_semantics=("arbitrary", "arbitrary"),
                                             vmem_limit_bytes=VMEM_LIMIT),
        name="proj",
    )(x, mod, w_r, conv_w, alog_row, dt_row, cos_t, sin_t)


def _delta_kernel(q_ref, k_ref, v_ref, gb_ref, za_ref, na_ref, oa_ref, st_ref,
                  s_scr, wq_s, ut_s, akd_s, gl_s):
    bsz, ct = q_ref.shape[0], q_ref.shape[1]
    nch = ct // CHUNK
    t = pl.program_id(0)
    wslot = t % 2
    rslot = 1 - wslot

    @pl.when(t == 0)
    def _():
        s_scr[...] = jnp.zeros(s_scr.shape, F32)
        wq_s[...] = jnp.zeros(wq_s.shape, BF16)
        ut_s[...] = jnp.zeros(ut_s.shape, F32)
        akd_s[...] = jnp.zeros(akd_s.shape, BF16)
        gl_s[...] = jnp.zeros(gl_s.shape, F32)

    units = [(b, c, h) for b in range(bsz) for c in range(nch) for h in range(A_HEADS)]
    uid = {u_: i for i, u_ in enumerate(units)}
    rows = lambda c: slice(c * CHUNK, (c + 1) * CHUNK)
    lanes = lambda h: slice(h * A_DK, (h + 1) * A_DK)
    na = na_ref[...]

    s_cur = {(b, h): s_scr[b * A_HEADS + h] for b in range(bsz) for h in range(A_HEADS)}
    ws, uu = {}, {}

    def rec_ws(c):
        for b in range(bsz):
            for h in range(A_HEADS):
                i = uid[b, c, h]
                ws[b, h] = _dot(wq_s[rslot, i], s_cur[b, h].astype(BF16))
                uu[b, h] = (ut_s[rslot, i] - ws[b, h][:CHUNK]).astype(BF16)

    def rec_ou(c):
        zpad = jnp.zeros((CHUNK, A_DV), BF16)
        for b in range(bsz):
            u_bd = jnp.concatenate(
                [jnp.concatenate([uu[b, h] if hh == h else zpad for hh in range(A_HEADS)], axis=-1)
                 for h in range(A_HEADS)], axis=0)
            ou = _dot(akd_s[rslot, b * nch + c], u_bd)
            for h in range(A_HEADS):
                o = ws[b, h][CHUNK:] + ou[:CHUNK, lanes(h)]
                s_cur[b, h] = gl_s[rslot, uid[b, c, h]] * s_cur[b, h] + ou[CHUNK:, lanes(h)]
                on = o * lax.rsqrt(jnp.mean(o * o, axis=-1, keepdims=True) + RMS_EPS) * na
                oa_ref[b, rows(c), lanes(h)] = (on * _silu(za_ref[b, rows(c), lanes(h)])).astype(BF16)

    rec_stages = []
    for c in range(nch):
        rec_stages += [lambda c=c: rec_ws(c), lambda c=c: rec_ou(c)]

    def run_rec(n_left_after):
        while rec_stages and len(rec_stages) > n_left_after:
            rec_stages.pop(0)()

    pk = A_HEADS * CHUNK
    low = _lane((CHUNK, LANES)) < CHUNK
    low_row = _lane((1, LANES)) < CHUNK
    ti_p = lax.broadcasted_iota(jnp.int32, (CHUNK, pk), 0)
    ii_p = _lane((CHUNK, pk)) % CHUNK
    zero64 = jnp.zeros((CHUNK, LANES), BF16)

    def pack(parts):
        return jnp.concatenate([jnp.where(low, parts[0], parts[1]), jnp.where(low, parts[2], parts[3])], axis=-1)

    def block_diag(x16):
        blocks = []
        for h in range(A_HEADS):
            pair, first = h // 2, h % 2 == 0
            piece = jnp.where(low if first else jnp.logical_not(low), x16[:, pair * LANES:(pair + 1) * LANES], zero64)
            blocks.append(jnp.concatenate([piece, zero64] if pair == 0 else [zero64, piece], axis=-1))
        return jnp.concatenate(blocks, axis=0)

    zrhs = jnp.zeros((CHUNK, 2 * A_DK), BF16)
    n_rec = len(rec_stages)
    n_slots = 8 * (bsz // DELTA_WAVE)
    done = [0]

    def stage_done():
        done[0] += 1
        run_rec(n_rec - (done[0] * n_rec) // n_slots)

    def decay_terms(b, beta, g_col, g_last, eg, dec_p, beta_p):
        gbv = gb_ref[b]
        rin = lax.broadcasted_iota(jnp.int32, gbv.shape, 0) % CHUNK
        gcs = gbv
        s = 1
        while s < CHUNK:
            gcs = gcs + jnp.where(rin >= s, pltpu.roll(gcs, s, axis=0), 0.0)
            s *= 2
        gcs_t = gcs.T
        for c in range(nch):
            r0 = c * CHUNK
            pair_lanes = slice((c // 2) * LANES, (c // 2 + 1) * LANES)
            g_rows = []
            for h in range(A_HEADS):
                u_ = (b, c, h)
                beta[u_] = jnp.broadcast_to(gbv[rows(c), h:h + 1], (CHUNK, A_DK))
                g_col[u_] = jnp.broadcast_to(gcs[rows(c), A_HEADS + h:A_HEADS + h + 1], (CHUNK, A_DK))
                g_last[u_] = gcs[r0 + CHUNK - 1:r0 + CHUNK, A_HEADS + h:A_HEADS + h + 1]
                eg[u_] = jnp.exp(g_col[u_])
                g_row = gcs_t[A_HEADS + h:A_HEADS + h + 1, pair_lanes]
                g_rows.append(g_row if c % 2 == h % 2 else pltpu.roll(g_row, CHUNK, axis=1))
            g_row_p = jnp.concatenate([jnp.where(low_row, g_rows[0], g_rows[1]),
                                       jnp.where(low_row, g_rows[2], g_rows[3])], axis=-1)
            g_col_p = pack([g_col[b, c, h] for h in range(A_HEADS)])
            dec_p[b, c] = jnp.exp(jnp.where(ti_p >= ii_p, g_col_p - g_row_p, -jnp.inf))
            beta_p[b, c] = pack([beta[b, c, h] for h in range(A_HEADS)])

    def prepare(bs):
        groups_b = [(b, c) for b in bs for c in range(nch)]
        beta, g_col, g_last, eg, dec_p, beta_p = {}, {}, {}, {}, {}, {}
        for b in bs:
            decay_terms(b, beta, g_col, g_last, eg, dec_p, beta_p)

        nmat = {}
        for (b, c) in groups_b:
            k16 = k_ref[b, rows(c), :].astype(BF16)
            q16 = q_ref[b, rows(c), :].astype(BF16)
            k_heads = jnp.concatenate(
                [jnp.concatenate([k16[:, lanes(h)] if hh == h else zero64 for hh in range(A_HEADS)], axis=-1)
                 for h in range(A_HEADS)], axis=0)
            kq = _dot_nt(jnp.concatenate([k16, q16], axis=0), k_heads)
            nmat[b, c] = -(beta_p[b, c] * kq[:CHUNK] * jnp.where(ti_p > ii_p, dec_p[b, c], 0.0))
            akd_s[wslot, b * nch + c, 0:CHUNK, :] = (kq[CHUNK:] * dec_p[b, c]).astype(BF16)
        stage_done()

        rsum = dict(nmat)
        pw16 = {g_: nmat[g_].astype(BF16) for g_ in groups_b}
        pw = {g_: _dot(pw16[g_], block_diag(pw16[g_])) for g_ in groups_b}
        stage_done()
        for step in range(1, 6):
            last = step == 5
            pw16 = {g_: pw[g_].astype(BF16) for g_ in groups_b}
            rp = {}
            for g_ in groups_b:
                r16 = rsum[g_].astype(BF16)
                rp[g_] = _dot(r16 if last else jnp.concatenate([r16, pw16[g_]], axis=0), block_diag(pw16[g_]))
            for g_ in groups_b:
                rsum[g_] = rsum[g_] + pw[g_] + rp[g_][:CHUNK]
                if not last:
                    pw[g_] = rp[g_][CHUNK:]
            stage_done()

        for (b, c) in groups_b:
            for h in range(A_HEADS):
                u_ = (b, c, h)
                i = uid[u_]
                kc = k_ref[b, rows(c), lanes(h)]
                rhs = jnp.concatenate([(beta[u_] * eg[u_]) * kc, beta[u_] * v_ref[b, rows(c), lanes(h)]],
                                      axis=-1)
                rhs16 = rhs.astype(BF16)
                rhs_rows = jnp.concatenate([rhs16 if hh == h else zrhs for hh in range(A_HEADS)], axis=0)
                sol = rhs + _dot(rsum[b, c].astype(BF16), rhs_rows)
                wq_s[wslot, i] = jnp.concatenate([sol[:, :A_DK], eg[u_] * q_ref[b, rows(c), lanes(h)]],
                                                 axis=0).astype(BF16)
                ut_s[wslot, i] = sol[:, A_DK:]
                gl_s[wslot, i] = jnp.broadcast_to(jnp.exp(g_last[u_]), (1, A_DV))
        for (b, c) in groups_b:
            kd = [jnp.exp(g_last[b, c, h] - g_col[b, c, h]) * k_ref[b, rows(c), lanes(h)]
                  for h in range(A_HEADS)]
            for p in range(A_HEADS // 2):
                akd_s[wslot, b * nch + c, CHUNK:, p * LANES:(p + 1) * LANES] = (
                    jnp.concatenate([kd[2 * p], kd[2 * p + 1]], axis=0).T.astype(BF16))
        stage_done()

    for b0 in range(0, bsz, DELTA_WAVE):
        prepare(range(b0, b0 + DELTA_WAVE))
    run_rec(0)

    for b in range(bsz):
        for h in range(A_HEADS):
            s_scr[b * A_HEADS + h] = s_cur[b, h]

    @pl.when(t == pl.num_programs(0) - 1)
    def _():
        for b in range(bsz):
            for h in range(A_HEADS):
                st_ref[b, h] = s_cur[b, h]


def _delta(qkv, gb, za, na_row):
    bsz, t, _ = qkv.shape
    ct = DELTA_CT
    nt = t // ct
    n_units = bsz * (ct // CHUNK) * A_HEADS
    prep = lambda w, j=0: pl.BlockSpec((bsz, ct, w), lambda i: (0, jnp.minimum(i, nt - 1), j))
    rec = lambda w: pl.BlockSpec((bsz, ct, w), lambda i: (0, jnp.maximum(i - 1, 0), 0))
    return pl.pallas_call(
        _delta_kernel,
        grid=(nt + 1,),
        in_specs=[prep(A_WIDTH, 0), prep(A_WIDTH, 1), prep(A_WIDTH, 2), prep(LANES), rec(A_WIDTH),
                  pl.BlockSpec((1, A_DV), lambda i: (0, 0))],
        out_specs=[rec(A_WIDTH),
                   pl.BlockSpec((bsz, A_HEADS, A_DK, A_DV), lambda i: (0, 0, 0, 0))],
        out_shape=[jax.ShapeDtypeStruct((bsz, t, A_WIDTH), BF16),
                   jax.ShapeDtypeStruct((bsz, A_HEADS, A_DK, A_DV), F32)],
        scratch_shapes=[pltpu.VMEM((bsz * A_HEADS, A_DK, A_DV), F32),
                        pltpu.VMEM((2, n_units, 2 * CHUNK, A_DK), BF16),
                        pltpu.VMEM((2, n_units, CHUNK, A_DV), F32),
                        pltpu.VMEM((2, n_units // A_HEADS, CHUNK + A_DK, A_HEADS * CHUNK), BF16),
                        pltpu.VMEM((2, n_units, 1, A_DV), F32)],
        compiler_params=pltpu.CompilerParams(dimension_semantics=("arbitrary",),
                                             vmem_limit_bytes=VMEM_LIMIT),
        name="delta",
    )(qkv, qkv, qkv, gb, za, na_row)


def _swa_out_kernel(sink_ref, qb_ref, kc_ref, kp_ref, krc_ref, krp_ref, v0c_ref, v0p_ref, v1c_ref, v1p_ref,
                    zb_ref, oa_ref, x_ref, gate_ref, w_ref, g_ref, b_ref, y_ref):
    n = pl.program_id(1)
    gate = gate_ref[pl.ds(pl.program_id(0), 1), :]
    tq = qb_ref.shape[1]
    blk = WINDOW
    kx = (jnp.concatenate([kp_ref[0], kc_ref[0]], axis=0), jnp.concatenate([krp_ref[0], krc_ref[0]], axis=0))
    vd = (jnp.concatenate([v0p_ref[0], v0c_ref[0]], axis=0), jnp.concatenate([v1p_ref[0], v1c_ref[0]], axis=0))

    a = lax.broadcasted_iota(jnp.int32, (2 * blk, 2 * blk), 0) % blk
    j = lax.broadcasted_iota(jnp.int32, (2 * blk, 2 * blk), 1)
    rel = a + blk - j
    band = (rel >= 0) & (rel <= WINDOW)
    band_first = band & ((n > 0) | (j >= blk))
    top = lax.broadcasted_iota(jnp.int32, (2 * blk, 1), 0) < blk
    low = _lane((blk, LANES)) < B_HD
    zero = jnp.zeros((blk, LANES), BF16)

    qrows = lambda i: slice(i * blk, (i + 1) * blk)
    krows = lambda i: slice(i * blk, (i + 2) * blk)
    sink = {(kh, half): jnp.where(top, sink_ref[kh * B_GROUP + half] * LOG2E,
                                  sink_ref[kh * B_GROUP + half + 2] * LOG2E)
            for kh in range(B_KV_HEADS) for half in range(2)}
    for i0 in range(0, tq // blk, SWA_WAVE):
        blocks = range(i0, i0 + SWA_WAVE)
        units = [(i, kh, half) for i in blocks for kh in range(B_KV_HEADS) for half in range(2)]
        mix_a = {i: _dot(oa_ref[0, qrows(i), :], w_ref[0:A_WIDTH, :]) for i in blocks}
        sc = {}
        for (i, kh, half) in units:
            qs = []
            for g in range(2):
                grp = kh * 2 + g
                xg = qb_ref[0, qrows(i), grp * LANES:(grp + 1) * LANES]
                qs.append(jnp.where(low if half == 0 else jnp.logical_not(low), xg, zero))
            qz = jnp.concatenate(qs, axis=0)
            sc[i, kh, half] = _dot_nt(qz, kx[0 if kh == half else 1][krows(i)])
        p, den = {}, {}
        for u_ in units:
            i, kh, half = u_
            s_m = jnp.where(band_first if i == 0 else band, sc[u_], -jnp.inf)
            m = jnp.maximum(jnp.max(s_m, axis=-1, keepdims=True), sink[kh, half])
            e = jnp.exp2(s_m - m)
            den[u_] = jnp.sum(e, axis=-1, keepdims=True) + jnp.exp2(sink[kh, half] - m)
            p[u_] = e.astype(BF16)
        pv = {u_: _dot(p[u_], vd[u_[1]][krows(u_[0])]) for u_ in units}
        outs = {u_: pv[u_] / den[u_] for u_ in units}
        for i in blocks:
            ob = []
            for grp in range(B_WIDTH // LANES):
                kh, g = grp // 2, grp % 2
                og = jnp.where(low, outs[i, kh, 0][g * blk:(g + 1) * blk], outs[i, kh, 1][g * blk:(g + 1) * blk])
                ob.append((og * _silu(zb_ref[0, qrows(i), grp * LANES:(grp + 1) * LANES])).astype(BF16))
            mix = mix_a[i] + _dot(jnp.concatenate(ob, axis=-1), w_ref[A_WIDTH:MIX_WIDTH, :])
            r = DEEPNORM_ALPHA * x_ref[0, qrows(i), :] + (1.0 + gate) * mix
            y_ref[0, qrows(i), :] = _layer_norm(r, g_ref[...], b_ref[...])


def _swa_out(sinks, qb, kvl, zb, oa, x, mod, mod_row0, w_out, ln_g, ln_b):
    bsz, t, _ = qb.shape
    tq = SWA_TQ
    per = tq // WINDOW
    cur = lambda w: pl.BlockSpec((1, tq, w), lambda b, i: (b, i, 0))
    kv_cur = lambda j: pl.BlockSpec((1, tq, LANES), lambda b, i: (b, i, j))
    kv_prev = lambda j: pl.BlockSpec((1, WINDOW, LANES), lambda b, i: (b, jnp.maximum(i * per - 1, 0), j))
    const2 = lambda s: pl.BlockSpec(s, lambda b, i: (0, 0))
    return pl.pallas_call(
        _swa_out_kernel,
        grid=(bsz, t // tq),
        in_specs=[pl.BlockSpec(memory_space=pltpu.SMEM), cur(B_WIDTH),
                  kv_cur(0), kv_prev(0), kv_cur(1), kv_prev(1), kv_cur(2), kv_prev(2), kv_cur(3), kv_prev(3),
                  cur(B_WIDTH), cur(A_WIDTH), cur(D_MODEL),
                  pl.BlockSpec((8, D_MODEL), lambda b, i: (mod_row0 // 8, 2)),
                  const2((MIX_WIDTH, D_MODEL)), const2((1, D_MODEL)), const2((1, D_MODEL))],
        out_specs=cur(D_MODEL),
        out_shape=jax.ShapeDtypeStruct((bsz, t, D_MODEL), F32),
        compiler_params=pltpu.CompilerParams(dimension_semantics=("arbitrary", "arbitrary"),
                                             vmem_limit_bytes=VMEM_LIMIT),
        name="swa_out",
    )(sinks, qb, kvl, kvl, kvl, kvl, kvl, kvl, kvl, kvl, zb, oa, x, mod, w_out, ln_g, ln_b)


def _out_kernel(oa_ref, ob_ref, x_ref, gate_ref, w_ref, g_ref, b_ref, y_ref):
    mix = _dot(oa_ref[...], w_ref[0:A_WIDTH, :]) + _dot(ob_ref[...], w_ref[A_WIDTH:MIX_WIDTH, :])
    r = DEEPNORM_ALPHA * x_ref[...] + (1.0 + gate_ref[...]) * mix
    y_ref[...] = _layer_norm(r, g_ref[...], b_ref[...])


def _out(oa, ob, x, mod, w_out, ln_g, ln_b):
    n = x.shape[0]
    full = lambda s: pl.BlockSpec(s, lambda i: (0, 0))
    return pl.pallas_call(
        _out_kernel,
        grid=(1,),
        in_specs=[full((n, A_WIDTH)), full((n, B_WIDTH)), full((n, D_MODEL)),
                  pl.BlockSpec((n, D_MODEL), lambda i: (0, 2)),
                  full((MIX_WIDTH, D_MODEL)), full((1, D_MODEL)), full((1, D_MODEL))],
        out_specs=full((n, D_MODEL)),
        out_shape=jax.ShapeDtypeStruct((n, D_MODEL), F32),
        compiler_params=pltpu.CompilerParams(dimension_semantics=("arbitrary",),
                                             vmem_limit_bytes=VMEM_LIMIT),
        name="out",
    )(oa, ob, x, mod, w_out, ln_g, ln_b)


def _sproj_kernel(x_ref, mod_ref, w_ref, cw_ref, cst_ref, alog_ref, dt_ref, cos_ref, sin_ref,
                  q_ref, k_ref, v_ref, za_ref, gb_ref, qb_ref, kb_ref, vb_ref, zb_ref, ncs_ref):
    shift = mod_ref[:, 0:D_MODEL]
    scale = mod_ref[:, D_MODEL:2 * D_MODEL]
    h = (x_ref[...] * (1.0 + scale) + shift).astype(BF16)

    for gi, o_ref in enumerate((q_ref, k_ref, v_ref)):
        c0 = gi * A_WIDTH
        cs = slice(c0, c0 + A_WIDTH)
        u = _dot(h, w_ref[:, cs])
        acc = cst_ref[0, :, cs] * cw_ref[0:1, cs]
        acc = acc + cst_ref[1, :, cs] * cw_ref[1:2, cs]
        acc = acc + cst_ref[2, :, cs] * cw_ref[2:3, cs]
        acc = acc + u * cw_ref[3:4, cs]
        y = _silu(acc)
        if gi == 0:
            y = _l2norm_heads(y, A_DK ** -0.5)
        elif gi == 1:
            y = _l2norm_heads(y, 1.0)
        o_ref[...] = y
        ncs_ref[0, :, cs] = cst_ref[1, :, cs]
        ncs_ref[1, :, cs] = cst_ref[2, :, cs]
        ncs_ref[2, :, cs] = u

    za_ref[...] = _dot(h, w_ref[:, C_ZA:C_ZA + A_WIDTH])
    gb_ref[...] = _gate_lanes(_dot(h, w_ref[:, C_BD:C_BD + LANES]), alog_ref[...], dt_ref[...])

    cos = cos_ref[...]
    sin = sin_ref[...]
    uq = _dot(h, w_ref[:, C_QB:C_QB + B_WIDTH])
    for g in range(B_WIDTH // LANES):
        qb_ref[:, g * LANES:(g + 1) * LANES] = (
            _rotary_group(uq[:, g * LANES:(g + 1) * LANES], cos, sin) * (B_HD ** -0.5))
    kb_ref[...] = _rotary_group(_dot(h, w_ref[:, C_KB:C_KB + LANES]), cos, sin)
    vb_ref[...] = _dot(h, w_ref[:, C_VB:C_VB + LANES])
    zb_ref[...] = _dot(h, w_ref[:, C_ZB:C_ZB + B_WIDTH])


def _sproj(x, mod_s, w_r, conv_w, cst, alog_row, dt_row, cos_row, sin_row):
    n = x.shape[0]
    full = lambda s: pl.BlockSpec(s, lambda i: (0,) * len(s))
    wide = lambda w: jax.ShapeDtypeStruct((n, w), F32)
    return pl.pallas_call(
        _sproj_kernel,
        grid=(1,),
        in_specs=[full((n, D_MODEL)), pl.BlockSpec((n, 3 * D_MODEL), lambda i: (0, 0)),
                  full((D_MODEL, W_COLS)),
                  full((CONV_W, A_QKV)), full((CONV_W - 1, n, A_QKV)),
                  full((1, LANES)), full((1, LANES)), full((1, LANES)), full((1, LANES))],
        out_specs=[full((n, A_WIDTH)), full((n, A_WIDTH)), full((n, A_WIDTH)), full((n, A_WIDTH)),
                   full((n, LANES)), full((n, B_WIDTH)), full((n, LANES)), full((n, LANES)),
                   full((n, B_WIDTH)), full((CONV_W - 1, n, A_QKV))],
        out_shape=[wide(A_WIDTH), wide(A_WIDTH), wide(A_WIDTH), wide(A_WIDTH), wide(LANES),
                   wide(B_WIDTH), wide(LANES), wide(LANES), wide(B_WIDTH),
                   jax.ShapeDtypeStruct((CONV_W - 1, n, A_QKV), F32)],
        compiler_params=pltpu.CompilerParams(dimension_semantics=("arbitrary",),
                                             vmem_limit_bytes=VMEM_LIMIT),
        name="sproj",
    )(x, mod_s, w_r, conv_w, cst, alog_row, dt_row, cos_row, sin_row)


def _sstep_kernel(sink_ref, q_ref, k_ref, v_ref, gb_ref, za_ref, na_ref, st_ref,
                  qb_ref, kn_ref, vn_ref, zb_ref, ck_ref, cv_ref,
                  oa_ref, ob_ref, nst_ref, nck_ref, ncv_ref,
                  o_scr, ob_scr):
    bt = q_ref.shape[0]
    gbv = gb_ref[...]

    pick = (lax.broadcasted_iota(jnp.int32, (bt, bt * A_DV), 1) // A_DV
            == lax.broadcasted_iota(jnp.int32, (bt, bt * A_DV), 0))
    pick = jnp.where(pick, 1.0, 0.0).astype(BF16)
    for h in range(A_HEADS):
        hs = slice(h * A_DK, (h + 1) * A_DK)
        q_rep = _dot(q_ref[:, hs].T.astype(BF16), pick)
        k_rep = _dot(k_ref[:, hs].T.astype(BF16), pick)
        for bb in range(bt):
            eg = jnp.exp(gbv[bb:bb + 1, A_HEADS + h:A_HEADS + h + 1])
            beta = gbv[bb:bb + 1, h:h + 1]
            kcol = k_rep[:, bb * A_DV:(bb + 1) * A_DV]
            qcol = q_rep[:, bb * A_DV:(bb + 1) * A_DV]
            s1 = eg * st_ref[bb, h]
            pred = jnp.sum(kcol * s1, axis=0, keepdims=True)
            upd = beta * (v_ref[bb:bb + 1, hs] - pred)
            s2 = s1 + kcol * upd
            nst_ref[bb, h] = s2
            o_scr[bb:bb + 1, hs] = jnp.sum(qcol * s2, axis=0, keepdims=True)
    na = na_ref[...]
    for h in range(A_HEADS):
        hs = slice(h * A_DK, (h + 1) * A_DK)
        o = o_scr[:, hs]
        on = o * lax.rsqrt(jnp.mean(o * o, axis=-1, keepdims=True) + RMS_EPS) * na
        oa_ref[:, hs] = (on * _silu(za_ref[:, hs])).astype(BF16)

    row8 = lax.broadcasted_iota(jnp.int32, (B_HEADS, LANES), 0)
    lane8 = _lane((B_HEADS, LANES))
    own_half = (lane8 >= B_HD) == (row8 >= B_GROUP)
    rcol = lax.broadcasted_iota(jnp.int32, (B_HEADS, 1), 0)
    sink = jnp.zeros((B_HEADS, 1), F32)
    for r in range(B_HEADS):
        sink = jnp.where(rcol == r, sink_ref[r], sink)
    qv = qb_ref[...]
    qv_r = jnp.concatenate([pltpu.roll(qv[:, g * LANES:(g + 1) * LANES], B_HD, axis=1)
                            for g in range(B_WIDTH // LANES)], axis=-1)
    kn_t = kn_ref[...].T
    vn_t = vn_ref[...].T
    newest = _lane((LANES, WINDOW)) == WINDOW - 1
    qzs, scs = [], []
    for bb in range(bt):
        qz = jnp.zeros((B_HEADS, LANES), F32)
        for r in range(B_HEADS):
            grp, half, kh = r // 2, r % 2, r // B_GROUP
            src = qv if half == kh else qv_r
            qz = jnp.where(row8 == r, src[bb:bb + 1, grp * LANES:(grp + 1) * LANES], qz)
        qzs.append(jnp.where(own_half, qz, 0.0))
    for bb in range(bt):
        scs.append(_dot(qzs[bb], ck_ref[bb]))
    ps, pnews, dens = [], [], []
    for bb in range(bt):
        sc_new = jnp.sum(qzs[bb] * kn_ref[bb:bb + 1, :], axis=-1, keepdims=True)
        m = jnp.maximum(jnp.maximum(jnp.max(scs[bb], axis=-1, keepdims=True), sc_new), sink)
        p = jnp.exp(scs[bb] - m)
        p_new = jnp.exp(sc_new - m)
        ps.append(p)
        pnews.append(p_new)
        dens.append(jnp.sum(p, axis=-1, keepdims=True) + p_new + jnp.exp(sink - m))
    pvs = [_dot_nt(ps[bb], cv_ref[bb]) for bb in range(bt)]
    for bb in range(bt):
        o = (pvs[bb] + pnews[bb] * vn_ref[bb:bb + 1, :]) / dens[bb]
        o = jnp.where(own_half, o, 0.0)
        ob_scr[bb * B_HEADS:(bb + 1) * B_HEADS, :] = o + pltpu.roll(o, B_HD, axis=1)
    for bb in range(bt):
        nck_ref[bb] = jnp.where(newest, kn_t[:, bb:bb + 1], pltpu.roll(ck_ref[bb], WINDOW - 1, axis=1))
        ncv_ref[bb] = jnp.where(newest, vn_t[:, bb:bb + 1], pltpu.roll(cv_ref[bb], WINDOW - 1, axis=1))
    low = _lane((bt, LANES)) < B_HD
    for grp in range(B_WIDTH // LANES):
        even = ob_scr[pl.ds(2 * grp, bt, stride=B_HEADS), :]
        odd = ob_scr[pl.ds(2 * grp + 1, bt, stride=B_HEADS), :]
        gs = slice(grp * LANES, (grp + 1) * LANES)
        ob_ref[:, gs] = (jnp.where(low, even, odd) * _silu(zb_ref[:, gs])).astype(BF16)


def _sstep(sinks, q, k, v, gb, za, na_row, state, qb, kn, vn, zb, ck, cv):
    n = q.shape[0]
    bt = STEP_BT
    row = lambda w: pl.BlockSpec((bt, w), lambda i: (i, 0))
    st_spec = pl.BlockSpec((bt, A_HEADS, A_DK, A_DV), lambda i: (i, 0, 0, 0))
    c_spec = pl.BlockSpec((bt, WINDOW, LANES), lambda i: (i, 0, 0))
    return pl.pallas_call(
        _sstep_kernel,
        grid=(n // bt,),
        in_specs=[pl.BlockSpec(memory_space=pltpu.SMEM),
                  row(A_WIDTH), row(A_WIDTH), row(A_WIDTH), row(LANES), row(A_WIDTH),
                  pl.BlockSpec((1, A_DV), lambda i: (0, 0)), st_spec,
                  row(B_WIDTH), row(LANES), row(LANES), row(B_WIDTH), c_spec, c_spec],
        out_specs=[row(A_WIDTH), row(B_WIDTH), st_spec, c_spec, c_spec],
        out_shape=[jax.ShapeDtypeStruct((n, A_WIDTH), BF16),
                   jax.ShapeDtypeStruct((n, B_WIDTH), BF16),
                   jax.ShapeDtypeStruct((n, A_HEADS, A_DK, A_DV), F32),
                   jax.ShapeDtypeStruct((n, WINDOW, LANES), F32),
                   jax.ShapeDtypeStruct((n, WINDOW, LANES), F32)],
        scratch_shapes=[pltpu.VMEM((bt, A_WIDTH), F32), pltpu.VMEM((bt * B_HEADS, LANES), F32)],
        compiler_params=pltpu.CompilerParams(dimension_semantics=("arbitrary",),
                                             vmem_limit_bytes=VMEM_LIMIT),
        name="sstep",
    )(sinks, q, k, v, gb, za, na_row, state, qb, kn, vn, zb, ck, cv)


def _rope_tables(pos):
    half = B_HD // 2
    inv = 1.0 / (ROPE_THETA ** (np.arange(half, dtype=np.float64) / half))
    ang = np.asarray(pos, np.float64)[:, None] * inv[None, :]
    cos, sin = np.cos(ang), np.sin(ang)
    reps = LANES // B_HD
    return (jnp.asarray(np.tile(np.concatenate([cos, cos], -1), (1, reps)), F32),
            jnp.asarray(np.tile(np.concatenate([-sin, sin], -1), (1, reps)), F32))


def _pad_row(vec, offset):
    return jnp.pad(vec.astype(F32).reshape(1, -1), ((0, 0), (offset, LANES - offset - vec.shape[0])))


def _layer(x_prompt, x_sample, state_conv, state_delta, cache_k, cache_v, c_prompt, c_sample,
           w_ada, b_ada, w_in, conv_w, a_log, dt_bias, norm_a, sinks, w_out, ln_g, ln_b):
    bsz, seq, _ = x_prompt.shape
    n_s = x_sample.shape[0]

    w_r = _wprep(jnp.swapaxes(w_in, 0, 1))
    w_o = w_out.astype(BF16)
    alog_row = _pad_row(a_log, A_HEADS)
    dt_row = _pad_row(dt_bias, A_HEADS)
    na_row = norm_a.reshape(1, A_DV)
    g_row = ln_g.reshape(1, D_MODEL)
    b_row = ln_b.reshape(1, D_MODEL)

    assert n_s % 8 == 0 and bsz <= 8
    mod = _ada(c_sample, c_prompt, w_ada, b_ada.reshape(1, 3 * D_MODEL))

    cos_p, sin_p = _rope_tables(np.arange(seq))
    (qkv, za, gb, qb, kvl, zb, conv_p, kb_last, vb_last) = _proj(
        x_prompt, mod, n_s, w_r, conv_w, alog_row, dt_row, cos_p, sin_p)
    oa, delta_p = _delta(qkv, gb, za, na_row)
    y_p = _swa_out(sinks, qb, kvl, zb, oa, x_prompt, mod, n_s, w_o, g_row, b_row)
    swa_k_p = kb_last.reshape(bsz, WINDOW, B_KV_HEADS, B_HD)
    swa_v_p = vb_last.reshape(bsz, WINDOW, B_KV_HEADS, B_HD)

    cos_s, sin_s = _rope_tables(np.array([PAST_LEN]))
    xs = x_sample.reshape(n_s, D_MODEL)
    cst = jnp.transpose(state_conv, (1, 0, 2))
    sq, sk, sv, sza, sgb, sqb, skn, svn, szb, ncs = _sproj(xs, mod, w_r, conv_w, cst, alog_row, dt_row,
                                                           cos_s, sin_s)
    soa, sob, delta_s, nck, ncv = _sstep(sinks, sq, sk, sv, sgb, sza, na_row, state_delta,
                                         sqb, skn, svn, szb,
                                         jnp.swapaxes(cache_k.reshape(n_s, WINDOW, LANES), 1, 2),
                                         jnp.swapaxes(cache_v.reshape(n_s, WINDOW, LANES), 1, 2))
    y_s = _out(soa, sob, xs, mod, w_o, g_row, b_row)
    conv_s = jnp.transpose(ncs, (1, 0, 2))
    unpack = lambda c: jnp.swapaxes(c, 1, 2).reshape(n_s, WINDOW, B_KV_HEADS, B_HD)
    return (y_p, y_s.reshape(n_s, 1, D_MODEL), conv_p, delta_p, swa_k_p, swa_v_p,
            conv_s, delta_s, unpack(nck), unpack(ncv))


def kernel(x_prompt, x_sample, state_conv, state_delta, cache_swa_k, cache_swa_v, c_prompt, c_sample,
           w_ada, b_ada, w_in, conv_w, a_log, dt_bias, norm_a, sinks, w_out, ln_g, ln_b):
    assert w_ada.shape[0] == DEPTH == 1
    outs = _layer(x_prompt, x_sample, state_conv[0], state_delta[0], cache_swa_k[0], cache_swa_v[0],
                  c_prompt, c_sample, w_ada[0], b_ada[0], w_in[0], conv_w[0], a_log[0], dt_bias[0],
                  norm_a[0], sinks[0], w_out[0], ln_g[0], ln_b[0])
    y_p, y_s = outs[0], outs[1]
    return (y_p, y_s) + tuple(o[None] for o in outs[2:])
```

```python
import jax
import jax.numpy as jnp
import numpy as np
from jax import lax
from jax.experimental import pallas as pl
from jax.experimental.pallas import tpu as pltpu

F32 = jnp.float32
BF16 = jnp.bfloat16

D_MODEL = 1024
DEPTH = 1
PAST_LEN = 8192
A_HEADS = 4
A_DK = 128
A_DV = 128
A_WIDTH = A_HEADS * A_DV
A_QKV = 3 * A_WIDTH
CONV_W = 4
CHUNK = 64
B_HEADS = 8
B_KV_HEADS = 2
B_HD = 64
B_GROUP = B_HEADS // B_KV_HEADS
B_WIDTH = B_HEADS * B_HD
B_KV_WIDTH = B_KV_HEADS * B_HD
WINDOW = 128
ROPE_THETA = 10000.0
MIX_WIDTH = A_WIDTH + B_WIDTH
DEEPNORM_ALPHA = (2 * DEPTH) ** 0.25
LOG2E = 1.4426950408889634
LN_EPS = 1e-5
RMS_EPS = 1e-6
L2_EPS = 1e-6

OFF_A_Z = A_QKV
OFF_A_BETA = OFF_A_Z + A_WIDTH
OFF_A_DECAY = OFF_A_BETA + A_HEADS
OFF_B_Q = OFF_A_DECAY + A_HEADS
OFF_B_K = OFF_B_Q + B_WIDTH
OFF_B_V = OFF_B_K + B_KV_WIDTH
OFF_B_Z = OFF_B_V + B_KV_WIDTH
PROJ_COLS = OFF_B_Z + B_WIDTH

LANES = 128
C_QKV = 0
C_ZA = C_QKV + A_QKV
C_QB = C_ZA + A_WIDTH
C_KB = C_QB + B_WIDTH
C_VB = C_KB + B_KV_WIDTH
C_ZB = C_VB + B_KV_WIDTH
C_BD = C_ZB + B_WIDTH
WPREP_TN = 256
W_COLS = C_BD + WPREP_TN

VMEM_LIMIT = 56 * 1024 * 1024

ADA_TN = 1536
PROJ_TM = 512
PROJ_CW = 256
PROJ_PARTS = 4
DELTA_CT = 256
DELTA_WAVE = 2
SWA_TQ = 512
SWA_WAVE = 2
STEP_BT = 16


def _dot(a, b):
    return jnp.dot(a, b, preferred_element_type=F32)


def _dot_nt(a, b):
    return lax.dot_general(a, b, (((1,), (1,)), ((), ())), preferred_element_type=F32)


def _silu(x):
    return x * jax.nn.sigmoid(x)


def _softplus(x):
    return jnp.maximum(x, 0.0) + jnp.log1p(jnp.exp(-jnp.abs(x)))


def _lane(shape):
    return lax.broadcasted_iota(jnp.int32, shape, len(shape) - 1)


def _l2norm_heads(y, scale):
    outs = []
    for h in range(y.shape[1] // A_DK):
        xh = y[:, h * A_DK:(h + 1) * A_DK]
        ss = jnp.sum(xh * xh, axis=-1, keepdims=True)
        xn = xh * lax.rsqrt(ss + L2_EPS)
        outs.append(xn * scale if scale != 1.0 else xn)
    return jnp.concatenate(outs, axis=-1)


def _rotary_group(xg, cos, sin_signed):
    lane = _lane(xg.shape)
    swapped = jnp.where((lane % B_HD) < (B_HD // 2),
                        pltpu.roll(xg, LANES - B_HD // 2, axis=1),
                        pltpu.roll(xg, B_HD // 2, axis=1))
    return xg * cos + swapped * sin_signed


def _kv_layouts(kb, vb):
    low = _lane(kb.shape) < B_HD
    kbr = pltpu.roll(kb, B_HD, axis=1)
    vbr = pltpu.roll(vb, B_HD, axis=1)
    return kb, kbr, jnp.where(low, vb, vbr), jnp.where(low, vbr, vb)


def _gate_lanes(bd, alog_row, dt_row):
    lane = _lane(bd.shape)
    g = -jnp.exp(alog_row) * _softplus(bd + dt_row)
    return jnp.where(lane < A_HEADS, jax.nn.sigmoid(bd), g)


def _layer_norm(r, g, b):
    mu = jnp.mean(r, axis=-1, keepdims=True)
    d = r - mu
    var = jnp.mean(d * d, axis=-1, keepdims=True)
    return d * lax.rsqrt(var + LN_EPS) * g + b


def _wprep_kernel(wa_ref, wb_ref, o_ref):
    tn = wa_ref.shape[0]
    o_ref[:, 0:tn] = wa_ref[...].T.astype(BF16)
    xb = wb_ref[...]
    tail = pl.program_id(0) == pl.num_programs(0) - 1
    row = lax.broadcasted_iota(jnp.int32, xb.shape, 0)
    xb = jnp.where(jnp.logical_and(tail, row >= 2 * A_HEADS), 0.0, xb)
    o_ref[:, tn:2 * tn] = xb.T.astype(BF16)


def _wprep(w_t):
    tn = WPREP_TN
    n_a, n_b = OFF_A_BETA // tn, (PROJ_COLS - OFF_B_Q) // tn
    assert n_a * tn == OFF_A_BETA and n_b * tn == PROJ_COLS - OFF_B_Q and OFF_A_BETA + tn <= PROJ_COLS

    assert (n_a + n_b + 1) % 2 == 0

    def src_row(j):
        return jnp.where(j < n_a, j * tn, jnp.where(j < n_a + n_b, OFF_B_Q + (j - n_a) * tn, OFF_A_BETA))

    src = lambda k: pl.BlockSpec((pl.Element(tn), pl.Element(D_MODEL)),
                                 lambda j: (pl.multiple_of(src_row(2 * j + k), 8), 0))
    return pl.pallas_call(
        _wprep_kernel,
        grid=((n_a + n_b + 1) // 2,),
        in_specs=[src(0), src(1)],
        out_specs=pl.BlockSpec((D_MODEL, 2 * tn), lambda j: (0, j)),
        out_shape=jax.ShapeDtypeStruct((D_MODEL, W_COLS), BF16),
        compiler_params=pltpu.CompilerParams(dimension_semantics=("arbitrary",),
                                             vmem_limit_bytes=VMEM_LIMIT),
        name="wprep",
    )(w_t, w_t)


def _ada_kernel(cs_ref, cp_ref, w_ref, b_ref, o_ref):
    n_s, n_p = cs_ref.shape[0], cp_ref.shape[0]
    w = w_ref[...].astype(BF16)
    o_ref[0:n_s, :] = _dot(cs_ref[...].astype(BF16), w) + b_ref[...]
    o_ref[n_s:n_s + 8, :] = jnp.zeros((8, o_ref.shape[1]), F32)
    o_ref[n_s:n_s + n_p, :] = _dot(cp_ref[...].astype(BF16), w) + b_ref[...]


def _ada(c_sample, c_prompt, w_ada, b_ada):
    n_s, n_p = c_sample.shape[0], c_prompt.shape[0]
    rows = n_s + 8
    tn = ADA_TN
    return pl.pallas_call(
        _ada_kernel,
        grid=(3 * D_MODEL // tn,),
        in_specs=[pl.BlockSpec((n_s, D_MODEL), lambda j: (0, 0)),
                  pl.BlockSpec((n_p, D_MODEL), lambda j: (0, 0)),
                  pl.BlockSpec((D_MODEL, tn), lambda j: (0, j)),
                  pl.BlockSpec((1, tn), lambda j: (0, j))],
        out_specs=pl.BlockSpec((rows, tn), lambda j: (0, j)),
        out_shape=jax.ShapeDtypeStruct((rows, 3 * D_MODEL), F32),
        compiler_params=pltpu.CompilerParams(dimension_semantics=("arbitrary",),
                                             vmem_limit_bytes=VMEM_LIMIT),
        name="ada",
    )(c_sample, c_prompt, w_ada, b_ada)


def _proj_kernel(x_ref, mod_ref, w_ref, cw_ref, alog_ref, dt_ref, cos_ref, sin_ref,
                 qkv_ref, za_ref, gb_ref, qb_ref, kvl_ref, zb_ref, cst_ref, kbl_ref, vbl_ref, ubuf):
    tm = x_ref.shape[1]
    t = pl.program_id(1)

    @pl.when(t == 0)
    def _():
        ubuf[...] = jnp.zeros(ubuf.shape, F32)

    brow = pl.ds(pl.program_id(0), 1)
    shift = mod_ref[brow, 0:D_MODEL]
    scale = mod_ref[brow, D_MODEL:2 * D_MODEL]

    rp = tm // PROJ_PARTS
    cw = PROJ_CW
    sub = lax.broadcasted_iota(jnp.int32, (rp // 8, 8, cw), 1)
    pieces = [slice(c0, c0 + cw) for c0 in range(0, A_QKV, cw)]

    def matmuls(r0):
        h = (x_ref[0, r0:r0 + rp, :] * (1.0 + scale) + shift).astype(BF16)
        return ([_dot(h, w_ref[:, cs]) for cs in pieces],
                _dot(h, w_ref[:, C_ZA:C_ZA + A_WIDTH]), _dot(h, w_ref[:, C_BD:C_BD + LANES]),
                _dot(h, w_ref[:, C_QB:C_QB + B_WIDTH]), _dot(h, w_ref[:, C_KB:C_KB + 2 * LANES]),
                _dot(h, w_ref[:, C_ZB:C_ZB + B_WIDTH]))

    def epilogue(r0, results):
        rs = slice(r0, r0 + rp)
        us, za, ubd, uq, ukv, zb = results

        def conv_epilogue(cs, u):
            gi = cs.start // A_WIDTH
            groups = jnp.concatenate([ubuf[:, cs], u], axis=0).reshape(rp // 8 + 1, 8, cw)
            acc = None
            for j in range(CONV_W - 1, 0, -1):
                rot = pltpu.roll(groups, j, axis=1)
                term = (jnp.where(sub < j, rot[:-1], rot[1:]).reshape(rp, cw)
                        * cw_ref[CONV_W - 1 - j:CONV_W - j, cs])
                acc = term if acc is None else acc + term
            y = _silu(acc + u * cw_ref[CONV_W - 1:CONV_W, cs])
            if gi == 0:
                y = _l2norm_heads(y, A_DK ** -0.5)
            elif gi == 1:
                y = _l2norm_heads(y, 1.0)
            qkv_ref[0, rs, cs] = y
            ubuf[:, cs] = u[rp - 8:rp]
            if r0 + rp == tm:
                cst_ref[0, :, cs] = u[rp - (CONV_W - 1):rp]

        for cs, u in zip(pieces, us):
            conv_epilogue(cs, u)
        za_ref[0, rs, :] = za
        zb_ref[0, rs, :] = zb
        gb_ref[0, rs, :] = _gate_lanes(ubd, alog_ref[...], dt_ref[...])
        cos = cos_ref[rs, :]
        sin = sin_ref[rs, :]
        for g in range(B_WIDTH // LANES):
            qb_ref[0, rs, g * LANES:(g + 1) * LANES] = (
                _rotary_group(uq[:, g * LANES:(g + 1) * LANES], cos, sin) * (B_HD ** -0.5 * LOG2E)).astype(BF16)
        kb = _rotary_group(ukv[:, 0:LANES], cos, sin)
        vb = ukv[:, LANES:2 * LANES]
        for j, val in enumerate(_kv_layouts(kb, vb)):
            kvl_ref[0, rs, j * LANES:(j + 1) * LANES] = val.astype(BF16)
        return kb, vb

    starts = list(range(0, tm, rp))
    pending = matmuls(starts[0])
    for p, r0 in enumerate(starts):
        results = pending
        if p + 1 < len(starts):
            pending = matmuls(starts[p + 1])
        kb, vb = epilogue(r0, results)

    @pl.when(t == pl.num_programs(1) - 1)
    def _():
        kbl_ref[0] = kb[rp - WINDOW:rp]
        vbl_ref[0] = vb[rp - WINDOW:rp]


def _proj(x, mod, mod_row0, w_r, conv_w, alog_row, dt_row, cos_t, sin_t):
    bsz, t, _ = x.shape
    tm = PROJ_TM
    row = lambda w: pl.BlockSpec((1, tm, w), lambda b, i: (b, i, 0))
    const2 = lambda s: pl.BlockSpec(s, lambda b, i: (0, 0))
    per_b = lambda r, w: pl.BlockSpec((1, r, w), lambda b, i: (b, 0, 0))
    wide = lambda w, dt=F32: jax.ShapeDtypeStruct((bsz, t, w), dt)
    return pl.pallas_call(
        _proj_kernel,
        grid=(bsz, t // tm),
        in_specs=[row(D_MODEL),
                  pl.BlockSpec((8, 3 * D_MODEL), lambda b, i: (mod_row0 // 8, 0)),
                  const2((D_MODEL, W_COLS)),
                  const2((CONV_W, A_QKV)),
                  const2((1, LANES)), const2((1, LANES)),
                  pl.BlockSpec((tm, LANES), lambda b, i: (i, 0)),
                  pl.BlockSpec((tm, LANES), lambda b, i: (i, 0))],
        out_specs=[row(A_QKV), row(A_WIDTH), row(LANES), row(B_WIDTH), row(4 * LANES), row(B_WIDTH),
                   per_b(CONV_W - 1, A_QKV), per_b(WINDOW, LANES), per_b(WINDOW, LANES)],
        out_shape=[wide(A_QKV), wide(A_WIDTH), wide(LANES), wide(B_WIDTH, BF16), wide(4 * LANES, BF16),
                   wide(B_WIDTH),
                   jax.ShapeDtypeStruct((bsz, CONV_W - 1, A_QKV), F32),
                   jax.ShapeDtypeStruct((bsz, WINDOW, LANES), F32),
                   jax.ShapeDtypeStruct((bsz, WINDOW, LANES), F32)],
        scratch_shapes=[pltpu.VMEM((8, A_QKV), F32)],
        compiler_params=pltpu.CompilerParams(dimension_semantics=("arbitrary", "arbitrary"),
                                             vmem_limit_bytes=VMEM_LIMIT),
        name="proj",
    )(x, mod, w_r, conv_w, alog_row, dt_row, cos_t, sin_t)


def _delta_kernel(q_ref, k_ref, v_ref, gb_ref, za_ref, na_ref, oa_ref, st_ref,
                  s_scr, wq_s, ut_s, akd_s, gl_s):
    bsz, ct = q_ref.shape[0], q_ref.shape[1]
    nch = ct // CHUNK
    t = pl.program_id(0)
    wslot = t % 2
    rslot = 1 - wslot

    @pl.when(t == 0)
    def _():
        s_scr[...] = jnp.zeros(s_scr.shape, F32)
        wq_s[...] = jnp.zeros(wq_s.shape, BF16)
        ut_s[...] = jnp.zeros(ut_s.shape, F32)
        akd_s[...] = jnp.zeros(akd_s.shape, BF16)
        gl_s[...] = jnp.zeros(gl_s.shape, F32)

    units = [(b, c, h) for b in range(bsz) for c in range(nch) for h in range(A_HEADS)]
    uid = {u_: i for i, u_ in enumerate(units)}
    rows = lambda c: slice(c * CHUNK, (c + 1) * CHUNK)
    lanes = lambda h: slice(h * A_DK, (h + 1) * A_DK)
    na = na_ref[...]

    s_cur = {(b, h): s_scr[b * A_HEADS + h] for b in range(bsz) for h in range(A_HEADS)}
    ws, uu = {}, {}

    def rec_ws(c):
        for b in range(bsz):
            for h in range(A_HEADS):
                i = uid[b, c, h]
                ws[b, h] = _dot(wq_s[rslot, i], s_cur[b, h].astype(BF16))
                uu[b, h] = (ut_s[rslot, i] - ws[b, h][:CHUNK]).astype(BF16)

    def rec_ou(c):
        zpad = jnp.zeros((CHUNK, A_DV), BF16)
        for b in range(bsz):
            u_bd = jnp.concatenate(
                [jnp.concatenate([uu[b, h] if hh == h else zpad for hh in range(A_HEADS)], axis=-1)
                 for h in range(A_HEADS)], axis=0)
            ou = _dot(akd_s[rslot, b * nch + c], u_bd)
            for h in range(A_HEADS):
                o = ws[b, h][CHUNK:] + ou[:CHUNK, lanes(h)]
                s_cur[b, h] = gl_s[rslot, uid[b, c, h]] * s_cur[b, h] + ou[CHUNK:, lanes(h)]
                on = o * lax.rsqrt(jnp.mean(o * o, axis=-1, keepdims=True) + RMS_EPS) * na
                oa_ref[b, rows(c), lanes(h)] = (on * _silu(za_ref[b, rows(c), lanes(h)])).astype(BF16)

    rec_stages = []
    for c in range(nch):
        rec_stages += [lambda c=c: rec_ws(c), lambda c=c: rec_ou(c)]

    def run_rec(n_left_after):
        while rec_stages and len(rec_stages) > n_left_after:
            rec_stages.pop(0)()

    pk = A_HEADS * CHUNK
    low = _lane((CHUNK, LANES)) < CHUNK
    low_row = _lane((1, LANES)) < CHUNK
    ti_p = lax.broadcasted_iota(jnp.int32, (CHUNK, pk), 0)
    ii_p = _lane((CHUNK, pk)) % CHUNK
    zero64 = jnp.zeros((CHUNK, LANES), BF16)

    def pack(parts):
        return jnp.concatenate([jnp.where(low, parts[0], parts[1]), jnp.where(low, parts[2], parts[3])], axis=-1)

    def block_diag(x16):
        blocks = []
        for h in range(A_HEADS):
            pair, first = h // 2, h % 2 == 0
            piece = jnp.where(low if first else jnp.logical_not(low), x16[:, pair * LANES:(pair + 1) * LANES], zero64)
            blocks.append(jnp.concatenate([piece, zero64] if pair == 0 else [zero64, piece], axis=-1))
        return jnp.concatenate(blocks, axis=0)

    zrhs = jnp.zeros((CHUNK, 2 * A_DK), BF16)
    n_rec = len(rec_stages)
    n_slots = 8 * (bsz // DELTA_WAVE)
    done = [0]

    def stage_done():
        done[0] += 1
        run_rec(n_rec - (done[0] * n_rec) // n_slots)

    def decay_terms(b, beta, g_col, g_last, eg, dec_p, beta_p):
        gbv = gb_ref[b]
        rin = lax.broadcasted_iota(jnp.int32, gbv.shape, 0) % CHUNK
        gcs = gbv
        s = 1
        while s < CHUNK:
            gcs = gcs + jnp.where(rin >= s, pltpu.roll(gcs, s, axis=0), 0.0)
            s *= 2
        gcs_t = gcs.T
        for c in range(nch):
            r0 = c * CHUNK
            pair_lanes = slice((c // 2) * LANES, (c // 2 + 1) * LANES)
            g_rows = []
            for h in range(A_HEADS):
                u_ = (b, c, h)
                beta[u_] = jnp.broadcast_to(gbv[rows(c), h:h + 1], (CHUNK, A_DK))
                g_col[u_] = jnp.broadcast_to(gcs[rows(c), A_HEADS + h:A_HEADS + h + 1], (CHUNK, A_DK))
                g_last[u_] = gcs[r0 + CHUNK - 1:r0 + CHUNK, A_HEADS + h:A_HEADS + h + 1]
                eg[u_] = jnp.exp(g_col[u_])
                g_row = gcs_t[A_HEADS + h:A_HEADS + h + 1, pair_lanes]
                g_rows.append(g_row if c % 2 == h % 2 else pltpu.roll(g_row, CHUNK, axis=1))
            g_row_p = jnp.concatenate([jnp.where(low_row, g_rows[0], g_rows[1]),
                                       jnp.where(low_row, g_rows[2], g_rows[3])], axis=-1)
            g_col_p = pack([g_col[b, c, h] for h in range(A_HEADS)])
            dec_p[b, c] = jnp.exp(jnp.where(ti_p >= ii_p, g_col_p - g_row_p, -jnp.inf))
            beta_p[b, c] = pack([beta[b, c, h] for h in range(A_HEADS)])

    def prepare(bs):
        groups_b = [(b, c) for b in bs for c in range(nch)]
        beta, g_col, g_last, eg, dec_p, beta_p = {}, {}, {}, {}, {}, {}
        for b in bs:
            decay_terms(b, beta, g_col, g_last, eg, dec_p, beta_p)

        nmat = {}
        for (b, c) in groups_b:
            k16 = k_ref[b, rows(c), :].astype(BF16)
            q16 = q_ref[b, rows(c), :].astype(BF16)
            k_heads = jnp.concatenate(
                [jnp.concatenate([k16[:, lanes(h)] if hh == h else zero64 for hh in range(A_HEADS)], axis=-1)
                 for h in range(A_HEADS)], axis=0)
            kq = _dot_nt(jnp.concatenate([k16, q16], axis=0), k_heads)
            nmat[b, c] = -(beta_p[b, c] * kq[:CHUNK] * jnp.where(ti_p > ii_p, dec_p[b, c], 0.0))
            akd_s[wslot, b * nch + c, 0:CHUNK, :] = (kq[CHUNK:] * dec_p[b, c]).astype(BF16)
        stage_done()

        rsum = dict(nmat)
        pw16 = {g_: nmat[g_].astype(BF16) for g_ in groups_b}
        pw = {g_: _dot(pw16[g_], block_diag(pw16[g_])) for g_ in groups_b}
        stage_done()
        for step in range(1, 6):
            last = step == 5
            pw16 = {g_: pw[g_].astype(BF16) for g_ in groups_b}
            rp = {}
            for g_ in groups_b:
                r16 = rsum[g_].astype(BF16)
                rp[g_] = _dot(r16 if last else jnp.concatenate([r16, pw16[g_]], axis=0), block_diag(pw16[g_]))
            for g_ in groups_b:
                rsum[g_] = rsum[g_] + pw[g_] + rp[g_][:CHUNK]
                if not last:
                    pw[g_] = rp[g_][CHUNK:]
            stage_done()

        for (b, c) in groups_b:
            for h in range(A_HEADS):
                u_ = (b, c, h)
                i = uid[u_]
                kc = k_ref[b, rows(c), lanes(h)]
                rhs = jnp.concatenate([(beta[u_] * eg[u_]) * kc, beta[u_] * v_ref[b, rows(c), lanes(h)]],
                                      axis=-1)
                rhs16 = rhs.astype(BF16)
                rhs_rows = jnp.concatenate([rhs16 if hh == h else zrhs for hh in range(A_HEADS)], axis=0)
                sol = rhs + _dot(rsum[b, c].astype(BF16), rhs_rows)
                wq_s[wslot, i] = jnp.concatenate([sol[:, :A_DK], eg[u_] * q_ref[b, rows(c), lanes(h)]],
                                                 axis=0).astype(BF16)
                ut_s[wslot, i] = sol[:, A_DK:]
                gl_s[wslot, i] = jnp.broadcast_to(jnp.exp(g_last[u_]), (1, A_DV))
        for (b, c) in groups_b:
            kd = [jnp.exp(g_last[b, c, h] - g_col[b, c, h]) * k_ref[b, rows(c), lanes(h)]
                  for h in range(A_HEADS)]
            for p in range(A_HEADS // 2):
                akd_s[wslot, b * nch + c, CHUNK:, p * LANES:(p + 1) * LANES] = (
                    jnp.concatenate([kd[2 * p], kd[2 * p + 1]], axis=0).T.astype(BF16))
        stage_done()

    for b0 in range(0, bsz, DELTA_WAVE):
        prepare(range(b0, b0 + DELTA_WAVE))
    run_rec(0)

    for b in range(bsz):
        for h in range(A_HEADS):
            s_scr[b * A_HEADS + h] = s_cur[b, h]

    @pl.when(t == pl.num_programs(0) - 1)
    def _():
        for b in range(bsz):
            for h in range(A_HEADS):
                st_ref[b, h] = s_cur[b, h]


def _delta(qkv, gb, za, na_row):
    bsz, t, _ = qkv.shape
    ct = DELTA_CT
    nt = t // ct
    n_units = bsz * (ct // CHUNK) * A_HEADS
    prep = lambda w, j=0: pl.BlockSpec((bsz, ct, w), lambda i: (0, jnp.minimum(i, nt - 1), j))
    rec = lambda w: pl.BlockSpec((bsz, ct, w), lambda i: (0, jnp.maximum(i - 1, 0), 0))
    return pl.pallas_call(
        _delta_kernel,
        grid=(nt + 1,),
        in_specs=[prep(A_WIDTH, 0), prep(A_WIDTH, 1), prep(A_WIDTH, 2), prep(LANES), rec(A_WIDTH),
                  pl.BlockSpec((1, A_DV), lambda i: (0, 0))],
        out_specs=[rec(A_WIDTH),
                   pl.BlockSpec((bsz, A_HEADS, A_DK, A_DV), lambda i: (0, 0, 0, 0))],
        out_shape=[jax.ShapeDtypeStruct((bsz, t, A_WIDTH), BF16),
                   jax.ShapeDtypeStruct((bsz, A_HEADS, A_DK, A_DV), F32)],
        scratch_shapes=[pltpu.VMEM((bsz * A_HEADS, A_DK, A_DV), F32),
                        pltpu.VMEM((2, n_units, 2 * CHUNK, A_DK), BF16),
                        pltpu.VMEM((2, n_units, CHUNK, A_DV), F32),
                        pltpu.VMEM((2, n_units // A_HEADS, CHUNK + A_DK, A_HEADS * CHUNK), BF16),
                        pltpu.VMEM((2, n_units, 1, A_DV), F32)],
        compiler_params=pltpu.CompilerParams(dimension_semantics=("arbitrary",),
                                             vmem_limit_bytes=VMEM_LIMIT),
        name="delta",
    )(qkv, qkv, qkv, gb, za, na_row)


def _swa_out_kernel(sink_ref, qb_ref, kc_ref, kp_ref, krc_ref, krp_ref, v0c_ref, v0p_ref, v1c_ref, v1p_ref,
                    zb_ref, oa_ref, x_ref, gate_ref, w_ref, g_ref, b_ref, y_ref):
    n = pl.program_id(1)
    gate = gate_ref[pl.ds(pl.program_id(0), 1), :]
    tq = qb_ref.shape[1]
    blk = WINDOW
    kx = (jnp.concatenate([kp_ref[0], kc_ref[0]], axis=0), jnp.concatenate([krp_ref[0], krc_ref[0]], axis=0))
    vd = (jnp.concatenate([v0p_ref[0], v0c_ref[0]], axis=0), jnp.concatenate([v1p_ref[0], v1c_ref[0]], axis=0))

    a = lax.broadcasted_iota(jnp.int32, (2 * blk, 2 * blk), 0) % blk
    j = lax.broadcasted_iota(jnp.int32, (2 * blk, 2 * blk), 1)
    rel = a + blk - j
    band = (rel >= 0) & (rel <= WINDOW)
    band_first = band & ((n > 0) | (j >= blk))
    top = lax.broadcasted_iota(jnp.int32, (2 * blk, 1), 0) < blk
    low = _lane((blk, LANES)) < B_HD
    zero = jnp.zeros((blk, LANES), BF16)

    qrows = lambda i: slice(i * blk, (i + 1) * blk)
    krows = lambda i: slice(i * blk, (i + 2) * blk)
    sink = {(kh, half): jnp.where(top, sink_ref[kh * B_GROUP + half] * LOG2E,
                                  sink_ref[kh * B_GROUP + half + 2] * LOG2E)
            for kh in range(B_KV_HEADS) for half in range(2)}
    for i0 in range(0, tq // blk, SWA_WAVE):
        blocks = range(i0, i0 + SWA_WAVE)
        units = [(i, kh, half) for i in blocks for kh in range(B_KV_HEADS) for half in range(2)]
        mix_a = {i: _dot(oa_ref[0, qrows(i), :], w_ref[0:A_WIDTH, :]) for i in blocks}
        sc = {}
        for (i, kh, half) in units:
            qs = []
            for g in range(2):
                grp = kh * 2 + g
                xg = qb_ref[0, qrows(i), grp * LANES:(grp + 1) * LANES]
                qs.append(jnp.where(low if half == 0 else jnp.logical_not(low), xg, zero))
            qz = jnp.concatenate(qs, axis=0)
            sc[i, kh, half] = _dot_nt(qz, kx[0 if kh == half else 1][krows(i)])
        p, den = {}, {}
        for u_ in units:
            i, kh, half = u_
            s_m = jnp.where(band_first if i == 0 else band, sc[u_], -jnp.inf)
            m = jnp.maximum(jnp.max(s_m, axis=-1, keepdims=True), sink[kh, half])
            e = jnp.exp2(s_m - m)
            den[u_] = jnp.sum(e, axis=-1, keepdims=True) + jnp.exp2(sink[kh, half] - m)
            p[u_] = e.astype(BF16)
        pv = {u_: _dot(p[u_], vd[u_[1]][krows(u_[0])]) for u_ in units}
        outs = {u_: pv[u_] / den[u_] for u_ in units}
        for i in blocks:
            ob = []
            for grp in range(B_WIDTH // LANES):
                kh, g = grp // 2, grp % 2
                og = jnp.where(low, outs[i, kh, 0][g * blk:(g + 1) * blk], outs[i, kh, 1][g * blk:(g + 1) * blk])
                ob.append((og * _silu(zb_ref[0, qrows(i), grp * LANES:(grp + 1) * LANES])).astype(BF16))
            mix = mix_a[i] + _dot(jnp.concatenate(ob, axis=-1), w_ref[A_WIDTH:MIX_WIDTH, :])
            r = DEEPNORM_ALPHA * x_ref[0, qrows(i), :] + (1.0 + gate) * mix
            y_ref[0, qrows(i), :] = _layer_norm(r, g_ref[...], b_ref[...])


def _swa_out(sinks, qb, kvl, zb, oa, x, mod, mod_row0, w_out, ln_g, ln_b):
    bsz, t, _ = qb.shape
    tq = SWA_TQ
    per = tq // WINDOW
    cur = lambda w: pl.BlockSpec((1, tq, w), lambda b, i: (b, i, 0))
    kv_cur = lambda j: pl.BlockSpec((1, tq, LANES), lambda b, i: (b, i, j))
    kv_prev = lambda j: pl.BlockSpec((1, WINDOW, LANES), lambda b, i: (b, jnp.maximum(i * per - 1, 0), j))
    const2 = lambda s: pl.BlockSpec(s, lambda b, i: (0, 0))
    return pl.pallas_call(
        _swa_out_kernel,
        grid=(bsz, t // tq),
        in_specs=[pl.BlockSpec(memory_space=pltpu.SMEM), cur(B_WIDTH),
                  kv_cur(0), kv_prev(0), kv_cur(1), kv_prev(1), kv_cur(2), kv_prev(2), kv_cur(3), kv_prev(3),
                  cur(B_WIDTH), cur(A_WIDTH), cur(D_MODEL),
                  pl.BlockSpec((8, D_MODEL), lambda b, i: (mod_row0 // 8, 2)),
                  const2((MIX_WIDTH, D_MODEL)), const2((1, D_MODEL)), const2((1, D_MODEL))],
        out_specs=cur(D_MODEL),
        out_shape=jax.ShapeDtypeStruct((bsz, t, D_MODEL), F32),
        compiler_params=pltpu.CompilerParams(dimension_semantics=("arbitrary", "arbitrary"),
                                             vmem_limit_bytes=VMEM_LIMIT),
        name="swa_out",
    )(sinks, qb, kvl, kvl, kvl, kvl, kvl, kvl, kvl, kvl, zb, oa, x, mod, w_out, ln_g, ln_b)


def _out_kernel(oa_ref, ob_ref, x_ref, gate_ref, w_ref, g_ref, b_ref, y_ref):
    mix = _dot(oa_ref[...], w_ref[0:A_WIDTH, :]) + _dot(ob_ref[...], w_ref[A_WIDTH:MIX_WIDTH, :])
    r = DEEPNORM_ALPHA * x_ref[...] + (1.0 + gate_ref[...]) * mix
    y_ref[...] = _layer_norm(r, g_ref[...], b_ref[...])


def _out(oa, ob, x, mod, w_out, ln_g, ln_b):
    n = x.shape[0]
    full = lambda s: pl.BlockSpec(s, lambda i: (0, 0))
    return pl.pallas_call(
        _out_kernel,
        grid=(1,),
        in_specs=[full((n, A_WIDTH)), full((n, B_WIDTH)), full((n, D_MODEL)),
                  pl.BlockSpec((n, D_MODEL), lambda i: (0, 2)),
                  full((MIX_WIDTH, D_MODEL)), full((1, D_MODEL)), full((1, D_MODEL))],
        out_specs=full((n, D_MODEL)),
        out_shape=jax.ShapeDtypeStruct((n, D_MODEL), F32),
        compiler_params=pltpu.CompilerParams(dimension_semantics=("arbitrary",),
                                             vmem_limit_bytes=VMEM_LIMIT),
        name="out",
    )(oa, ob, x, mod, w_out, ln_g, ln_b)


def _sproj_kernel(x_ref, mod_ref, w_ref, cw_ref, cst_ref, alog_ref, dt_ref, cos_ref, sin_ref,
                  q_ref, k_ref, v_ref, za_ref, gb_ref, qb_ref, kb_ref, vb_ref, zb_ref, ncs_ref):
    shift = mod_ref[:, 0:D_MODEL]
    scale = mod_ref[:, D_MODEL:2 * D_MODEL]
    h = (x_ref[...] * (1.0 + scale) + shift).astype(BF16)

    for gi, o_ref in enumerate((q_ref, k_ref, v_ref)):
        c0 = gi * A_WIDTH
        cs = slice(c0, c0 + A_WIDTH)
        u = _dot(h, w_ref[:, cs])
        acc = cst_ref[0, :, cs] * cw_ref[0:1, cs]
        acc = acc + cst_ref[1, :, cs] * cw_ref[1:2, cs]
        acc = acc + cst_ref[2, :, cs] * cw_ref[2:3, cs]
        acc = acc + u * cw_ref[3:4, cs]
        y = _silu(acc)
        if gi == 0:
            y = _l2norm_heads(y, A_DK ** -0.5)
        elif gi == 1:
            y = _l2norm_heads(y, 1.0)
        o_ref[...] = y
        ncs_ref[0, :, cs] = cst_ref[1, :, cs]
        ncs_ref[1, :, cs] = cst_ref[2, :, cs]
        ncs_ref[2, :, cs] = u

    za_ref[...] = _dot(h, w_ref[:, C_ZA:C_ZA + A_WIDTH])
    gb_ref[...] = _gate_lanes(_dot(h, w_ref[:, C_BD:C_BD + LANES]), alog_ref[...], dt_ref[...])

    cos = cos_ref[...]
    sin = sin_ref[...]
    uq = _dot(h, w_ref[:, C_QB:C_QB + B_WIDTH])
    for g in range(B_WIDTH // LANES):
        qb_ref[:, g * LANES:(g + 1) * LANES] = (
            _rotary_group(uq[:, g * LANES:(g + 1) * LANES], cos, sin) * (B_HD ** -0.5))
    kb_ref[...] = _rotary_group(_dot(h, w_ref[:, C_KB:C_KB + LANES]), cos, sin)
    vb_ref[...] = _dot(h, w_ref[:, C_VB:C_VB + LANES])
    zb_ref[...] = _dot(h, w_ref[:, C_ZB:C_ZB + B_WIDTH])


def _sproj(x, mod_s, w_r, conv_w, cst, alog_row, dt_row, cos_row, sin_row):
    n = x.shape[0]
    full = lambda s: pl.BlockSpec(s, lambda i: (0,) * len(s))
    wide = lambda w: jax.ShapeDtypeStruct((n, w), F32)
    return pl.pallas_call(
        _sproj_kernel,
        grid=(1,),
        in_specs=[full((n, D_MODEL)), pl.BlockSpec((n, 3 * D_MODEL), lambda i: (0, 0)),
                  full((D_MODEL, W_COLS)),
                  full((CONV_W, A_QKV)), full((CONV_W - 1, n, A_QKV)),
                  full((1, LANES)), full((1, LANES)), full((1, LANES)), full((1, LANES))],
        out_specs=[full((n, A_WIDTH)), full((n, A_WIDTH)), full((n, A_WIDTH)), full((n, A_WIDTH)),
                   full((n, LANES)), full((n, B_WIDTH)), full((n, LANES)), full((n, LANES)),
                   full((n, B_WIDTH)), full((CONV_W - 1, n, A_QKV))],
        out_shape=[wide(A_WIDTH), wide(A_WIDTH), wide(A_WIDTH), wide(A_WIDTH), wide(LANES),
                   wide(B_WIDTH), wide(LANES), wide(LANES), wide(B_WIDTH),
                   jax.ShapeDtypeStruct((CONV_W - 1, n, A_QKV), F32)],
        compiler_params=pltpu.CompilerParams(dimension_semantics=("arbitrary",),
                                             vmem_limit_bytes=VMEM_LIMIT),
        name="sproj",
    )(x, mod_s, w_r, conv_w, cst, alog_row, dt_row, cos_row, sin_row)


def _sstep_kernel(sink_ref, q_ref, k_ref, v_ref, gb_ref, za_ref, na_ref, st_ref,
                  qb_ref, kn_ref, vn_ref, zb_ref, ck_ref, cv_ref,
                  oa_ref, ob_ref, nst_ref, nck_ref, ncv_ref,
                  o_scr, ob_scr):
    bt = q_ref.shape[0]
    gbv = gb_ref[...]

    pick = (lax.broadcasted_iota(jnp.int32, (bt, bt * A_DV), 1) // A_DV
            == lax.broadcasted_iota(jnp.int32, (bt, bt * A_DV), 0))
    pick = jnp.where(pick, 1.0, 0.0).astype(BF16)
    for h in range(A_HEADS):
        hs = slice(h * A_DK, (h + 1) * A_DK)
        q_rep = _dot(q_ref[:, hs].T.astype(BF16), pick)
        k_rep = _dot(k_ref[:, hs].T.astype(BF16), pick)
        for bb in range(bt):
            eg = jnp.exp(gbv[bb:bb + 1, A_HEADS + h:A_HEADS + h + 1])
            beta = gbv[bb:bb + 1, h:h + 1]
            kcol = k_rep[:, bb * A_DV:(bb + 1) * A_DV]
            qcol = q_rep[:, bb * A_DV:(bb + 1) * A_DV]
            s1 = eg * st_ref[bb, h]
            pred = jnp.sum(kcol * s1, axis=0, keepdims=True)
            upd = beta * (v_ref[bb:bb + 1, hs] - pred)
            s2 = s1 + kcol * upd
            nst_ref[bb, h] = s2
            o_scr[bb:bb + 1, hs] = jnp.sum(qcol * s2, axis=0, keepdims=True)
    na = na_ref[...]
    for h in range(A_HEADS):
        hs = slice(h * A_DK, (h + 1) * A_DK)
        o = o_scr[:, hs]
        on = o * lax.rsqrt(jnp.mean(o * o, axis=-1, keepdims=True) + RMS_EPS) * na
        oa_ref[:, hs] = (on * _silu(za_ref[:, hs])).astype(BF16)

    row8 = lax.broadcasted_iota(jnp.int32, (B_HEADS, LANES), 0)
    lane8 = _lane((B_HEADS, LANES))
    own_half = (lane8 >= B_HD) == (row8 >= B_GROUP)
    rcol = lax.broadcasted_iota(jnp.int32, (B_HEADS, 1), 0)
    sink = jnp.zeros((B_HEADS, 1), F32)
    for r in range(B_HEADS):
        sink = jnp.where(rcol == r, sink_ref[r], sink)
    qv = qb_ref[...]
    qv_r = jnp.concatenate([pltpu.roll(qv[:, g * LANES:(g + 1) * LANES], B_HD, axis=1)
                            for g in range(B_WIDTH // LANES)], axis=-1)
    kn_t = kn_ref[...].T
    vn_t = vn_ref[...].T
    newest = _lane((LANES, WINDOW)) == WINDOW - 1
    qzs, scs = [], []
    for bb in range(bt):
        qz = jnp.zeros((B_HEADS, LANES), F32)
        for r in range(B_HEADS):
            grp, half, kh = r // 2, r % 2, r // B_GROUP
            src = qv if half == kh else qv_r
            qz = jnp.where(row8 == r, src[bb:bb + 1, grp * LANES:(grp + 1) * LANES], qz)
        qzs.append(jnp.where(own_half, qz, 0.0))
    for bb in range(bt):
        scs.append(_dot(qzs[bb], ck_ref[bb]))
    ps, pnews, dens = [], [], []
    for bb in range(bt):
        sc_new = jnp.sum(qzs[bb] * kn_ref[bb:bb + 1, :], axis=-1, keepdims=True)
        m = jnp.maximum(jnp.maximum(jnp.max(scs[bb], axis=-1, keepdims=True), sc_new), sink)
        p = jnp.exp(scs[bb] - m)
        p_new = jnp.exp(sc_new - m)
        ps.append(p)
        pnews.append(p_new)
        dens.append(jnp.sum(p, axis=-1, keepdims=True) + p_new + jnp.exp(sink - m))
    pvs = [_dot_nt(ps[bb], cv_ref[bb]) for bb in range(bt)]
    for bb in range(bt):
        o = (pvs[bb] + pnews[bb] * vn_ref[bb:bb + 1, :]) / dens[bb]
        o = jnp.where(own_half, o, 0.0)
        ob_scr[bb * B_HEADS:(bb + 1) * B_HEADS, :] = o + pltpu.roll(o, B_HD, axis=1)
    for bb in range(bt):
        nck_ref[bb] = jnp.where(newest, kn_t[:, bb:bb + 1], pltpu.roll(ck_ref[bb], WINDOW - 1, axis=1))
        ncv_ref[bb] = jnp.where(newest, vn_t[:, bb:bb + 1], pltpu.roll(cv_ref[bb], WINDOW - 1, axis=1))
    low = _lane((bt, LANES)) < B_HD
    for grp in range(B_WIDTH // LANES):
        even = ob_scr[pl.ds(2 * grp, bt, stride=B_HEADS), :]
        odd = ob_scr[pl.ds(2 * grp + 1, bt, stride=B_HEADS), :]
        gs = slice(grp * LANES, (grp + 1) * LANES)
        ob_ref[:, gs] = (jnp.where(low, even, odd) * _silu(zb_ref[:, gs])).astype(BF16)


def _sstep(sinks, q, k, v, gb, za, na_row, state, qb, kn, vn, zb, ck, cv):
    n = q.shape[0]
    bt = STEP_BT
    row = lambda w: pl.BlockSpec((bt, w), lambda i: (i, 0))
    st_spec = pl.BlockSpec((bt, A_HEADS, A_DK, A_DV), lambda i: (i, 0, 0, 0))
    c_spec = pl.BlockSpec((bt, WINDOW, LANES), lambda i: (i, 0, 0))
    return pl.pallas_call(
        _sstep_kernel,
        grid=(n // bt,),
        in_specs=[pl.BlockSpec(memory_space=pltpu.SMEM),
                  row(A_WIDTH), row(A_WIDTH), row(A_WIDTH), row(LANES), row(A_WIDTH),
                  pl.BlockSpec((1, A_DV), lambda i: (0, 0)), st_spec,
                  row(B_WIDTH), row(LANES), row(LANES), row(B_WIDTH), c_spec, c_spec],
        out_specs=[row(A_WIDTH), row(B_WIDTH), st_spec, c_spec, c_spec],
        out_shape=[jax.ShapeDtypeStruct((n, A_WIDTH), BF16),
                   jax.ShapeDtypeStruct((n, B_WIDTH), BF16),
                   jax.ShapeDtypeStruct((n, A_HEADS, A_DK, A_DV), F32),
                   jax.ShapeDtypeStruct((n, WINDOW, LANES), F32),
                   jax.ShapeDtypeStruct((n, WINDOW, LANES), F32)],
        scratch_shapes=[pltpu.VMEM((bt, A_WIDTH), F32), pltpu.VMEM((bt * B_HEADS, LANES), F32)],
        compiler_params=pltpu.CompilerParams(dimension_semantics=("arbitrary",),
                                             vmem_limit_bytes=VMEM_LIMIT),
        name="sstep",
    )(sinks, q, k, v, gb, za, na_row, state, qb, kn, vn, zb, ck, cv)


def _rope_tables(pos):
    half = B_HD // 2
    inv = 1.0 / (ROPE_THETA ** (np.arange(half, dtype=np.float64) / half))
    ang = np.asarray(pos, np.float64)[:, None] * inv[None, :]
    cos, sin = np.cos(ang), np.sin(ang)
    reps = LANES // B_HD
    return (jnp.asarray(np.tile(np.concatenate([cos, cos], -1), (1, reps)), F32),
            jnp.asarray(np.tile(np.concatenate([-sin, sin], -1), (1, reps)), F32))


def _pad_row(vec, offset):
    return jnp.pad(vec.astype(F32).reshape(1, -1), ((0, 0), (offset, LANES - offset - vec.shape[0])))


def _layer(x_prompt, x_sample, state_conv, state_delta, cache_k, cache_v, c_prompt, c_sample,
           w_ada, b_ada, w_in, conv_w, a_log, dt_bias, norm_a, sinks, w_out, ln_g, ln_b):
    bsz, seq, _ = x_prompt.shape
    n_s = x_sample.shape[0]

    w_r = _wprep(jnp.swapaxes(w_in, 0, 1))
    w_o = w_out.astype(BF16)
    alog_row = _pad_row(a_log, A_HEADS)
    dt_row = _pad_row(dt_bias, A_HEADS)
    na_row = norm_a.reshape(1, A_DV)
    g_row = ln_g.reshape(1, D_MODEL)
    b_row = ln_b.reshape(1, D_MODEL)

    assert n_s % 8 == 0 and bsz <= 8
    mod = _ada(c_sample, c_prompt, w_ada, b_ada.reshape(1, 3 * D_MODEL))

    cos_p, sin_p = _rope_tables(np.arange(seq))
    (qkv, za, gb, qb, kvl, zb, conv_p, kb_last, vb_last) = _proj(
        x_prompt, mod, n_s, w_r, conv_w, alog_row, dt_row, cos_p, sin_p)
    oa, delta_p = _delta(qkv, gb, za, na_row)
    y_p = _swa_out(sinks, qb, kvl, zb, oa, x_prompt, mod, n_s, w_o, g_row, b_row)
    swa_k_p = kb_last.reshape(bsz, WINDOW, B_KV_HEADS, B_HD)
    swa_v_p = vb_last.reshape(bsz, WINDOW, B_KV_HEADS, B_HD)

    cos_s, sin_s = _rope_tables(np.array([PAST_LEN]))
    xs = x_sample.reshape(n_s, D_MODEL)
    cst = jnp.transpose(state_conv, (1, 0, 2))
    sq, sk, sv, sza, sgb, sqb, skn, svn, szb, ncs = _sproj(xs, mod, w_r, conv_w, cst, alog_row, dt_row,
                                                           cos_s, sin_s)
    soa, sob, delta_s, nck, ncv = _sstep(sinks, sq, sk, sv, sgb, sza, na_row, state_delta,
                                         sqb, skn, svn, szb,
                                         jnp.swapaxes(cache_k.reshape(n_s, WINDOW, LANES), 1, 2),
                                         jnp.swapaxes(cache_v.reshape(n_s, WINDOW, LANES), 1, 2))
    y_s = _out(soa, sob, xs, mod, w_o, g_row, b_row)
    conv_s = jnp.transpose(ncs, (1, 0, 2))
    unpack = lambda c: jnp.swapaxes(c, 1, 2).reshape(n_s, WINDOW, B_KV_HEADS, B_HD)
    return (y_p, y_s.reshape(n_s, 1, D_MODEL), conv_p, delta_p, swa_k_p, swa_v_p,
            conv_s, delta_s, unpack(nck), unpack(ncv))


def kernel(x_prompt, x_sample, state_conv, state_delta, cache_swa_k, cache_swa_v, c_prompt, c_sample,
           w_ada, b_ada, w_in, conv_w, a_log, dt_bias, norm_a, sinks, w_out, ln_g, ln_b):
    assert w_ada.shape[0] == DEPTH == 1
    outs = _layer(x_prompt, x_sample, state_conv[0], state_delta[0], cache_swa_k[0], cache_swa_v[0],
                  c_prompt, c_sample, w_ada[0], b_ada[0], w_in[0], conv_w[0], a_log[0], dt_bias[0],
                  norm_a[0], sinks[0], w_out[0], ln_g[0], ln_b[0])
    y_p, y_s = outs[0], outs[1]
    return (y_p, y_s) + tuple(o[None] for o in outs[2:])
```

```python
import jax
import jax.numpy as jnp
import numpy as np
from jax import lax
from jax.experimental import pallas as pl
from jax.experimental.pallas import tpu as pltpu

F32 = jnp.float32
BF16 = jnp.bfloat16

D_MODEL = 1024
DEPTH = 1
PAST_LEN = 8192
A_HEADS = 4
A_DK = 128
A_DV = 128
A_WIDTH = A_HEADS * A_DV
A_QKV = 3 * A_WIDTH
CONV_W = 4
CHUNK = 64
B_HEADS = 8
B_KV_HEADS = 2
B_HD = 64
B_GROUP = B_HEADS // B_KV_HEADS
B_WIDTH = B_HEADS * B_HD
B_KV_WIDTH = B_KV_HEADS * B_HD
WINDOW = 128
ROPE_THETA = 10000.0
MIX_WIDTH = A_WIDTH + B_WIDTH
DEEPNORM_ALPHA = (2 * DEPTH) ** 0.25
LOG2E = 1.4426950408889634
LN_EPS = 1e-5
RMS_EPS = 1e-6
L2_EPS = 1e-6

OFF_A_Z = A_QKV
OFF_A_BETA = OFF_A_Z + A_WIDTH
OFF_A_DECAY = OFF_A_BETA + A_HEADS
OFF_B_Q = OFF_A_DECAY + A_HEADS
OFF_B_K = OFF_B_Q + B_WIDTH
OFF_B_V = OFF_B_K + B_KV_WIDTH
OFF_B_Z = OFF_B_V + B_KV_WIDTH
PROJ_COLS = OFF_B_Z + B_WIDTH

LANES = 128
C_QKV = 0
C_ZA = C_QKV + A_QKV
C_QB = C_ZA + A_WIDTH
C_KB = C_QB + B_WIDTH
C_VB = C_KB + B_KV_WIDTH
C_ZB = C_VB + B_KV_WIDTH
C_BD = C_ZB + B_WIDTH
WPREP_TN = 256
W_COLS = C_BD + WPREP_TN

VMEM_LIMIT = 56 * 1024 * 1024

ADA_TN = 1536
PROJ_TM = 512
PROJ_CW = 256
PROJ_PARTS = 4
DELTA_CT = 256
DELTA_WAVE = 2
SWA_TQ = 512
SWA_WAVE = 2
STEP_BT = 16


def _dot(a, b):
    return jnp.dot(a, b, preferred_element_type=F32)


def _dot_nt(a, b):
    return lax.dot_general(a, b, (((1,), (1,)), ((), ())), preferred_element_type=F32)


def _silu(x):
    return x * jax.nn.sigmoid(x)


def _softplus(x):
    return jnp.maximum(x, 0.0) + jnp.log1p(jnp.exp(-jnp.abs(x)))


def _lane(shape):
    return lax.broadcasted_iota(jnp.int32, shape, len(shape) - 1)


def _l2norm_heads(y, scale):
    outs = []
    for h in range(y.shape[1] // A_DK):
        xh = y[:, h * A_DK:(h + 1) * A_DK]
        ss = jnp.sum(xh * xh, axis=-1, keepdims=True)
        xn = xh * lax.rsqrt(ss + L2_EPS)
        outs.append(xn * scale if scale != 1.0 else xn)
    return jnp.concatenate(outs, axis=-1)


def _rotary_group(xg, cos, sin_signed):
    lane = _lane(xg.shape)
    swapped = jnp.where((lane % B_HD) < (B_HD // 2),
                        pltpu.roll(xg, LANES - B_HD // 2, axis=1),
                        pltpu.roll(xg, B_HD // 2, axis=1))
    return xg * cos + swapped * sin_signed


def _kv_layouts(kb, vb):
    low = _lane(kb.shape) < B_HD
    kbr = pltpu.roll(kb, B_HD, axis=1)
    vbr = pltpu.roll(vb, B_HD, axis=1)
    return kb, kbr, jnp.where(low, vb, vbr), jnp.where(low, vbr, vb)


def _gate_lanes(bd, alog_row, dt_row):
    lane = _lane(bd.shape)
    g = -jnp.exp(alog_row) * _softplus(bd + dt_row)
    return jnp.where(lane < A_HEADS, jax.nn.sigmoid(bd), g)


def _layer_norm(r, g, b):
    mu = jnp.mean(r, axis=-1, keepdims=True)
    d = r - mu
    var = jnp.mean(d * d, axis=-1, keepdims=True)
    return d * lax.rsqrt(var + LN_EPS) * g + b


def _wprep_kernel(wa_ref, wb_ref, o_ref):
    tn = wa_ref.shape[0]
    o_ref[:, 0:tn] = wa_ref[...].T.astype(BF16)
    xb = wb_ref[...]
    tail = pl.program_id(0) == pl.num_programs(0) - 1
    row = lax.broadcasted_iota(jnp.int32, xb.shape, 0)
    xb = jnp.where(jnp.logical_and(tail, row >= 2 * A_HEADS), 0.0, xb)
    o_ref[:, tn:2 * tn] = xb.T.astype(BF16)


def _wprep(w_t):
    tn = WPREP_TN
    n_a, n_b = OFF_A_BETA // tn, (PROJ_COLS - OFF_B_Q) // tn
    assert n_a * tn == OFF_A_BETA and n_b * tn == PROJ_COLS - OFF_B_Q and OFF_A_BETA + tn <= PROJ_COLS

    assert (n_a + n_b + 1) % 2 == 0

    def src_row(j):
        return jnp.where(j < n_a, j * tn, jnp.where(j < n_a + n_b, OFF_B_Q + (j - n_a) * tn, OFF_A_BETA))

    src = lambda k: pl.BlockSpec((pl.Element(tn), pl.Element(D_MODEL)),
                                 lambda j: (pl.multiple_of(src_row(2 * j + k), 8), 0))
    return pl.pallas_call(
        _wprep_kernel,
        grid=((n_a + n_b + 1) // 2,),
        in_specs=[src(0), src(1)],
        out_specs=pl.BlockSpec((D_MODEL, 2 * tn), lambda j: (0, j)),
        out_shape=jax.ShapeDtypeStruct((D_MODEL, W_COLS), BF16),
        compiler_params=pltpu.CompilerParams(dimension_semantics=("arbitrary",),
                                             vmem_limit_bytes=VMEM_LIMIT),
        name="wprep",
    )(w_t, w_t)


def _ada_kernel(cs_ref, cp_ref, w_ref, b_ref, o_ref):
    n_s, n_p = cs_ref.shape[0], cp_ref.shape[0]
    w = w_ref[...].astype(BF16)
    o_ref[0:n_s, :] = _dot(cs_ref[...].astype(BF16), w) + b_ref[...]
    o_ref[n_s:n_s + 8, :] = jnp.zeros((8, o_ref.shape[1]), F32)
    o_ref[n_s:n_s + n_p, :] = _dot(cp_ref[...].astype(BF16), w) + b_ref[...]


def _ada(c_sample, c_prompt, w_ada, b_ada):
    n_s, n_p = c_sample.shape[0], c_prompt.shape[0]
    rows = n_s + 8
    tn = ADA_TN
    return pl.pallas_call(
        _ada_kernel,
        grid=(3 * D_MODEL // tn,),
        in_specs=[pl.BlockSpec((n_s, D_MODEL), lambda j: (0, 0)),
                  pl.BlockSpec((n_p, D_MODEL), lambda j: (0, 0)),
                  pl.BlockSpec((D_MODEL, tn), lambda j: (0, j)),
                  pl.BlockSpec((1, tn), lambda j: (0, j))],
        out_specs=pl.BlockSpec((rows, tn), lambda j: (0, j)),
        out_shape=jax.ShapeDtypeStruct((rows, 3 * D_MODEL), F32),
        compiler_params=pltpu.CompilerParams(dimension_semantics=("arbitrary",),
                                             vmem_limit_bytes=VMEM_LIMIT),
        name="ada",
    )(c_sample, c_prompt, w_ada, b_ada)


def _proj_kernel(x_ref, mod_ref, w_ref, cw_ref, alog_ref, dt_ref, cos_ref, sin_ref,
                 qkv_ref, za_ref, gb_ref, qb_ref, kvl_ref, zb_ref, cst_ref, kbl_ref, vbl_ref, ubuf):
    tm = x_ref.shape[1]
    t = pl.program_id(1)

    @pl.when(t == 0)
    def _():
        ubuf[...] = jnp.zeros(ubuf.shape, F32)

    brow = pl.ds(pl.program_id(0), 1)
    shift = mod_ref[brow, 0:D_MODEL]
    scale = mod_ref[brow, D_MODEL:2 * D_MODEL]

    rp = tm // PROJ_PARTS
    cw = PROJ_CW
    sub = lax.broadcasted_iota(jnp.int32, (rp // 8, 8, cw), 1)
    pieces = [slice(c0, c0 + cw) for c0 in range(0, A_QKV, cw)]

    def matmuls(r0):
        h = (x_ref[0, r0:r0 + rp, :] * (1.0 + scale) + shift).astype(BF16)
        return ([_dot(h, w_ref[:, cs]) for cs in pieces],
                _dot(h, w_ref[:, C_ZA:C_ZA + A_WIDTH]), _dot(h, w_ref[:, C_BD:C_BD + LANES]),
                _dot(h, w_ref[:, C_QB:C_QB + B_WIDTH]), _dot(h, w_ref[:, C_KB:C_KB + 2 * LANES]),
                _dot(h, w_ref[:, C_ZB:C_ZB + B_WIDTH]))

    def epilogue(r0, results):
        rs = slice(r0, r0 + rp)
        us, za, ubd, uq, ukv, zb = results

        def conv_epilogue(cs, u):
            gi = cs.start // A_WIDTH
            groups = jnp.concatenate([ubuf[:, cs], u], axis=0).reshape(rp // 8 + 1, 8, cw)
            acc = None
            for j in range(CONV_W - 1, 0, -1):
                rot = pltpu.roll(groups, j, axis=1)
                term = (jnp.where(sub < j, rot[:-1], rot[1:]).reshape(rp, cw)
                        * cw_ref[CONV_W - 1 - j:CONV_W - j, cs])
                acc = term if acc is None else acc + term
            y = _silu(acc + u * cw_ref[CONV_W - 1:CONV_W, cs])
            if gi == 0:
                y = _l2norm_heads(y, A_DK ** -0.5)
            elif gi == 1:
                y = _l2norm_heads(y, 1.0)
            qkv_ref[0, rs, cs] = y
            ubuf[:, cs] = u[rp - 8:rp]
            if r0 + rp == tm:
                cst_ref[0, :, cs] = u[rp - (CONV_W - 1):rp]

        for cs, u in zip(pieces, us):
            conv_epilogue(cs, u)
        za_ref[0, rs, :] = za
        zb_ref[0, rs, :] = zb
        gb_ref[0, rs, :] = _gate_lanes(ubd, alog_ref[...], dt_ref[...])
        cos = cos_ref[rs, :]
        sin = sin_ref[rs, :]
        for g in range(B_WIDTH // LANES):
            qb_ref[0, rs, g * LANES:(g + 1) * LANES] = (
                _rotary_group(uq[:, g * LANES:(g + 1) * LANES], cos, sin) * (B_HD ** -0.5 * LOG2E)).astype(BF16)
        kb = _rotary_group(ukv[:, 0:LANES], cos, sin)
        vb = ukv[:, LANES:2 * LANES]
        for j, val in enumerate(_kv_layouts(kb, vb)):
            kvl_ref[0, rs, j * LANES:(j + 1) * LANES] = val.astype(BF16)
        return kb, vb

    starts = list(range(0, tm, rp))
    pending = matmuls(starts[0])
    for p, r0 in enumerate(starts):
        results = pending
        if p + 1 < len(starts):
            pending = matmuls(starts[p + 1])
        kb, vb = epilogue(r0, results)

    @pl.when(t == pl.num_programs(1) - 1)
    def _():
        kbl_ref[0] = kb[rp - WINDOW:rp]
        vbl_ref[0] = vb[rp - WINDOW:rp]


def _proj(x, mod, mod_row0, w_r, conv_w, alog_row, dt_row, cos_t, sin_t):
    bsz, t, _ = x.shape
    tm = PROJ_TM
    row = lambda w: pl.BlockSpec((1, tm, w), lambda b, i: (b, i, 0))
    const2 = lambda s: pl.BlockSpec(s, lambda b, i: (0, 0))
    per_b = lambda r, w: pl.BlockSpec((1, r, w), lambda b, i: (b, 0, 0))
    wide = lambda w, dt=F32: jax.ShapeDtypeStruct((bsz, t, w), dt)
    return pl.pallas_call(
        _proj_kernel,
        grid=(bsz, t // tm),
        in_specs=[row(D_MODEL),
                  pl.BlockSpec((8, 3 * D_MODEL), lambda b, i: (mod_row0 // 8, 0)),
                  const2((D_MODEL, W_COLS)),
                  const2((CONV_W, A_QKV)),
                  const2((1, LANES)), const2((1, LANES)),
                  pl.BlockSpec((tm, LANES), lambda b, i: (i, 0)),
                  pl.BlockSpec((tm, LANES), lambda b, i: (i, 0))],
        out_specs=[row(A_QKV), row(A_WIDTH), row(LANES), row(B_WIDTH), row(4 * LANES), row(B_WIDTH),
                   per_b(CONV_W - 1, A_QKV), per_b(WINDOW, LANES), per_b(WINDOW, LANES)],
        out_shape=[wide(A_QKV), wide(A_WIDTH), wide(LANES), wide(B_WIDTH, BF16), wide(4 * LANES, BF16),
                   wide(B_WIDTH),
                   jax.ShapeDtypeStruct((bsz, CONV_W - 1, A_QKV), F32),
                   jax.ShapeDtypeStruct((bsz, WINDOW, LANES), F32),
                   jax.ShapeDtypeStruct((bsz, WINDOW, LANES), F32)],
        scratch_shapes=[pltpu.VMEM((8, A_QKV), F32)],
        compiler_params=pltpu.CompilerParams(dimension_semantics=("arbitrary", "arbitrary"),
                                             vmem_limit_bytes=VMEM_LIMIT),
        name="proj",
    )(x, mod, w_r, conv_w, alog_row, dt_row, cos_t, sin_t)


def _delta_kernel(q_ref, k_ref, v_ref, gb_ref, za_ref, na_ref, oa_ref, st_ref,
                  s_scr, wq_s, ut_s, akd_s, gl_s):
    bsz, ct = q_ref.shape[0], q_ref.shape[1]
    nch = ct // CHUNK
    t = pl.program_id(0)
    wslot = t % 2
    rslot = 1 - wslot

    @pl.when(t == 0)
    def _():
        s_scr[...] = jnp.zeros(s_scr.shape, F32)
        wq_s[...] = jnp.zeros(wq_s.shape, BF16)
        ut_s[...] = jnp.zeros(ut_s.shape, F32)
        akd_s[...] = jnp.zeros(akd_s.shape, BF16)
        gl_s[...] = jnp.zeros(gl_s.shape, F32)

    units = [(b, c, h) for b in range(bsz) for c in range(nch) for h in range(A_HEADS)]
    uid = {u_: i for i, u_ in enumerate(units)}
    rows = lambda c: slice(c * CHUNK, (c + 1) * CHUNK)
    lanes = lambda h: slice(h * A_DK, (h + 1) * A_DK)
    na = na_ref[...]

    s_cur = {(b, h): s_scr[b * A_HEADS + h] for b in range(bsz) for h in range(A_HEADS)}
    ws, uu = {}, {}

    def rec_ws(c):
        for b in range(bsz):
            for h in range(A_HEADS):
                i = uid[b, c, h]
                ws[b, h] = _dot(wq_s[rslot, i], s_cur[b, h].astype(BF16))
                uu[b, h] = (ut_s[rslot, i] - ws[b, h][:CHUNK]).astype(BF16)

    def rec_ou(c):
        zpad = jnp.zeros((CHUNK, A_DV), BF16)
        for b in range(bsz):
            u_bd = jnp.concatenate(
                [jnp.concatenate([uu[b, h] if hh == h else zpad for hh in range(A_HEADS)], axis=-1)
                 for h in range(A_HEADS)], axis=0)
            ou = _dot(akd_s[rslot, b * nch + c], u_bd)
            for h in range(A_HEADS):
                o = ws[b, h][CHUNK:] + ou[:CHUNK, lanes(h)]
                s_cur[b, h] = gl_s[rslot, uid[b, c, h]] * s_cur[b, h] + ou[CHUNK:, lanes(h)]
                on = o * lax.rsqrt(jnp.mean(o * o, axis=-1, keepdims=True) + RMS_EPS) * na
                oa_ref[b, rows(c), lanes(h)] = (on * _silu(za_ref[b, rows(c), lanes(h)])).astype(BF16)

    rec_stages = []
    for c in range(nch):
        rec_stages += [lambda c=c: rec_ws(c), lambda c=c: rec_ou(c)]

    def run_rec(n_left_after):
        while rec_stages and len(rec_stages) > n_left_after:
            rec_stages.pop(0)()

    pk = A_HEADS * CHUNK
    low = _lane((CHUNK, LANES)) < CHUNK
    low_row = _lane((1, LANES)) < CHUNK
    ti_p = lax.broadcasted_iota(jnp.int32, (CHUNK, pk), 0)
    ii_p = _lane((CHUNK, pk)) % CHUNK
    zero64 = jnp.zeros((CHUNK, LANES), BF16)

    def pack(parts):
        return jnp.concatenate([jnp.where(low, parts[0], parts[1]), jnp.where(low, parts[2], parts[3])], axis=-1)

    def block_diag(x16):
        blocks = []
        for h in range(A_HEADS):
            pair, first = h // 2, h % 2 == 0
            piece = jnp.where(low if first else jnp.logical_not(low), x16[:, pair * LANES:(pair + 1) * LANES], zero64)
            blocks.append(jnp.concatenate([piece, zero64] if pair == 0 else [zero64, piece], axis=-1))
        return jnp.concatenate(blocks, axis=0)

    zrhs = jnp.zeros((CHUNK, 2 * A_DK), BF16)
    n_rec = len(rec_stages)
    n_slots = 8 * (bsz // DELTA_WAVE)
    done = [0]

    def stage_done():
        done[0] += 1
        run_rec(n_rec - 1 - (done[0] * n_rec) // n_slots)

    def decay_terms(b, beta, g_col, g_last, eg, dec_p, beta_p):
        gbv = gb_ref[b]
        rin = lax.broadcasted_iota(jnp.int32, gbv.shape, 0) % CHUNK
        gcs = gbv
        s = 1
        while s < CHUNK:
            gcs = gcs + jnp.where(rin >= s, pltpu.roll(gcs, s, axis=0), 0.0)
            s *= 2
        gcs_t = gcs.T
        for c in range(nch):
            r0 = c * CHUNK
            pair_lanes = slice((c // 2) * LANES, (c // 2 + 1) * LANES)
            g_rows = []
            for h in range(A_HEADS):
                u_ = (b, c, h)
                beta[u_] = jnp.broadcast_to(gbv[rows(c), h:h + 1], (CHUNK, A_DK))
                g_col[u_] = jnp.broadcast_to(gcs[rows(c), A_HEADS + h:A_HEADS + h + 1], (CHUNK, A_DK))
                g_last[u_] = gcs[r0 + CHUNK - 1:r0 + CHUNK, A_HEADS + h:A_HEADS + h + 1]
                eg[u_] = jnp.exp(g_col[u_])
                g_row = gcs_t[A_HEADS + h:A_HEADS + h + 1, pair_lanes]
                g_rows.append(g_row if c % 2 == h % 2 else pltpu.roll(g_row, CHUNK, axis=1))
            g_row_p = jnp.concatenate([jnp.where(low_row, g_rows[0], g_rows[1]),
                                       jnp.where(low_row, g_rows[2], g_rows[3])], axis=-1)
            g_col_p = pack([g_col[b, c, h] for h in range(A_HEADS)])
            dec_p[b, c] = jnp.exp(jnp.where(ti_p >= ii_p, g_col_p - g_row_p, -jnp.inf))
            beta_p[b, c] = pack([beta[b, c, h] for h in range(A_HEADS)])

    def prepare(bs):
        groups_b = [(b, c) for b in bs for c in range(nch)]
        beta, g_col, g_last, eg, dec_p, beta_p = {}, {}, {}, {}, {}, {}
        for b in bs:
            decay_terms(b, beta, g_col, g_last, eg, dec_p, beta_p)

        nmat = {}
        for (b, c) in groups_b:
            k16 = k_ref[b, rows(c), :].astype(BF16)
            q16 = q_ref[b, rows(c), :].astype(BF16)
            k_heads = jnp.concatenate(
                [jnp.concatenate([k16[:, lanes(h)] if hh == h else zero64 for hh in range(A_HEADS)], axis=-1)
                 for h in range(A_HEADS)], axis=0)
            kq = _dot_nt(jnp.concatenate([k16, q16], axis=0), k_heads)
            nmat[b, c] = -(beta_p[b, c] * kq[:CHUNK] * jnp.where(ti_p > ii_p, dec_p[b, c], 0.0))
            akd_s[wslot, b * nch + c, 0:CHUNK, :] = (kq[CHUNK:] * dec_p[b, c]).astype(BF16)
        stage_done()

        rsum = dict(nmat)
        pw16 = {g_: nmat[g_].astype(BF16) for g_ in groups_b}
        pw = {g_: _dot(pw16[g_], block_diag(pw16[g_])) for g_ in groups_b}
        stage_done()
        for step in range(1, 6):
            last = step == 5
            pw16 = {g_: pw[g_].astype(BF16) for g_ in groups_b}
            rp = {}
            for g_ in groups_b:
                r16 = rsum[g_].astype(BF16)
                rp[g_] = _dot(r16 if last else jnp.concatenate([r16, pw16[g_]], axis=0), block_diag(pw16[g_]))
            for g_ in groups_b:
                rsum[g_] = rsum[g_] + pw[g_] + rp[g_][:CHUNK]
                if not last:
                    pw[g_] = rp[g_][CHUNK:]
            stage_done()

        for (b, c) in groups_b:
            for h in range(A_HEADS):
                u_ = (b, c, h)
                i = uid[u_]
                kc = k_ref[b, rows(c), lanes(h)]
                rhs = jnp.concatenate([(beta[u_] * eg[u_]) * kc, beta[u_] * v_ref[b, rows(c), lanes(h)]],
                                      axis=-1)
                rhs16 = rhs.astype(BF16)
                rhs_rows = jnp.concatenate([rhs16 if hh == h else zrhs for hh in range(A_HEADS)], axis=0)
                sol = rhs + _dot(rsum[b, c].astype(BF16), rhs_rows)
                wq_s[wslot, i] = jnp.concatenate([sol[:, :A_DK], eg[u_] * q_ref[b, rows(c), lanes(h)]],
                                                 axis=0).astype(BF16)
                ut_s[wslot, i] = sol[:, A_DK:]
                gl_s[wslot, i] = jnp.broadcast_to(jnp.exp(g_last[u_]), (1, A_DV))
        for (b, c) in groups_b:
            kd = [jnp.exp(g_last[b, c, h] - g_col[b, c, h]) * k_ref[b, rows(c), lanes(h)]
                  for h in range(A_HEADS)]
            for p in range(A_HEADS // 2):
                akd_s[wslot, b * nch + c, CHUNK:, p * LANES:(p + 1) * LANES] = (
                    jnp.concatenate([kd[2 * p], kd[2 * p + 1]], axis=0).T.astype(BF16))
        stage_done()

    run_rec(n_rec - 1)
    for b0 in range(0, bsz, DELTA_WAVE):
        prepare(range(b0, b0 + DELTA_WAVE))
    run_rec(0)

    for b in range(bsz):
        for h in range(A_HEADS):
            s_scr[b * A_HEADS + h] = s_cur[b, h]

    @pl.when(t == pl.num_programs(0) - 1)
    def _():
        for b in range(bsz):
            for h in range(A_HEADS):
                st_ref[b, h] = s_cur[b, h]


def _delta(qkv, gb, za, na_row):
    bsz, t, _ = qkv.shape
    ct = DELTA_CT
    nt = t // ct
    n_units = bsz * (ct // CHUNK) * A_HEADS
    prep = lambda w, j=0: pl.BlockSpec((bsz, ct, w), lambda i: (0, jnp.minimum(i, nt - 1), j))
    rec = lambda w: pl.BlockSpec((bsz, ct, w), lambda i: (0, jnp.maximum(i - 1, 0), 0))
    return pl.pallas_call(
        _delta_kernel,
        grid=(nt + 1,),
        in_specs=[prep(A_WIDTH, 0), prep(A_WIDTH, 1), prep(A_WIDTH, 2), prep(LANES), rec(A_WIDTH),
                  pl.BlockSpec((1, A_DV), lambda i: (0, 0))],
        out_specs=[rec(A_WIDTH),
                   pl.BlockSpec((bsz, A_HEADS, A_DK, A_DV), lambda i: (0, 0, 0, 0))],
        out_shape=[jax.ShapeDtypeStruct((bsz, t, A_WIDTH), BF16),
                   jax.ShapeDtypeStruct((bsz, A_HEADS, A_DK, A_DV), F32)],
        scratch_shapes=[pltpu.VMEM((bsz * A_HEADS, A_DK, A_DV), F32),
                        pltpu.VMEM((2, n_units, 2 * CHUNK, A_DK), BF16),
                        pltpu.VMEM((2, n_units, CHUNK, A_DV), F32),
                        pltpu.VMEM((2, n_units // A_HEADS, CHUNK + A_DK, A_HEADS * CHUNK), BF16),
                        pltpu.VMEM((2, n_units, 1, A_DV), F32)],
        compiler_params=pltpu.CompilerParams(dimension_semantics=("arbitrary",),
                                             vmem_limit_bytes=VMEM_LIMIT),
        name="delta",
    )(qkv, qkv, qkv, gb, za, na_row)


def _swa_out_kernel(sink_ref, qb_ref, kc_ref, kp_ref, krc_ref, krp_ref, v0c_ref, v0p_ref, v1c_ref, v1p_ref,
                    zb_ref, oa_ref, x_ref, gate_ref, w_ref, g_ref, b_ref, y_ref):
    n = pl.program_id(1)
    gate = gate_ref[pl.ds(pl.program_id(0), 1), :]
    tq = qb_ref.shape[1]
    blk = WINDOW
    kx = (jnp.concatenate([kp_ref[0], kc_ref[0]], axis=0), jnp.concatenate([krp_ref[0], krc_ref[0]], axis=0))
    vd = (jnp.concatenate([v0p_ref[0], v0c_ref[0]], axis=0), jnp.concatenate([v1p_ref[0], v1c_ref[0]], axis=0))

    a = lax.broadcasted_iota(jnp.int32, (2 * blk, 2 * blk), 0) % blk
    j = lax.broadcasted_iota(jnp.int32, (2 * blk, 2 * blk), 1)
    rel = a + blk - j
    band = (rel >= 0) & (rel <= WINDOW)
    band_first = band & ((n > 0) | (j >= blk))
    top = lax.broadcasted_iota(jnp.int32, (2 * blk, 1), 0) < blk
    low = _lane((blk, LANES)) < B_HD
    zero = jnp.zeros((blk, LANES), BF16)

    qrows = lambda i: slice(i * blk, (i + 1) * blk)
    krows = lambda i: slice(i * blk, (i + 2) * blk)
    sink = {(kh, half): jnp.where(top, sink_ref[kh * B_GROUP + half] * LOG2E,
                                  sink_ref[kh * B_GROUP + half + 2] * LOG2E)
            for kh in range(B_KV_HEADS) for half in range(2)}
    def wave_units(i0):
        blocks = range(i0, i0 + SWA_WAVE)
        return blocks, [(i, kh, half) for i in blocks for kh in range(B_KV_HEADS) for half in range(2)]

    def score_matmuls(i0):
        blocks, units = wave_units(i0)
        mix_a = {i: _dot(oa_ref[0, qrows(i), :], w_ref[0:A_WIDTH, :]) for i in blocks}
        sc = {}
        for (i, kh, half) in units:
            qs = []
            for g in range(2):
                grp = kh * 2 + g
                xg = qb_ref[0, qrows(i), grp * LANES:(grp + 1) * LANES]
                qs.append(jnp.where(low if half == 0 else jnp.logical_not(low), xg, zero))
            qz = jnp.concatenate(qs, axis=0)
            sc[i, kh, half] = _dot_nt(qz, kx[0 if kh == half else 1][krows(i)])
        return mix_a, sc

    def finish(i0, mix_a, sc):
        blocks, units = wave_units(i0)
        p, den = {}, {}
        for u_ in units:
            i, kh, half = u_
            s_m = jnp.where(band_first if i == 0 else band, sc[u_], -jnp.inf)
            m = jnp.maximum(jnp.max(s_m, axis=-1, keepdims=True), sink[kh, half])
            e = jnp.exp2(s_m - m)
            den[u_] = jnp.sum(e, axis=-1, keepdims=True) + jnp.exp2(sink[kh, half] - m)
            p[u_] = e.astype(BF16)
        pv = {u_: _dot(p[u_], vd[u_[1]][krows(u_[0])]) for u_ in units}
        outs = {u_: pv[u_] / den[u_] for u_ in units}
        for i in blocks:
            ob = []
            for grp in range(B_WIDTH // LANES):
                kh, g = grp // 2, grp % 2
                og = jnp.where(low, outs[i, kh, 0][g * blk:(g + 1) * blk], outs[i, kh, 1][g * blk:(g + 1) * blk])
                ob.append((og * _silu(zb_ref[0, qrows(i), grp * LANES:(grp + 1) * LANES])).astype(BF16))
            mix = mix_a[i] + _dot(jnp.concatenate(ob, axis=-1), w_ref[A_WIDTH:MIX_WIDTH, :])
            r = DEEPNORM_ALPHA * x_ref[0, qrows(i), :] + (1.0 + gate) * mix
            y_ref[0, qrows(i), :] = _layer_norm(r, g_ref[...], b_ref[...])

    starts = list(range(0, tq // blk, SWA_WAVE))
    pending = score_matmuls(starts[0])
    for w, i0 in enumerate(starts):
        ready = pending
        if w + 1 < len(starts):
            pending = score_matmuls(starts[w + 1])
        finish(i0, *ready)


def _swa_out(sinks, qb, kvl, zb, oa, x, mod, mod_row0, w_out, ln_g, ln_b):
    bsz, t, _ = qb.shape
    tq = SWA_TQ
    per = tq // WINDOW
    cur = lambda w: pl.BlockSpec((1, tq, w), lambda b, i: (b, i, 0))
    kv_cur = lambda j: pl.BlockSpec((1, tq, LANES), lambda b, i: (b, i, j))
    kv_prev = lambda j: pl.BlockSpec((1, WINDOW, LANES), lambda b, i: (b, jnp.maximum(i * per - 1, 0), j))
    const2 = lambda s: pl.BlockSpec(s, lambda b, i: (0, 0))
    return pl.pallas_call(
        _swa_out_kernel,
        grid=(bsz, t // tq),
        in_specs=[pl.BlockSpec(memory_space=pltpu.SMEM), cur(B_WIDTH),
                  kv_cur(0), kv_prev(0), kv_cur(1), kv_prev(1), kv_cur(2), kv_prev(2), kv_cur(3), kv_prev(3),
                  cur(B_WIDTH), cur(A_WIDTH), cur(D_MODEL),
                  pl.BlockSpec((8, D_MODEL), lambda b, i: (mod_row0 // 8, 2)),
                  const2((MIX_WIDTH, D_MODEL)), const2((1, D_MODEL)), const2((1, D_MODEL))],
        out_specs=cur(D_MODEL),
        out_shape=jax.ShapeDtypeStruct((bsz, t, D_MODEL), F32),
        compiler_params=pltpu.CompilerParams(dimension_semantics=("arbitrary", "arbitrary"),
                                             vmem_limit_bytes=VMEM_LIMIT),
        name="swa_out",
    )(sinks, qb, kvl, kvl, kvl, kvl, kvl, kvl, kvl, kvl, zb, oa, x, mod, w_out, ln_g, ln_b)


def _out_kernel(oa_ref, ob_ref, x_ref, gate_ref, w_ref, g_ref, b_ref, y_ref):
    mix = _dot(oa_ref[...], w_ref[0:A_WIDTH, :]) + _dot(ob_ref[...], w_ref[A_WIDTH:MIX_WIDTH, :])
    r = DEEPNORM_ALPHA * x_ref[...] + (1.0 + gate_ref[...]) * mix
    y_ref[...] = _layer_norm(r, g_ref[...], b_ref[...])


def _out(oa, ob, x, mod, w_out, ln_g, ln_b):
    n = x.shape[0]
    full = lambda s: pl.BlockSpec(s, lambda i: (0, 0))
    return pl.pallas_call(
        _out_kernel,
        grid=(1,),
        in_specs=[full((n, A_WIDTH)), full((n, B_WIDTH)), full((n, D_MODEL)),
                  pl.BlockSpec((n, D_MODEL), lambda i: (0, 2)),
                  full((MIX_WIDTH, D_MODEL)), full((1, D_MODEL)), full((1, D_MODEL))],
        out_specs=full((n, D_MODEL)),
        out_shape=jax.ShapeDtypeStruct((n, D_MODEL), F32),
        compiler_params=pltpu.CompilerParams(dimension_semantics=("arbitrary",),
                                             vmem_limit_bytes=VMEM_LIMIT),
        name="out",
    )(oa, ob, x, mod, w_out, ln_g, ln_b)


def _sproj_kernel(x_ref, mod_ref, w_ref, cw_ref, cst_ref, alog_ref, dt_ref, cos_ref, sin_ref,
                  q_ref, k_ref, v_ref, za_ref, gb_ref, qb_ref, kb_ref, vb_ref, zb_ref, ncs_ref):
    shift = mod_ref[:, 0:D_MODEL]
    scale = mod_ref[:, D_MODEL:2 * D_MODEL]
    h = (x_ref[...] * (1.0 + scale) + shift).astype(BF16)

    for gi, o_ref in enumerate((q_ref, k_ref, v_ref)):
        c0 = gi * A_WIDTH
        cs = slice(c0, c0 + A_WIDTH)
        u = _dot(h, w_ref[:, cs])
        acc = cst_ref[0, :, cs] * cw_ref[0:1, cs]
        acc = acc + cst_ref[1, :, cs] * cw_ref[1:2, cs]
        acc = acc + cst_ref[2, :, cs] * cw_ref[2:3, cs]
        acc = acc + u * cw_ref[3:4, cs]
        y = _silu(acc)
        if gi == 0:
            y = _l2norm_heads(y, A_DK ** -0.5)
        elif gi == 1:
            y = _l2norm_heads(y, 1.0)
        o_ref[...] = y
        ncs_ref[0, :, cs] = cst_ref[1, :, cs]
        ncs_ref[1, :, cs] = cst_ref[2, :, cs]
        ncs_ref[2, :, cs] = u

    za_ref[...] = _dot(h, w_ref[:, C_ZA:C_ZA + A_WIDTH])
    gb_ref[...] = _gate_lanes(_dot(h, w_ref[:, C_BD:C_BD + LANES]), alog_ref[...], dt_ref[...])

    cos = cos_ref[...]
    sin = sin_ref[...]
    uq = _dot(h, w_ref[:, C_QB:C_QB + B_WIDTH])
    for g in range(B_WIDTH // LANES):
        qb_ref[:, g * LANES:(g + 1) * LANES] = (
            _rotary_group(uq[:, g * LANES:(g + 1) * LANES], cos, sin) * (B_HD ** -0.5))
    kb_ref[...] = _rotary_group(_dot(h, w_ref[:, C_KB:C_KB + LANES]), cos, sin)
    vb_ref[...] = _dot(h, w_ref[:, C_VB:C_VB + LANES])
    zb_ref[...] = _dot(h, w_ref[:, C_ZB:C_ZB + B_WIDTH])


def _sproj(x, mod_s, w_r, conv_w, cst, alog_row, dt_row, cos_row, sin_row):
    n = x.shape[0]
    full = lambda s: pl.BlockSpec(s, lambda i: (0,) * len(s))
    wide = lambda w: jax.ShapeDtypeStruct((n, w), F32)
    return pl.pallas_call(
        _sproj_kernel,
        grid=(1,),
        in_specs=[full((n, D_MODEL)), pl.BlockSpec((n, 3 * D_MODEL), lambda i: (0, 0)),
                  full((D_MODEL, W_COLS)),
                  full((CONV_W, A_QKV)), full((CONV_W - 1, n, A_QKV)),
                  full((1, LANES)), full((1, LANES)), full((1, LANES)), full((1, LANES))],
        out_specs=[full((n, A_WIDTH)), full((n, A_WIDTH)), full((n, A_WIDTH)), full((n, A_WIDTH)),
                   full((n, LANES)), full((n, B_WIDTH)), full((n, LANES)), full((n, LANES)),
                   full((n, B_WIDTH)), full((CONV_W - 1, n, A_QKV))],
        out_shape=[wide(A_WIDTH), wide(A_WIDTH), wide(A_WIDTH), wide(A_WIDTH), wide(LANES),
                   wide(B_WIDTH), wide(LANES), wide(LANES), wide(B_WIDTH),
                   jax.ShapeDtypeStruct((CONV_W - 1, n, A_QKV), F32)],
        compiler_params=pltpu.CompilerParams(dimension_semantics=("arbitrary",),
                                             vmem_limit_bytes=VMEM_LIMIT),
        name="sproj",
    )(x, mod_s, w_r, conv_w, cst, alog_row, dt_row, cos_row, sin_row)


def _sstep_kernel(sink_ref, q_ref, k_ref, v_ref, gb_ref, za_ref, na_ref, st_ref,
                  qb_ref, kn_ref, vn_ref, zb_ref, ck_ref, cv_ref,
                  oa_ref, ob_ref, nst_ref, nck_ref, ncv_ref,
                  o_scr, ob_scr):
    bt = q_ref.shape[0]
    gbv = gb_ref[...]

    pick = (lax.broadcasted_iota(jnp.int32, (bt, bt * A_DV), 1) // A_DV
            == lax.broadcasted_iota(jnp.int32, (bt, bt * A_DV), 0))
    pick = jnp.where(pick, 1.0, 0.0).astype(BF16)
    for h in range(A_HEADS):
        hs = slice(h * A_DK, (h + 1) * A_DK)
        q_rep = _dot(q_ref[:, hs].T.astype(BF16), pick)
        k_rep = _dot(k_ref[:, hs].T.astype(BF16), pick)
        for bb in range(bt):
            eg = jnp.exp(gbv[bb:bb + 1, A_HEADS + h:A_HEADS + h + 1])
            beta = gbv[bb:bb + 1, h:h + 1]
            kcol = k_rep[:, bb * A_DV:(bb + 1) * A_DV]
            qcol = q_rep[:, bb * A_DV:(bb + 1) * A_DV]
            s1 = eg * st_ref[bb, h]
            pred = jnp.sum(kcol * s1, axis=0, keepdims=True)
            upd = beta * (v_ref[bb:bb + 1, hs] - pred)
            s2 = s1 + kcol * upd
            nst_ref[bb, h] = s2
            o_scr[bb:bb + 1, hs] = jnp.sum(qcol * s2, axis=0, keepdims=True)
    na = na_ref[...]
    for h in range(A_HEADS):
        hs = slice(h * A_DK, (h + 1) * A_DK)
        o = o_scr[:, hs]
        on = o * lax.rsqrt(jnp.mean(o * o, axis=-1, keepdims=True) + RMS_EPS) * na
        oa_ref[:, hs] = (on * _silu(za_ref[:, hs])).astype(BF16)

    row8 = lax.broadcasted_iota(jnp.int32, (B_HEADS, LANES), 0)
    lane8 = _lane((B_HEADS, LANES))
    own_half = (lane8 >= B_HD) == (row8 >= B_GROUP)
    rcol = lax.broadcasted_iota(jnp.int32, (B_HEADS, 1), 0)
    sink = jnp.zeros((B_HEADS, 1), F32)
    for r in range(B_HEADS):
        sink = jnp.where(rcol == r, sink_ref[r], sink)
    qv = qb_ref[...]
    qv_r = jnp.concatenate([pltpu.roll(qv[:, g * LANES:(g + 1) * LANES], B_HD, axis=1)
                            for g in range(B_WIDTH // LANES)], axis=-1)
    kn_t = kn_ref[...].T
    vn_t = vn_ref[...].T
    newest = _lane((LANES, WINDOW)) == WINDOW - 1
    qzs, scs = [], []
    for bb in range(bt):
        qz = jnp.zeros((B_HEADS, LANES), F32)
        for r in range(B_HEADS):
            grp, half, kh = r // 2, r % 2, r // B_GROUP
            src = qv if half == kh else qv_r
            qz = jnp.where(row8 == r, src[bb:bb + 1, grp * LANES:(grp + 1) * LANES], qz)
        qzs.append(jnp.where(own_half, qz, 0.0))
    for bb in range(bt):
        scs.append(_dot(qzs[bb], ck_ref[bb]))
    ps, pnews, dens = [], [], []
    for bb in range(bt):
        sc_new = jnp.sum(qzs[bb] * kn_ref[bb:bb + 1, :], axis=-1, keepdims=True)
        m = jnp.maximum(jnp.maximum(jnp.max(scs[bb], axis=-1, keepdims=True), sc_new), sink)
        p = jnp.exp(scs[bb] - m)
        p_new = jnp.exp(sc_new - m)
        ps.append(p)
        pnews.append(p_new)
        dens.append(jnp.sum(p, axis=-1, keepdims=True) + p_new + jnp.exp(sink - m))
    pvs = [_dot_nt(ps[bb], cv_ref[bb]) for bb in range(bt)]
    for bb in range(bt):
        o = (pvs[bb] + pnews[bb] * vn_ref[bb:bb + 1, :]) / dens[bb]
        o = jnp.where(own_half, o, 0.0)
        ob_scr[bb * B_HEADS:(bb + 1) * B_HEADS, :] = o + pltpu.roll(o, B_HD, axis=1)
    for bb in range(bt):
        nck_ref[bb] = jnp.where(newest, kn_t[:, bb:bb + 1], pltpu.roll(ck_ref[bb], WINDOW - 1, axis=1))
        ncv_ref[bb] = jnp.where(newest, vn_t[:, bb:bb + 1], pltpu.roll(cv_ref[bb], WINDOW - 1, axis=1))
    low = _lane((bt, LANES)) < B_HD
    for grp in range(B_WIDTH // LANES):
        even = ob_scr[pl.ds(2 * grp, bt, stride=B_HEADS), :]
        odd = ob_scr[pl.ds(2 * grp + 1, bt, stride=B_HEADS), :]
        gs = slice(grp * LANES, (grp + 1) * LANES)
        ob_ref[:, gs] = (jnp.where(low, even, odd) * _silu(zb_ref[:, gs])).astype(BF16)


def _sstep(sinks, q, k, v, gb, za, na_row, state, qb, kn, vn, zb, ck, cv):
    n = q.shape[0]
    bt = STEP_BT
    row = lambda w: pl.BlockSpec((bt, w), lambda i: (i, 0))
    st_spec = pl.BlockSpec((bt, A_HEADS, A_DK, A_DV), lambda i: (i, 0, 0, 0))
    c_spec = pl.BlockSpec((bt, WINDOW, LANES), lambda i: (i, 0, 0))
    return pl.pallas_call(
        _sstep_kernel,
        grid=(n // bt,),
        in_specs=[pl.BlockSpec(memory_space=pltpu.SMEM),
                  row(A_WIDTH), row(A_WIDTH), row(A_WIDTH), row(LANES), row(A_WIDTH),
                  pl.BlockSpec((1, A_DV), lambda i: (0, 0)), st_spec,
                  row(B_WIDTH), row(LANES), row(LANES), row(B_WIDTH), c_spec, c_spec],
        out_specs=[row(A_WIDTH), row(B_WIDTH), st_spec, c_spec, c_spec],
        out_shape=[jax.ShapeDtypeStruct((n, A_WIDTH), BF16),
                   jax.ShapeDtypeStruct((n, B_WIDTH), BF16),
                   jax.ShapeDtypeStruct((n, A_HEADS, A_DK, A_DV), F32),
                   jax.ShapeDtypeStruct((n, WINDOW, LANES), F32),
                   jax.ShapeDtypeStruct((n, WINDOW, LANES), F32)],
        scratch_shapes=[pltpu.VMEM((bt, A_WIDTH), F32), pltpu.VMEM((bt * B_HEADS, LANES), F32)],
        compiler_params=pltpu.CompilerParams(dimension_semantics=("arbitrary",),
                                             vmem_limit_bytes=VMEM_LIMIT),
        name="sstep",
    )(sinks, q, k, v, gb, za, na_row, state, qb, kn, vn, zb, ck, cv)


def _rope_tables(pos):
    half = B_HD // 2
    inv = 1.0 / (ROPE_THETA ** (np.arange(half, dtype=np.float64) / half))
    ang = np.asarray(pos, np.float64)[:, None] * inv[None, :]
    cos, sin = np.cos(ang), np.sin(ang)
    reps = LANES // B_HD
    return (jnp.asarray(np.tile(np.concatenate([cos, cos], -1), (1, reps)), F32),
            jnp.asarray(np.tile(np.concatenate([-sin, sin], -1), (1, reps)), F32))


def _pad_row(vec, offset):
    return jnp.pad(vec.astype(F32).reshape(1, -1), ((0, 0), (offset, LANES - offset - vec.shape[0])))


def _layer(x_prompt, x_sample, state_conv, state_delta, cache_k, cache_v, c_prompt, c_sample,
           w_ada, b_ada, w_in, conv_w, a_log, dt_bias, norm_a, sinks, w_out, ln_g, ln_b):
    bsz, seq, _ = x_prompt.shape
    n_s = x_sample.shape[0]

    w_r = _wprep(jnp.swapaxes(w_in, 0, 1))
    w_o = w_out.astype(BF16)
    alog_row = _pad_row(a_log, A_HEADS)
    dt_row = _pad_row(dt_bias, A_HEADS)
    na_row = norm_a.reshape(1, A_DV)
    g_row = ln_g.reshape(1, D_MODEL)
    b_row = ln_b.reshape(1, D_MODEL)

    assert n_s % 8 == 0 and bsz <= 8
    mod = _ada(c_sample, c_prompt, w_ada, b_ada.reshape(1, 3 * D_MODEL))

    cos_p, sin_p = _rope_tables(np.arange(seq))
    (qkv, za, gb, qb, kvl, zb, conv_p, kb_last, vb_last) = _proj(
        x_prompt, mod, n_s, w_r, conv_w, alog_row, dt_row, cos_p, sin_p)
    oa, delta_p = _delta(qkv, gb, za, na_row)
    y_p = _swa_out(sinks, qb, kvl, zb, oa, x_prompt, mod, n_s, w_o, g_row, b_row)
    swa_k_p = kb_last.reshape(bsz, WINDOW, B_KV_HEADS, B_HD)
    swa_v_p = vb_last.reshape(bsz, WINDOW, B_KV_HEADS, B_HD)

    cos_s, sin_s = _rope_tables(np.array([PAST_LEN]))
    xs = x_sample.reshape(n_s, D_MODEL)
    cst = jnp.transpose(state_conv, (1, 0, 2))
    sq, sk, sv, sza, sgb, sqb, skn, svn, szb, ncs = _sproj(xs, mod, w_r, conv_w, cst, alog_row, dt_row,
                                                           cos_s, sin_s)
    soa, sob, delta_s, nck, ncv = _sstep(sinks, sq, sk, sv, sgb, sza, na_row, state_delta,
                                         sqb, skn, svn, szb,
                                         jnp.swapaxes(cache_k.reshape(n_s, WINDOW, LANES), 1, 2),
                                         jnp.swapaxes(cache_v.reshape(n_s, WINDOW, LANES), 1, 2))
    y_s = _out(soa, sob, xs, mod, w_o, g_row, b_row)
    conv_s = jnp.transpose(ncs, (1, 0, 2))
    unpack = lambda c: jnp.swapaxes(c, 1, 2).reshape(n_s, WINDOW, B_KV_HEADS, B_HD)
    return (y_p, y_s.reshape(n_s, 1, D_MODEL), conv_p, delta_p, swa_k_p, swa_v_p,
            conv_s, delta_s, unpack(nck), unpack(ncv))


def kernel(x_prompt, x_sample, state_conv, state_delta, cache_swa_k, cache_swa_v, c_prompt, c_sample,
           w_ada, b_ada, w_in, conv_w, a_log, dt_bias, norm_a, sinks, w_out, ln_g, ln_b):
    assert w_ada.shape[0] == DEPTH == 1
    outs = _layer(x_prompt, x_sample, state_conv[0], state_delta[0], cache_swa_k[0], cache_swa_v[0],
                  c_prompt, c_sample, w_ada[0], b_ada[0], w_in[0], conv_w[0], a_log[0], dt_bias[0],
                  norm_a[0], sinks[0], w_out[0], ln_g[0], ln_b[0])
    y_p, y_s = outs[0], outs[1]
    return (y_p, y_s) + tuple(o[None] for o in outs[2:])
```

```python
import jax
import jax.numpy as jnp
import numpy as np
from jax import lax
from jax.experimental import pallas as pl
from jax.experimental.pallas import tpu as pltpu

F32 = jnp.float32
BF16 = jnp.bfloat16

D_MODEL = 1024
DEPTH = 1
PAST_LEN = 8192
A_HEADS = 4
A_DK = 128
A_DV = 128
A_WIDTH = A_HEADS * A_DV
A_QKV = 3 * A_WIDTH
CONV_W = 4
CHUNK = 64
B_HEADS = 8
B_KV_HEADS = 2
B_HD = 64
B_GROUP = B_HEADS // B_KV_HEADS
B_WIDTH = B_HEADS * B_HD
B_KV_WIDTH = B_KV_HEADS * B_HD
WINDOW = 128
ROPE_THETA = 10000.0
MIX_WIDTH = A_WIDTH + B_WIDTH
DEEPNORM_ALPHA = (2 * DEPTH) ** 0.25
LOG2E = 1.4426950408889634
LN_EPS = 1e-5
RMS_EPS = 1e-6
L2_EPS = 1e-6

OFF_A_Z = A_QKV
OFF_A_BETA = OFF_A_Z + A_WIDTH
OFF_A_DECAY = OFF_A_BETA + A_HEADS
OFF_B_Q = OFF_A_DECAY + A_HEADS
OFF_B_K = OFF_B_Q + B_WIDTH
OFF_B_V = OFF_B_K + B_KV_WIDTH
OFF_B_Z = OFF_B_V + B_KV_WIDTH
PROJ_COLS = OFF_B_Z + B_WIDTH

LANES = 128
C_QKV = 0
C_ZA = C_QKV + A_QKV
C_QB = C_ZA + A_WIDTH
C_KB = C_QB + B_WIDTH
C_VB = C_KB + B_KV_WIDTH
C_ZB = C_VB + B_KV_WIDTH
C_BD = C_ZB + B_WIDTH
WPREP_TN = 256
W_COLS = C_BD + WPREP_TN

VMEM_LIMIT = 56 * 1024 * 1024

ADA_TN = 1536
PROJ_TM = 512
PROJ_CW = 256
PROJ_PARTS = 2
DELTA_CT = 256
DELTA_WAVE = 2
SWA_TQ = 1024
SWA_WAVE = 2
STEP_BT = 16


def _dot(a, b):
    return jnp.dot(a, b, preferred_element_type=F32)


def _dot_nt(a, b):
    return lax.dot_general(a, b, (((1,), (1,)), ((), ())), preferred_element_type=F32)


def _silu(x):
    return x * jax.nn.sigmoid(x)


def _softplus(x):
    return jnp.maximum(x, 0.0) + jnp.log1p(jnp.exp(-jnp.abs(x)))


def _lane(shape):
    return lax.broadcasted_iota(jnp.int32, shape, len(shape) - 1)


def _l2norm_heads(y, scale):
    outs = []
    for h in range(y.shape[1] // A_DK):
        xh = y[:, h * A_DK:(h + 1) * A_DK]
        ss = jnp.sum(xh * xh, axis=-1, keepdims=True)
        xn = xh * lax.rsqrt(ss + L2_EPS)
        outs.append(xn * scale if scale != 1.0 else xn)
    return jnp.concatenate(outs, axis=-1)


def _rotary_group(xg, cos, sin_signed):
    lane = _lane(xg.shape)
    swapped = jnp.where((lane % B_HD) < (B_HD // 2),
                        pltpu.roll(xg, LANES - B_HD // 2, axis=1),
                        pltpu.roll(xg, B_HD // 2, axis=1))
    return xg * cos + swapped * sin_signed


def _kv_layouts(kb, vb):
    low = _lane(kb.shape) < B_HD
    kbr = pltpu.roll(kb, B_HD, axis=1)
    vbr = pltpu.roll(vb, B_HD, axis=1)
    return kb, kbr, jnp.where(low, vb, vbr), jnp.where(low, vbr, vb)


def _gate_lanes(bd, alog_row, dt_row):
    lane = _lane(bd.shape)
    g = -jnp.exp(alog_row) * _softplus(bd + dt_row)
    return jnp.where(lane < A_HEADS, jax.nn.sigmoid(bd), g)


def _layer_norm(r, g, b):
    mu = jnp.mean(r, axis=-1, keepdims=True)
    d = r - mu
    var = jnp.mean(d * d, axis=-1, keepdims=True)
    return d * lax.rsqrt(var + LN_EPS) * g + b


def _wprep_kernel(wa_ref, wb_ref, o_ref):
    tn = wa_ref.shape[0]
    o_ref[:, 0:tn] = wa_ref[...].T.astype(BF16)
    xb = wb_ref[...]
    tail = pl.program_id(0) == pl.num_programs(0) - 1
    row = lax.broadcasted_iota(jnp.int32, xb.shape, 0)
    xb = jnp.where(jnp.logical_and(tail, row >= 2 * A_HEADS), 0.0, xb)
    o_ref[:, tn:2 * tn] = xb.T.astype(BF16)


def _wprep(w_t):
    tn = WPREP_TN
    n_a, n_b = OFF_A_BETA // tn, (PROJ_COLS - OFF_B_Q) // tn
    assert n_a * tn == OFF_A_BETA and n_b * tn == PROJ_COLS - OFF_B_Q and OFF_A_BETA + tn <= PROJ_COLS

    assert (n_a + n_b + 1) % 2 == 0

    def src_row(j):
        return jnp.where(j < n_a, j * tn, jnp.where(j < n_a + n_b, OFF_B_Q + (j - n_a) * tn, OFF_A_BETA))

    src = lambda k: pl.BlockSpec((pl.Element(tn), pl.Element(D_MODEL)),
                                 lambda j: (pl.multiple_of(src_row(2 * j + k), 8), 0))
    return pl.pallas_call(
        _wprep_kernel,
        grid=((n_a + n_b + 1) // 2,),
        in_specs=[src(0), src(1)],
        out_specs=pl.BlockSpec((D_MODEL, 2 * tn), lambda j: (0, j)),
        out_shape=jax.ShapeDtypeStruct((D_MODEL, W_COLS), BF16),
        compiler_params=pltpu.CompilerParams(dimension_semantics=("arbitrary",),
                                             vmem_limit_bytes=VMEM_LIMIT),
        name="wprep",
    )(w_t, w_t)


def _ada_kernel(cs_ref, cp_ref, w_ref, b_ref, o_ref):
    n_s, n_p = cs_ref.shape[0], cp_ref.shape[0]
    w = w_ref[...].astype(BF16)
    o_ref[0:n_s, :] = _dot(cs_ref[...].astype(BF16), w) + b_ref[...]
    o_ref[n_s:n_s + 8, :] = jnp.zeros((8, o_ref.shape[1]), F32)
    o_ref[n_s:n_s + n_p, :] = _dot(cp_ref[...].astype(BF16), w) + b_ref[...]


def _ada(c_sample, c_prompt, w_ada, b_ada):
    n_s, n_p = c_sample.shape[0], c_prompt.shape[0]
    rows = n_s + 8
    tn = ADA_TN
    return pl.pallas_call(
        _ada_kernel,
        grid=(3 * D_MODEL // tn,),
        in_specs=[pl.BlockSpec((n_s, D_MODEL), lambda j: (0, 0)),
                  pl.BlockSpec((n_p, D_MODEL), lambda j: (0, 0)),
                  pl.BlockSpec((D_MODEL, tn), lambda j: (0, j)),
                  pl.BlockSpec((1, tn), lambda j: (0, j))],
        out_specs=pl.BlockSpec((rows, tn), lambda j: (0, j)),
        out_shape=jax.ShapeDtypeStruct((rows, 3 * D_MODEL), F32),
        compiler_params=pltpu.CompilerParams(dimension_semantics=("arbitrary",),
                                             vmem_limit_bytes=VMEM_LIMIT),
        name="ada",
    )(c_sample, c_prompt, w_ada, b_ada)


def _proj_kernel(x_ref, mod_ref, w_ref, cw_ref, alog_ref, dt_ref, cos_ref, sin_ref,
                 qkv_ref, za_ref, gb_ref, qb_ref, kvl_ref, zb_ref, cst_ref, kbl_ref, vbl_ref, ubuf):
    tm = x_ref.shape[1]
    t = pl.program_id(1)

    @pl.when(t == 0)
    def _():
        ubuf[...] = jnp.zeros(ubuf.shape, F32)

    brow = pl.ds(pl.program_id(0), 1)
    shift = mod_ref[brow, 0:D_MODEL]
    scale = mod_ref[brow, D_MODEL:2 * D_MODEL]

    rp = tm // PROJ_PARTS
    cw = PROJ_CW
    sub = lax.broadcasted_iota(jnp.int32, (rp // 8, 8, cw), 1)
    pieces = [slice(c0, c0 + cw) for c0 in range(0, A_QKV, cw)]

    def matmuls(r0):
        h = (x_ref[0, r0:r0 + rp, :] * (1.0 + scale) + shift).astype(BF16)
        return ([_dot(h, w_ref[:, cs]) for cs in pieces],
                _dot(h, w_ref[:, C_ZA:C_ZA + A_WIDTH]), _dot(h, w_ref[:, C_BD:C_BD + LANES]),
                _dot(h, w_ref[:, C_QB:C_QB + B_WIDTH]), _dot(h, w_ref[:, C_KB:C_KB + 2 * LANES]),
                _dot(h, w_ref[:, C_ZB:C_ZB + B_WIDTH]))

    def epilogue(r0, results):
        rs = slice(r0, r0 + rp)
        us, za, ubd, uq, ukv, zb = results

        def conv_epilogue(cs, u):
            gi = cs.start // A_WIDTH
            groups = jnp.concatenate([ubuf[:, cs], u], axis=0).reshape(rp // 8 + 1, 8, cw)
            acc = None
            for j in range(CONV_W - 1, 0, -1):
                rot = pltpu.roll(groups, j, axis=1)
                term = (jnp.where(sub < j, rot[:-1], rot[1:]).reshape(rp, cw)
                        * cw_ref[CONV_W - 1 - j:CONV_W - j, cs])
                acc = term if acc is None else acc + term
            y = _silu(acc + u * cw_ref[CONV_W - 1:CONV_W, cs])
            if gi == 0:
                y = _l2norm_heads(y, A_DK ** -0.5)
            elif gi == 1:
                y = _l2norm_heads(y, 1.0)
            qkv_ref[0, rs, cs] = y
            ubuf[:, cs] = u[rp - 8:rp]
            if r0 + rp == tm:
                cst_ref[0, :, cs] = u[rp - (CONV_W - 1):rp]

        for cs, u in zip(pieces, us):
            conv_epilogue(cs, u)
        za_ref[0, rs, :] = za
        zb_ref[0, rs, :] = zb
        gb_ref[0, rs, :] = _gate_lanes(ubd, alog_ref[...], dt_ref[...])
        cos = cos_ref[rs, :]
        sin = sin_ref[rs, :]
        for g in range(B_WIDTH // LANES):
            qb_ref[0, rs, g * LANES:(g + 1) * LANES] = (
                _rotary_group(uq[:, g * LANES:(g + 1) * LANES], cos, sin) * (B_HD ** -0.5 * LOG2E)).astype(BF16)
        kb = _rotary_group(ukv[:, 0:LANES], cos, sin)
        vb = ukv[:, LANES:2 * LANES]
        for j, val in enumerate(_kv_layouts(kb, vb)):
            kvl_ref[0, rs, j * LANES:(j + 1) * LANES] = val.astype(BF16)
        return kb, vb

    starts = list(range(0, tm, rp))
    pending = matmuls(starts[0])
    for p, r0 in enumerate(starts):
        results = pending
        if p + 1 < len(starts):
            pending = matmuls(starts[p + 1])
        kb, vb = epilogue(r0, results)

    @pl.when(t == pl.num_programs(1) - 1)
    def _():
        kbl_ref[0] = kb[rp - WINDOW:rp]
        vbl_ref[0] = vb[rp - WINDOW:rp]


def _proj(x, mod, mod_row0, w_r, conv_w, alog_row, dt_row, cos_t, sin_t):
    bsz, t, _ = x.shape
    tm = PROJ_TM
    row = lambda w: pl.BlockSpec((1, tm, w), lambda b, i: (b, i, 0))
    const2 = lambda s: pl.BlockSpec(s, lambda b, i: (0, 0))
    per_b = lambda r, w: pl.BlockSpec((1, r, w), lambda b, i: (b, 0, 0))
    wide = lambda w, dt=F32: jax.ShapeDtypeStruct((bsz, t, w), dt)
    return pl.pallas_call(
        _proj_kernel,
        grid=(bsz, t // tm),
        in_specs=[row(D_MODEL),
                  pl.BlockSpec((8, 3 * D_MODEL), lambda b, i: (mod_row0 // 8, 0)),
                  const2((D_MODEL, W_COLS)),
                  const2((CONV_W, A_QKV)),
                  const2((1, LANES)), const2((1, LANES)),
                  pl.BlockSpec((tm, LANES), lambda b, i: (i, 0)),
                  pl.BlockSpec((tm, LANES), lambda b, i: (i, 0))],
        out_specs=[row(A_QKV), row(A_WIDTH), row(LANES), row(B_WIDTH), row(4 * LANES), row(B_WIDTH),
                   per_b(CONV_W - 1, A_QKV), per_b(WINDOW, LANES), per_b(WINDOW, LANES)],
        out_shape=[wide(A_QKV), wide(A_WIDTH), wide(LANES), wide(B_WIDTH, BF16), wide(4 * LANES, BF16),
                   wide(B_WIDTH),
                   jax.ShapeDtypeStruct((bsz, CONV_W - 1, A_QKV), F32),
                   jax.ShapeDtypeStruct((bsz, WINDOW, LANES), F32),
                   jax.ShapeDtypeStruct((bsz, WINDOW, LANES), F32)],
        scratch_shapes=[pltpu.VMEM((8, A_QKV), F32)],
        compiler_params=pltpu.CompilerParams(dimension_semantics=("arbitrary", "arbitrary"),
                                             vmem_limit_bytes=VMEM_LIMIT),
        name="proj",
    )(x, mod, w_r, conv_w, alog_row, dt_row, cos_t, sin_t)


def _delta_kernel(q_ref, k_ref, v_ref, gb_ref, za_ref, na_ref, oa_ref, st_ref,
                  s_scr, wq_s, ut_s, akd_s, gl_s):
    bsz, ct = q_ref.shape[0], q_ref.shape[1]
    nch = ct // CHUNK
    t = pl.program_id(0)
    wslot = t % 2
    rslot = 1 - wslot

    @pl.when(t == 0)
    def _():
        s_scr[...] = jnp.zeros(s_scr.shape, F32)
        wq_s[...] = jnp.zeros(wq_s.shape, BF16)
        ut_s[...] = jnp.zeros(ut_s.shape, F32)
        akd_s[...] = jnp.zeros(akd_s.shape, BF16)
        gl_s[...] = jnp.zeros(gl_s.shape, F32)

    units = [(b, c, h) for b in range(bsz) for c in range(nch) for h in range(A_HEADS)]
    uid = {u_: i for i, u_ in enumerate(units)}
    rows = lambda c: slice(c * CHUNK, (c + 1) * CHUNK)
    lanes = lambda h: slice(h * A_DK, (h + 1) * A_DK)
    na = na_ref[...]

    s_cur = {(b, h): s_scr[b * A_HEADS + h] for b in range(bsz) for h in range(A_HEADS)}
    ws, uu = {}, {}

    def rec_ws(c):
        for b in range(bsz):
            for h in range(A_HEADS):
                i = uid[b, c, h]
                ws[b, h] = _dot(wq_s[rslot, i], s_cur[b, h].astype(BF16))
                uu[b, h] = (ut_s[rslot, i] - ws[b, h][:CHUNK]).astype(BF16)

    def rec_ou(c):
        zpad = jnp.zeros((CHUNK, A_DV), BF16)
        for b in range(bsz):
            u_bd = jnp.concatenate(
                [jnp.concatenate([uu[b, h] if hh == h else zpad for hh in range(A_HEADS)], axis=-1)
                 for h in range(A_HEADS)], axis=0)
            ou = _dot(akd_s[rslot, b * nch + c], u_bd)
            for h in range(A_HEADS):
                o = ws[b, h][CHUNK:] + ou[:CHUNK, lanes(h)]
                s_cur[b, h] = gl_s[rslot, uid[b, c, h]] * s_cur[b, h] + ou[CHUNK:, lanes(h)]
                on = o * lax.rsqrt(jnp.mean(o * o, axis=-1, keepdims=True) + RMS_EPS) * na
                oa_ref[b, rows(c), lanes(h)] = (on * _silu(za_ref[b, rows(c), lanes(h)])).astype(BF16)

    rec_stages = []
    for c in range(nch):
        rec_stages += [lambda c=c: rec_ws(c), lambda c=c: rec_ou(c)]

    def run_rec(n_left_after):
        while rec_stages and len(rec_stages) > n_left_after:
            rec_stages.pop(0)()

    pk = A_HEADS * CHUNK
    low = _lane((CHUNK, LANES)) < CHUNK
    low_row = _lane((1, LANES)) < CHUNK
    ti_p = lax.broadcasted_iota(jnp.int32, (CHUNK, pk), 0)
    ii_p = _lane((CHUNK, pk)) % CHUNK
    zero64 = jnp.zeros((CHUNK, LANES), BF16)

    def pack(parts):
        return jnp.concatenate([jnp.where(low, parts[0], parts[1]), jnp.where(low, parts[2], parts[3])], axis=-1)

    def block_diag(x16):
        blocks = []
        for h in range(A_HEADS):
            pair, first = h // 2, h % 2 == 0
            piece = jnp.where(low if first else jnp.logical_not(low), x16[:, pair * LANES:(pair + 1) * LANES], zero64)
            blocks.append(jnp.concatenate([piece, zero64] if pair == 0 else [zero64, piece], axis=-1))
        return jnp.concatenate(blocks, axis=0)

    zrhs = jnp.zeros((CHUNK, 2 * A_DK), BF16)
    n_rec = len(rec_stages)
    n_slots = 8 * (bsz // DELTA_WAVE)
    done = [0]

    def stage_done():
        done[0] += 1
        run_rec(n_rec - 1 - (done[0] * n_rec) // n_slots)

    def decay_terms(b, beta, g_col, g_last, eg, dec_p, beta_p):
        gbv = gb_ref[b]
        rin = lax.broadcasted_iota(jnp.int32, gbv.shape, 0) % CHUNK
        gcs = gbv
        s = 1
        while s < CHUNK:
            gcs = gcs + jnp.where(rin >= s, pltpu.roll(gcs, s, axis=0), 0.0)
            s *= 2
        gcs_t = gcs.T
        for c in range(nch):
            r0 = c * CHUNK
            pair_lanes = slice((c // 2) * LANES, (c // 2 + 1) * LANES)
            g_rows = []
            for h in range(A_HEADS):
                u_ = (b, c, h)
                beta[u_] = jnp.broadcast_to(gbv[rows(c), h:h + 1], (CHUNK, A_DK))
                g_col[u_] = jnp.broadcast_to(gcs[rows(c), A_HEADS + h:A_HEADS + h + 1], (CHUNK, A_DK))
                g_last[u_] = gcs[r0 + CHUNK - 1:r0 + CHUNK, A_HEADS + h:A_HEADS + h + 1]
                eg[u_] = jnp.exp(g_col[u_])
                g_row = gcs_t[A_HEADS + h:A_HEADS + h + 1, pair_lanes]
                g_rows.append(g_row if c % 2 == h % 2 else pltpu.roll(g_row, CHUNK, axis=1))
            g_row_p = jnp.concatenate([jnp.where(low_row, g_rows[0], g_rows[1]),
                                       jnp.where(low_row, g_rows[2], g_rows[3])], axis=-1)
            g_col_p = pack([g_col[b, c, h] for h in range(A_HEADS)])
            dec_p[b, c] = jnp.exp(jnp.where(ti_p >= ii_p, g_col_p - g_row_p, -jnp.inf))
            beta_p[b, c] = pack([beta[b, c, h] for h in range(A_HEADS)])

    def prepare(bs):
        groups_b = [(b, c) for b in bs for c in range(nch)]
        beta, g_col, g_last, eg, dec_p, beta_p = {}, {}, {}, {}, {}, {}
        for b in bs:
            decay_terms(b, beta, g_col, g_last, eg, dec_p, beta_p)

        nmat = {}
        for (b, c) in groups_b:
            k16 = k_ref[b, rows(c), :].astype(BF16)
            q16 = q_ref[b, rows(c), :].astype(BF16)
            k_heads = jnp.concatenate(
                [jnp.concatenate([k16[:, lanes(h)] if hh == h else zero64 for hh in range(A_HEADS)], axis=-1)
                 for h in range(A_HEADS)], axis=0)
            kq = _dot_nt(jnp.concatenate([k16, q16], axis=0), k_heads)
            nmat[b, c] = -(beta_p[b, c] * kq[:CHUNK] * jnp.where(ti_p > ii_p, dec_p[b, c], 0.0))
            akd_s[wslot, b * nch + c, 0:CHUNK, :] = (kq[CHUNK:] * dec_p[b, c]).astype(BF16)
        stage_done()

        rsum = dict(nmat)
        pw16 = {g_: nmat[g_].astype(BF16) for g_ in groups_b}
        pw = {g_: _dot(pw16[g_], block_diag(pw16[g_])) for g_ in groups_b}
        stage_done()
        for step in range(1, 6):
            last = step == 5
            pw16 = {g_: pw[g_].astype(BF16) for g_ in groups_b}
            rp = {}
            for g_ in groups_b:
                r16 = rsum[g_].astype(BF16)
                rp[g_] = _dot(r16 if last else jnp.concatenate([r16, pw16[g_]], axis=0), block_diag(pw16[g_]))
            for g_ in groups_b:
                rsum[g_] = rsum[g_] + pw[g_] + rp[g_][:CHUNK]
                if not last:
                    pw[g_] = rp[g_][CHUNK:]
            stage_done()

        for (b, c) in groups_b:
            for h in range(A_HEADS):
                u_ = (b, c, h)
                i = uid[u_]
                kc = k_ref[b, rows(c), lanes(h)]
                rhs = jnp.concatenate([(beta[u_] * eg[u_]) * kc, beta[u_] * v_ref[b, rows(c), lanes(h)]],
                                      axis=-1)
                rhs16 = rhs.astype(BF16)
                rhs_rows = jnp.concatenate([rhs16 if hh == h else zrhs for hh in range(A_HEADS)], axis=0)
                sol = rhs + _dot(rsum[b, c].astype(BF16), rhs_rows)
                wq_s[wslot, i] = jnp.concatenate([sol[:, :A_DK], eg[u_] * q_ref[b, rows(c), lanes(h)]],
                                                 axis=0).astype(BF16)
                ut_s[wslot, i] = sol[:, A_DK:]
                gl_s[wslot, i] = jnp.broadcast_to(jnp.exp(g_last[u_]), (1, A_DV))
        for (b, c) in groups_b:
            kd = [jnp.exp(g_last[b, c, h] - g_col[b, c, h]) * k_ref[b, rows(c), lanes(h)]
                  for h in range(A_HEADS)]
            for p in range(A_HEADS // 2):
                akd_s[wslot, b * nch + c, CHUNK:, p * LANES:(p + 1) * LANES] = (
                    jnp.concatenate([kd[2 * p], kd[2 * p + 1]], axis=0).T.astype(BF16))
        stage_done()

    run_rec(n_rec - 1)
    for b0 in range(0, bsz, DELTA_WAVE):
        prepare(range(b0, b0 + DELTA_WAVE))
    run_rec(0)

    for b in range(bsz):
        for h in range(A_HEADS):
            s_scr[b * A_HEADS + h] = s_cur[b, h]

    @pl.when(t == pl.num_programs(0) - 1)
    def _():
        for b in range(bsz):
            for h in range(A_HEADS):
                st_ref[b, h] = s_cur[b, h]


def _delta(qkv, gb, za, na_row):
    bsz, t, _ = qkv.shape
    ct = DELTA_CT
    nt = t // ct
    n_units = bsz * (ct // CHUNK) * A_HEADS
    prep = lambda w, j=0: pl.BlockSpec((bsz, ct, w), lambda i: (0, jnp.minimum(i, nt - 1), j))
    rec = lambda w: pl.BlockSpec((bsz, ct, w), lambda i: (0, jnp.maximum(i - 1, 0), 0))
    return pl.pallas_call(
        _delta_kernel,
        grid=(nt + 1,),
        in_specs=[prep(A_WIDTH, 0), prep(A_WIDTH, 1), prep(A_WIDTH, 2), prep(LANES), rec(A_WIDTH),
                  pl.BlockSpec((1, A_DV), lambda i: (0, 0))],
        out_specs=[rec(A_WIDTH),
                   pl.BlockSpec((bsz, A_HEADS, A_DK, A_DV), lambda i: (0, 0, 0, 0))],
        out_shape=[jax.ShapeDtypeStruct((bsz, t, A_WIDTH), BF16),
                   jax.ShapeDtypeStruct((bsz, A_HEADS, A_DK, A_DV), F32)],
        scratch_shapes=[pltpu.VMEM((bsz * A_HEADS, A_DK, A_DV), F32),
                        pltpu.VMEM((2, n_units, 2 * CHUNK, A_DK), BF16),
                        pltpu.VMEM((2, n_units, CHUNK, A_DV), F32),
                        pltpu.VMEM((2, n_units // A_HEADS, CHUNK + A_DK, A_HEADS * CHUNK), BF16),
                        pltpu.VMEM((2, n_units, 1, A_DV), F32)],
        compiler_params=pltpu.CompilerParams(dimension_semantics=("arbitrary",),
                                             vmem_limit_bytes=VMEM_LIMIT),
        name="delta",
    )(qkv, qkv, qkv, gb, za, na_row)


def _swa_out_kernel(sink_ref, qb_ref, kc_ref, kp_ref, krc_ref, krp_ref, v0c_ref, v0p_ref, v1c_ref, v1p_ref,
                    zb_ref, oa_ref, x_ref, gate_ref, w_ref, g_ref, b_ref, y_ref):
    n = pl.program_id(1)
    gate = gate_ref[pl.ds(pl.program_id(0), 1), :]
    tq = qb_ref.shape[1]
    blk = WINDOW
    kx = (jnp.concatenate([kp_ref[0], kc_ref[0]], axis=0), jnp.concatenate([krp_ref[0], krc_ref[0]], axis=0))
    vd = (jnp.concatenate([v0p_ref[0], v0c_ref[0]], axis=0), jnp.concatenate([v1p_ref[0], v1c_ref[0]], axis=0))

    a = lax.broadcasted_iota(jnp.int32, (2 * blk, 2 * blk), 0) % blk
    j = lax.broadcasted_iota(jnp.int32, (2 * blk, 2 * blk), 1)
    rel = a + blk - j
    band = (rel >= 0) & (rel <= WINDOW)
    band_first = band & ((n > 0) | (j >= blk))
    top = lax.broadcasted_iota(jnp.int32, (2 * blk, 1), 0) < blk
    low = _lane((blk, LANES)) < B_HD
    zero = jnp.zeros((blk, LANES), BF16)

    qrows = lambda i: slice(i * blk, (i + 1) * blk)
    krows = lambda i: slice(i * blk, (i + 2) * blk)
    sink = {(kh, half): jnp.where(top, sink_ref[kh * B_GROUP + half] * LOG2E,
                                  sink_ref[kh * B_GROUP + half + 2] * LOG2E)
            for kh in range(B_KV_HEADS) for half in range(2)}
    def wave_units(i0):
        blocks = range(i0, i0 + SWA_WAVE)
        return blocks, [(i, kh, half) for i in blocks for kh in range(B_KV_HEADS) for half in range(2)]

    def score_matmuls(i0):
        blocks, units = wave_units(i0)
        mix_a = {i: _dot(oa_ref[0, qrows(i), :], w_ref[0:A_WIDTH, :]) for i in blocks}
        sc = {}
        for (i, kh, half) in units:
            qs = []
            for g in range(2):
                grp = kh * 2 + g
                xg = qb_ref[0, qrows(i), grp * LANES:(grp + 1) * LANES]
                qs.append(jnp.where(low if half == 0 else jnp.logical_not(low), xg, zero))
            qz = jnp.concatenate(qs, axis=0)
            sc[i, kh, half] = _dot_nt(qz, kx[0 if kh == half else 1][krows(i)])
        return mix_a, sc

    def finish(i0, mix_a, sc):
        blocks, units = wave_units(i0)
        p, den = {}, {}
        for u_ in units:
            i, kh, half = u_
            s_m = jnp.where(band_first if i == 0 else band, sc[u_], -jnp.inf)
            m = jnp.maximum(jnp.max(s_m, axis=-1, keepdims=True), sink[kh, half])
            e = jnp.exp2(s_m - m)
            den[u_] = jnp.sum(e, axis=-1, keepdims=True) + jnp.exp2(sink[kh, half] - m)
            p[u_] = e.astype(BF16)
        pv = {u_: _dot(p[u_], vd[u_[1]][krows(u_[0])]) for u_ in units}
        outs = {u_: pv[u_] / den[u_] for u_ in units}
        for i in blocks:
            ob = []
            for grp in range(B_WIDTH // LANES):
                kh, g = grp // 2, grp % 2
                og = jnp.where(low, outs[i, kh, 0][g * blk:(g + 1) * blk], outs[i, kh, 1][g * blk:(g + 1) * blk])
                ob.append((og * _silu(zb_ref[0, qrows(i), grp * LANES:(grp + 1) * LANES])).astype(BF16))
            mix = mix_a[i] + _dot(jnp.concatenate(ob, axis=-1), w_ref[A_WIDTH:MIX_WIDTH, :])
            r = DEEPNORM_ALPHA * x_ref[0, qrows(i), :] + (1.0 + gate) * mix
            y_ref[0, qrows(i), :] = _layer_norm(r, g_ref[...], b_ref[...])

    starts = list(range(0, tq // blk, SWA_WAVE))
    pending = score_matmuls(starts[0])
    for w, i0 in enumerate(starts):
        ready = pending
        if w + 1 < len(starts):
            pending = score_matmuls(starts[w + 1])
        finish(i0, *ready)


def _swa_out(sinks, qb, kvl, zb, oa, x, mod, mod_row0, w_out, ln_g, ln_b):
    bsz, t, _ = qb.shape
    tq = SWA_TQ
    per = tq // WINDOW
    cur = lambda w: pl.BlockSpec((1, tq, w), lambda b, i: (b, i, 0))
    kv_cur = lambda j: pl.BlockSpec((1, tq, LANES), lambda b, i: (b, i, j))
    kv_prev = lambda j: pl.BlockSpec((1, WINDOW, LANES), lambda b, i: (b, jnp.maximum(i * per - 1, 0), j))
    const2 = lambda s: pl.BlockSpec(s, lambda b, i: (0, 0))
    return pl.pallas_call(
        _swa_out_kernel,
        grid=(bsz, t // tq),
        in_specs=[pl.BlockSpec(memory_space=pltpu.SMEM), cur(B_WIDTH),
                  kv_cur(0), kv_prev(0), kv_cur(1), kv_prev(1), kv_cur(2), kv_prev(2), kv_cur(3), kv_prev(3),
                  cur(B_WIDTH), cur(A_WIDTH), cur(D_MODEL),
                  pl.BlockSpec((8, D_MODEL), lambda b, i: (mod_row0 // 8, 2)),
                  const2((MIX_WIDTH, D_MODEL)), const2((1, D_MODEL)), const2((1, D_MODEL))],
        out_specs=cur(D_MODEL),
        out_shape=jax.ShapeDtypeStruct((bsz, t, D_MODEL), F32),
        compiler_params=pltpu.CompilerParams(dimension_semantics=("arbitrary", "arbitrary"),
                                             vmem_limit_bytes=VMEM_LIMIT),
        name="swa_out",
    )(sinks, qb, kvl, kvl, kvl, kvl, kvl, kvl, kvl, kvl, zb, oa, x, mod, w_out, ln_g, ln_b)


def _out_kernel(oa_ref, ob_ref, x_ref, gate_ref, w_ref, g_ref, b_ref, y_ref):
    mix = _dot(oa_ref[...], w_ref[0:A_WIDTH, :]) + _dot(ob_ref[...], w_ref[A_WIDTH:MIX_WIDTH, :])
    r = DEEPNORM_ALPHA * x_ref[...] + (1.0 + gate_ref[...]) * mix
    y_ref[...] = _layer_norm(r, g_ref[...], b_ref[...])


def _out(oa, ob, x, mod, w_out, ln_g, ln_b):
    n = x.shape[0]
    full = lambda s: pl.BlockSpec(s, lambda i: (0, 0))
    return pl.pallas_call(
        _out_kernel,
        grid=(1,),
        in_specs=[full((n, A_WIDTH)), full((n, B_WIDTH)), full((n, D_MODEL)),
                  pl.BlockSpec((n, D_MODEL), lambda i: (0, 2)),
                  full((MIX_WIDTH, D_MODEL)), full((1, D_MODEL)), full((1, D_MODEL))],
        out_specs=full((n, D_MODEL)),
        out_shape=jax.ShapeDtypeStruct((n, D_MODEL), F32),
        compiler_params=pltpu.CompilerParams(dimension_semantics=("arbitrary",),
                                             vmem_limit_bytes=VMEM_LIMIT),
        name="out",
    )(oa, ob, x, mod, w_out, ln_g, ln_b)


def _sproj_kernel(x_ref, mod_ref, w_ref, cw_ref, cst_ref, alog_ref, dt_ref, cos_ref, sin_ref,
                  q_ref, k_ref, v_ref, za_ref, gb_ref, qb_ref, kb_ref, vb_ref, zb_ref, ncs_ref):
    shift = mod_ref[:, 0:D_MODEL]
    scale = mod_ref[:, D_MODEL:2 * D_MODEL]
    h = (x_ref[...] * (1.0 + scale) + shift).astype(BF16)

    for gi, o_ref in enumerate((q_ref, k_ref, v_ref)):
        c0 = gi * A_WIDTH
        cs = slice(c0, c0 + A_WIDTH)
        u = _dot(h, w_ref[:, cs])
        acc = cst_ref[0, :, cs] * cw_ref[0:1, cs]
        acc = acc + cst_ref[1, :, cs] * cw_ref[1:2, cs]
        acc = acc + cst_ref[2, :, cs] * cw_ref[2:3, cs]
        acc = acc + u * cw_ref[3:4, cs]
        y = _silu(acc)
        if gi == 0:
            y = _l2norm_heads(y, A_DK ** -0.5)
        elif gi == 1:
            y = _l2norm_heads(y, 1.0)
        o_ref[...] = y
        ncs_ref[0, :, cs] = cst_ref[1, :, cs]
        ncs_ref[1, :, cs] = cst_ref[2, :, cs]
        ncs_ref[2, :, cs] = u

    za_ref[...] = _dot(h, w_ref[:, C_ZA:C_ZA + A_WIDTH])
    gb_ref[...] = _gate_lanes(_dot(h, w_ref[:, C_BD:C_BD + LANES]), alog_ref[...], dt_ref[...])

    cos = cos_ref[...]
    sin = sin_ref[...]
    uq = _dot(h, w_ref[:, C_QB:C_QB + B_WIDTH])
    for g in range(B_WIDTH // LANES):
        qb_ref[:, g * LANES:(g + 1) * LANES] = (
            _rotary_group(uq[:, g * LANES:(g + 1) * LANES], cos, sin) * (B_HD ** -0.5))
    kb_ref[...] = _rotary_group(_dot(h, w_ref[:, C_KB:C_KB + LANES]), cos, sin)
    vb_ref[...] = _dot(h, w_ref[:, C_VB:C_VB + LANES])
    zb_ref[...] = _dot(h, w_ref[:, C_ZB:C_ZB + B_WIDTH])


def _sproj(x, mod_s, w_r, conv_w, cst, alog_row, dt_row, cos_row, sin_row):
    n = x.shape[0]
    full = lambda s: pl.BlockSpec(s, lambda i: (0,) * len(s))
    wide = lambda w: jax.ShapeDtypeStruct((n, w), F32)
    return pl.pallas_call(
        _sproj_kernel,
        grid=(1,),
        in_specs=[full((n, D_MODEL)), pl.BlockSpec((n, 3 * D_MODEL), lambda i: (0, 0)),
                  full((D_MODEL, W_COLS)),
                  full((CONV_W, A_QKV)), full((CONV_W - 1, n, A_QKV)),
                  full((1, LANES)), full((1, LANES)), full((1, LANES)), full((1, LANES))],
        out_specs=[full((n, A_WIDTH)), full((n, A_WIDTH)), full((n, A_WIDTH)), full((n, A_WIDTH)),
                   full((n, LANES)), full((n, B_WIDTH)), full((n, LANES)), full((n, LANES)),
                   full((n, B_WIDTH)), full((CONV_W - 1, n, A_QKV))],
        out_shape=[wide(A_WIDTH), wide(A_WIDTH), wide(A_WIDTH), wide(A_WIDTH), wide(LANES),
                   wide(B_WIDTH), wide(LANES), wide(LANES), wide(B_WIDTH),
                   jax.ShapeDtypeStruct((CONV_W - 1, n, A_QKV), F32)],
        compiler_params=pltpu.CompilerParams(dimension_semantics=("arbitrary",),
                                             vmem_limit_bytes=VMEM_LIMIT),
        name="sproj",
    )(x, mod_s, w_r, conv_w, cst, alog_row, dt_row, cos_row, sin_row)


def _sstep_kernel(sink_ref, q_ref, k_ref, v_ref, gb_ref, za_ref, na_ref, st_ref,
                  qb_ref, kn_ref, vn_ref, zb_ref, ck_ref, cv_ref,
                  oa_ref, ob_ref, nst_ref, nck_ref, ncv_ref,
                  o_scr, ob_scr):
    bt = q_ref.shape[0]
    gbv = gb_ref[...]

    pick = (lax.broadcasted_iota(jnp.int32, (bt, bt * A_DV), 1) // A_DV
            == lax.broadcasted_iota(jnp.int32, (bt, bt * A_DV), 0))
    pick = jnp.where(pick, 1.0, 0.0).astype(BF16)
    for h in range(A_HEADS):
        hs = slice(h * A_DK, (h + 1) * A_DK)
        q_rep = _dot(q_ref[:, hs].T.astype(BF16), pick)
        k_rep = _dot(k_ref[:, hs].T.astype(BF16), pick)
        for bb in range(bt):
            eg = jnp.exp(gbv[bb:bb + 1, A_HEADS + h:A_HEADS + h + 1])
            beta = gbv[bb:bb + 1, h:h + 1]
            kcol = k_rep[:, bb * A_DV:(bb + 1) * A_DV]
            qcol = q_rep[:, bb * A_DV:(bb + 1) * A_DV]
            s1 = eg * st_ref[bb, h]
            pred = jnp.sum(kcol * s1, axis=0, keepdims=True)
            upd = beta * (v_ref[bb:bb + 1, hs] - pred)
            s2 = s1 + kcol * upd
            nst_ref[bb, h] = s2
            o_scr[bb:bb + 1, hs] = jnp.sum(qcol * s2, axis=0, keepdims=True)
    na = na_ref[...]
    for h in range(A_HEADS):
        hs = slice(h * A_DK, (h + 1) * A_DK)
        o = o_scr[:, hs]
        on = o * lax.rsqrt(jnp.mean(o * o, axis=-1, keepdims=True) + RMS_EPS) * na
        oa_ref[:, hs] = (on * _silu(za_ref[:, hs])).astype(BF16)

    row8 = lax.broadcasted_iota(jnp.int32, (B_HEADS, LANES), 0)
    lane8 = _lane((B_HEADS, LANES))
    own_half = (lane8 >= B_HD) == (row8 >= B_GROUP)
    rcol = lax.broadcasted_iota(jnp.int32, (B_HEADS, 1), 0)
    sink = jnp.zeros((B_HEADS, 1), F32)
    for r in range(B_HEADS):
        sink = jnp.where(rcol == r, sink_ref[r], sink)
    qv = qb_ref[...]
    qv_r = jnp.concatenate([pltpu.roll(qv[:, g * LANES:(g + 1) * LANES], B_HD, axis=1)
                            for g in range(B_WIDTH // LANES)], axis=-1)
    kn_t = kn_ref[...].T
    vn_t = vn_ref[...].T
    newest = _lane((LANES, WINDOW)) == WINDOW - 1
    qzs, scs = [], []
    for bb in range(bt):
        qz = jnp.zeros((B_HEADS, LANES), F32)
        for r in range(B_HEADS):
            grp, half, kh = r // 2, r % 2, r // B_GROUP
            src = qv if half == kh else qv_r
            qz = jnp.where(row8 == r, src[bb:bb + 1, grp * LANES:(grp + 1) * LANES], qz)
        qzs.append(jnp.where(own_half, qz, 0.0))
    for bb in range(bt):
        scs.append(_dot(qzs[bb], ck_ref[bb]))
    ps, pnews, dens = [], [], []
    for bb in range(bt):
        sc_new = jnp.sum(qzs[bb] * kn_ref[bb:bb + 1, :], axis=-1, keepdims=True)
        m = jnp.maximum(jnp.maximum(jnp.max(scs[bb], axis=-1, keepdims=True), sc_new), sink)
        p = jnp.exp(scs[bb] - m)
        p_new = jnp.exp(sc_new - m)
        ps.append(p)
        pnews.append(p_new)
        dens.append(jnp.sum(p, axis=-1, keepdims=True) + p_new + jnp.exp(sink - m))
    pvs = [_dot_nt(ps[bb], cv_ref[bb]) for bb in range(bt)]
    for bb in range(bt):
        o = (pvs[bb] + pnews[bb] * vn_ref[bb:bb + 1, :]) / dens[bb]
        o = jnp.where(own_half, o, 0.0)
        ob_scr[bb * B_HEADS:(bb + 1) * B_HEADS, :] = o + pltpu.roll(o, B_HD, axis=1)
    for bb in range(bt):
        nck_ref[bb] = jnp.where(newest, kn_t[:, bb:bb + 1], pltpu.roll(ck_ref[bb], WINDOW - 1, axis=1))
        ncv_ref[bb] = jnp.where(newest, vn_t[:, bb:bb + 1], pltpu.roll(cv_ref[bb], WINDOW - 1, axis=1))
    low = _lane((bt, LANES)) < B_HD
    for grp in range(B_WIDTH // LANES):
        even = ob_scr[pl.ds(2 * grp, bt, stride=B_HEADS), :]
        odd = ob_scr[pl.ds(2 * grp + 1, bt, stride=B_HEADS), :]
        gs = slice(grp * LANES, (grp + 1) * LANES)
        ob_ref[:, gs] = (jnp.where(low, even, odd) * _silu(zb_ref[:, gs])).astype(BF16)


def _sstep(sinks, q, k, v, gb, za, na_row, state, qb, kn, vn, zb, ck, cv):
    n = q.shape[0]
    bt = STEP_BT
    row = lambda w: pl.BlockSpec((bt, w), lambda i: (i, 0))
    st_spec = pl.BlockSpec((bt, A_HEADS, A_DK, A_DV), lambda i: (i, 0, 0, 0))
    c_spec = pl.BlockSpec((bt, WINDOW, LANES), lambda i: (i, 0, 0))
    return pl.pallas_call(
        _sstep_kernel,
        grid=(n // bt,),
        in_specs=[pl.BlockSpec(memory_space=pltpu.SMEM),
                  row(A_WIDTH), row(A_WIDTH), row(A_WIDTH), row(LANES), row(A_WIDTH),
                  pl.BlockSpec((1, A_DV), lambda i: (0, 0)), st_spec,
                  row(B_WIDTH), row(LANES), row(LANES), row(B_WIDTH), c_spec, c_spec],
        out_specs=[row(A_WIDTH), row(B_WIDTH), st_spec, c_spec, c_spec],
        out_shape=[jax.ShapeDtypeStruct((n, A_WIDTH), BF16),
                   jax.ShapeDtypeStruct((n, B_WIDTH), BF16),
                   jax.ShapeDtypeStruct((n, A_HEADS, A_DK, A_DV), F32),
                   jax.ShapeDtypeStruct((n, WINDOW, LANES), F32),
                   jax.ShapeDtypeStruct((n, WINDOW, LANES), F32)],
        scratch_shapes=[pltpu.VMEM((bt, A_WIDTH), F32), pltpu.VMEM((bt * B_HEADS, LANES), F32)],
        compiler_params=pltpu.CompilerParams(dimension_semantics=("arbitrary",),
                                             vmem_limit_bytes=VMEM_LIMIT),
        name="sstep",
    )(sinks, q, k, v, gb, za, na_row, state, qb, kn, vn, zb, ck, cv)


def _rope_tables(pos):
    half = B_HD // 2
    inv = 1.0 / (ROPE_THETA ** (np.arange(half, dtype=np.float64) / half))
    ang = np.asarray(pos, np.float64)[:, None] * inv[None, :]
    cos, sin = np.cos(ang), np.sin(ang)
    reps = LANES // B_HD
    return (jnp.asarray(np.tile(np.concatenate([cos, cos], -1), (1, reps)), F32),
            jnp.asarray(np.tile(np.concatenate([-sin, sin], -1), (1, reps)), F32))


def _pad_row(vec, offset):
    return jnp.pad(vec.astype(F32).reshape(1, -1), ((0, 0), (offset, LANES - offset - vec.shape[0])))


def _layer(x_prompt, x_sample, state_conv, state_delta, cache_k, cache_v, c_prompt, c_sample,
           w_ada, b_ada, w_in, conv_w, a_log, dt_bias, norm_a, sinks, w_out, ln_g, ln_b):
    bsz, seq, _ = x_prompt.shape
    n_s = x_sample.shape[0]

    w_r = _wprep(jnp.swapaxes(w_in, 0, 1))
    w_o = w_out.astype(BF16)
    alog_row = _pad_row(a_log, A_HEADS)
    dt_row = _pad_row(dt_bias, A_HEADS)
    na_row = norm_a.reshape(1, A_DV)
    g_row = ln_g.reshape(1, D_MODEL)
    b_row = ln_b.reshape(1, D_MODEL)

    assert n_s % 8 == 0 and bsz <= 8
    mod = _ada(c_sample, c_prompt, w_ada, b_ada.reshape(1, 3 * D_MODEL))

    cos_p, sin_p = _rope_tables(np.arange(seq))
    (qkv, za, gb, qb, kvl, zb, conv_p, kb_last, vb_last) = _proj(
        x_prompt, mod, n_s, w_r, conv_w, alog_row, dt_row, cos_p, sin_p)
    oa, delta_p = _delta(qkv, gb, za, na_row)
    y_p = _swa_out(sinks, qb, kvl, zb, oa, x_prompt, mod, n_s, w_o, g_row, b_row)
    swa_k_p = kb_last.reshape(bsz, WINDOW, B_KV_HEADS, B_HD)
    swa_v_p = vb_last.reshape(bsz, WINDOW, B_KV_HEADS, B_HD)

    cos_s, sin_s = _rope_tables(np.array([PAST_LEN]))
    xs = x_sample.reshape(n_s, D_MODEL)
    cst = jnp.transpose(state_conv, (1, 0, 2))
    sq, sk, sv, sza, sgb, sqb, skn, svn, szb, ncs = _sproj(xs, mod, w_r, conv_w, cst, alog_row, dt_row,
                                                           cos_s, sin_s)
    soa, sob, delta_s, nck, ncv = _sstep(sinks, sq, sk, sv, sgb, sza, na_row, state_delta,
                                         sqb, skn, svn, szb,
                                         jnp.swapaxes(cache_k.reshape(n_s, WINDOW, LANES), 1, 2),
                                         jnp.swapaxes(cache_v.reshape(n_s, WINDOW, LANES), 1, 2))
    y_s = _out(soa, sob, xs, mod, w_o, g_row, b_row)
    conv_s = jnp.transpose(ncs, (1, 0, 2))
    unpack = lambda c: jnp.swapaxes(c, 1, 2).reshape(n_s, WINDOW, B_KV_HEADS, B_HD)
    return (y_p, y_s.reshape(n_s, 1, D_MODEL), conv_p, delta_p, swa_k_p, swa_v_p,
            conv_s, delta_s, unpack(nck), unpack(ncv))


def kernel(x_prompt, x_sample, state_conv, state_delta, cache_swa_k, cache_swa_v, c_prompt, c_sample,
           w_ada, b_ada, w_in, conv_w, a_log, dt_bias, norm_a, sinks, w_out, ln_g, ln_b):
    assert w_ada.shape[0] == DEPTH == 1
    outs = _layer(x_prompt, x_sample, state_conv[0], state_delta[0], cache_swa_k[0], cache_swa_v[0],
                  c_prompt, c_sample, w_ada[0], b_ada[0], w_in[0], conv_w[0], a_log[0], dt_bias[0],
                  norm_a[0], sinks[0], w_out[0], ln_g[0], ln_b[0])
    y_p, y_s = outs[0], outs[1]
    return (y_p, y_s) + tuple(o[None] for o in outs[2:])
```

```python
import jax
import jax.numpy as jnp
import numpy as np
from jax import lax
from jax.experimental import pallas as pl
from jax.experimental.pallas import tpu as pltpu

F32 = jnp.float32
BF16 = jnp.bfloat16

D_MODEL = 1024
DEPTH = 1
PAST_LEN = 8192
A_HEADS = 4
A_DK = 128
A_DV = 128
A_WIDTH = A_HEADS * A_DV
A_QKV = 3 * A_WIDTH
CONV_W = 4
CHUNK = 64
B_HEADS = 8
B_KV_HEADS = 2
B_HD = 64
B_GROUP = B_HEADS // B_KV_HEADS
B_WIDTH = B_HEADS * B_HD
B_KV_WIDTH = B_KV_HEADS * B_HD
WINDOW = 128
ROPE_THETA = 10000.0
MIX_WIDTH = A_WIDTH + B_WIDTH
DEEPNORM_ALPHA = (2 * DEPTH) ** 0.25
LOG2E = 1.4426950408889634
LN_EPS = 1e-5
RMS_EPS = 1e-6
L2_EPS = 1e-6

OFF_A_Z = A_QKV
OFF_A_BETA = OFF_A_Z + A_WIDTH
OFF_A_DECAY = OFF_A_BETA + A_HEADS
OFF_B_Q = OFF_A_DECAY + A_HEADS
OFF_B_K = OFF_B_Q + B_WIDTH
OFF_B_V = OFF_B_K + B_KV_WIDTH
OFF_B_Z = OFF_B_V + B_KV_WIDTH
PROJ_COLS = OFF_B_Z + B_WIDTH

LANES = 128
C_QKV = 0
C_ZA = C_QKV + A_QKV
C_QB = C_ZA + A_WIDTH
C_KB = C_QB + B_WIDTH
C_VB = C_KB + B_KV_WIDTH
C_ZB = C_VB + B_KV_WIDTH
C_BD = C_ZB + B_WIDTH
WPREP_TN = 256
W_COLS = C_BD + WPREP_TN

VMEM_LIMIT = 56 * 1024 * 1024

ADA_TN = 1536
PROJ_TM = 512
PROJ_CW = 256
PROJ_PARTS = 2
DELTA_CT = 256
DELTA_WAVE = 2
SWA_TQ = 512
SWA_WAVE = 2
STEP_BT = 16


def _dot(a, b):
    return jnp.dot(a, b, preferred_element_type=F32)


def _dot_nt(a, b):
    return lax.dot_general(a, b, (((1,), (1,)), ((), ())), preferred_element_type=F32)


def _silu(x):
    return x * jax.nn.sigmoid(x)


def _softplus(x):
    return jnp.maximum(x, 0.0) + jnp.log1p(jnp.exp(-jnp.abs(x)))


def _lane(shape):
    return lax.broadcasted_iota(jnp.int32, shape, len(shape) - 1)


def _l2norm_heads(y, scale):
    outs = []
    for h in range(y.shape[1] // A_DK):
        xh = y[:, h * A_DK:(h + 1) * A_DK]
        ss = jnp.sum(xh * xh, axis=-1, keepdims=True)
        xn = xh * lax.rsqrt(ss + L2_EPS)
        outs.append(xn * scale if scale != 1.0 else xn)
    return jnp.concatenate(outs, axis=-1)


def _rotary_group(xg, cos, sin_signed):
    lane = _lane(xg.shape)
    swapped = jnp.where((lane % B_HD) < (B_HD // 2),
                        pltpu.roll(xg, LANES - B_HD // 2, axis=1),
                        pltpu.roll(xg, B_HD // 2, axis=1))
    return xg * cos + swapped * sin_signed


def _kv_layouts(kb, vb):
    low = _lane(kb.shape) < B_HD
    kbr = pltpu.roll(kb, B_HD, axis=1)
    vbr = pltpu.roll(vb, B_HD, axis=1)
    return kb, kbr, jnp.where(low, vb, vbr), jnp.where(low, vbr, vb)


def _gate_lanes(bd, alog_row, dt_row):
    lane = _lane(bd.shape)
    g = -jnp.exp(alog_row) * _softplus(bd + dt_row)
    return jnp.where(lane < A_HEADS, jax.nn.sigmoid(bd), g)


def _layer_norm(r, g, b):
    mu = jnp.mean(r, axis=-1, keepdims=True)
    d = r - mu
    var = jnp.mean(d * d, axis=-1, keepdims=True)
    return d * lax.rsqrt(var + LN_EPS) * g + b


def _wprep_kernel(wa_ref, wb_ref, o_ref):
    tn = wa_ref.shape[0]
    o_ref[:, 0:tn] = wa_ref[...].T.astype(BF16)
    xb = wb_ref[...]
    tail = pl.program_id(0) == pl.num_programs(0) - 1
    row = lax.broadcasted_iota(jnp.int32, xb.shape, 0)
    xb = jnp.where(jnp.logical_and(tail, row >= 2 * A_HEADS), 0.0, xb)
    o_ref[:, tn:2 * tn] = xb.T.astype(BF16)


def _wprep(w_t):
    tn = WPREP_TN
    n_a, n_b = OFF_A_BETA // tn, (PROJ_COLS - OFF_B_Q) // tn
    assert n_a * tn == OFF_A_BETA and n_b * tn == PROJ_COLS - OFF_B_Q and OFF_A_BETA + tn <= PROJ_COLS

    assert (n_a + n_b + 1) % 2 == 0

    def src_row(j):
        return jnp.where(j < n_a, j * tn, jnp.where(j < n_a + n_b, OFF_B_Q + (j - n_a) * tn, OFF_A_BETA))

    src = lambda k: pl.BlockSpec((pl.Element(tn), pl.Element(D_MODEL)),
                                 lambda j: (pl.multiple_of(src_row(2 * j + k), 8), 0))
    return pl.pallas_call(
        _wprep_kernel,
        grid=((n_a + n_b + 1) // 2,),
        in_specs=[src(0), src(1)],
        out_specs=pl.BlockSpec((D_MODEL, 2 * tn), lambda j: (0, j)),
        out_shape=jax.ShapeDtypeStruct((D_MODEL, W_COLS), BF16),
        compiler_params=pltpu.CompilerParams(dimension_semantics=("arbitrary",),
                                             vmem_limit_bytes=VMEM_LIMIT),
        name="wprep",
    )(w_t, w_t)


def _ada_kernel(cs_ref, cp_ref, w_ref, b_ref, o_ref):
    n_s, n_p = cs_ref.shape[0], cp_ref.shape[0]
    w = w_ref[...].astype(BF16)
    o_ref[0:n_s, :] = _dot(cs_ref[...].astype(BF16), w) + b_ref[...]
    o_ref[n_s:n_s + 8, :] = jnp.zeros((8, o_ref.shape[1]), F32)
    o_ref[n_s:n_s + n_p, :] = _dot(cp_ref[...].astype(BF16), w) + b_ref[...]


def _ada(c_sample, c_prompt, w_ada, b_ada):
    n_s, n_p = c_sample.shape[0], c_prompt.shape[0]
    rows = n_s + 8
    tn = ADA_TN
    return pl.pallas_call(
        _ada_kernel,
        grid=(3 * D_MODEL // tn,),
        in_specs=[pl.BlockSpec((n_s, D_MODEL), lambda j: (0, 0)),
                  pl.BlockSpec((n_p, D_MODEL), lambda j: (0, 0)),
                  pl.BlockSpec((D_MODEL, tn), lambda j: (0, j)),
                  pl.BlockSpec((1, tn), lambda j: (0, j))],
        out_specs=pl.BlockSpec((rows, tn), lambda j: (0, j)),
        out_shape=jax.ShapeDtypeStruct((rows, 3 * D_MODEL), F32),
        compiler_params=pltpu.CompilerParams(dimension_semantics=("arbitrary",),
                                             vmem_limit_bytes=VMEM_LIMIT),
        name="ada",
    )(c_sample, c_prompt, w_ada, b_ada)


def _proj_kernel(x_ref, mod_ref, w_ref, cw_ref, alog_ref, dt_ref, cos_ref, sin_ref,
                 qkv_ref, za_ref, gb_ref, qb_ref, kvl_ref, zb_ref, cst_ref, kbl_ref, vbl_ref, ubuf):
    tm = x_ref.shape[1]
    t = pl.program_id(1)

    @pl.when(t == 0)
    def _():
        ubuf[...] = jnp.zeros(ubuf.shape, F32)

    brow = pl.ds(pl.program_id(0), 1)
    shift = mod_ref[brow, 0:D_MODEL]
    scale = mod_ref[brow, D_MODEL:2 * D_MODEL]

    rp = tm // PROJ_PARTS
    cw = PROJ_CW
    sub = lax.broadcasted_iota(jnp.int32, (rp // 8, 8, cw), 1)
    pieces = [slice(c0, c0 + cw) for c0 in range(0, A_QKV, cw)]

    def matmuls(r0):
        h = (x_ref[0, r0:r0 + rp, :] * (1.0 + scale) + shift).astype(BF16)
        return ([_dot(h, w_ref[:, cs]) for cs in pieces],
                _dot(h, w_ref[:, C_ZA:C_ZA + A_WIDTH]), _dot(h, w_ref[:, C_BD:C_BD + LANES]),
                _dot(h, w_ref[:, C_QB:C_QB + B_WIDTH]), _dot(h, w_ref[:, C_KB:C_KB + 2 * LANES]),
                _dot(h, w_ref[:, C_ZB:C_ZB + B_WIDTH]))

    def epilogue(r0, results):
        rs = slice(r0, r0 + rp)
        us, za, ubd, uq, ukv, zb = results

        def conv_epilogue(cs, u):
            gi = cs.start // A_WIDTH
            groups = jnp.concatenate([ubuf[:, cs], u], axis=0).reshape(rp // 8 + 1, 8, cw)
            acc = None
            for j in range(CONV_W - 1, 0, -1):
                rot = pltpu.roll(groups, j, axis=1)
                term = (jnp.where(sub < j, rot[:-1], rot[1:]).reshape(rp, cw)
                        * cw_ref[CONV_W - 1 - j:CONV_W - j, cs])
                acc = term if acc is None else acc + term
            y = _silu(acc + u * cw_ref[CONV_W - 1:CONV_W, cs])
            if gi == 0:
                y = _l2norm_heads(y, A_DK ** -0.5)
            elif gi == 1:
                y = _l2norm_heads(y, 1.0)
            qkv_ref[0, rs, cs] = y
            ubuf[:, cs] = u[rp - 8:rp]
            if r0 + rp == tm:
                cst_ref[0, :, cs] = u[rp - (CONV_W - 1):rp]

        for cs, u in zip(pieces, us):
            conv_epilogue(cs, u)
        za_ref[0, rs, :] = za
        zb_ref[0, rs, :] = zb
        gb_ref[0, rs, :] = _gate_lanes(ubd, alog_ref[...], dt_ref[...])
        cos = cos_ref[rs, :]
        sin = sin_ref[rs, :]
        for g in range(B_WIDTH // LANES):
            qb_ref[0, rs, g * LANES:(g + 1) * LANES] = (
                _rotary_group(uq[:, g * LANES:(g + 1) * LANES], cos, sin) * (B_HD ** -0.5 * LOG2E)).astype(BF16)
        kb = _rotary_group(ukv[:, 0:LANES], cos, sin)
        vb = ukv[:, LANES:2 * LANES]
        for j, val in enumerate(_kv_layouts(kb, vb)):
            kvl_ref[0, rs, j * LANES:(j + 1) * LANES] = val.astype(BF16)
        return kb, vb

    starts = list(range(0, tm, rp))
    pending = matmuls(starts[0])
    for p, r0 in enumerate(starts):
        results = pending
        if p + 1 < len(starts):
            pending = matmuls(starts[p + 1])
        kb, vb = epilogue(r0, results)

    @pl.when(t == pl.num_programs(1) - 1)
    def _():
        kbl_ref[0] = kb[rp - WINDOW:rp]
        vbl_ref[0] = vb[rp - WINDOW:rp]


def _proj(x, mod, mod_row0, w_r, conv_w, alog_row, dt_row, cos_t, sin_t):
    bsz, t, _ = x.shape
    tm = PROJ_TM
    row = lambda w: pl.BlockSpec((1, tm, w), lambda b, i: (b, i, 0))
    const2 = lambda s: pl.BlockSpec(s, lambda b, i: (0, 0))
    per_b = lambda r, w: pl.BlockSpec((1, r, w), lambda b, i: (b, 0, 0))
    wide = lambda w, dt=F32: jax.ShapeDtypeStruct((bsz, t, w), dt)
    return pl.pallas_call(
        _proj_kernel,
        grid=(bsz, t // tm),
        in_specs=[row(D_MODEL),
                  pl.BlockSpec((8, 3 * D_MODEL), lambda b, i: (mod_row0 // 8, 0)),
                  const2((D_MODEL, W_COLS)),
                  const2((CONV_W, A_QKV)),
                  const2((1, LANES)), const2((1, LANES)),
                  pl.BlockSpec((tm, LANES), lambda b, i: (i, 0)),
                  pl.BlockSpec((tm, LANES), lambda b, i: (i, 0))],
        out_specs=[row(A_QKV), row(A_WIDTH), row(LANES), row(B_WIDTH), row(4 * LANES), row(B_WIDTH),
                   per_b(CONV_W - 1, A_QKV), per_b(WINDOW, LANES), per_b(WINDOW, LANES)],
        out_shape=[wide(A_QKV), wide(A_WIDTH), wide(LANES), wide(B_WIDTH, BF16), wide(4 * LANES, BF16),
                   wide(B_WIDTH),
                   jax.ShapeDtypeStruct((bsz, CONV_W - 1, A_QKV), F32),
                   jax.ShapeDtypeStruct((bsz, WINDOW, LANES), F32),
                   jax.ShapeDtypeStruct((bsz, WINDOW, LANES), F32)],
        scratch_shapes=[pltpu.VMEM((8, A_QKV), F32)],
        compiler_params=pltpu.CompilerParams(dimension_semantics=("arbitrary", "arbitrary"),
                                             vmem_limit_bytes=VMEM_LIMIT),
        name="proj",
    )(x, mod, w_r, conv_w, alog_row, dt_row, cos_t, sin_t)


def _delta_kernel(q_ref, k_ref, v_ref, gb_ref, za_ref, na_ref, oa_ref, st_ref,
                  s_scr, wq_s, ut_s, akd_s, gl_s):
    bsz, ct = q_ref.shape[0], q_ref.shape[1]
    nch = ct // CHUNK
    t = pl.program_id(0)
    wslot = t % 2
    rslot = 1 - wslot

    @pl.when(t == 0)
    def _():
        s_scr[...] = jnp.zeros(s_scr.shape, F32)
        wq_s[...] = jnp.zeros(wq_s.shape, BF16)
        ut_s[...] = jnp.zeros(ut_s.shape, F32)
        akd_s[...] = jnp.zeros(akd_s.shape, BF16)
        gl_s[...] = jnp.zeros(gl_s.shape, F32)

    units = [(b, c, h) for b in range(bsz) for c in range(nch) for h in range(A_HEADS)]
    uid = {u_: i for i, u_ in enumerate(units)}
    rows = lambda c: slice(c * CHUNK, (c + 1) * CHUNK)
    lanes = lambda h: slice(h * A_DK, (h + 1) * A_DK)
    na = na_ref[...]

    s_cur = {(b, h): s_scr[b * A_HEADS + h] for b in range(bsz) for h in range(A_HEADS)}
    ws, uu = {}, {}

    def rec_ws(c):
        for b in range(bsz):
            for h in range(A_HEADS):
                i = uid[b, c, h]
                ws[b, h] = _dot(wq_s[rslot, i], s_cur[b, h].astype(BF16))
                uu[b, h] = (ut_s[rslot, i] - ws[b, h][:CHUNK]).astype(BF16)

    def rec_ou(c):
        zpad = jnp.zeros((CHUNK, A_DV), BF16)
        for b in range(bsz):
            u_bd = jnp.concatenate(
                [jnp.concatenate([uu[b, h] if hh == h else zpad for hh in range(A_HEADS)], axis=-1)
                 for h in range(A_HEADS)], axis=0)
            ou = _dot(akd_s[rslot, b * nch + c], u_bd)
            for h in range(A_HEADS):
                o = ws[b, h][CHUNK:] + ou[:CHUNK, lanes(h)]
                s_cur[b, h] = gl_s[rslot, uid[b, c, h]] * s_cur[b, h] + ou[CHUNK:, lanes(h)]
                on = o * lax.rsqrt(jnp.mean(o * o, axis=-1, keepdims=True) + RMS_EPS) * na
                oa_ref[b, rows(c), lanes(h)] = (on * _silu(za_ref[b, rows(c), lanes(h)])).astype(BF16)

    rec_stages = []
    for c in range(nch):
        rec_stages += [lambda c=c: rec_ws(c), lambda c=c: rec_ou(c)]

    def run_rec(n_left_after):
        while rec_stages and len(rec_stages) > n_left_after:
            rec_stages.pop(0)()

    pk = A_HEADS * CHUNK
    low = _lane((CHUNK, LANES)) < CHUNK
    low_row = _lane((1, LANES)) < CHUNK
    ti_p = lax.broadcasted_iota(jnp.int32, (CHUNK, pk), 0)
    ii_p = _lane((CHUNK, pk)) % CHUNK
    zero64 = jnp.zeros((CHUNK, LANES), BF16)

    def pack(parts):
        return jnp.concatenate([jnp.where(low, parts[0], parts[1]), jnp.where(low, parts[2], parts[3])], axis=-1)

    def block_diag(x16):
        blocks = []
        for h in range(A_HEADS):
            pair, first = h // 2, h % 2 == 0
            piece = jnp.where(low if first else jnp.logical_not(low), x16[:, pair * LANES:(pair + 1) * LANES], zero64)
            blocks.append(jnp.concatenate([piece, zero64] if pair == 0 else [zero64, piece], axis=-1))
        return jnp.concatenate(blocks, axis=0)

    zrhs = jnp.zeros((CHUNK, 2 * A_DK), BF16)
    n_rec = len(rec_stages)
    n_slots = 8 * (bsz // DELTA_WAVE)
    done = [0]

    def stage_done():
        done[0] += 1
        run_rec(n_rec - 1 - (done[0] * n_rec) // n_slots)

    def decay_terms(b, beta, g_col, g_last, eg, dec_p, beta_p):
        gbv = gb_ref[b]
        rin = lax.broadcasted_iota(jnp.int32, gbv.shape, 0) % CHUNK
        gcs = gbv
        s = 1
        while s < CHUNK:
            gcs = gcs + jnp.where(rin >= s, pltpu.roll(gcs, s, axis=0), 0.0)
            s *= 2
        gcs_t = gcs.T
        for c in range(nch):
            r0 = c * CHUNK
            pair_lanes = slice((c // 2) * LANES, (c // 2 + 1) * LANES)
            g_rows = []
            for h in range(A_HEADS):
                u_ = (b, c, h)
                beta[u_] = jnp.broadcast_to(gbv[rows(c), h:h + 1], (CHUNK, A_DK))
                g_col[u_] = jnp.broadcast_to(gcs[rows(c), A_HEADS + h:A_HEADS + h + 1], (CHUNK, A_DK))
                g_last[u_] = gcs[r0 + CHUNK - 1:r0 + CHUNK, A_HEADS + h:A_HEADS + h + 1]
                eg[u_] = jnp.exp(g_col[u_])
                g_row = gcs_t[A_HEADS + h:A_HEADS + h + 1, pair_lanes]
                g_rows.append(g_row if c % 2 == h % 2 else pltpu.roll(g_row, CHUNK, axis=1))
            g_row_p = jnp.concatenate([jnp.where(low_row, g_rows[0], g_rows[1]),
                                       jnp.where(low_row, g_rows[2], g_rows[3])], axis=-1)
            g_col_p = pack([g_col[b, c, h] for h in range(A_HEADS)])
            dec_p[b, c] = jnp.exp(jnp.where(ti_p >= ii_p, g_col_p - g_row_p, -jnp.inf))
            beta_p[b, c] = pack([beta[b, c, h] for h in range(A_HEADS)])

    def prepare(bs):
        groups_b = [(b, c) for b in bs for c in range(nch)]
        beta, g_col, g_last, eg, dec_p, beta_p = {}, {}, {}, {}, {}, {}
        for b in bs:
            decay_terms(b, beta, g_col, g_last, eg, dec_p, beta_p)

        nmat = {}
        for (b, c) in groups_b:
            k16 = k_ref[b, rows(c), :].astype(BF16)
            q16 = q_ref[b, rows(c), :].astype(BF16)
            k_heads = jnp.concatenate(
                [jnp.concatenate([k16[:, lanes(h)] if hh == h else zero64 for hh in range(A_HEADS)], axis=-1)
                 for h in range(A_HEADS)], axis=0)
            kq = _dot_nt(jnp.concatenate([k16, q16], axis=0), k_heads)
            nmat[b, c] = -(beta_p[b, c] * kq[:CHUNK] * jnp.where(ti_p > ii_p, dec_p[b, c], 0.0))
            akd_s[wslot, b * nch + c, 0:CHUNK, :] = (kq[CHUNK:] * dec_p[b, c]).astype(BF16)
        stage_done()

        rsum = dict(nmat)
        pw16 = {g_: nmat[g_].astype(BF16) for g_ in groups_b}
        pw = {g_: _dot(pw16[g_], block_diag(pw16[g_])) for g_ in groups_b}
        stage_done()
        for step in range(1, 6):
            last = step == 5
            pw16 = {g_: pw[g_].astype(BF16) for g_ in groups_b}
            rp = {}
            for g_ in groups_b:
                r16 = rsum[g_].astype(BF16)
                rp[g_] = _dot(r16 if last else jnp.concatenate([r16, pw16[g_]], axis=0), block_diag(pw16[g_]))
            for g_ in groups_b:
                rsum[g_] = rsum[g_] + pw[g_] + rp[g_][:CHUNK]
                if not last:
                    pw[g_] = rp[g_][CHUNK:]
            stage_done()

        for (b, c) in groups_b:
            for h in range(A_HEADS):
                u_ = (b, c, h)
                i = uid[u_]
                kc = k_ref[b, rows(c), lanes(h)]
                rhs = jnp.concatenate([(beta[u_] * eg[u_]) * kc, beta[u_] * v_ref[b, rows(c), lanes(h)]],
                                      axis=-1)
                rhs16 = rhs.astype(BF16)
                rhs_rows = jnp.concatenate([rhs16 if hh == h else zrhs for hh in range(A_HEADS)], axis=0)
                sol = rhs + _dot(rsum[b, c].astype(BF16), rhs_rows)
                wq_s[wslot, i] = jnp.concatenate([sol[:, :A_DK], eg[u_] * q_ref[b, rows(c), lanes(h)]],
                                                 axis=0).astype(BF16)
                ut_s[wslot, i] = sol[:, A_DK:]
                gl_s[wslot, i] = jnp.broadcast_to(jnp.exp(g_last[u_]), (1, A_DV))
        for (b, c) in groups_b:
            kd = [jnp.exp(g_last[b, c, h] - g_col[b, c, h]) * k_ref[b, rows(c), lanes(h)]
                  for h in range(A_HEADS)]
            for p in range(A_HEADS // 2):
                akd_s[wslot, b * nch + c, CHUNK:, p * LANES:(p + 1) * LANES] = (
                    jnp.concatenate([kd[2 * p], kd[2 * p + 1]], axis=0).T.astype(BF16))
        stage_done()

    run_rec(n_rec - 1)
    for b0 in range(0, bsz, DELTA_WAVE):
        prepare(range(b0, b0 + DELTA_WAVE))
    run_rec(0)

    for b in range(bsz):
        for h in range(A_HEADS):
            s_scr[b * A_HEADS + h] = s_cur[b, h]

    @pl.when(t == pl.num_programs(0) - 1)
    def _():
        for b in range(bsz):
            for h in range(A_HEADS):
                st_ref[b, h] = s_cur[b, h]


def _delta(qkv, gb, za, na_row):
    bsz, t, _ = qkv.shape
    ct = DELTA_CT
    nt = t // ct
    n_units = bsz * (ct // CHUNK) * A_HEADS
    prep = lambda w, j=0: pl.BlockSpec((bsz, ct, w), lambda i: (0, jnp.minimum(i, nt - 1), j))
    rec = lambda w: pl.BlockSpec((bsz, ct, w), lambda i: (0, jnp.maximum(i - 1, 0), 0))
    return pl.pallas_call(
        _delta_kernel,
        grid=(nt + 1,),
        in_specs=[prep(A_WIDTH, 0), prep(A_WIDTH, 1), prep(A_WIDTH, 2), prep(LANES), rec(A_WIDTH),
                  pl.BlockSpec((1, A_DV), lambda i: (0, 0))],
        out_specs=[rec(A_WIDTH),
                   pl.BlockSpec((bsz, A_HEADS, A_DK, A_DV), lambda i: (0, 0, 0, 0))],
        out_shape=[jax.ShapeDtypeStruct((bsz, t, A_WIDTH), BF16),
                   jax.ShapeDtypeStruct((bsz, A_HEADS, A_DK, A_DV), F32)],
        scratch_shapes=[pltpu.VMEM((bsz * A_HEADS, A_DK, A_DV), F32),
                        pltpu.VMEM((2, n_units, 2 * CHUNK, A_DK), BF16),
                        pltpu.VMEM((2, n_units, CHUNK, A_DV), F32),
                        pltpu.VMEM((2, n_units // A_HEADS, CHUNK + A_DK, A_HEADS * CHUNK), BF16),
                        pltpu.VMEM((2, n_units, 1, A_DV), F32)],
        compiler_params=pltpu.CompilerParams(dimension_semantics=("arbitrary",),
                                             vmem_limit_bytes=VMEM_LIMIT),
        name="delta",
    )(qkv, qkv, qkv, gb, za, na_row)


def _swa_out_kernel(sink_ref, qb_ref, kc_ref, kp_ref, krc_ref, krp_ref, v0c_ref, v0p_ref, v1c_ref, v1p_ref,
                    zb_ref, oa_ref, x_ref, gate_ref, w_ref, g_ref, b_ref, y_ref):
    n = pl.program_id(1)
    gate = gate_ref[pl.ds(pl.program_id(0), 1), :]
    tq = qb_ref.shape[1]
    blk = WINDOW
    kx = (jnp.concatenate([kp_ref[0], kc_ref[0]], axis=0), jnp.concatenate([krp_ref[0], krc_ref[0]], axis=0))
    vd = (jnp.concatenate([v0p_ref[0], v0c_ref[0]], axis=0), jnp.concatenate([v1p_ref[0], v1c_ref[0]], axis=0))

    a = lax.broadcasted_iota(jnp.int32, (2 * blk, 2 * blk), 0) % blk
    j = lax.broadcasted_iota(jnp.int32, (2 * blk, 2 * blk), 1)
    rel = a + blk - j
    band = (rel >= 0) & (rel <= WINDOW)
    band_first = band & ((n > 0) | (j >= blk))
    top = lax.broadcasted_iota(jnp.int32, (2 * blk, 1), 0) < blk
    low = _lane((blk, LANES)) < B_HD
    zero = jnp.zeros((blk, LANES), BF16)

    qrows = lambda i: slice(i * blk, (i + 1) * blk)
    krows = lambda i: slice(i * blk, (i + 2) * blk)
    sink = {(kh, half): jnp.where(top, sink_ref[kh * B_GROUP + half] * LOG2E,
                                  sink_ref[kh * B_GROUP + half + 2] * LOG2E)
            for kh in range(B_KV_HEADS) for half in range(2)}
    def wave_units(i0):
        blocks = range(i0, i0 + SWA_WAVE)
        return blocks, [(i, kh, half) for i in blocks for kh in range(B_KV_HEADS) for half in range(2)]

    def score_matmuls(i0):
        blocks, units = wave_units(i0)
        mix_a = {i: _dot(oa_ref[0, qrows(i), :], w_ref[0:A_WIDTH, :]) for i in blocks}
        sc = {}
        for (i, kh, half) in units:
            qs = []
            for g in range(2):
                grp = kh * 2 + g
                xg = qb_ref[0, qrows(i), grp * LANES:(grp + 1) * LANES]
                qs.append(jnp.where(low if half == 0 else jnp.logical_not(low), xg, zero))
            qz = jnp.concatenate(qs, axis=0)
            sc[i, kh, half] = _dot_nt(qz, kx[0 if kh == half else 1][krows(i)])
        return mix_a, sc

    def finish(i0, mix_a, sc):
        blocks, units = wave_units(i0)
        p, den = {}, {}
        for u_ in units:
            i, kh, half = u_
            s_m = jnp.where(band_first if i == 0 else band, sc[u_], -jnp.inf)
            m = jnp.maximum(jnp.max(s_m, axis=-1, keepdims=True), sink[kh, half])
            e = jnp.exp2(s_m - m)
            den[u_] = jnp.sum(e, axis=-1, keepdims=True) + jnp.exp2(sink[kh, half] - m)
            p[u_] = e.astype(BF16)
        pv = {u_: _dot(p[u_], vd[u_[1]][krows(u_[0])]) for u_ in units}
        outs = {u_: pv[u_] / den[u_] for u_ in units}
        for i in blocks:
            ob = []
            for grp in range(B_WIDTH // LANES):
                kh, g = grp // 2, grp % 2
                og = jnp.where(low, outs[i, kh, 0][g * blk:(g + 1) * blk], outs[i, kh, 1][g * blk:(g + 1) * blk])
                ob.append((og * _silu(zb_ref[0, qrows(i), grp * LANES:(grp + 1) * LANES])).astype(BF16))
            mix = mix_a[i] + _dot(jnp.concatenate(ob, axis=-1), w_ref[A_WIDTH:MIX_WIDTH, :])
            r = DEEPNORM_ALPHA * x_ref[0, qrows(i), :] + (1.0 + gate) * mix
            y_ref[0, qrows(i), :] = _layer_norm(r, g_ref[...], b_ref[...])

    starts = list(range(0, tq // blk, SWA_WAVE))
    pending = score_matmuls(starts[0])
    for w, i0 in enumerate(starts):
        ready = pending
        if w + 1 < len(starts):
            pending = score_matmuls(starts[w + 1])
        finish(i0, *ready)


def _swa_out(sinks, qb, kvl, zb, oa, x, mod, mod_row0, w_out, ln_g, ln_b):
    bsz, t, _ = qb.shape
    tq = SWA_TQ
    per = tq // WINDOW
    cur = lambda w: pl.BlockSpec((1, tq, w), lambda b, i: (b, i, 0))
    kv_cur = lambda j: pl.BlockSpec((1, tq, LANES), lambda b, i: (b, i, j))
    kv_prev = lambda j: pl.BlockSpec((1, WINDOW, LANES), lambda b, i: (b, jnp.maximum(i * per - 1, 0), j))
    const2 = lambda s: pl.BlockSpec(s, lambda b, i: (0, 0))
    return pl.pallas_call(
        _swa_out_kernel,
        grid=(bsz, t // tq),
        in_specs=[pl.BlockSpec(memory_space=pltpu.SMEM), cur(B_WIDTH),
                  kv_cur(0), kv_prev(0), kv_cur(1), kv_prev(1), kv_cur(2), kv_prev(2), kv_cur(3), kv_prev(3),
                  cur(B_WIDTH), cur(A_WIDTH), cur(D_MODEL),
                  pl.BlockSpec((8, D_MODEL), lambda b, i: (mod_row0 // 8, 2)),
                  const2((MIX_WIDTH, D_MODEL)), const2((1, D_MODEL)), const2((1, D_MODEL))],
        out_specs=cur(D_MODEL),
        out_shape=jax.ShapeDtypeStruct((bsz, t, D_MODEL), F32),
        compiler_params=pltpu.CompilerParams(dimension_semantics=("arbitrary", "arbitrary"),
                                             vmem_limit_bytes=VMEM_LIMIT),
        name="swa_out",
    )(sinks, qb, kvl, kvl, kvl, kvl, kvl, kvl, kvl, kvl, zb, oa, x, mod, w_out, ln_g, ln_b)


def _out_kernel(oa_ref, ob_ref, x_ref, gate_ref, w_ref, g_ref, b_ref, y_ref):
    mix = _dot(oa_ref[...], w_ref[0:A_WIDTH, :]) + _dot(ob_ref[...], w_ref[A_WIDTH:MIX_WIDTH, :])
    r = DEEPNORM_ALPHA * x_ref[...] + (1.0 + gate_ref[...]) * mix
    y_ref[...] = _layer_norm(r, g_ref[...], b_ref[...])


def _out(oa, ob, x, mod, w_out, ln_g, ln_b):
    n = x.shape[0]
    full = lambda s: pl.BlockSpec(s, lambda i: (0, 0))
    return pl.pallas_call(
        _out_kernel,
        grid=(1,),
        in_specs=[full((n, A_WIDTH)), full((n, B_WIDTH)), full((n, D_MODEL)),
                  pl.BlockSpec((n, D_MODEL), lambda i: (0, 2)),
                  full((MIX_WIDTH, D_MODEL)), full((1, D_MODEL)), full((1, D_MODEL))],
        out_specs=full((n, D_MODEL)),
        out_shape=jax.ShapeDtypeStruct((n, D_MODEL), F32),
        compiler_params=pltpu.CompilerParams(dimension_semantics=("arbitrary",),
                                             vmem_limit_bytes=VMEM_LIMIT),
        name="out",
    )(oa, ob, x, mod, w_out, ln_g, ln_b)


def _sproj_kernel(x_ref, mod_ref, w_ref, cw_ref, cst_ref, alog_ref, dt_ref, cos_ref, sin_ref,
                  q_ref, k_ref, v_ref, za_ref, gb_ref, qb_ref, kb_ref, vb_ref, zb_ref, ncs_ref):
    shift = mod_ref[:, 0:D_MODEL]
    scale = mod_ref[:, D_MODEL:2 * D_MODEL]
    h = (x_ref[...] * (1.0 + scale) + shift).astype(BF16)

    for gi, o_ref in enumerate((q_ref, k_ref, v_ref)):
        c0 = gi * A_WIDTH
        cs = slice(c0, c0 + A_WIDTH)
        u = _dot(h, w_ref[:, cs])
        acc = cst_ref[0, :, cs] * cw_ref[0:1, cs]
        acc = acc + cst_ref[1, :, cs] * cw_ref[1:2, cs]
        acc = acc + cst_ref[2, :, cs] * cw_ref[2:3, cs]
        acc = acc + u * cw_ref[3:4, cs]
        y = _silu(acc)
        if gi == 0:
            y = _l2norm_heads(y, A_DK ** -0.5)
        elif gi == 1:
            y = _l2norm_heads(y, 1.0)
        o_ref[...] = y
        ncs_ref[0, :, cs] = cst_ref[1, :, cs]
        ncs_ref[1, :, cs] = cst_ref[2, :, cs]
        ncs_ref[2, :, cs] = u

    za_ref[...] = _dot(h, w_ref[:, C_ZA:C_ZA + A_WIDTH])
    gb_ref[...] = _gate_lanes(_dot(h, w_ref[:, C_BD:C_BD + LANES]), alog_ref[...], dt_ref[...])

    cos = cos_ref[...]
    sin = sin_ref[...]
    uq = _dot(h, w_ref[:, C_QB:C_QB + B_WIDTH])
    for g in range(B_WIDTH // LANES):
        qb_ref[:, g * LANES:(g + 1) * LANES] = (
            _rotary_group(uq[:, g * LANES:(g + 1) * LANES], cos, sin) * (B_HD ** -0.5))
    kb_ref[...] = _rotary_group(_dot(h, w_ref[:, C_KB:C_KB + LANES]), cos, sin)
    vb_ref[...] = _dot(h, w_ref[:, C_VB:C_VB + LANES])
    zb_ref[...] = _dot(h, w_ref[:, C_ZB:C_ZB + B_WIDTH])


def _sproj(x, mod_s, w_r, conv_w, cst, alog_row, dt_row, cos_row, sin_row):
    n = x.shape[0]
    full = lambda s: pl.BlockSpec(s, lambda i: (0,) * len(s))
    wide = lambda w: jax.ShapeDtypeStruct((n, w), F32)
    return pl.pallas_call(
        _sproj_kernel,
        grid=(1,),
        in_specs=[full((n, D_MODEL)), pl.BlockSpec((n, 3 * D_MODEL), lambda i: (0, 0)),
                  full((D_MODEL, W_COLS)),
                  full((CONV_W, A_QKV)), full((CONV_W - 1, n, A_QKV)),
                  full((1, LANES)), full((1, LANES)), full((1, LANES)), full((1, LANES))],
        out_specs=[full((n, A_WIDTH)), full((n, A_WIDTH)), full((n, A_WIDTH)), full((n, A_WIDTH)),
                   full((n, LANES)), full((n, B_WIDTH)), full((n, LANES)), full((n, LANES)),
                   full((n, B_WIDTH)), full((CONV_W - 1, n, A_QKV))],
        out_shape=[wide(A_WIDTH), wide(A_WIDTH), wide(A_WIDTH), wide(A_WIDTH), wide(LANES),
                   wide(B_WIDTH), wide(LANES), wide(LANES), wide(B_WIDTH),
                   jax.ShapeDtypeStruct((CONV_W - 1, n, A_QKV), F32)],
        compiler_params=pltpu.CompilerParams(dimension_semantics=("arbitrary",),
                                             vmem_limit_bytes=VMEM_LIMIT),
        name="sproj",
    )(x, mod_s, w_r, conv_w, cst, alog_row, dt_row, cos_row, sin_row)


def _sstep_kernel(sink_ref, q_ref, k_ref, v_ref, gb_ref, za_ref, na_ref, st_ref,
                  qb_ref, kn_ref, vn_ref, zb_ref, ck_ref, cv_ref,
                  oa_ref, ob_ref, nst_ref, nck_ref, ncv_ref,
                  o_scr, ob_scr):
    bt = q_ref.shape[0]
    gbv = gb_ref[...]

    pick = (lax.broadcasted_iota(jnp.int32, (bt, bt * A_DV), 1) // A_DV
            == lax.broadcasted_iota(jnp.int32, (bt, bt * A_DV), 0))
    pick = jnp.where(pick, 1.0, 0.0).astype(BF16)
    for h in range(A_HEADS):
        hs = slice(h * A_DK, (h + 1) * A_DK)
        q_rep = _dot(q_ref[:, hs].T.astype(BF16), pick)
        k_rep = _dot(k_ref[:, hs].T.astype(BF16), pick)
        for bb in range(bt):
            eg = jnp.exp(gbv[bb:bb + 1, A_HEADS + h:A_HEADS + h + 1])
            beta = gbv[bb:bb + 1, h:h + 1]
            kcol = k_rep[:, bb * A_DV:(bb + 1) * A_DV]
            qcol = q_rep[:, bb * A_DV:(bb + 1) * A_DV]
            s1 = eg * st_ref[bb, h]
            pred = jnp.sum(kcol * s1, axis=0, keepdims=True)
            upd = beta * (v_ref[bb:bb + 1, hs] - pred)
            s2 = s1 + kcol * upd
            nst_ref[bb, h] = s2
            o_scr[bb:bb + 1, hs] = jnp.sum(qcol * s2, axis=0, keepdims=True)
    na = na_ref[...]
    for h in range(A_HEADS):
        hs = slice(h * A_DK, (h + 1) * A_DK)
        o = o_scr[:, hs]
        on = o * lax.rsqrt(jnp.mean(o * o, axis=-1, keepdims=True) + RMS_EPS) * na
        oa_ref[:, hs] = (on * _silu(za_ref[:, hs])).astype(BF16)

    row8 = lax.broadcasted_iota(jnp.int32, (B_HEADS, LANES), 0)
    lane8 = _lane((B_HEADS, LANES))
    own_half = (lane8 >= B_HD) == (row8 >= B_GROUP)
    rcol = lax.broadcasted_iota(jnp.int32, (B_HEADS, 1), 0)
    sink = jnp.zeros((B_HEADS, 1), F32)
    for r in range(B_HEADS):
        sink = jnp.where(rcol == r, sink_ref[r], sink)
    qv = qb_ref[...]
    qv_r = jnp.concatenate([pltpu.roll(qv[:, g * LANES:(g + 1) * LANES], B_HD, axis=1)
                            for g in range(B_WIDTH // LANES)], axis=-1)
    kn_t = kn_ref[...].T
    vn_t = vn_ref[...].T
    newest = _lane((LANES, WINDOW)) == WINDOW - 1
    qzs, scs = [], []
    for bb in range(bt):
        qz = jnp.zeros((B_HEADS, LANES), F32)
        for r in range(B_HEADS):
            grp, half, kh = r // 2, r % 2, r // B_GROUP
            src = qv if half == kh else qv_r
            qz = jnp.where(row8 == r, src[bb:bb + 1, grp * LANES:(grp + 1) * LANES], qz)
        qzs.append(jnp.where(own_half, qz, 0.0))
    for bb in range(bt):
        scs.append(_dot(qzs[bb], ck_ref[bb]))
    ps, pnews, dens = [], [], []
    for bb in range(bt):
        sc_new = jnp.sum(qzs[bb] * kn_ref[bb:bb + 1, :], axis=-1, keepdims=True)
        m = jnp.maximum(jnp.maximum(jnp.max(scs[bb], axis=-1, keepdims=True), sc_new), sink)
        p = jnp.exp(scs[bb] - m)
        p_new = jnp.exp(sc_new - m)
        ps.append(p)
        pnews.append(p_new)
        dens.append(jnp.sum(p, axis=-1, keepdims=True) + p_new + jnp.exp(sink - m))
    pvs = [_dot_nt(ps[bb], cv_ref[bb]) for bb in range(bt)]
    for bb in range(bt):
        o = (pvs[bb] + pnews[bb] * vn_ref[bb:bb + 1, :]) / dens[bb]
        o = jnp.where(own_half, o, 0.0)
        ob_scr[bb * B_HEADS:(bb + 1) * B_HEADS, :] = o + pltpu.roll(o, B_HD, axis=1)
    for bb in range(bt):
        nck_ref[bb] = jnp.where(newest, kn_t[:, bb:bb + 1], pltpu.roll(ck_ref[bb], WINDOW - 1, axis=1))
        ncv_ref[bb] = jnp.where(newest, vn_t[:, bb:bb + 1], pltpu.roll(cv_ref[bb], WINDOW - 1, axis=1))
    low = _lane((bt, LANES)) < B_HD
    for grp in range(B_WIDTH // LANES):
        even = ob_scr[pl.ds(2 * grp, bt, stride=B_HEADS), :]
        odd = ob_scr[pl.ds(2 * grp + 1, bt, stride=B_HEADS), :]
        gs = slice(grp * LANES, (grp + 1) * LANES)
        ob_ref[:, gs] = (jnp.where(low, even, odd) * _silu(zb_ref[:, gs])).astype(BF16)


def _sstep(sinks, q, k, v, gb, za, na_row, state, qb, kn, vn, zb, ck, cv):
    n = q.shape[0]
    bt = STEP_BT
    row = lambda w: pl.BlockSpec((bt, w), lambda i: (i, 0))
    st_spec = pl.BlockSpec((bt, A_HEADS, A_DK, A_DV), lambda i: (i, 0, 0, 0))
    c_spec = pl.BlockSpec((bt, WINDOW, LANES), lambda i: (i, 0, 0))
    return pl.pallas_call(
        _sstep_kernel,
        grid=(n // bt,),
        in_specs=[pl.BlockSpec(memory_space=pltpu.SMEM),
                  row(A_WIDTH), row(A_WIDTH), row(A_WIDTH), row(LANES), row(A_WIDTH),
                  pl.BlockSpec((1, A_DV), lambda i: (0, 0)), st_spec,
                  row(B_WIDTH), row(LANES), row(LANES), row(B_WIDTH), c_spec, c_spec],
        out_specs=[row(A_WIDTH), row(B_WIDTH), st_spec, c_spec, c_spec],
        out_shape=[jax.ShapeDtypeStruct((n, A_WIDTH), BF16),
                   jax.ShapeDtypeStruct((n, B_WIDTH), BF16),
                   jax.ShapeDtypeStruct((n, A_HEADS, A_DK, A_DV), F32),
                   jax.ShapeDtypeStruct((n, WINDOW, LANES), F32),
                   jax.ShapeDtypeStruct((n, WINDOW, LANES), F32)],
        scratch_shapes=[pltpu.VMEM((bt, A_WIDTH), F32), pltpu.VMEM((bt * B_HEADS, LANES), F32)],
        compiler_params=pltpu.CompilerParams(dimension_semantics=("arbitrary",),
                                             vmem_limit_bytes=VMEM_LIMIT),
        name="sstep",
    )(sinks, q, k, v, gb, za, na_row, state, qb, kn, vn, zb, ck, cv)


def _rope_tables(pos):
    half = B_HD // 2
    inv = 1.0 / (ROPE_THETA ** (np.arange(half, dtype=np.float64) / half))
    ang = np.asarray(pos, np.float64)[:, None] * inv[None, :]
    cos, sin = np.cos(ang), np.sin(ang)
    reps = LANES // B_HD
    return (jnp.asarray(np.tile(np.concatenate([cos, cos], -1), (1, reps)), F32),
            jnp.asarray(np.tile(np.concatenate([-sin, sin], -1), (1, reps)), F32))


def _pad_row(vec, offset):
    return jnp.pad(vec.astype(F32).reshape(1, -1), ((0, 0), (offset, LANES - offset - vec.shape[0])))


def _layer(x_prompt, x_sample, state_conv, state_delta, cache_k, cache_v, c_prompt, c_sample,
           w_ada, b_ada, w_in, conv_w, a_log, dt_bias, norm_a, sinks, w_out, ln_g, ln_b):
    bsz, seq, _ = x_prompt.shape
    n_s = x_sample.shape[0]

    w_r = _wprep(jnp.swapaxes(w_in, 0, 1))
    w_o = w_out.astype(BF16)
    alog_row = _pad_row(a_log, A_HEADS)
    dt_row = _pad_row(dt_bias, A_HEADS)
    na_row = norm_a.reshape(1, A_DV)
    g_row = ln_g.reshape(1, D_MODEL)
    b_row = ln_b.reshape(1, D_MODEL)

    assert n_s % 8 == 0 and bsz <= 8
    mod = _ada(c_sample, c_prompt, w_ada, b_ada.reshape(1, 3 * D_MODEL))

    cos_p, sin_p = _rope_tables(np.arange(seq))
    (qkv, za, gb, qb, kvl, zb, conv_p, kb_last, vb_last) = _proj(
        x_prompt, mod, n_s, w_r, conv_w, alog_row, dt_row, cos_p, sin_p)
    oa, delta_p = _delta(qkv, gb, za, na_row)
    y_p = _swa_out(sinks, qb, kvl, zb, oa, x_prompt, mod, n_s, w_o, g_row, b_row)
    swa_k_p = kb_last.reshape(bsz, WINDOW, B_KV_HEADS, B_HD)
    swa_v_p = vb_last.reshape(bsz, WINDOW, B_KV_HEADS, B_HD)

    cos_s, sin_s = _rope_tables(np.array([PAST_LEN]))
    xs = x_sample.reshape(n_s, D_MODEL)
    cst = jnp.transpose(state_conv, (1, 0, 2))
    sq, sk, sv, sza, sgb, sqb, skn, svn, szb, ncs = _sproj(xs, mod, w_r, conv_w, cst, alog_row, dt_row,
                                                           cos_s, sin_s)
    soa, sob, delta_s, nck, ncv = _sstep(sinks, sq, sk, sv, sgb, sza, na_row, state_delta,
                                         sqb, skn, svn, szb,
                                         jnp.swapaxes(cache_k.reshape(n_s, WINDOW, LANES), 1, 2),
                                         jnp.swapaxes(cache_v.reshape(n_s, WINDOW, LANES), 1, 2))
    y_s = _out(soa, sob, xs, mod, w_o, g_row, b_row)
    conv_s = jnp.transpose(ncs, (1, 0, 2))
    unpack = lambda c: jnp.swapaxes(c, 1, 2).reshape(n_s, WINDOW, B_KV_HEADS, B_HD)
    return (y_p, y_s.reshape(n_s, 1, D_MODEL), conv_p, delta_p, swa_k_p, swa_v_p,
            conv_s, delta_s, unpack(nck), unpack(ncv))


def kernel(x_prompt, x_sample, state_conv, state_delta, cache_swa_k, cache_swa_v, c_prompt, c_sample,
           w_ada, b_ada, w_in, conv_w, a_log, dt_bias, norm_a, sinks, w_out, ln_g, ln_b):
    assert w_ada.shape[0] == DEPTH == 1
    outs = _layer(x_prompt, x_sample, state_conv[0], state_delta[0], cache_swa_k[0], cache_swa_v[0],
                  c_prompt, c_sample, w_ada[0], b_ada[0], w_in[0], conv_w[0], a_log[0], dt_bias[0],
                  norm_a[0], sinks[0], w_out[0], ln_g[0], ln_b[0])
    y_p, y_s = outs[0], outs[1]
    return (y_p, y_s) + tuple(o[None] for o in outs[2:])
```

```python
import jax
import jax.numpy as jnp
import numpy as np
from jax import lax
from jax.experimental import pallas as pl
from jax.experimental.pallas import tpu as pltpu

F32 = jnp.float32
BF16 = jnp.bfloat16

D_MODEL = 1024
DEPTH = 1
PAST_LEN = 8192
A_HEADS = 4
A_DK = 128
A_DV = 128
A_WIDTH = A_HEADS * A_DV
A_QKV = 3 * A_WIDTH
CONV_W = 4
CHUNK = 64
B_HEADS = 8
B_KV_HEADS = 2
B_HD = 64
B_GROUP = B_HEADS // B_KV_HEADS
B_WIDTH = B_HEADS * B_HD
B_KV_WIDTH = B_KV_HEADS * B_HD
WINDOW = 128
ROPE_THETA = 10000.0
MIX_WIDTH = A_WIDTH + B_WIDTH
DEEPNORM_ALPHA = (2 * DEPTH) ** 0.25
LOG2E = 1.4426950408889634
LN_EPS = 1e-5
RMS_EPS = 1e-6
L2_EPS = 1e-6

OFF_A_Z = A_QKV
OFF_A_BETA = OFF_A_Z + A_WIDTH
OFF_A_DECAY = OFF_A_BETA + A_HEADS
OFF_B_Q = OFF_A_DECAY + A_HEADS
OFF_B_K = OFF_B_Q + B_WIDTH
OFF_B_V = OFF_B_K + B_KV_WIDTH
OFF_B_Z = OFF_B_V + B_KV_WIDTH
PROJ_COLS = OFF_B_Z + B_WIDTH

LANES = 128
C_QKV = 0
C_ZA = C_QKV + A_QKV
C_QB = C_ZA + A_WIDTH
C_KB = C_QB + B_WIDTH
C_VB = C_KB + B_KV_WIDTH
C_ZB = C_VB + B_KV_WIDTH
C_BD = C_ZB + B_WIDTH
WPREP_TN = 256
W_COLS = C_BD + WPREP_TN

VMEM_LIMIT = 56 * 1024 * 1024

ADA_TN = 1536
PROJ_TM = 512
PROJ_CW = 256
PROJ_PART_ROWS = (128, 256, 128)
DELTA_CT = 256
DELTA_WAVE = 2
SWA_TQ = 512
SWA_WAVE = 2
STEP_BT = 16


def _dot(a, b):
    return jnp.dot(a, b, preferred_element_type=F32)


def _dot_nt(a, b):
    return lax.dot_general(a, b, (((1,), (1,)), ((), ())), preferred_element_type=F32)


def _silu(x):
    return x * jax.nn.sigmoid(x)


def _softplus(x):
    return jnp.maximum(x, 0.0) + jnp.log1p(jnp.exp(-jnp.abs(x)))


def _lane(shape):
    return lax.broadcasted_iota(jnp.int32, shape, len(shape) - 1)


def _l2norm_heads(y, scale):
    outs = []
    for h in range(y.shape[1] // A_DK):
        xh = y[:, h * A_DK:(h + 1) * A_DK]
        ss = jnp.sum(xh * xh, axis=-1, keepdims=True)
        xn = xh * lax.rsqrt(ss + L2_EPS)
        outs.append(xn * scale if scale != 1.0 else xn)
    return jnp.concatenate(outs, axis=-1)


def _rotary_group(xg, cos, sin_signed):
    lane = _lane(xg.shape)
    swapped = jnp.where((lane % B_HD) < (B_HD // 2),
                        pltpu.roll(xg, LANES - B_HD // 2, axis=1),
                        pltpu.roll(xg, B_HD // 2, axis=1))
    return xg * cos + swapped * sin_signed


def _kv_layouts(kb, vb):
    low = _lane(kb.shape) < B_HD
    kbr = pltpu.roll(kb, B_HD, axis=1)
    vbr = pltpu.roll(vb, B_HD, axis=1)
    return kb, kbr, jnp.where(low, vb, vbr), jnp.where(low, vbr, vb)


def _gate_lanes(bd, alog_row, dt_row):
    lane = _lane(bd.shape)
    g = -jnp.exp(alog_row) * _softplus(bd + dt_row)
    return jnp.where(lane < A_HEADS, jax.nn.sigmoid(bd), g)


def _layer_norm(r, g, b):
    mu = jnp.mean(r, axis=-1, keepdims=True)
    d = r - mu
    var = jnp.mean(d * d, axis=-1, keepdims=True)
    return d * lax.rsqrt(var + LN_EPS) * g + b


def _wprep_kernel(wa_ref, wb_ref, o_ref):
    tn = wa_ref.shape[0]
    o_ref[:, 0:tn] = wa_ref[...].T.astype(BF16)
    xb = wb_ref[...]
    tail = pl.program_id(0) == pl.num_programs(0) - 1
    row = lax.broadcasted_iota(jnp.int32, xb.shape, 0)
    xb = jnp.where(jnp.logical_and(tail, row >= 2 * A_HEADS), 0.0, xb)
    o_ref[:, tn:2 * tn] = xb.T.astype(BF16)


def _wprep(w_t):
    tn = WPREP_TN
    n_a, n_b = OFF_A_BETA // tn, (PROJ_COLS - OFF_B_Q) // tn
    assert n_a * tn == OFF_A_BETA and n_b * tn == PROJ_COLS - OFF_B_Q and OFF_A_BETA + tn <= PROJ_COLS

    assert (n_a + n_b + 1) % 2 == 0

    def src_row(j):
        return jnp.where(j < n_a, j * tn, jnp.where(j < n_a + n_b, OFF_B_Q + (j - n_a) * tn, OFF_A_BETA))

    src = lambda k: pl.BlockSpec((pl.Element(tn), pl.Element(D_MODEL)),
                                 lambda j: (pl.multiple_of(src_row(2 * j + k), 8), 0))
    return pl.pallas_call(
        _wprep_kernel,
        grid=((n_a + n_b + 1) // 2,),
        in_specs=[src(0), src(1)],
        out_specs=pl.BlockSpec((D_MODEL, 2 * tn), lambda j: (0, j)),
        out_shape=jax.ShapeDtypeStruct((D_MODEL, W_COLS), BF16),
        compiler_params=pltpu.CompilerParams(dimension_semantics=("arbitrary",),
                                             vmem_limit_bytes=VMEM_LIMIT),
        name="wprep",
    )(w_t, w_t)


def _ada_kernel(cs_ref, cp_ref, w_ref, b_ref, o_ref):
    n_s, n_p = cs_ref.shape[0], cp_ref.shape[0]
    w = w_ref[...].astype(BF16)
    o_ref[0:n_s, :] = _dot(cs_ref[...].astype(BF16), w) + b_ref[...]
    o_ref[n_s:n_s + 8, :] = jnp.zeros((8, o_ref.shape[1]), F32)
    o_ref[n_s:n_s + n_p, :] = _dot(cp_ref[...].astype(BF16), w) + b_ref[...]


def _ada(c_sample, c_prompt, w_ada, b_ada):
    n_s, n_p = c_sample.shape[0], c_prompt.shape[0]
    rows = n_s + 8
    tn = ADA_TN
    return pl.pallas_call(
        _ada_kernel,
        grid=(3 * D_MODEL // tn,),
        in_specs=[pl.BlockSpec((n_s, D_MODEL), lambda j: (0, 0)),
                  pl.BlockSpec((n_p, D_MODEL), lambda j: (0, 0)),
                  pl.BlockSpec((D_MODEL, tn), lambda j: (0, j)),
                  pl.BlockSpec((1, tn), lambda j: (0, j))],
        out_specs=pl.BlockSpec((rows, tn), lambda j: (0, j)),
        out_shape=jax.ShapeDtypeStruct((rows, 3 * D_MODEL), F32),
        compiler_params=pltpu.CompilerParams(dimension_semantics=("arbitrary",),
                                             vmem_limit_bytes=VMEM_LIMIT),
        name="ada",
    )(c_sample, c_prompt, w_ada, b_ada)


def _proj_kernel(x_ref, mod_ref, w_ref, cw_ref, alog_ref, dt_ref, cos_ref, sin_ref,
                 qkv_ref, za_ref, gb_ref, qb_ref, kvl_ref, zb_ref, cst_ref, kbl_ref, vbl_ref, ubuf):
    tm = x_ref.shape[1]
    t = pl.program_id(1)

    @pl.when(t == 0)
    def _():
        ubuf[...] = jnp.zeros(ubuf.shape, F32)

    brow = pl.ds(pl.program_id(0), 1)
    shift = mod_ref[brow, 0:D_MODEL]
    scale = mod_ref[brow, D_MODEL:2 * D_MODEL]

    assert sum(PROJ_PART_ROWS) == tm and PROJ_PART_ROWS[-1] >= WINDOW
    cw = PROJ_CW
    pieces = [slice(c0, c0 + cw) for c0 in range(0, A_QKV, cw)]

    def matmuls(r0, rp):
        h = (x_ref[0, r0:r0 + rp, :] * (1.0 + scale) + shift).astype(BF16)
        return ([_dot(h, w_ref[:, cs]) for cs in pieces],
                _dot(h, w_ref[:, C_ZA:C_ZA + A_WIDTH]), _dot(h, w_ref[:, C_BD:C_BD + LANES]),
                _dot(h, w_ref[:, C_QB:C_QB + B_WIDTH]), _dot(h, w_ref[:, C_KB:C_KB + 2 * LANES]),
                _dot(h, w_ref[:, C_ZB:C_ZB + B_WIDTH]))

    def epilogue(r0, rp, results):
        rs = slice(r0, r0 + rp)
        us, za, ubd, uq, ukv, zb = results
        sub = lax.broadcasted_iota(jnp.int32, (rp // 8, 8, cw), 1)

        def conv_epilogue(cs, u):
            gi = cs.start // A_WIDTH
            groups = jnp.concatenate([ubuf[:, cs], u], axis=0).reshape(rp // 8 + 1, 8, cw)
            acc = None
            for j in range(CONV_W - 1, 0, -1):
                rot = pltpu.roll(groups, j, axis=1)
                term = (jnp.where(sub < j, rot[:-1], rot[1:]).reshape(rp, cw)
                        * cw_ref[CONV_W - 1 - j:CONV_W - j, cs])
                acc = term if acc is None else acc + term
            y = _silu(acc + u * cw_ref[CONV_W - 1:CONV_W, cs])
            if gi == 0:
                y = _l2norm_heads(y, A_DK ** -0.5)
            elif gi == 1:
                y = _l2norm_heads(y, 1.0)
            qkv_ref[0, rs, cs] = y
            ubuf[:, cs] = u[rp - 8:rp]
            if r0 + rp == tm:
                cst_ref[0, :, cs] = u[rp - (CONV_W - 1):rp]

        for cs, u in zip(pieces, us):
            conv_epilogue(cs, u)
        za_ref[0, rs, :] = za
        zb_ref[0, rs, :] = zb
        gb_ref[0, rs, :] = _gate_lanes(ubd, alog_ref[...], dt_ref[...])
        cos = cos_ref[rs, :]
        sin = sin_ref[rs, :]
        for g in range(B_WIDTH // LANES):
            qb_ref[0, rs, g * LANES:(g + 1) * LANES] = (
                _rotary_group(uq[:, g * LANES:(g + 1) * LANES], cos, sin) * (B_HD ** -0.5 * LOG2E)).astype(BF16)
        kb = _rotary_group(ukv[:, 0:LANES], cos, sin)
        vb = ukv[:, LANES:2 * LANES]
        for j, val in enumerate(_kv_layouts(kb, vb)):
            kvl_ref[0, rs, j * LANES:(j + 1) * LANES] = val.astype(BF16)
        return kb, vb

    parts, r0 = [], 0
    for rp in PROJ_PART_ROWS:
        parts.append((r0, rp))
        r0 += rp
    pending = matmuls(*parts[0])
    for p, part in enumerate(parts):
        results = pending
        if p + 1 < len(parts):
            pending = matmuls(*parts[p + 1])
        kb, vb = epilogue(*part, results)

    @pl.when(t == pl.num_programs(1) - 1)
    def _():
        kbl_ref[0] = kb[kb.shape[0] - WINDOW:, :]
        vbl_ref[0] = vb[vb.shape[0] - WINDOW:, :]


def _proj(x, mod, mod_row0, w_r, conv_w, alog_row, dt_row, cos_t, sin_t):
    bsz, t, _ = x.shape
    tm = PROJ_TM
    row = lambda w: pl.BlockSpec((1, tm, w), lambda b, i: (b, i, 0))
    const2 = lambda s: pl.BlockSpec(s, lambda b, i: (0, 0))
    per_b = lambda r, w: pl.BlockSpec((1, r, w), lambda b, i: (b, 0, 0))
    wide = lambda w, dt=F32: jax.ShapeDtypeStruct((bsz, t, w), dt)
    return pl.pallas_call(
        _proj_kernel,
        grid=(bsz, t // tm),
        in_specs=[row(D_MODEL),
                  pl.BlockSpec((8, 3 * D_MODEL), lambda b, i: (mod_row0 // 8, 0)),
                  const2((D_MODEL, W_COLS)),
                  const2((CONV_W, A_QKV)),
                  const2((1, LANES)), const2((1, LANES)),
                  pl.BlockSpec((tm, LANES), lambda b, i: (i, 0)),
                  pl.BlockSpec((tm, LANES), lambda b, i: (i, 0))],
        out_specs=[row(A_QKV), row(A_WIDTH), row(LANES), row(B_WIDTH), row(4 * LANES), row(B_WIDTH),
                   per_b(CONV_W - 1, A_QKV), per_b(WINDOW, LANES), per_b(WINDOW, LANES)],
        out_shape=[wide(A_QKV), wide(A_WIDTH), wide(LANES), wide(B_WIDTH, BF16), wide(4 * LANES, BF16),
                   wide(B_WIDTH),
                   jax.ShapeDtypeStruct((bsz, CONV_W - 1, A_QKV), F32),
                   jax.ShapeDtypeStruct((bsz, WINDOW, LANES), F32),
                   jax.ShapeDtypeStruct((bsz, WINDOW, LANES), F32)],
        scratch_shapes=[pltpu.VMEM((8, A_QKV), F32)],
        compiler_params=pltpu.CompilerParams(dimension_semantics=("arbitrary", "arbitrary"),
                                             vmem_limit_bytes=VMEM_LIMIT),
        name="proj",
    )(x, mod, w_r, conv_w, alog_row, dt_row, cos_t, sin_t)


def _delta_kernel(q_ref, k_ref, v_ref, gb_ref, za_ref, na_ref, oa_ref, st_ref,
                  s_scr, wq_s, ut_s, akd_s, gl_s):
    bsz, ct = q_ref.shape[0], q_ref.shape[1]
    nch = ct // CHUNK
    t = pl.program_id(0)
    wslot = t % 2
    rslot = 1 - wslot

    @pl.when(t == 0)
    def _():
        s_scr[...] = jnp.zeros(s_scr.shape, F32)
        wq_s[...] = jnp.zeros(wq_s.shape, BF16)
        ut_s[...] = jnp.zeros(ut_s.shape, F32)
        akd_s[...] = jnp.zeros(akd_s.shape, BF16)
        gl_s[...] = jnp.zeros(gl_s.shape, F32)

    units = [(b, c, h) for b in range(bsz) for c in range(nch) for h in range(A_HEADS)]
    uid = {u_: i for i, u_ in enumerate(units)}
    rows = lambda c: slice(c * CHUNK, (c + 1) * CHUNK)
    lanes = lambda h: slice(h * A_DK, (h + 1) * A_DK)
    na = na_ref[...]

    s_cur = {(b, h): s_scr[b * A_HEADS + h] for b in range(bsz) for h in range(A_HEADS)}
    ws, uu = {}, {}

    def rec_ws(c):
        for b in range(bsz):
            for h in range(A_HEADS):
                i = uid[b, c, h]
                ws[b, h] = _dot(wq_s[rslot, i], s_cur[b, h].astype(BF16))
                uu[b, h] = (ut_s[rslot, i] - ws[b, h][:CHUNK]).astype(BF16)

    def rec_ou(c):
        zpad = jnp.zeros((CHUNK, A_DV), BF16)
        for b in range(bsz):
            u_bd = jnp.concatenate(
                [jnp.concatenate([uu[b, h] if hh == h else zpad for hh in range(A_HEADS)], axis=-1)
                 for h in range(A_HEADS)], axis=0)
            ou = _dot(akd_s[rslot, b * nch + c], u_bd)
            for h in range(A_HEADS):
                o = ws[b, h][CHUNK:] + ou[:CHUNK, lanes(h)]
                s_cur[b, h] = gl_s[rslot, uid[b, c, h]] * s_cur[b, h] + ou[CHUNK:, lanes(h)]
                on = o * lax.rsqrt(jnp.mean(o * o, axis=-1, keepdims=True) + RMS_EPS) * na
                oa_ref[b, rows(c), lanes(h)] = (on * _silu(za_ref[b, rows(c), lanes(h)])).astype(BF16)

    rec_stages = []
    for c in range(nch):
        rec_stages += [lambda c=c: rec_ws(c), lambda c=c: rec_ou(c)]

    def run_rec(n_left_after):
        while rec_stages and len(rec_stages) > n_left_after:
            rec_stages.pop(0)()

    pk = A_HEADS * CHUNK
    low = _lane((CHUNK, LANES)) < CHUNK
    low_row = _lane((1, LANES)) < CHUNK
    ti_p = lax.broadcasted_iota(jnp.int32, (CHUNK, pk), 0)
    ii_p = _lane((CHUNK, pk)) % CHUNK
    zero64 = jnp.zeros((CHUNK, LANES), BF16)

    def pack(parts):
        return jnp.concatenate([jnp.where(low, parts[0], parts[1]), jnp.where(low, parts[2], parts[3])], axis=-1)

    def block_diag(x16):
        blocks = []
        for h in range(A_HEADS):
            pair, first = h // 2, h % 2 == 0
            piece = jnp.where(low if first else jnp.logical_not(low), x16[:, pair * LANES:(pair + 1) * LANES], zero64)
            blocks.append(jnp.concatenate([piece, zero64] if pair == 0 else [zero64, piece], axis=-1))
        return jnp.concatenate(blocks, axis=0)

    zrhs = jnp.zeros((CHUNK, 2 * A_DK), BF16)
    n_rec = len(rec_stages)
    n_slots = 8 * (bsz // DELTA_WAVE)
    done = [0]

    def stage_done():
        done[0] += 1
        run_rec(n_rec - 1 - (done[0] * n_rec) // n_slots)

    def decay_terms(b, beta, g_col, g_last, eg, dec_p, beta_p):
        gbv = gb_ref[b]
        rin = lax.broadcasted_iota(jnp.int32, gbv.shape, 0) % CHUNK
        gcs = gbv
        s = 1
        while s < CHUNK:
            gcs = gcs + jnp.where(rin >= s, pltpu.roll(gcs, s, axis=0), 0.0)
            s *= 2
        gcs_t = gcs.T
        for c in range(nch):
            r0 = c * CHUNK
            pair_lanes = slice((c // 2) * LANES, (c // 2 + 1) * LANES)
            g_rows = []
            for h in range(A_HEADS):
                u_ = (b, c, h)
                beta[u_] = jnp.broadcast_to(gbv[rows(c), h:h + 1], (CHUNK, A_DK))
                g_col[u_] = jnp.broadcast_to(gcs[rows(c), A_HEADS + h:A_HEADS + h + 1], (CHUNK, A_DK))
                g_last[u_] = gcs[r0 + CHUNK - 1:r0 + CHUNK, A_HEADS + h:A_HEADS + h + 1]
                eg[u_] = jnp.exp(g_col[u_])
                g_row = gcs_t[A_HEADS + h:A_HEADS + h + 1, pair_lanes]
                g_rows.append(g_row if c % 2 == h % 2 else pltpu.roll(g_row, CHUNK, axis=1))
            g_row_p = jnp.concatenate([jnp.where(low_row, g_rows[0], g_rows[1]),
                                       jnp.where(low_row, g_rows[2], g_rows[3])], axis=-1)
            g_col_p = pack([g_col[b, c, h] for h in range(A_HEADS)])
            dec_p[b, c] = jnp.exp(jnp.where(ti_p >= ii_p, g_col_p - g_row_p, -jnp.inf))
            beta_p[b, c] = pack([beta[b, c, h] for h in range(A_HEADS)])

    def prepare(bs):
        groups_b = [(b, c) for b in bs for c in range(nch)]
        beta, g_col, g_last, eg, dec_p, beta_p = {}, {}, {}, {}, {}, {}
        for b in bs:
            decay_terms(b, beta, g_col, g_last, eg, dec_p, beta_p)

        nmat = {}
        for (b, c) in groups_b:
            k16 = k_ref[b, rows(c), :].astype(BF16)
            q16 = q_ref[b, rows(c), :].astype(BF16)
            k_heads = jnp.concatenate(
                [jnp.concatenate([k16[:, lanes(h)] if hh == h else zero64 for hh in range(A_HEADS)], axis=-1)
                 for h in range(A_HEADS)], axis=0)
            kq = _dot_nt(jnp.concatenate([k16, q16], axis=0), k_heads)
            nmat[b, c] = -(beta_p[b, c] * kq[:CHUNK] * jnp.where(ti_p > ii_p, dec_p[b, c], 0.0))
            akd_s[wslot, b * nch + c, 0:CHUNK, :] = (kq[CHUNK:] * dec_p[b, c]).astype(BF16)
        stage_done()

        rsum = dict(nmat)
        pw16 = {g_: nmat[g_].astype(BF16) for g_ in groups_b}
        pw = {g_: _dot(pw16[g_], block_diag(pw16[g_])) for g_ in groups_b}
        stage_done()
        for step in range(1, 6):
            last = step == 5
            pw16 = {g_: pw[g_].astype(BF16) for g_ in groups_b}
            rp = {}
            for g_ in groups_b:
                r16 = rsum[g_].astype(BF16)
                rp[g_] = _dot(r16 if last else jnp.concatenate([r16, pw16[g_]], axis=0), block_diag(pw16[g_]))
            for g_ in groups_b:
                rsum[g_] = rsum[g_] + pw[g_] + rp[g_][:CHUNK]
                if not last:
                    pw[g_] = rp[g_][CHUNK:]
            stage_done()

        for (b, c) in groups_b:
            for h in range(A_HEADS):
                u_ = (b, c, h)
                i = uid[u_]
                kc = k_ref[b, rows(c), lanes(h)]
                rhs = jnp.concatenate([(beta[u_] * eg[u_]) * kc, beta[u_] * v_ref[b, rows(c), lanes(h)]],
                                      axis=-1)
                rhs16 = rhs.astype(BF16)
                rhs_rows = jnp.concatenate([rhs16 if hh == h else zrhs for hh in range(A_HEADS)], axis=0)
                sol = rhs + _dot(rsum[b, c].astype(BF16), rhs_rows)
                wq_s[wslot, i] = jnp.concatenate([sol[:, :A_DK], eg[u_] * q_ref[b, rows(c), lanes(h)]],
                                                 axis=0).astype(BF16)
                ut_s[wslot, i] = sol[:, A_DK:]
                gl_s[wslot, i] = jnp.broadcast_to(jnp.exp(g_last[u_]), (1, A_DV))
        for (b, c) in groups_b:
            kd = [jnp.exp(g_last[b, c, h] - g_col[b, c, h]) * k_ref[b, rows(c), lanes(h)]
                  for h in range(A_HEADS)]
            for p in range(A_HEADS // 2):
                akd_s[wslot, b * nch + c, CHUNK:, p * LANES:(p + 1) * LANES] = (
                    jnp.concatenate([kd[2 * p], kd[2 * p + 1]], axis=0).T.astype(BF16))
        stage_done()

    run_rec(n_rec - 1)
    for b0 in range(0, bsz, DELTA_WAVE):
        prepare(range(b0, b0 + DELTA_WAVE))
    run_rec(0)

    for b in range(bsz):
        for h in range(A_HEADS):
            s_scr[b * A_HEADS + h] = s_cur[b, h]

    @pl.when(t == pl.num_programs(0) - 1)
    def _():
        for b in range(bsz):
            for h in range(A_HEADS):
                st_ref[b, h] = s_cur[b, h]


def _delta(qkv, gb, za, na_row):
    bsz, t, _ = qkv.shape
    ct = DELTA_CT
    nt = t // ct
    n_units = bsz * (ct // CHUNK) * A_HEADS
    prep = lambda w, j=0: pl.BlockSpec((bsz, ct, w), lambda i: (0, jnp.minimum(i, nt - 1), j))
    rec = lambda w: pl.BlockSpec((bsz, ct, w), lambda i: (0, jnp.maximum(i - 1, 0), 0))
    return pl.pallas_call(
        _delta_kernel,
        grid=(nt + 1,),
        in_specs=[prep(A_WIDTH, 0), prep(A_WIDTH, 1), prep(A_WIDTH, 2), prep(LANES), rec(A_WIDTH),
                  pl.BlockSpec((1, A_DV), lambda i: (0, 0))],
        out_specs=[rec(A_WIDTH),
                   pl.BlockSpec((bsz, A_HEADS, A_DK, A_DV), lambda i: (0, 0, 0, 0))],
        out_shape=[jax.ShapeDtypeStruct((bsz, t, A_WIDTH), BF16),
                   jax.ShapeDtypeStruct((bsz, A_HEADS, A_DK, A_DV), F32)],
        scratch_shapes=[pltpu.VMEM((bsz * A_HEADS, A_DK, A_DV), F32),
                        pltpu.VMEM((2, n_units, 2 * CHUNK, A_DK), BF16),
                        pltpu.VMEM((2, n_units, CHUNK, A_DV), F32),
                        pltpu.VMEM((2, n_units // A_HEADS, CHUNK + A_DK, A_HEADS * CHUNK), BF16),
                        pltpu.VMEM((2, n_units, 1, A_DV), F32)],
        compiler_params=pltpu.CompilerParams(dimension_semantics=("arbitrary",),
                                             vmem_limit_bytes=VMEM_LIMIT),
        name="delta",
    )(qkv, qkv, qkv, gb, za, na_row)


def _swa_out_kernel(sink_ref, qb_ref, kc_ref, kp_ref, krc_ref, krp_ref, v0c_ref, v0p_ref, v1c_ref, v1p_ref,
                    zb_ref, oa_ref, x_ref, gate_ref, w_ref, g_ref, b_ref, y_ref):
    n = pl.program_id(1)
    gate = gate_ref[pl.ds(pl.program_id(0), 1), :]
    tq = qb_ref.shape[1]
    blk = WINDOW
    kx = (jnp.concatenate([kp_ref[0], kc_ref[0]], axis=0), jnp.concatenate([krp_ref[0], krc_ref[0]], axis=0))
    vd = (jnp.concatenate([v0p_ref[0], v0c_ref[0]], axis=0), jnp.concatenate([v1p_ref[0], v1c_ref[0]], axis=0))

    a = lax.broadcasted_iota(jnp.int32, (2 * blk, 2 * blk), 0) % blk
    j = lax.broadcasted_iota(jnp.int32, (2 * blk, 2 * blk), 1)
    rel = a + blk - j
    band = (rel >= 0) & (rel <= WINDOW)
    band_first = band & ((n > 0) | (j >= blk))
    top = lax.broadcasted_iota(jnp.int32, (2 * blk, 1), 0) < blk
    low = _lane((blk, LANES)) < B_HD
    zero = jnp.zeros((blk, LANES), BF16)

    qrows = lambda i: slice(i * blk, (i + 1) * blk)
    krows = lambda i: slice(i * blk, (i + 2) * blk)
    sink = {(kh, half): jnp.where(top, sink_ref[kh * B_GROUP + half] * LOG2E,
                                  sink_ref[kh * B_GROUP + half + 2] * LOG2E)
            for kh in range(B_KV_HEADS) for half in range(2)}
    def wave_units(i0):
        blocks = range(i0, i0 + SWA_WAVE)
        return blocks, [(i, kh, half) for i in blocks for kh in range(B_KV_HEADS) for half in range(2)]

    def score_matmuls(i0):
        blocks, units = wave_units(i0)
        mix_a = {i: _dot(oa_ref[0, qrows(i), :], w_ref[0:A_WIDTH, :]) for i in blocks}
        sc = {}
        for (i, kh, half) in units:
            qs = []
            for g in range(2):
                grp = kh * 2 + g
                xg = qb_ref[0, qrows(i), grp * LANES:(grp + 1) * LANES]
                qs.append(jnp.where(low if half == 0 else jnp.logical_not(low), xg, zero))
            qz = jnp.concatenate(qs, axis=0)
            sc[i, kh, half] = _dot_nt(qz, kx[0 if kh == half else 1][krows(i)])
        return mix_a, sc

    def finish(i0, mix_a, sc):
        blocks, units = wave_units(i0)
        p, den = {}, {}
        for u_ in units:
            i, kh, half = u_
            s_m = jnp.where(band_first if i == 0 else band, sc[u_], -jnp.inf)
            m = jnp.maximum(jnp.max(s_m, axis=-1, keepdims=True), sink[kh, half])
            e = jnp.exp2(s_m - m)
            den[u_] = jnp.sum(e, axis=-1, keepdims=True) + jnp.exp2(sink[kh, half] - m)
            p[u_] = e.astype(BF16)
        pv = {u_: _dot(p[u_], vd[u_[1]][krows(u_[0])]) for u_ in units}
        outs = {u_: pv[u_] / den[u_] for u_ in units}
        for i in blocks:
            ob = []
            for grp in range(B_WIDTH // LANES):
                kh, g = grp // 2, grp % 2
                og = jnp.where(low, outs[i, kh, 0][g * blk:(g + 1) * blk], outs[i, kh, 1][g * blk:(g + 1) * blk])
                ob.append((og * _silu(zb_ref[0, qrows(i), grp * LANES:(grp + 1) * LANES])).astype(BF16))
            mix = mix_a[i] + _dot(jnp.concatenate(ob, axis=-1), w_ref[A_WIDTH:MIX_WIDTH, :])
            r = DEEPNORM_ALPHA * x_ref[0, qrows(i), :] + (1.0 + gate) * mix
            y_ref[0, qrows(i), :] = _layer_norm(r, g_ref[...], b_ref[...])

    starts = list(range(0, tq // blk, SWA_WAVE))
    pending = score_matmuls(starts[0])
    for w, i0 in enumerate(starts):
        ready = pending
        if w + 1 < len(starts):
            pending = score_matmuls(starts[w + 1])
        finish(i0, *ready)


def _swa_out(sinks, qb, kvl, zb, oa, x, mod, mod_row0, w_out, ln_g, ln_b):
    bsz, t, _ = qb.shape
    tq = SWA_TQ
    per = tq // WINDOW
    cur = lambda w: pl.BlockSpec((1, tq, w), lambda b, i: (b, i, 0))
    kv_cur = lambda j: pl.BlockSpec((1, tq, LANES), lambda b, i: (b, i, j))
    kv_prev = lambda j: pl.BlockSpec((1, WINDOW, LANES), lambda b, i: (b, jnp.maximum(i * per - 1, 0), j))
    const2 = lambda s: pl.BlockSpec(s, lambda b, i: (0, 0))
    return pl.pallas_call(
        _swa_out_kernel,
        grid=(bsz, t // tq),
        in_specs=[pl.BlockSpec(memory_space=pltpu.SMEM), cur(B_WIDTH),
                  kv_cur(0), kv_prev(0), kv_cur(1), kv_prev(1), kv_cur(2), kv_prev(2), kv_cur(3), kv_prev(3),
                  cur(B_WIDTH), cur(A_WIDTH), cur(D_MODEL),
                  pl.BlockSpec((8, D_MODEL), lambda b, i: (mod_row0 // 8, 2)),
                  const2((MIX_WIDTH, D_MODEL)), const2((1, D_MODEL)), const2((1, D_MODEL))],
        out_specs=cur(D_MODEL),
        out_shape=jax.ShapeDtypeStruct((bsz, t, D_MODEL), F32),
        compiler_params=pltpu.CompilerParams(dimension_semantics=("arbitrary", "arbitrary"),
                                             vmem_limit_bytes=VMEM_LIMIT),
        name="swa_out",
    )(sinks, qb, kvl, kvl, kvl, kvl, kvl, kvl, kvl, kvl, zb, oa, x, mod, w_out, ln_g, ln_b)


def _out_kernel(oa_ref, ob_ref, x_ref, gate_ref, w_ref, g_ref, b_ref, y_ref):
    mix = _dot(oa_ref[...], w_ref[0:A_WIDTH, :]) + _dot(ob_ref[...], w_ref[A_WIDTH:MIX_WIDTH, :])
    r = DEEPNORM_ALPHA * x_ref[...] + (1.0 + gate_ref[...]) * mix
    y_ref[...] = _layer_norm(r, g_ref[...], b_ref[...])


def _out(oa, ob, x, mod, w_out, ln_g, ln_b):
    n = x.shape[0]
    full = lambda s: pl.BlockSpec(s, lambda i: (0, 0))
    return pl.pallas_call(
        _out_kernel,
        grid=(1,),
        in_specs=[full((n, A_WIDTH)), full((n, B_WIDTH)), full((n, D_MODEL)),
                  pl.BlockSpec((n, D_MODEL), lambda i: (0, 2)),
                  full((MIX_WIDTH, D_MODEL)), full((1, D_MODEL)), full((1, D_MODEL))],
        out_specs=full((n, D_MODEL)),
        out_shape=jax.ShapeDtypeStruct((n, D_MODEL), F32),
        compiler_params=pltpu.CompilerParams(dimension_semantics=("arbitrary",),
                                             vmem_limit_bytes=VMEM_LIMIT),
        name="out",
    )(oa, ob, x, mod, w_out, ln_g, ln_b)


def _sproj_kernel(x_ref, mod_ref, w_ref, cw_ref, cst_ref, alog_ref, dt_ref, cos_ref, sin_ref,
                  q_ref, k_ref, v_ref, za_ref, gb_ref, qb_ref, kb_ref, vb_ref, zb_ref, ncs_ref):
    shift = mod_ref[:, 0:D_MODEL]
    scale = mod_ref[:, D_MODEL:2 * D_MODEL]
    h = (x_ref[...] * (1.0 + scale) + shift).astype(BF16)

    for gi, o_ref in enumerate((q_ref, k_ref, v_ref)):
        c0 = gi * A_WIDTH
        cs = slice(c0, c0 + A_WIDTH)
        u = _dot(h, w_ref[:, cs])
        acc = cst_ref[0, :, cs] * cw_ref[0:1, cs]
        acc = acc + cst_ref[1, :, cs] * cw_ref[1:2, cs]
        acc = acc + cst_ref[2, :, cs] * cw_ref[2:3, cs]
        acc = acc + u * cw_ref[3:4, cs]
        y = _silu(acc)
        if gi == 0:
            y = _l2norm_heads(y, A_DK ** -0.5)
        elif gi == 1:
            y = _l2norm_heads(y, 1.0)
        o_ref[...] = y
        ncs_ref[0, :, cs] = cst_ref[1, :, cs]
        ncs_ref[1, :, cs] = cst_ref[2, :, cs]
        ncs_ref[2, :, cs] = u

    za_ref[...] = _dot(h, w_ref[:, C_ZA:C_ZA + A_WIDTH])
    gb_ref[...] = _gate_lanes(_dot(h, w_ref[:, C_BD:C_BD + LANES]), alog_ref[...], dt_ref[...])

    cos = cos_ref[...]
    sin = sin_ref[...]
    uq = _dot(h, w_ref[:, C_QB:C_QB + B_WIDTH])
    for g in range(B_WIDTH // LANES):
        qb_ref[:, g * LANES:(g + 1) * LANES] = (
            _rotary_group(uq[:, g * LANES:(g + 1) * LANES], cos, sin) * (B_HD ** -0.5))
    kb_ref[...] = _rotary_group(_dot(h, w_ref[:, C_KB:C_KB + LANES]), cos, sin)
    vb_ref[...] = _dot(h, w_ref[:, C_VB:C_VB + LANES])
    zb_ref[...] = _dot(h, w_ref[:, C_ZB:C_ZB + B_WIDTH])


def _sproj(x, mod_s, w_r, conv_w, cst, alog_row, dt_row, cos_row, sin_row):
    n = x.shape[0]
    full = lambda s: pl.BlockSpec(s, lambda i: (0,) * len(s))
    wide = lambda w: jax.ShapeDtypeStruct((n, w), F32)
    return pl.pallas_call(
        _sproj_kernel,
        grid=(1,),
        in_specs=[full((n, D_MODEL)), pl.BlockSpec((n, 3 * D_MODEL), lambda i: (0, 0)),
                  full((D_MODEL, W_COLS)),
                  full((CONV_W, A_QKV)), full((CONV_W - 1, n, A_QKV)),
                  full((1, LANES)), full((1, LANES)), full((1, LANES)), full((1, LANES))],
        out_specs=[full((n, A_WIDTH)), full((n, A_WIDTH)), full((n, A_WIDTH)), full((n, A_WIDTH)),
                   full((n, LANES)), full((n, B_WIDTH)), full((n, LANES)), full((n, LANES)),
                   full((n, B_WIDTH)), full((CONV_W - 1, n, A_QKV))],
        out_shape=[wide(A_WIDTH), wide(A_WIDTH), wide(A_WIDTH), wide(A_WIDTH), wide(LANES),
                   wide(B_WIDTH), wide(LANES), wide(LANES), wide(B_WIDTH),
                   jax.ShapeDtypeStruct((CONV_W - 1, n, A_QKV), F32)],
        compiler_params=pltpu.CompilerParams(dimension_semantics=("arbitrary",),
                                             vmem_limit_bytes=VMEM_LIMIT),
        name="sproj",
    )(x, mod_s, w_r, conv_w, cst, alog_row, dt_row, cos_row, sin_row)


def _sstep_kernel(sink_ref, q_ref, k_ref, v_ref, gb_ref, za_ref, na_ref, st_ref,
                  qb_ref, kn_ref, vn_ref, zb_ref, ck_ref, cv_ref,
                  oa_ref, ob_ref, nst_ref, nck_ref, ncv_ref,
                  o_scr, ob_scr):
    bt = q_ref.shape[0]
    gbv = gb_ref[...]

    pick = (lax.broadcasted_iota(jnp.int32, (bt, bt * A_DV), 1) // A_DV
            == lax.broadcasted_iota(jnp.int32, (bt, bt * A_DV), 0))
    pick = jnp.where(pick, 1.0, 0.0).astype(BF16)
    for h in range(A_HEADS):
        hs = slice(h * A_DK, (h + 1) * A_DK)
        q_rep = _dot(q_ref[:, hs].T.astype(BF16), pick)
        k_rep = _dot(k_ref[:, hs].T.astype(BF16), pick)
        for bb in range(bt):
            eg = jnp.exp(gbv[bb:bb + 1, A_HEADS + h:A_HEADS + h + 1])
            beta = gbv[bb:bb + 1, h:h + 1]
            kcol = k_rep[:, bb * A_DV:(bb + 1) * A_DV]
            qcol = q_rep[:, bb * A_DV:(bb + 1) * A_DV]
            s1 = eg * st_ref[bb, h]
            pred = jnp.sum(kcol * s1, axis=0, keepdims=True)
            upd = beta * (v_ref[bb:bb + 1, hs] - pred)
            s2 = s1 + kcol * upd
            nst_ref[bb, h] = s2
            o_scr[bb:bb + 1, hs] = jnp.sum(qcol * s2, axis=0, keepdims=True)
    na = na_ref[...]
    for h in range(A_HEADS):
        hs = slice(h * A_DK, (h + 1) * A_DK)
        o = o_scr[:, hs]
        on = o * lax.rsqrt(jnp.mean(o * o, axis=-1, keepdims=True) + RMS_EPS) * na
        oa_ref[:, hs] = (on * _silu(za_ref[:, hs])).astype(BF16)

    row8 = lax.broadcasted_iota(jnp.int32, (B_HEADS, LANES), 0)
    lane8 = _lane((B_HEADS, LANES))
    own_half = (lane8 >= B_HD) == (row8 >= B_GROUP)
    rcol = lax.broadcasted_iota(jnp.int32, (B_HEADS, 1), 0)
    sink = jnp.zeros((B_HEADS, 1), F32)
    for r in range(B_HEADS):
        sink = jnp.where(rcol == r, sink_ref[r], sink)
    qv = qb_ref[...]
    qv_r = jnp.concatenate([pltpu.roll(qv[:, g * LANES:(g + 1) * LANES], B_HD, axis=1)
                            for g in range(B_WIDTH // LANES)], axis=-1)
    kn_t = kn_ref[...].T
    vn_t = vn_ref[...].T
    newest = _lane((LANES, WINDOW)) == WINDOW - 1
    qzs, scs = [], []
    for bb in range(bt):
        qz = jnp.zeros((B_HEADS, LANES), F32)
        for r in range(B_HEADS):
            grp, half, kh = r // 2, r % 2, r // B_GROUP
            src = qv if half == kh else qv_r
            qz = jnp.where(row8 == r, src[bb:bb + 1, grp * LANES:(grp + 1) * LANES], qz)
        qzs.append(jnp.where(own_half, qz, 0.0))
    for bb in range(bt):
        scs.append(_dot(qzs[bb], ck_ref[bb]))
    ps, pnews, dens = [], [], []
    for bb in range(bt):
        sc_new = jnp.sum(qzs[bb] * kn_ref[bb:bb + 1, :], axis=-1, keepdims=True)
        m = jnp.maximum(jnp.maximum(jnp.max(scs[bb], axis=-1, keepdims=True), sc_new), sink)
        p = jnp.exp(scs[bb] - m)
        p_new = jnp.exp(sc_new - m)
        ps.append(p)
        pnews.append(p_new)
        dens.append(jnp.sum(p, axis=-1, keepdims=True) + p_new + jnp.exp(sink - m))
    pvs = [_dot_nt(ps[bb], cv_ref[bb]) for bb in range(bt)]
    for bb in range(bt):
        o = (pvs[bb] + pnews[bb] * vn_ref[bb:bb + 1, :]) / dens[bb]
        o = jnp.where(own_half, o, 0.0)
        ob_scr[bb * B_HEADS:(bb + 1) * B_HEADS, :] = o + pltpu.roll(o, B_HD, axis=1)
    for bb in range(bt):
        nck_ref[bb] = jnp.where(newest, kn_t[:, bb:bb + 1], pltpu.roll(ck_ref[bb], WINDOW - 1, axis=1))
        ncv_ref[bb] = jnp.where(newest, vn_t[:, bb:bb + 1], pltpu.roll(cv_ref[bb], WINDOW - 1, axis=1))
    low = _lane((bt, LANES)) < B_HD
    for grp in range(B_WIDTH // LANES):
        even = ob_scr[pl.ds(2 * grp, bt, stride=B_HEADS), :]
        odd = ob_scr[pl.ds(2 * grp + 1, bt, stride=B_HEADS), :]
        gs = slice(grp * LANES, (grp + 1) * LANES)
        ob_ref[:, gs] = (jnp.where(low, even, odd) * _silu(zb_ref[:, gs])).astype(BF16)


def _sstep(sinks, q, k, v, gb, za, na_row, state, qb, kn, vn, zb, ck, cv):
    n = q.shape[0]
    bt = STEP_BT
    row = lambda w: pl.BlockSpec((bt, w), lambda i: (i, 0))
    st_spec = pl.BlockSpec((bt, A_HEADS, A_DK, A_DV), lambda i: (i, 0, 0, 0))
    c_spec = pl.BlockSpec((bt, WINDOW, LANES), lambda i: (i, 0, 0))
    return pl.pallas_call(
        _sstep_kernel,
        grid=(n // bt,),
        in_specs=[pl.BlockSpec(memory_space=pltpu.SMEM),
                  row(A_WIDTH), row(A_WIDTH), row(A_WIDTH), row(LANES), row(A_WIDTH),
                  pl.BlockSpec((1, A_DV), lambda i: (0, 0)), st_spec,
                  row(B_WIDTH), row(LANES), row(LANES), row(B_WIDTH), c_spec, c_spec],
        out_specs=[row(A_WIDTH), row(B_WIDTH), st_spec, c_spec, c_spec],
        out_shape=[jax.ShapeDtypeStruct((n, A_WIDTH), BF16),
                   jax.ShapeDtypeStruct((n, B_WIDTH), BF16),
                   jax.ShapeDtypeStruct((n, A_HEADS, A_DK, A_DV), F32),
                   jax.ShapeDtypeStruct((n, WINDOW, LANES), F32),
                   jax.ShapeDtypeStruct((n, WINDOW, LANES), F32)],
        scratch_shapes=[pltpu.VMEM((bt, A_WIDTH), F32), pltpu.VMEM((bt * B_HEADS, LANES), F32)],
        compiler_params=pltpu.CompilerParams(dimension_semantics=("arbitrary",),
                                             vmem_limit_bytes=VMEM_LIMIT),
        name="sstep",
    )(sinks, q, k, v, gb, za, na_row, state, qb, kn, vn, zb, ck, cv)


def _rope_tables(pos):
    half = B_HD // 2
    inv = 1.0 / (ROPE_THETA ** (np.arange(half, dtype=np.float64) / half))
    ang = np.asarray(pos, np.float64)[:, None] * inv[None, :]
    cos, sin = np.cos(ang), np.sin(ang)
    reps = LANES // B_HD
    return (jnp.asarray(np.tile(np.concatenate([cos, cos], -1), (1, reps)), F32),
            jnp.asarray(np.tile(np.concatenate([-sin, sin], -1), (1, reps)), F32))


def _pad_row(vec, offset):
    return jnp.pad(vec.astype(F32).reshape(1, -1), ((0, 0), (offset, LANES - offset - vec.shape[0])))


def _layer(x_prompt, x_sample, state_conv, state_delta, cache_k, cache_v, c_prompt, c_sample,
           w_ada, b_ada, w_in, conv_w, a_log, dt_bias, norm_a, sinks, w_out, ln_g, ln_b):
    bsz, seq, _ = x_prompt.shape
    n_s = x_sample.shape[0]

    w_r = _wprep(jnp.swapaxes(w_in, 0, 1))
    w_o = w_out.astype(BF16)
    alog_row = _pad_row(a_log, A_HEADS)
    dt_row = _pad_row(dt_bias, A_HEADS)
    na_row = norm_a.reshape(1, A_DV)
    g_row = ln_g.reshape(1, D_MODEL)
    b_row = ln_b.reshape(1, D_MODEL)

    assert n_s % 8 == 0 and bsz <= 8
    mod = _ada(c_sample, c_prompt, w_ada, b_ada.reshape(1, 3 * D_MODEL))

    cos_p, sin_p = _rope_tables(np.arange(seq))
    (qkv, za, gb, qb, kvl, zb, conv_p, kb_last, vb_last) = _proj(
        x_prompt, mod, n_s, w_r, conv_w, alog_row, dt_row, cos_p, sin_p)
    oa, delta_p = _delta(qkv, gb, za, na_row)
    y_p = _swa_out(sinks, qb, kvl, zb, oa, x_prompt, mod, n_s, w_o, g_row, b_row)
    swa_k_p = kb_last.reshape(bsz, WINDOW, B_KV_HEADS, B_HD)
    swa_v_p = vb_last.reshape(bsz, WINDOW, B_KV_HEADS, B_HD)

    cos_s, sin_s = _rope_tables(np.array([PAST_LEN]))
    xs = x_sample.reshape(n_s, D_MODEL)
    cst = jnp.transpose(state_conv, (1, 0, 2))
    sq, sk, sv, sza, sgb, sqb, skn, svn, szb, ncs = _sproj(xs, mod, w_r, conv_w, cst, alog_row, dt_row,
                                                           cos_s, sin_s)
    soa, sob, delta_s, nck, ncv = _sstep(sinks, sq, sk, sv, sgb, sza, na_row, state_delta,
                                         sqb, skn, svn, szb,
                                         jnp.swapaxes(cache_k.reshape(n_s, WINDOW, LANES), 1, 2),
                                         jnp.swapaxes(cache_v.reshape(n_s, WINDOW, LANES), 1, 2))
    y_s = _out(soa, sob, xs, mod, w_o, g_row, b_row)
    conv_s = jnp.transpose(ncs, (1, 0, 2))
    unpack = lambda c: jnp.swapaxes(c, 1, 2).reshape(n_s, WINDOW, B_KV_HEADS, B_HD)
    return (y_p, y_s.reshape(n_s, 1, D_MODEL), conv_p, delta_p, swa_k_p, swa_v_p,
            conv_s, delta_s, unpack(nck), unpack(ncv))


def kernel(x_prompt, x_sample, state_conv, state_delta, cache_swa_k, cache_swa_v, c_prompt, c_sample,
           w_ada, b_ada, w_in, conv_w, a_log, dt_bias, norm_a, sinks, w_out, ln_g, ln_b):
    assert w_ada.shape[0] == DEPTH == 1
    outs = _layer(x_prompt, x_sample, state_conv[0], state_delta[0], cache_swa_k[0], cache_swa_v[0],
                  c_prompt, c_sample, w_ada[0], b_ada[0], w_in[0], conv_w[0], a_log[0], dt_bias[0],
                  norm_a[0], sinks[0], w_out[0], ln_g[0], ln_b[0])
    y_p, y_s = outs[0], outs[1]
    return (y_p, y_s) + tuple(o[None] for o in outs[2:])
```

```python
import jax
import jax.numpy as jnp
import numpy as np
from jax import lax
from jax.experimental import pallas as pl
from jax.experimental.pallas import tpu as pltpu

F32 = jnp.float32
BF16 = jnp.bfloat16

D_MODEL = 1024
DEPTH = 1
PAST_LEN = 8192
A_HEADS = 4
A_DK = 128
A_DV = 128
A_WIDTH = A_HEADS * A_DV
A_QKV = 3 * A_WIDTH
CONV_W = 4
CHUNK = 64
B_HEADS = 8
B_KV_HEADS = 2
B_HD = 64
B_GROUP = B_HEADS // B_KV_HEADS
B_WIDTH = B_HEADS * B_HD
B_KV_WIDTH = B_KV_HEADS * B_HD
WINDOW = 128
ROPE_THETA = 10000.0
MIX_WIDTH = A_WIDTH + B_WIDTH
DEEPNORM_ALPHA = (2 * DEPTH) ** 0.25
LOG2E = 1.4426950408889634
LN_EPS = 1e-5
RMS_EPS = 1e-6
L2_EPS = 1e-6

OFF_A_Z = A_QKV
OFF_A_BETA = OFF_A_Z + A_WIDTH
OFF_A_DECAY = OFF_A_BETA + A_HEADS
OFF_B_Q = OFF_A_DECAY + A_HEADS
OFF_B_K = OFF_B_Q + B_WIDTH
OFF_B_V = OFF_B_K + B_KV_WIDTH
OFF_B_Z = OFF_B_V + B_KV_WIDTH
PROJ_COLS = OFF_B_Z + B_WIDTH

LANES = 128
C_QKV = 0
C_ZA = C_QKV + A_QKV
C_QB = C_ZA + A_WIDTH
C_KB = C_QB + B_WIDTH
C_VB = C_KB + B_KV_WIDTH
C_ZB = C_VB + B_KV_WIDTH
C_BD = C_ZB + B_WIDTH
WPREP_TN = 256
W_COLS = C_BD + WPREP_TN

VMEM_LIMIT = 56 * 1024 * 1024

ADA_TN = 1536
PROJ_TM = 1024
PROJ_CW = 256
PROJ_PART_ROWS = (128, 256, 256, 256, 128)
DELTA_CT = 256
DELTA_WAVE = 2
SWA_TQ = 512
SWA_WAVE = 2
STEP_BT = 16


def _dot(a, b):
    return jnp.dot(a, b, preferred_element_type=F32)


def _dot_nt(a, b):
    return lax.dot_general(a, b, (((1,), (1,)), ((), ())), preferred_element_type=F32)


def _silu(x):
    return x * jax.nn.sigmoid(x)


def _softplus(x):
    return jnp.maximum(x, 0.0) + jnp.log1p(jnp.exp(-jnp.abs(x)))


def _lane(shape):
    return lax.broadcasted_iota(jnp.int32, shape, len(shape) - 1)


def _l2norm_heads(y, scale):
    outs = []
    for h in range(y.shape[1] // A_DK):
        xh = y[:, h * A_DK:(h + 1) * A_DK]
        ss = jnp.sum(xh * xh, axis=-1, keepdims=True)
        xn = xh * lax.rsqrt(ss + L2_EPS)
        outs.append(xn * scale if scale != 1.0 else xn)
    return jnp.concatenate(outs, axis=-1)


def _rotary_group(xg, cos, sin_signed):
    lane = _lane(xg.shape)
    swapped = jnp.where((lane % B_HD) < (B_HD // 2),
                        pltpu.roll(xg, LANES - B_HD // 2, axis=1),
                        pltpu.roll(xg, B_HD // 2, axis=1))
    return xg * cos + swapped * sin_signed


def _kv_layouts(kb, vb):
    low = _lane(kb.shape) < B_HD
    kbr = pltpu.roll(kb, B_HD, axis=1)
    vbr = pltpu.roll(vb, B_HD, axis=1)
    return kb, kbr, jnp.where(low, vb, vbr), jnp.where(low, vbr, vb)


def _gate_lanes(bd, alog_row, dt_row):
    lane = _lane(bd.shape)
    g = -jnp.exp(alog_row) * _softplus(bd + dt_row)
    return jnp.where(lane < A_HEADS, jax.nn.sigmoid(bd), g)


def _layer_norm(r, g, b):
    mu = jnp.mean(r, axis=-1, keepdims=True)
    d = r - mu
    var = jnp.mean(d * d, axis=-1, keepdims=True)
    return d * lax.rsqrt(var + LN_EPS) * g + b


def _wprep_kernel(wa_ref, wb_ref, o_ref):
    tn = wa_ref.shape[0]
    o_ref[:, 0:tn] = wa_ref[...].T.astype(BF16)
    xb = wb_ref[...]
    tail = pl.program_id(0) == pl.num_programs(0) - 1
    row = lax.broadcasted_iota(jnp.int32, xb.shape, 0)
    xb = jnp.where(jnp.logical_and(tail, row >= 2 * A_HEADS), 0.0, xb)
    o_ref[:, tn:2 * tn] = xb.T.astype(BF16)


def _wprep(w_t):
    tn = WPREP_TN
    n_a, n_b = OFF_A_BETA // tn, (PROJ_COLS - OFF_B_Q) // tn
    assert n_a * tn == OFF_A_BETA and n_b * tn == PROJ_COLS - OFF_B_Q and OFF_A_BETA + tn <= PROJ_COLS

    assert (n_a + n_b + 1) % 2 == 0

    def src_row(j):
        return jnp.where(j < n_a, j * tn, jnp.where(j < n_a + n_b, OFF_B_Q + (j - n_a) * tn, OFF_A_BETA))

    src = lambda k: pl.BlockSpec((pl.Element(tn), pl.Element(D_MODEL)),
                                 lambda j: (pl.multiple_of(src_row(2 * j + k), 8), 0))
    return pl.pallas_call(
        _wprep_kernel,
        grid=((n_a + n_b + 1) // 2,),
        in_specs=[src(0), src(1)],
        out_specs=pl.BlockSpec((D_MODEL, 2 * tn), lambda j: (0, j)),
        out_shape=jax.ShapeDtypeStruct((D_MODEL, W_COLS), BF16),
        compiler_params=pltpu.CompilerParams(dimension_semantics=("arbitrary",),
                                             vmem_limit_bytes=VMEM_LIMIT),
        name="wprep",
    )(w_t, w_t)


def _ada_kernel(cs_ref, cp_ref, w_ref, b_ref, o_ref):
    n_s, n_p = cs_ref.shape[0], cp_ref.shape[0]
    w = w_ref[...].astype(BF16)
    o_ref[0:n_s, :] = _dot(cs_ref[...].astype(BF16), w) + b_ref[...]
    o_ref[n_s:n_s + 8, :] = jnp.zeros((8, o_ref.shape[1]), F32)
    o_ref[n_s:n_s + n_p, :] = _dot(cp_ref[...].astype(BF16), w) + b_ref[...]


def _ada(c_sample, c_prompt, w_ada, b_ada):
    n_s, n_p = c_sample.shape[0], c_prompt.shape[0]
    rows = n_s + 8
    tn = ADA_TN
    return pl.pallas_call(
        _ada_kernel,
        grid=(3 * D_MODEL // tn,),
        in_specs=[pl.BlockSpec((n_s, D_MODEL), lambda j: (0, 0)),
                  pl.BlockSpec((n_p, D_MODEL), lambda j: (0, 0)),
                  pl.BlockSpec((D_MODEL, tn), lambda j: (0, j)),
                  pl.BlockSpec((1, tn), lambda j: (0, j))],
        out_specs=pl.BlockSpec((rows, tn), lambda j: (0, j)),
        out_shape=jax.ShapeDtypeStruct((rows, 3 * D_MODEL), F32),
        compiler_params=pltpu.CompilerParams(dimension_semantics=("arbitrary",),
                                             vmem_limit_bytes=VMEM_LIMIT),
        name="ada",
    )(c_sample, c_prompt, w_ada, b_ada)


def _proj_kernel(x_ref, mod_ref, w_ref, cw_ref, alog_ref, dt_ref, cos_ref, sin_ref,
                 qkv_ref, za_ref, gb_ref, qb_ref, kvl_ref, zb_ref, cst_ref, kbl_ref, vbl_ref, ubuf):
    tm = x_ref.shape[1]
    t = pl.program_id(1)

    @pl.when(t == 0)
    def _():
        ubuf[...] = jnp.zeros(ubuf.shape, F32)

    brow = pl.ds(pl.program_id(0), 1)
    shift = mod_ref[brow, 0:D_MODEL]
    scale = mod_ref[brow, D_MODEL:2 * D_MODEL]

    assert sum(PROJ_PART_ROWS) == tm and PROJ_PART_ROWS[-1] >= WINDOW
    cw = PROJ_CW
    pieces = [slice(c0, c0 + cw) for c0 in range(0, A_QKV, cw)]

    def matmuls(r0, rp):
        h = (x_ref[0, r0:r0 + rp, :] * (1.0 + scale) + shift).astype(BF16)
        return ([_dot(h, w_ref[:, cs]) for cs in pieces],
                _dot(h, w_ref[:, C_ZA:C_ZA + A_WIDTH]), _dot(h, w_ref[:, C_BD:C_BD + LANES]),
                _dot(h, w_ref[:, C_QB:C_QB + B_WIDTH]), _dot(h, w_ref[:, C_KB:C_KB + 2 * LANES]),
                _dot(h, w_ref[:, C_ZB:C_ZB + B_WIDTH]))

    def epilogue(r0, rp, results):
        rs = slice(r0, r0 + rp)
        us, za, ubd, uq, ukv, zb = results
        sub = lax.broadcasted_iota(jnp.int32, (rp // 8, 8, cw), 1)

        def conv_epilogue(cs, u):
            gi = cs.start // A_WIDTH
            groups = jnp.concatenate([ubuf[:, cs], u], axis=0).reshape(rp // 8 + 1, 8, cw)
            acc = None
            for j in range(CONV_W - 1, 0, -1):
                rot = pltpu.roll(groups, j, axis=1)
                term = (jnp.where(sub < j, rot[:-1], rot[1:]).reshape(rp, cw)
                        * cw_ref[CONV_W - 1 - j:CONV_W - j, cs])
                acc = term if acc is None else acc + term
            y = _silu(acc + u * cw_ref[CONV_W - 1:CONV_W, cs])
            if gi == 0:
                y = _l2norm_heads(y, A_DK ** -0.5)
            elif gi == 1:
                y = _l2norm_heads(y, 1.0)
            qkv_ref[0, rs, cs] = y
            ubuf[:, cs] = u[rp - 8:rp]
            if r0 + rp == tm:
                cst_ref[0, :, cs] = u[rp - (CONV_W - 1):rp]

        for cs, u in zip(pieces, us):
            conv_epilogue(cs, u)
        za_ref[0, rs, :] = za
        zb_ref[0, rs, :] = zb
        gb_ref[0, rs, :] = _gate_lanes(ubd, alog_ref[...], dt_ref[...])
        cos = cos_ref[rs, :]
        sin = sin_ref[rs, :]
        for g in range(B_WIDTH // LANES):
            qb_ref[0, rs, g * LANES:(g + 1) * LANES] = (
                _rotary_group(uq[:, g * LANES:(g + 1) * LANES], cos, sin) * (B_HD ** -0.5 * LOG2E)).astype(BF16)
        kb = _rotary_group(ukv[:, 0:LANES], cos, sin)
        vb = ukv[:, LANES:2 * LANES]
        for j, val in enumerate(_kv_layouts(kb, vb)):
            kvl_ref[0, rs, j * LANES:(j + 1) * LANES] = val.astype(BF16)
        return kb, vb

    parts, r0 = [], 0
    for rp in PROJ_PART_ROWS:
        parts.append((r0, rp))
        r0 += rp
    pending = matmuls(*parts[0])
    for p, part in enumerate(parts):
        results = pending
        if p + 1 < len(parts):
            pending = matmuls(*parts[p + 1])
        kb, vb = epilogue(*part, results)

    @pl.when(t == pl.num_programs(1) - 1)
    def _():
        kbl_ref[0] = kb[kb.shape[0] - WINDOW:, :]
        vbl_ref[0] = vb[vb.shape[0] - WINDOW:, :]


def _proj(x, mod, mod_row0, w_r, conv_w, alog_row, dt_row, cos_t, sin_t):
    bsz, t, _ = x.shape
    tm = PROJ_TM
    row = lambda w: pl.BlockSpec((1, tm, w), lambda b, i: (b, i, 0))
    const2 = lambda s: pl.BlockSpec(s, lambda b, i: (0, 0))
    per_b = lambda r, w: pl.BlockSpec((1, r, w), lambda b, i: (b, 0, 0))
    wide = lambda w, dt=F32: jax.ShapeDtypeStruct((bsz, t, w), dt)
    return pl.pallas_call(
        _proj_kernel,
        grid=(bsz, t // tm),
        in_specs=[row(D_MODEL),
                  pl.BlockSpec((8, 3 * D_MODEL), lambda b, i: (mod_row0 // 8, 0)),
                  pl.BlockSpec((D_MODEL, W_COLS), lambda b, i: (0, 0), pipeline_mode=pl.Buffered(1)),
                  const2((CONV_W, A_QKV)),
                  const2((1, LANES)), const2((1, LANES)),
                  pl.BlockSpec((tm, LANES), lambda b, i: (i, 0)),
                  pl.BlockSpec((tm, LANES), lambda b, i: (i, 0))],
        out_specs=[row(A_QKV), row(A_WIDTH), row(LANES), row(B_WIDTH), row(4 * LANES), row(B_WIDTH),
                   per_b(CONV_W - 1, A_QKV), per_b(WINDOW, LANES), per_b(WINDOW, LANES)],
        out_shape=[wide(A_QKV), wide(A_WIDTH), wide(LANES), wide(B_WIDTH, BF16), wide(4 * LANES, BF16),
                   wide(B_WIDTH),
                   jax.ShapeDtypeStruct((bsz, CONV_W - 1, A_QKV), F32),
                   jax.ShapeDtypeStruct((bsz, WINDOW, LANES), F32),
                   jax.ShapeDtypeStruct((bsz, WINDOW, LANES), F32)],
        scratch_shapes=[pltpu.VMEM((8, A_QKV), F32)],
        compiler_params=pltpu.CompilerParams(dimension_semantics=("arbitrary", "arbitrary"),
                                             vmem_limit_bytes=VMEM_LIMIT),
        name="proj",
    )(x, mod, w_r, conv_w, alog_row, dt_row, cos_t, sin_t)


def _delta_kernel(q_ref, k_ref, v_ref, gb_ref, za_ref, na_ref, oa_ref, st_ref,
                  s_scr, wq_s, ut_s, akd_s, gl_s):
    bsz, ct = q_ref.shape[0], q_ref.shape[1]
    nch = ct // CHUNK
    t = pl.program_id(0)
    wslot = t % 2
    rslot = 1 - wslot

    @pl.when(t == 0)
    def _():
        s_scr[...] = jnp.zeros(s_scr.shape, F32)
        wq_s[...] = jnp.zeros(wq_s.shape, BF16)
        ut_s[...] = jnp.zeros(ut_s.shape, F32)
        akd_s[...] = jnp.zeros(akd_s.shape, BF16)
        gl_s[...] = jnp.zeros(gl_s.shape, F32)

    units = [(b, c, h) for b in range(bsz) for c in range(nch) for h in range(A_HEADS)]
    uid = {u_: i for i, u_ in enumerate(units)}
    rows = lambda c: slice(c * CHUNK, (c + 1) * CHUNK)
    lanes = lambda h: slice(h * A_DK, (h + 1) * A_DK)
    na = na_ref[...]

    s_cur = {(b, h): s_scr[b * A_HEADS + h] for b in range(bsz) for h in range(A_HEADS)}
    ws, uu = {}, {}

    def rec_ws(c):
        for b in range(bsz):
            for h in range(A_HEADS):
                i = uid[b, c, h]
                ws[b, h] = _dot(wq_s[rslot, i], s_cur[b, h].astype(BF16))
                uu[b, h] = (ut_s[rslot, i] - ws[b, h][:CHUNK]).astype(BF16)

    def rec_ou(c):
        zpad = jnp.zeros((CHUNK, A_DV), BF16)
        for b in range(bsz):
            u_bd = jnp.concatenate(
                [jnp.concatenate([uu[b, h] if hh == h else zpad for hh in range(A_HEADS)], axis=-1)
                 for h in range(A_HEADS)], axis=0)
            ou = _dot(akd_s[rslot, b * nch + c], u_bd)
            for h in range(A_HEADS):
                o = ws[b, h][CHUNK:] + ou[:CHUNK, lanes(h)]
                s_cur[b, h] = gl_s[rslot, uid[b, c, h]] * s_cur[b, h] + ou[CHUNK:, lanes(h)]
                on = o * lax.rsqrt(jnp.mean(o * o, axis=-1, keepdims=True) + RMS_EPS) * na
                oa_ref[b, rows(c), lanes(h)] = (on * _silu(za_ref[b, rows(c), lanes(h)])).astype(BF16)

    rec_stages = []
    for c in range(nch):
        rec_stages += [lambda c=c: rec_ws(c), lambda c=c: rec_ou(c)]

    def run_rec(n_left_after):
        while rec_stages and len(rec_stages) > n_left_after:
            rec_stages.pop(0)()

    pk = A_HEADS * CHUNK
    low = _lane((CHUNK, LANES)) < CHUNK
    low_row = _lane((1, LANES)) < CHUNK
    ti_p = lax.broadcasted_iota(jnp.int32, (CHUNK, pk), 0)
    ii_p = _lane((CHUNK, pk)) % CHUNK
    zero64 = jnp.zeros((CHUNK, LANES), BF16)

    def pack(parts):
        return jnp.concatenate([jnp.where(low, parts[0], parts[1]), jnp.where(low, parts[2], parts[3])], axis=-1)

    def block_diag(x16):
        blocks = []
        for h in range(A_HEADS):
            pair, first = h // 2, h % 2 == 0
            piece = jnp.where(low if first else jnp.logical_not(low), x16[:, pair * LANES:(pair + 1) * LANES], zero64)
            blocks.append(jnp.concatenate([piece, zero64] if pair == 0 else [zero64, piece], axis=-1))
        return jnp.concatenate(blocks, axis=0)

    zrhs = jnp.zeros((CHUNK, 2 * A_DK), BF16)
    n_rec = len(rec_stages)
    n_slots = 8 * (bsz // DELTA_WAVE)
    done = [0]

    def stage_done():
        done[0] += 1
        run_rec(n_rec - 1 - (done[0] * n_rec) // n_slots)

    def decay_terms(b, beta, g_col, g_last, eg, dec_p, beta_p):
        gbv = gb_ref[b]
        rin = lax.broadcasted_iota(jnp.int32, gbv.shape, 0) % CHUNK
        gcs = gbv
        s = 1
        while s < CHUNK:
            gcs = gcs + jnp.where(rin >= s, pltpu.roll(gcs, s, axis=0), 0.0)
            s *= 2
        gcs_t = gcs.T
        for c in range(nch):
            r0 = c * CHUNK
            pair_lanes = slice((c // 2) * LANES, (c // 2 + 1) * LANES)
            g_rows = []
            for h in range(A_HEADS):
                u_ = (b, c, h)
                beta[u_] = jnp.broadcast_to(gbv[rows(c), h:h + 1], (CHUNK, A_DK))
                g_col[u_] = jnp.broadcast_to(gcs[rows(c), A_HEADS + h:A_HEADS + h + 1], (CHUNK, A_DK))
                g_last[u_] = gcs[r0 + CHUNK - 1:r0 + CHUNK, A_HEADS + h:A_HEADS + h + 1]
                eg[u_] = jnp.exp(g_col[u_])
                g_row = gcs_t[A_HEADS + h:A_HEADS + h + 1, pair_lanes]
                g_rows.append(g_row if c % 2 == h % 2 else pltpu.roll(g_row, CHUNK, axis=1))
            g_row_p = jnp.concatenate([jnp.where(low_row, g_rows[0], g_rows[1]),
                                       jnp.where(low_row, g_rows[2], g_rows[3])], axis=-1)
            g_col_p = pack([g_col[b, c, h] for h in range(A_HEADS)])
            dec_p[b, c] = jnp.exp(jnp.where(ti_p >= ii_p, g_col_p - g_row_p, -jnp.inf))
            beta_p[b, c] = pack([beta[b, c, h] for h in range(A_HEADS)])

    def prepare(bs):
        groups_b = [(b, c) for b in bs for c in range(nch)]
        beta, g_col, g_last, eg, dec_p, beta_p = {}, {}, {}, {}, {}, {}
        for b in bs:
            decay_terms(b, beta, g_col, g_last, eg, dec_p, beta_p)

        nmat = {}
        for (b, c) in groups_b:
            k16 = k_ref[b, rows(c), :].astype(BF16)
            q16 = q_ref[b, rows(c), :].astype(BF16)
            k_heads = jnp.concatenate(
                [jnp.concatenate([k16[:, lanes(h)] if hh == h else zero64 for hh in range(A_HEADS)], axis=-1)
                 for h in range(A_HEADS)], axis=0)
            kq = _dot_nt(jnp.concatenate([k16, q16], axis=0), k_heads)
            nmat[b, c] = -(beta_p[b, c] * kq[:CHUNK] * jnp.where(ti_p > ii_p, dec_p[b, c], 0.0))
            akd_s[wslot, b * nch + c, 0:CHUNK, :] = (kq[CHUNK:] * dec_p[b, c]).astype(BF16)
        stage_done()

        rsum = dict(nmat)
        pw16 = {g_: nmat[g_].astype(BF16) for g_ in groups_b}
        pw = {g_: _dot(pw16[g_], block_diag(pw16[g_])) for g_ in groups_b}
        stage_done()
        for step in range(1, 6):
            last = step == 5
            pw16 = {g_: pw[g_].astype(BF16) for g_ in groups_b}
            rp = {}
            for g_ in groups_b:
                r16 = rsum[g_].astype(BF16)
                rp[g_] = _dot(r16 if last else jnp.concatenate([r16, pw16[g_]], axis=0), block_diag(pw16[g_]))
            for g_ in groups_b:
                rsum[g_] = rsum[g_] + pw[g_] + rp[g_][:CHUNK]
                if not last:
                    pw[g_] = rp[g_][CHUNK:]
            stage_done()

        for (b, c) in groups_b:
            for h in range(A_HEADS):
                u_ = (b, c, h)
                i = uid[u_]
                kc = k_ref[b, rows(c), lanes(h)]
                rhs = jnp.concatenate([(beta[u_] * eg[u_]) * kc, beta[u_] * v_ref[b, rows(c), lanes(h)]],
                                      axis=-1)
                rhs16 = rhs.astype(BF16)
                rhs_rows = jnp.concatenate([rhs16 if hh == h else zrhs for hh in range(A_HEADS)], axis=0)
                sol = rhs + _dot(rsum[b, c].astype(BF16), rhs_rows)
                wq_s[wslot, i] = jnp.concatenate([sol[:, :A_DK], eg[u_] * q_ref[b, rows(c), lanes(h)]],
                                                 axis=0).astype(BF16)
                ut_s[wslot, i] = sol[:, A_DK:]
                gl_s[wslot, i] = jnp.broadcast_to(jnp.exp(g_last[u_]), (1, A_DV))
        for (b, c) in groups_b:
            kd = [jnp.exp(g_last[b, c, h] - g_col[b, c, h]) * k_ref[b, rows(c), lanes(h)]
                  for h in range(A_HEADS)]
            for p in range(A_HEADS // 2):
                akd_s[wslot, b * nch + c, CHUNK:, p * LANES:(p + 1) * LANES] = (
                    jnp.concatenate([kd[2 * p], kd[2 * p + 1]], axis=0).T.astype(BF16))
        stage_done()

    run_rec(n_rec - 1)
    for b0 in range(0, bsz, DELTA_WAVE):
        prepare(range(b0, b0 + DELTA_WAVE))
    run_rec(0)

    for b in range(bsz):
        for h in range(A_HEADS):
            s_scr[b * A_HEADS + h] = s_cur[b, h]

    @pl.when(t == pl.num_programs(0) - 1)
    def _():
        for b in range(bsz):
            for h in range(A_HEADS):
                st_ref[b, h] = s_cur[b, h]


def _delta(qkv, gb, za, na_row):
    bsz, t, _ = qkv.shape
    ct = DELTA_CT
    nt = t // ct
    n_units = bsz * (ct // CHUNK) * A_HEADS
    prep = lambda w, j=0: pl.BlockSpec((bsz, ct, w), lambda i: (0, jnp.minimum(i, nt - 1), j))
    rec = lambda w: pl.BlockSpec((bsz, ct, w), lambda i: (0, jnp.maximum(i - 1, 0), 0))
    return pl.pallas_call(
        _delta_kernel,
        grid=(nt + 1,),
        in_specs=[prep(A_WIDTH, 0), prep(A_WIDTH, 1), prep(A_WIDTH, 2), prep(LANES), rec(A_WIDTH),
                  pl.BlockSpec((1, A_DV), lambda i: (0, 0))],
        out_specs=[rec(A_WIDTH),
                   pl.BlockSpec((bsz, A_HEADS, A_DK, A_DV), lambda i: (0, 0, 0, 0))],
        out_shape=[jax.ShapeDtypeStruct((bsz, t, A_WIDTH), BF16),
                   jax.ShapeDtypeStruct((bsz, A_HEADS, A_DK, A_DV), F32)],
        scratch_shapes=[pltpu.VMEM((bsz * A_HEADS, A_DK, A_DV), F32),
                        pltpu.VMEM((2, n_units, 2 * CHUNK, A_DK), BF16),
                        pltpu.VMEM((2, n_units, CHUNK, A_DV), F32),
                        pltpu.VMEM((2, n_units // A_HEADS, CHUNK + A_DK, A_HEADS * CHUNK), BF16),
                        pltpu.VMEM((2, n_units, 1, A_DV), F32)],
        compiler_params=pltpu.CompilerParams(dimension_semantics=("arbitrary",),
                                             vmem_limit_bytes=VMEM_LIMIT),
        name="delta",
    )(qkv, qkv, qkv, gb, za, na_row)


def _swa_out_kernel(sink_ref, qb_ref, kc_ref, kp_ref, krc_ref, krp_ref, v0c_ref, v0p_ref, v1c_ref, v1p_ref,
                    zb_ref, oa_ref, x_ref, gate_ref, w_ref, g_ref, b_ref, y_ref):
    n = pl.program_id(1)
    gate = gate_ref[pl.ds(pl.program_id(0), 1), :]
    tq = qb_ref.shape[1]
    blk = WINDOW
    kx = (jnp.concatenate([kp_ref[0], kc_ref[0]], axis=0), jnp.concatenate([krp_ref[0], krc_ref[0]], axis=0))
    vd = (jnp.concatenate([v0p_ref[0], v0c_ref[0]], axis=0), jnp.concatenate([v1p_ref[0], v1c_ref[0]], axis=0))

    a = lax.broadcasted_iota(jnp.int32, (2 * blk, 2 * blk), 0) % blk
    j = lax.broadcasted_iota(jnp.int32, (2 * blk, 2 * blk), 1)
    rel = a + blk - j
    band = (rel >= 0) & (rel <= WINDOW)
    band_first = band & ((n > 0) | (j >= blk))
    top = lax.broadcasted_iota(jnp.int32, (2 * blk, 1), 0) < blk
    low = _lane((blk, LANES)) < B_HD
    zero = jnp.zeros((blk, LANES), BF16)

    qrows = lambda i: slice(i * blk, (i + 1) * blk)
    krows = lambda i: slice(i * blk, (i + 2) * blk)
    sink = {(kh, half): jnp.where(top, sink_ref[kh * B_GROUP + half] * LOG2E,
                                  sink_ref[kh * B_GROUP + half + 2] * LOG2E)
            for kh in range(B_KV_HEADS) for half in range(2)}
    def wave_units(i0):
        blocks = range(i0, i0 + SWA_WAVE)
        return blocks, [(i, kh, half) for i in blocks for kh in range(B_KV_HEADS) for half in range(2)]

    def score_matmuls(i0):
        blocks, units = wave_units(i0)
        mix_a = {i: _dot(oa_ref[0, qrows(i), :], w_ref[0:A_WIDTH, :]) for i in blocks}
        sc = {}
        for (i, kh, half) in units:
            qs = []
            for g in range(2):
                grp = kh * 2 + g
                xg = qb_ref[0, qrows(i), grp * LANES:(grp + 1) * LANES]
                qs.append(jnp.where(low if half == 0 else jnp.logical_not(low), xg, zero))
            qz = jnp.concatenate(qs, axis=0)
            sc[i, kh, half] = _dot_nt(qz, kx[0 if kh == half else 1][krows(i)])
        return mix_a, sc

    def finish(i0, mix_a, sc):
        blocks, units = wave_units(i0)
        p, den = {}, {}
        for u_ in units:
            i, kh, half = u_
            s_m = jnp.where(band_first if i == 0 else band, sc[u_], -jnp.inf)
            m = jnp.maximum(jnp.max(s_m, axis=-1, keepdims=True), sink[kh, half])
            e = jnp.exp2(s_m - m)
            den[u_] = jnp.sum(e, axis=-1, keepdims=True) + jnp.exp2(sink[kh, half] - m)
            p[u_] = e.astype(BF16)
        pv = {u_: _dot(p[u_], vd[u_[1]][krows(u_[0])]) for u_ in units}
        outs = {u_: pv[u_] / den[u_] for u_ in units}
        for i in blocks:
            ob = []
            for grp in range(B_WIDTH // LANES):
                kh, g = grp // 2, grp % 2
                og = jnp.where(low, outs[i, kh, 0][g * blk:(g + 1) * blk], outs[i, kh, 1][g * blk:(g + 1) * blk])
                ob.append((og * _silu(zb_ref[0, qrows(i), grp * LANES:(grp + 1) * LANES])).astype(BF16))
            mix = mix_a[i] + _dot(jnp.concatenate(ob, axis=-1), w_ref[A_WIDTH:MIX_WIDTH, :])
            r = DEEPNORM_ALPHA * x_ref[0, qrows(i), :] + (1.0 + gate) * mix
            y_ref[0, qrows(i), :] = _layer_norm(r, g_ref[...], b_ref[...])

    starts = list(range(0, tq // blk, SWA_WAVE))
    pending = score_matmuls(starts[0])
    for w, i0 in enumerate(starts):
        ready = pending
        if w + 1 < len(starts):
            pending = score_matmuls(starts[w + 1])
        finish(i0, *ready)


def _swa_out(sinks, qb, kvl, zb, oa, x, mod, mod_row0, w_out, ln_g, ln_b):
    bsz, t, _ = qb.shape
    tq = SWA_TQ
    per = tq // WINDOW
    cur = lambda w: pl.BlockSpec((1, tq, w), lambda b, i: (b, i, 0))
    kv_cur = lambda j: pl.BlockSpec((1, tq, LANES), lambda b, i: (b, i, j))
    kv_prev = lambda j: pl.BlockSpec((1, WINDOW, LANES), lambda b, i: (b, jnp.maximum(i * per - 1, 0), j))
    const2 = lambda s: pl.BlockSpec(s, lambda b, i: (0, 0))
    return pl.pallas_call(
        _swa_out_kernel,
        grid=(bsz, t // tq),
        in_specs=[pl.BlockSpec(memory_space=pltpu.SMEM), cur(B_WIDTH),
                  kv_cur(0), kv_prev(0), kv_cur(1), kv_prev(1), kv_cur(2), kv_prev(2), kv_cur(3), kv_prev(3),
                  cur(B_WIDTH), cur(A_WIDTH), cur(D_MODEL),
                  pl.BlockSpec((8, D_MODEL), lambda b, i: (mod_row0 // 8, 2)),
                  const2((MIX_WIDTH, D_MODEL)), const2((1, D_MODEL)), const2((1, D_MODEL))],
        out_specs=cur(D_MODEL),
        out_shape=jax.ShapeDtypeStruct((bsz, t, D_MODEL), F32),
        compiler_params=pltpu.CompilerParams(dimension_semantics=("arbitrary", "arbitrary"),
                                             vmem_limit_bytes=VMEM_LIMIT),
        name="swa_out",
    )(sinks, qb, kvl, kvl, kvl, kvl, kvl, kvl, kvl, kvl, zb, oa, x, mod, w_out, ln_g, ln_b)


def _out_kernel(oa_ref, ob_ref, x_ref, gate_ref, w_ref, g_ref, b_ref, y_ref):
    mix = _dot(oa_ref[...], w_ref[0:A_WIDTH, :]) + _dot(ob_ref[...], w_ref[A_WIDTH:MIX_WIDTH, :])
    r = DEEPNORM_ALPHA * x_ref[...] + (1.0 + gate_ref[...]) * mix
    y_ref[...] = _layer_norm(r, g_ref[...], b_ref[...])


def _out(oa, ob, x, mod, w_out, ln_g, ln_b):
    n = x.shape[0]
    full = lambda s: pl.BlockSpec(s, lambda i: (0, 0))
    return pl.pallas_call(
        _out_kernel,
        grid=(1,),
        in_specs=[full((n, A_WIDTH)), full((n, B_WIDTH)), full((n, D_MODEL)),
                  pl.BlockSpec((n, D_MODEL), lambda i: (0, 2)),
                  full((MIX_WIDTH, D_MODEL)), full((1, D_MODEL)), full((1, D_MODEL))],
        out_specs=full((n, D_MODEL)),
        out_shape=jax.ShapeDtypeStruct((n, D_MODEL), F32),
        compiler_params=pltpu.CompilerParams(dimension_semantics=("arbitrary",),
                                             vmem_limit_bytes=VMEM_LIMIT),
        name="out",
    )(oa, ob, x, mod, w_out, ln_g, ln_b)


def _sproj_kernel(x_ref, mod_ref, w_ref, cw_ref, cst_ref, alog_ref, dt_ref, cos_ref, sin_ref,
                  q_ref, k_ref, v_ref, za_ref, gb_ref, qb_ref, kb_ref, vb_ref, zb_ref, ncs_ref):
    shift = mod_ref[:, 0:D_MODEL]
    scale = mod_ref[:, D_MODEL:2 * D_MODEL]
    h = (x_ref[...] * (1.0 + scale) + shift).astype(BF16)

    for gi, o_ref in enumerate((q_ref, k_ref, v_ref)):
        c0 = gi * A_WIDTH
        cs = slice(c0, c0 + A_WIDTH)
        u = _dot(h, w_ref[:, cs])
        acc = cst_ref[0, :, cs] * cw_ref[0:1, cs]
        acc = acc + cst_ref[1, :, cs] * cw_ref[1:2, cs]
        acc = acc + cst_ref[2, :, cs] * cw_ref[2:3, cs]
        acc = acc + u * cw_ref[3:4, cs]
        y = _silu(acc)
        if gi == 0:
            y = _l2norm_heads(y, A_DK ** -0.5)
        elif gi == 1:
            y = _l2norm_heads(y, 1.0)
        o_ref[...] = y
        ncs_ref[0, :, cs] = cst_ref[1, :, cs]
        ncs_ref[1, :, cs] = cst_ref[2, :, cs]
        ncs_ref[2, :, cs] = u

    za_ref[...] = _dot(h, w_ref[:, C_ZA:C_ZA + A_WIDTH])
    gb_ref[...] = _gate_lanes(_dot(h, w_ref[:, C_BD:C_BD + LANES]), alog_ref[...], dt_ref[...])

    cos = cos_ref[...]
    sin = sin_ref[...]
    uq = _dot(h, w_ref[:, C_QB:C_QB + B_WIDTH])
    for g in range(B_WIDTH // LANES):
        qb_ref[:, g * LANES:(g + 1) * LANES] = (
            _rotary_group(uq[:, g * LANES:(g + 1) * LANES], cos, sin) * (B_HD ** -0.5))
    kb_ref[...] = _rotary_group(_dot(h, w_ref[:, C_KB:C_KB + LANES]), cos, sin)
    vb_ref[...] = _dot(h, w_ref[:, C_VB:C_VB + LANES])
    zb_ref[...] = _dot(h, w_ref[:, C_ZB:C_ZB + B_WIDTH])


def _sproj(x, mod_s, w_r, conv_w, cst, alog_row, dt_row, cos_row, sin_row):
    n = x.shape[0]
    full = lambda s: pl.BlockSpec(s, lambda i: (0,) * len(s))
    wide = lambda w: jax.ShapeDtypeStruct((n, w), F32)
    return pl.pallas_call(
        _sproj_kernel,
        grid=(1,),
        in_specs=[full((n, D_MODEL)), pl.BlockSpec((n, 3 * D_MODEL), lambda i: (0, 0)),
                  full((D_MODEL, W_COLS)),
                  full((CONV_W, A_QKV)), full((CONV_W - 1, n, A_QKV)),
                  full((1, LANES)), full((1, LANES)), full((1, LANES)), full((1, LANES))],
        out_specs=[full((n, A_WIDTH)), full((n, A_WIDTH)), full((n, A_WIDTH)), full((n, A_WIDTH)),
                   full((n, LANES)), full((n, B_WIDTH)), full((n, LANES)), full((n, LANES)),
                   full((n, B_WIDTH)), full((CONV_W - 1, n, A_QKV))],
        out_shape=[wide(A_WIDTH), wide(A_WIDTH), wide(A_WIDTH), wide(A_WIDTH), wide(LANES),
                   wide(B_WIDTH), wide(LANES), wide(LANES), wide(B_WIDTH),
                   jax.ShapeDtypeStruct((CONV_W - 1, n, A_QKV), F32)],
        compiler_params=pltpu.CompilerParams(dimension_semantics=("arbitrary",),
                                             vmem_limit_bytes=VMEM_LIMIT),
        name="sproj",
    )(x, mod_s, w_r, conv_w, cst, alog_row, dt_row, cos_row, sin_row)


def _sstep_kernel(sink_ref, q_ref, k_ref, v_ref, gb_ref, za_ref, na_ref, st_ref,
                  qb_ref, kn_ref, vn_ref, zb_ref, ck_ref, cv_ref,
                  oa_ref, ob_ref, nst_ref, nck_ref, ncv_ref,
                  o_scr, ob_scr):
    bt = q_ref.shape[0]
    gbv = gb_ref[...]

    pick = (lax.broadcasted_iota(jnp.int32, (bt, bt * A_DV), 1) // A_DV
            == lax.broadcasted_iota(jnp.int32, (bt, bt * A_DV), 0))
    pick = jnp.where(pick, 1.0, 0.0).astype(BF16)
    for h in range(A_HEADS):
        hs = slice(h * A_DK, (h + 1) * A_DK)
        q_rep = _dot(q_ref[:, hs].T.astype(BF16), pick)
        k_rep = _dot(k_ref[:, hs].T.astype(BF16), pick)
        for bb in range(bt):
            eg = jnp.exp(gbv[bb:bb + 1, A_HEADS + h:A_HEADS + h + 1])
            beta = gbv[bb:bb + 1, h:h + 1]
            kcol = k_rep[:, bb * A_DV:(bb + 1) * A_DV]
            qcol = q_rep[:, bb * A_DV:(bb + 1) * A_DV]
            s1 = eg * st_ref[bb, h]
            pred = jnp.sum(kcol * s1, axis=0, keepdims=True)
            upd = beta * (v_ref[bb:bb + 1, hs] - pred)
            s2 = s1 + kcol * upd
            nst_ref[bb, h] = s2
            o_scr[bb:bb + 1, hs] = jnp.sum(qcol * s2, axis=0, keepdims=True)
    na = na_ref[...]
    for h in range(A_HEADS):
        hs = slice(h * A_DK, (h + 1) * A_DK)
        o = o_scr[:, hs]
        on = o * lax.rsqrt(jnp.mean(o * o, axis=-1, keepdims=True) + RMS_EPS) * na
        oa_ref[:, hs] = (on * _silu(za_ref[:, hs])).astype(BF16)

    row8 = lax.broadcasted_iota(jnp.int32, (B_HEADS, LANES), 0)
    lane8 = _lane((B_HEADS, LANES))
    own_half = (lane8 >= B_HD) == (row8 >= B_GROUP)
    rcol = lax.broadcasted_iota(jnp.int32, (B_HEADS, 1), 0)
    sink = jnp.zeros((B_HEADS, 1), F32)
    for r in range(B_HEADS):
        sink = jnp.where(rcol == r, sink_ref[r], sink)
    qv = qb_ref[...]
    qv_r = jnp.concatenate([pltpu.roll(qv[:, g * LANES:(g + 1) * LANES], B_HD, axis=1)
                            for g in range(B_WIDTH // LANES)], axis=-1)
    kn_t = kn_ref[...].T
    vn_t = vn_ref[...].T
    newest = _lane((LANES, WINDOW)) == WINDOW - 1
    qzs, scs = [], []
    for bb in range(bt):
        qz = jnp.zeros((B_HEADS, LANES), F32)
        for r in range(B_HEADS):
            grp, half, kh = r // 2, r % 2, r // B_GROUP
            src = qv if half == kh else qv_r
            qz = jnp.where(row8 == r, src[bb:bb + 1, grp * LANES:(grp + 1) * LANES], qz)
        qzs.append(jnp.where(own_half, qz, 0.0))
    for bb in range(bt):
        scs.append(_dot(qzs[bb], ck_ref[bb]))
    ps, pnews, dens = [], [], []
    for bb in range(bt):
        sc_new = jnp.sum(qzs[bb] * kn_ref[bb:bb + 1, :], axis=-1, keepdims=True)
        m = jnp.maximum(jnp.maximum(jnp.max(scs[bb], axis=-1, keepdims=True), sc_new), sink)
        p = jnp.exp(scs[bb] - m)
        p_new = jnp.exp(sc_new - m)
        ps.append(p)
        pnews.append(p_new)
        dens.append(jnp.sum(p, axis=-1, keepdims=True) + p_new + jnp.exp(sink - m))
    pvs = [_dot_nt(ps[bb], cv_ref[bb]) for bb in range(bt)]
    for bb in range(bt):
        o = (pvs[bb] + pnews[bb] * vn_ref[bb:bb + 1, :]) / dens[bb]
        o = jnp.where(own_half, o, 0.0)
        ob_scr[bb * B_HEADS:(bb + 1) * B_HEADS, :] = o + pltpu.roll(o, B_HD, axis=1)
    for bb in range(bt):
        nck_ref[bb] = jnp.where(newest, kn_t[:, bb:bb + 1], pltpu.roll(ck_ref[bb], WINDOW - 1, axis=1))
        ncv_ref[bb] = jnp.where(newest, vn_t[:, bb:bb + 1], pltpu.roll(cv_ref[bb], WINDOW - 1, axis=1))
    low = _lane((bt, LANES)) < B_HD
    for grp in range(B_WIDTH // LANES):
        even = ob_scr[pl.ds(2 * grp, bt, stride=B_HEADS), :]
        odd = ob_scr[pl.ds(2 * grp + 1, bt, stride=B_HEADS), :]
        gs = slice(grp * LANES, (grp + 1) * LANES)
        ob_ref[:, gs] = (jnp.where(low, even, odd) * _silu(zb_ref[:, gs])).astype(BF16)


def _sstep(sinks, q, k, v, gb, za, na_row, state, qb, kn, vn, zb, ck, cv):
    n = q.shape[0]
    bt = STEP_BT
    row = lambda w: pl.BlockSpec((bt, w), lambda i: (i, 0))
    st_spec = pl.BlockSpec((bt, A_HEADS, A_DK, A_DV), lambda i: (i, 0, 0, 0))
    c_spec = pl.BlockSpec((bt, WINDOW, LANES), lambda i: (i, 0, 0))
    return pl.pallas_call(
        _sstep_kernel,
        grid=(n // bt,),
        in_specs=[pl.BlockSpec(memory_space=pltpu.SMEM),
                  row(A_WIDTH), row(A_WIDTH), row(A_WIDTH), row(LANES), row(A_WIDTH),
                  pl.BlockSpec((1, A_DV), lambda i: (0, 0)), st_spec,
                  row(B_WIDTH), row(LANES), row(LANES), row(B_WIDTH), c_spec, c_spec],
        out_specs=[row(A_WIDTH), row(B_WIDTH), st_spec, c_spec, c_spec],
        out_shape=[jax.ShapeDtypeStruct((n, A_WIDTH), BF16),
                   jax.ShapeDtypeStruct((n, B_WIDTH), BF16),
                   jax.ShapeDtypeStruct((n, A_HEADS, A_DK, A_DV), F32),
                   jax.ShapeDtypeStruct((n, WINDOW, LANES), F32),
                   jax.ShapeDtypeStruct((n, WINDOW, LANES), F32)],
        scratch_shapes=[pltpu.VMEM((bt, A_WIDTH), F32), pltpu.VMEM((bt * B_HEADS, LANES), F32)],
        compiler_params=pltpu.CompilerParams(dimension_semantics=("arbitrary",),
                                             vmem_limit_bytes=VMEM_LIMIT),
        name="sstep",
    )(sinks, q, k, v, gb, za, na_row, state, qb, kn, vn, zb, ck, cv)


def _rope_tables(pos):
    half = B_HD // 2
    inv = 1.0 / (ROPE_THETA ** (np.arange(half, dtype=np.float64) / half))
    ang = np.asarray(pos, np.float64)[:, None] * inv[None, :]
    cos, sin = np.cos(ang), np.sin(ang)
    reps = LANES // B_HD
    return (jnp.asarray(np.tile(np.concatenate([cos, cos], -1), (1, reps)), F32),
            jnp.asarray(np.tile(np.concatenate([-sin, sin], -1), (1, reps)), F32))


def _pad_row(vec, offset):
    return jnp.pad(vec.astype(F32).reshape(1, -1), ((0, 0), (offset, LANES - offset - vec.shape[0])))


def _layer(x_prompt, x_sample, state_conv, state_delta, cache_k, cache_v, c_prompt, c_sample,
           w_ada, b_ada, w_in, conv_w, a_log, dt_bias, norm_a, sinks, w_out, ln_g, ln_b):
    bsz, seq, _ = x_prompt.shape
    n_s = x_sample.shape[0]

    w_r = _wprep(jnp.swapaxes(w_in, 0, 1))
    w_o = w_out.astype(BF16)
    alog_row = _pad_row(a_log, A_HEADS)
    dt_row = _pad_row(dt_bias, A_HEADS)
    na_row = norm_a.reshape(1, A_DV)
    g_row = ln_g.reshape(1, D_MODEL)
    b_row = ln_b.reshape(1, D_MODEL)

    assert n_s % 8 == 0 and bsz <= 8
    mod = _ada(c_sample, c_prompt, w_ada, b_ada.reshape(1, 3 * D_MODEL))

    cos_p, sin_p = _rope_tables(np.arange(seq))
    (qkv, za, gb, qb, kvl, zb, conv_p, kb_last, vb_last) = _proj(
        x_prompt, mod, n_s, w_r, conv_w, alog_row, dt_row, cos_p, sin_p)
    oa, delta_p = _delta(qkv, gb, za, na_row)
    y_p = _swa_out(sinks, qb, kvl, zb, oa, x_prompt, mod, n_s, w_o, g_row, b_row)
    swa_k_p = kb_last.reshape(bsz, WINDOW, B_KV_HEADS, B_HD)
    swa_v_p = vb_last.reshape(bsz, WINDOW, B_KV_HEADS, B_HD)

    cos_s, sin_s = _rope_tables(np.array([PAST_LEN]))
    xs = x_sample.reshape(n_s, D_MODEL)
    cst = jnp.transpose(state_conv, (1, 0, 2))
    sq, sk, sv, sza, sgb, sqb, skn, svn, szb, ncs = _sproj(xs, mod, w_r, conv_w, cst, alog_row, dt_row,
                                                           cos_s, sin_s)
    soa, sob, delta_s, nck, ncv = _sstep(sinks, sq, sk, sv, sgb, sza, na_row, state_delta,
                                         sqb, skn, svn, szb,
                                         jnp.swapaxes(cache_k.reshape(n_s, WINDOW, LANES), 1, 2),
                                         jnp.swapaxes(cache_v.reshape(n_s, WINDOW, LANES), 1, 2))
    y_s = _out(soa, sob, xs, mod, w_o, g_row, b_row)
    conv_s = jnp.transpose(ncs, (1, 0, 2))
    unpack = lambda c: jnp.swapaxes(c, 1, 2).reshape(n_s, WINDOW, B_KV_HEADS, B_HD)
    return (y_p, y_s.reshape(n_s, 1, D_MODEL), conv_p, delta_p, swa_k_p, swa_v_p,
            conv_s, delta_s, unpack(nck), unpack(ncv))


def kernel(x_prompt, x_sample, state_conv, state_delta, cache_swa_k, cache_swa_v, c_prompt, c_sample,
           w_ada, b_ada, w_in, conv_w, a_log, dt_bias, norm_a, sinks, w_out, ln_g, ln_b):
    assert w_ada.shape[0] == DEPTH == 1
    outs = _layer(x_prompt, x_sample, state_conv[0], state_delta[0], cache_swa_k[0], cache_swa_v[0],
                  c_prompt, c_sample, w_ada[0], b_ada[0], w_in[0], conv_w[0], a_log[0], dt_bias[0],
                  norm_a[0], sinks[0], w_out[0], ln_g[0], ln_b[0])
    y_p, y_s = outs[0], outs[1]
    return (y_p, y_s) + tuple(o[None] for o in outs[2:])
```

```python
import jax
import jax.numpy as jnp
import numpy as np
from jax import lax
from jax.experimental import pallas as pl
from jax.experimental.pallas import tpu as pltpu

F32 = jnp.float32
BF16 = jnp.bfloat16

D_MODEL = 1024
DEPTH = 1
PAST_LEN = 8192
A_HEADS = 4
A_DK = 128
A_DV = 128
A_WIDTH = A_HEADS * A_DV
A_QKV = 3 * A_WIDTH
CONV_W = 4
CHUNK = 64
B_HEADS = 8
B_KV_HEADS = 2
B_HD = 64
B_GROUP = B_HEADS // B_KV_HEADS
B_WIDTH = B_HEADS * B_HD
B_KV_WIDTH = B_KV_HEADS * B_HD
WINDOW = 128
ROPE_THETA = 10000.0
MIX_WIDTH = A_WIDTH + B_WIDTH
DEEPNORM_ALPHA = (2 * DEPTH) ** 0.25
LOG2E = 1.4426950408889634
LN_EPS = 1e-5
RMS_EPS = 1e-6
L2_EPS = 1e-6

OFF_A_Z = A_QKV
OFF_A_BETA = OFF_A_Z + A_WIDTH
OFF_A_DECAY = OFF_A_BETA + A_HEADS
OFF_B_Q = OFF_A_DECAY + A_HEADS
OFF_B_K = OFF_B_Q + B_WIDTH
OFF_B_V = OFF_B_K + B_KV_WIDTH
OFF_B_Z = OFF_B_V + B_KV_WIDTH
PROJ_COLS = OFF_B_Z + B_WIDTH

LANES = 128
C_QKV = 0
C_ZA = C_QKV + A_QKV
C_QB = C_ZA + A_WIDTH
C_KB = C_QB + B_WIDTH
C_VB = C_KB + B_KV_WIDTH
C_ZB = C_VB + B_KV_WIDTH
C_BD = C_ZB + B_WIDTH
WPREP_TN = 256
W_COLS = C_BD + WPREP_TN

VMEM_LIMIT = 56 * 1024 * 1024

ADA_TN = 1536
PROJ_TM = 512
PROJ_CW = 256
PROJ_PART_ROWS = (128, 256, 128)
DELTA_CT = 256
DELTA_WAVE = 2
SWA_TQ = 512
SWA_WAVE = 2
STEP_BT = 16


def _dot(a, b):
    return jnp.dot(a, b, preferred_element_type=F32)


def _dot_nt(a, b):
    return lax.dot_general(a, b, (((1,), (1,)), ((), ())), preferred_element_type=F32)


def _silu(x):
    return x * jax.nn.sigmoid(x)


def _softplus(x):
    return jnp.maximum(x, 0.0) + jnp.log1p(jnp.exp(-jnp.abs(x)))


def _lane(shape):
    return lax.broadcasted_iota(jnp.int32, shape, len(shape) - 1)


def _l2norm_heads(y, scale):
    outs = []
    for h in range(y.shape[1] // A_DK):
        xh = y[:, h * A_DK:(h + 1) * A_DK]
        ss = jnp.sum(xh * xh, axis=-1, keepdims=True)
        xn = xh * lax.rsqrt(ss + L2_EPS)
        outs.append(xn * scale if scale != 1.0 else xn)
    return jnp.concatenate(outs, axis=-1)


def _rotary_group(xg, cos, sin_signed):
    lane = _lane(xg.shape)
    swapped = jnp.where((lane % B_HD) < (B_HD // 2),
                        pltpu.roll(xg, LANES - B_HD // 2, axis=1),
                        pltpu.roll(xg, B_HD // 2, axis=1))
    return xg * cos + swapped * sin_signed


def _kv_layouts(kb, vb):
    low = _lane(kb.shape) < B_HD
    kbr = pltpu.roll(kb, B_HD, axis=1)
    vbr = pltpu.roll(vb, B_HD, axis=1)
    return kb, kbr, jnp.where(low, vb, vbr), jnp.where(low, vbr, vb)


def _gate_lanes(bd, alog_row, dt_row):
    lane = _lane(bd.shape)
    g = -jnp.exp(alog_row) * _softplus(bd + dt_row)
    return jnp.where(lane < A_HEADS, jax.nn.sigmoid(bd), g)


def _layer_norm(r, g, b):
    mu = jnp.mean(r, axis=-1, keepdims=True)
    d = r - mu
    var = jnp.mean(d * d, axis=-1, keepdims=True)
    return d * lax.rsqrt(var + LN_EPS) * g + b


def _wprep_kernel(wa_ref, wb_ref, o_ref):
    tn = wa_ref.shape[0]
    o_ref[:, 0:tn] = wa_ref[...].T.astype(BF16)
    xb = wb_ref[...]
    tail = pl.program_id(0) == pl.num_programs(0) - 1
    row = lax.broadcasted_iota(jnp.int32, xb.shape, 0)
    xb = jnp.where(jnp.logical_and(tail, row >= 2 * A_HEADS), 0.0, xb)
    o_ref[:, tn:2 * tn] = xb.T.astype(BF16)


def _wprep(w_t):
    tn = WPREP_TN
    n_a, n_b = OFF_A_BETA // tn, (PROJ_COLS - OFF_B_Q) // tn
    assert n_a * tn == OFF_A_BETA and n_b * tn == PROJ_COLS - OFF_B_Q and OFF_A_BETA + tn <= PROJ_COLS

    assert (n_a + n_b + 1) % 2 == 0

    def src_row(j):
        return jnp.where(j < n_a, j * tn, jnp.where(j < n_a + n_b, OFF_B_Q + (j - n_a) * tn, OFF_A_BETA))

    src = lambda k: pl.BlockSpec((pl.Element(tn), pl.Element(D_MODEL)),
                                 lambda j: (pl.multiple_of(src_row(2 * j + k), 8), 0))
    return pl.pallas_call(
        _wprep_kernel,
        grid=((n_a + n_b + 1) // 2,),
        in_specs=[src(0), src(1)],
        out_specs=pl.BlockSpec((D_MODEL, 2 * tn), lambda j: (0, j)),
        out_shape=jax.ShapeDtypeStruct((D_MODEL, W_COLS), BF16),
        compiler_params=pltpu.CompilerParams(dimension_semantics=("arbitrary",),
                                             vmem_limit_bytes=VMEM_LIMIT),
        name="wprep",
    )(w_t, w_t)


def _ada_kernel(cs_ref, cp_ref, w_ref, b_ref, o_ref):
    n_s, n_p = cs_ref.shape[0], cp_ref.shape[0]
    w = w_ref[...].astype(BF16)
    o_ref[0:n_s, :] = _dot(cs_ref[...].astype(BF16), w) + b_ref[...]
    o_ref[n_s:n_s + 8, :] = jnp.zeros((8, o_ref.shape[1]), F32)
    o_ref[n_s:n_s + n_p, :] = _dot(cp_ref[...].astype(BF16), w) + b_ref[...]


def _ada(c_sample, c_prompt, w_ada, b_ada):
    n_s, n_p = c_sample.shape[0], c_prompt.shape[0]
    rows = n_s + 8
    tn = ADA_TN
    return pl.pallas_call(
        _ada_kernel,
        grid=(3 * D_MODEL // tn,),
        in_specs=[pl.BlockSpec((n_s, D_MODEL), lambda j: (0, 0)),
                  pl.BlockSpec((n_p, D_MODEL), lambda j: (0, 0)),
                  pl.BlockSpec((D_MODEL, tn), lambda j: (0, j)),
                  pl.BlockSpec((1, tn), lambda j: (0, j))],
        out_specs=pl.BlockSpec((rows, tn), lambda j: (0, j)),
        out_shape=jax.ShapeDtypeStruct((rows, 3 * D_MODEL), F32),
        compiler_params=pltpu.CompilerParams(dimension_semantics=("arbitrary",),
                                             vmem_limit_bytes=VMEM_LIMIT),
        name="ada",
    )(c_sample, c_prompt, w_ada, b_ada)


def _proj_kernel(x_ref, xn_ref, mod_ref, w_ref, cw_ref, alog_ref, dt_ref, cos_ref, sin_ref,
                 qkv_ref, za_ref, gb_ref, qb_ref, kvl_ref, zb_ref, cst_ref, kbl_ref, vbl_ref, ubuf, head):
    tm = x_ref.shape[1]
    t = pl.program_id(1)

    brow = pl.ds(pl.program_id(0), 1)
    shift = mod_ref[brow, 0:D_MODEL]
    scale = mod_ref[brow, D_MODEL:2 * D_MODEL]

    assert sum(PROJ_PART_ROWS) == tm and PROJ_PART_ROWS[-1] >= WINDOW
    cw = PROJ_CW
    pieces = [slice(c0, c0 + cw) for c0 in range(0, A_QKV, cw)]
    wide = (slice(C_ZA, C_ZA + A_WIDTH), slice(C_BD, C_BD + LANES), slice(C_QB, C_QB + B_WIDTH),
            slice(C_KB, C_KB + 2 * LANES), slice(C_ZB, C_ZB + B_WIDTH))

    def tiles(x_rows):
        h = (x_rows * (1.0 + scale) + shift).astype(BF16)
        return ([_dot(h, w_ref[:, cs]) for cs in pieces],) + tuple(_dot(h, w_ref[:, cs]) for cs in wide)

    def matmuls(r0, rp):
        return tiles(x_ref[0, r0:r0 + rp, :])

    def keep(results):
        for cs, val in zip(pieces + list(wide), list(results[0]) + list(results[1:])):
            head[:, cs] = val

    def kept():
        return ([head[:, cs] for cs in pieces],) + tuple(head[:, cs] for cs in wide)

    @pl.when(t == 0)
    def _():
        ubuf[...] = jnp.zeros(ubuf.shape, F32)
        keep(matmuls(0, PROJ_PART_ROWS[0]))

    def epilogue(r0, rp, results):
        rs = slice(r0, r0 + rp)
        us, za, ubd, uq, ukv, zb = results
        sub = lax.broadcasted_iota(jnp.int32, (rp // 8, 8, cw), 1)

        def conv_epilogue(cs, u):
            gi = cs.start // A_WIDTH
            groups = jnp.concatenate([ubuf[:, cs], u], axis=0).reshape(rp // 8 + 1, 8, cw)
            acc = None
            for j in range(CONV_W - 1, 0, -1):
                rot = pltpu.roll(groups, j, axis=1)
                term = (jnp.where(sub < j, rot[:-1], rot[1:]).reshape(rp, cw)
                        * cw_ref[CONV_W - 1 - j:CONV_W - j, cs])
                acc = term if acc is None else acc + term
            y = _silu(acc + u * cw_ref[CONV_W - 1:CONV_W, cs])
            if gi == 0:
                y = _l2norm_heads(y, A_DK ** -0.5)
            elif gi == 1:
                y = _l2norm_heads(y, 1.0)
            qkv_ref[0, rs, cs] = y
            ubuf[:, cs] = u[rp - 8:rp]
            if r0 + rp == tm:
                cst_ref[0, :, cs] = u[rp - (CONV_W - 1):rp]

        for cs, u in zip(pieces, us):
            conv_epilogue(cs, u)
        za_ref[0, rs, :] = za
        zb_ref[0, rs, :] = zb
        gb_ref[0, rs, :] = _gate_lanes(ubd, alog_ref[...], dt_ref[...])
        cos = cos_ref[rs, :]
        sin = sin_ref[rs, :]
        for g in range(B_WIDTH // LANES):
            qb_ref[0, rs, g * LANES:(g + 1) * LANES] = (
                _rotary_group(uq[:, g * LANES:(g + 1) * LANES], cos, sin) * (B_HD ** -0.5 * LOG2E)).astype(BF16)
        kb = _rotary_group(ukv[:, 0:LANES], cos, sin)
        vb = ukv[:, LANES:2 * LANES]
        for j, val in enumerate(_kv_layouts(kb, vb)):
            kvl_ref[0, rs, j * LANES:(j + 1) * LANES] = val.astype(BF16)
        return kb, vb

    parts, r0 = [], 0
    for rp in PROJ_PART_ROWS:
        parts.append((r0, rp))
        r0 += rp
    pending = kept()
    for p, part in enumerate(parts):
        results = pending
        pending = matmuls(*parts[p + 1]) if p + 1 < len(parts) else tiles(xn_ref[0])
        kb, vb = epilogue(*part, results)
    keep(pending)

    @pl.when(t == pl.num_programs(1) - 1)
    def _():
        kbl_ref[0] = kb[kb.shape[0] - WINDOW:, :]
        vbl_ref[0] = vb[vb.shape[0] - WINDOW:, :]


def _proj(x, mod, mod_row0, w_r, conv_w, alog_row, dt_row, cos_t, sin_t):
    bsz, t, _ = x.shape
    tm = PROJ_TM
    n0 = PROJ_PART_ROWS[0]
    assert tm % n0 == 0
    row = lambda w: pl.BlockSpec((1, tm, w), lambda b, i: (b, i, 0))
    const2 = lambda s: pl.BlockSpec(s, lambda b, i: (0, 0))
    per_b = lambda r, w: pl.BlockSpec((1, r, w), lambda b, i: (b, 0, 0))
    wide = lambda w, dt=F32: jax.ShapeDtypeStruct((bsz, t, w), dt)
    return pl.pallas_call(
        _proj_kernel,
        grid=(bsz, t // tm),
        in_specs=[row(D_MODEL),
                  pl.BlockSpec((1, n0, D_MODEL), lambda b, i: (b, jnp.minimum((i + 1) * (tm // n0), t // n0 - 1), 0)),
                  pl.BlockSpec((8, 3 * D_MODEL), lambda b, i: (mod_row0 // 8, 0)),
                  const2((D_MODEL, W_COLS)),
                  const2((CONV_W, A_QKV)),
                  const2((1, LANES)), const2((1, LANES)),
                  pl.BlockSpec((tm, LANES), lambda b, i: (i, 0)),
                  pl.BlockSpec((tm, LANES), lambda b, i: (i, 0))],
        out_specs=[row(A_QKV), row(A_WIDTH), row(LANES), row(B_WIDTH), row(4 * LANES), row(B_WIDTH),
                   per_b(CONV_W - 1, A_QKV), per_b(WINDOW, LANES), per_b(WINDOW, LANES)],
        out_shape=[wide(A_QKV), wide(A_WIDTH), wide(LANES), wide(B_WIDTH, BF16), wide(4 * LANES, BF16),
                   wide(B_WIDTH),
                   jax.ShapeDtypeStruct((bsz, CONV_W - 1, A_QKV), F32),
                   jax.ShapeDtypeStruct((bsz, WINDOW, LANES), F32),
                   jax.ShapeDtypeStruct((bsz, WINDOW, LANES), F32)],
        scratch_shapes=[pltpu.VMEM((8, A_QKV), F32), pltpu.VMEM((n0, W_COLS), F32)],
        compiler_params=pltpu.CompilerParams(dimension_semantics=("arbitrary", "arbitrary"),
                                             vmem_limit_bytes=VMEM_LIMIT),
        name="proj",
    )(x, x, mod, w_r, conv_w, alog_row, dt_row, cos_t, sin_t)


def _delta_kernel(q_ref, k_ref, v_ref, gb_ref, za_ref, na_ref, oa_ref, st_ref,
                  s_scr, wq_s, ut_s, akd_s, gl_s):
    bsz, ct = q_ref.shape[0], q_ref.shape[1]
    nch = ct // CHUNK
    t = pl.program_id(0)
    wslot = t % 2
    rslot = 1 - wslot

    @pl.when(t == 0)
    def _():
        s_scr[...] = jnp.zeros(s_scr.shape, F32)
        wq_s[...] = jnp.zeros(wq_s.shape, BF16)
        ut_s[...] = jnp.zeros(ut_s.shape, F32)
        akd_s[...] = jnp.zeros(akd_s.shape, BF16)
        gl_s[...] = jnp.zeros(gl_s.shape, F32)

    units = [(b, c, h) for b in range(bsz) for c in range(nch) for h in range(A_HEADS)]
    uid = {u_: i for i, u_ in enumerate(units)}
    rows = lambda c: slice(c * CHUNK, (c + 1) * CHUNK)
    lanes = lambda h: slice(h * A_DK, (h + 1) * A_DK)
    na = na_ref[...]

    s_cur = {(b, h): s_scr[b * A_HEADS + h] for b in range(bsz) for h in range(A_HEADS)}
    ws, uu = {}, {}

    def rec_ws(c):
        for b in range(bsz):
            for h in range(A_HEADS):
                i = uid[b, c, h]
                ws[b, h] = _dot(wq_s[rslot, i], s_cur[b, h].astype(BF16))
                uu[b, h] = (ut_s[rslot, i] - ws[b, h][:CHUNK]).astype(BF16)

    def rec_ou(c):
        zpad = jnp.zeros((CHUNK, A_DV), BF16)
        for b in range(bsz):
            u_bd = jnp.concatenate(
                [jnp.concatenate([uu[b, h] if hh == h else zpad for hh in range(A_HEADS)], axis=-1)
                 for h in range(A_HEADS)], axis=0)
            ou = _dot(akd_s[rslot, b * nch + c], u_bd)
            for h in range(A_HEADS):
                o = ws[b, h][CHUNK:] + ou[:CHUNK, lanes(h)]
                s_cur[b, h] = gl_s[rslot, uid[b, c, h]] * s_cur[b, h] + ou[CHUNK:, lanes(h)]
                on = o * lax.rsqrt(jnp.mean(o * o, axis=-1, keepdims=True) + RMS_EPS) * na
                oa_ref[b, rows(c), lanes(h)] = (on * _silu(za_ref[b, rows(c), lanes(h)])).astype(BF16)

    rec_stages = []
    for c in range(nch):
        rec_stages += [lambda c=c: rec_ws(c), lambda c=c: rec_ou(c)]

    def run_rec(n_left_after):
        while rec_stages and len(rec_stages) > n_left_after:
            rec_stages.pop(0)()

    pk = A_HEADS * CHUNK
    low = _lane((CHUNK, LANES)) < CHUNK
    low_row = _lane((1, LANES)) < CHUNK
    ti_p = lax.broadcasted_iota(jnp.int32, (CHUNK, pk), 0)
    ii_p = _lane((CHUNK, pk)) % CHUNK
    zero64 = jnp.zeros((CHUNK, LANES), BF16)

    def pack(parts):
        return jnp.concatenate([jnp.where(low, parts[0], parts[1]), jnp.where(low, parts[2], parts[3])], axis=-1)

    def block_diag(x16):
        blocks = []
        for h in range(A_HEADS):
            pair, first = h // 2, h % 2 == 0
            piece = jnp.where(low if first else jnp.logical_not(low), x16[:, pair * LANES:(pair + 1) * LANES], zero64)
            blocks.append(jnp.concatenate([piece, zero64] if pair == 0 else [zero64, piece], axis=-1))
        return jnp.concatenate(blocks, axis=0)

    zrhs = jnp.zeros((CHUNK, 2 * A_DK), BF16)
    n_rec = len(rec_stages)
    n_slots = 8 * (bsz // DELTA_WAVE)
    done = [0]

    def stage_done():
        done[0] += 1
        run_rec(n_rec - 1 - (done[0] * n_rec) // n_slots)

    def decay_terms(b, beta, g_col, g_last, eg, dec_p, beta_p):
        gbv = gb_ref[b]
        rin = lax.broadcasted_iota(jnp.int32, gbv.shape, 0) % CHUNK
        gcs = gbv
        s = 1
        while s < CHUNK:
            gcs = gcs + jnp.where(rin >= s, pltpu.roll(gcs, s, axis=0), 0.0)
            s *= 2
        gcs_t = gcs.T
        for c in range(nch):
            r0 = c * CHUNK
            pair_lanes = slice((c // 2) * LANES, (c // 2 + 1) * LANES)
            g_rows = []
            for h in range(A_HEADS):
                u_ = (b, c, h)
                beta[u_] = jnp.broadcast_to(gbv[rows(c), h:h + 1], (CHUNK, A_DK))
                g_col[u_] = jnp.broadcast_to(gcs[rows(c), A_HEADS + h:A_HEADS + h + 1], (CHUNK, A_DK))
                g_last[u_] = gcs[r0 + CHUNK - 1:r0 + CHUNK, A_HEADS + h:A_HEADS + h + 1]
                eg[u_] = jnp.exp(g_col[u_])
                g_row = gcs_t[A_HEADS + h:A_HEADS + h + 1, pair_lanes]
                g_rows.append(g_row if c % 2 == h % 2 else pltpu.roll(g_row, CHUNK, axis=1))
            g_row_p = jnp.concatenate([jnp.where(low_row, g_rows[0], g_rows[1]),
                                       jnp.where(low_row, g_rows[2], g_rows[3])], axis=-1)
            g_col_p = pack([g_col[b, c, h] for h in range(A_HEADS)])
            dec_p[b, c] = jnp.exp(jnp.where(ti_p >= ii_p, g_col_p - g_row_p, -jnp.inf))
            beta_p[b, c] = pack([beta[b, c, h] for h in range(A_HEADS)])

    def prepare(bs):
        groups_b = [(b, c) for b in bs for c in range(nch)]
        beta, g_col, g_last, eg, dec_p, beta_p = {}, {}, {}, {}, {}, {}
        for b in bs:
            decay_terms(b, beta, g_col, g_last, eg, dec_p, beta_p)

        nmat = {}
        for (b, c) in groups_b:
            k16 = k_ref[b, rows(c), :].astype(BF16)
            q16 = q_ref[b, rows(c), :].astype(BF16)
            k_heads = jnp.concatenate(
                [jnp.concatenate([k16[:, lanes(h)] if hh == h else zero64 for hh in range(A_HEADS)], axis=-1)
                 for h in range(A_HEADS)], axis=0)
            kq = _dot_nt(jnp.concatenate([k16, q16], axis=0), k_heads)
            nmat[b, c] = -(beta_p[b, c] * kq[:CHUNK] * jnp.where(ti_p > ii_p, dec_p[b, c], 0.0))
            akd_s[wslot, b * nch + c, 0:CHUNK, :] = (kq[CHUNK:] * dec_p[b, c]).astype(BF16)
        stage_done()

        rsum = dict(nmat)
        pw16 = {g_: nmat[g_].astype(BF16) for g_ in groups_b}
        pw = {g_: _dot(pw16[g_], block_diag(pw16[g_])) for g_ in groups_b}
        stage_done()
        for step in range(1, 6):
            last = step == 5
            pw16 = {g_: pw[g_].astype(BF16) for g_ in groups_b}
            rp = {}
            for g_ in groups_b:
                r16 = rsum[g_].astype(BF16)
                rp[g_] = _dot(r16 if last else jnp.concatenate([r16, pw16[g_]], axis=0), block_diag(pw16[g_]))
            for g_ in groups_b:
                rsum[g_] = rsum[g_] + pw[g_] + rp[g_][:CHUNK]
                if not last:
                    pw[g_] = rp[g_][CHUNK:]
            stage_done()

        for (b, c) in groups_b:
            for h in range(A_HEADS):
                u_ = (b, c, h)
                i = uid[u_]
                kc = k_ref[b, rows(c), lanes(h)]
                rhs = jnp.concatenate([(beta[u_] * eg[u_]) * kc, beta[u_] * v_ref[b, rows(c), lanes(h)]],
                                      axis=-1)
                rhs16 = rhs.astype(BF16)
                rhs_rows = jnp.concatenate([rhs16 if hh == h else zrhs for hh in range(A_HEADS)], axis=0)
                sol = rhs + _dot(rsum[b, c].astype(BF16), rhs_rows)
                wq_s[wslot, i] = jnp.concatenate([sol[:, :A_DK], eg[u_] * q_ref[b, rows(c), lanes(h)]],
                                                 axis=0).astype(BF16)
                ut_s[wslot, i] = sol[:, A_DK:]
                gl_s[wslot, i] = jnp.broadcast_to(jnp.exp(g_last[u_]), (1, A_DV))
        for (b, c) in groups_b:
            kd = [jnp.exp(g_last[b, c, h] - g_col[b, c, h]) * k_ref[b, rows(c), lanes(h)]
                  for h in range(A_HEADS)]
            for p in range(A_HEADS // 2):
                akd_s[wslot, b * nch + c, CHUNK:, p * LANES:(p + 1) * LANES] = (
                    jnp.concatenate([kd[2 * p], kd[2 * p + 1]], axis=0).T.astype(BF16))
        stage_done()

    run_rec(n_rec - 1)
    for b0 in range(0, bsz, DELTA_WAVE):
        prepare(range(b0, b0 + DELTA_WAVE))
    run_rec(0)

    for b in range(bsz):
        for h in range(A_HEADS):
            s_scr[b * A_HEADS + h] = s_cur[b, h]

    @pl.when(t == pl.num_programs(0) - 1)
    def _():
        for b in range(bsz):
            for h in range(A_HEADS):
                st_ref[b, h] = s_cur[b, h]


def _delta(qkv, gb, za, na_row):
    bsz, t, _ = qkv.shape
    ct = DELTA_CT
    nt = t // ct
    n_units = bsz * (ct // CHUNK) * A_HEADS
    prep = lambda w, j=0: pl.BlockSpec((bsz, ct, w), lambda i: (0, jnp.minimum(i, nt - 1), j))
    rec = lambda w: pl.BlockSpec((bsz, ct, w), lambda i: (0, jnp.maximum(i - 1, 0), 0))
    return pl.pallas_call(
        _delta_kernel,
        grid=(nt + 1,),
        in_specs=[prep(A_WIDTH, 0), prep(A_WIDTH, 1), prep(A_WIDTH, 2), prep(LANES), rec(A_WIDTH),
                  pl.BlockSpec((1, A_DV), lambda i: (0, 0))],
        out_specs=[rec(A_WIDTH),
                   pl.BlockSpec((bsz, A_HEADS, A_DK, A_DV), lambda i: (0, 0, 0, 0))],
        out_shape=[jax.ShapeDtypeStruct((bsz, t, A_WIDTH), BF16),
                   jax.ShapeDtypeStruct((bsz, A_HEADS, A_DK, A_DV), F32)],
        scratch_shapes=[pltpu.VMEM((bsz * A_HEADS, A_DK, A_DV), F32),
                        pltpu.VMEM((2, n_units, 2 * CHUNK, A_DK), BF16),
                        pltpu.VMEM((2, n_units, CHUNK, A_DV), F32),
                        pltpu.VMEM((2, n_units // A_HEADS, CHUNK + A_DK, A_HEADS * CHUNK), BF16),
                        pltpu.VMEM((2, n_units, 1, A_DV), F32)],
        compiler_params=pltpu.CompilerParams(dimension_semantics=("arbitrary",),
                                             vmem_limit_bytes=VMEM_LIMIT),
        name="delta",
    )(qkv, qkv, qkv, gb, za, na_row)


def _swa_out_kernel(sink_ref, qb_ref, kc_ref, kp_ref, krc_ref, krp_ref, v0c_ref, v0p_ref, v1c_ref, v1p_ref,
                    zb_ref, oa_ref, x_ref, gate_ref, w_ref, g_ref, b_ref, y_ref):
    n = pl.program_id(1)
    gate = gate_ref[pl.ds(pl.program_id(0), 1), :]
    tq = qb_ref.shape[1]
    blk = WINDOW
    kx = (jnp.concatenate([kp_ref[0], kc_ref[0]], axis=0), jnp.concatenate([krp_ref[0], krc_ref[0]], axis=0))
    vd = (jnp.concatenate([v0p_ref[0], v0c_ref[0]], axis=0), jnp.concatenate([v1p_ref[0], v1c_ref[0]], axis=0))

    a = lax.broadcasted_iota(jnp.int32, (2 * blk, 2 * blk), 0) % blk
    j = lax.broadcasted_iota(jnp.int32, (2 * blk, 2 * blk), 1)
    rel = a + blk - j
    band = (rel >= 0) & (rel <= WINDOW)
    band_first = band & ((n > 0) | (j >= blk))
    top = lax.broadcasted_iota(jnp.int32, (2 * blk, 1), 0) < blk
    low = _lane((blk, LANES)) < B_HD
    zero = jnp.zeros((blk, LANES), BF16)

    qrows = lambda i: slice(i * blk, (i + 1) * blk)
    krows = lambda i: slice(i * blk, (i + 2) * blk)
    sink = {(kh, half): jnp.where(top, sink_ref[kh * B_GROUP + half] * LOG2E,
                                  sink_ref[kh * B_GROUP + half + 2] * LOG2E)
            for kh in range(B_KV_HEADS) for half in range(2)}
    def wave_units(i0):
        blocks = range(i0, i0 + SWA_WAVE)
        return blocks, [(i, kh, half) for i in blocks for kh in range(B_KV_HEADS) for half in range(2)]

    def score_matmuls(i0):
        blocks, units = wave_units(i0)
        mix_a = {i: _dot(oa_ref[0, qrows(i), :], w_ref[0:A_WIDTH, :]) for i in blocks}
        sc = {}
        for (i, kh, half) in units:
            qs = []
            for g in range(2):
                grp = kh * 2 + g
                xg = qb_ref[0, qrows(i), grp * LANES:(grp + 1) * LANES]
                qs.append(jnp.where(low if half == 0 else jnp.logical_not(low), xg, zero))
            qz = jnp.concatenate(qs, axis=0)
            sc[i, kh, half] = _dot_nt(qz, kx[0 if kh == half else 1][krows(i)])
        return mix_a, sc

    def finish(i0, mix_a, sc):
        blocks, units = wave_units(i0)
        p, den = {}, {}
        for u_ in units:
            i, kh, half = u_
            s_m = jnp.where(band_first if i == 0 else band, sc[u_], -jnp.inf)
            m = jnp.maximum(jnp.max(s_m, axis=-1, keepdims=True), sink[kh, half])
            e = jnp.exp2(s_m - m)
            den[u_] = jnp.sum(e, axis=-1, keepdims=True) + jnp.exp2(sink[kh, half] - m)
            p[u_] = e.astype(BF16)
        pv = {u_: _dot(p[u_], vd[u_[1]][krows(u_[0])]) for u_ in units}
        outs = {u_: pv[u_] / den[u_] for u_ in units}
        for i in blocks:
            ob = []
            for grp in range(B_WIDTH // LANES):
                kh, g = grp // 2, grp % 2
                og = jnp.where(low, outs[i, kh, 0][g * blk:(g + 1) * blk], outs[i, kh, 1][g * blk:(g + 1) * blk])
                ob.append((og * _silu(zb_ref[0, qrows(i), grp * LANES:(grp + 1) * LANES])).astype(BF16))
            mix = mix_a[i] + _dot(jnp.concatenate(ob, axis=-1), w_ref[A_WIDTH:MIX_WIDTH, :])
            r = DEEPNORM_ALPHA * x_ref[0, qrows(i), :] + (1.0 + gate) * mix
            y_ref[0, qrows(i), :] = _layer_norm(r, g_ref[...], b_ref[...])

    starts = list(range(0, tq // blk, SWA_WAVE))
    pending = score_matmuls(starts[0])
    for w, i0 in enumerate(starts):
        ready = pending
        if w + 1 < len(starts):
            pending = score_matmuls(starts[w + 1])
        finish(i0, *ready)


def _swa_out(sinks, qb, kvl, zb, oa, x, mod, mod_row0, w_out, ln_g, ln_b):
    bsz, t, _ = qb.shape
    tq = SWA_TQ
    per = tq // WINDOW
    cur = lambda w: pl.BlockSpec((1, tq, w), lambda b, i: (b, i, 0))
    kv_cur = lambda j: pl.BlockSpec((1, tq, LANES), lambda b, i: (b, i, j))
    kv_prev = lambda j: pl.BlockSpec((1, WINDOW, LANES), lambda b, i: (b, jnp.maximum(i * per - 1, 0), j))
    const2 = lambda s: pl.BlockSpec(s, lambda b, i: (0, 0))
    return pl.pallas_call(
        _swa_out_kernel,
        grid=(bsz, t // tq),
        in_specs=[pl.BlockSpec(memory_space=pltpu.SMEM), cur(B_WIDTH),
                  kv_cur(0), kv_prev(0), kv_cur(1), kv_prev(1), kv_cur(2), kv_prev(2), kv_cur(3), kv_prev(3),
                  cur(B_WIDTH), cur(A_WIDTH), cur(D_MODEL),
                  pl.BlockSpec((8, D_MODEL), lambda b, i: (mod_row0 // 8, 2)),
                  const2((MIX_WIDTH, D_MODEL)), const2((1, D_MODEL)), const2((1, D_MODEL))],
        out_specs=cur(D_MODEL),
        out_shape=jax.ShapeDtypeStruct((bsz, t, D_MODEL), F32),
        compiler_params=pltpu.CompilerParams(dimension_semantics=("arbitrary", "arbitrary"),
                                             vmem_limit_bytes=VMEM_LIMIT),
        name="swa_out",
    )(sinks, qb, kvl, kvl, kvl, kvl, kvl, kvl, kvl, kvl, zb, oa, x, mod, w_out, ln_g, ln_b)


def _out_kernel(oa_ref, ob_ref, x_ref, gate_ref, w_ref, g_ref, b_ref, y_ref):
    mix = _dot(oa_ref[...], w_ref[0:A_WIDTH, :]) + _dot(ob_ref[...], w_ref[A_WIDTH:MIX_WIDTH, :])
    r = DEEPNORM_ALPHA * x_ref[...] + (1.0 + gate_ref[...]) * mix
    y_ref[...] = _layer_norm(r, g_ref[...], b_ref[...])


def _out(oa, ob, x, mod, w_out, ln_g, ln_b):
    n = x.shape[0]
    full = lambda s: pl.BlockSpec(s, lambda i: (0, 0))
    return pl.pallas_call(
        _out_kernel,
        grid=(1,),
        in_specs=[full((n, A_WIDTH)), full((n, B_WIDTH)), full((n, D_MODEL)),
                  pl.BlockSpec((n, D_MODEL), lambda i: (0, 2)),
                  full((MIX_WIDTH, D_MODEL)), full((1, D_MODEL)), full((1, D_MODEL))],
        out_specs=full((n, D_MODEL)),
        out_shape=jax.ShapeDtypeStruct((n, D_MODEL), F32),
        compiler_params=pltpu.CompilerParams(dimension_semantics=("arbitrary",),
                                             vmem_limit_bytes=VMEM_LIMIT),
        name="out",
    )(oa, ob, x, mod, w_out, ln_g, ln_b)


def _sproj_kernel(x_ref, mod_ref, w_ref, cw_ref, cst_ref, alog_ref, dt_ref, cos_ref, sin_ref,
                  q_ref, k_ref, v_ref, za_ref, gb_ref, qb_ref, kb_ref, vb_ref, zb_ref, ncs_ref):
    shift = mod_ref[:, 0:D_MODEL]
    scale = mod_ref[:, D_MODEL:2 * D_MODEL]
    h = (x_ref[...] * (1.0 + scale) + shift).astype(BF16)

    for gi, o_ref in enumerate((q_ref, k_ref, v_ref)):
        c0 = gi * A_WIDTH
        cs = slice(c0, c0 + A_WIDTH)
        u = _dot(h, w_ref[:, cs])
        acc = cst_ref[0, :, cs] * cw_ref[0:1, cs]
        acc = acc + cst_ref[1, :, cs] * cw_ref[1:2, cs]
        acc = acc + cst_ref[2, :, cs] * cw_ref[2:3, cs]
        acc = acc + u * cw_ref[3:4, cs]
        y = _silu(acc)
        if gi == 0:
            y = _l2norm_heads(y, A_DK ** -0.5)
        elif gi == 1:
            y = _l2norm_heads(y, 1.0)
        o_ref[...] = y
        ncs_ref[0, :, cs] = cst_ref[1, :, cs]
        ncs_ref[1, :, cs] = cst_ref[2, :, cs]
        ncs_ref[2, :, cs] = u

    za_ref[...] = _dot(h, w_ref[:, C_ZA:C_ZA + A_WIDTH])
    gb_ref[...] = _gate_lanes(_dot(h, w_ref[:, C_BD:C_BD + LANES]), alog_ref[...], dt_ref[...])

    cos = cos_ref[...]
    sin = sin_ref[...]
    uq = _dot(h, w_ref[:, C_QB:C_QB + B_WIDTH])
    for g in range(B_WIDTH // LANES):
        qb_ref[:, g * LANES:(g + 1) * LANES] = (
            _rotary_group(uq[:, g * LANES:(g + 1) * LANES], cos, sin) * (B_HD ** -0.5))
    kb_ref[...] = _rotary_group(_dot(h, w_ref[:, C_KB:C_KB + LANES]), cos, sin)
    vb_ref[...] = _dot(h, w_ref[:, C_VB:C_VB + LANES])
    zb_ref[...] = _dot(h, w_ref[:, C_ZB:C_ZB + B_WIDTH])


def _sproj(x, mod_s, w_r, conv_w, cst, alog_row, dt_row, cos_row, sin_row):
    n = x.shape[0]
    full = lambda s: pl.BlockSpec(s, lambda i: (0,) * len(s))
    wide = lambda w: jax.ShapeDtypeStruct((n, w), F32)
    return pl.pallas_call(
        _sproj_kernel,
        grid=(1,),
        in_specs=[full((n, D_MODEL)), pl.BlockSpec((n, 3 * D_MODEL), lambda i: (0, 0)),
                  full((D_MODEL, W_COLS)),
                  full((CONV_W, A_QKV)), full((CONV_W - 1, n, A_QKV)),
                  full((1, LANES)), full((1, LANES)), full((1, LANES)), full((1, LANES))],
        out_specs=[full((n, A_WIDTH)), full((n, A_WIDTH)), full((n, A_WIDTH)), full((n, A_WIDTH)),
                   full((n, LANES)), full((n, B_WIDTH)), full((n, LANES)), full((n, LANES)),
                   full((n, B_WIDTH)), full((CONV_W - 1, n, A_QKV))],
        out_shape=[wide(A_WIDTH), wide(A_WIDTH), wide(A_WIDTH), wide(A_WIDTH), wide(LANES),
                   wide(B_WIDTH), wide(LANES), wide(LANES), wide(B_WIDTH),
                   jax.ShapeDtypeStruct((CONV_W - 1, n, A_QKV), F32)],
        compiler_params=pltpu.CompilerParams(dimension_semantics=("arbitrary",),
                                             vmem_limit_bytes=VMEM_LIMIT),
        name="sproj",
    )(x, mod_s, w_r, conv_w, cst, alog_row, dt_row, cos_row, sin_row)


def _sstep_kernel(sink_ref, q_ref, k_ref, v_ref, gb_ref, za_ref, na_ref, st_ref,
                  qb_ref, kn_ref, vn_ref, zb_ref, ck_ref, cv_ref,
                  oa_ref, ob_ref, nst_ref, nck_ref, ncv_ref,
                  o_scr, ob_scr):
    bt = q_ref.shape[0]
    gbv = gb_ref[...]

    pick = (lax.broadcasted_iota(jnp.int32, (bt, bt * A_DV), 1) // A_DV
            == lax.broadcasted_iota(jnp.int32, (bt, bt * A_DV), 0))
    pick = jnp.where(pick, 1.0, 0.0).astype(BF16)
    for h in range(A_HEADS):
        hs = slice(h * A_DK, (h + 1) * A_DK)
        q_rep = _dot(q_ref[:, hs].T.astype(BF16), pick)
        k_rep = _dot(k_ref[:, hs].T.astype(BF16), pick)
        for bb in range(bt):
            eg = jnp.exp(gbv[bb:bb + 1, A_HEADS + h:A_HEADS + h + 1])
            beta = gbv[bb:bb + 1, h:h + 1]
            kcol = k_rep[:, bb * A_DV:(bb + 1) * A_DV]
            qcol = q_rep[:, bb * A_DV:(bb + 1) * A_DV]
            s1 = eg * st_ref[bb, h]
            pred = jnp.sum(kcol * s1, axis=0, keepdims=True)
            upd = beta * (v_ref[bb:bb + 1, hs] - pred)
            s2 = s1 + kcol * upd
            nst_ref[bb, h] = s2
            o_scr[bb:bb + 1, hs] = jnp.sum(qcol * s2, axis=0, keepdims=True)
    na = na_ref[...]
    for h in range(A_HEADS):
        hs = slice(h * A_DK, (h + 1) * A_DK)
        o = o_scr[:, hs]
        on = o * lax.rsqrt(jnp.mean(o * o, axis=-1, keepdims=True) + RMS_EPS) * na
        oa_ref[:, hs] = (on * _silu(za_ref[:, hs])).astype(BF16)

    row8 = lax.broadcasted_iota(jnp.int32, (B_HEADS, LANES), 0)
    lane8 = _lane((B_HEADS, LANES))
    own_half = (lane8 >= B_HD) == (row8 >= B_GROUP)
    rcol = lax.broadcasted_iota(jnp.int32, (B_HEADS, 1), 0)
    sink = jnp.zeros((B_HEADS, 1), F32)
    for r in range(B_HEADS):
        sink = jnp.where(rcol == r, sink_ref[r], sink)
    qv = qb_ref[...]
    qv_r = jnp.concatenate([pltpu.roll(qv[:, g * LANES:(g + 1) * LANES], B_HD, axis=1)
                            for g in range(B_WIDTH // LANES)], axis=-1)
    kn_t = kn_ref[...].T
    vn_t = vn_ref[...].T
    newest = _lane((LANES, WINDOW)) == WINDOW - 1
    qzs, scs = [], []
    for bb in range(bt):
        qz = jnp.zeros((B_HEADS, LANES), F32)
        for r in range(B_HEADS):
            grp, half, kh = r // 2, r % 2, r // B_GROUP
            src = qv if half == kh else qv_r
            qz = jnp.where(row8 == r, src[bb:bb + 1, grp * LANES:(grp + 1) * LANES], qz)
        qzs.append(jnp.where(own_half, qz, 0.0))
    for bb in range(bt):
        scs.append(_dot(qzs[bb], ck_ref[bb]))
    ps, pnews, dens = [], [], []
    for bb in range(bt):
        sc_new = jnp.sum(qzs[bb] * kn_ref[bb:bb + 1, :], axis=-1, keepdims=True)
        m = jnp.maximum(jnp.maximum(jnp.max(scs[bb], axis=-1, keepdims=True), sc_new), sink)
        p = jnp.exp(scs[bb] - m)
        p_new = jnp.exp(sc_new - m)
        ps.append(p)
        pnews.append(p_new)
        dens.append(jnp.sum(p, axis=-1, keepdims=True) + p_new + jnp.exp(sink - m))
    pvs = [_dot_nt(ps[bb], cv_ref[bb]) for bb in range(bt)]
    for bb in range(bt):
        o = (pvs[bb] + pnews[bb] * vn_ref[bb:bb + 1, :]) / dens[bb]
        o = jnp.where(own_half, o, 0.0)
        ob_scr[bb * B_HEADS:(bb + 1) * B_HEADS, :] = o + pltpu.roll(o, B_HD, axis=1)
    for bb in range(bt):
        nck_ref[bb] = jnp.where(newest, kn_t[:, bb:bb + 1], pltpu.roll(ck_ref[bb], WINDOW - 1, axis=1))
        ncv_ref[bb] = jnp.where(newest, vn_t[:, bb:bb + 1], pltpu.roll(cv_ref[bb], WINDOW - 1, axis=1))
    low = _lane((bt, LANES)) < B_HD
    for grp in range(B_WIDTH // LANES):
        even = ob_scr[pl.ds(2 * grp, bt, stride=B_HEADS), :]
        odd = ob_scr[pl.ds(2 * grp + 1, bt, stride=B_HEADS), :]
        gs = slice(grp * LANES, (grp + 1) * LANES)
        ob_ref[:, gs] = (jnp.where(low, even, odd) * _silu(zb_ref[:, gs])).astype(BF16)


def _sstep(sinks, q, k, v, gb, za, na_row, state, qb, kn, vn, zb, ck, cv):
    n = q.shape[0]
    bt = STEP_BT
    row = lambda w: pl.BlockSpec((bt, w), lambda i: (i, 0))
    st_spec = pl.BlockSpec((bt, A_HEADS, A_DK, A_DV), lambda i: (i, 0, 0, 0))
    c_spec = pl.BlockSpec((bt, WINDOW, LANES), lambda i: (i, 0, 0))
    return pl.pallas_call(
        _sstep_kernel,
        grid=(n // bt,),
        in_specs=[pl.BlockSpec(memory_space=pltpu.SMEM),
                  row(A_WIDTH), row(A_WIDTH), row(A_WIDTH), row(LANES), row(A_WIDTH),
                  pl.BlockSpec((1, A_DV), lambda i: (0, 0)), st_spec,
                  row(B_WIDTH), row(LANES), row(LANES), row(B_WIDTH), c_spec, c_spec],
        out_specs=[row(A_WIDTH), row(B_WIDTH), st_spec, c_spec, c_spec],
        out_shape=[jax.ShapeDtypeStruct((n, A_WIDTH), BF16),
                   jax.ShapeDtypeStruct((n, B_WIDTH), BF16),
                   jax.ShapeDtypeStruct((n, A_HEADS, A_DK, A_DV), F32),
                   jax.ShapeDtypeStruct((n, WINDOW, LANES), F32),
                   jax.ShapeDtypeStruct((n, WINDOW, LANES), F32)],
        scratch_shapes=[pltpu.VMEM((bt, A_WIDTH), F32), pltpu.VMEM((bt * B_HEADS, LANES), F32)],
        compiler_params=pltpu.CompilerParams(dimension_semantics=("arbitrary",),
                                             vmem_limit_bytes=VMEM_LIMIT),
        name="sstep",
    )(sinks, q, k, v, gb, za, na_row, state, qb, kn, vn, zb, ck, cv)


def _rope_tables(pos):
    half = B_HD // 2
    inv = 1.0 / (ROPE_THETA ** (np.arange(half, dtype=np.float64) / half))
    ang = np.asarray(pos, np.float64)[:, None] * inv[None, :]
    cos, sin = np.cos(ang), np.sin(ang)
    reps = LANES // B_HD
    return (jnp.asarray(np.tile(np.concatenate([cos, cos], -1), (1, reps)), F32),
            jnp.asarray(np.tile(np.concatenate([-sin, sin], -1), (1, reps)), F32))


def _pad_row(vec, offset):
    return jnp.pad(vec.astype(F32).reshape(1, -1), ((0, 0), (offset, LANES - offset - vec.shape[0])))


def _layer(x_prompt, x_sample, state_conv, state_delta, cache_k, cache_v, c_prompt, c_sample,
           w_ada, b_ada, w_in, conv_w, a_log, dt_bias, norm_a, sinks, w_out, ln_g, ln_b):
    bsz, seq, _ = x_prompt.shape
    n_s = x_sample.shape[0]

    w_r = _wprep(jnp.swapaxes(w_in, 0, 1))
    w_o = w_out.astype(BF16)
    alog_row = _pad_row(a_log, A_HEADS)
    dt_row = _pad_row(dt_bias, A_HEADS)
    na_row = norm_a.reshape(1, A_DV)
    g_row = ln_g.reshape(1, D_MODEL)
    b_row = ln_b.reshape(1, D_MODEL)

    assert n_s % 8 == 0 and bsz <= 8
    mod = _ada(c_sample, c_prompt, w_ada, b_ada.reshape(1, 3 * D_MODEL))

    cos_p, sin_p = _rope_tables(np.arange(seq))
    (qkv, za, gb, qb, kvl, zb, conv_p, kb_last, vb_last) = _proj(
        x_prompt, mod, n_s, w_r, conv_w, alog_row, dt_row, cos_p, sin_p)
    oa, delta_p = _delta(qkv, gb, za, na_row)
    y_p = _swa_out(sinks, qb, kvl, zb, oa, x_prompt, mod, n_s, w_o, g_row, b_row)
    swa_k_p = kb_last.reshape(bsz, WINDOW, B_KV_HEADS, B_HD)
    swa_v_p = vb_last.reshape(bsz, WINDOW, B_KV_HEADS, B_HD)

    cos_s, sin_s = _rope_tables(np.array([PAST_LEN]))
    xs = x_sample.reshape(n_s, D_MODEL)
    cst = jnp.transpose(state_conv, (1, 0, 2))
    sq, sk, sv, sza, sgb, sqb, skn, svn, szb, ncs = _sproj(xs, mod, w_r, conv_w, cst, alog_row, dt_row,
                                                           cos_s, sin_s)
    soa, sob, delta_s, nck, ncv = _sstep(sinks, sq, sk, sv, sgb, sza, na_row, state_delta,
                                         sqb, skn, svn, szb,
                                         jnp.swapaxes(cache_k.reshape(n_s, WINDOW, LANES), 1, 2),
                                         jnp.swapaxes(cache_v.reshape(n_s, WINDOW, LANES), 1, 2))
    y_s = _out(soa, sob, xs, mod, w_o, g_row, b_row)
    conv_s = jnp.transpose(ncs, (1, 0, 2))
    unpack = lambda c: jnp.swapaxes(c, 1, 2).reshape(n_s, WINDOW, B_KV_HEADS, B_HD)
    return (y_p, y_s.reshape(n_s, 1, D_MODEL), conv_p, delta_p, swa_k_p, swa_v_p,
            conv_s, delta_s, unpack(nck), unpack(ncv))


def kernel(x_prompt, x_sample, state_conv, state_delta, cache_swa_k, cache_swa_v, c_prompt, c_sample,
           w_ada, b_ada, w_in, conv_w, a_log, dt_bias, norm_a, sinks, w_out, ln_g, ln_b):
    assert w_ada.shape[0] == DEPTH == 1
    outs = _layer(x_prompt, x_sample, state_conv[0], state_delta[0], cache_swa_k[0], cache_swa_v[0],
                  c_prompt, c_sample, w_ada[0], b_ada[0], w_in[0], conv_w[0], a_log[0], dt_bias[0],
                  norm_a[0], sinks[0], w_out[0], ln_g[0], ln_b[0])
    y_p, y_s = outs[0], outs[1]
    return (y_p, y_s) + tuple(o[None] for o in outs[2:])
```

```python
import jax
import jax.numpy as jnp
import numpy as np
from jax import lax
from jax.experimental import pallas as pl
from jax.experimental.pallas import tpu as pltpu

F32 = jnp.float32
BF16 = jnp.bfloat16

D_MODEL = 1024
DEPTH = 1
PAST_LEN = 8192
A_HEADS = 4
A_DK = 128
A_DV = 128
A_WIDTH = A_HEADS * A_DV
A_QKV = 3 * A_WIDTH
CONV_W = 4
CHUNK = 64
B_HEADS = 8
B_KV_HEADS = 2
B_HD = 64
B_GROUP = B_HEADS // B_KV_HEADS
B_WIDTH = B_HEADS * B_HD
B_KV_WIDTH = B_KV_HEADS * B_HD
WINDOW = 128
ROPE_THETA = 10000.0
MIX_WIDTH = A_WIDTH + B_WIDTH
DEEPNORM_ALPHA = (2 * DEPTH) ** 0.25
LOG2E = 1.4426950408889634
LN_EPS = 1e-5
RMS_EPS = 1e-6
L2_EPS = 1e-6

OFF_A_Z = A_QKV
OFF_A_BETA = OFF_A_Z + A_WIDTH
OFF_A_DECAY = OFF_A_BETA + A_HEADS
OFF_B_Q = OFF_A_DECAY + A_HEADS
OFF_B_K = OFF_B_Q + B_WIDTH
OFF_B_V = OFF_B_K + B_KV_WIDTH
OFF_B_Z = OFF_B_V + B_KV_WIDTH
PROJ_COLS = OFF_B_Z + B_WIDTH

LANES = 128
C_QKV = 0
C_ZA = C_QKV + A_QKV
C_QB = C_ZA + A_WIDTH
C_KB = C_QB + B_WIDTH
C_VB = C_KB + B_KV_WIDTH
C_ZB = C_VB + B_KV_WIDTH
C_BD = C_ZB + B_WIDTH
WPREP_TN = 256
W_COLS = C_BD + WPREP_TN

VMEM_LIMIT = 56 * 1024 * 1024

ADA_TN = 1536
PROJ_TM = 512
PROJ_CW = 256
PROJ_PART_ROWS = (128, 256, 128)
DELTA_CT = 256
DELTA_WAVE = 2
SWA_TQ = 512
SWA_WAVE = 2
STEP_BT = 16


def _dot(a, b):
    return jnp.dot(a, b, preferred_element_type=F32)


def _dot_nt(a, b):
    return lax.dot_general(a, b, (((1,), (1,)), ((), ())), preferred_element_type=F32)


def _silu(x):
    return x * jax.nn.sigmoid(x)


def _softplus(x):
    return jnp.maximum(x, 0.0) + jnp.log1p(jnp.exp(-jnp.abs(x)))


def _lane(shape):
    return lax.broadcasted_iota(jnp.int32, shape, len(shape) - 1)


def _l2norm_heads(y, scale):
    outs = []
    for h in range(y.shape[1] // A_DK):
        xh = y[:, h * A_DK:(h + 1) * A_DK]
        ss = jnp.sum(xh * xh, axis=-1, keepdims=True)
        xn = xh * lax.rsqrt(ss + L2_EPS)
        outs.append(xn * scale if scale != 1.0 else xn)
    return jnp.concatenate(outs, axis=-1)


def _rotary_group(xg, cos, sin_signed):
    lane = _lane(xg.shape)
    swapped = jnp.where((lane % B_HD) < (B_HD // 2),
                        pltpu.roll(xg, LANES - B_HD // 2, axis=1),
                        pltpu.roll(xg, B_HD // 2, axis=1))
    return xg * cos + swapped * sin_signed


def _kv_layouts(kb, vb):
    low = _lane(kb.shape) < B_HD
    kbr = pltpu.roll(kb, B_HD, axis=1)
    vbr = pltpu.roll(vb, B_HD, axis=1)
    return kb, kbr, jnp.where(low, vb, vbr), jnp.where(low, vbr, vb)


def _gate_lanes(bd, alog_row, dt_row):
    lane = _lane(bd.shape)
    g = -jnp.exp(alog_row) * _softplus(bd + dt_row)
    return jnp.where(lane < A_HEADS, jax.nn.sigmoid(bd), g)


def _layer_norm(r, g, b):
    mu = jnp.mean(r, axis=-1, keepdims=True)
    d = r - mu
    var = jnp.mean(d * d, axis=-1, keepdims=True)
    return d * lax.rsqrt(var + LN_EPS) * g + b


def _wprep_kernel(wa_ref, wb_ref, o_ref):
    tn = wa_ref.shape[0]
    o_ref[:, 0:tn] = wa_ref[...].T.astype(BF16)
    xb = wb_ref[...]
    tail = pl.program_id(0) == pl.num_programs(0) - 1
    row = lax.broadcasted_iota(jnp.int32, xb.shape, 0)
    xb = jnp.where(jnp.logical_and(tail, row >= 2 * A_HEADS), 0.0, xb)
    o_ref[:, tn:2 * tn] = xb.T.astype(BF16)


def _wprep(w_t):
    tn = WPREP_TN
    n_a, n_b = OFF_A_BETA // tn, (PROJ_COLS - OFF_B_Q) // tn
    assert n_a * tn == OFF_A_BETA and n_b * tn == PROJ_COLS - OFF_B_Q and OFF_A_BETA + tn <= PROJ_COLS

    assert (n_a + n_b + 1) % 2 == 0

    def src_row(j):
        return jnp.where(j < n_a, j * tn, jnp.where(j < n_a + n_b, OFF_B_Q + (j - n_a) * tn, OFF_A_BETA))

    src = lambda k: pl.BlockSpec((pl.Element(tn), pl.Element(D_MODEL)),
                                 lambda j: (pl.multiple_of(src_row(2 * j + k), 8), 0))
    return pl.pallas_call(
        _wprep_kernel,
        grid=((n_a + n_b + 1) // 2,),
        in_specs=[src(0), src(1)],
        out_specs=pl.BlockSpec((D_MODEL, 2 * tn), lambda j: (0, j)),
        out_shape=jax.ShapeDtypeStruct((D_MODEL, W_COLS), BF16),
        compiler_params=pltpu.CompilerParams(dimension_semantics=("arbitrary",),
                                             vmem_limit_bytes=VMEM_LIMIT),
        name="wprep",
    )(w_t, w_t)


def _ada_kernel(cs_ref, cp_ref, w_ref, b_ref, o_ref):
    n_s, n_p = cs_ref.shape[0], cp_ref.shape[0]
    w = w_ref[...].astype(BF16)
    o_ref[0:n_s, :] = _dot(cs_ref[...].astype(BF16), w) + b_ref[...]
    o_ref[n_s:n_s + 8, :] = jnp.zeros((8, o_ref.shape[1]), F32)
    o_ref[n_s:n_s + n_p, :] = _dot(cp_ref[...].astype(BF16), w) + b_ref[...]


def _ada(c_sample, c_prompt, w_ada, b_ada):
    n_s, n_p = c_sample.shape[0], c_prompt.shape[0]
    rows = n_s + 8
    tn = ADA_TN
    return pl.pallas_call(
        _ada_kernel,
        grid=(3 * D_MODEL // tn,),
        in_specs=[pl.BlockSpec((n_s, D_MODEL), lambda j: (0, 0)),
                  pl.BlockSpec((n_p, D_MODEL), lambda j: (0, 0)),
                  pl.BlockSpec((D_MODEL, tn), lambda j: (0, j)),
                  pl.BlockSpec((1, tn), lambda j: (0, j))],
        out_specs=pl.BlockSpec((rows, tn), lambda j: (0, j)),
        out_shape=jax.ShapeDtypeStruct((rows, 3 * D_MODEL), F32),
        compiler_params=pltpu.CompilerParams(dimension_semantics=("arbitrary",),
                                             vmem_limit_bytes=VMEM_LIMIT),
        name="ada",
    )(c_sample, c_prompt, w_ada, b_ada)


def _proj_kernel(x_ref, mod_ref, w_ref, cw_ref, alog_ref, dt_ref, cos_ref, sin_ref,
                 qkv_ref, za_ref, gb_ref, qb_ref, kvl_ref, zb_ref, cst_ref, kbl_ref, vbl_ref, ubuf):
    tm = x_ref.shape[1]
    t = pl.program_id(1)

    @pl.when(t == 0)
    def _():
        ubuf[...] = jnp.zeros(ubuf.shape, F32)

    brow = pl.ds(pl.program_id(0), 1)
    shift = mod_ref[brow, 0:D_MODEL]
    scale = mod_ref[brow, D_MODEL:2 * D_MODEL]

    assert sum(PROJ_PART_ROWS) == tm and PROJ_PART_ROWS[-1] >= WINDOW
    cw = PROJ_CW
    pieces = [slice(c0, c0 + cw) for c0 in range(0, A_QKV, cw)]

    def matmuls(r0, rp):
        h = (x_ref[0, r0:r0 + rp, :] * (1.0 + scale) + shift).astype(BF16)
        return ([_dot(h, w_ref[:, cs]) for cs in pieces],
                _dot(h, w_ref[:, C_ZA:C_ZA + A_WIDTH]), _dot(h, w_ref[:, C_BD:C_BD + LANES]),
                _dot(h, w_ref[:, C_QB:C_QB + B_WIDTH]), _dot(h, w_ref[:, C_KB:C_KB + 2 * LANES]),
                _dot(h, w_ref[:, C_ZB:C_ZB + B_WIDTH]))

    def epilogue(r0, rp, results):
        rs = slice(r0, r0 + rp)
        us, za, ubd, uq, ukv, zb = results
        sub = lax.broadcasted_iota(jnp.int32, (rp // 8, 8, cw), 1)

        def conv_epilogue(cs, u):
            gi = cs.start // A_WIDTH
            groups = jnp.concatenate([ubuf[:, cs], u], axis=0).reshape(rp // 8 + 1, 8, cw)
            acc = None
            for j in range(CONV_W - 1, 0, -1):
                rot = pltpu.roll(groups, j, axis=1)
                term = (jnp.where(sub < j, rot[:-1], rot[1:]).reshape(rp, cw)
                        * cw_ref[CONV_W - 1 - j:CONV_W - j, cs])
                acc = term if acc is None else acc + term
            y = _silu(acc + u * cw_ref[CONV_W - 1:CONV_W, cs])
            if gi == 0:
                y = _l2norm_heads(y, A_DK ** -0.5)
            elif gi == 1:
                y = _l2norm_heads(y, 1.0)
            qkv_ref[0, rs, cs] = y
            ubuf[:, cs] = u[rp - 8:rp]
            if r0 + rp == tm:
                cst_ref[0, :, cs] = u[rp - (CONV_W - 1):rp]

        for cs, u in zip(pieces, us):
            conv_epilogue(cs, u)
        za_ref[0, rs, :] = za
        zb_ref[0, rs, :] = zb
        gb_ref[0, rs, :] = _gate_lanes(ubd, alog_ref[...], dt_ref[...])
        cos = cos_ref[rs, :]
        sin = sin_ref[rs, :]
        for g in range(B_WIDTH // LANES):
            qb_ref[0, rs, g * LANES:(g + 1) * LANES] = (
                _rotary_group(uq[:, g * LANES:(g + 1) * LANES], cos, sin) * (B_HD ** -0.5 * LOG2E)).astype(BF16)
        kb = _rotary_group(ukv[:, 0:LANES], cos, sin)
        vb = ukv[:, LANES:2 * LANES]
        for j, val in enumerate(_kv_layouts(kb, vb)):
            kvl_ref[0, rs, j * LANES:(j + 1) * LANES] = val.astype(BF16)
        return kb, vb

    parts, r0 = [], 0
    for rp in PROJ_PART_ROWS:
        parts.append((r0, rp))
        r0 += rp
    pending = matmuls(*parts[0])
    for p, part in enumerate(parts):
        results = pending
        if p + 1 < len(parts):
            pending = matmuls(*parts[p + 1])
        kb, vb = epilogue(*part, results)

    @pl.when(t == pl.num_programs(1) - 1)
    def _():
        kbl_ref[0] = kb[kb.shape[0] - WINDOW:, :]
        vbl_ref[0] = vb[vb.shape[0] - WINDOW:, :]


def _proj(x, mod, mod_row0, w_r, conv_w, alog_row, dt_row, cos_t, sin_t):
    bsz, t, _ = x.shape
    tm = PROJ_TM
    row = lambda w: pl.BlockSpec((1, tm, w), lambda b, i: (b, i, 0))
    const2 = lambda s: pl.BlockSpec(s, lambda b, i: (0, 0))
    per_b = lambda r, w: pl.BlockSpec((1, r, w), lambda b, i: (b, 0, 0))
    wide = lambda w, dt=F32: jax.ShapeDtypeStruct((bsz, t, w), dt)
    return pl.pallas_call(
        _proj_kernel,
        grid=(bsz, t // tm),
        in_specs=[row(D_MODEL),
                  pl.BlockSpec((8, 3 * D_MODEL), lambda b, i: (mod_row0 // 8, 0)),
                  const2((D_MODEL, W_COLS)),
                  const2((CONV_W, A_QKV)),
                  const2((1, LANES)), const2((1, LANES)),
                  pl.BlockSpec((tm, LANES), lambda b, i: (i, 0)),
                  pl.BlockSpec((tm, LANES), lambda b, i: (i, 0))],
        out_specs=[row(A_QKV), row(A_WIDTH), row(LANES), row(B_WIDTH), row(4 * LANES), row(B_WIDTH),
                   per_b(CONV_W - 1, A_QKV), per_b(WINDOW, LANES), per_b(WINDOW, LANES)],
        out_shape=[wide(A_QKV), wide(A_WIDTH), wide(LANES), wide(B_WIDTH, BF16), wide(4 * LANES, BF16),
                   wide(B_WIDTH),
                   jax.ShapeDtypeStruct((bsz, CONV_W - 1, A_QKV), F32),
                   jax.ShapeDtypeStruct((bsz, WINDOW, LANES), F32),
                   jax.ShapeDtypeStruct((bsz, WINDOW, LANES), F32)],
        scratch_shapes=[pltpu.VMEM((8, A_QKV), F32)],
        compiler_params=pltpu.CompilerParams(dimension_semantics=("arbitrary", "arbitrary"),
                                             vmem_limit_bytes=VMEM_LIMIT),
        name="proj",
    )(x, mod, w_r, conv_w, alog_row, dt_row, cos_t, sin_t)


def _delta_kernel(q_ref, k_ref, v_ref, gb_ref, za_ref, na_ref, oa_ref, st_ref,
                  s_scr, wq_s, ut_s, akd_s, gl_s):
    bsz, ct = q_ref.shape[0], q_ref.shape[1]
    nch = ct // CHUNK
    t = pl.program_id(0)
    wslot = t % 2
    rslot = 1 - wslot

    @pl.when(t == 0)
    def _():
        s_scr[...] = jnp.zeros(s_scr.shape, F32)
        wq_s[...] = jnp.zeros(wq_s.shape, BF16)
        ut_s[...] = jnp.zeros(ut_s.shape, F32)
        akd_s[...] = jnp.zeros(akd_s.shape, BF16)
        gl_s[...] = jnp.zeros(gl_s.shape, F32)

    units = [(b, c, h) for b in range(bsz) for c in range(nch) for h in range(A_HEADS)]
    uid = {u_: i for i, u_ in enumerate(units)}
    rows = lambda c: slice(c * CHUNK, (c + 1) * CHUNK)
    lanes = lambda h: slice(h * A_DK, (h + 1) * A_DK)
    na = na_ref[...]

    s_cur = {(b, h): s_scr[b * A_HEADS + h] for b in range(bsz) for h in range(A_HEADS)}
    ws, uu = {}, {}

    def rec_ws(c):
        for b in range(bsz):
            for h in range(A_HEADS):
                i = uid[b, c, h]
                ws[b, h] = _dot(wq_s[rslot, i], s_cur[b, h].astype(BF16))
                uu[b, h] = (ut_s[rslot, i] - ws[b, h][:CHUNK]).astype(BF16)

    def rec_ou(c):
        zpad = jnp.zeros((CHUNK, A_DV), BF16)
        for b in range(bsz):
            u_bd = jnp.concatenate(
                [jnp.concatenate([uu[b, h] if hh == h else zpad for hh in range(A_HEADS)], axis=-1)
                 for h in range(A_HEADS)], axis=0)
            ou = _dot(akd_s[rslot, b * nch + c], u_bd)
            for h in range(A_HEADS):
                o = ws[b, h][CHUNK:] + ou[:CHUNK, lanes(h)]
                s_cur[b, h] = gl_s[rslot, uid[b, c, h]] * s_cur[b, h] + ou[CHUNK:, lanes(h)]
                on = o * lax.rsqrt(jnp.mean(o * o, axis=-1, keepdims=True) + RMS_EPS) * na
                oa_ref[b, rows(c), lanes(h)] = (on * _silu(za_ref[b, rows(c), lanes(h)])).astype(BF16)

    rec_stages = []
    for c in range(nch):
        rec_stages += [lambda c=c: rec_ws(c), lambda c=c: rec_ou(c)]

    def run_rec(n_left_after):
        while rec_stages and len(rec_stages) > n_left_after:
            rec_stages.pop(0)()

    pk = A_HEADS * CHUNK
    low = _lane((CHUNK, LANES)) < CHUNK
    low_row = _lane((1, LANES)) < CHUNK
    ti_p = lax.broadcasted_iota(jnp.int32, (CHUNK, pk), 0)
    ii_p = _lane((CHUNK, pk)) % CHUNK
    zero64 = jnp.zeros((CHUNK, LANES), BF16)

    def pack(parts):
        return jnp.concatenate([jnp.where(low, parts[0], parts[1]), jnp.where(low, parts[2], parts[3])], axis=-1)

    def block_diag(x16):
        blocks = []
        for h in range(A_HEADS):
            pair, first = h // 2, h % 2 == 0
            piece = jnp.where(low if first else jnp.logical_not(low), x16[:, pair * LANES:(pair + 1) * LANES], zero64)
            blocks.append(jnp.concatenate([piece, zero64] if pair == 0 else [zero64, piece], axis=-1))
        return jnp.concatenate(blocks, axis=0)

    zrhs = jnp.zeros((CHUNK, 2 * A_DK), BF16)
    n_rec = len(rec_stages)
    n_slots = 8 * (bsz // DELTA_WAVE)
    done = [0]

    def stage_done():
        done[0] += 1
        run_rec(n_rec - 1 - (done[0] * n_rec) // n_slots)

    def decay_terms(b, beta, g_col, g_last, eg, dec_p, beta_p):
        gbv = gb_ref[b]
        rin = lax.broadcasted_iota(jnp.int32, gbv.shape, 0) % CHUNK
        gcs = gbv
        s = 1
        while s < CHUNK:
            gcs = gcs + jnp.where(rin >= s, pltpu.roll(gcs, s, axis=0), 0.0)
            s *= 2
        gcs_t = gcs.T
        for c in range(nch):
            r0 = c * CHUNK
            pair_lanes = slice((c // 2) * LANES, (c // 2 + 1) * LANES)
            g_rows = []
            for h in range(A_HEADS):
                u_ = (b, c, h)
                beta[u_] = jnp.broadcast_to(gbv[rows(c), h:h + 1], (CHUNK, A_DK))
                g_col[u_] = jnp.broadcast_to(gcs[rows(c), A_HEADS + h:A_HEADS + h + 1], (CHUNK, A_DK))
                g_last[u_] = gcs[r0 + CHUNK - 1:r0 + CHUNK, A_HEADS + h:A_HEADS + h + 1]
                eg[u_] = jnp.exp(g_col[u_])
                g_row = gcs_t[A_HEADS + h:A_HEADS + h + 1, pair_lanes]
                g_rows.append(g_row if c % 2 == h % 2 else pltpu.roll(g_row, CHUNK, axis=1))
            g_row_p = jnp.concatenate([jnp.where(low_row, g_rows[0], g_rows[1]),
                                       jnp.where(low_row, g_rows[2], g_rows[3])], axis=-1)
            g_col_p = pack([g_col[b, c, h] for h in range(A_HEADS)])
            dec_p[b, c] = jnp.exp(jnp.where(ti_p >= ii_p, g_col_p - g_row_p, -jnp.inf))
            beta_p[b, c] = pack([beta[b, c, h] for h in range(A_HEADS)])

    def prepare(bs):
        groups_b = [(b, c) for b in bs for c in range(nch)]
        beta, g_col, g_last, eg, dec_p, beta_p = {}, {}, {}, {}, {}, {}
        for b in bs:
            decay_terms(b, beta, g_col, g_last, eg, dec_p, beta_p)

        nmat = {}
        for (b, c) in groups_b:
            k16 = k_ref[b, rows(c), :].astype(BF16)
            q16 = q_ref[b, rows(c), :].astype(BF16)
            k_heads = jnp.concatenate(
                [jnp.concatenate([k16[:, lanes(h)] if hh == h else zero64 for hh in range(A_HEADS)], axis=-1)
                 for h in range(A_HEADS)], axis=0)
            kq = _dot_nt(jnp.concatenate([k16, q16], axis=0), k_heads)
            nmat[b, c] = -(beta_p[b, c] * kq[:CHUNK] * jnp.where(ti_p > ii_p, dec_p[b, c], 0.0))
            akd_s[wslot, b * nch + c, 0:CHUNK, :] = (kq[CHUNK:] * dec_p[b, c]).astype(BF16)
        stage_done()

        rsum = dict(nmat)
        pw16 = {g_: nmat[g_].astype(BF16) for g_ in groups_b}
        pw = {g_: _dot(pw16[g_], block_diag(pw16[g_])) for g_ in groups_b}
        stage_done()
        for step in range(1, 6):
            last = step == 5
            pw16 = {g_: pw[g_].astype(BF16) for g_ in groups_b}
            rp = {}
            for g_ in groups_b:
                r16 = rsum[g_].astype(BF16)
                rp[g_] = _dot(r16 if last else jnp.concatenate([r16, pw16[g_]], axis=0), block_diag(pw16[g_]))
            for g_ in groups_b:
                rsum[g_] = rsum[g_] + pw[g_] + rp[g_][:CHUNK]
                if not last:
                    pw[g_] = rp[g_][CHUNK:]
            stage_done()

        for (b, c) in groups_b:
            for h in range(A_HEADS):
                u_ = (b, c, h)
                i = uid[u_]
                kc = k_ref[b, rows(c), lanes(h)]
                rhs = jnp.concatenate([(beta[u_] * eg[u_]) * kc, beta[u_] * v_ref[b, rows(c), lanes(h)]],
                                      axis=-1)
                rhs16 = rhs.astype(BF16)
                rhs_rows = jnp.concatenate([rhs16 if hh == h else zrhs for hh in range(A_HEADS)], axis=0)
                sol = rhs + _dot(rsum[b, c].astype(BF16), rhs_rows)
                wq_s[wslot, i] = jnp.concatenate([sol[:, :A_DK], eg[u_] * q_ref[b, rows(c), lanes(h)]],
                                                 axis=0).astype(BF16)
                ut_s[wslot, i] = sol[:, A_DK:]
                gl_s[wslot, i] = jnp.broadcast_to(jnp.exp(g_last[u_]), (1, A_DV))
        for (b, c) in groups_b:
            kd = [jnp.exp(g_last[b, c, h] - g_col[b, c, h]) * k_ref[b, rows(c), lanes(h)]
                  for h in range(A_HEADS)]
            for p in range(A_HEADS // 2):
                akd_s[wslot, b * nch + c, CHUNK:, p * LANES:(p + 1) * LANES] = (
                    jnp.concatenate([kd[2 * p], kd[2 * p + 1]], axis=0).T.astype(BF16))
        stage_done()

    run_rec(n_rec - 1)
    for b0 in range(0, bsz, DELTA_WAVE):
        prepare(range(b0, b0 + DELTA_WAVE))
    run_rec(0)

    for b in range(bsz):
        for h in range(A_HEADS):
            s_scr[b * A_HEADS + h] = s_cur[b, h]

    @pl.when(t == pl.num_programs(0) - 1)
    def _():
        for b in range(bsz):
            for h in range(A_HEADS):
                st_ref[b, h] = s_cur[b, h]


def _delta(qkv, gb, za, na_row):
    bsz, t, _ = qkv.shape
    ct = DELTA_CT
    nt = t // ct
    n_units = bsz * (ct // CHUNK) * A_HEADS
    prep = lambda w, j=0: pl.BlockSpec((bsz, ct, w), lambda i: (0, jnp.minimum(i, nt - 1), j))
    rec = lambda w: pl.BlockSpec((bsz, ct, w), lambda i: (0, jnp.maximum(i - 1, 0), 0))
    return pl.pallas_call(
        _delta_kernel,
        grid=(nt + 1,),
        in_specs=[prep(A_WIDTH, 0), prep(A_WIDTH, 1), prep(A_WIDTH, 2), prep(LANES), rec(A_WIDTH),
                  pl.BlockSpec((1, A_DV), lambda i: (0, 0))],
        out_specs=[rec(A_WIDTH),
                   pl.BlockSpec((bsz, A_HEADS, A_DK, A_DV), lambda i: (0, 0, 0, 0))],
        out_shape=[jax.ShapeDtypeStruct((bsz, t, A_WIDTH), BF16),
                   jax.ShapeDtypeStruct((bsz, A_HEADS, A_DK, A_DV), F32)],
        scratch_shapes=[pltpu.VMEM((bsz * A_HEADS, A_DK, A_DV), F32),
                        pltpu.VMEM((2, n_units, 2 * CHUNK, A_DK), BF16),
                        pltpu.VMEM((2, n_units, CHUNK, A_DV), F32),
                        pltpu.VMEM((2, n_units // A_HEADS, CHUNK + A_DK, A_HEADS * CHUNK), BF16),
                        pltpu.VMEM((2, n_units, 1, A_DV), F32)],
        compiler_params=pltpu.CompilerParams(dimension_semantics=("arbitrary",),
                                             vmem_limit_bytes=VMEM_LIMIT),
        name="delta",
    )(qkv, qkv, qkv, gb, za, na_row)


def _swa_out_kernel(sink_ref, qb_ref, kc_ref, kp_ref, krc_ref, krp_ref, v0c_ref, v0p_ref, v1c_ref, v1p_ref,
                    zb_ref, oa_ref, x_ref, gate_ref, w_ref, g_ref, b_ref, y_ref):
    n = pl.program_id(1)
    gate = gate_ref[pl.ds(pl.program_id(0), 1), :]
    tq = qb_ref.shape[1]
    blk = WINDOW
    kx = (jnp.concatenate([kp_ref[0], kc_ref[0]], axis=0), jnp.concatenate([krp_ref[0], krc_ref[0]], axis=0))
    vd = (jnp.concatenate([v0p_ref[0], v0c_ref[0]], axis=0), jnp.concatenate([v1p_ref[0], v1c_ref[0]], axis=0))
    low_kv = _lane(vd[0].shape) < B_HD
    one_kv = jnp.ones(vd[0].shape, BF16)
    vsum = tuple((jnp.where(low_kv, v, one_kv), jnp.where(low_kv, one_kv, v)) for v in vd)

    a = lax.broadcasted_iota(jnp.int32, (2 * blk, 2 * blk), 0) % blk
    j = lax.broadcasted_iota(jnp.int32, (2 * blk, 2 * blk), 1)
    rel = a + blk - j
    band = (rel >= 0) & (rel <= WINDOW)
    band_first = band & ((n > 0) | (j >= blk))
    top = lax.broadcasted_iota(jnp.int32, (2 * blk, 1), 0) < blk
    low = _lane((blk, LANES)) < B_HD
    zero = jnp.zeros((blk, LANES), BF16)

    qrows = lambda i: slice(i * blk, (i + 1) * blk)
    krows = lambda i: slice(i * blk, (i + 2) * blk)
    sink = {(kh, half): jnp.where(top, sink_ref[kh * B_GROUP + half] * LOG2E,
                                  sink_ref[kh * B_GROUP + half + 2] * LOG2E)
            for kh in range(B_KV_HEADS) for half in range(2)}
    def wave_units(i0):
        blocks = range(i0, i0 + SWA_WAVE)
        return blocks, [(i, kh, half) for i in blocks for kh in range(B_KV_HEADS) for half in range(2)]

    def score_matmuls(i0):
        blocks, units = wave_units(i0)
        mix_a = {i: _dot(oa_ref[0, qrows(i), :], w_ref[0:A_WIDTH, :]) for i in blocks}
        sc = {}
        for (i, kh, half) in units:
            qs = []
            for g in range(2):
                grp = kh * 2 + g
                xg = qb_ref[0, qrows(i), grp * LANES:(grp + 1) * LANES]
                qs.append(jnp.where(low if half == 0 else jnp.logical_not(low), xg, zero))
            qz = jnp.concatenate(qs, axis=0)
            sc[i, kh, half] = _dot_nt(qz, kx[0 if kh == half else 1][krows(i)])
        return mix_a, sc

    def finish(i0, mix_a, sc):
        blocks, units = wave_units(i0)
        p, esink = {}, {}
        for u_ in units:
            i, kh, half = u_
            s_m = jnp.where(band_first if i == 0 else band, sc[u_], -jnp.inf)
            m = jnp.maximum(jnp.max(s_m, axis=-1, keepdims=True), sink[kh, half])
            p[u_] = jnp.exp2(s_m - m).astype(BF16)
            esink[u_] = jnp.exp2(sink[kh, half] - m)
        pv = {u_: _dot(p[u_], vsum[u_[1]][u_[2]][krows(u_[0])]) for u_ in units}
        outs = {u_: pv[u_] / (pltpu.roll(pv[u_], B_HD, axis=1) + esink[u_]) for u_ in units}
        for i in blocks:
            ob = []
            for grp in range(B_WIDTH // LANES):
                kh, g = grp // 2, grp % 2
                og = jnp.where(low, outs[i, kh, 0][g * blk:(g + 1) * blk], outs[i, kh, 1][g * blk:(g + 1) * blk])
                ob.append((og * _silu(zb_ref[0, qrows(i), grp * LANES:(grp + 1) * LANES])).astype(BF16))
            mix = mix_a[i] + _dot(jnp.concatenate(ob, axis=-1), w_ref[A_WIDTH:MIX_WIDTH, :])
            r = DEEPNORM_ALPHA * x_ref[0, qrows(i), :] + (1.0 + gate) * mix
            y_ref[0, qrows(i), :] = _layer_norm(r, g_ref[...], b_ref[...])

    starts = list(range(0, tq // blk, SWA_WAVE))
    pending = score_matmuls(starts[0])
    for w, i0 in enumerate(starts):
        ready = pending
        if w + 1 < len(starts):
            pending = score_matmuls(starts[w + 1])
        finish(i0, *ready)


def _swa_out(sinks, qb, kvl, zb, oa, x, mod, mod_row0, w_out, ln_g, ln_b):
    bsz, t, _ = qb.shape
    tq = SWA_TQ
    per = tq // WINDOW
    cur = lambda w: pl.BlockSpec((1, tq, w), lambda b, i: (b, i, 0))
    kv_cur = lambda j: pl.BlockSpec((1, tq, LANES), lambda b, i: (b, i, j))
    kv_prev = lambda j: pl.BlockSpec((1, WINDOW, LANES), lambda b, i: (b, jnp.maximum(i * per - 1, 0), j))
    const2 = lambda s: pl.BlockSpec(s, lambda b, i: (0, 0))
    return pl.pallas_call(
        _swa_out_kernel,
        grid=(bsz, t // tq),
        in_specs=[pl.BlockSpec(memory_space=pltpu.SMEM), cur(B_WIDTH),
                  kv_cur(0), kv_prev(0), kv_cur(1), kv_prev(1), kv_cur(2), kv_prev(2), kv_cur(3), kv_prev(3),
                  cur(B_WIDTH), cur(A_WIDTH), cur(D_MODEL),
                  pl.BlockSpec((8, D_MODEL), lambda b, i: (mod_row0 // 8, 2)),
                  const2((MIX_WIDTH, D_MODEL)), const2((1, D_MODEL)), const2((1, D_MODEL))],
        out_specs=cur(D_MODEL),
        out_shape=jax.ShapeDtypeStruct((bsz, t, D_MODEL), F32),
        compiler_params=pltpu.CompilerParams(dimension_semantics=("arbitrary", "arbitrary"),
                                             vmem_limit_bytes=VMEM_LIMIT),
        name="swa_out",
    )(sinks, qb, kvl, kvl, kvl, kvl, kvl, kvl, kvl, kvl, zb, oa, x, mod, w_out, ln_g, ln_b)


def _out_kernel(oa_ref, ob_ref, x_ref, gate_ref, w_ref, g_ref, b_ref, y_ref):
    mix = _dot(oa_ref[...], w_ref[0:A_WIDTH, :]) + _dot(ob_ref[...], w_ref[A_WIDTH:MIX_WIDTH, :])
    r = DEEPNORM_ALPHA * x_ref[...] + (1.0 + gate_ref[...]) * mix
    y_ref[...] = _layer_norm(r, g_ref[...], b_ref[...])


def _out(oa, ob, x, mod, w_out, ln_g, ln_b):
    n = x.shape[0]
    full = lambda s: pl.BlockSpec(s, lambda i: (0, 0))
    return pl.pallas_call(
        _out_kernel,
        grid=(1,),
        in_specs=[full((n, A_WIDTH)), full((n, B_WIDTH)), full((n, D_MODEL)),
                  pl.BlockSpec((n, D_MODEL), lambda i: (0, 2)),
                  full((MIX_WIDTH, D_MODEL)), full((1, D_MODEL)), full((1, D_MODEL))],
        out_specs=full((n, D_MODEL)),
        out_shape=jax.ShapeDtypeStruct((n, D_MODEL), F32),
        compiler_params=pltpu.CompilerParams(dimension_semantics=("arbitrary",),
                                             vmem_limit_bytes=VMEM_LIMIT),
        name="out",
    )(oa, ob, x, mod, w_out, ln_g, ln_b)


def _sproj_kernel(x_ref, mod_ref, w_ref, cw_ref, cst_ref, alog_ref, dt_ref, cos_ref, sin_ref,
                  q_ref, k_ref, v_ref, za_ref, gb_ref, qb_ref, kb_ref, vb_ref, zb_ref, ncs_ref):
    shift = mod_ref[:, 0:D_MODEL]
    scale = mod_ref[:, D_MODEL:2 * D_MODEL]
    h = (x_ref[...] * (1.0 + scale) + shift).astype(BF16)

    for gi, o_ref in enumerate((q_ref, k_ref, v_ref)):
        c0 = gi * A_WIDTH
        cs = slice(c0, c0 + A_WIDTH)
        u = _dot(h, w_ref[:, cs])
        acc = cst_ref[0, :, cs] * cw_ref[0:1, cs]
        acc = acc + cst_ref[1, :, cs] * cw_ref[1:2, cs]
        acc = acc + cst_ref[2, :, cs] * cw_ref[2:3, cs]
        acc = acc + u * cw_ref[3:4, cs]
        y = _silu(acc)
        if gi == 0:
            y = _l2norm_heads(y, A_DK ** -0.5)
        elif gi == 1:
            y = _l2norm_heads(y, 1.0)
        o_ref[...] = y
        ncs_ref[0, :, cs] = cst_ref[1, :, cs]
        ncs_ref[1, :, cs] = cst_ref[2, :, cs]
        ncs_ref[2, :, cs] = u

    za_ref[...] = _dot(h, w_ref[:, C_ZA:C_ZA + A_WIDTH])
    gb_ref[...] = _gate_lanes(_dot(h, w_ref[:, C_BD:C_BD + LANES]), alog_ref[...], dt_ref[...])

    cos = cos_ref[...]
    sin = sin_ref[...]
    uq = _dot(h, w_ref[:, C_QB:C_QB + B_WIDTH])
    for g in range(B_WIDTH // LANES):
        qb_ref[:, g * LANES:(g + 1) * LANES] = (
            _rotary_group(uq[:, g * LANES:(g + 1) * LANES], cos, sin) * (B_HD ** -0.5))
    kb_ref[...] = _rotary_group(_dot(h, w_ref[:, C_KB:C_KB + LANES]), cos, sin)
    vb_ref[...] = _dot(h, w_ref[:, C_VB:C_VB + LANES])
    zb_ref[...] = _dot(h, w_ref[:, C_ZB:C_ZB + B_WIDTH])


def _sproj(x, mod_s, w_r, conv_w, cst, alog_row, dt_row, cos_row, sin_row):
    n = x.shape[0]
    full = lambda s: pl.BlockSpec(s, lambda i: (0,) * len(s))
    wide = lambda w: jax.ShapeDtypeStruct((n, w), F32)
    return pl.pallas_call(
        _sproj_kernel,
        grid=(1,),
        in_specs=[full((n, D_MODEL)), pl.BlockSpec((n, 3 * D_MODEL), lambda i: (0, 0)),
                  full((D_MODEL, W_COLS)),
                  full((CONV_W, A_QKV)), full((CONV_W - 1, n, A_QKV)),
                  full((1, LANES)), full((1, LANES)), full((1, LANES)), full((1, LANES))],
        out_specs=[full((n, A_WIDTH)), full((n, A_WIDTH)), full((n, A_WIDTH)), full((n, A_WIDTH)),
                   full((n, LANES)), full((n, B_WIDTH)), full((n, LANES)), full((n, LANES)),
                   full((n, B_WIDTH)), full((CONV_W - 1, n, A_QKV))],
        out_shape=[wide(A_WIDTH), wide(A_WIDTH), wide(A_WIDTH), wide(A_WIDTH), wide(LANES),
                   wide(B_WIDTH), wide(LANES), wide(LANES), wide(B_WIDTH),
                   jax.ShapeDtypeStruct((CONV_W - 1, n, A_QKV), F32)],
        compiler_params=pltpu.CompilerParams(dimension_semantics=("arbitrary",),
                                             vmem_limit_bytes=VMEM_LIMIT),
        name="sproj",
    )(x, mod_s, w_r, conv_w, cst, alog_row, dt_row, cos_row, sin_row)


def _sstep_kernel(sink_ref, q_ref, k_ref, v_ref, gb_ref, za_ref, na_ref, st_ref,
                  qb_ref, kn_ref, vn_ref, zb_ref, ck_ref, cv_ref,
                  oa_ref, ob_ref, nst_ref, nck_ref, ncv_ref,
                  o_scr, ob_scr):
    bt = q_ref.shape[0]
    gbv = gb_ref[...]

    pick = (lax.broadcasted_iota(jnp.int32, (bt, bt * A_DV), 1) // A_DV
            == lax.broadcasted_iota(jnp.int32, (bt, bt * A_DV), 0))
    pick = jnp.where(pick, 1.0, 0.0).astype(BF16)
    for h in range(A_HEADS):
        hs = slice(h * A_DK, (h + 1) * A_DK)
        q_rep = _dot(q_ref[:, hs].T.astype(BF16), pick)
        k_rep = _dot(k_ref[:, hs].T.astype(BF16), pick)
        for bb in range(bt):
            eg = jnp.exp(gbv[bb:bb + 1, A_HEADS + h:A_HEADS + h + 1])
            beta = gbv[bb:bb + 1, h:h + 1]
            kcol = k_rep[:, bb * A_DV:(bb + 1) * A_DV]
            qcol = q_rep[:, bb * A_DV:(bb + 1) * A_DV]
            s1 = eg * st_ref[bb, h]
            pred = jnp.sum(kcol * s1, axis=0, keepdims=True)
            upd = beta * (v_ref[bb:bb + 1, hs] - pred)
            s2 = s1 + kcol * upd
            nst_ref[bb, h] = s2
            o_scr[bb:bb + 1, hs] = jnp.sum(qcol * s2, axis=0, keepdims=True)
    na = na_ref[...]
    for h in range(A_HEADS):
        hs = slice(h * A_DK, (h + 1) * A_DK)
        o = o_scr[:, hs]
        on = o * lax.rsqrt(jnp.mean(o * o, axis=-1, keepdims=True) + RMS_EPS) * na
        oa_ref[:, hs] = (on * _silu(za_ref[:, hs])).astype(BF16)

    row8 = lax.broadcasted_iota(jnp.int32, (B_HEADS, LANES), 0)
    lane8 = _lane((B_HEADS, LANES))
    own_half = (lane8 >= B_HD) == (row8 >= B_GROUP)
    rcol = lax.broadcasted_iota(jnp.int32, (B_HEADS, 1), 0)
    sink = jnp.zeros((B_HEADS, 1), F32)
    for r in range(B_HEADS):
        sink = jnp.where(rcol == r, sink_ref[r], sink)
    qv = qb_ref[...]
    qv_r = jnp.concatenate([pltpu.roll(qv[:, g * LANES:(g + 1) * LANES], B_HD, axis=1)
                            for g in range(B_WIDTH // LANES)], axis=-1)
    kn_t = kn_ref[...].T
    vn_t = vn_ref[...].T
    newest = _lane((LANES, WINDOW)) == WINDOW - 1
    qzs, scs = [], []
    for bb in range(bt):
        qz = jnp.zeros((B_HEADS, LANES), F32)
        for r in range(B_HEADS):
            grp, half, kh = r // 2, r % 2, r // B_GROUP
            src = qv if half == kh else qv_r
            qz = jnp.where(row8 == r, src[bb:bb + 1, grp * LANES:(grp + 1) * LANES], qz)
        qzs.append(jnp.where(own_half, qz, 0.0))
    for bb in range(bt):
        scs.append(_dot(qzs[bb], ck_ref[bb]))
    ps, pnews, dens = [], [], []
    for bb in range(bt):
        sc_new = jnp.sum(qzs[bb] * kn_ref[bb:bb + 1, :], axis=-1, keepdims=True)
        m = jnp.maximum(jnp.maximum(jnp.max(scs[bb], axis=-1, keepdims=True), sc_new), sink)
        p = jnp.exp(scs[bb] - m)
        p_new = jnp.exp(sc_new - m)
        ps.append(p)
        pnews.append(p_new)
        dens.append(jnp.sum(p, axis=-1, keepdims=True) + p_new + jnp.exp(sink - m))
    pvs = [_dot_nt(ps[bb], cv_ref[bb]) for bb in range(bt)]
    for bb in range(bt):
        o = (pvs[bb] + pnews[bb] * vn_ref[bb:bb + 1, :]) / dens[bb]
        o = jnp.where(own_half, o, 0.0)
        ob_scr[bb * B_HEADS:(bb + 1) * B_HEADS, :] = o + pltpu.roll(o, B_HD, axis=1)
    for bb in range(bt):
        nck_ref[bb] = jnp.where(newest, kn_t[:, bb:bb + 1], pltpu.roll(ck_ref[bb], WINDOW - 1, axis=1))
        ncv_ref[bb] = jnp.where(newest, vn_t[:, bb:bb + 1], pltpu.roll(cv_ref[bb], WINDOW - 1, axis=1))
    low = _lane((bt, LANES)) < B_HD
    for grp in range(B_WIDTH // LANES):
        even = ob_scr[pl.ds(2 * grp, bt, stride=B_HEADS), :]
        odd = ob_scr[pl.ds(2 * grp + 1, bt, stride=B_HEADS), :]
        gs = slice(grp * LANES, (grp + 1) * LANES)
        ob_ref[:, gs] = (jnp.where(low, even, odd) * _silu(zb_ref[:, gs])).astype(BF16)


def _sstep(sinks, q, k, v, gb, za, na_row, state, qb, kn, vn, zb, ck, cv):
    n = q.shape[0]
    bt = STEP_BT
    row = lambda w: pl.BlockSpec((bt, w), lambda i: (i, 0))
    st_spec = pl.BlockSpec((bt, A_HEADS, A_DK, A_DV), lambda i: (i, 0, 0, 0))
    c_spec = pl.BlockSpec((bt, WINDOW, LANES), lambda i: (i, 0, 0))
    return pl.pallas_call(
        _sstep_kernel,
        grid=(n // bt,),
        in_specs=[pl.BlockSpec(memory_space=pltpu.SMEM),
                  row(A_WIDTH), row(A_WIDTH), row(A_WIDTH), row(LANES), row(A_WIDTH),
                  pl.BlockSpec((1, A_DV), lambda i: (0, 0)), st_spec,
                  row(B_WIDTH), row(LANES), row(LANES), row(B_WIDTH), c_spec, c_spec],
        out_specs=[row(A_WIDTH), row(B_WIDTH), st_spec, c_spec, c_spec],
        out_shape=[jax.ShapeDtypeStruct((n, A_WIDTH), BF16),
                   jax.ShapeDtypeStruct((n, B_WIDTH), BF16),
                   jax.ShapeDtypeStruct((n, A_HEADS, A_DK, A_DV), F32),
                   jax.ShapeDtypeStruct((n, WINDOW, LANES), F32),
                   jax.ShapeDtypeStruct((n, WINDOW, LANES), F32)],
        scratch_shapes=[pltpu.VMEM((bt, A_WIDTH), F32), pltpu.VMEM((bt * B_HEADS, LANES), F32)],
        compiler_params=pltpu.CompilerParams(dimension_semantics=("arbitrary",),
                                             vmem_limit_bytes=VMEM_LIMIT),
        name="sstep",
    )(sinks, q, k, v, gb, za, na_row, state, qb, kn, vn, zb, ck, cv)


def _rope_tables(pos):
    half = B_HD // 2
    inv = 1.0 / (ROPE_THETA ** (np.arange(half, dtype=np.float64) / half))
    ang = np.asarray(pos, np.float64)[:, None] * inv[None, :]
    cos, sin = np.cos(ang), np.sin(ang)
    reps = LANES // B_HD
    return (jnp.asarray(np.tile(np.concatenate([cos, cos], -1), (1, reps)), F32),
            jnp.asarray(np.tile(np.concatenate([-sin, sin], -1), (1, reps)), F32))


def _pad_row(vec, offset):
    return jnp.pad(vec.astype(F32).reshape(1, -1), ((0, 0), (offset, LANES - offset - vec.shape[0])))


def _layer(x_prompt, x_sample, state_conv, state_delta, cache_k, cache_v, c_prompt, c_sample,
           w_ada, b_ada, w_in, conv_w, a_log, dt_bias, norm_a, sinks, w_out, ln_g, ln_b):
    bsz, seq, _ = x_prompt.shape
    n_s = x_sample.shape[0]

    w_r = _wprep(jnp.swapaxes(w_in, 0, 1))
    w_o = w_out.astype(BF16)
    alog_row = _pad_row(a_log, A_HEADS)
    dt_row = _pad_row(dt_bias, A_HEADS)
    na_row = norm_a.reshape(1, A_DV)
    g_row = ln_g.reshape(1, D_MODEL)
    b_row = ln_b.reshape(1, D_MODEL)

    assert n_s % 8 == 0 and bsz <= 8
    mod = _ada(c_sample, c_prompt, w_ada, b_ada.reshape(1, 3 * D_MODEL))

    cos_p, sin_p = _rope_tables(np.arange(seq))
    (qkv, za, gb, qb, kvl, zb, conv_p, kb_last, vb_last) = _proj(
        x_prompt, mod, n_s, w_r, conv_w, alog_row, dt_row, cos_p, sin_p)
    oa, delta_p = _delta(qkv, gb, za, na_row)
    y_p = _swa_out(sinks, qb, kvl, zb, oa, x_prompt, mod, n_s, w_o, g_row, b_row)
    swa_k_p = kb_last.reshape(bsz, WINDOW, B_KV_HEADS, B_HD)
    swa_v_p = vb_last.reshape(bsz, WINDOW, B_KV_HEADS, B_HD)

    cos_s, sin_s = _rope_tables(np.array([PAST_LEN]))
    xs = x_sample.reshape(n_s, D_MODEL)
    cst = jnp.transpose(state_conv, (1, 0, 2))
    sq, sk, sv, sza, sgb, sqb, skn, svn, szb, ncs = _sproj(xs, mod, w_r, conv_w, cst, alog_row, dt_row,
                                                           cos_s, sin_s)
    soa, sob, delta_s, nck, ncv = _sstep(sinks, sq, sk, sv, sgb, sza, na_row, state_delta,
                                         sqb, skn, svn, szb,
                                         jnp.swapaxes(cache_k.reshape(n_s, WINDOW, LANES), 1, 2),
                                         jnp.swapaxes(cache_v.reshape(n_s, WINDOW, LANES), 1, 2))
    y_s = _out(soa, sob, xs, mod, w_o, g_row, b_row)
    conv_s = jnp.transpose(ncs, (1, 0, 2))
    unpack = lambda c: jnp.swapaxes(c, 1, 2).reshape(n_s, WINDOW, B_KV_HEADS, B_HD)
    return (y_p, y_s.reshape(n_s, 1, D_MODEL), conv_p, delta_p, swa_k_p, swa_v_p,
            conv_s, delta_s, unpack(nck), unpack(ncv))


def kernel(x_prompt, x_sample, state_conv, state_delta, cache_swa_k, cache_swa_v, c_prompt, c_sample,
           w_ada, b_ada, w_in, conv_w, a_log, dt_bias, norm_a, sinks, w_out, ln_g, ln_b):
    assert w_ada.shape[0] == DEPTH == 1
    outs = _layer(x_prompt, x_sample, state_conv[0], state_delta[0], cache_swa_k[0], cache_swa_v[0],
                  c_prompt, c_sample, w_ada[0], b_ada[0], w_in[0], conv_w[0], a_log[0], dt_bias[0],
                  norm_a[0], sinks[0], w_out[0], ln_g[0], ln_b[0])
    y_p, y_s = outs[0], outs[1]
    return (y_p, y_s) + tuple(o[None] for o in outs[2:])
```

```python
import jax
import jax.numpy as jnp
import numpy as np
from jax import lax
from jax.experimental import pallas as pl
from jax.experimental.pallas import tpu as pltpu

F32 = jnp.float32
BF16 = jnp.bfloat16

D_MODEL = 1024
DEPTH = 1
PAST_LEN = 8192
A_HEADS = 4
A_DK = 128
A_DV = 128
A_WIDTH = A_HEADS * A_DV
A_QKV = 3 * A_WIDTH
CONV_W = 4
CHUNK = 64
B_HEADS = 8
B_KV_HEADS = 2
B_HD = 64
B_GROUP = B_HEADS // B_KV_HEADS
B_WIDTH = B_HEADS * B_HD
B_KV_WIDTH = B_KV_HEADS * B_HD
WINDOW = 128
ROPE_THETA = 10000.0
MIX_WIDTH = A_WIDTH + B_WIDTH
DEEPNORM_ALPHA = (2 * DEPTH) ** 0.25
LOG2E = 1.4426950408889634
LN_EPS = 1e-5
RMS_EPS = 1e-6
L2_EPS = 1e-6

OFF_A_Z = A_QKV
OFF_A_BETA = OFF_A_Z + A_WIDTH
OFF_A_DECAY = OFF_A_BETA + A_HEADS
OFF_B_Q = OFF_A_DECAY + A_HEADS
OFF_B_K = OFF_B_Q + B_WIDTH
OFF_B_V = OFF_B_K + B_KV_WIDTH
OFF_B_Z = OFF_B_V + B_KV_WIDTH
PROJ_COLS = OFF_B_Z + B_WIDTH

LANES = 128
C_QKV = 0
C_ZA = C_QKV + A_QKV
C_QB = C_ZA + A_WIDTH
C_KB = C_QB + B_WIDTH
C_VB = C_KB + B_KV_WIDTH
C_ZB = C_VB + B_KV_WIDTH
C_BD = C_ZB + B_WIDTH
WPREP_TN = 256
W_COLS = C_BD + WPREP_TN

VMEM_LIMIT = 56 * 1024 * 1024

ADA_TN = 1536
PROJ_TM = 512
PROJ_CW = 256
PROJ_PART_ROWS = (128, 256, 128)
DELTA_CT = 256
DELTA_WAVE = 2
SWA_TQ = 512
SWA_WAVE = 2
STEP_BT = 16


def _dot(a, b):
    return jnp.dot(a, b, preferred_element_type=F32)


def _dot_nt(a, b):
    return lax.dot_general(a, b, (((1,), (1,)), ((), ())), preferred_element_type=F32)


def _silu(x):
    return x * jax.nn.sigmoid(x)


def _softplus(x):
    return jnp.maximum(x, 0.0) + jnp.log1p(jnp.exp(-jnp.abs(x)))


def _lane(shape):
    return lax.broadcasted_iota(jnp.int32, shape, len(shape) - 1)


def _l2norm_heads(y, scale):
    outs = []
    for h in range(y.shape[1] // A_DK):
        xh = y[:, h * A_DK:(h + 1) * A_DK]
        ss = jnp.sum(xh * xh, axis=-1, keepdims=True)
        xn = xh * lax.rsqrt(ss + L2_EPS)
        outs.append(xn * scale if scale != 1.0 else xn)
    return jnp.concatenate(outs, axis=-1)


def _rotary_group(xg, cos, sin_signed):
    lane = _lane(xg.shape)
    swapped = jnp.where((lane % B_HD) < (B_HD // 2),
                        pltpu.roll(xg, LANES - B_HD // 2, axis=1),
                        pltpu.roll(xg, B_HD // 2, axis=1))
    return xg * cos + swapped * sin_signed


def _kv_layouts(kb, vb):
    low = _lane(kb.shape) < B_HD
    kbr = pltpu.roll(kb, B_HD, axis=1)
    vbr = pltpu.roll(vb, B_HD, axis=1)
    return kb, kbr, jnp.where(low, vb, vbr), jnp.where(low, vbr, vb)


def _gate_lanes(bd, alog_row, dt_row):
    lane = _lane(bd.shape)
    g = -jnp.exp(alog_row) * _softplus(bd + dt_row)
    return jnp.where(lane < A_HEADS, jax.nn.sigmoid(bd), g)


def _layer_norm(r, g, b):
    mu = jnp.mean(r, axis=-1, keepdims=True)
    d = r - mu
    var = jnp.mean(d * d, axis=-1, keepdims=True)
    return d * lax.rsqrt(var + LN_EPS) * g + b


def _wprep_kernel(wa_ref, wb_ref, o_ref):
    tn = wa_ref.shape[0]
    o_ref[:, 0:tn] = wa_ref[...].T.astype(BF16)
    xb = wb_ref[...]
    tail = pl.program_id(0) == pl.num_programs(0) - 1
    row = lax.broadcasted_iota(jnp.int32, xb.shape, 0)
    xb = jnp.where(jnp.logical_and(tail, row >= 2 * A_HEADS), 0.0, xb)
    o_ref[:, tn:2 * tn] = xb.T.astype(BF16)


def _wprep(w_t):
    tn = WPREP_TN
    n_a, n_b = OFF_A_BETA // tn, (PROJ_COLS - OFF_B_Q) // tn
    assert n_a * tn == OFF_A_BETA and n_b * tn == PROJ_COLS - OFF_B_Q and OFF_A_BETA + tn <= PROJ_COLS

    assert (n_a + n_b + 1) % 2 == 0

    def src_row(j):
        return jnp.where(j < n_a, j * tn, jnp.where(j < n_a + n_b, OFF_B_Q + (j - n_a) * tn, OFF_A_BETA))

    src = lambda k: pl.BlockSpec((pl.Element(tn), pl.Element(D_MODEL)),
                                 lambda j: (pl.multiple_of(src_row(2 * j + k), 8), 0))
    return pl.pallas_call(
        _wprep_kernel,
        grid=((n_a + n_b + 1) // 2,),
        in_specs=[src(0), src(1)],
        out_specs=pl.BlockSpec((D_MODEL, 2 * tn), lambda j: (0, j)),
        out_shape=jax.ShapeDtypeStruct((D_MODEL, W_COLS), BF16),
        compiler_params=pltpu.CompilerParams(dimension_semantics=("arbitrary",),
                                             vmem_limit_bytes=VMEM_LIMIT),
        name="wprep",
    )(w_t, w_t)


def _ada_kernel(cs_ref, cp_ref, w_ref, b_ref, o_ref):
    n_s, n_p = cs_ref.shape[0], cp_ref.shape[0]
    w = w_ref[...].astype(BF16)
    o_ref[0:n_s, :] = _dot(cs_ref[...].astype(BF16), w) + b_ref[...]
    o_ref[n_s:n_s + 8, :] = jnp.zeros((8, o_ref.shape[1]), F32)
    o_ref[n_s:n_s + n_p, :] = _dot(cp_ref[...].astype(BF16), w) + b_ref[...]


def _ada(c_sample, c_prompt, w_ada, b_ada):
    n_s, n_p = c_sample.shape[0], c_prompt.shape[0]
    rows = n_s + 8
    tn = ADA_TN
    return pl.pallas_call(
        _ada_kernel,
        grid=(3 * D_MODEL // tn,),
        in_specs=[pl.BlockSpec((n_s, D_MODEL), lambda j: (0, 0)),
                  pl.BlockSpec((n_p, D_MODEL), lambda j: (0, 0)),
                  pl.BlockSpec((D_MODEL, tn), lambda j: (0, j)),
                  pl.BlockSpec((1, tn), lambda j: (0, j))],
        out_specs=pl.BlockSpec((rows, tn), lambda j: (0, j)),
        out_shape=jax.ShapeDtypeStruct((rows, 3 * D_MODEL), F32),
        compiler_params=pltpu.CompilerParams(dimension_semantics=("arbitrary",),
                                             vmem_limit_bytes=VMEM_LIMIT),
        name="ada",
    )(c_sample, c_prompt, w_ada, b_ada)


def _proj_kernel(x_ref, mod_ref, w_ref, cw_ref, alog_ref, dt_ref, cos_ref, sin_ref,
                 qkv_ref, za_ref, gb_ref, qb_ref, kvl_ref, zb_ref, cst_ref, kbl_ref, vbl_ref, ubuf):
    tm = x_ref.shape[1]
    t = pl.program_id(1)

    @pl.when(t == 0)
    def _():
        ubuf[...] = jnp.zeros(ubuf.shape, F32)

    brow = pl.ds(pl.program_id(0), 1)
    shift = mod_ref[brow, 0:D_MODEL]
    scale = mod_ref[brow, D_MODEL:2 * D_MODEL]

    assert sum(PROJ_PART_ROWS) == tm and PROJ_PART_ROWS[-1] >= WINDOW
    cw = PROJ_CW
    pieces = [slice(c0, c0 + cw) for c0 in range(0, A_QKV, cw)]

    def matmuls(r0, rp):
        h = (x_ref[0, r0:r0 + rp, :] * (1.0 + scale) + shift).astype(BF16)
        return ([_dot(h, w_ref[:, cs]) for cs in pieces],
                _dot(h, w_ref[:, C_ZA:C_ZA + A_WIDTH]), _dot(h, w_ref[:, C_BD:C_BD + LANES]),
                _dot(h, w_ref[:, C_QB:C_QB + B_WIDTH]), _dot(h, w_ref[:, C_KB:C_KB + 2 * LANES]),
                _dot(h, w_ref[:, C_ZB:C_ZB + B_WIDTH]))

    def epilogue(r0, rp, results):
        rs = slice(r0, r0 + rp)
        us, za, ubd, uq, ukv, zb = results
        sub = lax.broadcasted_iota(jnp.int32, (rp // 8, 8, cw), 1)

        def conv_epilogue(cs, u):
            gi = cs.start // A_WIDTH
            groups = jnp.concatenate([ubuf[:, cs], u], axis=0).reshape(rp // 8 + 1, 8, cw)
            acc = None
            for j in range(CONV_W - 1, 0, -1):
                rot = pltpu.roll(groups, j, axis=1)
                term = (jnp.where(sub < j, rot[:-1], rot[1:]).reshape(rp, cw)
                        * cw_ref[CONV_W - 1 - j:CONV_W - j, cs])
                acc = term if acc is None else acc + term
            y = _silu(acc + u * cw_ref[CONV_W - 1:CONV_W, cs])
            if gi == 0:
                y = _l2norm_heads(y, A_DK ** -0.5)
            elif gi == 1:
                y = _l2norm_heads(y, 1.0)
            qkv_ref[0, rs, cs] = y
            ubuf[:, cs] = u[rp - 8:rp]
            if r0 + rp == tm:
                cst_ref[0, :, cs] = u[rp - (CONV_W - 1):rp]

        for cs, u in zip(pieces, us):
            conv_epilogue(cs, u)
        za_ref[0, rs, :] = za
        zb_ref[0, rs, :] = zb
        gb_ref[0, rs, :] = _gate_lanes(ubd, alog_ref[...], dt_ref[...])
        cos = cos_ref[rs, :]
        sin = sin_ref[rs, :]
        for g in range(B_WIDTH // LANES):
            qb_ref[0, rs, g * LANES:(g + 1) * LANES] = (
                _rotary_group(uq[:, g * LANES:(g + 1) * LANES], cos, sin) * (B_HD ** -0.5 * LOG2E)).astype(BF16)
        kb = _rotary_group(ukv[:, 0:LANES], cos, sin)
        vb = ukv[:, LANES:2 * LANES]
        for j, val in enumerate(_kv_layouts(kb, vb)):
            kvl_ref[0, rs, j * LANES:(j + 1) * LANES] = val.astype(BF16)
        return kb, vb

    parts, r0 = [], 0
    for rp in PROJ_PART_ROWS:
        parts.append((r0, rp))
        r0 += rp
    pending = matmuls(*parts[0])
    for p, part in enumerate(parts):
        results = pending
        if p + 1 < len(parts):
            pending = matmuls(*parts[p + 1])
        kb, vb = epilogue(*part, results)

    @pl.when(t == pl.num_programs(1) - 1)
    def _():
        kbl_ref[0] = kb[kb.shape[0] - WINDOW:, :]
        vbl_ref[0] = vb[vb.shape[0] - WINDOW:, :]


def _proj(x, mod, mod_row0, w_r, conv_w, alog_row, dt_row, cos_t, sin_t):
    bsz, t, _ = x.shape
    tm = PROJ_TM
    row = lambda w: pl.BlockSpec((1, tm, w), lambda b, i: (b, i, 0))
    const2 = lambda s: pl.BlockSpec(s, lambda b, i: (0, 0))
    per_b = lambda r, w: pl.BlockSpec((1, r, w), lambda b, i: (b, 0, 0))
    wide = lambda w, dt=F32: jax.ShapeDtypeStruct((bsz, t, w), dt)
    return pl.pallas_call(
        _proj_kernel,
        grid=(bsz, t // tm),
        in_specs=[row(D_MODEL),
                  pl.BlockSpec((8, 3 * D_MODEL), lambda b, i: (mod_row0 // 8, 0)),
                  const2((D_MODEL, W_COLS)),
                  const2((CONV_W, A_QKV)),
                  const2((1, LANES)), const2((1, LANES)),
                  pl.BlockSpec((tm, LANES), lambda b, i: (i, 0)),
                  pl.BlockSpec((tm, LANES), lambda b, i: (i, 0))],
        out_specs=[row(A_QKV), row(A_WIDTH), row(LANES), row(B_WIDTH), row(4 * LANES), row(B_WIDTH),
                   per_b(CONV_W - 1, A_QKV), per_b(WINDOW, LANES), per_b(WINDOW, LANES)],
        out_shape=[wide(A_QKV), wide(A_WIDTH), wide(LANES), wide(B_WIDTH, BF16), wide(4 * LANES, BF16),
                   wide(B_WIDTH),
                   jax.ShapeDtypeStruct((bsz, CONV_W - 1, A_QKV), F32),
                   jax.ShapeDtypeStruct((bsz, WINDOW, LANES), F32),
                   jax.ShapeDtypeStruct((bsz, WINDOW, LANES), F32)],
        scratch_shapes=[pltpu.VMEM((8, A_QKV), F32)],
        compiler_params=pltpu.CompilerParams(dimension_semantics=("arbitrary", "arbitrary"),
                                             vmem_limit_bytes=VMEM_LIMIT),
        name="proj",
    )(x, mod, w_r, conv_w, alog_row, dt_row, cos_t, sin_t)


def _delta_kernel(q_ref, k_ref, v_ref, gb_ref, za_ref, na_ref, oa_ref, st_ref,
                  s_scr, wq_s, ut_s, akd_s, gl_s):
    bsz, ct = q_ref.shape[0], q_ref.shape[1]
    nch = ct // CHUNK
    t = pl.program_id(0)
    wslot = t % 2
    rslot = 1 - wslot

    @pl.when(t == 0)
    def _():
        s_scr[...] = jnp.zeros(s_scr.shape, F32)
        wq_s[...] = jnp.zeros(wq_s.shape, BF16)
        ut_s[...] = jnp.zeros(ut_s.shape, F32)
        akd_s[...] = jnp.zeros(akd_s.shape, BF16)
        gl_s[...] = jnp.zeros(gl_s.shape, F32)

    units = [(b, c, h) for b in range(bsz) for c in range(nch) for h in range(A_HEADS)]
    uid = {u_: i for i, u_ in enumerate(units)}
    rows = lambda c: slice(c * CHUNK, (c + 1) * CHUNK)
    lanes = lambda h: slice(h * A_DK, (h + 1) * A_DK)
    na = na_ref[...]

    s_cur = {(b, h): s_scr[b * A_HEADS + h] for b in range(bsz) for h in range(A_HEADS)}
    ws, uu = {}, {}

    def rec_ws(c):
        for b in range(bsz):
            for h in range(A_HEADS):
                i = uid[b, c, h]
                ws[b, h] = _dot(wq_s[rslot, i], s_cur[b, h].astype(BF16))
                uu[b, h] = (ut_s[rslot, i] - ws[b, h][:CHUNK]).astype(BF16)

    def rec_ou(c):
        zpad = jnp.zeros((CHUNK, A_DV), BF16)
        for b in range(bsz):
            u_bd = jnp.concatenate(
                [jnp.concatenate([uu[b, h] if hh == h else zpad for hh in range(A_HEADS)], axis=-1)
                 for h in range(A_HEADS)], axis=0)
            ou = _dot(akd_s[rslot, b * nch + c], u_bd)
            for h in range(A_HEADS):
                o = ws[b, h][CHUNK:] + ou[:CHUNK, lanes(h)]
                s_cur[b, h] = gl_s[rslot, uid[b, c, h]] * s_cur[b, h] + ou[CHUNK:, lanes(h)]
                on = o * lax.rsqrt(jnp.mean(o * o, axis=-1, keepdims=True) + RMS_EPS) * na
                oa_ref[b, rows(c), lanes(h)] = (on * _silu(za_ref[b, rows(c), lanes(h)])).astype(BF16)

    rec_stages = []
    for c in range(nch):
        rec_stages += [lambda c=c: rec_ws(c), lambda c=c: rec_ou(c)]

    def run_rec(n_left_after):
        while rec_stages and len(rec_stages) > n_left_after:
            rec_stages.pop(0)()

    pk = A_HEADS * CHUNK
    low = _lane((CHUNK, LANES)) < CHUNK
    low_row = _lane((1, LANES)) < CHUNK
    ti_p = lax.broadcasted_iota(jnp.int32, (CHUNK, pk), 0)
    ii_p = _lane((CHUNK, pk)) % CHUNK
    zero64 = jnp.zeros((CHUNK, LANES), BF16)

    def pack(parts):
        return jnp.concatenate([jnp.where(low, parts[0], parts[1]), jnp.where(low, parts[2], parts[3])], axis=-1)

    def block_diag(x16):
        blocks = []
        for h in range(A_HEADS):
            pair, first = h // 2, h % 2 == 0
            piece = jnp.where(low if first else jnp.logical_not(low), x16[:, pair * LANES:(pair + 1) * LANES], zero64)
            blocks.append(jnp.concatenate([piece, zero64] if pair == 0 else [zero64, piece], axis=-1))
        return jnp.concatenate(blocks, axis=0)

    zrhs = jnp.zeros((CHUNK, 2 * A_DK), BF16)
    n_rec = len(rec_stages)
    n_slots = 8 * (bsz // DELTA_WAVE)
    done = [0]

    def stage_done():
        done[0] += 1
        run_rec(n_rec - 1 - (done[0] * n_rec) // n_slots)

    def decay_terms(b, beta, g_col, g_last, eg, dec_p, beta_p):
        gbv = gb_ref[b]
        rin = lax.broadcasted_iota(jnp.int32, gbv.shape, 0) % CHUNK
        gcs = gbv
        s = 1
        while s < CHUNK:
            gcs = gcs + jnp.where(rin >= s, pltpu.roll(gcs, s, axis=0), 0.0)
            s *= 2
        gcs_t = gcs.T
        for c in range(nch):
            r0 = c * CHUNK
            pair_lanes = slice((c // 2) * LANES, (c // 2 + 1) * LANES)
            g_rows = []
            for h in range(A_HEADS):
                u_ = (b, c, h)
                beta[u_] = jnp.broadcast_to(gbv[rows(c), h:h + 1], (CHUNK, A_DK))
                g_col[u_] = jnp.broadcast_to(gcs[rows(c), A_HEADS + h:A_HEADS + h + 1], (CHUNK, A_DK))
                g_last[u_] = gcs[r0 + CHUNK - 1:r0 + CHUNK, A_HEADS + h:A_HEADS + h + 1]
                eg[u_] = jnp.exp(g_col[u_])
                g_row = gcs_t[A_HEADS + h:A_HEADS + h + 1, pair_lanes]
                g_rows.append(g_row if c % 2 == h % 2 else pltpu.roll(g_row, CHUNK, axis=1))
            g_row_p = jnp.concatenate([jnp.where(low_row, g_rows[0], g_rows[1]),
                                       jnp.where(low_row, g_rows[2], g_rows[3])], axis=-1)
            g_col_p = pack([g_col[b, c, h] for h in range(A_HEADS)])
            dec_p[b, c] = jnp.exp(jnp.where(ti_p >= ii_p, g_col_p - g_row_p, -jnp.inf))
            beta_p[b, c] = pack([beta[b, c, h] for h in range(A_HEADS)])

    def prepare(bs):
        groups_b = [(b, c) for b in bs for c in range(nch)]
        beta, g_col, g_last, eg, dec_p, beta_p = {}, {}, {}, {}, {}, {}
        for b in bs:
            decay_terms(b, beta, g_col, g_last, eg, dec_p, beta_p)

        nmat = {}
        for (b, c) in groups_b:
            k16 = k_ref[b, rows(c), :].astype(BF16)
            q16 = q_ref[b, rows(c), :].astype(BF16)
            k_heads = jnp.concatenate(
                [jnp.concatenate([k16[:, lanes(h)] if hh == h else zero64 for hh in range(A_HEADS)], axis=-1)
                 for h in range(A_HEADS)], axis=0)
            kq = _dot_nt(jnp.concatenate([k16, q16], axis=0), k_heads)
            nmat[b, c] = -(beta_p[b, c] * kq[:CHUNK] * jnp.where(ti_p > ii_p, dec_p[b, c], 0.0))
            akd_s[wslot, b * nch + c, 0:CHUNK, :] = (kq[CHUNK:] * dec_p[b, c]).astype(BF16)
        stage_done()

        rsum = dict(nmat)
        pw16 = {g_: nmat[g_].astype(BF16) for g_ in groups_b}
        pw = {g_: _dot(pw16[g_], block_diag(pw16[g_])) for g_ in groups_b}
        stage_done()
        for step in range(1, 6):
            last = step == 5
            pw16 = {g_: pw[g_].astype(BF16) for g_ in groups_b}
            rp = {}
            for g_ in groups_b:
                r16 = rsum[g_].astype(BF16)
                rp[g_] = _dot(r16 if last else jnp.concatenate([r16, pw16[g_]], axis=0), block_diag(pw16[g_]))
            for g_ in groups_b:
                rsum[g_] = rsum[g_] + pw[g_] + rp[g_][:CHUNK]
                if not last:
                    pw[g_] = rp[g_][CHUNK:]
            stage_done()

        for (b, c) in groups_b:
            for h in range(A_HEADS):
                u_ = (b, c, h)
                i = uid[u_]
                kc = k_ref[b, rows(c), lanes(h)]
                rhs = jnp.concatenate([(beta[u_] * eg[u_]) * kc, beta[u_] * v_ref[b, rows(c), lanes(h)]],
                                      axis=-1)
                rhs16 = rhs.astype(BF16)
                rhs_rows = jnp.concatenate([rhs16 if hh == h else zrhs for hh in range(A_HEADS)], axis=0)
                sol = rhs + _dot(rsum[b, c].astype(BF16), rhs_rows)
                wq_s[wslot, i] = jnp.concatenate([sol[:, :A_DK], eg[u_] * q_ref[b, rows(c), lanes(h)]],
                                                 axis=0).astype(BF16)
                ut_s[wslot, i] = sol[:, A_DK:]
                gl_s[wslot, i] = jnp.broadcast_to(jnp.exp(g_last[u_]), (1, A_DV))
        for (b, c) in groups_b:
            kd = [jnp.exp(g_last[b, c, h] - g_col[b, c, h]) * k_ref[b, rows(c), lanes(h)]
                  for h in range(A_HEADS)]
            for p in range(A_HEADS // 2):
                akd_s[wslot, b * nch + c, CHUNK:, p * LANES:(p + 1) * LANES] = (
                    jnp.concatenate([kd[2 * p], kd[2 * p + 1]], axis=0).T.astype(BF16))
        stage_done()

    run_rec(n_rec - 1)
    for b0 in range(0, bsz, DELTA_WAVE):
        prepare(range(b0, b0 + DELTA_WAVE))
    run_rec(0)

    for b in range(bsz):
        for h in range(A_HEADS):
            s_scr[b * A_HEADS + h] = s_cur[b, h]

    @pl.when(t == pl.num_programs(0) - 1)
    def _():
        for b in range(bsz):
            for h in range(A_HEADS):
                st_ref[b, h] = s_cur[b, h]


def _delta(qkv, gb, za, na_row):
    bsz, t, _ = qkv.shape
    ct = DELTA_CT
    nt = t // ct
    n_units = bsz * (ct // CHUNK) * A_HEADS
    prep = lambda w, j=0: pl.BlockSpec((bsz, ct, w), lambda i: (0, jnp.minimum(i, nt - 1), j))
    rec = lambda w: pl.BlockSpec((bsz, ct, w), lambda i: (0, jnp.maximum(i - 1, 0), 0))
    return pl.pallas_call(
        _delta_kernel,
        grid=(nt + 1,),
        in_specs=[prep(A_WIDTH, 0), prep(A_WIDTH, 1), prep(A_WIDTH, 2), prep(LANES), rec(A_WIDTH),
                  pl.BlockSpec((1, A_DV), lambda i: (0, 0))],
        out_specs=[rec(A_WIDTH),
                   pl.BlockSpec((bsz, A_HEADS, A_DK, A_DV), lambda i: (0, 0, 0, 0))],
        out_shape=[jax.ShapeDtypeStruct((bsz, t, A_WIDTH), BF16),
                   jax.ShapeDtypeStruct((bsz, A_HEADS, A_DK, A_DV), F32)],
        scratch_shapes=[pltpu.VMEM((bsz * A_HEADS, A_DK, A_DV), F32),
                        pltpu.VMEM((2, n_units, 2 * CHUNK, A_DK), BF16),
                        pltpu.VMEM((2, n_units, CHUNK, A_DV), F32),
                        pltpu.VMEM((2, n_units // A_HEADS, CHUNK + A_DK, A_HEADS * CHUNK), BF16),
                        pltpu.VMEM((2, n_units, 1, A_DV), F32)],
        compiler_params=pltpu.CompilerParams(dimension_semantics=("arbitrary",),
                                             vmem_limit_bytes=VMEM_LIMIT),
        name="delta",
    )(qkv, qkv, qkv, gb, za, na_row)


def _swa_out_kernel(sink_ref, qb_ref, kc_ref, kp_ref, krc_ref, krp_ref, v0c_ref, v0p_ref, v1c_ref, v1p_ref,
                    zb_ref, oa_ref, x_ref, gate_ref, w_ref, g_ref, b_ref, y_ref):
    n = pl.program_id(1)
    gate = gate_ref[pl.ds(pl.program_id(0), 1), :]
    tq = qb_ref.shape[1]
    blk = WINDOW
    kx = (jnp.concatenate([kp_ref[0], kc_ref[0]], axis=0), jnp.concatenate([krp_ref[0], krc_ref[0]], axis=0))
    vd = (jnp.concatenate([v0p_ref[0], v0c_ref[0]], axis=0), jnp.concatenate([v1p_ref[0], v1c_ref[0]], axis=0))
    low_kv = _lane(vd[0].shape) < B_HD
    zero_kv = jnp.zeros(vd[0].shape, BF16)
    kxh = tuple((jnp.where(low_kv, k, zero_kv), jnp.where(low_kv, zero_kv, k)) for k in kx)
    one_kv = jnp.ones(vd[0].shape, BF16)
    vsum = tuple((jnp.where(low_kv, v, one_kv), jnp.where(low_kv, one_kv, v)) for v in vd)

    a = lax.broadcasted_iota(jnp.int32, (2 * blk, 2 * blk), 0) % blk
    j = lax.broadcasted_iota(jnp.int32, (2 * blk, 2 * blk), 1)
    rel = a + blk - j
    band = (rel >= 0) & (rel <= WINDOW)
    band_first = band & ((n > 0) | (j >= blk))
    top = lax.broadcasted_iota(jnp.int32, (2 * blk, 1), 0) < blk
    low = _lane((blk, LANES)) < B_HD

    qrows = lambda i: slice(i * blk, (i + 1) * blk)
    krows = lambda i: slice(i * blk, (i + 2) * blk)
    sink = {(kh, half): jnp.where(top, sink_ref[kh * B_GROUP + half] * LOG2E,
                                  sink_ref[kh * B_GROUP + half + 2] * LOG2E)
            for kh in range(B_KV_HEADS) for half in range(2)}
    def wave_units(i0):
        blocks = range(i0, i0 + SWA_WAVE)
        return blocks, [(i, kh, half) for i in blocks for kh in range(B_KV_HEADS) for half in range(2)]

    def score_matmuls(i0):
        blocks, units = wave_units(i0)
        mix_a = {i: _dot(oa_ref[0, qrows(i), :], w_ref[0:A_WIDTH, :]) for i in blocks}
        sc = {}
        for (i, kh, half) in units:
            q2 = jnp.concatenate([qb_ref[0, qrows(i), (kh * 2 + g) * LANES:(kh * 2 + g + 1) * LANES]
                                  for g in range(2)], axis=0)
            sc[i, kh, half] = _dot_nt(q2, kxh[0 if kh == half else 1][half][krows(i)])
        return mix_a, sc

    def finish(i0, mix_a, sc):
        blocks, units = wave_units(i0)
        p, esink = {}, {}
        for u_ in units:
            i, kh, half = u_
            s_m = jnp.where(band_first if i == 0 else band, sc[u_], -jnp.inf)
            m = jnp.maximum(jnp.max(s_m, axis=-1, keepdims=True), sink[kh, half])
            p[u_] = jnp.exp2(s_m - m).astype(BF16)
            esink[u_] = jnp.exp2(sink[kh, half] - m)
        pv = {u_: _dot(p[u_], vsum[u_[1]][u_[2]][krows(u_[0])]) for u_ in units}
        outs = {u_: pv[u_] / (pltpu.roll(pv[u_], B_HD, axis=1) + esink[u_]) for u_ in units}
        for i in blocks:
            ob = []
            for grp in range(B_WIDTH // LANES):
                kh, g = grp // 2, grp % 2
                og = jnp.where(low, outs[i, kh, 0][g * blk:(g + 1) * blk], outs[i, kh, 1][g * blk:(g + 1) * blk])
                ob.append((og * _silu(zb_ref[0, qrows(i), grp * LANES:(grp + 1) * LANES])).astype(BF16))
            mix = mix_a[i] + _dot(jnp.concatenate(ob, axis=-1), w_ref[A_WIDTH:MIX_WIDTH, :])
            r = DEEPNORM_ALPHA * x_ref[0, qrows(i), :] + (1.0 + gate) * mix
            y_ref[0, qrows(i), :] = _layer_norm(r, g_ref[...], b_ref[...])

    starts = list(range(0, tq // blk, SWA_WAVE))
    pending = score_matmuls(starts[0])
    for w, i0 in enumerate(starts):
        ready = pending
        if w + 1 < len(starts):
            pending = score_matmuls(starts[w + 1])
        finish(i0, *ready)


def _swa_out(sinks, qb, kvl, zb, oa, x, mod, mod_row0, w_out, ln_g, ln_b):
    bsz, t, _ = qb.shape
    tq = SWA_TQ
    per = tq // WINDOW
    cur = lambda w: pl.BlockSpec((1, tq, w), lambda b, i: (b, i, 0))
    kv_cur = lambda j: pl.BlockSpec((1, tq, LANES), lambda b, i: (b, i, j))
    kv_prev = lambda j: pl.BlockSpec((1, WINDOW, LANES), lambda b, i: (b, jnp.maximum(i * per - 1, 0), j))
    const2 = lambda s: pl.BlockSpec(s, lambda b, i: (0, 0))
    return pl.pallas_call(
        _swa_out_kernel,
        grid=(bsz, t // tq),
        in_specs=[pl.BlockSpec(memory_space=pltpu.SMEM), cur(B_WIDTH),
                  kv_cur(0), kv_prev(0), kv_cur(1), kv_prev(1), kv_cur(2), kv_prev(2), kv_cur(3), kv_prev(3),
                  cur(B_WIDTH), cur(A_WIDTH), cur(D_MODEL),
                  pl.BlockSpec((8, D_MODEL), lambda b, i: (mod_row0 // 8, 2)),
                  const2((MIX_WIDTH, D_MODEL)), const2((1, D_MODEL)), const2((1, D_MODEL))],
        out_specs=cur(D_MODEL),
        out_shape=jax.ShapeDtypeStruct((bsz, t, D_MODEL), F32),
        compiler_params=pltpu.CompilerParams(dimension_semantics=("arbitrary", "arbitrary"),
                                             vmem_limit_bytes=VMEM_LIMIT),
        name="swa_out",
    )(sinks, qb, kvl, kvl, kvl, kvl, kvl, kvl, kvl, kvl, zb, oa, x, mod, w_out, ln_g, ln_b)


def _out_kernel(oa_ref, ob_ref, x_ref, gate_ref, w_ref, g_ref, b_ref, y_ref):
    mix = _dot(oa_ref[...], w_ref[0:A_WIDTH, :]) + _dot(ob_ref[...], w_ref[A_WIDTH:MIX_WIDTH, :])
    r = DEEPNORM_ALPHA * x_ref[...] + (1.0 + gate_ref[...]) * mix
    y_ref[...] = _layer_norm(r, g_ref[...], b_ref[...])


def _out(oa, ob, x, mod, w_out, ln_g, ln_b):
    n = x.shape[0]
    full = lambda s: pl.BlockSpec(s, lambda i: (0, 0))
    return pl.pallas_call(
        _out_kernel,
        grid=(1,),
        in_specs=[full((n, A_WIDTH)), full((n, B_WIDTH)), full((n, D_MODEL)),
                  pl.BlockSpec((n, D_MODEL), lambda i: (0, 2)),
                  full((MIX_WIDTH, D_MODEL)), full((1, D_MODEL)), full((1, D_MODEL))],
        out_specs=full((n, D_MODEL)),
        out_shape=jax.ShapeDtypeStruct((n, D_MODEL), F32),
        compiler_params=pltpu.CompilerParams(dimension_semantics=("arbitrary",),
                                             vmem_limit_bytes=VMEM_LIMIT),
        name="out",
    )(oa, ob, x, mod, w_out, ln_g, ln_b)


def _sproj_kernel(x_ref, mod_ref, w_ref, cw_ref, cst_ref, alog_ref, dt_ref, cos_ref, sin_ref,
                  q_ref, k_ref, v_ref, za_ref, gb_ref, qb_ref, kb_ref, vb_ref, zb_ref, ncs_ref):
    shift = mod_ref[:, 0:D_MODEL]
    scale = mod_ref[:, D_MODEL:2 * D_MODEL]
    h = (x_ref[...] * (1.0 + scale) + shift).astype(BF16)

    for gi, o_ref in enumerate((q_ref, k_ref, v_ref)):
        c0 = gi * A_WIDTH
        cs = slice(c0, c0 + A_WIDTH)
        u = _dot(h, w_ref[:, cs])
        acc = cst_ref[0, :, cs] * cw_ref[0:1, cs]
        acc = acc + cst_ref[1, :, cs] * cw_ref[1:2, cs]
        acc = acc + cst_ref[2, :, cs] * cw_ref[2:3, cs]
        acc = acc + u * cw_ref[3:4, cs]
        y = _silu(acc)
        if gi == 0:
            y = _l2norm_heads(y, A_DK ** -0.5)
        elif gi == 1:
            y = _l2norm_heads(y, 1.0)
        o_ref[...] = y
        ncs_ref[0, :, cs] = cst_ref[1, :, cs]
        ncs_ref[1, :, cs] = cst_ref[2, :, cs]
        ncs_ref[2, :, cs] = u

    za_ref[...] = _dot(h, w_ref[:, C_ZA:C_ZA + A_WIDTH])
    gb_ref[...] = _gate_lanes(_dot(h, w_ref[:, C_BD:C_BD + LANES]), alog_ref[...], dt_ref[...])

    cos = cos_ref[...]
    sin = sin_ref[...]
    uq = _dot(h, w_ref[:, C_QB:C_QB + B_WIDTH])
    for g in range(B_WIDTH // LANES):
        qb_ref[:, g * LANES:(g + 1) * LANES] = (
            _rotary_group(uq[:, g * LANES:(g + 1) * LANES], cos, sin) * (B_HD ** -0.5))
    kb_ref[...] = _rotary_group(_dot(h, w_ref[:, C_KB:C_KB + LANES]), cos, sin)
    vb_ref[...] = _dot(h, w_ref[:, C_VB:C_VB + LANES])
    zb_ref[...] = _dot(h, w_ref[:, C_ZB:C_ZB + B_WIDTH])


def _sproj(x, mod_s, w_r, conv_w, cst, alog_row, dt_row, cos_row, sin_row):
    n = x.shape[0]
    full = lambda s: pl.BlockSpec(s, lambda i: (0,) * len(s))
    wide = lambda w: jax.ShapeDtypeStruct((n, w), F32)
    return pl.pallas_call(
        _sproj_kernel,
        grid=(1,),
        in_specs=[full((n, D_MODEL)), pl.BlockSpec((n, 3 * D_MODEL), lambda i: (0, 0)),
                  full((D_MODEL, W_COLS)),
                  full((CONV_W, A_QKV)), full((CONV_W - 1, n, A_QKV)),
                  full((1, LANES)), full((1, LANES)), full((1, LANES)), full((1, LANES))],
        out_specs=[full((n, A_WIDTH)), full((n, A_WIDTH)), full((n, A_WIDTH)), full((n, A_WIDTH)),
                   full((n, LANES)), full((n, B_WIDTH)), full((n, LANES)), full((n, LANES)),
                   full((n, B_WIDTH)), full((CONV_W - 1, n, A_QKV))],
        out_shape=[wide(A_WIDTH), wide(A_WIDTH), wide(A_WIDTH), wide(A_WIDTH), wide(LANES),
                   wide(B_WIDTH), wide(LANES), wide(LANES), wide(B_WIDTH),
                   jax.ShapeDtypeStruct((CONV_W - 1, n, A_QKV), F32)],
        compiler_params=pltpu.CompilerParams(dimension_semantics=("arbitrary",),
                                             vmem_limit_bytes=VMEM_LIMIT),
        name="sproj",
    )(x, mod_s, w_r, conv_w, cst, alog_row, dt_row, cos_row, sin_row)


def _sstep_kernel(sink_ref, q_ref, k_ref, v_ref, gb_ref, za_ref, na_ref, st_ref,
                  qb_ref, kn_ref, vn_ref, zb_ref, ck_ref, cv_ref,
                  oa_ref, ob_ref, nst_ref, nck_ref, ncv_ref,
                  o_scr, ob_scr):
    bt = q_ref.shape[0]
    gbv = gb_ref[...]

    pick = (lax.broadcasted_iota(jnp.int32, (bt, bt * A_DV), 1) // A_DV
            == lax.broadcasted_iota(jnp.int32, (bt, bt * A_DV), 0))
    pick = jnp.where(pick, 1.0, 0.0).astype(BF16)
    for h in range(A_HEADS):
        hs = slice(h * A_DK, (h + 1) * A_DK)
        q_rep = _dot(q_ref[:, hs].T.astype(BF16), pick)
        k_rep = _dot(k_ref[:, hs].T.astype(BF16), pick)
        for bb in range(bt):
            eg = jnp.exp(gbv[bb:bb + 1, A_HEADS + h:A_HEADS + h + 1])
            beta = gbv[bb:bb + 1, h:h + 1]
            kcol = k_rep[:, bb * A_DV:(bb + 1) * A_DV]
            qcol = q_rep[:, bb * A_DV:(bb + 1) * A_DV]
            s1 = eg * st_ref[bb, h]
            pred = jnp.sum(kcol * s1, axis=0, keepdims=True)
            upd = beta * (v_ref[bb:bb + 1, hs] - pred)
            s2 = s1 + kcol * upd
            nst_ref[bb, h] = s2
            o_scr[bb:bb + 1, hs] = jnp.sum(qcol * s2, axis=0, keepdims=True)
    na = na_ref[...]
    for h in range(A_HEADS):
        hs = slice(h * A_DK, (h + 1) * A_DK)
        o = o_scr[:, hs]
        on = o * lax.rsqrt(jnp.mean(o * o, axis=-1, keepdims=True) + RMS_EPS) * na
        oa_ref[:, hs] = (on * _silu(za_ref[:, hs])).astype(BF16)

    row8 = lax.broadcasted_iota(jnp.int32, (B_HEADS, LANES), 0)
    lane8 = _lane((B_HEADS, LANES))
    own_half = (lane8 >= B_HD) == (row8 >= B_GROUP)
    rcol = lax.broadcasted_iota(jnp.int32, (B_HEADS, 1), 0)
    sink = jnp.zeros((B_HEADS, 1), F32)
    for r in range(B_HEADS):
        sink = jnp.where(rcol == r, sink_ref[r], sink)
    qv = qb_ref[...]
    qv_r = jnp.concatenate([pltpu.roll(qv[:, g * LANES:(g + 1) * LANES], B_HD, axis=1)
                            for g in range(B_WIDTH // LANES)], axis=-1)
    kn_t = kn_ref[...].T
    vn_t = vn_ref[...].T
    newest = _lane((LANES, WINDOW)) == WINDOW - 1
    qzs, scs = [], []
    for bb in range(bt):
        qz = jnp.zeros((B_HEADS, LANES), F32)
        for r in range(B_HEADS):
            grp, half, kh = r // 2, r % 2, r // B_GROUP
            src = qv if half == kh else qv_r
            qz = jnp.where(row8 == r, src[bb:bb + 1, grp * LANES:(grp + 1) * LANES], qz)
        qzs.append(jnp.where(own_half, qz, 0.0))
    for bb in range(bt):
        scs.append(_dot(qzs[bb], ck_ref[bb]))
    ps, pnews, dens = [], [], []
    for bb in range(bt):
        sc_new = jnp.sum(qzs[bb] * kn_ref[bb:bb + 1, :], axis=-1, keepdims=True)
        m = jnp.maximum(jnp.maximum(jnp.max(scs[bb], axis=-1, keepdims=True), sc_new), sink)
        p = jnp.exp(scs[bb] - m)
        p_new = jnp.exp(sc_new - m)
        ps.append(p)
        pnews.append(p_new)
        dens.append(jnp.sum(p, axis=-1, keepdims=True) + p_new + jnp.exp(sink - m))
    pvs = [_dot_nt(ps[bb], cv_ref[bb]) for bb in range(bt)]
    for bb in range(bt):
        o = (pvs[bb] + pnews[bb] * vn_ref[bb:bb + 1, :]) / dens[bb]
        o = jnp.where(own_half, o, 0.0)
        ob_scr[bb * B_HEADS:(bb + 1) * B_HEADS, :] = o + pltpu.roll(o, B_HD, axis=1)
    for bb in range(bt):
        nck_ref[bb] = jnp.where(newest, kn_t[:, bb:bb + 1], pltpu.roll(ck_ref[bb], WINDOW - 1, axis=1))
        ncv_ref[bb] = jnp.where(newest, vn_t[:, bb:bb + 1], pltpu.roll(cv_ref[bb], WINDOW - 1, axis=1))
    low = _lane((bt, LANES)) < B_HD
    for grp in range(B_WIDTH // LANES):
        even = ob_scr[pl.ds(2 * grp, bt, stride=B_HEADS), :]
        odd = ob_scr[pl.ds(2 * grp + 1, bt, stride=B_HEADS), :]
        gs = slice(grp * LANES, (grp + 1) * LANES)
        ob_ref[:, gs] = (jnp.where(low, even, odd) * _silu(zb_ref[:, gs])).astype(BF16)


def _sstep(sinks, q, k, v, gb, za, na_row, state, qb, kn, vn, zb, ck, cv):
    n = q.shape[0]
    bt = STEP_BT
    row = lambda w: pl.BlockSpec((bt, w), lambda i: (i, 0))
    st_spec = pl.BlockSpec((bt, A_HEADS, A_DK, A_DV), lambda i: (i, 0, 0, 0))
    c_spec = pl.BlockSpec((bt, WINDOW, LANES), lambda i: (i, 0, 0))
    return pl.pallas_call(
        _sstep_kernel,
        grid=(n // bt,),
        in_specs=[pl.BlockSpec(memory_space=pltpu.SMEM),
                  row(A_WIDTH), row(A_WIDTH), row(A_WIDTH), row(LANES), row(A_WIDTH),
                  pl.BlockSpec((1, A_DV), lambda i: (0, 0)), st_spec,
                  row(B_WIDTH), row(LANES), row(LANES), row(B_WIDTH), c_spec, c_spec],
        out_specs=[row(A_WIDTH), row(B_WIDTH), st_spec, c_spec, c_spec],
        out_shape=[jax.ShapeDtypeStruct((n, A_WIDTH), BF16),
                   jax.ShapeDtypeStruct((n, B_WIDTH), BF16),
                   jax.ShapeDtypeStruct((n, A_HEADS, A_DK, A_DV), F32),
                   jax.ShapeDtypeStruct((n, WINDOW, LANES), F32),
                   jax.ShapeDtypeStruct((n, WINDOW, LANES), F32)],
        scratch_shapes=[pltpu.VMEM((bt, A_WIDTH), F32), pltpu.VMEM((bt * B_HEADS, LANES), F32)],
        compiler_params=pltpu.CompilerParams(dimension_semantics=("arbitrary",),
                                             vmem_limit_bytes=VMEM_LIMIT),
        name="sstep",
    )(sinks, q, k, v, gb, za, na_row, state, qb, kn, vn, zb, ck, cv)


def _rope_tables(pos):
    half = B_HD // 2
    inv = 1.0 / (ROPE_THETA ** (np.arange(half, dtype=np.float64) / half))
    ang = np.asarray(pos, np.float64)[:, None] * inv[None, :]
    cos, sin = np.cos(ang), np.sin(ang)
    reps = LANES // B_HD
    return (jnp.asarray(np.tile(np.concatenate([cos, cos], -1), (1, reps)), F32),
            jnp.asarray(np.tile(np.concatenate([-sin, sin], -1), (1, reps)), F32))


def _pad_row(vec, offset):
    return jnp.pad(vec.astype(F32).reshape(1, -1), ((0, 0), (offset, LANES - offset - vec.shape[0])))


def _layer(x_prompt, x_sample, state_conv, state_delta, cache_k, cache_v, c_prompt, c_sample,
           w_ada, b_ada, w_in, conv_w, a_log, dt_bias, norm_a, sinks, w_out, ln_g, ln_b):
    bsz, seq, _ = x_prompt.shape
    n_s = x_sample.shape[0]

    w_r = _wprep(jnp.swapaxes(w_in, 0, 1))
    w_o = w_out.astype(BF16)
    alog_row = _pad_row(a_log, A_HEADS)
    dt_row = _pad_row(dt_bias, A_HEADS)
    na_row = norm_a.reshape(1, A_DV)
    g_row = ln_g.reshape(1, D_MODEL)
    b_row = ln_b.reshape(1, D_MODEL)

    assert n_s % 8 == 0 and bsz <= 8
    mod = _ada(c_sample, c_prompt, w_ada, b_ada.reshape(1, 3 * D_MODEL))

    cos_p, sin_p = _rope_tables(np.arange(seq))
    (qkv, za, gb, qb, kvl, zb, conv_p, kb_last, vb_last) = _proj(
        x_prompt, mod, n_s, w_r, conv_w, alog_row, dt_row, cos_p, sin_p)
    oa, delta_p = _delta(qkv, gb, za, na_row)
    y_p = _swa_out(sinks, qb, kvl, zb, oa, x_prompt, mod, n_s, w_o, g_row, b_row)
    swa_k_p = kb_last.reshape(bsz, WINDOW, B_KV_HEADS, B_HD)
    swa_v_p = vb_last.reshape(bsz, WINDOW, B_KV_HEADS, B_HD)

    cos_s, sin_s = _rope_tables(np.array([PAST_LEN]))
    xs = x_sample.reshape(n_s, D_MODEL)
    cst = jnp.transpose(state_conv, (1, 0, 2))
    sq, sk, sv, sza, sgb, sqb, skn, svn, szb, ncs = _sproj(xs, mod, w_r, conv_w, cst, alog_row, dt_row,
                                                           cos_s, sin_s)
    soa, sob, delta_s, nck, ncv = _sstep(sinks, sq, sk, sv, sgb, sza, na_row, state_delta,
                                         sqb, skn, svn, szb,
                                         jnp.swapaxes(cache_k.reshape(n_s, WINDOW, LANES), 1, 2),
                                         jnp.swapaxes(cache_v.reshape(n_s, WINDOW, LANES), 1, 2))
    y_s = _out(soa, sob, xs, mod, w_o, g_row, b_row)
    conv_s = jnp.transpose(ncs, (1, 0, 2))
    unpack = lambda c: jnp.swapaxes(c, 1, 2).reshape(n_s, WINDOW, B_KV_HEADS, B_HD)
    return (y_p, y_s.reshape(n_s, 1, D_MODEL), conv_p, delta_p, swa_k_p, swa_v_p,
            conv_s, delta_s, unpack(nck), unpack(ncv))


def kernel(x_prompt, x_sample, state_conv, state_delta, cache_swa_k, cache_swa_v, c_prompt, c_sample,
           w_ada, b_ada, w_in, conv_w, a_log, dt_bias, norm_a, sinks, w_out, ln_g, ln_b):
    assert w_ada.shape[0] == DEPTH == 1
    outs = _layer(x_prompt, x_sample, state_conv[0], state_delta[0], cache_swa_k[0], cache_swa_v[0],
                  c_prompt, c_sample, w_ada[0], b_ada[0], w_in[0], conv_w[0], a_log[0], dt_bias[0],
                  norm_a[0], sinks[0], w_out[0], ln_g[0], ln_b[0])
    y_p, y_s = outs[0], outs[1]
    return (y_p, y_s) + tuple(o[None] for o in outs[2:])
```
